```python
import jax, jax.numpy as jnp
from jax import lax
import numpy as np

D_MODEL = 1024
BATCH = 32
SEQ = 256
DEPTH = 2
DEC_BATCH = 2
DEC_SEQ = 1024
PAST_LEN = 256

GRID_W = 64
EPS = 1e-6
ROPE_BASE = 10000.0
BLOCK = 128
NEG_INF = -1e30

MLA_HEADS = 4
MLA_NOPE = 64
MLA_ROPE = 32
MLA_V = 64
MLA_Q_LORA = 256
MLA_KV_LORA = 128
HG_HEADS = 4
HG_DK = 64
HG_DV = 64
HG_CHUNK = 32
FN_GROUPS = 4
FN_WIDTH = 256
SWA_HEADS = 4
SWA_KV_HEADS = 2
SWA_HD = 64
WINDOW = 128
N_EXPERTS = 64
TOP_K = 6
D_EXPERT = 256
D_SHARED = 256
ROUTE_SCALE = 2.5

MLA_IN = MLA_Q_LORA + MLA_KV_LORA + MLA_ROPE
HG_IN = 3 * HG_HEADS * HG_DK + 2 * HG_HEADS * HG_DV
FN_IN = FN_WIDTH
SWA_IN = (SWA_HEADS + 2 * SWA_KV_HEADS) * SWA_HD
IN_COLS = MLA_IN + HG_IN + FN_IN + SWA_IN
IN_SPLITS = (MLA_IN, MLA_IN + HG_IN, MLA_IN + HG_IN + FN_IN)
MIX_W = MLA_HEADS * MLA_V + HG_HEADS * HG_DV + FN_WIDTH + SWA_HEADS * SWA_HD

kernel_name = 'hybrid_prefix_diffusion_step'


def rmsnorm(x, g):
    xf = x.astype(jnp.float32)
    y = xf * lax.rsqrt(jnp.mean(xf * xf, axis=-1, keepdims=True) + EPS)
    return (y * g.astype(jnp.float32)).astype(x.dtype)


def rope_tables(pos, dim):
    nf = dim // 4
    inv = ROPE_BASE ** (-jnp.arange(nf, dtype=jnp.float32) / nf)
    ang = pos.astype(jnp.float32)[:, :, None] * inv
    return jnp.cos(ang), jnp.sin(ang)


def apply_rope(x, tables):
    cos, sin = tables
    B, T, H, dim = x.shape
    xr = x.astype(jnp.float32).reshape(B, T, H, 2, dim // 4, 2)
    a, b = xr[..., 0], xr[..., 1]
    c, s = cos[None, :, None], sin[None, :, None]
    out = jnp.stack([a * c - b * s, a * s + b * c], axis=-1)
    return out.reshape(B, T, H, dim).astype(x.dtype)


def sink_softmax(s, sink):
    sk = jnp.broadcast_to(sink.astype(jnp.float32)[..., None, None], s.shape[:-1] + (1,))
    return jax.nn.softmax(jnp.concatenate([s, sk], axis=-1), axis=-1)[..., :-1]


def dense_attention(q, k, v, scale, sink=None):
    B, Tq, KVH, G, dq = q.shape
    nb = Tq // BLOCK
    qb = jnp.moveaxis(q.reshape(B, nb, BLOCK, KVH, G, dq), 1, 0)

    def one_block(qi):
        s = jnp.einsum('bqhgd,bkhd->bhgqk', qi, k).astype(jnp.float32) * scale
        p = jax.nn.softmax(s, axis=-1) if sink is None else sink_softmax(s, sink)
        return jnp.einsum('bhgqk,bkhe->bqhge', p.astype(v.dtype), v)

    o = lax.map(one_block, qb)
    return jnp.moveaxis(o, 0, 1).reshape(B, Tq, KVH, G, -1)


def window_ctx_attention(q, k, v, kc, vc, sink, scale):
    B, T, KVH, G, d = q.shape
    nb = T // BLOCK
    qb = q.reshape(B, nb, BLOCK, KVH, G, d)
    padw = ((0, 0), (BLOCK, BLOCK), (0, 0), (0, 0))
    kp, vp = jnp.pad(k, padw), jnp.pad(v, padw)
    idx = jnp.arange(nb)[:, None] * BLOCK + jnp.arange(3 * BLOCK)[None, :]
    kb, vb = kp[:, idx], vp[:, idx]
    qpos = jnp.arange(nb)[:, None] * BLOCK + jnp.arange(BLOCK)[None, :]
    kpos = idx - BLOCK
    valid = ((jnp.abs(qpos[:, :, None] - kpos[:, None, :]) <= WINDOW)
             & (kpos[:, None, :] >= 0) & (kpos[:, None, :] < T))
    s_loc = jnp.einsum('bnqhgd,bnkhd->bnhgqk', qb, kb).astype(jnp.float32) * scale
    s_loc = jnp.where(valid[None, :, None, None], s_loc, NEG_INF)
    s_ctx = jnp.einsum('bnqhgd,bchd->bnhgqc', qb, kc).astype(jnp.float32) * scale
    p = sink_softmax(jnp.concatenate([s_loc, s_ctx], axis=-1), sink)
    p_loc, p_ctx = p[..., :3 * BLOCK].astype(v.dtype), p[..., 3 * BLOCK:].astype(v.dtype)
    o = (jnp.einsum('bnhgqk,bnkhe->bnqhge', p_loc, vb)
         + jnp.einsum('bnhgqc,bche->bnqhge', p_ctx, vc))
    return o.reshape(B, T, KVH, G, d)


def hgrn2_chunk_scan(q, k, v, logf, s0):
    B, T, H, DK = q.shape
    C = HG_CHUNK
    n = T // C
    r = lambda a: a.reshape(B, n, C, H, a.shape[-1]).transpose(1, 0, 3, 2, 4)
    qc, kc, vc = r(q), r(k), r(v)
    G = jnp.cumsum(r(logf).astype(jnp.float32), axis=-2)
    mask = jnp.tril(jnp.ones((C, C), dtype=bool))
    diff = G[..., :, None, :] - G[..., None, :, :]
    decay = jnp.exp(jnp.where(mask[:, :, None], diff, -jnp.inf))
    A = jnp.einsum('nbhtsd,nbhsd->nbhts', qc[..., :, None, :] * decay, kc)
    o_intra = jnp.einsum('nbhts,nbhse->nbhte', A, vc)
    q_dec = qc * jnp.exp(G)
    k_dec = kc * jnp.exp(G[..., -1:, :] - G)
    chunk_decay = jnp.exp(G[..., -1, :])

    def step(S, inp):
        qd, kd, vv, cd = inp
        o_inter = jnp.einsum('bhtd,bhde->bhte', qd, S)
        S_new = cd[..., None] * S + jnp.einsum('bhsd,bhse->bhde', kd, vv)
        return S_new, o_inter

    S_fin, o_inter = lax.scan(step, s0.astype(jnp.float32), (q_dec, k_dec, vc, chunk_decay))
    o = o_intra + o_inter
    return o.transpose(1, 0, 3, 2, 4).reshape(B, T, H, -1), S_fin


def mla_mixer(u, q_norm_g, w_q_up, kv_norm_g, w_kv_up, rope, cache):
    B, T, _ = u.shape
    c_q, c_kv, k_rope = jnp.split(u, [MLA_Q_LORA, MLA_Q_LORA + MLA_KV_LORA], axis=-1)
    q = jnp.einsum('btr,rk->btk', rmsnorm(c_q, q_norm_g), w_q_up).reshape(B, T, MLA_HEADS, MLA_NOPE + MLA_ROPE)
    q_nope, q_rope = q[..., :MLA_NOPE], q[..., MLA_NOPE:]
    c_kv = rmsnorm(c_kv, kv_norm_g)
    k_rope = k_rope[:, :, None, :]
    if cache is None:
        ckv_all, kr_all = c_kv, k_rope
        new = (c_kv, k_rope[:, :, 0])
    else:
        q_rope, k_rope = apply_rope(q_rope, rope), apply_rope(k_rope, rope)
        ckv_ctx, kr_ctx = cache
        ckv_all = jnp.concatenate([ckv_ctx.astype(c_kv.dtype), c_kv], axis=1)
        kr_all = jnp.concatenate([kr_ctx[:, :, None].astype(k_rope.dtype), k_rope], axis=1)
        new = ()
    Tk = ckv_all.shape[1]
    kv = jnp.einsum('btr,rk->btk', ckv_all, w_kv_up).reshape(B, Tk, MLA_HEADS, MLA_NOPE + MLA_V)
    k_nope, v = kv[..., :MLA_NOPE], kv[..., MLA_NOPE:]
    k = jnp.concatenate([k_nope, jnp.broadcast_to(kr_all, (B, Tk, MLA_HEADS, MLA_ROPE))], axis=-1)
    qf = jnp.concatenate([q_nope, q_rope], axis=-1)[:, :, :, None, :]
    o = dense_attention(qf, k, v, (MLA_NOPE + MLA_ROPE) ** -0.5)
    return o.reshape(B, T, MLA_HEADS * MLA_V), new


def hgrn_mixer(u, lb_f, lb_b, norm_g, state):
    B, T, _ = u.shape
    HK, HV = HG_HEADS * HG_DK, HG_HEADS * HG_DV
    q, zf, zb, i, g = jnp.split(u, [HK, 2 * HK, 3 * HK, 3 * HK + HV], axis=-1)
    q = jax.nn.silu(q).reshape(B, T, HG_HEADS, HG_DK)
    i = i.reshape(B, T, HG_HEADS, HG_DV)

    def gates(z, lb):
        f = lb + (1.0 - lb) * jax.nn.sigmoid(z.astype(jnp.float32))
        return (1.0 - f).reshape(B, T, HG_HEADS, HG_DK), jnp.log(f).reshape(B, T, HG_HEADS, HG_DK)

    kf, gf = gates(zf, lb_f)
    kb, gb = gates(zb, lb_b)
    if state is None:
        s0 = jnp.zeros((B, 2, HG_HEADS, HG_DK, HG_DV), jnp.float32)
    else:
        s0 = state.astype(jnp.float32)
    flip = lambda a: jnp.flip(a, axis=1)
    o_f, s_f = hgrn2_chunk_scan(q, kf, i, gf, s0[:, 0])
    o_b, s_b = hgrn2_chunk_scan(flip(q), flip(kb), flip(i), flip(gb), s0[:, 1])
    o = rmsnorm(o_f + flip(o_b), norm_g).reshape(B, T, HV) * jax.nn.silu(g)
    new = (jnp.stack([s_f, s_b], axis=1).astype(u.dtype),) if state is None else ()
    return o.astype(u.dtype), new


def fourier_mixer(u, w):
    B, T, _ = u.shape
    ug = u.astype(jnp.float32).reshape(B, T, FN_GROUPS, FN_WIDTH // FN_GROUPS)
    y = jnp.fft.fft2(ug, axes=(1, 3), norm='ortho').real
    return jnp.einsum('btk,kj->btj', y.reshape(B, T, FN_WIDTH).astype(u.dtype), w)


def swa_mixer(u, sink, rope, cache):
    B, T, _ = u.shape
    G = SWA_HEADS // SWA_KV_HEADS
    q, k, v = jnp.split(u, [SWA_HEADS * SWA_HD, (SWA_HEADS + SWA_KV_HEADS) * SWA_HD], axis=-1)
    q = q.reshape(B, T, SWA_HEADS, SWA_HD)
    k = k.reshape(B, T, SWA_KV_HEADS, SWA_HD)
    v = v.reshape(B, T, SWA_KV_HEADS, SWA_HD)
    sink = sink.reshape(SWA_KV_HEADS, G)
    scale = SWA_HD ** -0.5
    if cache is None:
        o = dense_attention(q.reshape(B, T, SWA_KV_HEADS, G, SWA_HD), k, v, scale, sink)
        new = (k, v)
    else:
        q, k = apply_rope(q, rope), apply_rope(k, rope)
        kc, vc = cache
        o = window_ctx_attention(q.reshape(B, T, SWA_KV_HEADS, G, SWA_HD), k, v,
                                 kc.astype(k.dtype), vc.astype(v.dtype), sink, scale)
        new = ()
    return o.reshape(B, T, SWA_HEADS * SWA_HD), new


def moe_ffn(h, w_router, b_router, w_gate, w_up, w_down, ws_gate, ws_up, ws_down):
    scores = jax.nn.sigmoid(jnp.einsum('btd,de->bte', h, w_router).astype(jnp.float32))
    _, idx = lax.top_k(scores + b_router.astype(jnp.float32), TOP_K)
    sel = jnp.take_along_axis(scores, idx, axis=-1)
    wts = ROUTE_SCALE * sel / jnp.sum(sel, axis=-1, keepdims=True)
    gate = jnp.sum(jax.nn.one_hot(idx, N_EXPERTS, dtype=jnp.float32) * wts[..., None], axis=-2).astype(h.dtype)
    hid = (jax.nn.silu(jnp.einsum('btd,edf->btef', h, w_gate))
           * jnp.einsum('btd,edf->btef', h, w_up) * gate[..., None])
    routed = jnp.einsum('btef,efd->btd', hid, w_down)
    shared = jnp.einsum('btf,fd->btd', jax.nn.silu(h @ ws_gate) * (h @ ws_up), ws_down)
    return routed + shared


def trunk_layer(x, cvec, P, l, lb, ropes, cache):
    m = (jnp.einsum('bd,dk->bk', jax.nn.silu(cvec), P['w_ada'][l]) + P['b_ada'][l])[:, None, :]
    sh1, sc1, g1, sh2, sc2, g2 = jnp.split(m, 6, axis=-1)
    h = rmsnorm(x, P['norm1_g'][l]) * (1 + sc1) + sh1
    u = jnp.einsum('btd,dk->btk', h, P['w_in'][l])
    u_mla, u_hg, u_fn, u_swa = jnp.split(u, IN_SPLITS, axis=-1)
    o_mla, n_mla = mla_mixer(u_mla, P['mla_q_norm_g'][l], P['mla_w_q_up'][l], P['mla_kv_norm_g'][l],
                             P['mla_w_kv_up'][l], ropes[0], None if cache is None else cache[0:2])
    o_hg, n_hg = hgrn_mixer(u_hg, lb[0, l], lb[1, l], P['hg_norm_g'][l], None if cache is None else cache[4])
    o_fn = fourier_mixer(u_fn, P['fn_w'][l])
    o_swa, n_swa = swa_mixer(u_swa, P['swa_sink'][l], ropes[1], None if cache is None else cache[2:4])
    mix = jnp.einsum('btk,kd->btd', jnp.concatenate([o_mla, o_hg, o_fn, o_swa], axis=-1), P['w_out'][l])
    x = x + g1 * mix
    h = rmsnorm(x, P['norm2_g'][l]) * (1 + sc2) + sh2
    x = x + g2 * moe_ffn(h, P['moe_w_router'][l], P['moe_b_router'][l], P['moe_w_gate'][l], P['moe_w_up'][l],
                         P['moe_w_down'][l], P['sh_w_gate'][l], P['sh_w_up'][l], P['sh_w_down'][l])
    return x, n_mla + n_swa + n_hg


def setup_inputs(seed: int = 0) -> dict:
    key = jax.random.key(seed)
    ks = list(jax.random.split(key, 40))
    cnt = [0]

    def nk():
        cnt[0] += 1
        return ks[cnt[0] - 1]

    def nrm(shape, scale=1.0):
        return scale * jax.random.normal(nk(), shape, jnp.float32)

    def gain(shape):
        return 1.0 + 0.05 * jax.random.normal(nk(), shape, jnp.float32)

    D = D_MODEL
    return {
        'x_prompt': nrm((BATCH, SEQ, D)),
        'x_sample': nrm((DEC_BATCH, DEC_SEQ, D)),
        'c': nrm((DEC_BATCH, D)),
        'cache_mla_ckv': nrm((DEC_BATCH, DEPTH, PAST_LEN, MLA_KV_LORA)),
        'cache_mla_krope': nrm((DEC_BATCH, DEPTH, PAST_LEN, MLA_ROPE)),
        'cache_swa_k': nrm((DEC_BATCH, DEPTH, PAST_LEN, SWA_KV_HEADS, SWA_HD)),
        'cache_swa_v': nrm((DEC_BATCH, DEPTH, PAST_LEN, SWA_KV_HEADS, SWA_HD)),
        'state_hgrn': nrm((DEC_BATCH, DEPTH, 2, HG_HEADS, HG_DK, HG_DV), 0.5),
        'c_ctx': nrm((D,)),
        'w_ada': nrm((DEPTH, D, 6 * D), 0.5 * D ** -0.5),
        'b_ada': nrm((DEPTH, 6 * D), 0.02),
        'norm1_g': gain((DEPTH, D)),
        'norm2_g': gain((DEPTH, D)),
        'w_in': nrm((DEPTH, D, IN_COLS), D ** -0.5),
        'mla_q_norm_g': gain((DEPTH, MLA_Q_LORA)),
        'mla_w_q_up': nrm((DEPTH, MLA_Q_LORA, MLA_HEADS * (MLA_NOPE + MLA_ROPE)), MLA_Q_LORA ** -0.5),
        'mla_kv_norm_g': gain((DEPTH, MLA_KV_LORA)),
        'mla_w_kv_up': nrm((DEPTH, MLA_KV_LORA, MLA_HEADS * (MLA_NOPE + MLA_V)), MLA_KV_LORA ** -0.5),
        'hg_lb_logits': nrm((2, DEPTH, HG_HEADS * HG_DK), 0.5),
        'hg_norm_g': gain((DEPTH, HG_DV)),
        'fn_w': nrm((DEPTH, FN_WIDTH, FN_WIDTH), FN_WIDTH ** -0.5),
        'swa_sink': nrm((DEPTH, SWA_HEADS), 0.5),
        'w_out': nrm((DEPTH, MIX_W, D), MIX_W ** -0.5),
        'moe_w_router': nrm((DEPTH, D, N_EXPERTS), D ** -0.5),
        'moe_b_router': nrm((DEPTH, N_EXPERTS), 0.01),
        'moe_w_gate': nrm((DEPTH, N_EXPERTS, D, D_EXPERT), D ** -0.5),
        'moe_w_up': nrm((DEPTH, N_EXPERTS, D, D_EXPERT), D ** -0.5),
        'moe_w_down': nrm((DEPTH, N_EXPERTS, D_EXPERT, D), D_EXPERT ** -0.5),
        'sh_w_gate': nrm((DEPTH, D, D_SHARED), D ** -0.5),
        'sh_w_up': nrm((DEPTH, D, D_SHARED), D ** -0.5),
        'sh_w_down': nrm((DEPTH, D_SHARED, D), D_SHARED ** -0.5),
        'final_norm_g': gain((D,)),
    }


def reference(x_prompt, x_sample, c, cache_mla_ckv, cache_mla_krope, cache_swa_k, cache_swa_v, state_hgrn,
              c_ctx, w_ada, b_ada, norm1_g, norm2_g, w_in, mla_q_norm_g, mla_w_q_up, mla_kv_norm_g, mla_w_kv_up,
              hg_lb_logits, hg_norm_g, fn_w, swa_sink, w_out, moe_w_router, moe_b_router, moe_w_gate, moe_w_up,
              moe_w_down, sh_w_gate, sh_w_up, sh_w_down, final_norm_g):
    P = {'w_ada': w_ada, 'b_ada': b_ada, 'norm1_g': norm1_g, 'norm2_g': norm2_g, 'w_in': w_in,
         'mla_q_norm_g': mla_q_norm_g, 'mla_w_q_up': mla_w_q_up, 'mla_kv_norm_g': mla_kv_norm_g,
         'mla_w_kv_up': mla_w_kv_up, 'hg_norm_g': hg_norm_g, 'fn_w': fn_w, 'swa_sink': swa_sink, 'w_out': w_out,
         'moe_w_router': moe_w_router, 'moe_b_router': moe_b_router, 'moe_w_gate': moe_w_gate,
         'moe_w_up': moe_w_up, 'moe_w_down': moe_w_down, 'sh_w_gate': sh_w_gate, 'sh_w_up': sh_w_up,
         'sh_w_down': sh_w_down}
    lb = jnp.cumsum(jax.nn.softmax(hg_lb_logits.astype(jnp.float32), axis=1), axis=1)
    lb = lb - lb[:, :1]

    xp = x_prompt
    cc = c_ctx[None, :]
    per_layer = []
    for l in range(DEPTH):
        xp, new = trunk_layer(xp, cc, P, l, lb, (None, None), None)
        per_layer.append(new)
    y_prompt = rmsnorm(xp, final_norm_g)

    rows = x_sample.shape[1] // GRID_W
    r_idx, c_idx = jnp.meshgrid(jnp.arange(rows), jnp.arange(GRID_W), indexing='ij')
    lat_pos = jnp.stack([r_idx.reshape(-1), c_idx.reshape(-1)], axis=-1)
    ropes = (rope_tables(lat_pos, MLA_ROPE), rope_tables(lat_pos, SWA_HD))
    xs = x_sample
    for l in range(DEPTH):
        cache_l = (cache_mla_ckv[:, l], cache_mla_krope[:, l], cache_swa_k[:, l], cache_swa_v[:, l], state_hgrn[:, l])
        xs, _ = trunk_layer(xs, c, P, l, lb, ropes, cache_l)
    y_sample = rmsnorm(xs, final_norm_g)

    new_mla_ckv = jnp.stack([n[0] for n in per_layer], axis=1)
    new_mla_krope = jnp.stack([n[1] for n in per_layer], axis=1)
    new_swa_k = jnp.stack([n[2] for n in per_layer], axis=1)
    new_swa_v = jnp.stack([n[3] for n in per_layer], axis=1)
    new_state_hgrn = jnp.stack([n[4] for n in per_layer], axis=1)
    return (y_prompt, y_sample, new_mla_ckv, new_mla_krope, new_swa_k, new_swa_v, new_state_hgrn)
```

```python
import functools

import numpy as np
import jax
import jax.numpy as jnp
from jax import lax
from jax.experimental import pallas as pl
from jax.experimental.pallas import tpu as pltpu

F32 = jnp.float32
BF16 = jnp.bfloat16

D_MODEL = 1024
BATCH = 32
SEQ = 256
DEPTH = 2
DEC_BATCH = 2
DEC_SEQ = 1024
PAST_LEN = 256
GRID_W = 64
EPS = 1e-6
ROPE_BASE = 10000.0
NEG_INF = -1e30

MLA_HEADS = 4
MLA_NOPE = 64
MLA_ROPE = 32
MLA_V = 64
MLA_Q_LORA = 256
MLA_KV_LORA = 128
HG_HEADS = 4
HG_DK = 64
HG_DV = 64
HG_W = HG_HEADS * HG_DK
FN_GROUPS = 4
FN_WIDTH = 256
SWA_HEADS = 4
SWA_KV_HEADS = 2
SWA_HD = 64
WINDOW = 128
N_EXPERTS = 64
TOP_K = 6
D_EXPERT = 256
D_SHARED = 256
ROUTE_SCALE = 2.5

MLA_IN = MLA_Q_LORA + MLA_KV_LORA + MLA_ROPE
HG_IN = 3 * HG_HEADS * HG_DK + 2 * HG_HEADS * HG_DV
FN_IN = FN_WIDTH
SWA_IN = (SWA_HEADS + 2 * SWA_KV_HEADS) * SWA_HD

N_CTX = BATCH * SEQ
N_LAT = DEC_BATCH * DEC_SEQ
N_TOK = N_CTX + N_LAT

MLA_PACK = 512
U_COLS = MLA_PACK + HG_IN + FN_IN + SWA_IN

ROW_TILE = 256
CTX_TILES = N_CTX // ROW_TILE
LAT_TILES_PER_BATCH = DEC_SEQ // ROW_TILE
HG_CHUNK = 32
MOE_TOK_TILE = 2048
SWA_QBLK = 128
MLA_QBLK = 256
VMEM_LIMIT = 56 * 1024 * 1024


def _dot(a, b):
    return jnp.dot(a, b, preferred_element_type=F32)


def _dot_nt(a, b):
    return lax.dot_general(a, b, (((1,), (1,)), ((), ())), preferred_element_type=F32)


def _dot_tn(a, b):
    return lax.dot_general(a, b, (((0,), (0,)), ((), ())), preferred_element_type=F32)


def _dot_f32(a, b):
    return jnp.dot(a, b, precision=lax.Precision.HIGHEST, preferred_element_type=F32)


def _rms(x, g):
    return x * lax.rsqrt(jnp.mean(x * x, axis=-1, keepdims=True) + EPS) * g


def _silu(x):
    return x * jax.nn.sigmoid(x)


def _mod_row(i):
    return jnp.where(i < CTX_TILES, 0, 1 + (i - CTX_TILES) // LAT_TILES_PER_BATCH)


def _params(*sem):
    return pltpu.CompilerParams(dimension_semantics=sem, vmem_limit_bytes=VMEM_LIMIT)


ADA_COLS = 1536


def _ada_kernel(cv_ref, w_ref, b_ref, o_ref):
    a = _silu(cv_ref[...]).astype(BF16)
    o_ref[...] = _dot(a, w_ref[...].astype(BF16)) + b_ref[...]


def _ada(cv8, w_ada, b_ada):
    return pl.pallas_call(
        _ada_kernel,
        grid=(DEPTH, 6 * D_MODEL // ADA_COLS),
        in_specs=[
            pl.BlockSpec((8, D_MODEL), lambda l, j: (0, 0)),
            pl.BlockSpec((None, D_MODEL, ADA_COLS), lambda l, j: (l, 0, j)),
            pl.BlockSpec((None, 1, ADA_COLS), lambda l, j: (l, 0, j)),
        ],
        out_specs=pl.BlockSpec((None, 8, ADA_COLS), lambda l, j: (l, 0, j)),
        out_shape=jax.ShapeDtypeStruct((DEPTH, 8, 6 * D_MODEL), F32),
        compiler_params=_params("arbitrary", "arbitrary"),
        name="ada",
    )(cv8, w_ada, b_ada.reshape(DEPTH, 1, 6 * D_MODEL))


def _in_kernel(*refs, has_routed):
    if has_routed:
        (x_ref, r_ref, modp_ref, mod_ref, g_ref, w_ref,
         xo_ref, umla_ref, uhg_ref, ufn_ref, uswa_ref) = refs
        x = x_ref[...] + modp_ref[:, 5 * D_MODEL:6 * D_MODEL] * r_ref[...]
        xo_ref[...] = x
    else:
        x_ref, mod_ref, g_ref, w_ref, umla_ref, uhg_ref, ufn_ref, uswa_ref = refs
        x = x_ref[...]
    sh1 = mod_ref[:, 0:D_MODEL]
    sc1 = mod_ref[:, D_MODEL:2 * D_MODEL]
    h = _rms(x, g_ref[...]) * (1.0 + sc1) + sh1
    u = _dot(h.astype(BF16), w_ref[...])
    o = 0
    for ref, width in ((umla_ref, MLA_PACK), (uhg_ref, HG_IN), (ufn_ref, FN_IN), (uswa_ref, SWA_IN)):
        ref[...] = u[:, o:o + width].astype(ref.dtype)
        o += width


def _in_proj(x, routed, mod_prev, mod, g, w):
    has_routed = routed is not None
    row = lambda i: (i, 0)
    modspec = pl.BlockSpec((None, 1, 6 * D_MODEL), lambda i: (_mod_row(i), 0, 0))
    xspec = pl.BlockSpec((ROW_TILE, D_MODEL), row)
    in_specs = [xspec]
    args = [x]
    if has_routed:
        in_specs += [xspec, modspec]
        args += [routed, mod_prev]
    in_specs += [modspec, pl.BlockSpec((1, D_MODEL), lambda i: (0, 0)),
                 pl.BlockSpec((D_MODEL, U_COLS), lambda i: (0, 0))]
    args += [mod, g, w]
    widths = (MLA_PACK, HG_IN, FN_IN, SWA_IN)
    out_specs = [pl.BlockSpec((ROW_TILE, wd), row) for wd in widths]
    out_shape = [jax.ShapeDtypeStruct((N_TOK, wd), F32) for wd in widths]
    if has_routed:
        out_specs = [xspec] + out_specs
        out_shape = [jax.ShapeDtypeStruct((N_TOK, D_MODEL), F32)] + out_shape
    return pl.pallas_call(
        functools.partial(_in_kernel, has_routed=has_routed),
        grid=(N_TOK // ROW_TILE,),
        in_specs=in_specs, out_specs=out_specs, out_shape=out_shape,
        compiler_params=_params("arbitrary"),
        name="in_proj",
    )(*args)


def _rope(x, cos, sin_signed):
    lane = lax.broadcasted_iota(jnp.int32, x.shape, 1)
    width = x.shape[1]
    swapped = jnp.where(lane % 2 == 0, pltpu.roll(x, width - 1, 1), pltpu.roll(x, 1, 1))
    return x * cos + swapped * sin_signed


def _stack_heads(x, n_heads, head_w):
    lane = lax.broadcasted_iota(jnp.int32, x.shape, 1)
    return jnp.concatenate([jnp.where(lane // head_w == h, x, 0.0) for h in range(n_heads)], axis=0)


def _unstack_heads(o, n_heads, head_w):
    t = o.shape[0] // n_heads
    lane = lax.broadcasted_iota(jnp.int32, (t, o.shape[1]), 1)
    out = jnp.zeros((t, o.shape[1]), F32)
    for h in range(n_heads):
        out = jnp.where(lane // head_w == h, o[h * t:(h + 1) * t], out)
    return out


MLA_SCALE = (MLA_NOPE + MLA_ROPE) ** -0.5
MLA_QW = MLA_HEADS * MLA_NOPE + MLA_HEADS * MLA_ROPE
MLA_NW = MLA_HEADS * MLA_NOPE


def _mla_attend(q, kcat, v):
    qs = jnp.concatenate([_stack_heads(q[:, :MLA_NW], MLA_HEADS, MLA_NOPE),
                          _stack_heads(q[:, MLA_NW:], MLA_HEADS, MLA_ROPE)], axis=1)
    s = _dot_nt(qs.astype(BF16), kcat) * MLA_SCALE
    p = jnp.exp(s - jnp.max(s, axis=-1, keepdims=True))
    o = _dot(p.astype(BF16), v) / jnp.sum(p, axis=-1, keepdims=True)
    return _unstack_heads(o, MLA_HEADS, MLA_V)


def _mla_ctx_kernel(u_ref, qg_ref, wq_ref, kvg_ref, wkv_ref, o_ref, ckv_ref):
    u = u_ref[...]
    q = _dot(_rms(u[:, :MLA_Q_LORA], qg_ref[...]).astype(BF16), wq_ref[...])
    ckv = _rms(u[:, MLA_Q_LORA:MLA_Q_LORA + MLA_KV_LORA], kvg_ref[...])
    ckv_ref[...] = ckv
    kv = _dot(ckv.astype(BF16), wkv_ref[...])
    kr4 = u[:, MLA_Q_LORA + MLA_KV_LORA:]
    kcat = jnp.concatenate([kv[:, :MLA_NW], kr4], axis=1).astype(BF16)
    o_ref[...] = _mla_attend(q, kcat, kv[:, MLA_NW:].astype(BF16)).astype(o_ref.dtype)


def _mla_ctx(u_mla, qg, wq, kvg, wkv):
    full = lambda shape: pl.BlockSpec(shape, lambda b: (0, 0))
    return pl.pallas_call(
        _mla_ctx_kernel,
        grid=(BATCH,),
        in_specs=[pl.BlockSpec((SEQ, MLA_PACK), lambda b: (b, 0)),
                  full((1, MLA_Q_LORA)), full((MLA_Q_LORA, MLA_QW)),
                  full((1, MLA_KV_LORA)), full((MLA_KV_LORA, 2 * MLA_NW))],
        out_specs=[pl.BlockSpec((SEQ, MLA_NW), lambda b: (b, 0)),
                   pl.BlockSpec((SEQ, MLA_KV_LORA), lambda b: (b, 0))],
        out_shape=[jax.ShapeDtypeStruct((N_CTX, MLA_NW), BF16),
                   jax.ShapeDtypeStruct((N_CTX, MLA_KV_LORA), F32)],
        compiler_params=_params("arbitrary"),
        name="mla_ctx",
    )(u_mla, qg, wq, kvg, wkv)


MLA_TK = PAST_LEN + DEC_SEQ


def _mla_lat_kernel(u_ref, cckv_ref, ckr_ref, cos_ref, sin_ref, qg_ref, wq_ref, kvg_ref, wkv_ref,
                    o_ref, kcat_s, v_s):
    i = pl.program_id(1)

    @pl.when(i == 0)
    def _():
        u = u_ref[...]
        ckv_new = _rms(u[:, MLA_Q_LORA:MLA_Q_LORA + MLA_KV_LORA], kvg_ref[...])
        ckv_all = jnp.concatenate([cckv_ref[...], ckv_new], axis=0)
        kv = _dot(ckv_all.astype(BF16), wkv_ref[...])
        kr_new = _rope(u[:, MLA_Q_LORA + MLA_KV_LORA:], cos_ref[...], sin_ref[...])
        ckr = ckr_ref[...]
        kr_all = jnp.concatenate([jnp.concatenate([ckr] * MLA_HEADS, axis=1), kr_new], axis=0)
        kcat_s[...] = jnp.concatenate([kv[:, :MLA_NW], kr_all], axis=1).astype(BF16)
        v_s[...] = kv[:, MLA_NW:].astype(BF16)

    r0 = pl.multiple_of(i * MLA_QBLK, MLA_QBLK)
    cq = u_ref[pl.ds(r0, MLA_QBLK), 0:MLA_Q_LORA]
    q = _dot(_rms(cq, qg_ref[...]).astype(BF16), wq_ref[...])
    qr = _rope(q[:, MLA_NW:], cos_ref[pl.ds(r0, MLA_QBLK), :], sin_ref[pl.ds(r0, MLA_QBLK), :])
    q = jnp.concatenate([q[:, :MLA_NW], qr], axis=1)
    o_ref[...] = _mla_attend(q, kcat_s[...], v_s[...]).astype(o_ref.dtype)


def _mla_lat(u_mla, cache_ckv, cache_kr, l, cos, sin, qg, wq, kvg, wkv):
    full = lambda shape: pl.BlockSpec(shape, lambda b, i: (0, 0))
    nq = DEC_SEQ // MLA_QBLK
    return pl.pallas_call(
        _mla_lat_kernel,
        grid=(DEC_BATCH, nq),
        in_specs=[pl.BlockSpec((DEC_SEQ, MLA_PACK), lambda b, i: (N_CTX // DEC_SEQ + b, 0)),
                  pl.BlockSpec((None, None, PAST_LEN, MLA_KV_LORA), lambda b, i: (b, l, 0, 0)),
                  pl.BlockSpec((None, None, PAST_LEN, MLA_ROPE), lambda b, i: (b, l, 0, 0)),
                  full((DEC_SEQ, MLA_HEADS * MLA_ROPE)), full((DEC_SEQ, MLA_HEADS * MLA_ROPE)),
                  full((1, MLA_Q_LORA)), full((MLA_Q_LORA, MLA_QW)),
                  full((1, MLA_KV_LORA)), full((MLA_KV_LORA, 2 * MLA_NW))],
        out_specs=pl.BlockSpec((MLA_QBLK, MLA_NW), lambda b, i: (b * nq + i, 0)),
        out_shape=jax.ShapeDtypeStruct((N_LAT, MLA_NW), BF16),
        scratch_shapes=[pltpu.VMEM((MLA_TK, MLA_QW), BF16), pltpu.VMEM((MLA_TK, MLA_NW), BF16)],
        compiler_params=_params("arbitrary", "arbitrary"),
        name="mla_lat",
    )(u_mla, cache_ckv, cache_kr, cos, sin, qg, wq, kvg, wkv)


def _hgrn_kernel(*refs, seq, has_state):
    if has_state:
        (u_ref, lbf_ref, lbb_ref, ng_ref, s0_ref, o_ref,
         q_s, kf_s, gf_s, kb_s, gb_s, of_s, ob_s, stf_s, stb_s) = refs
    else:
        (u_ref, lbf_ref, lbb_ref, ng_ref, o_ref, so_ref,
         q_s, kf_s, gf_s, kb_s, gb_s, of_s, ob_s, stf_s, stb_s) = refs
    C = HG_CHUNK
    W = HG_W
    n_chunks = seq // C

    q_s[...] = _silu(u_ref[:, 0:W])
    ff = lbf_ref[...] + (1.0 - lbf_ref[...]) * jax.nn.sigmoid(u_ref[:, W:2 * W])
    kf_s[...] = 1.0 - ff
    gf_s[...] = jnp.log(ff)
    fb = lbb_ref[...] + (1.0 - lbb_ref[...]) * jax.nn.sigmoid(u_ref[:, 2 * W:3 * W])
    kb_s[...] = 1.0 - fb
    gb_s[...] = jnp.log(fb)

    rr = lax.broadcasted_iota(jnp.int32, (W, W), 0)
    cc = lax.broadcasted_iota(jnp.int32, (W, W), 1)
    blockdiag = rr // HG_DK == cc // HG_DK
    if has_state:
        for st, d in ((stf_s, 0), (stb_s, 1)):
            rows = []
            for h in range(HG_HEADS):
                z = lambda n: jnp.zeros((HG_DV, n * HG_DK), F32)
                parts = ([z(h)] if h else []) + [s0_ref[d, h]] + ([z(HG_HEADS - 1 - h)] if h < HG_HEADS - 1 else [])
                rows.append(jnp.concatenate(parts, axis=1) if len(parts) > 1 else parts[0])
            st[...] = jnp.concatenate(rows, axis=0)
    else:
        stf_s[...] = jnp.zeros((W, W), F32)
        stb_s[...] = jnp.zeros((W, W), F32)

    ri = lax.broadcasted_iota(jnp.int32, (C, C), 0)
    ci = lax.broadcasted_iota(jnp.int32, (C, C), 1)
    lower = (ri >= ci).astype(F32)
    upper = (ci >= ri).astype(F32)
    rs = lax.broadcasted_iota(jnp.int32, (HG_HEADS * C, C), 0) % C
    cs = lax.broadcasted_iota(jnp.int32, (HG_HEADS * C, C), 1)

    def chunk(r, k_s, g_s, o_s, st_s, tri, keep, mid, last):
        q = q_s[pl.ds(r, C), :]
        k = k_s[pl.ds(r, C), :]
        v = u_ref[pl.ds(r, C), 3 * W:4 * W].astype(BF16)
        G = _dot_f32(tri, g_s[pl.ds(r, C), :])
        Gm = G[mid:mid + 1, :]
        Gl = G[last:last + 1, :]
        qe = _stack_heads(q * jnp.exp(G - Gm), HG_HEADS, HG_DK)
        ke = k * jnp.exp(Gm - G)
        A = jnp.where(keep, _dot_nt(qe.astype(BF16), ke.astype(BF16)), 0.0)
        o_intra = _unstack_heads(_dot(A.astype(BF16), v), HG_HEADS, HG_DV)
        st = st_s[...]
        o_inter = _dot_nt((q * jnp.exp(G)).astype(BF16), st.astype(BF16))
        o_s[pl.ds(r, C), :] = o_intra + o_inter
        k2 = (k * jnp.exp(Gl - G)).astype(BF16)
        st_s[...] = st * jnp.exp(Gl) + jnp.where(blockdiag, _dot_tn(v, k2), 0.0)

    def fwd(c, carry):
        r = pl.multiple_of(c * C, C)
        chunk(r, kf_s, gf_s, of_s, stf_s, lower, rs >= cs, C // 2 - 1, C - 1)
        return carry

    def bwd(c, carry):
        r = pl.multiple_of((n_chunks - 1 - c) * C, C)
        chunk(r, kb_s, gb_s, ob_s, stb_s, upper, cs >= rs, C // 2, 0)
        return carry

    lax.fori_loop(0, n_chunks, fwd, 0)
    lax.fori_loop(0, n_chunks, bwd, 0)

    o = of_s[...] + ob_s[...]
    ms = _dot_f32(o * o, jnp.where(blockdiag, 1.0 / HG_DV, 0.0))
    on = o * lax.rsqrt(ms + EPS) * ng_ref[...]
    o_ref[...] = (on * _silu(u_ref[:, 4 * W:5 * W])).astype(o_ref.dtype)

    if not has_state:
        for st, d in ((stf_s, 0), (stb_s, 1)):
            for h in range(HG_HEADS):
                so_ref[d, h] = st[h * HG_DV:(h + 1) * HG_DV, h * HG_DK:(h + 1) * HG_DK]


def _hgrn(u_hg, lbf, lbb, ng4, state_t, *, seq, n_batch, row_block0):
    has_state = state_t is not None
    full = lambda shape: pl.BlockSpec(shape, lambda b: (0, 0))
    in_specs = [pl.BlockSpec((seq, HG_IN), lambda b: (row_block0 + b, 0)),
                full((1, HG_W)), full((1, HG_W)), full((1, HG_W))]
    args = [u_hg, lbf, lbb, ng4]
    st_spec = pl.BlockSpec((None, 2, HG_HEADS, HG_DV, HG_DK), lambda b: (b, 0, 0, 0, 0))
    o_spec = pl.BlockSpec((seq, HG_W), lambda b: (b, 0))
    o_shape = jax.ShapeDtypeStruct((n_batch * seq, HG_W), BF16)
    if has_state:
        in_specs.append(st_spec)
        args.append(state_t)
        out_specs, out_shape = o_spec, o_shape
    else:
        out_specs = [o_spec, st_spec]
        out_shape = [o_shape, jax.ShapeDtypeStruct((n_batch, 2, HG_HEADS, HG_DV, HG_DK), F32)]
    return pl.pallas_call(
        functools.partial(_hgrn_kernel, seq=seq, has_state=has_state),
        grid=(n_batch,),
        in_specs=in_specs, out_specs=out_specs, out_shape=out_shape,
        scratch_shapes=[pltpu.VMEM((seq, HG_W), F32)] * 7 + [pltpu.VMEM((HG_W, HG_W), F32)] * 2,
        compiler_params=_params("arbitrary"),
        name="hgrn_lat" if has_state else "hgrn_ctx",
    )(*args)


def _dft_tables(n):
    j = np.arange(n, dtype=np.int64)
    ang = 2.0 * np.pi * ((j[:, None] * j[None, :]) % n).astype(np.float64) / n
    return np.cos(ang) / np.sqrt(n), np.sin(ang) / np.sqrt(n)


def _fourier_tables(seq):
    gw = FN_WIDTH // FN_GROUPS
    cg, sg = _dft_tables(gw)
    eye = np.eye(FN_GROUPS)
    chan = np.concatenate([np.kron(eye, cg), np.kron(eye, sg)], axis=1)
    ct, st = _dft_tables(seq)
    pos = np.concatenate([ct, -st], axis=1)
    return jnp.asarray(chan, F32).astype(BF16), jnp.asarray(pos, F32).astype(BF16)


def _fourier_kernel(x_ref, chan_ref, pos_ref, w_ref, o_ref):
    x12 = _dot(x_ref[...].astype(BF16), chan_ref[...])
    z = jnp.concatenate([x12[:, :FN_WIDTH], x12[:, FN_WIDTH:]], axis=0).astype(BF16)
    y = _dot(pos_ref[...], z)
    o_ref[...] = _dot(y.astype(BF16), w_ref[...]).astype(o_ref.dtype)


def _fourier(u_fn, w, *, seq, n_batch, row_block0):
    chan, pos = _fourier_tables(seq)
    full = lambda shape: pl.BlockSpec(shape, lambda b: (0, 0))
    return pl.pallas_call(
        _fourier_kernel,
        grid=(n_batch,),
        in_specs=[pl.BlockSpec((seq, FN_WIDTH), lambda b: (row_block0 + b, 0)),
                  full((FN_WIDTH, 2 * FN_WIDTH)), full((seq, 2 * seq)), full((FN_WIDTH, FN_WIDTH))],
        out_specs=pl.BlockSpec((seq, FN_WIDTH), lambda b: (b, 0)),
        out_shape=jax.ShapeDtypeStruct((n_batch * seq, FN_WIDTH), BF16),
        compiler_params=_params("arbitrary"),
        name="fourier",
    )(u_fn, chan, pos, w)


SWA_SCALE = SWA_HD ** -0.5
SWA_QW = SWA_HEADS * SWA_HD
SWA_KW = SWA_KV_HEADS * SWA_HD
SWA_STACK_ORDER = (0, 2, 1, 3)


def _swa_stack_q(q):
    return jnp.concatenate([_stack_heads(q[:, :SWA_KW], SWA_KV_HEADS, SWA_HD),
                            _stack_heads(q[:, SWA_KW:], SWA_KV_HEADS, SWA_HD)], axis=0)


def _swa_unstack_o(o):
    t = o.shape[0] // SWA_HEADS
    return jnp.concatenate([_unstack_heads(o[:2 * t], SWA_KV_HEADS, SWA_HD),
                            _unstack_heads(o[2 * t:], SWA_KV_HEADS, SWA_HD)], axis=1)


def _sink_rows(sink_ref, t):
    return jnp.concatenate([jnp.full((t, 1), sink_ref[h], F32) for h in SWA_STACK_ORDER], axis=0)


def _swa_ctx_kernel(sink_ref, u_ref, o_ref):
    u = u_ref[...]
    qs = _swa_stack_q(u[:, :SWA_QW]).astype(BF16)
    k = u[:, SWA_QW:SWA_QW + SWA_KW].astype(BF16)
    v = u[:, SWA_QW + SWA_KW:].astype(BF16)
    s = _dot_nt(qs, k) * SWA_SCALE
    sink = _sink_rows(sink_ref, SEQ)
    m = jnp.maximum(jnp.max(s, axis=-1, keepdims=True), sink)
    p = jnp.exp(s - m)
    denom = jnp.sum(p, axis=-1, keepdims=True) + jnp.exp(sink - m)
    o_ref[...] = _swa_unstack_o(_dot(p.astype(BF16), v) / denom).astype(o_ref.dtype)


def _swa_ctx(u_swa, sink):
    return pl.pallas_call(
        _swa_ctx_kernel,
        grid=(BATCH,),
        in_specs=[pl.BlockSpec(memory_space=pltpu.SMEM),
                  pl.BlockSpec((SEQ, SWA_IN), lambda b: (b, 0))],
        out_specs=pl.BlockSpec((SEQ, SWA_QW), lambda b: (b, 0)),
        out_shape=jax.ShapeDtypeStruct((N_CTX, SWA_QW), BF16),
        compiler_params=_params("arbitrary"),
        name="swa_ctx",
    )(sink, u_swa)


SWA_PAD = DEC_SEQ + 2 * SWA_QBLK


def _swa_lat_kernel(sink_ref, u_ref, kc_ref, vc_ref, cosq_ref, sinq_ref, cosk_ref, sin_k_ref,
                    o_ref, k_s, v_s):
    i = pl.program_id(1)
    B = SWA_QBLK

    @pl.when(i == 0)
    def _():
        zeros = jnp.zeros((B, SWA_KW), BF16)
        k = _rope(u_ref[:, SWA_QW:SWA_QW + SWA_KW], cosk_ref[...], sin_k_ref[...]).astype(BF16)
        k_s[...] = jnp.concatenate([zeros, k, zeros], axis=0)
        v_s[...] = jnp.concatenate([zeros, u_ref[:, SWA_QW + SWA_KW:].astype(BF16), zeros], axis=0)

    r0 = pl.multiple_of(i * B, B)
    q = _rope(u_ref[pl.ds(r0, B), 0:SWA_QW], cosq_ref[pl.ds(r0, B), :], sinq_ref[pl.ds(r0, B), :])
    qs = _swa_stack_q(q).astype(BF16)
    s_loc = _dot_nt(qs, k_s[pl.ds(r0, 3 * B), :]) * SWA_SCALE
    row = lax.broadcasted_iota(jnp.int32, s_loc.shape, 0) % B
    col = lax.broadcasted_iota(jnp.int32, s_loc.shape, 1)
    kpos = r0 - B + col
    valid = (jnp.abs(row + B - col) <= WINDOW) & (kpos >= 0) & (kpos < DEC_SEQ)
    s_loc = jnp.where(valid, s_loc, NEG_INF)
    s_ctx = _dot_nt(qs, kc_ref[...].astype(BF16)) * SWA_SCALE
    sink = _sink_rows(sink_ref, B)
    m = jnp.maximum(jnp.maximum(jnp.max(s_loc, axis=-1, keepdims=True),
                                jnp.max(s_ctx, axis=-1, keepdims=True)), sink)
    p_loc = jnp.exp(s_loc - m)
    p_ctx = jnp.exp(s_ctx - m)
    denom = (jnp.sum(p_loc, axis=-1, keepdims=True) + jnp.sum(p_ctx, axis=-1, keepdims=True)
             + jnp.exp(sink - m))
    o = _dot(p_loc.astype(BF16), v_s[pl.ds(r0, 3 * B), :]) + _dot(p_ctx.astype(BF16), vc_ref[...].astype(BF16))
    o_ref[...] = _swa_unstack_o(o / denom).astype(o_ref.dtype)


def _swa_lat(u_swa, cache_k, cache_v, l, sink, cosq, sinq, cosk, sink_k):
    full = lambda shape: pl.BlockSpec(shape, lambda b, i: (0, 0))
    nq = DEC_SEQ // SWA_QBLK
    cache_spec = pl.BlockSpec((None, None, PAST_LEN, SWA_KW), lambda b, i: (b, l, 0, 0))
    return pl.pallas_call(
        _swa_lat_kernel,
        grid=(DEC_BATCH, nq),
        in_specs=[pl.BlockSpec(memory_space=pltpu.SMEM),
                  pl.BlockSpec((DEC_SEQ, SWA_IN), lambda b, i: (N_CTX // DEC_SEQ + b, 0)),
                  cache_spec, cache_spec,
                  full((DEC_SEQ, SWA_QW)), full((DEC_SEQ, SWA_QW)),
                  full((DEC_SEQ, SWA_KW)), full((DEC_SEQ, SWA_KW))],
        out_specs=pl.BlockSpec((SWA_QBLK, SWA_QW), lambda b, i: (b * nq + i, 0)),
        out_shape=jax.ShapeDtypeStruct((N_LAT, SWA_QW), BF16),
        scratch_shapes=[pltpu.VMEM((SWA_PAD, SWA_KW), BF16), pltpu.VMEM((SWA_PAD, SWA_KW), BF16)],
        compiler_params=_params("arbitrary", "arbitrary"),
        name="swa_lat",
    )(sink, u_swa, cache_k, cache_v, cosq, sinq, cosk, sink_k)


def _out_kernel(x_ref, omla_ref, ohg_ref, ofn_ref, oswa_ref, mod_ref, g_ref, wo_ref,
                wr_ref, br_ref, wsg_ref, wsu_ref, wsd_ref, x1_ref, h_ref, gate_ref):
    mix = jnp.zeros((ROW_TILE, D_MODEL), F32)
    for j, ref in enumerate((omla_ref, ohg_ref, ofn_ref, oswa_ref)):
        mix = mix + _dot(ref[...], wo_ref[j * 256:(j + 1) * 256, :])
    g1 = mod_ref[:, 2 * D_MODEL:3 * D_MODEL]
    sh2 = mod_ref[:, 3 * D_MODEL:4 * D_MODEL]
    sc2 = mod_ref[:, 4 * D_MODEL:5 * D_MODEL]
    g2 = mod_ref[:, 5 * D_MODEL:6 * D_MODEL]
    x1 = x_ref[...] + g1 * mix
    h = _rms(x1, g_ref[...]) * (1.0 + sc2) + sh2
    hb = h.astype(BF16)
    h_ref[...] = hb

    scores = jax.nn.sigmoid(_dot_f32(h, wr_ref[...]))
    sel = scores + br_ref[...]
    lane = lax.broadcasted_iota(jnp.int32, sel.shape, 1)
    gate = jnp.zeros_like(scores)
    for _ in range(TOP_K):
        best = jnp.max(sel, axis=-1, keepdims=True)
        first = jnp.min(jnp.where(sel == best, lane, N_EXPERTS), axis=-1, keepdims=True)
        pick = lane == first
        gate = jnp.where(pick, scores, gate)
        sel = jnp.where(pick, -jnp.inf, sel)
    gate_ref[...] = ROUTE_SCALE * gate / jnp.sum(gate, axis=-1, keepdims=True)

    hid = _silu(_dot(hb, wsg_ref[...])) * _dot(hb, wsu_ref[...])
    x1_ref[...] = x1 + g2 * _dot(hid.astype(BF16), wsd_ref[...])


def _out_proj(x, o_mla, o_hg, o_fn, o_swa, mod, g, wo, wr, br, wsg, wsu, wsd):
    row = lambda i: (i, 0)
    full = lambda shape: pl.BlockSpec(shape, lambda i: (0, 0))
    mix_spec = pl.BlockSpec((ROW_TILE, 256), row)
    return pl.pallas_call(
        _out_kernel,
        grid=(N_TOK // ROW_TILE,),
        in_specs=[pl.BlockSpec((ROW_TILE, D_MODEL), row), mix_spec, mix_spec, mix_spec, mix_spec,
                  pl.BlockSpec((None, 1, 6 * D_MODEL), lambda i: (_mod_row(i), 0, 0)),
                  full((1, D_MODEL)), full((D_MODEL, D_MODEL)),
                  full((D_MODEL, N_EXPERTS)), full((1, N_EXPERTS)),
                  full((D_MODEL, D_SHARED)), full((D_MODEL, D_SHARED)), full((D_SHARED, D_MODEL))],
        out_specs=[pl.BlockSpec((ROW_TILE, D_MODEL), row), pl.BlockSpec((ROW_TILE, D_MODEL), row),
                   pl.BlockSpec((ROW_TILE, N_EXPERTS), row)],
        out_shape=[jax.ShapeDtypeStruct((N_TOK, D_MODEL), F32),
                   jax.ShapeDtypeStruct((N_TOK, D_MODEL), BF16),
                   jax.ShapeDtypeStruct((N_TOK, N_EXPERTS), F32)],
        compiler_params=_params("arbitrary"),
        name="out_proj",
    )(x, o_mla, o_hg, o_fn, o_swa, mod, g, wo, wr, br, wsg, wsu, wsd)


def _moe_kernel(h_ref, gate_ref, wg_ref, wu_ref, wd_ref, o_ref):
    e = pl.program_id(1)

    @pl.when(e == 0)
    def _():
        o_ref[...] = jnp.zeros_like(o_ref)

    h = h_ref[...]
    gate = gate_ref[...]
    lane = lax.broadcasted_iota(jnp.int32, gate.shape, 1)
    gcol = jnp.sum(jnp.where(lane == e, gate, 0.0), axis=-1, keepdims=True)
    hid = _silu(_dot(h, wg_ref[...].astype(BF16))) * _dot(h, wu_ref[...].astype(BF16)) * gcol
    o_ref[...] += _dot(hid.astype(BF16), wd_ref[...].astype(BF16))


def _moe(h, gate, w_gate, w_up, w_down, l):
    tile = lambda i, e: (i, 0)
    return pl.pallas_call(
        _moe_kernel,
        grid=(N_TOK // MOE_TOK_TILE, N_EXPERTS),
        in_specs=[pl.BlockSpec((MOE_TOK_TILE, D_MODEL), tile),
                  pl.BlockSpec((MOE_TOK_TILE, N_EXPERTS), tile),
                  pl.BlockSpec((None, None, D_MODEL, D_EXPERT), lambda i, e: (l, e, 0, 0)),
                  pl.BlockSpec((None, None, D_MODEL, D_EXPERT), lambda i, e: (l, e, 0, 0)),
                  pl.BlockSpec((None, None, D_EXPERT, D_MODEL), lambda i, e: (l, e, 0, 0))],
        out_specs=pl.BlockSpec((MOE_TOK_TILE, D_MODEL), tile),
        out_shape=jax.ShapeDtypeStruct((N_TOK, D_MODEL), F32),
        compiler_params=_params("arbitrary", "arbitrary"),
        name="moe",
    )(h, gate, w_gate, w_up, w_down)


def _final_kernel(x_ref, r_ref, mod_ref, g_ref, o_ref):
    x = x_ref[...] + mod_ref[:, 5 * D_MODEL:6 * D_MODEL] * r_ref[...]
    o_ref[...] = _rms(x, g_ref[...])


def _final(x1, routed, mod, g):
    row = lambda i: (i, 0)
    xspec = pl.BlockSpec((ROW_TILE, D_MODEL), row)
    return pl.pallas_call(
        _final_kernel,
        grid=(N_TOK // ROW_TILE,),
        in_specs=[xspec, xspec, pl.BlockSpec((None, 1, 6 * D_MODEL), lambda i: (_mod_row(i), 0, 0)),
                  pl.BlockSpec((1, D_MODEL), lambda i: (0, 0))],
        out_specs=xspec,
        out_shape=jax.ShapeDtypeStruct((N_TOK, D_MODEL), F32),
        compiler_params=_params("arbitrary"),
        name="final_norm",
    )(x1, routed, mod, g)


def _rope_full_tables(dim, n_rep):
    rows = DEC_SEQ // GRID_W
    r_idx, c_idx = np.meshgrid(np.arange(rows), np.arange(GRID_W), indexing='ij')
    pos = jnp.asarray(np.stack([r_idx.reshape(-1), c_idx.reshape(-1)], axis=-1), F32)
    nf = dim // 4
    inv = ROPE_BASE ** (-jnp.arange(nf, dtype=F32) / nf)
    ang = pos[:, :, None] * inv
    ang = jnp.repeat(ang.reshape(DEC_SEQ, 2 * nf), 2, axis=1)
    sign = jnp.tile(jnp.asarray([-1.0, 1.0], F32), dim // 2)
    return jnp.tile(jnp.cos(ang), (1, n_rep)), jnp.tile(jnp.sin(ang) * sign, (1, n_rep))


def _pack_w_in(w):
    c0 = MLA_Q_LORA + MLA_KV_LORA
    kr = w[:, c0:MLA_IN]
    s0 = MLA_IN + HG_IN + FN_IN
    qh = [w[:, s0 + h * SWA_HD:s0 + (h + 1) * SWA_HD] for h in SWA_STACK_ORDER]
    return jnp.concatenate([w[:, :c0], kr, kr, kr, kr, w[:, MLA_IN:s0]] + qh
                           + [w[:, s0 + SWA_QW:]], axis=1).astype(BF16)


def _pack_w_q_up(w):
    hd = MLA_NOPE + MLA_ROPE
    nope = [w[:, h * hd:h * hd + MLA_NOPE] for h in range(MLA_HEADS)]
    rope = [w[:, h * hd + MLA_NOPE:(h + 1) * hd] for h in range(MLA_HEADS)]
    return jnp.concatenate(nope + rope, axis=1).astype(BF16)


def _pack_w_kv_up(w):
    hd = MLA_NOPE + MLA_V
    kn = [w[:, h * hd:h * hd + MLA_NOPE] for h in range(MLA_HEADS)]
    vv = [w[:, h * hd + MLA_NOPE:(h + 1) * hd] for h in range(MLA_HEADS)]
    return jnp.concatenate(kn + vv, axis=1).astype(BF16)


def _pack_w_out(w):
    s0 = 3 * 256
    rows = [w[s0 + h * SWA_HD:s0 + (h + 1) * SWA_HD] for h in SWA_STACK_ORDER]
    return jnp.concatenate([w[:s0]] + rows, axis=0).astype(BF16)


def kernel(x_prompt, x_sample, c, cache_mla_ckv, cache_mla_krope, cache_swa_k, cache_swa_v, state_hgrn,
           c_ctx, w_ada, b_ada, norm1_g, norm2_g, w_in, mla_q_norm_g, mla_w_q_up, mla_kv_norm_g, mla_w_kv_up,
           hg_lb_logits, hg_norm_g, fn_w, swa_sink, w_out, moe_w_router, moe_b_router, moe_w_gate, moe_w_up,
           moe_w_down, sh_w_gate, sh_w_up, sh_w_down, final_norm_g):
    x = jnp.concatenate([x_prompt.reshape(N_CTX, D_MODEL), x_sample.reshape(N_LAT, D_MODEL)], axis=0)
    cv8 = jnp.concatenate([c_ctx[None, :], c, jnp.zeros((8 - 1 - DEC_BATCH, D_MODEL), F32)], axis=0)
    mods = _ada(cv8, w_ada, b_ada).reshape(DEPTH, 8, 1, 6 * D_MODEL)

    lb = jnp.cumsum(jax.nn.softmax(hg_lb_logits.astype(F32), axis=1), axis=1)
    lb = lb - lb[:, :1]

    cos_m, sin_m = _rope_full_tables(MLA_ROPE, MLA_HEADS)
    cos_q, sin_q = _rope_full_tables(SWA_HD, SWA_HEADS)
    cos_k, sin_k = cos_q[:, :SWA_KW], sin_q[:, :SWA_KW]
    cache_k = cache_swa_k.reshape(DEC_BATCH, DEPTH, PAST_LEN, SWA_KW)
    cache_v = cache_swa_v.reshape(DEC_BATCH, DEPTH, PAST_LEN, SWA_KW)
    state_t = jnp.swapaxes(state_hgrn, -1, -2)

    ctx_blk_lat = N_CTX // DEC_SEQ
    routed = None
    x1 = None
    new_ckv, new_kr, new_k, new_v, new_st = [], [], [], [], []
    for l in range(DEPTH):
        w_in_p = _pack_w_in(w_in[l])
        if l == 0:
            u_mla, u_hg, u_fn, u_swa = _in_proj(x, None, None, mods[l], norm1_g[l][None], w_in_p)
        else:
            x, u_mla, u_hg, u_fn, u_swa = _in_proj(x1, routed, mods[l - 1], mods[l], norm1_g[l][None], w_in_p)

        qg, kvg = mla_q_norm_g[l][None], mla_kv_norm_g[l][None]
        wq, wkv = _pack_w_q_up(mla_w_q_up[l]), _pack_w_kv_up(mla_w_kv_up[l])
        o_mla_c, ckv_c = _mla_ctx(u_mla, qg, wq, kvg, wkv)
        o_mla_l = _mla_lat(u_mla, cache_mla_ckv, cache_mla_krope, l, cos_m, sin_m, qg, wq, kvg, wkv)

        lbf, lbb = lb[0, l][None], lb[1, l][None]
        ng4 = jnp.tile(hg_norm_g[l], HG_HEADS)[None]
        o_hg_c, st_c = _hgrn(u_hg, lbf, lbb, ng4, None, seq=SEQ, n_batch=BATCH, row_block0=0)
        o_hg_l = _hgrn(u_hg, lbf, lbb, ng4, state_t[:, l], seq=DEC_SEQ, n_batch=DEC_BATCH,
                       row_block0=ctx_blk_lat)

        fw = fn_w[l].astype(BF16)
        o_fn_c = _fourier(u_fn, fw, seq=SEQ, n_batch=BATCH, row_block0=0)
        o_fn_l = _fourier(u_fn, fw, seq=DEC_SEQ, n_batch=DEC_BATCH, row_block0=ctx_blk_lat)

        sink = swa_sink[l]
        o_swa_c = _swa_ctx(u_swa, sink)
        o_swa_l = _swa_lat(u_swa, cache_k, cache_v, l, sink, cos_q, sin_q, cos_k, sin_k)

        cat = lambda a, b: jnp.concatenate([a, b], axis=0)
        x1, h2, gate = _out_proj(
            x, cat(o_mla_c, o_mla_l), cat(o_hg_c, o_hg_l), cat(o_fn_c, o_fn_l), cat(o_swa_c, o_swa_l),
            mods[l], norm2_g[l][None], _pack_w_out(w_out[l]),
            moe_w_router[l], moe_b_router[l][None],
            sh_w_gate[l].astype(BF16), sh_w_up[l].astype(BF16), sh_w_down[l].astype(BF16))
        routed = _moe(h2, gate, moe_w_gate, moe_w_up, moe_w_down, l)

        new_ckv.append(ckv_c.reshape(BATCH, SEQ, MLA_KV_LORA))
        new_kr.append(u_mla[:N_CTX, MLA_Q_LORA + MLA_KV_LORA:MLA_IN].reshape(BATCH, SEQ, MLA_ROPE))
        new_k.append(u_swa[:N_CTX, SWA_QW:SWA_QW + SWA_KW].reshape(BATCH, SEQ, SWA_KV_HEADS, SWA_HD))
        new_v.append(u_swa[:N_CTX, SWA_QW + SWA_KW:].reshape(BATCH, SEQ, SWA_KV_HEADS, SWA_HD))
        new_st.append(jnp.swapaxes(st_c, -1, -2))

    y = _final(x1, routed, mods[DEPTH - 1], final_norm_g[None])
    y_prompt = y[:N_CTX].reshape(BATCH, SEQ, D_MODEL)
    y_sample = y[N_CTX:].reshape(DEC_BATCH, DEC_SEQ, D_MODEL)
    stack = lambda xs: jnp.stack(xs, axis=1)
    return (y_prompt, y_sample, stack(new_ckv), stack(new_kr), stack(new_k), stack(new_v), stack(new_st))
```

```python
import functools

import numpy as np
import jax
import jax.numpy as jnp
from jax import lax
from jax.experimental import pallas as pl
from jax.experimental.pallas import tpu as pltpu

F32 = jnp.float32
BF16 = jnp.bfloat16

D_MODEL = 1024
BATCH = 32
SEQ = 256
DEPTH = 2
DEC_BATCH = 2
DEC_SEQ = 1024
PAST_LEN = 256
GRID_W = 64
EPS = 1e-6
ROPE_BASE = 10000.0
NEG_INF = -1e30

MLA_HEADS = 4
MLA_NOPE = 64
MLA_ROPE = 32
MLA_V = 64
MLA_Q_LORA = 256
MLA_KV_LORA = 128
HG_HEADS = 4
HG_DK = 64
HG_DV = 64
HG_W = HG_HEADS * HG_DK
FN_GROUPS = 4
FN_WIDTH = 256
SWA_HEADS = 4
SWA_KV_HEADS = 2
SWA_HD = 64
WINDOW = 128
N_EXPERTS = 64
TOP_K = 6
D_EXPERT = 256
D_SHARED = 256
ROUTE_SCALE = 2.5

MLA_IN = MLA_Q_LORA + MLA_KV_LORA + MLA_ROPE
HG_IN = 3 * HG_HEADS * HG_DK + 2 * HG_HEADS * HG_DV
FN_IN = FN_WIDTH
SWA_IN = (SWA_HEADS + 2 * SWA_KV_HEADS) * SWA_HD

N_CTX = BATCH * SEQ
N_LAT = DEC_BATCH * DEC_SEQ
N_TOK = N_CTX + N_LAT

MLA_PACK = 512
U_COLS = MLA_PACK + HG_IN + FN_IN + SWA_IN

ROW_TILE = 256
CTX_TILES = N_CTX // ROW_TILE
LAT_TILES_PER_BATCH = DEC_SEQ // ROW_TILE
HG_CHUNK = 32
MOE_TOK_TILE = 2048
SWA_QBLK = 128
MLA_QBLK = 256
VMEM_LIMIT = 56 * 1024 * 1024


def _dot(a, b):
    return jnp.dot(a, b, preferred_element_type=F32)


def _dot_nt(a, b):
    return lax.dot_general(a, b, (((1,), (1,)), ((), ())), preferred_element_type=F32)


def _dot_tn(a, b):
    return lax.dot_general(a, b, (((0,), (0,)), ((), ())), preferred_element_type=F32)


def _dot_f32(a, b):
    return jnp.dot(a, b, precision=lax.Precision.HIGHEST, preferred_element_type=F32)


def _rms(x, g):
    return x * lax.rsqrt(jnp.mean(x * x, axis=-1, keepdims=True) + EPS) * g


def _silu(x):
    return x * jax.nn.sigmoid(x)


def _mod_row(i):
    return jnp.where(i < CTX_TILES, 0, 1 + (i - CTX_TILES) // LAT_TILES_PER_BATCH)


def _params(*sem):
    return pltpu.CompilerParams(dimension_semantics=sem, vmem_limit_bytes=VMEM_LIMIT)


ADA_COLS = 1536


def _ada_kernel(cv_ref, w_ref, b_ref, o_ref):
    a = _silu(cv_ref[...]).astype(BF16)
    o_ref[...] = _dot(a, w_ref[...].astype(BF16)) + b_ref[...]


def _ada(cv8, w_ada, b_ada):
    return pl.pallas_call(
        _ada_kernel,
        grid=(DEPTH, 6 * D_MODEL // ADA_COLS),
        in_specs=[
            pl.BlockSpec((8, D_MODEL), lambda l, j: (0, 0)),
            pl.BlockSpec((None, D_MODEL, ADA_COLS), lambda l, j: (l, 0, j)),
            pl.BlockSpec((None, 1, ADA_COLS), lambda l, j: (l, 0, j)),
        ],
        out_specs=pl.BlockSpec((None, 8, ADA_COLS), lambda l, j: (l, 0, j)),
        out_shape=jax.ShapeDtypeStruct((DEPTH, 8, 6 * D_MODEL), F32),
        compiler_params=_params("arbitrary", "arbitrary"),
        name="ada",
    )(cv8, w_ada, b_ada.reshape(DEPTH, 1, 6 * D_MODEL))


def _in_kernel(x_ref, mod_ref, g_ref, w_ref, umla_ref, uhg_ref, ufn_ref, uswa_ref):
    x = x_ref[...]
    sh1 = mod_ref[:, 0:D_MODEL]
    sc1 = mod_ref[:, D_MODEL:2 * D_MODEL]
    h = _rms(x, g_ref[...]) * (1.0 + sc1) + sh1
    u = _dot(h.astype(BF16), w_ref[...])
    o = 0
    for ref, width in ((umla_ref, MLA_PACK), (uhg_ref, HG_IN), (ufn_ref, FN_IN), (uswa_ref, SWA_IN)):
        ref[...] = u[:, o:o + width].astype(ref.dtype)
        o += width


def _in_proj(x, mod, g, w):
    row = lambda i: (i, 0)
    widths = (MLA_PACK, HG_IN, FN_IN, SWA_IN)
    return pl.pallas_call(
        _in_kernel,
        grid=(N_TOK // ROW_TILE,),
        in_specs=[pl.BlockSpec((ROW_TILE, D_MODEL), row),
                  pl.BlockSpec((None, 1, 6 * D_MODEL), lambda i: (_mod_row(i), 0, 0)),
                  pl.BlockSpec((1, D_MODEL), lambda i: (0, 0)),
                  pl.BlockSpec((D_MODEL, U_COLS), lambda i: (0, 0))],
        out_specs=[pl.BlockSpec((ROW_TILE, wd), row) for wd in widths],
        out_shape=[jax.ShapeDtypeStruct((N_TOK, wd), F32) for wd in widths],
        compiler_params=_params("arbitrary"),
        name="in_proj",
    )(x, mod, g, w)


def _rope(x, cos, sin_signed):
    lane = lax.broadcasted_iota(jnp.int32, x.shape, 1)
    width = x.shape[1]
    swapped = jnp.where(lane % 2 == 0, pltpu.roll(x, width - 1, 1), pltpu.roll(x, 1, 1))
    return x * cos + swapped * sin_signed


def _stack_heads(x, n_heads, head_w):
    lane = lax.broadcasted_iota(jnp.int32, x.shape, 1)
    return jnp.concatenate([jnp.where(lane // head_w == h, x, 0.0) for h in range(n_heads)], axis=0)


def _unstack_heads(o, n_heads, head_w):
    t = o.shape[0] // n_heads
    lane = lax.broadcasted_iota(jnp.int32, (t, o.shape[1]), 1)
    out = jnp.zeros((t, o.shape[1]), F32)
    for h in range(n_heads):
        out = jnp.where(lane // head_w == h, o[h * t:(h + 1) * t], out)
    return out


MLA_SCALE = (MLA_NOPE + MLA_ROPE) ** -0.5
MLA_QW = MLA_HEADS * MLA_NOPE + MLA_HEADS * MLA_ROPE
MLA_NW = MLA_HEADS * MLA_NOPE


def _mla_attend(q, kcat, v):
    qs = jnp.concatenate([_stack_heads(q[:, :MLA_NW], MLA_HEADS, MLA_NOPE),
                          _stack_heads(q[:, MLA_NW:], MLA_HEADS, MLA_ROPE)], axis=1)
    s = _dot_nt(qs.astype(BF16), kcat) * MLA_SCALE
    p = jnp.exp(s - jnp.max(s, axis=-1, keepdims=True))
    o = _dot(p.astype(BF16), v) / jnp.sum(p, axis=-1, keepdims=True)
    return _unstack_heads(o, MLA_HEADS, MLA_V)


def _mla_ctx_kernel(u_ref, qg_ref, wq_ref, kvg_ref, wkv_ref, o_ref, ckv_ref):
    u = u_ref[...]
    q = _dot(_rms(u[:, :MLA_Q_LORA], qg_ref[...]).astype(BF16), wq_ref[...])
    ckv = _rms(u[:, MLA_Q_LORA:MLA_Q_LORA + MLA_KV_LORA], kvg_ref[...])
    ckv_ref[...] = ckv
    kv = _dot(ckv.astype(BF16), wkv_ref[...])
    kr4 = u[:, MLA_Q_LORA + MLA_KV_LORA:]
    kcat = jnp.concatenate([kv[:, :MLA_NW], kr4], axis=1).astype(BF16)
    o_ref[...] = _mla_attend(q, kcat, kv[:, MLA_NW:].astype(BF16)).astype(o_ref.dtype)


def _mla_ctx(u_mla, qg, wq, kvg, wkv):
    full = lambda shape: pl.BlockSpec(shape, lambda b: (0, 0))
    return pl.pallas_call(
        _mla_ctx_kernel,
        grid=(BATCH,),
        in_specs=[pl.BlockSpec((SEQ, MLA_PACK), lambda b: (b, 0)),
                  full((1, MLA_Q_LORA)), full((MLA_Q_LORA, MLA_QW)),
                  full((1, MLA_KV_LORA)), full((MLA_KV_LORA, 2 * MLA_NW))],
        out_specs=[pl.BlockSpec((SEQ, MLA_NW), lambda b: (b, 0)),
                   pl.BlockSpec((SEQ, MLA_KV_LORA), lambda b: (b, 0))],
        out_shape=[jax.ShapeDtypeStruct((N_CTX, MLA_NW), BF16),
                   jax.ShapeDtypeStruct((N_CTX, MLA_KV_LORA), F32)],
        compiler_params=_params("arbitrary"),
        name="mla_ctx",
    )(u_mla, qg, wq, kvg, wkv)


MLA_TK = PAST_LEN + DEC_SEQ


def _mla_lat_kernel(u_ref, cckv_ref, ckr_ref, cos_ref, sin_ref, qg_ref, wq_ref, kvg_ref, wkv_ref,
                    o_ref, kcat_s, v_s):
    i = pl.program_id(1)

    @pl.when(i == 0)
    def _():
        u = u_ref[...]
        ckv_new = _rms(u[:, MLA_Q_LORA:MLA_Q_LORA + MLA_KV_LORA], kvg_ref[...])
        ckv_all = jnp.concatenate([cckv_ref[...], ckv_new], axis=0)
        kv = _dot(ckv_all.astype(BF16), wkv_ref[...])
        kr_new = _rope(u[:, MLA_Q_LORA + MLA_KV_LORA:], cos_ref[...], sin_ref[...])
        ckr = ckr_ref[...]
        kr_all = jnp.concatenate([jnp.concatenate([ckr] * MLA_HEADS, axis=1), kr_new], axis=0)
        kcat_s[...] = jnp.concatenate([kv[:, :MLA_NW], kr_all], axis=1).astype(BF16)
        v_s[...] = kv[:, MLA_NW:].astype(BF16)

    r0 = pl.multiple_of(i * MLA_QBLK, MLA_QBLK)
    cq = u_ref[pl.ds(r0, MLA_QBLK), 0:MLA_Q_LORA]
    q = _dot(_rms(cq, qg_ref[...]).astype(BF16), wq_ref[...])
    qr = _rope(q[:, MLA_NW:], cos_ref[pl.ds(r0, MLA_QBLK), :], sin_ref[pl.ds(r0, MLA_QBLK), :])
    q = jnp.concatenate([q[:, :MLA_NW], qr], axis=1)
    o_ref[...] = _mla_attend(q, kcat_s[...], v_s[...]).astype(o_ref.dtype)


def _mla_lat(u_mla, cache_ckv, cache_kr, l, cos, sin, qg, wq, kvg, wkv):
    full = lambda shape: pl.BlockSpec(shape, lambda b, i: (0, 0))
    nq = DEC_SEQ // MLA_QBLK
    return pl.pallas_call(
        _mla_lat_kernel,
        grid=(DEC_BATCH, nq),
        in_specs=[pl.BlockSpec((DEC_SEQ, MLA_PACK), lambda b, i: (N_CTX // DEC_SEQ + b, 0)),
                  pl.BlockSpec((None, None, PAST_LEN, MLA_KV_LORA), lambda b, i: (b, l, 0, 0)),
                  pl.BlockSpec((None, None, PAST_LEN, MLA_ROPE), lambda b, i: (b, l, 0, 0)),
                  full((DEC_SEQ, MLA_HEADS * MLA_ROPE)), full((DEC_SEQ, MLA_HEADS * MLA_ROPE)),
                  full((1, MLA_Q_LORA)), full((MLA_Q_LORA, MLA_QW)),
                  full((1, MLA_KV_LORA)), full((MLA_KV_LORA, 2 * MLA_NW))],
        out_specs=pl.BlockSpec((MLA_QBLK, MLA_NW), lambda b, i: (b * nq + i, 0)),
        out_shape=jax.ShapeDtypeStruct((N_LAT, MLA_NW), BF16),
        scratch_shapes=[pltpu.VMEM((MLA_TK, MLA_QW), BF16), pltpu.VMEM((MLA_TK, MLA_NW), BF16)],
        compiler_params=_params("arbitrary", "arbitrary"),
        name="mla_lat",
    )(u_mla, cache_ckv, cache_kr, cos, sin, qg, wq, kvg, wkv)


def _hgrn_kernel(*refs, seq, has_state):
    if has_state:
        (u_ref, lbf_ref, lbb_ref, ng_ref, s0_ref, o_ref,
         q_s, kf_s, gf_s, kb_s, gb_s, of_s, ob_s, stf_s, stb_s) = refs
    else:
        (u_ref, lbf_ref, lbb_ref, ng_ref, o_ref, so_ref,
         q_s, kf_s, gf_s, kb_s, gb_s, of_s, ob_s, stf_s, stb_s) = refs
    C = HG_CHUNK
    W = HG_W
    n_chunks = seq // C

    q_s[...] = _silu(u_ref[:, 0:W])
    ff = lbf_ref[...] + (1.0 - lbf_ref[...]) * jax.nn.sigmoid(u_ref[:, W:2 * W])
    kf_s[...] = 1.0 - ff
    gf_s[...] = jnp.log(ff)
    fb = lbb_ref[...] + (1.0 - lbb_ref[...]) * jax.nn.sigmoid(u_ref[:, 2 * W:3 * W])
    kb_s[...] = 1.0 - fb
    gb_s[...] = jnp.log(fb)

    rr = lax.broadcasted_iota(jnp.int32, (W, W), 0)
    cc = lax.broadcasted_iota(jnp.int32, (W, W), 1)
    blockdiag = rr // HG_DK == cc // HG_DK
    if has_state:
        for st, d in ((stf_s, 0), (stb_s, 1)):
            rows = []
            for h in range(HG_HEADS):
                z = lambda n: jnp.zeros((HG_DV, n * HG_DK), F32)
                parts = ([z(h)] if h else []) + [s0_ref[d, h]] + ([z(HG_HEADS - 1 - h)] if h < HG_HEADS - 1 else [])
                rows.append(jnp.concatenate(parts, axis=1) if len(parts) > 1 else parts[0])
            st[...] = jnp.concatenate(rows, axis=0)
    else:
        stf_s[...] = jnp.zeros((W, W), F32)
        stb_s[...] = jnp.zeros((W, W), F32)

    ri = lax.broadcasted_iota(jnp.int32, (C, C), 0)
    ci = lax.broadcasted_iota(jnp.int32, (C, C), 1)
    lower = (ri >= ci).astype(F32)
    upper = (ci >= ri).astype(F32)
    rs = lax.broadcasted_iota(jnp.int32, (HG_HEADS * C, C), 0) % C
    cs = lax.broadcasted_iota(jnp.int32, (HG_HEADS * C, C), 1)

    def chunk(r, k_s, g_s, o_s, st_s, tri, keep, mid, last):
        q = q_s[pl.ds(r, C), :]
        k = k_s[pl.ds(r, C), :]
        v = u_ref[pl.ds(r, C), 3 * W:4 * W].astype(BF16)
        G = _dot_f32(tri, g_s[pl.ds(r, C), :])
        Gm = G[mid:mid + 1, :]
        Gl = G[last:last + 1, :]
        qe = _stack_heads(q * jnp.exp(G - Gm), HG_HEADS, HG_DK)
        ke = k * jnp.exp(Gm - G)
        A = jnp.where(keep, _dot_nt(qe.astype(BF16), ke.astype(BF16)), 0.0)
        o_intra = _unstack_heads(_dot(A.astype(BF16), v), HG_HEADS, HG_DV)
        st = st_s[...]
        o_inter = _dot_nt((q * jnp.exp(G)).astype(BF16), st.astype(BF16))
        o_s[pl.ds(r, C), :] = o_intra + o_inter
        k2 = (k * jnp.exp(Gl - G)).astype(BF16)
        st_s[...] = st * jnp.exp(Gl) + jnp.where(blockdiag, _dot_tn(v, k2), 0.0)

    def fwd(c, carry):
        r = pl.multiple_of(c * C, C)
        chunk(r, kf_s, gf_s, of_s, stf_s, lower, rs >= cs, C // 2 - 1, C - 1)
        return carry

    def bwd(c, carry):
        r = pl.multiple_of((n_chunks - 1 - c) * C, C)
        chunk(r, kb_s, gb_s, ob_s, stb_s, upper, cs >= rs, C // 2, 0)
        return carry

    lax.fori_loop(0, n_chunks, fwd, 0)
    lax.fori_loop(0, n_chunks, bwd, 0)

    o = of_s[...] + ob_s[...]
    ms = _dot_f32(o * o, jnp.where(blockdiag, 1.0 / HG_DV, 0.0))
    on = o * lax.rsqrt(ms + EPS) * ng_ref[...]
    o_ref[...] = (on * _silu(u_ref[:, 4 * W:5 * W])).astype(o_ref.dtype)

    if not has_state:
        for st, d in ((stf_s, 0), (stb_s, 1)):
            for h in range(HG_HEADS):
                so_ref[d, h] = st[h * HG_DV:(h + 1) * HG_DV, h * HG_DK:(h + 1) * HG_DK]


def _hgrn(u_hg, lbf, lbb, ng4, state_t, *, seq, n_batch, row_block0):
    has_state = state_t is not None
    full = lambda shape: pl.BlockSpec(shape, lambda b: (0, 0))
    in_specs = [pl.BlockSpec((seq, HG_IN), lambda b: (row_block0 + b, 0)),
                full((1, HG_W)), full((1, HG_W)), full((1, HG_W))]
    args = [u_hg, lbf, lbb, ng4]
    st_spec = pl.BlockSpec((None, 2, HG_HEADS, HG_DV, HG_DK), lambda b: (b, 0, 0, 0, 0))
    o_spec = pl.BlockSpec((seq, HG_W), lambda b: (b, 0))
    o_shape = jax.ShapeDtypeStruct((n_batch * seq, HG_W), BF16)
    if has_state:
        in_specs.append(st_spec)
        args.append(state_t)
        out_specs, out_shape = o_spec, o_shape
    else:
        out_specs = [o_spec, st_spec]
        out_shape = [o_shape, jax.ShapeDtypeStruct((n_batch, 2, HG_HEADS, HG_DV, HG_DK), F32)]
    return pl.pallas_call(
        functools.partial(_hgrn_kernel, seq=seq, has_state=has_state),
        grid=(n_batch,),
        in_specs=in_specs, out_specs=out_specs, out_shape=out_shape,
        scratch_shapes=[pltpu.VMEM((seq, HG_W), F32)] * 7 + [pltpu.VMEM((HG_W, HG_W), F32)] * 2,
        compiler_params=_params("arbitrary"),
        name="hgrn_lat" if has_state else "hgrn_ctx",
    )(*args)


def _dft_tables(n):
    j = np.arange(n, dtype=np.int64)
    ang = 2.0 * np.pi * ((j[:, None] * j[None, :]) % n).astype(np.float64) / n
    return np.cos(ang) / np.sqrt(n), np.sin(ang) / np.sqrt(n)


def _fourier_tables(seq):
    gw = FN_WIDTH // FN_GROUPS
    cg, sg = _dft_tables(gw)
    eye = np.eye(FN_GROUPS)
    chan = np.concatenate([np.kron(eye, cg), np.kron(eye, sg)], axis=1)
    ct, st = _dft_tables(seq)
    pos = np.concatenate([ct, -st], axis=1)
    return jnp.asarray(chan, F32).astype(BF16), jnp.asarray(pos, F32).astype(BF16)


def _fourier_kernel(x_ref, chan_ref, pos_ref, w_ref, o_ref):
    x12 = _dot(x_ref[...].astype(BF16), chan_ref[...])
    z = jnp.concatenate([x12[:, :FN_WIDTH], x12[:, FN_WIDTH:]], axis=0).astype(BF16)
    y = _dot(pos_ref[...], z)
    o_ref[...] = _dot(y.astype(BF16), w_ref[...]).astype(o_ref.dtype)


def _fourier(u_fn, w, *, seq, n_batch, row_block0):
    chan, pos = _fourier_tables(seq)
    full = lambda shape: pl.BlockSpec(shape, lambda b: (0, 0))
    return pl.pallas_call(
        _fourier_kernel,
        grid=(n_batch,),
        in_specs=[pl.BlockSpec((seq, FN_WIDTH), lambda b: (row_block0 + b, 0)),
                  full((FN_WIDTH, 2 * FN_WIDTH)), full((seq, 2 * seq)), full((FN_WIDTH, FN_WIDTH))],
        out_specs=pl.BlockSpec((seq, FN_WIDTH), lambda b: (b, 0)),
        out_shape=jax.ShapeDtypeStruct((n_batch * seq, FN_WIDTH), BF16),
        compiler_params=_params("arbitrary"),
        name="fourier",
    )(u_fn, chan, pos, w)


SWA_SCALE = SWA_HD ** -0.5
SWA_QW = SWA_HEADS * SWA_HD
SWA_KW = SWA_KV_HEADS * SWA_HD
SWA_STACK_ORDER = (0, 2, 1, 3)


def _swa_stack_q(q):
    return jnp.concatenate([_stack_heads(q[:, :SWA_KW], SWA_KV_HEADS, SWA_HD),
                            _stack_heads(q[:, SWA_KW:], SWA_KV_HEADS, SWA_HD)], axis=0)


def _swa_unstack_o(o):
    t = o.shape[0] // SWA_HEADS
    return jnp.concatenate([_unstack_heads(o[:2 * t], SWA_KV_HEADS, SWA_HD),
                            _unstack_heads(o[2 * t:], SWA_KV_HEADS, SWA_HD)], axis=1)


def _sink_rows(sink_ref, t):
    return jnp.concatenate([jnp.full((t, 1), sink_ref[h], F32) for h in SWA_STACK_ORDER], axis=0)


def _swa_ctx_kernel(sink_ref, u_ref, o_ref):
    u = u_ref[...]
    qs = _swa_stack_q(u[:, :SWA_QW]).astype(BF16)
    k = u[:, SWA_QW:SWA_QW + SWA_KW].astype(BF16)
    v = u[:, SWA_QW + SWA_KW:].astype(BF16)
    s = _dot_nt(qs, k) * SWA_SCALE
    sink = _sink_rows(sink_ref, SEQ)
    m = jnp.maximum(jnp.max(s, axis=-1, keepdims=True), sink)
    p = jnp.exp(s - m)
    denom = jnp.sum(p, axis=-1, keepdims=True) + jnp.exp(sink - m)
    o_ref[...] = _swa_unstack_o(_dot(p.astype(BF16), v) / denom).astype(o_ref.dtype)


def _swa_ctx(u_swa, sink):
    return pl.pallas_call(
        _swa_ctx_kernel,
        grid=(BATCH,),
        in_specs=[pl.BlockSpec(memory_space=pltpu.SMEM),
                  pl.BlockSpec((SEQ, SWA_IN), lambda b: (b, 0))],
        out_specs=pl.BlockSpec((SEQ, SWA_QW), lambda b: (b, 0)),
        out_shape=jax.ShapeDtypeStruct((N_CTX, SWA_QW), BF16),
        compiler_params=_params("arbitrary"),
        name="swa_ctx",
    )(sink, u_swa)


SWA_PAD = DEC_SEQ + 2 * SWA_QBLK


def _swa_lat_kernel(sink_ref, u_ref, kc_ref, vc_ref, cosq_ref, sinq_ref, cosk_ref, sin_k_ref,
                    o_ref, k_s, v_s):
    i = pl.program_id(1)
    B = SWA_QBLK

    @pl.when(i == 0)
    def _():
        zeros = jnp.zeros((B, SWA_KW), BF16)
        k = _rope(u_ref[:, SWA_QW:SWA_QW + SWA_KW], cosk_ref[...], sin_k_ref[...]).astype(BF16)
        k_s[...] = jnp.concatenate([zeros, k, zeros], axis=0)
        v_s[...] = jnp.concatenate([zeros, u_ref[:, SWA_QW + SWA_KW:].astype(BF16), zeros], axis=0)

    r0 = pl.multiple_of(i * B, B)
    q = _rope(u_ref[pl.ds(r0, B), 0:SWA_QW], cosq_ref[pl.ds(r0, B), :], sinq_ref[pl.ds(r0, B), :])
    qs = _swa_stack_q(q).astype(BF16)
    s_loc = _dot_nt(qs, k_s[pl.ds(r0, 3 * B), :]) * SWA_SCALE
    row = lax.broadcasted_iota(jnp.int32, s_loc.shape, 0) % B
    col = lax.broadcasted_iota(jnp.int32, s_loc.shape, 1)
    kpos = r0 - B + col
    valid = (jnp.abs(row + B - col) <= WINDOW) & (kpos >= 0) & (kpos < DEC_SEQ)
    s_loc = jnp.where(valid, s_loc, NEG_INF)
    s_ctx = _dot_nt(qs, kc_ref[...].astype(BF16)) * SWA_SCALE
    sink = _sink_rows(sink_ref, B)
    m = jnp.maximum(jnp.maximum(jnp.max(s_loc, axis=-1, keepdims=True),
                                jnp.max(s_ctx, axis=-1, keepdims=True)), sink)
    p_loc = jnp.exp(s_loc - m)
    p_ctx = jnp.exp(s_ctx - m)
    denom = (jnp.sum(p_loc, axis=-1, keepdims=True) + jnp.sum(p_ctx, axis=-1, keepdims=True)
             + jnp.exp(sink - m))
    o = _dot(p_loc.astype(BF16), v_s[pl.ds(r0, 3 * B), :]) + _dot(p_ctx.astype(BF16), vc_ref[...].astype(BF16))
    o_ref[...] = _swa_unstack_o(o / denom).astype(o_ref.dtype)


def _swa_lat(u_swa, cache_k, cache_v, l, sink, cosq, sinq, cosk, sink_k):
    full = lambda shape: pl.BlockSpec(shape, lambda b, i: (0, 0))
    nq = DEC_SEQ // SWA_QBLK
    cache_spec = pl.BlockSpec((None, None, PAST_LEN, SWA_KW), lambda b, i: (b, l, 0, 0))
    return pl.pallas_call(
        _swa_lat_kernel,
        grid=(DEC_BATCH, nq),
        in_specs=[pl.BlockSpec(memory_space=pltpu.SMEM),
                  pl.BlockSpec((DEC_SEQ, SWA_IN), lambda b, i: (N_CTX // DEC_SEQ + b, 0)),
                  cache_spec, cache_spec,
                  full((DEC_SEQ, SWA_QW)), full((DEC_SEQ, SWA_QW)),
                  full((DEC_SEQ, SWA_KW)), full((DEC_SEQ, SWA_KW))],
        out_specs=pl.BlockSpec((SWA_QBLK, SWA_QW), lambda b, i: (b * nq + i, 0)),
        out_shape=jax.ShapeDtypeStruct((N_LAT, SWA_QW), BF16),
        scratch_shapes=[pltpu.VMEM((SWA_PAD, SWA_KW), BF16), pltpu.VMEM((SWA_PAD, SWA_KW), BF16)],
        compiler_params=_params("arbitrary", "arbitrary"),
        name="swa_lat",
    )(sink, u_swa, cache_k, cache_v, cosq, sinq, cosk, sink_k)


MOE_CAP = 48
MOE_SEG = N_EXPERTS * MOE_CAP
N_BLK = N_TOK // ROW_TILE
SEG_CHUNKS = 4
SEG_CHUNK = MOE_SEG // SEG_CHUNKS
NOT_KEPT = 512.0


def _slot_table():
    r = np.arange(MOE_SEG)
    t = np.zeros((128, MOE_SEG), np.float32)
    t[r // MOE_CAP, r] = 1.0
    t[N_EXPERTS, r] = -(r % MOE_CAP)
    return t


def _out_kernel(x_ref, omla_ref, ohg_ref, ofn_ref, oswa_ref, mod_ref, g_ref, wo_ref,
                wrt_ref, br_ref, wsg_ref, wsu_ref, wsd_ref, slot_t_ref,
                x1_ref, h_ref, xs_ref, gate_ref, rank_ref, ovf_ref, novf_ref):
    mix = jnp.zeros((ROW_TILE, D_MODEL), F32)
    for j, ref in enumerate((omla_ref, ohg_ref, ofn_ref, oswa_ref)):
        mix = mix + _dot(ref[...], wo_ref[j * 256:(j + 1) * 256, :])
    g1 = mod_ref[:, 2 * D_MODEL:3 * D_MODEL]
    sh2 = mod_ref[:, 3 * D_MODEL:4 * D_MODEL]
    sc2 = mod_ref[:, 4 * D_MODEL:5 * D_MODEL]
    g2 = mod_ref[:, 5 * D_MODEL:6 * D_MODEL]
    x1 = x_ref[...] + g1 * mix
    h = _rms(x1, g_ref[...]) * (1.0 + sc2) + sh2
    hb = h.astype(BF16)
    h_ref[...] = hb

    logits = lax.dot_general(wrt_ref[...], h, (((1,), (1,)), ((), ())),
                             precision=lax.Precision.HIGHEST, preferred_element_type=F32)
    scores = jax.nn.sigmoid(logits)
    sel = scores + br_ref[...]
    eidx = lax.broadcasted_iota(jnp.int32, sel.shape, 0)
    gate = jnp.zeros_like(scores)
    picked = jnp.zeros_like(scores)
    for _ in range(TOP_K):
        best = jnp.max(sel, axis=0, keepdims=True)
        first = jnp.min(jnp.where(sel == best, eidx, N_EXPERTS), axis=0, keepdims=True)
        pick = eidx == first
        gate = jnp.where(pick, scores, gate)
        picked = jnp.where(pick, 1.0, picked)
        sel = jnp.where(pick, -jnp.inf, sel)
    gate = ROUTE_SCALE * gate / jnp.sum(gate, axis=0, keepdims=True)

    ti = lax.broadcasted_iota(jnp.int32, (ROW_TILE, ROW_TILE), 0)
    tj = lax.broadcasted_iota(jnp.int32, (ROW_TILE, ROW_TILE), 1)
    rank = _dot(picked.astype(BF16), jnp.where(ti < tj, 1.0, 0.0).astype(BF16))
    keep = (picked > 0.0) & (rank < MOE_CAP)
    over = (picked > 0.0) & (rank >= MOE_CAP)
    rank_code = jnp.where(keep, rank, NOT_KEPT)
    gate_ref[...] = jnp.where(keep, gate, 0.0)
    rank_ref[...] = rank_code
    ovf_ref[...] = jnp.where(over, gate, 0.0)
    novf_ref[...] = jnp.full(novf_ref.shape, jnp.sum(jnp.where(over, 1.0, 0.0)), F32)

    ones_row = jnp.where(eidx == 0, 1.0, 0.0)
    code = jnp.concatenate([rank_code, ones_row], axis=0).astype(BF16)
    seg_experts = N_EXPERTS // SEG_CHUNKS
    for c in range(SEG_CHUNKS):
        d = _dot(slot_t_ref[c * SEG_CHUNK:(c + 1) * SEG_CHUNK, :], code)
        onehot = jnp.where(d == 0.0, 1.0, 0.0).astype(BF16)
        rows = _dot(onehot, hb).astype(BF16)
        xs_ref[c * seg_experts:(c + 1) * seg_experts] = rows.reshape(seg_experts, MOE_CAP, D_MODEL)

    hid = _silu(_dot(hb, wsg_ref[...])) * _dot(hb, wsu_ref[...])
    x1_ref[...] = x1 + g2 * _dot(hid.astype(BF16), wsd_ref[...])


def _out_proj(x, o_mla, o_hg, o_fn, o_swa, mod, g, wo, wrt, br, wsg, wsu, wsd):
    row = lambda i: (i, 0)
    col = lambda i: (0, i)
    full = lambda shape: pl.BlockSpec(shape, lambda i: (0, 0))
    mix_spec = pl.BlockSpec((ROW_TILE, 256), row)
    slot_t = jnp.asarray(_slot_table().T, F32).astype(BF16)
    et_spec = pl.BlockSpec((N_EXPERTS, ROW_TILE), col)
    et_shape = jax.ShapeDtypeStruct((N_EXPERTS, N_TOK), F32)
    return pl.pallas_call(
        _out_kernel,
        grid=(N_BLK,),
        in_specs=[pl.BlockSpec((ROW_TILE, D_MODEL), row), mix_spec, mix_spec, mix_spec, mix_spec,
                  pl.BlockSpec((None, 1, 6 * D_MODEL), lambda i: (_mod_row(i), 0, 0)),
                  full((1, D_MODEL)), full((D_MODEL, D_MODEL)),
                  full((N_EXPERTS, D_MODEL)), full((N_EXPERTS, 1)),
                  full((D_MODEL, D_SHARED)), full((D_MODEL, D_SHARED)), full((D_SHARED, D_MODEL)),
                  full((MOE_SEG, 128))],
        out_specs=[pl.BlockSpec((ROW_TILE, D_MODEL), row), pl.BlockSpec((ROW_TILE, D_MODEL), row),
                   pl.BlockSpec((N_EXPERTS, None, MOE_CAP, D_MODEL), lambda i: (0, i, 0, 0)),
                   et_spec, et_spec, et_spec,
                   pl.BlockSpec((None, 8, 128), lambda i: (i, 0, 0))],
        out_shape=[jax.ShapeDtypeStruct((N_TOK, D_MODEL), F32),
                   jax.ShapeDtypeStruct((N_TOK, D_MODEL), BF16),
                   jax.ShapeDtypeStruct((N_EXPERTS, N_BLK, MOE_CAP, D_MODEL), BF16),
                   et_shape, et_shape, et_shape,
                   jax.ShapeDtypeStruct((N_BLK, 8, 128), F32)],
        compiler_params=_params("arbitrary"),
        name="out_proj",
    )(x, o_mla, o_hg, o_fn, o_swa, mod, g, wo, wrt, br, wsg, wsu, wsd, slot_t)


def _expert_kernel(x_ref, wg_ref, wu_ref, wd_ref, y_ref):
    x = x_ref[...].reshape(N_BLK * MOE_CAP, D_MODEL)
    hid = _silu(_dot(x, wg_ref[...].astype(BF16))) * _dot(x, wu_ref[...].astype(BF16))
    y = _dot(hid.astype(BF16), wd_ref[...].astype(BF16))
    y_ref[...] = y.astype(BF16).reshape(N_BLK, MOE_CAP, D_MODEL)


def _experts(xs, w_gate, w_up, w_down, l):
    rows = pl.BlockSpec((None, N_BLK, MOE_CAP, D_MODEL), lambda e: (e, 0, 0, 0))
    return pl.pallas_call(
        _expert_kernel,
        grid=(N_EXPERTS,),
        in_specs=[rows,
                  pl.BlockSpec((None, None, D_MODEL, D_EXPERT), lambda e: (l, e, 0, 0)),
                  pl.BlockSpec((None, None, D_MODEL, D_EXPERT), lambda e: (l, e, 0, 0)),
                  pl.BlockSpec((None, None, D_EXPERT, D_MODEL), lambda e: (l, e, 0, 0))],
        out_specs=rows,
        out_shape=jax.ShapeDtypeStruct(xs.shape, BF16),
        compiler_params=_params("arbitrary"),
        name="experts",
    )(xs, w_gate, w_up, w_down)


def _combine_kernel(*refs, has_overflow, final):
    refs = list(refs)
    x1_ref, y_ref, gate_ref, rank_ref, mod_ref, slot_ref = refs[:6]
    rest = refs[6:]
    rovf_ref = rest.pop(0) if has_overflow else None
    fg_ref = rest.pop(0) if final else None
    o_ref = rest.pop(0)

    gate = gate_ref[...].T
    rank_code = rank_ref[...].T
    lane = lax.broadcasted_iota(jnp.int32, gate.shape, 1)
    lhs = jnp.concatenate(
        [jnp.concatenate([rank_code, jnp.where(lane == 0, 1.0, 0.0)], axis=1),
         jnp.concatenate([gate, jnp.zeros_like(gate)], axis=1)], axis=0).astype(BF16)
    routed = jnp.zeros((ROW_TILE, D_MODEL), F32)
    seg_experts = N_EXPERTS // SEG_CHUNKS
    for c in range(SEG_CHUNKS):
        dg = _dot(lhs, slot_ref[:, c * SEG_CHUNK:(c + 1) * SEG_CHUNK])
        weights = jnp.where(dg[:ROW_TILE] == 0.0, dg[ROW_TILE:], 0.0).astype(BF16)
        y = y_ref[c * seg_experts:(c + 1) * seg_experts].reshape(SEG_CHUNK, D_MODEL)
        routed = routed + _dot(weights, y)
    if has_overflow:
        routed = routed + rovf_ref[...]
    x = x1_ref[...] + mod_ref[:, 5 * D_MODEL:6 * D_MODEL] * routed
    o_ref[...] = _rms(x, fg_ref[...]) if final else x


def _combine(x1, ys, gate_t, rank_t, mod, routed_overflow, final_g):
    has_overflow = routed_overflow is not None
    final = final_g is not None
    row = lambda i: (i, 0)
    col = lambda i: (0, i)
    xspec = pl.BlockSpec((ROW_TILE, D_MODEL), row)
    et_spec = pl.BlockSpec((N_EXPERTS, ROW_TILE), col)
    slot = jnp.asarray(_slot_table(), F32).astype(BF16)
    in_specs = [xspec, pl.BlockSpec((N_EXPERTS, None, MOE_CAP, D_MODEL), lambda i: (0, i, 0, 0)),
                et_spec, et_spec,
                pl.BlockSpec((None, 1, 6 * D_MODEL), lambda i: (_mod_row(i), 0, 0)),
                pl.BlockSpec((128, MOE_SEG), lambda i: (0, 0))]
    args = [x1, ys, gate_t, rank_t, mod, slot]
    if has_overflow:
        in_specs.append(xspec)
        args.append(routed_overflow)
    if final:
        in_specs.append(pl.BlockSpec((1, D_MODEL), lambda i: (0, 0)))
        args.append(final_g)
    return pl.pallas_call(
        functools.partial(_combine_kernel, has_overflow=has_overflow, final=final),
        grid=(N_BLK,),
        in_specs=in_specs, out_specs=xspec,
        out_shape=jax.ShapeDtypeStruct((N_TOK, D_MODEL), F32),
        compiler_params=_params("arbitrary"),
        name="combine",
    )(*args)


def _dense_moe_kernel(h_ref, gate_ref, wg_ref, wu_ref, wd_ref, o_ref):
    e = pl.program_id(1)

    @pl.when(e == 0)
    def _():
        o_ref[...] = jnp.zeros_like(o_ref)

    h = h_ref[...]
    gate = gate_ref[...].T
    lane = lax.broadcasted_iota(jnp.int32, gate.shape, 1)
    gcol = jnp.sum(jnp.where(lane == e, gate, 0.0), axis=-1, keepdims=True)
    hid = _silu(_dot(h, wg_ref[...].astype(BF16))) * _dot(h, wu_ref[...].astype(BF16)) * gcol
    o_ref[...] += _dot(hid.astype(BF16), wd_ref[...].astype(BF16))


def _dense_moe(h, gate_t, w_gate, w_up, w_down, l):
    tile = lambda i, e: (i, 0)
    return pl.pallas_call(
        _dense_moe_kernel,
        grid=(N_TOK // MOE_TOK_TILE, N_EXPERTS),
        in_specs=[pl.BlockSpec((MOE_TOK_TILE, D_MODEL), tile),
                  pl.BlockSpec((N_EXPERTS, MOE_TOK_TILE), lambda i, e: (0, i)),
                  pl.BlockSpec((None, None, D_MODEL, D_EXPERT), lambda i, e: (l, e, 0, 0)),
                  pl.BlockSpec((None, None, D_MODEL, D_EXPERT), lambda i, e: (l, e, 0, 0)),
                  pl.BlockSpec((None, None, D_EXPERT, D_MODEL), lambda i, e: (l, e, 0, 0))],
        out_specs=pl.BlockSpec((MOE_TOK_TILE, D_MODEL), tile),
        out_shape=jax.ShapeDtypeStruct((N_TOK, D_MODEL), F32),
        compiler_params=_params("arbitrary", "arbitrary"),
        name="dense_moe",
    )(h, gate_t, w_gate, w_up, w_down)


def _rope_full_tables(dim, n_rep):
    rows = DEC_SEQ // GRID_W
    r_idx, c_idx = np.meshgrid(np.arange(rows), np.arange(GRID_W), indexing='ij')
    pos = jnp.asarray(np.stack([r_idx.reshape(-1), c_idx.reshape(-1)], axis=-1), F32)
    nf = dim // 4
    inv = ROPE_BASE ** (-jnp.arange(nf, dtype=F32) / nf)
    ang = pos[:, :, None] * inv
    ang = jnp.repeat(ang.reshape(DEC_SEQ, 2 * nf), 2, axis=1)
    sign = jnp.tile(jnp.asarray([-1.0, 1.0], F32), dim // 2)
    return jnp.tile(jnp.cos(ang), (1, n_rep)), jnp.tile(jnp.sin(ang) * sign, (1, n_rep))


def _pack_w_in(w):
    c0 = MLA_Q_LORA + MLA_KV_LORA
    kr = w[:, c0:MLA_IN]
    s0 = MLA_IN + HG_IN + FN_IN
    qh = [w[:, s0 + h * SWA_HD:s0 + (h + 1) * SWA_HD] for h in SWA_STACK_ORDER]
    return jnp.concatenate([w[:, :c0], kr, kr, kr, kr, w[:, MLA_IN:s0]] + qh
                           + [w[:, s0 + SWA_QW:]], axis=1).astype(BF16)


def _pack_w_q_up(w):
    hd = MLA_NOPE + MLA_ROPE
    nope = [w[:, h * hd:h * hd + MLA_NOPE] for h in range(MLA_HEADS)]
    rope = [w[:, h * hd + MLA_NOPE:(h + 1) * hd] for h in range(MLA_HEADS)]
    return jnp.concatenate(nope + rope, axis=1).astype(BF16)


def _pack_w_kv_up(w):
    hd = MLA_NOPE + MLA_V
    kn = [w[:, h * hd:h * hd + MLA_NOPE] for h in range(MLA_HEADS)]
    vv = [w[:, h * hd + MLA_NOPE:(h + 1) * hd] for h in range(MLA_HEADS)]
    return jnp.concatenate(kn + vv, axis=1).astype(BF16)


def _pack_w_out(w):
    s0 = 3 * 256
    rows = [w[s0 + h * SWA_HD:s0 + (h + 1) * SWA_HD] for h in SWA_STACK_ORDER]
    return jnp.concatenate([w[:s0]] + rows, axis=0).astype(BF16)


def kernel(x_prompt, x_sample, c, cache_mla_ckv, cache_mla_krope, cache_swa_k, cache_swa_v, state_hgrn,
           c_ctx, w_ada, b_ada, norm1_g, norm2_g, w_in, mla_q_norm_g, mla_w_q_up, mla_kv_norm_g, mla_w_kv_up,
           hg_lb_logits, hg_norm_g, fn_w, swa_sink, w_out, moe_w_router, moe_b_router, moe_w_gate, moe_w_up,
           moe_w_down, sh_w_gate, sh_w_up, sh_w_down, final_norm_g):
    x = jnp.concatenate([x_prompt.reshape(N_CTX, D_MODEL), x_sample.reshape(N_LAT, D_MODEL)], axis=0)
    cv8 = jnp.concatenate([c_ctx[None, :], c, jnp.zeros((8 - 1 - DEC_BATCH, D_MODEL), F32)], axis=0)
    mods = _ada(cv8, w_ada, b_ada).reshape(DEPTH, 8, 1, 6 * D_MODEL)

    lb = jnp.cumsum(jax.nn.softmax(hg_lb_logits.astype(F32), axis=1), axis=1)
    lb = lb - lb[:, :1]

    cos_m, sin_m = _rope_full_tables(MLA_ROPE, MLA_HEADS)
    cos_q, sin_q = _rope_full_tables(SWA_HD, SWA_HEADS)
    cos_k, sin_k = cos_q[:, :SWA_KW], sin_q[:, :SWA_KW]
    cache_k = cache_swa_k.reshape(DEC_BATCH, DEPTH, PAST_LEN, SWA_KW)
    cache_v = cache_swa_v.reshape(DEC_BATCH, DEPTH, PAST_LEN, SWA_KW)
    state_t = jnp.swapaxes(state_hgrn, -1, -2)

    ctx_blk_lat = N_CTX // DEC_SEQ
    new_ckv, new_kr, new_k, new_v, new_st = [], [], [], [], []
    for l in range(DEPTH):
        u_mla, u_hg, u_fn, u_swa = _in_proj(x, mods[l], norm1_g[l][None], _pack_w_in(w_in[l]))

        qg, kvg = mla_q_norm_g[l][None], mla_kv_norm_g[l][None]
        wq, wkv = _pack_w_q_up(mla_w_q_up[l]), _pack_w_kv_up(mla_w_kv_up[l])
        o_mla_c, ckv_c = _mla_ctx(u_mla, qg, wq, kvg, wkv)
        o_mla_l = _mla_lat(u_mla, cache_mla_ckv, cache_mla_krope, l, cos_m, sin_m, qg, wq, kvg, wkv)

        lbf, lbb = lb[0, l][None], lb[1, l][None]
        ng4 = jnp.tile(hg_norm_g[l], HG_HEADS)[None]
        o_hg_c, st_c = _hgrn(u_hg, lbf, lbb, ng4, None, seq=SEQ, n_batch=BATCH, row_block0=0)
        o_hg_l = _hgrn(u_hg, lbf, lbb, ng4, state_t[:, l], seq=DEC_SEQ, n_batch=DEC_BATCH,
                       row_block0=ctx_blk_lat)

        fw = fn_w[l].astype(BF16)
        o_fn_c = _fourier(u_fn, fw, seq=SEQ, n_batch=BATCH, row_block0=0)
        o_fn_l = _fourier(u_fn, fw, seq=DEC_SEQ, n_batch=DEC_BATCH, row_block0=ctx_blk_lat)

        sink = swa_sink[l]
        o_swa_c = _swa_ctx(u_swa, sink)
        o_swa_l = _swa_lat(u_swa, cache_k, cache_v, l, sink, cos_q, sin_q, cos_k, sin_k)

        cat = lambda a, b: jnp.concatenate([a, b], axis=0)
        x1, h2, xs, gate_t, rank_t, ovf_t, n_ovf = _out_proj(
            x, cat(o_mla_c, o_mla_l), cat(o_hg_c, o_hg_l), cat(o_fn_c, o_fn_l), cat(o_swa_c, o_swa_l),
            mods[l], norm2_g[l][None], _pack_w_out(w_out[l]),
            moe_w_router[l].T, moe_b_router[l][:, None],
            sh_w_gate[l].astype(BF16), sh_w_up[l].astype(BF16), sh_w_down[l].astype(BF16))
        ys = _experts(xs, moe_w_gate, moe_w_up, moe_w_down, l)
        final_g = final_norm_g[None] if l == DEPTH - 1 else None

        def with_overflow(x1=x1, ys=ys, gate_t=gate_t, rank_t=rank_t, h2=h2, ovf_t=ovf_t, l=l, final_g=final_g):
            extra = _dense_moe(h2, ovf_t, moe_w_gate, moe_w_up, moe_w_down, l)
            return _combine(x1, ys, gate_t, rank_t, mods[l], extra, final_g)

        def without_overflow(x1=x1, ys=ys, gate_t=gate_t, rank_t=rank_t, l=l, final_g=final_g):
            return _combine(x1, ys, gate_t, rank_t, mods[l], None, final_g)

        x = lax.cond(jnp.sum(n_ovf[:, 0, 0]) > 0.0, with_overflow, without_overflow)

        new_ckv.append(ckv_c.reshape(BATCH, SEQ, MLA_KV_LORA))
        new_kr.append(u_mla[:N_CTX, MLA_Q_LORA + MLA_KV_LORA:MLA_IN].reshape(BATCH, SEQ, MLA_ROPE))
        new_k.append(u_swa[:N_CTX, SWA_QW:SWA_QW + SWA_KW].reshape(BATCH, SEQ, SWA_KV_HEADS, SWA_HD))
        new_v.append(u_swa[:N_CTX, SWA_QW + SWA_KW:].reshape(BATCH, SEQ, SWA_KV_HEADS, SWA_HD))
        new_st.append(jnp.swapaxes(st_c, -1, -2))

    y = x
    y_prompt = y[:N_CTX].reshape(BATCH, SEQ, D_MODEL)
    y_sample = y[N_CTX:].reshape(DEC_BATCH, DEC_SEQ, D_MODEL)
    stack = lambda xs: jnp.stack(xs, axis=1)
    return (y_prompt, y_sample, stack(new_ckv), stack(new_kr), stack(new_k), stack(new_v), stack(new_st))
```

```python
import functools

import numpy as np
import jax
import jax.numpy as jnp
from jax import lax
from jax.experimental import pallas as pl
from jax.experimental.pallas import tpu as pltpu

F32 = jnp.float32
BF16 = jnp.bfloat16

D_MODEL = 1024
BATCH = 32
SEQ = 256
DEPTH = 2
DEC_BATCH = 2
DEC_SEQ = 1024
PAST_LEN = 256
GRID_W = 64
EPS = 1e-6
ROPE_BASE = 10000.0
NEG_INF = -1e30

MLA_HEADS = 4
MLA_NOPE = 64
MLA_ROPE = 32
MLA_V = 64
MLA_Q_LORA = 256
MLA_KV_LORA = 128
HG_HEADS = 4
HG_DK = 64
HG_DV = 64
HG_W = HG_HEADS * HG_DK
FN_GROUPS = 4
FN_WIDTH = 256
SWA_HEADS = 4
SWA_KV_HEADS = 2
SWA_HD = 64
WINDOW = 128
N_EXPERTS = 64
TOP_K = 6
D_EXPERT = 256
D_SHARED = 256
ROUTE_SCALE = 2.5

MLA_IN = MLA_Q_LORA + MLA_KV_LORA + MLA_ROPE
HG_IN = 3 * HG_HEADS * HG_DK + 2 * HG_HEADS * HG_DV
FN_IN = FN_WIDTH
SWA_IN = (SWA_HEADS + 2 * SWA_KV_HEADS) * SWA_HD

N_CTX = BATCH * SEQ
N_LAT = DEC_BATCH * DEC_SEQ
N_TOK = N_CTX + N_LAT

MLA_PACK = 512
U_COLS = MLA_PACK + HG_IN + FN_IN + SWA_IN

ROW_TILE = 256
CTX_TILES = N_CTX // ROW_TILE
LAT_TILES_PER_BATCH = DEC_SEQ // ROW_TILE
HG_CHUNK = 32
MOE_TOK_TILE = 2048
SWA_QBLK = 128
MLA_QBLK = 256
VMEM_LIMIT = 56 * 1024 * 1024


def _dot(a, b):
    return jnp.dot(a, b, preferred_element_type=F32)


def _dot_nt(a, b):
    return lax.dot_general(a, b, (((1,), (1,)), ((), ())), preferred_element_type=F32)


def _dot_tn(a, b):
    return lax.dot_general(a, b, (((0,), (0,)), ((), ())), preferred_element_type=F32)


def _dot_f32(a, b):
    return jnp.dot(a, b, precision=lax.Precision.HIGHEST, preferred_element_type=F32)


def _rms(x, g):
    return x * lax.rsqrt(jnp.mean(x * x, axis=-1, keepdims=True) + EPS) * g


def _silu(x):
    return x * jax.nn.sigmoid(x)


def _mod_row(i):
    return jnp.where(i < CTX_TILES, 0, 1 + (i - CTX_TILES) // LAT_TILES_PER_BATCH)


def _params(*sem):
    return pltpu.CompilerParams(dimension_semantics=sem, vmem_limit_bytes=VMEM_LIMIT)


ADA_COLS = 1536


def _ada_kernel(cv_ref, w_ref, b_ref, o_ref):
    a = _silu(cv_ref[...]).astype(BF16)
    o_ref[...] = _dot(a, w_ref[...].astype(BF16)) + b_ref[...]


def _ada(cv8, w_ada, b_ada):
    return pl.pallas_call(
        _ada_kernel,
        grid=(DEPTH, 6 * D_MODEL // ADA_COLS),
        in_specs=[
            pl.BlockSpec((8, D_MODEL), lambda l, j: (0, 0)),
            pl.BlockSpec((None, D_MODEL, ADA_COLS), lambda l, j: (l, 0, j)),
            pl.BlockSpec((None, 1, ADA_COLS), lambda l, j: (l, 0, j)),
        ],
        out_specs=pl.BlockSpec((None, 8, ADA_COLS), lambda l, j: (l, 0, j)),
        out_shape=jax.ShapeDtypeStruct((DEPTH, 8, 6 * D_MODEL), F32),
        compiler_params=_params("arbitrary", "arbitrary"),
        name="ada",
    )(cv8, w_ada, b_ada.reshape(DEPTH, 1, 6 * D_MODEL))


def _in_kernel(x_ref, mod_ref, g_ref, w_ref, umla_ref, uhg_ref, ufn_ref, uswa_ref):
    x = x_ref[...]
    sh1 = mod_ref[:, 0:D_MODEL]
    sc1 = mod_ref[:, D_MODEL:2 * D_MODEL]
    h = _rms(x, g_ref[...]) * (1.0 + sc1) + sh1
    u = _dot(h.astype(BF16), w_ref[...])
    o = 0
    for ref, width in ((umla_ref, MLA_PACK), (uhg_ref, HG_IN), (ufn_ref, FN_IN), (uswa_ref, SWA_IN)):
        ref[...] = u[:, o:o + width].astype(ref.dtype)
        o += width


def _in_proj(x, mod, g, w):
    row = lambda i: (i, 0)
    widths = (MLA_PACK, HG_IN, FN_IN, SWA_IN)
    return pl.pallas_call(
        _in_kernel,
        grid=(N_TOK // ROW_TILE,),
        in_specs=[pl.BlockSpec((ROW_TILE, D_MODEL), row),
                  pl.BlockSpec((None, 1, 6 * D_MODEL), lambda i: (_mod_row(i), 0, 0)),
                  pl.BlockSpec((1, D_MODEL), lambda i: (0, 0)),
                  pl.BlockSpec((D_MODEL, U_COLS), lambda i: (0, 0))],
        out_specs=[pl.BlockSpec((ROW_TILE, wd), row) for wd in widths],
        out_shape=[jax.ShapeDtypeStruct((N_TOK, wd), F32) for wd in widths],
        compiler_params=_params("arbitrary"),
        name="in_proj",
    )(x, mod, g, w)


def _rope(x, cos, sin_signed):
    lane = lax.broadcasted_iota(jnp.int32, x.shape, 1)
    width = x.shape[1]
    swapped = jnp.where(lane % 2 == 0, pltpu.roll(x, width - 1, 1), pltpu.roll(x, 1, 1))
    return x * cos + swapped * sin_signed


def _stack_heads(x, n_heads, head_w):
    lane = lax.broadcasted_iota(jnp.int32, x.shape, 1)
    return jnp.concatenate([jnp.where(lane // head_w == h, x, 0.0) for h in range(n_heads)], axis=0)


def _unstack_heads(o, n_heads, head_w):
    t = o.shape[0] // n_heads
    lane = lax.broadcasted_iota(jnp.int32, (t, o.shape[1]), 1)
    out = jnp.zeros((t, o.shape[1]), F32)
    for h in range(n_heads):
        out = jnp.where(lane // head_w == h, o[h * t:(h + 1) * t], out)
    return out


MLA_SCALE = (MLA_NOPE + MLA_ROPE) ** -0.5
MLA_QW = MLA_HEADS * MLA_NOPE + MLA_HEADS * MLA_ROPE
MLA_NW = MLA_HEADS * MLA_NOPE


def _mla_attend(q, kcat, v):
    qs = jnp.concatenate([_stack_heads(q[:, :MLA_NW], MLA_HEADS, MLA_NOPE),
                          _stack_heads(q[:, MLA_NW:], MLA_HEADS, MLA_ROPE)], axis=1)
    s = _dot_nt(qs.astype(BF16), kcat) * MLA_SCALE
    p = jnp.exp(s - jnp.max(s, axis=-1, keepdims=True))
    o = _dot(p.astype(BF16), v) / jnp.sum(p, axis=-1, keepdims=True)
    return _unstack_heads(o, MLA_HEADS, MLA_V)


def _mla_ctx_kernel(u_ref, qg_ref, wq_ref, kvg_ref, wkv_ref, o_ref, ckv_ref):
    u = u_ref[...]
    q = _dot(_rms(u[:, :MLA_Q_LORA], qg_ref[...]).astype(BF16), wq_ref[...])
    ckv = _rms(u[:, MLA_Q_LORA:MLA_Q_LORA + MLA_KV_LORA], kvg_ref[...])
    ckv_ref[...] = ckv
    kv = _dot(ckv.astype(BF16), wkv_ref[...])
    kr4 = u[:, MLA_Q_LORA + MLA_KV_LORA:]
    kcat = jnp.concatenate([kv[:, :MLA_NW], kr4], axis=1).astype(BF16)
    o_ref[...] = _mla_attend(q, kcat, kv[:, MLA_NW:].astype(BF16)).astype(o_ref.dtype)


def _mla_ctx(u_mla, qg, wq, kvg, wkv):
    full = lambda shape: pl.BlockSpec(shape, lambda b: (0, 0))
    return pl.pallas_call(
        _mla_ctx_kernel,
        grid=(BATCH,),
        in_specs=[pl.BlockSpec((SEQ, MLA_PACK), lambda b: (b, 0)),
                  full((1, MLA_Q_LORA)), full((MLA_Q_LORA, MLA_QW)),
                  full((1, MLA_KV_LORA)), full((MLA_KV_LORA, 2 * MLA_NW))],
        out_specs=[pl.BlockSpec((SEQ, MLA_NW), lambda b: (b, 0)),
                   pl.BlockSpec((SEQ, MLA_KV_LORA), lambda b: (b, 0))],
        out_shape=[jax.ShapeDtypeStruct((N_CTX, MLA_NW), BF16),
                   jax.ShapeDtypeStruct((N_CTX, MLA_KV_LORA), F32)],
        compiler_params=_params("arbitrary"),
        name="mla_ctx",
    )(u_mla, qg, wq, kvg, wkv)


MLA_TK = PAST_LEN + DEC_SEQ


def _mla_lat_kernel(u_ref, cckv_ref, ckr_ref, cos_ref, sin_ref, qg_ref, wq_ref, kvg_ref, wkv_ref,
                    o_ref, kcat_s, v_s):
    i = pl.program_id(1)

    @pl.when(i == 0)
    def _():
        u = u_ref[...]
        ckv_new = _rms(u[:, MLA_Q_LORA:MLA_Q_LORA + MLA_KV_LORA], kvg_ref[...])
        ckv_all = jnp.concatenate([cckv_ref[...], ckv_new], axis=0)
        kv = _dot(ckv_all.astype(BF16), wkv_ref[...])
        kr_new = _rope(u[:, MLA_Q_LORA + MLA_KV_LORA:], cos_ref[...], sin_ref[...])
        ckr = ckr_ref[...]
        kr_all = jnp.concatenate([jnp.concatenate([ckr] * MLA_HEADS, axis=1), kr_new], axis=0)
        kcat_s[...] = jnp.concatenate([kv[:, :MLA_NW], kr_all], axis=1).astype(BF16)
        v_s[...] = kv[:, MLA_NW:].astype(BF16)

    r0 = pl.multiple_of(i * MLA_QBLK, MLA_QBLK)
    cq = u_ref[pl.ds(r0, MLA_QBLK), 0:MLA_Q_LORA]
    q = _dot(_rms(cq, qg_ref[...]).astype(BF16), wq_ref[...])
    qr = _rope(q[:, MLA_NW:], cos_ref[pl.ds(r0, MLA_QBLK), :], sin_ref[pl.ds(r0, MLA_QBLK), :])
    q = jnp.concatenate([q[:, :MLA_NW], qr], axis=1)
    o_ref[...] = _mla_attend(q, kcat_s[...], v_s[...]).astype(o_ref.dtype)


def _mla_lat(u_mla, cache_ckv, cache_kr, l, cos, sin, qg, wq, kvg, wkv):
    full = lambda shape: pl.BlockSpec(shape, lambda b, i: (0, 0))
    nq = DEC_SEQ // MLA_QBLK
    return pl.pallas_call(
        _mla_lat_kernel,
        grid=(DEC_BATCH, nq),
        in_specs=[pl.BlockSpec((DEC_SEQ, MLA_PACK), lambda b, i: (N_CTX // DEC_SEQ + b, 0)),
                  pl.BlockSpec((None, None, PAST_LEN, MLA_KV_LORA), lambda b, i: (b, l, 0, 0)),
                  pl.BlockSpec((None, None, PAST_LEN, MLA_ROPE), lambda b, i: (b, l, 0, 0)),
                  full((DEC_SEQ, MLA_HEADS * MLA_ROPE)), full((DEC_SEQ, MLA_HEADS * MLA_ROPE)),
                  full((1, MLA_Q_LORA)), full((MLA_Q_LORA, MLA_QW)),
                  full((1, MLA_KV_LORA)), full((MLA_KV_LORA, 2 * MLA_NW))],
        out_specs=pl.BlockSpec((MLA_QBLK, MLA_NW), lambda b, i: (b * nq + i, 0)),
        out_shape=jax.ShapeDtypeStruct((N_LAT, MLA_NW), BF16),
        scratch_shapes=[pltpu.VMEM((MLA_TK, MLA_QW), BF16), pltpu.VMEM((MLA_TK, MLA_NW), BF16)],
        compiler_params=_params("arbitrary", "arbitrary"),
        name="mla_lat",
    )(u_mla, cache_ckv, cache_kr, cos, sin, qg, wq, kvg, wkv)


def _hgrn_kernel(*refs, seq, has_state):
    if has_state:
        (u_ref, lbf_ref, lbb_ref, ng_ref, s0_ref, o_ref,
         q_s, kf_s, gf_s, kb_s, gb_s, of_s, ob_s, stf_s, stb_s) = refs
    else:
        (u_ref, lbf_ref, lbb_ref, ng_ref, o_ref, so_ref,
         q_s, kf_s, gf_s, kb_s, gb_s, of_s, ob_s, stf_s, stb_s) = refs
    C = HG_CHUNK
    W = HG_W
    n_chunks = seq // C

    q_s[...] = _silu(u_ref[:, 0:W])
    ff = lbf_ref[...] + (1.0 - lbf_ref[...]) * jax.nn.sigmoid(u_ref[:, W:2 * W])
    kf_s[...] = 1.0 - ff
    gf_s[...] = jnp.log(ff)
    fb = lbb_ref[...] + (1.0 - lbb_ref[...]) * jax.nn.sigmoid(u_ref[:, 2 * W:3 * W])
    kb_s[...] = 1.0 - fb
    gb_s[...] = jnp.log(fb)

    rr = lax.broadcasted_iota(jnp.int32, (W, W), 0)
    cc = lax.broadcasted_iota(jnp.int32, (W, W), 1)
    blockdiag = rr // HG_DK == cc // HG_DK
    if has_state:
        for st, d in ((stf_s, 0), (stb_s, 1)):
            rows = []
            for h in range(HG_HEADS):
                z = lambda n: jnp.zeros((HG_DV, n * HG_DK), F32)
                parts = ([z(h)] if h else []) + [s0_ref[d, h]] + ([z(HG_HEADS - 1 - h)] if h < HG_HEADS - 1 else [])
                rows.append(jnp.concatenate(parts, axis=1) if len(parts) > 1 else parts[0])
            st[...] = jnp.concatenate(rows, axis=0)
    else:
        stf_s[...] = jnp.zeros((W, W), F32)
        stb_s[...] = jnp.zeros((W, W), F32)

    ri = lax.broadcasted_iota(jnp.int32, (C, C), 0)
    ci = lax.broadcasted_iota(jnp.int32, (C, C), 1)
    lower = (ri >= ci).astype(F32)
    upper = (ci >= ri).astype(F32)
    rs = lax.broadcasted_iota(jnp.int32, (HG_HEADS * C, C), 0) % C
    cs = lax.broadcasted_iota(jnp.int32, (HG_HEADS * C, C), 1)

    def chunk(r, k_s, g_s, o_s, st_s, tri, keep, mid, last):
        q = q_s[pl.ds(r, C), :]
        k = k_s[pl.ds(r, C), :]
        v = u_ref[pl.ds(r, C), 3 * W:4 * W].astype(BF16)
        G = _dot_f32(tri, g_s[pl.ds(r, C), :])
        Gm = G[mid:mid + 1, :]
        Gl = G[last:last + 1, :]
        qe = _stack_heads(q * jnp.exp(G - Gm), HG_HEADS, HG_DK)
        ke = k * jnp.exp(Gm - G)
        A = jnp.where(keep, _dot_nt(qe.astype(BF16), ke.astype(BF16)), 0.0)
        o_intra = _unstack_heads(_dot(A.astype(BF16), v), HG_HEADS, HG_DV)
        st = st_s[...]
        o_inter = _dot_nt((q * jnp.exp(G)).astype(BF16), st.astype(BF16))
        o_s[pl.ds(r, C), :] = o_intra + o_inter
        k2 = (k * jnp.exp(Gl - G)).astype(BF16)
        st_s[...] = st * jnp.exp(Gl) + jnp.where(blockdiag, _dot_tn(v, k2), 0.0)

    def fwd(c, carry):
        r = pl.multiple_of(c * C, C)
        chunk(r, kf_s, gf_s, of_s, stf_s, lower, rs >= cs, C // 2 - 1, C - 1)
        return carry

    def bwd(c, carry):
        r = pl.multiple_of((n_chunks - 1 - c) * C, C)
        chunk(r, kb_s, gb_s, ob_s, stb_s, upper, cs >= rs, C // 2, 0)
        return carry

    lax.fori_loop(0, n_chunks, fwd, 0)
    lax.fori_loop(0, n_chunks, bwd, 0)

    o = of_s[...] + ob_s[...]
    ms = _dot_f32(o * o, jnp.where(blockdiag, 1.0 / HG_DV, 0.0))
    on = o * lax.rsqrt(ms + EPS) * ng_ref[...]
    o_ref[...] = (on * _silu(u_ref[:, 4 * W:5 * W])).astype(o_ref.dtype)

    if not has_state:
        for st, d in ((stf_s, 0), (stb_s, 1)):
            for h in range(HG_HEADS):
                so_ref[d, h] = st[h * HG_DV:(h + 1) * HG_DV, h * HG_DK:(h + 1) * HG_DK]


def _hgrn(u_hg, lbf, lbb, ng4, state_t, *, seq, n_batch, row_block0):
    has_state = state_t is not None
    full = lambda shape: pl.BlockSpec(shape, lambda b: (0, 0))
    in_specs = [pl.BlockSpec((seq, HG_IN), lambda b: (row_block0 + b, 0)),
                full((1, HG_W)), full((1, HG_W)), full((1, HG_W))]
    args = [u_hg, lbf, lbb, ng4]
    st_spec = pl.BlockSpec((None, 2, HG_HEADS, HG_DV, HG_DK), lambda b: (b, 0, 0, 0, 0))
    o_spec = pl.BlockSpec((seq, HG_W), lambda b: (b, 0))
    o_shape = jax.ShapeDtypeStruct((n_batch * seq, HG_W), BF16)
    if has_state:
        in_specs.append(st_spec)
        args.append(state_t)
        out_specs, out_shape = o_spec, o_shape
    else:
        out_specs = [o_spec, st_spec]
        out_shape = [o_shape, jax.ShapeDtypeStruct((n_batch, 2, HG_HEADS, HG_DV, HG_DK), F32)]
    return pl.pallas_call(
        functools.partial(_hgrn_kernel, seq=seq, has_state=has_state),
        grid=(n_batch,),
        in_specs=in_specs, out_specs=out_specs, out_shape=out_shape,
        scratch_shapes=[pltpu.VMEM((seq, HG_W), F32)] * 7 + [pltpu.VMEM((HG_W, HG_W), F32)] * 2,
        compiler_params=_params("arbitrary"),
        name="hgrn_lat" if has_state else "hgrn_ctx",
    )(*args)


def _dft_tables(n):
    j = np.arange(n, dtype=np.int64)
    ang = 2.0 * np.pi * ((j[:, None] * j[None, :]) % n).astype(np.float64) / n
    return np.cos(ang) / np.sqrt(n), np.sin(ang) / np.sqrt(n)


def _fourier_tables(seq):
    gw = FN_WIDTH // FN_GROUPS
    cg, sg = _dft_tables(gw)
    eye = np.eye(FN_GROUPS)
    chan = np.concatenate([np.kron(eye, cg), np.kron(eye, sg)], axis=1)
    ct, st = _dft_tables(seq)
    pos = np.concatenate([ct, -st], axis=1)
    return jnp.asarray(chan, F32).astype(BF16), jnp.asarray(pos, F32).astype(BF16)


def _fourier_kernel(x_ref, chan_ref, pos_ref, w_ref, o_ref):
    x12 = _dot(x_ref[...].astype(BF16), chan_ref[...])
    z = jnp.concatenate([x12[:, :FN_WIDTH], x12[:, FN_WIDTH:]], axis=0).astype(BF16)
    y = _dot(pos_ref[...], z)
    o_ref[...] = _dot(y.astype(BF16), w_ref[...]).astype(o_ref.dtype)


def _fourier(u_fn, w, *, seq, n_batch, row_block0):
    chan, pos = _fourier_tables(seq)
    full = lambda shape: pl.BlockSpec(shape, lambda b: (0, 0))
    return pl.pallas_call(
        _fourier_kernel,
        grid=(n_batch,),
        in_specs=[pl.BlockSpec((seq, FN_WIDTH), lambda b: (row_block0 + b, 0)),
                  full((FN_WIDTH, 2 * FN_WIDTH)), full((seq, 2 * seq)), full((FN_WIDTH, FN_WIDTH))],
        out_specs=pl.BlockSpec((seq, FN_WIDTH), lambda b: (b, 0)),
        out_shape=jax.ShapeDtypeStruct((n_batch * seq, FN_WIDTH), BF16),
        compiler_params=_params("arbitrary"),
        name="fourier",
    )(u_fn, chan, pos, w)


SWA_SCALE = SWA_HD ** -0.5
SWA_QW = SWA_HEADS * SWA_HD
SWA_KW = SWA_KV_HEADS * SWA_HD
SWA_STACK_ORDER = (0, 2, 1, 3)


def _swa_stack_q(q):
    return jnp.concatenate([_stack_heads(q[:, :SWA_KW], SWA_KV_HEADS, SWA_HD),
                            _stack_heads(q[:, SWA_KW:], SWA_KV_HEADS, SWA_HD)], axis=0)


def _swa_unstack_o(o):
    t = o.shape[0] // SWA_HEADS
    return jnp.concatenate([_unstack_heads(o[:2 * t], SWA_KV_HEADS, SWA_HD),
                            _unstack_heads(o[2 * t:], SWA_KV_HEADS, SWA_HD)], axis=1)


def _sink_rows(sink_ref, t):
    return jnp.concatenate([jnp.full((t, 1), sink_ref[h], F32) for h in SWA_STACK_ORDER], axis=0)


def _swa_ctx_kernel(sink_ref, u_ref, o_ref):
    u = u_ref[...]
    qs = _swa_stack_q(u[:, :SWA_QW]).astype(BF16)
    k = u[:, SWA_QW:SWA_QW + SWA_KW].astype(BF16)
    v = u[:, SWA_QW + SWA_KW:].astype(BF16)
    s = _dot_nt(qs, k) * SWA_SCALE
    sink = _sink_rows(sink_ref, SEQ)
    m = jnp.maximum(jnp.max(s, axis=-1, keepdims=True), sink)
    p = jnp.exp(s - m)
    denom = jnp.sum(p, axis=-1, keepdims=True) + jnp.exp(sink - m)
    o_ref[...] = _swa_unstack_o(_dot(p.astype(BF16), v) / denom).astype(o_ref.dtype)


def _swa_ctx(u_swa, sink):
    return pl.pallas_call(
        _swa_ctx_kernel,
        grid=(BATCH,),
        in_specs=[pl.BlockSpec(memory_space=pltpu.SMEM),
                  pl.BlockSpec((SEQ, SWA_IN), lambda b: (b, 0))],
        out_specs=pl.BlockSpec((SEQ, SWA_QW), lambda b: (b, 0)),
        out_shape=jax.ShapeDtypeStruct((N_CTX, SWA_QW), BF16),
        compiler_params=_params("arbitrary"),
        name="swa_ctx",
    )(sink, u_swa)


SWA_PAD = DEC_SEQ + 2 * SWA_QBLK


def _swa_lat_kernel(sink_ref, u_ref, kc_ref, vc_ref, cosq_ref, sinq_ref, cosk_ref, sin_k_ref,
                    o_ref, k_s, v_s):
    i = pl.program_id(1)
    B = SWA_QBLK

    @pl.when(i == 0)
    def _():
        zeros = jnp.zeros((B, SWA_KW), BF16)
        k = _rope(u_ref[:, SWA_QW:SWA_QW + SWA_KW], cosk_ref[...], sin_k_ref[...]).astype(BF16)
        k_s[...] = jnp.concatenate([zeros, k, zeros], axis=0)
        v_s[...] = jnp.concatenate([zeros, u_ref[:, SWA_QW + SWA_KW:].astype(BF16), zeros], axis=0)

    r0 = pl.multiple_of(i * B, B)
    q = _rope(u_ref[pl.ds(r0, B), 0:SWA_QW], cosq_ref[pl.ds(r0, B), :], sinq_ref[pl.ds(r0, B), :])
    qs = _swa_stack_q(q).astype(BF16)
    s_loc = _dot_nt(qs, k_s[pl.ds(r0, 3 * B), :]) * SWA_SCALE
    row = lax.broadcasted_iota(jnp.int32, s_loc.shape, 0) % B
    col = lax.broadcasted_iota(jnp.int32, s_loc.shape, 1)
    kpos = r0 - B + col
    valid = (jnp.abs(row + B - col) <= WINDOW) & (kpos >= 0) & (kpos < DEC_SEQ)
    s_loc = jnp.where(valid, s_loc, NEG_INF)
    s_ctx = _dot_nt(qs, kc_ref[...].astype(BF16)) * SWA_SCALE
    sink = _sink_rows(sink_ref, B)
    m = jnp.maximum(jnp.maximum(jnp.max(s_loc, axis=-1, keepdims=True),
                                jnp.max(s_ctx, axis=-1, keepdims=True)), sink)
    p_loc = jnp.exp(s_loc - m)
    p_ctx = jnp.exp(s_ctx - m)
    denom = (jnp.sum(p_loc, axis=-1, keepdims=True) + jnp.sum(p_ctx, axis=-1, keepdims=True)
             + jnp.exp(sink - m))
    o = _dot(p_loc.astype(BF16), v_s[pl.ds(r0, 3 * B), :]) + _dot(p_ctx.astype(BF16), vc_ref[...].astype(BF16))
    o_ref[...] = _swa_unstack_o(o / denom).astype(o_ref.dtype)


def _swa_lat(u_swa, cache_k, cache_v, l, sink, cosq, sinq, cosk, sink_k):
    full = lambda shape: pl.BlockSpec(shape, lambda b, i: (0, 0))
    nq = DEC_SEQ // SWA_QBLK
    cache_spec = pl.BlockSpec((None, None, PAST_LEN, SWA_KW), lambda b, i: (b, l, 0, 0))
    return pl.pallas_call(
        _swa_lat_kernel,
        grid=(DEC_BATCH, nq),
        in_specs=[pl.BlockSpec(memory_space=pltpu.SMEM),
                  pl.BlockSpec((DEC_SEQ, SWA_IN), lambda b, i: (N_CTX // DEC_SEQ + b, 0)),
                  cache_spec, cache_spec,
                  full((DEC_SEQ, SWA_QW)), full((DEC_SEQ, SWA_QW)),
                  full((DEC_SEQ, SWA_KW)), full((DEC_SEQ, SWA_KW))],
        out_specs=pl.BlockSpec((SWA_QBLK, SWA_QW), lambda b, i: (b * nq + i, 0)),
        out_shape=jax.ShapeDtypeStruct((N_LAT, SWA_QW), BF16),
        scratch_shapes=[pltpu.VMEM((SWA_PAD, SWA_KW), BF16), pltpu.VMEM((SWA_PAD, SWA_KW), BF16)],
        compiler_params=_params("arbitrary", "arbitrary"),
        name="swa_lat",
    )(sink, u_swa, cache_k, cache_v, cosq, sinq, cosk, sink_k)


N_BLK = N_TOK // ROW_TILE
SEG_ALIGN = 16
LOCAL_ROWS = 2560
LOCAL_CHUNK = 512
EXPERT_TILE = 256
SORTED_ROWS = 115200
MAX_TILES = SORTED_ROWS // EXPERT_TILE
NOT_PICKED = -1.0
NO_DEST = 4095.0


def _out_kernel(x_ref, omla_ref, ohg_ref, ofn_ref, oswa_ref, mod_ref, g_ref, wo_ref,
                wrt_ref, br_ref, wsg_ref, wsu_ref, wsd_ref,
                x1_ref, h_ref, gate_ref, rank_ref, cnt_ref):
    mix = jnp.zeros((ROW_TILE, D_MODEL), F32)
    for j, ref in enumerate((omla_ref, ohg_ref, ofn_ref, oswa_ref)):
        mix = mix + _dot(ref[...], wo_ref[j * 256:(j + 1) * 256, :])
    g1 = mod_ref[:, 2 * D_MODEL:3 * D_MODEL]
    sh2 = mod_ref[:, 3 * D_MODEL:4 * D_MODEL]
    sc2 = mod_ref[:, 4 * D_MODEL:5 * D_MODEL]
    g2 = mod_ref[:, 5 * D_MODEL:6 * D_MODEL]
    x1 = x_ref[...] + g1 * mix
    h = _rms(x1, g_ref[...]) * (1.0 + sc2) + sh2
    hb = h.astype(BF16)
    h_ref[...] = hb

    logits = lax.dot_general(wrt_ref[...], h, (((1,), (1,)), ((), ())),
                             precision=lax.Precision.HIGHEST, preferred_element_type=F32)
    scores = jax.nn.sigmoid(logits)
    sel = scores + br_ref[...]
    eidx = lax.broadcasted_iota(jnp.int32, sel.shape, 0)
    gate = jnp.zeros_like(scores)
    picked = jnp.zeros_like(scores)
    for _ in range(TOP_K):
        best = jnp.max(sel, axis=0, keepdims=True)
        first = jnp.min(jnp.where(sel == best, eidx, N_EXPERTS), axis=0, keepdims=True)
        pick = eidx == first
        gate = jnp.where(pick, scores, gate)
        picked = jnp.where(pick, 1.0, picked)
        sel = jnp.where(pick, -jnp.inf, sel)
    gate = ROUTE_SCALE * gate / jnp.sum(gate, axis=0, keepdims=True)

    ti = lax.broadcasted_iota(jnp.int32, (ROW_TILE, ROW_TILE), 0)
    tj = lax.broadcasted_iota(jnp.int32, (ROW_TILE, ROW_TILE), 1)
    pb = picked.astype(BF16)
    rank = _dot(pb, jnp.where(ti < tj, 1.0, 0.0).astype(BF16))
    gate_ref[...] = gate
    rank_ref[...] = jnp.where(picked > 0.0, rank, NOT_PICKED)
    counts = _dot_nt(jnp.ones((8, ROW_TILE), BF16), pb)
    cnt_ref[...] = jnp.concatenate([counts, jnp.zeros_like(counts)], axis=1)

    hid =_silu(_dot(hb, wsg_ref[...])) * _dot(hb, wsu_ref[...])
    x1_ref[...] = x1 + g2 * _dot(hid.astype(BF16), wsd_ref[...])


def _out_proj(x, o_mla, o_hg, o_fn, o_swa, mod, g, wo, wrt, br, wsg, wsu, wsd):
    row = lambda i: (i, 0)
    col = lambda i: (0, i)
    full = lambda shape: pl.BlockSpec(shape, lambda i: (0, 0))
    mix_spec = pl.BlockSpec((ROW_TILE, 256), row)
    et_spec = pl.BlockSpec((N_EXPERTS, ROW_TILE), col)
    et_shape = jax.ShapeDtypeStruct((N_EXPERTS, N_TOK), F32)
    return pl.pallas_call(
        _out_kernel,
        grid=(N_BLK,),
        in_specs=[pl.BlockSpec((ROW_TILE, D_MODEL), row), mix_spec, mix_spec, mix_spec, mix_spec,
                  pl.BlockSpec((None, 1, 6 * D_MODEL), lambda i: (_mod_row(i), 0, 0)),
                  full((1, D_MODEL)), full((D_MODEL, D_MODEL)),
                  full((N_EXPERTS, D_MODEL)), full((N_EXPERTS, 1)),
                  full((D_MODEL, D_SHARED)), full((D_MODEL, D_SHARED)), full((D_SHARED, D_MODEL))],
        out_specs=[pl.BlockSpec((ROW_TILE, D_MODEL), row), pl.BlockSpec((ROW_TILE, D_MODEL), row),
                   et_spec, et_spec,
                   pl.BlockSpec((None, 8, 128), lambda i: (i, 0, 0))],
        out_shape=[jax.ShapeDtypeStruct((N_TOK, D_MODEL), F32),
                   jax.ShapeDtypeStruct((N_TOK, D_MODEL), BF16),
                   et_shape, et_shape,
                   jax.ShapeDtypeStruct((N_BLK, 8, 128), F32)],
        compiler_params=_params("arbitrary"),
        name="out_proj",
    )(x, o_mla, o_hg, o_fn, o_swa, mod, g, wo, wrt, br, wsg, wsu, wsd)


def _segment_plan(cnt):
    cnt = cnt[:, 0, :N_EXPERTS].astype(jnp.int32)
    seg = (cnt + (SEG_ALIGN - 1)) // SEG_ALIGN * SEG_ALIGN
    local = jnp.cumsum(seg, axis=1) - seg
    total = jnp.sum(seg, axis=1)
    per_expert = jnp.sum(seg, axis=0)
    padded = (per_expert + (EXPERT_TILE - 1)) // EXPERT_TILE * EXPERT_TILE
    ends = jnp.cumsum(padded)
    start = ends - padded
    dest = start[None, :] + jnp.cumsum(seg, axis=0) - seg
    n_tiles = ends[-1] // EXPERT_TILE
    tile_expert = jnp.searchsorted(ends // EXPERT_TILE, jnp.arange(MAX_TILES, dtype=jnp.int32), side='right')
    tile_expert = jnp.minimum(tile_expert, tile_expert[jnp.maximum(n_tiles - 1, 0)]).astype(jnp.int32)
    plan = dict(seg=seg, local=local, total=total.astype(jnp.int32), dest=dest.astype(jnp.int32),
                tail_start=(start + per_expert).astype(jnp.int32), tail_rows=(padded - per_expert).astype(jnp.int32),
                n_tiles=n_tiles.reshape(1).astype(jnp.int32), tile_expert=tile_expert)
    segf, localf = seg.astype(F32), local.astype(F32)
    pad_lanes = lambda a: jnp.concatenate([a, jnp.zeros_like(a)], axis=1)[:, None, :]
    plan.update(seg_row=pad_lanes(segf), local_row=pad_lanes(localf),
                seg_col=segf[:, :, None], local_col=localf[:, :, None])
    return plan


def _local_dest_digits(rank, local_start):
    dest = jnp.where(rank >= 0.0, local_start + rank, NO_DEST)
    hi = jnp.floor(dest * (1.0 / 64.0))
    return hi, dest - 64.0 * hi


def _dispatch_kernel(dest_ref, seg_ref, local_ref, total_ref, tail_start_ref, tail_rows_ref, nt_ref,
                     h_ref, rank_ref, local_col_ref, local_row_ref, seg_row_ref,
                     xs_hbm, buf, zeros, sems, zsem, usem):
    b = pl.program_id(0)
    slot = b % 2

    def wait_block(blk, s):
        n = pl.multiple_of(total_ref[blk], SEG_ALIGN)
        pltpu.make_async_copy(buf.at[s, pl.ds(0, n)], xs_hbm.at[pl.ds(0, n)], sems.at[s]).wait()

    @pl.when(b == 0)
    def _():
        zeros[...] = jnp.zeros_like(zeros)

    def unused_tiles(action):
        for k in range(-(-MAX_TILES // N_BLK)):
            t = nt_ref[0] + b + k * N_BLK

            @pl.when(t < MAX_TILES)
            def _():
                cp = pltpu.make_async_copy(zeros, xs_hbm.at[pl.ds(pl.multiple_of(t * EXPERT_TILE, EXPERT_TILE),
                                                                  EXPERT_TILE)], usem)
                cp.start() if action == "start" else cp.wait()

    unused_tiles("start")

    @pl.when(b >= 2)
    def _():
        wait_block(b - 2, slot)

    hi, lo = _local_dest_digits(rank_ref[...], local_col_ref[...])
    code = jnp.concatenate([hi, lo], axis=0).astype(BF16)
    hb = h_ref[...]
    lstart = local_row_ref[:, :N_EXPERTS]
    lend = lstart + seg_row_ref[:, :N_EXPERTS]
    for c in range(LOCAL_ROWS // LOCAL_CHUNK):
        r = (lax.broadcasted_iota(jnp.int32, (LOCAL_CHUNK, N_EXPERTS), 0) + c * LOCAL_CHUNK).astype(F32)
        member = (r >= lstart) & (r < lend)
        table = jnp.concatenate([jnp.where(member, 64.0, 0.0), jnp.where(member, 1.0, 0.0)], axis=1).astype(BF16)
        d = _dot(table, code)
        rr = (lax.broadcasted_iota(jnp.int32, (LOCAL_CHUNK, ROW_TILE), 0) + c * LOCAL_CHUNK).astype(F32)
        onehot = jnp.where(d == rr, 1.0, 0.0).astype(BF16)
        buf[slot, c * LOCAL_CHUNK:(c + 1) * LOCAL_CHUNK, :] = _dot(onehot, hb).astype(BF16)

    for e in range(N_EXPERTS):
        n = pl.multiple_of(seg_ref[b, e], SEG_ALIGN)

        @pl.when(n > 0)
        def _():
            src = pl.multiple_of(local_ref[b, e], SEG_ALIGN)
            dst = pl.multiple_of(dest_ref[b, e], SEG_ALIGN)
            pltpu.make_async_copy(buf.at[slot, pl.ds(src, n)], xs_hbm.at[pl.ds(dst, n)], sems.at[slot]).start()

    unused_tiles("wait")

    @pl.when(b == N_BLK - 1)
    def _():
        for e in range(N_EXPERTS):
            n = pl.multiple_of(tail_rows_ref[e], SEG_ALIGN)

            @pl.when(n > 0)
            def _():
                dst = pl.multiple_of(tail_start_ref[e], SEG_ALIGN)
                pltpu.make_async_copy(zeros.at[pl.ds(0, n)], xs_hbm.at[pl.ds(dst, n)], zsem).start()
        for e in range(N_EXPERTS):
            n = pl.multiple_of(tail_rows_ref[e], SEG_ALIGN)

            @pl.when(n > 0)
            def _():
                pltpu.make_async_copy(zeros.at[pl.ds(0, n)], xs_hbm.at[pl.ds(0, n)], zsem).wait()
        wait_block(b - 1, 1 - slot)
        wait_block(b, slot)


def _dispatch(h, rank_t, plan):
    blk = lambda shape, imap: pl.BlockSpec(shape, imap)
    return pl.pallas_call(
        _dispatch_kernel,
        grid_spec=pltpu.PrefetchScalarGridSpec(
            num_scalar_prefetch=7, grid=(N_BLK,),
            in_specs=[blk((ROW_TILE, D_MODEL), lambda b, *_: (b, 0)),
                      blk((N_EXPERTS, ROW_TILE), lambda b, *_: (0, b)),
                      blk((None, N_EXPERTS, 1), lambda b, *_: (b, 0, 0)),
                      blk((None, 1, 128), lambda b, *_: (b, 0, 0)),
                      blk((None, 1, 128), lambda b, *_: (b, 0, 0))],
            out_specs=pl.BlockSpec(memory_space=pl.ANY),
            scratch_shapes=[pltpu.VMEM((2, LOCAL_ROWS, D_MODEL), BF16),
                            pltpu.VMEM((EXPERT_TILE, D_MODEL), BF16),
                            pltpu.SemaphoreType.DMA((2,)), pltpu.SemaphoreType.DMA(()),
                            pltpu.SemaphoreType.DMA(())]),
        out_shape=jax.ShapeDtypeStruct((SORTED_ROWS, D_MODEL), BF16),
        compiler_params=_params("arbitrary"),
        name="dispatch",
    )(plan['dest'], plan['seg'], plan['local'], plan['total'], plan['tail_start'], plan['tail_rows'],
      plan['n_tiles'], h, rank_t, plan['local_col'], plan['local_row'], plan['seg_row'])


def _expert_kernel(te_ref, nt_ref, x_ref, wg_ref, wu_ref, wd_ref, y_ref):
    active = pl.program_id(0) < nt_ref[0]

    @pl.when(active)
    def _():
        x = x_ref[...]
        hid = _silu(_dot(x, wg_ref[...].astype(BF16))) * _dot(x, wu_ref[...].astype(BF16))
        y_ref[...] = _dot(hid.astype(BF16), wd_ref[...].astype(BF16)).astype(BF16)

    @pl.when(jnp.logical_not(active))
    def _():
        y_ref[...] = jnp.zeros_like(y_ref)


def _experts(xs, plan, w_gate, w_up, w_down, l):
    rows_in = pl.BlockSpec((EXPERT_TILE, D_MODEL), lambda i, te, nt: (jnp.minimum(i, nt[0] - 1), 0))
    rows_out = pl.BlockSpec((EXPERT_TILE, D_MODEL), lambda i, te, nt: (i, 0))
    wspec = lambda shape: pl.BlockSpec((None, None) + shape, lambda i, te, nt: (l, te[i], 0, 0))
    return pl.pallas_call(
        _expert_kernel,
        grid_spec=pltpu.PrefetchScalarGridSpec(
            num_scalar_prefetch=2, grid=(MAX_TILES,),
            in_specs=[rows_in, wspec((D_MODEL, D_EXPERT)), wspec((D_MODEL, D_EXPERT)), wspec((D_EXPERT, D_MODEL))],
            out_specs=rows_out),
        out_shape=jax.ShapeDtypeStruct((SORTED_ROWS, D_MODEL), BF16),
        compiler_params=_params("arbitrary"),
        name="experts",
    )(plan['tile_expert'], plan['n_tiles'], xs, w_gate, w_up, w_down)


def _combine_kernel(*refs, final):
    refs = list(refs)
    dest_ref, seg_ref, local_ref, total_ref = refs[:4]
    x1_ref, ys_hbm, gate_ref, rank_ref, mod_ref, local_row_ref, local_col_ref, seg_col_ref = refs[4:12]
    rest = refs[12:]
    fg_ref = rest.pop(0) if final else None
    o_ref, buf, sems = rest
    b = pl.program_id(0)
    slot = b % 2

    def fetch(blk, s):
        for e in range(N_EXPERTS):
            n = pl.multiple_of(seg_ref[blk, e], SEG_ALIGN)

            @pl.when(n > 0)
            def _():
                src = pl.multiple_of(dest_ref[blk, e], SEG_ALIGN)
                dst = pl.multiple_of(local_ref[blk, e], SEG_ALIGN)
                pltpu.make_async_copy(ys_hbm.at[pl.ds(src, n)], buf.at[s, pl.ds(dst, n)], sems.at[s]).start()

    @pl.when(b == 0)
    def _():
        buf[...] = jnp.zeros_like(buf)
        fetch(0, 0)

    @pl.when(b + 1 < N_BLK)
    def _():
        fetch(b + 1, 1 - slot)

    n_rows = pl.multiple_of(total_ref[b], SEG_ALIGN)
    pltpu.make_async_copy(ys_hbm.at[pl.ds(0, n_rows)], buf.at[slot, pl.ds(0, n_rows)], sems.at[slot]).wait()

    gate = gate_ref[...].T
    hi, lo = _local_dest_digits(rank_ref[...].T, local_row_ref[:, :N_EXPERTS])
    code = jnp.concatenate([hi, lo], axis=1).astype(BF16)
    gb = gate.astype(BF16)
    lstart = local_col_ref[...]
    lend = lstart + seg_col_ref[...]
    routed = jnp.zeros((ROW_TILE, D_MODEL), F32)
    for c in range(LOCAL_ROWS // LOCAL_CHUNK):
        r = (lax.broadcasted_iota(jnp.int32, (N_EXPERTS, LOCAL_CHUNK), 1) + c * LOCAL_CHUNK).astype(F32)
        member = (r >= lstart) & (r < lend)
        ones = jnp.where(member, 1.0, 0.0)
        table = jnp.concatenate([ones * 64.0, ones], axis=0).astype(BF16)
        d = _dot(code, table)
        gx = _dot(gb, ones.astype(BF16))
        rr = (lax.broadcasted_iota(jnp.int32, (ROW_TILE, LOCAL_CHUNK), 1) + c * LOCAL_CHUNK).astype(F32)
        weights = jnp.where(d == rr, gx, 0.0).astype(BF16)
        routed = routed + _dot(weights, buf[slot, c * LOCAL_CHUNK:(c + 1) * LOCAL_CHUNK, :])
    x = x1_ref[...] + mod_ref[:, 5 * D_MODEL:6 * D_MODEL] * routed
    o_ref[...] = _rms(x, fg_ref[...]) if final else x


def _combine(x1, ys, gate_t, rank_t, mod, plan, final_g):
    final = final_g is not None
    blk = lambda shape, imap: pl.BlockSpec(shape, imap)
    xspec = blk((ROW_TILE, D_MODEL), lambda b, *_: (b, 0))
    et_spec = blk((N_EXPERTS, ROW_TILE), lambda b, *_: (0, b))
    in_specs = [xspec, pl.BlockSpec(memory_space=pl.ANY), et_spec, et_spec,
                blk((None, 1, 6 * D_MODEL), lambda b, *_: (_mod_row(b), 0, 0)),
                blk((None, 1, 128), lambda b, *_: (b, 0, 0)),
                blk((None, N_EXPERTS, 1), lambda b, *_: (b, 0, 0)),
                blk((None, N_EXPERTS, 1), lambda b, *_: (b, 0, 0))]
    args = [x1, ys, gate_t, rank_t, mod, plan['local_row'], plan['local_col'], plan['seg_col']]
    if final:
        in_specs.append(blk((1, D_MODEL), lambda b, *_: (0, 0)))
        args.append(final_g)
    return pl.pallas_call(
        functools.partial(_combine_kernel, final=final),
        grid_spec=pltpu.PrefetchScalarGridSpec(
            num_scalar_prefetch=4, grid=(N_BLK,),
            in_specs=in_specs, out_specs=xspec,
            scratch_shapes=[pltpu.VMEM((2, LOCAL_ROWS, D_MODEL), BF16), pltpu.SemaphoreType.DMA((2,))]),
        out_shape=jax.ShapeDtypeStruct((N_TOK, D_MODEL), F32),
        compiler_params=_params("arbitrary"),
        name="combine",
    )(plan['dest'], plan['seg'], plan['local'], plan['total'], *args)


def _rope_full_tables(dim, n_rep):
    rows = DEC_SEQ // GRID_W
    r_idx, c_idx = np.meshgrid(np.arange(rows), np.arange(GRID_W), indexing='ij')
    pos = jnp.asarray(np.stack([r_idx.reshape(-1), c_idx.reshape(-1)], axis=-1), F32)
    nf = dim // 4
    inv = ROPE_BASE ** (-jnp.arange(nf, dtype=F32) / nf)
    ang = pos[:, :, None] * inv
    ang = jnp.repeat(ang.reshape(DEC_SEQ, 2 * nf), 2, axis=1)
    sign = jnp.tile(jnp.asarray([-1.0, 1.0], F32), dim // 2)
    return jnp.tile(jnp.cos(ang), (1, n_rep)), jnp.tile(jnp.sin(ang) * sign, (1, n_rep))


def _pack_w_in(w):
    c0 = MLA_Q_LORA + MLA_KV_LORA
    kr = w[:, c0:MLA_IN]
    s0 = MLA_IN + HG_IN + FN_IN
    qh = [w[:, s0 + h * SWA_HD:s0 + (h + 1) * SWA_HD] for h in SWA_STACK_ORDER]
    return jnp.concatenate([w[:, :c0], kr, kr, kr, kr, w[:, MLA_IN:s0]] + qh
                           + [w[:, s0 + SWA_QW:]], axis=1).astype(BF16)


def _pack_w_q_up(w):
    hd = MLA_NOPE + MLA_ROPE
    nope = [w[:, h * hd:h * hd + MLA_NOPE] for h in range(MLA_HEADS)]
    rope = [w[:, h * hd + MLA_NOPE:(h + 1) * hd] for h in range(MLA_HEADS)]
    return jnp.concatenate(nope + rope, axis=1).astype(BF16)


def _pack_w_kv_up(w):
    hd = MLA_NOPE + MLA_V
    kn = [w[:, h * hd:h * hd + MLA_NOPE] for h in range(MLA_HEADS)]
    vv = [w[:, h * hd + MLA_NOPE:(h + 1) * hd] for h in range(MLA_HEADS)]
    return jnp.concatenate(kn + vv, axis=1).astype(BF16)


def _pack_w_out(w):
    s0 = 3 * 256
    rows = [w[s0 + h * SWA_HD:s0 + (h + 1) * SWA_HD] for h in SWA_STACK_ORDER]
    return jnp.concatenate([w[:s0]] + rows, axis=0).astype(BF16)


def kernel(x_prompt, x_sample, c, cache_mla_ckv, cache_mla_krope, cache_swa_k, cache_swa_v, state_hgrn,
           c_ctx, w_ada, b_ada, norm1_g, norm2_g, w_in, mla_q_norm_g, mla_w_q_up, mla_kv_norm_g, mla_w_kv_up,
           hg_lb_logits, hg_norm_g, fn_w, swa_sink, w_out, moe_w_router, moe_b_router, moe_w_gate, moe_w_up,
           moe_w_down, sh_w_gate, sh_w_up, sh_w_down, final_norm_g):
    x = jnp.concatenate([x_prompt.reshape(N_CTX, D_MODEL), x_sample.reshape(N_LAT, D_MODEL)], axis=0)
    cv8 = jnp.concatenate([c_ctx[None, :], c, jnp.zeros((8 - 1 - DEC_BATCH, D_MODEL), F32)], axis=0)
    mods = _ada(cv8, w_ada, b_ada).reshape(DEPTH, 8, 1, 6 * D_MODEL)

    lb = jnp.cumsum(jax.nn.softmax(hg_lb_logits.astype(F32), axis=1), axis=1)
    lb = lb - lb[:, :1]

    cos_m, sin_m = _rope_full_tables(MLA_ROPE, MLA_HEADS)
    cos_q, sin_q = _rope_full_tables(SWA_HD, SWA_HEADS)
    cos_k, sin_k = cos_q[:, :SWA_KW], sin_q[:, :SWA_KW]
    cache_k = cache_swa_k.reshape(DEC_BATCH, DEPTH, PAST_LEN, SWA_KW)
    cache_v = cache_swa_v.reshape(DEC_BATCH, DEPTH, PAST_LEN, SWA_KW)
    state_t = jnp.swapaxes(state_hgrn, -1, -2)

    ctx_blk_lat = N_CTX // DEC_SEQ
    new_ckv, new_kr, new_k, new_v, new_st = [], [], [], [], []
    for l in range(DEPTH):
        u_mla, u_hg, u_fn, u_swa = _in_proj(x, mods[l], norm1_g[l][None], _pack_w_in(w_in[l]))

        qg, kvg = mla_q_norm_g[l][None], mla_kv_norm_g[l][None]
        wq, wkv = _pack_w_q_up(mla_w_q_up[l]), _pack_w_kv_up(mla_w_kv_up[l])
        o_mla_c, ckv_c = _mla_ctx(u_mla, qg, wq, kvg, wkv)
        o_mla_l = _mla_lat(u_mla, cache_mla_ckv, cache_mla_krope, l, cos_m, sin_m, qg, wq, kvg, wkv)

        lbf, lbb = lb[0, l][None], lb[1, l][None]
        ng4 = jnp.tile(hg_norm_g[l], HG_HEADS)[None]
        o_hg_c, st_c = _hgrn(u_hg, lbf, lbb, ng4, None, seq=SEQ, n_batch=BATCH, row_block0=0)
        o_hg_l = _hgrn(u_hg, lbf, lbb, ng4, state_t[:, l], seq=DEC_SEQ, n_batch=DEC_BATCH,
                       row_block0=ctx_blk_lat)

        fw = fn_w[l].astype(BF16)
        o_fn_c = _fourier(u_fn, fw, seq=SEQ, n_batch=BATCH, row_block0=0)
        o_fn_l = _fourier(u_fn, fw, seq=DEC_SEQ, n_batch=DEC_BATCH, row_block0=ctx_blk_lat)

        sink = swa_sink[l]
        o_swa_c = _swa_ctx(u_swa, sink)
        o_swa_l = _swa_lat(u_swa, cache_k, cache_v, l, sink, cos_q, sin_q, cos_k, sin_k)

        cat = lambda a, b: jnp.concatenate([a, b], axis=0)
        x1, h2, gate_t, rank_t, cnt = _out_proj(
            x, cat(o_mla_c, o_mla_l), cat(o_hg_c, o_hg_l), cat(o_fn_c, o_fn_l), cat(o_swa_c, o_swa_l),
            mods[l], norm2_g[l][None], _pack_w_out(w_out[l]),
            moe_w_router[l].T, moe_b_router[l][:, None],
            sh_w_gate[l].astype(BF16), sh_w_up[l].astype(BF16), sh_w_down[l].astype(BF16))
        plan = _segment_plan(cnt)
        xs = _dispatch(h2, rank_t, plan)
        ys = _experts(xs, plan, moe_w_gate, moe_w_up, moe_w_down, l)
        x = _combine(x1, ys, gate_t, rank_t, mods[l], plan, final_norm_g[None] if l == DEPTH - 1 else None)

        new_ckv.append(ckv_c.reshape(BATCH, SEQ, MLA_KV_LORA))
        new_kr.append(u_mla[:N_CTX, MLA_Q_LORA + MLA_KV_LORA:MLA_IN].reshape(BATCH, SEQ, MLA_ROPE))
        new_k.append(u_swa[:N_CTX, SWA_QW:SWA_QW + SWA_KW].reshape(BATCH, SEQ, SWA_KV_HEADS, SWA_HD))
        new_v.append(u_swa[:N_CTX, SWA_QW + SWA_KW:].reshape(BATCH, SEQ, SWA_KV_HEADS, SWA_HD))
        new_st.append(jnp.swapaxes(st_c, -1, -2))

    y = x
    y_prompt = y[:N_CTX].reshape(BATCH, SEQ, D_MODEL)
    y_sample = y[N_CTX:].reshape(DEC_BATCH, DEC_SEQ, D_MODEL)
    stack = lambda xs: jnp.stack(xs, axis=1)
    return (y_prompt, y_sample, stack(new_ckv), stack(new_kr), stack(new_k), stack(new_v), stack(new_st))
```

```python
import functools

import numpy as np
import jax
import jax.numpy as jnp
from jax import lax
from jax.experimental import pallas as pl
from jax.experimental.pallas import tpu as pltpu

F32 = jnp.float32
BF16 = jnp.bfloat16

D_MODEL = 1024
BATCH = 32
SEQ = 256
DEPTH = 2
DEC_BATCH = 2
DEC_SEQ = 1024
PAST_LEN = 256
GRID_W = 64
EPS = 1e-6
ROPE_BASE = 10000.0
NEG_INF = -1e30

MLA_HEADS = 4
MLA_NOPE = 64
MLA_ROPE = 32
MLA_V = 64
MLA_Q_LORA = 256
MLA_KV_LORA = 128
HG_HEADS = 4
HG_DK = 64
HG_DV = 64
HG_W = HG_HEADS * HG_DK
FN_GROUPS = 4
FN_WIDTH = 256
SWA_HEADS = 4
SWA_KV_HEADS = 2
SWA_HD = 64
WINDOW = 128
N_EXPERTS = 64
TOP_K = 6
D_EXPERT = 256
D_SHARED = 256
ROUTE_SCALE = 2.5

MLA_IN = MLA_Q_LORA + MLA_KV_LORA + MLA_ROPE
HG_IN = 3 * HG_HEADS * HG_DK + 2 * HG_HEADS * HG_DV
FN_IN = FN_WIDTH
SWA_IN = (SWA_HEADS + 2 * SWA_KV_HEADS) * SWA_HD

N_CTX = BATCH * SEQ
N_LAT = DEC_BATCH * DEC_SEQ
N_TOK = N_CTX + N_LAT

MLA_PACK = 512
U_COLS = MLA_PACK + HG_IN + FN_IN + SWA_IN

ROW_TILE = 256
CTX_TILES = N_CTX // ROW_TILE
LAT_TILES_PER_BATCH = DEC_SEQ // ROW_TILE
HG_CHUNK = 32
HG_BLOCK = 256
SWA_QBLK = 128
MLA_QBLK = 256
VMEM_LIMIT = 56 * 1024 * 1024


def _dot(a, b):
    return jnp.dot(a, b, preferred_element_type=F32)


def _dot_nt(a, b):
    return lax.dot_general(a, b, (((1,), (1,)), ((), ())), preferred_element_type=F32)


def _dot_tn(a, b):
    return lax.dot_general(a, b, (((0,), (0,)), ((), ())), preferred_element_type=F32)


def _split3(x):
    hi = x.astype(BF16)
    r1 = x - hi.astype(F32)
    mid = r1.astype(BF16)
    return hi, mid, (r1 - mid.astype(F32)).astype(BF16)


def _dot_exact_lhs(a, b):
    ab = a.astype(BF16)
    hi, mid, lo = _split3(b)
    return (_dot(ab, lo) + _dot(ab, mid)) + _dot(ab, hi)


def _dot_exact_rhs(a, b):
    bb = b.astype(BF16)
    hi, mid, lo = _split3(a)
    return (_dot(lo, bb) + _dot(mid, bb)) + _dot(hi, bb)


def _rms(x, g):
    return x * lax.rsqrt(jnp.mean(x * x, axis=-1, keepdims=True) + EPS) * g


def _silu(x):
    return x * jax.nn.sigmoid(x)


def _mod_row(i):
    return jnp.where(i < CTX_TILES, 0, 1 + (i - CTX_TILES) // LAT_TILES_PER_BATCH)


def _params(*sem):
    return pltpu.CompilerParams(dimension_semantics=sem, vmem_limit_bytes=VMEM_LIMIT)


ADA_COLS = 1536


def _ada_kernel(cv_ref, w_ref, b_ref, o_ref):
    a = _silu(cv_ref[...]).astype(BF16)
    o_ref[...] = _dot(a, w_ref[...].astype(BF16)) + b_ref[...]


def _ada(cv8, w_ada, b_ada):
    return pl.pallas_call(
        _ada_kernel,
        grid=(DEPTH, 6 * D_MODEL // ADA_COLS),
        in_specs=[
            pl.BlockSpec((8, D_MODEL), lambda l, j: (0, 0)),
            pl.BlockSpec((None, D_MODEL, ADA_COLS), lambda l, j: (l, 0, j)),
            pl.BlockSpec((None, 1, ADA_COLS), lambda l, j: (l, 0, j)),
        ],
        out_specs=pl.BlockSpec((None, 8, ADA_COLS), lambda l, j: (l, 0, j)),
        out_shape=jax.ShapeDtypeStruct((DEPTH, 8, 6 * D_MODEL), F32),
        compiler_params=_params("arbitrary", "arbitrary"),
        name="ada",
    )(cv8, w_ada, b_ada.reshape(DEPTH, 1, 6 * D_MODEL))


def _ctx_lat_specs(width):
    return [pl.BlockSpec((ROW_TILE, width), lambda i, *_: (jnp.minimum(i, CTX_TILES - 1), 0)),
            pl.BlockSpec((ROW_TILE, width), lambda i, *_: (jnp.maximum(i - CTX_TILES, 0), 0))]


def _row_specs(parts, width):
    return _ctx_lat_specs(width) if len(parts) == 2 else [pl.BlockSpec((ROW_TILE, width), lambda i, *_: (i, 0))]


def _read_rows(refs):
    if len(refs) == 1:
        return refs[0][...]
    return jnp.where(pl.program_id(0) < CTX_TILES, refs[0][...], refs[1][...])


def _in_kernel(*refs, n_x):
    x = _read_rows(refs[:n_x])
    mod_ref, g_ref, w_ref, umla_ref, uhg_ref, ufn_ref, uswa_ref = refs[n_x:]
    sh1 = mod_ref[:, 0:D_MODEL]
    sc1 = mod_ref[:, D_MODEL:2 * D_MODEL]
    h = _rms(x, g_ref[...]) * (1.0 + sc1) + sh1
    u = _dot(h.astype(BF16), w_ref[...])
    o = 0
    for ref, width in ((umla_ref, MLA_PACK), (uhg_ref, HG_IN), (ufn_ref, FN_IN), (uswa_ref, SWA_IN)):
        ref[...] = u[:, o:o + width].astype(ref.dtype)
        o += width


def _in_proj(x_parts, mod, g, w):
    row = lambda i: (i, 0)
    widths = (MLA_PACK, HG_IN, FN_IN, SWA_IN)
    return pl.pallas_call(
        functools.partial(_in_kernel, n_x=len(x_parts)),
        grid=(N_TOK // ROW_TILE,),
        in_specs=_row_specs(x_parts, D_MODEL) + [
            pl.BlockSpec((None, 1, 6 * D_MODEL), lambda i: (_mod_row(i), 0, 0)),
            pl.BlockSpec((1, D_MODEL), lambda i: (0, 0)),
            pl.BlockSpec((D_MODEL, U_COLS), lambda i: (0, 0))],
        out_specs=[pl.BlockSpec((ROW_TILE, wd), row) for wd in widths],
        out_shape=[jax.ShapeDtypeStruct((N_TOK, wd), F32) for wd in widths],
        compiler_params=_params("arbitrary"),
        name="in_proj",
    )(*x_parts, mod, g, w)


def _rope(x, cos, sin_signed):
    lane = lax.broadcasted_iota(jnp.int32, x.shape, 1)
    width = x.shape[1]
    swapped = jnp.where(lane % 2 == 0, pltpu.roll(x, width - 1, 1), pltpu.roll(x, 1, 1))
    return x * cos + swapped * sin_signed


def _stack_heads(x, n_heads, head_w):
    lane = lax.broadcasted_iota(jnp.int32, x.shape, 1)
    return jnp.concatenate([jnp.where(lane // head_w == h, x, 0.0) for h in range(n_heads)], axis=0)


def _unstack_heads(o, n_heads, head_w):
    t = o.shape[0] // n_heads
    lane = lax.broadcasted_iota(jnp.int32, (t, o.shape[1]), 1)
    out = jnp.zeros((t, o.shape[1]), F32)
    for h in range(n_heads):
        out = jnp.where(lane // head_w == h, o[h * t:(h + 1) * t], out)
    return out


MLA_SCALE = (MLA_NOPE + MLA_ROPE) ** -0.5
MLA_QW = MLA_HEADS * MLA_NOPE + MLA_HEADS * MLA_ROPE
MLA_NW = MLA_HEADS * MLA_NOPE


def _mla_attend(q, kcat, v):
    qs = jnp.concatenate([_stack_heads(q[:, :MLA_NW], MLA_HEADS, MLA_NOPE),
                          _stack_heads(q[:, MLA_NW:], MLA_HEADS, MLA_ROPE)], axis=1)
    s = _dot_nt(qs.astype(BF16), kcat) * MLA_SCALE
    p = jnp.exp(s - jnp.max(s, axis=-1, keepdims=True))
    o = _dot(p.astype(BF16), v) / jnp.sum(p, axis=-1, keepdims=True)
    return _unstack_heads(o, MLA_HEADS, MLA_V)


def _mla_ctx_kernel(u_ref, qg_ref, wq_ref, kvg_ref, wkv_ref, o_ref, ckv_ref):
    u = u_ref[...]
    q = _dot(_rms(u[:, :MLA_Q_LORA], qg_ref[...]).astype(BF16), wq_ref[...])
    ckv = _rms(u[:, MLA_Q_LORA:MLA_Q_LORA + MLA_KV_LORA], kvg_ref[...])
    ckv_ref[...] = ckv
    kv = _dot(ckv.astype(BF16), wkv_ref[...])
    kr4 = u[:, MLA_Q_LORA + MLA_KV_LORA:]
    kcat = jnp.concatenate([kv[:, :MLA_NW], kr4], axis=1).astype(BF16)
    o_ref[...] = _mla_attend(q, kcat, kv[:, MLA_NW:].astype(BF16)).astype(o_ref.dtype)


def _mla_ctx(u_mla, qg, wq, kvg, wkv):
    full = lambda shape: pl.BlockSpec(shape, lambda b: (0, 0))
    return pl.pallas_call(
        _mla_ctx_kernel,
        grid=(BATCH,),
        in_specs=[pl.BlockSpec((SEQ, MLA_PACK), lambda b: (b, 0)),
                  full((1, MLA_Q_LORA)), full((MLA_Q_LORA, MLA_QW)),
                  full((1, MLA_KV_LORA)), full((MLA_KV_LORA, 2 * MLA_NW))],
        out_specs=[pl.BlockSpec((SEQ, MLA_NW), lambda b: (b, 0)),
                   pl.BlockSpec((SEQ, MLA_KV_LORA), lambda b: (b, 0))],
        out_shape=[jax.ShapeDtypeStruct((N_CTX, MLA_NW), BF16),
                   jax.ShapeDtypeStruct((N_CTX, MLA_KV_LORA), F32)],
        compiler_params=_params("arbitrary"),
        name="mla_ctx",
    )(u_mla, qg, wq, kvg, wkv)


MLA_TK = PAST_LEN + DEC_SEQ


def _mla_lat_kernel(u_ref, cckv_ref, ckr_ref, cos_ref, sin_ref, qg_ref, wq_ref, kvg_ref, wkv_ref,
                    o_ref, kcat_s, v_s):
    i = pl.program_id(1)

    @pl.when(i == 0)
    def _():
        u = u_ref[...]
        ckv_new = _rms(u[:, MLA_Q_LORA:MLA_Q_LORA + MLA_KV_LORA], kvg_ref[...])
        ckv_all = jnp.concatenate([cckv_ref[...], ckv_new], axis=0)
        kv = _dot(ckv_all.astype(BF16), wkv_ref[...])
        kr_new = _rope(u[:, MLA_Q_LORA + MLA_KV_LORA:], cos_ref[...], sin_ref[...])
        ckr = ckr_ref[...]
        kr_all = jnp.concatenate([jnp.concatenate([ckr] * MLA_HEADS, axis=1), kr_new], axis=0)
        kcat_s[...] = jnp.concatenate([kv[:, :MLA_NW], kr_all], axis=1).astype(BF16)
        v_s[...] = kv[:, MLA_NW:].astype(BF16)

    r0 = pl.multiple_of(i * MLA_QBLK, MLA_QBLK)
    cq = u_ref[pl.ds(r0, MLA_QBLK), 0:MLA_Q_LORA]
    q = _dot(_rms(cq, qg_ref[...]).astype(BF16), wq_ref[...])
    qr = _rope(q[:, MLA_NW:], cos_ref[pl.ds(r0, MLA_QBLK), :], sin_ref[pl.ds(r0, MLA_QBLK), :])
    q = jnp.concatenate([q[:, :MLA_NW], qr], axis=1)
    o_ref[...] = _mla_attend(q, kcat_s[...], v_s[...]).astype(o_ref.dtype)


def _mla_lat(u_mla, cache_ckv, cache_kr, l, cos, sin, qg, wq, kvg, wkv):
    full = lambda shape: pl.BlockSpec(shape, lambda b, i: (0, 0))
    nq = DEC_SEQ // MLA_QBLK
    return pl.pallas_call(
        _mla_lat_kernel,
        grid=(DEC_BATCH, nq),
        in_specs=[pl.BlockSpec((DEC_SEQ, MLA_PACK), lambda b, i: (N_CTX // DEC_SEQ + b, 0)),
                  pl.BlockSpec((None, None, PAST_LEN, MLA_KV_LORA), lambda b, i: (b, l, 0, 0)),
                  pl.BlockSpec((None, None, PAST_LEN, MLA_ROPE), lambda b, i: (b, l, 0, 0)),
                  full((DEC_SEQ, MLA_HEADS * MLA_ROPE)), full((DEC_SEQ, MLA_HEADS * MLA_ROPE)),
                  full((1, MLA_Q_LORA)), full((MLA_Q_LORA, MLA_QW)),
                  full((1, MLA_KV_LORA)), full((MLA_KV_LORA, 2 * MLA_NW))],
        out_specs=pl.BlockSpec((MLA_QBLK, MLA_NW), lambda b, i: (b * nq + i, 0)),
        out_shape=jax.ShapeDtypeStruct((N_LAT, MLA_NW), BF16),
        scratch_shapes=[pltpu.VMEM((MLA_TK, MLA_QW), BF16), pltpu.VMEM((MLA_TK, MLA_NW), BF16)],
        compiler_params=_params("arbitrary", "arbitrary"),
        name="mla_lat",
    )(u_mla, cache_ckv, cache_kr, cos, sin, qg, wq, kvg, wkv)


def _hgrn_kernel(*refs, seq, has_state):
    if has_state:
        (u_ref, lbf_ref, lbb_ref, ng_ref, s0_ref, o_ref,
         q_s, kf_s, gf_s, kb_s, gb_s, of_s, ob_s, stf_s, stb_s) = refs
    else:
        (u_ref, lbf_ref, lbb_ref, ng_ref, o_ref, so_ref,
         q_s, kf_s, gf_s, kb_s, gb_s, of_s, ob_s, stf_s, stb_s) = refs
    C = HG_CHUNK
    W = HG_W

    q_s[...] = _silu(u_ref[:, 0:W])
    ff = lbf_ref[...] + (1.0 - lbf_ref[...]) * jax.nn.sigmoid(u_ref[:, W:2 * W])
    kf_s[...] = 1.0 - ff
    gf_s[...] = jnp.log(ff)
    fb = lbb_ref[...] + (1.0 - lbb_ref[...]) * jax.nn.sigmoid(u_ref[:, 2 * W:3 * W])
    kb_s[...] = 1.0 - fb
    gb_s[...] = jnp.log(fb)

    rr = lax.broadcasted_iota(jnp.int32, (W, W), 0)
    cc = lax.broadcasted_iota(jnp.int32, (W, W), 1)
    blockdiag = rr // HG_DK == cc // HG_DK
    if has_state:
        for st, d in ((stf_s, 0), (stb_s, 1)):
            rows = []
            for h in range(HG_HEADS):
                z = lambda n: jnp.zeros((HG_DV, n * HG_DK), F32)
                parts = ([z(h)] if h else []) + [s0_ref[d, h]] + ([z(HG_HEADS - 1 - h)] if h < HG_HEADS - 1 else [])
                rows.append(jnp.concatenate(parts, axis=1) if len(parts) > 1 else parts[0])
            st[...] = jnp.concatenate(rows, axis=0)
    else:
        stf_s[...] = jnp.zeros((W, W), F32)
        stb_s[...] = jnp.zeros((W, W), F32)

    B = HG_BLOCK
    per_block = B // C
    n_blocks = seq // B
    ri = lax.broadcasted_iota(jnp.int32, (B, B), 0)
    ci = lax.broadcasted_iota(jnp.int32, (B, B), 1)
    same_chunk = ri // C == ci // C
    rs = lax.broadcasted_iota(jnp.int32, (HG_HEADS * B, B), 0) % B
    cs = lax.broadcasted_iota(jnp.int32, (HG_HEADS * B, B), 1)
    same_chunk_s = rs // C == cs // C

    def sums(to_end, to_mid):
        f = lambda m: jnp.where(same_chunk & m, 1.0, 0.0)
        whole = jnp.where(same_chunk, 1.0, 0.0)
        return jnp.concatenate([f(to_end), f(to_end) - f(to_mid), whole - f(to_end)], axis=0)

    mid_f = (ri // C) * C + (C // 2 - 1)
    mid_b = (ri // C) * C + C // 2
    sums_f = sums(ci <= ri, ci <= mid_f)
    sums_b = sums(ci >= ri, ci >= mid_b)
    keep_f = same_chunk_s & (rs >= cs)
    keep_b = same_chunk_s & (cs >= rs)

    def block(r, k_s, g_s, o_s, st_s, sum_mat, keep, order):
        q = q_s[pl.ds(r, B), :]
        k = k_s[pl.ds(r, B), :]
        v = u_ref[pl.ds(r, B), 3 * W:4 * W].astype(BF16)
        gs = _dot_exact_lhs(sum_mat, g_s[pl.ds(r, B), :])
        G, Gq, Gk2 = gs[:B], gs[B:2 * B], gs[2 * B:]
        qe = _stack_heads(q * jnp.exp(Gq), HG_HEADS, HG_DK)
        ke = k * jnp.exp(-Gq)
        A = jnp.where(keep, _dot_nt(qe.astype(BF16), ke.astype(BF16)), 0.0)
        o_intra = _unstack_heads(_dot(A.astype(BF16), v), HG_HEADS, HG_DV)
        qg = (q * jnp.exp(G)).astype(BF16)
        k2 = (k * jnp.exp(Gk2)).astype(BF16)
        decay = jnp.exp(G + Gk2)
        st = st_s[...]
        o_inter = [None] * per_block
        for c in order:
            rows = slice(c * C, (c + 1) * C)
            o_inter[c] = _dot_nt(qg[rows], st.astype(BF16))
            st = st * decay[c * C:c * C + 1] + jnp.where(blockdiag, _dot_tn(v[rows], k2[rows]), 0.0)
        st_s[...] = st
        o_s[pl.ds(r, B), :] = o_intra + jnp.concatenate(o_inter, axis=0)

    def fwd(i, carry):
        block(pl.multiple_of(i * B, B), kf_s, gf_s, of_s, stf_s, sums_f, keep_f, range(per_block))
        return carry

    def bwd(i, carry):
        block(pl.multiple_of((n_blocks - 1 - i) * B, B), kb_s, gb_s, ob_s, stb_s, sums_b, keep_b,
              range(per_block - 1, -1, -1))
        return carry

    lax.fori_loop(0, n_blocks, fwd, 0)
    lax.fori_loop(0, n_blocks, bwd, 0)

    o = of_s[...] + ob_s[...]
    ms = _dot_exact_rhs(o * o, jnp.where(blockdiag, 1.0 / HG_DV, 0.0))
    on = o * lax.rsqrt(ms + EPS) * ng_ref[...]
    o_ref[...] = (on * _silu(u_ref[:, 4 * W:5 * W])).astype(o_ref.dtype)

    if not has_state:
        for st, d in ((stf_s, 0), (stb_s, 1)):
            for h in range(HG_HEADS):
                so_ref[d, h] = st[h * HG_DV:(h + 1) * HG_DV, h * HG_DK:(h + 1) * HG_DK]


def _hgrn(u_hg, lbf, lbb, ng4, state_t, *, seq, n_batch, row_block0):
    has_state = state_t is not None
    full = lambda shape: pl.BlockSpec(shape, lambda b: (0, 0))
    in_specs = [pl.BlockSpec((seq, HG_IN), lambda b: (row_block0 + b, 0)),
                full((1, HG_W)), full((1, HG_W)), full((1, HG_W))]
    args = [u_hg, lbf, lbb, ng4]
    st_spec = pl.BlockSpec((None, 2, HG_HEADS, HG_DV, HG_DK), lambda b: (b, 0, 0, 0, 0))
    o_spec = pl.BlockSpec((seq, HG_W), lambda b: (b, 0))
    o_shape = jax.ShapeDtypeStruct((n_batch * seq, HG_W), BF16)
    if has_state:
        in_specs.append(st_spec)
        args.append(state_t)
        out_specs, out_shape = o_spec, o_shape
    else:
        out_specs = [o_spec, st_spec]
        out_shape = [o_shape, jax.ShapeDtypeStruct((n_batch, 2, HG_HEADS, HG_DV, HG_DK), F32)]
    return pl.pallas_call(
        functools.partial(_hgrn_kernel, seq=seq, has_state=has_state),
        grid=(n_batch,),
        in_specs=in_specs, out_specs=out_specs, out_shape=out_shape,
        scratch_shapes=[pltpu.VMEM((seq, HG_W), F32)] * 7 + [pltpu.VMEM((HG_W, HG_W), F32)] * 2,
        compiler_params=_params("arbitrary"),
        name="hgrn_lat" if has_state else "hgrn_ctx",
    )(*args)


def _dft_tables(n):
    j = np.arange(n, dtype=np.int64)
    ang = 2.0 * np.pi * ((j[:, None] * j[None, :]) % n).astype(np.float64) / n
    return np.cos(ang) / np.sqrt(n), np.sin(ang) / np.sqrt(n)


def _fourier_tables(seq):
    gw = FN_WIDTH // FN_GROUPS
    cg, sg = _dft_tables(gw)
    eye = np.eye(FN_GROUPS)
    chan = np.concatenate([np.kron(eye, cg), np.kron(eye, sg)], axis=1)
    ct, st = _dft_tables(seq)
    pos = np.concatenate([ct, -st], axis=1)
    return jnp.asarray(chan, F32).astype(BF16), jnp.asarray(pos, F32).astype(BF16)


def _fourier_kernel(x_ref, chan_ref, pos_ref, w_ref, o_ref):
    x12 = _dot(x_ref[...].astype(BF16), chan_ref[...])
    z = jnp.concatenate([x12[:, :FN_WIDTH], x12[:, FN_WIDTH:]], axis=0).astype(BF16)
    y = _dot(pos_ref[...], z)
    o_ref[...] = _dot(y.astype(BF16), w_ref[...]).astype(o_ref.dtype)


def _fourier(u_fn, w, *, seq, n_batch, row_block0):
    chan, pos = _fourier_tables(seq)
    full = lambda shape: pl.BlockSpec(shape, lambda b: (0, 0))
    return pl.pallas_call(
        _fourier_kernel,
        grid=(n_batch,),
        in_specs=[pl.BlockSpec((seq, FN_WIDTH), lambda b: (row_block0 + b, 0)),
                  full((FN_WIDTH, 2 * FN_WIDTH)), full((seq, 2 * seq)), full((FN_WIDTH, FN_WIDTH))],
        out_specs=pl.BlockSpec((seq, FN_WIDTH), lambda b: (b, 0)),
        out_shape=jax.ShapeDtypeStruct((n_batch * seq, FN_WIDTH), BF16),
        compiler_params=_params("arbitrary"),
        name="fourier",
    )(u_fn, chan, pos, w)


SWA_SCALE = SWA_HD ** -0.5
SWA_QW = SWA_HEADS * SWA_HD
SWA_KW = SWA_KV_HEADS * SWA_HD
SWA_STACK_ORDER = (0, 2, 1, 3)


def _swa_stack_q(q):
    return jnp.concatenate([_stack_heads(q[:, :SWA_KW], SWA_KV_HEADS, SWA_HD),
                            _stack_heads(q[:, SWA_KW:], SWA_KV_HEADS, SWA_HD)], axis=0)


def _swa_unstack_o(o):
    t = o.shape[0] // SWA_HEADS
    return jnp.concatenate([_unstack_heads(o[:2 * t], SWA_KV_HEADS, SWA_HD),
                            _unstack_heads(o[2 * t:], SWA_KV_HEADS, SWA_HD)], axis=1)


def _sink_rows(sink_ref, t):
    return jnp.concatenate([jnp.full((t, 1), sink_ref[h], F32) for h in SWA_STACK_ORDER], axis=0)


def _swa_ctx_kernel(sink_ref, u_ref, o_ref):
    u = u_ref[...]
    qs = _swa_stack_q(u[:, :SWA_QW]).astype(BF16)
    k = u[:, SWA_QW:SWA_QW + SWA_KW].astype(BF16)
    v = u[:, SWA_QW + SWA_KW:].astype(BF16)
    s = _dot_nt(qs, k) * SWA_SCALE
    sink = _sink_rows(sink_ref, SEQ)
    m = jnp.maximum(jnp.max(s, axis=-1, keepdims=True), sink)
    p = jnp.exp(s - m)
    denom = jnp.sum(p, axis=-1, keepdims=True) + jnp.exp(sink - m)
    o_ref[...] = _swa_unstack_o(_dot(p.astype(BF16), v) / denom).astype(o_ref.dtype)


def _swa_ctx(u_swa, sink):
    return pl.pallas_call(
        _swa_ctx_kernel,
        grid=(BATCH,),
        in_specs=[pl.BlockSpec(memory_space=pltpu.SMEM),
                  pl.BlockSpec((SEQ, SWA_IN), lambda b: (b, 0))],
        out_specs=pl.BlockSpec((SEQ, SWA_QW), lambda b: (b, 0)),
        out_shape=jax.ShapeDtypeStruct((N_CTX, SWA_QW), BF16),
        compiler_params=_params("arbitrary"),
        name="swa_ctx",
    )(sink, u_swa)


SWA_PAD = DEC_SEQ + 2 * SWA_QBLK


def _swa_lat_kernel(sink_ref, u_ref, kc_ref, vc_ref, cosq_ref, sinq_ref, cosk_ref, sin_k_ref,
                    o_ref, k_s, v_s):
    i = pl.program_id(1)
    B = SWA_QBLK

    @pl.when(i == 0)
    def _():
        zeros = jnp.zeros((B, SWA_KW), BF16)
        k = _rope(u_ref[:, SWA_QW:SWA_QW + SWA_KW], cosk_ref[...], sin_k_ref[...]).astype(BF16)
        k_s[...] = jnp.concatenate([zeros, k, zeros], axis=0)
        v_s[...] = jnp.concatenate([zeros, u_ref[:, SWA_QW + SWA_KW:].astype(BF16), zeros], axis=0)

    r0 = pl.multiple_of(i * B, B)
    q = _rope(u_ref[pl.ds(r0, B), 0:SWA_QW], cosq_ref[pl.ds(r0, B), :], sinq_ref[pl.ds(r0, B), :])
    qs = _swa_stack_q(q).astype(BF16)
    s_loc = _dot_nt(qs, k_s[pl.ds(r0, 3 * B), :]) * SWA_SCALE
    row = lax.broadcasted_iota(jnp.int32, s_loc.shape, 0) % B
    col = lax.broadcasted_iota(jnp.int32, s_loc.shape, 1)
    kpos = r0 - B + col
    valid = (jnp.abs(row + B - col) <= WINDOW) & (kpos >= 0) & (kpos < DEC_SEQ)
    s_loc = jnp.where(valid, s_loc, NEG_INF)
    s_ctx = _dot_nt(qs, kc_ref[...].astype(BF16)) * SWA_SCALE
    sink = _sink_rows(sink_ref, B)
    m = jnp.maximum(jnp.maximum(jnp.max(s_loc, axis=-1, keepdims=True),
                                jnp.max(s_ctx, axis=-1, keepdims=True)), sink)
    p_loc = jnp.exp(s_loc - m)
    p_ctx = jnp.exp(s_ctx - m)
    denom = (jnp.sum(p_loc, axis=-1, keepdims=True) + jnp.sum(p_ctx, axis=-1, keepdims=True)
             + jnp.exp(sink - m))
    o = _dot(p_loc.astype(BF16), v_s[pl.ds(r0, 3 * B), :]) + _dot(p_ctx.astype(BF16), vc_ref[...].astype(BF16))
    o_ref[...] = _swa_unstack_o(o / denom).astype(o_ref.dtype)


def _swa_lat(u_swa, cache_k, cache_v, l, sink, cosq, sinq, cosk, sink_k):
    full = lambda shape: pl.BlockSpec(shape, lambda b, i: (0, 0))
    nq = DEC_SEQ // SWA_QBLK
    cache_spec = pl.BlockSpec((None, None, PAST_LEN, SWA_KW), lambda b, i: (b, l, 0, 0))
    return pl.pallas_call(
        _swa_lat_kernel,
        grid=(DEC_BATCH, nq),
        in_specs=[pl.BlockSpec(memory_space=pltpu.SMEM),
                  pl.BlockSpec((DEC_SEQ, SWA_IN), lambda b, i: (N_CTX // DEC_SEQ + b, 0)),
                  cache_spec, cache_spec,
                  full((DEC_SEQ, SWA_QW)), full((DEC_SEQ, SWA_QW)),
                  full((DEC_SEQ, SWA_KW)), full((DEC_SEQ, SWA_KW))],
        out_specs=pl.BlockSpec((SWA_QBLK, SWA_QW), lambda b, i: (b * nq + i, 0)),
        out_shape=jax.ShapeDtypeStruct((N_LAT, SWA_QW), BF16),
        scratch_shapes=[pltpu.VMEM((SWA_PAD, SWA_KW), BF16), pltpu.VMEM((SWA_PAD, SWA_KW), BF16)],
        compiler_params=_params("arbitrary", "arbitrary"),
        name="swa_lat",
    )(sink, u_swa, cache_k, cache_v, cosq, sinq, cosk, sink_k)


N_BLK = N_TOK // ROW_TILE
SEG_ALIGN = 16
LOCAL_ROWS = 2560
LOCAL_CHUNK = 512
EXPERT_TILE = 256
SORTED_ROWS = 115200
MAX_TILES = SORTED_ROWS // EXPERT_TILE
NOT_PICKED = -1.0
NO_DEST = 4095.0


def _out_kernel(*refs, n_x):
    x = _read_rows(refs[:n_x])
    mixers = refs[n_x:n_x + 8]
    (mod_ref, g_ref, wo_ref, wrt_ref, br_ref, wsg_ref, wsu_ref, wsd_ref,
     x1_ref, h_ref, gate_ref, rank_ref, cnt_ref) = refs[n_x + 8:]
    mix = jnp.zeros((ROW_TILE, D_MODEL), F32)
    for j in range(4):
        mix = mix + _dot(_read_rows(mixers[2 * j:2 * j + 2]), wo_ref[j * 256:(j + 1) * 256, :])
    g1 = mod_ref[:, 2 * D_MODEL:3 * D_MODEL]
    sh2 = mod_ref[:, 3 * D_MODEL:4 * D_MODEL]
    sc2 = mod_ref[:, 4 * D_MODEL:5 * D_MODEL]
    g2 = mod_ref[:, 5 * D_MODEL:6 * D_MODEL]
    x1 = x + g1 * mix
    h = _rms(x1, g_ref[...]) * (1.0 + sc2) + sh2
    hb = h.astype(BF16)
    h_ref[...] = hb

    logits = lax.dot_general(wrt_ref[...], h, (((1,), (1,)), ((), ())),
                             precision=lax.Precision.HIGHEST, preferred_element_type=F32)
    scores = jax.nn.sigmoid(logits)
    sel = scores + br_ref[...]
    eidx = lax.broadcasted_iota(jnp.int32, sel.shape, 0)
    gate = jnp.zeros_like(scores)
    picked = jnp.zeros_like(scores)
    for _ in range(TOP_K):
        best = jnp.max(sel, axis=0, keepdims=True)
        first = jnp.min(jnp.where(sel == best, eidx, N_EXPERTS), axis=0, keepdims=True)
        pick = eidx == first
        gate = jnp.where(pick, scores, gate)
        picked = jnp.where(pick, 1.0, picked)
        sel = jnp.where(pick, -jnp.inf, sel)
    gate = ROUTE_SCALE * gate / jnp.sum(gate, axis=0, keepdims=True)

    ti = lax.broadcasted_iota(jnp.int32, (ROW_TILE, ROW_TILE), 0)
    tj = lax.broadcasted_iota(jnp.int32, (ROW_TILE, ROW_TILE), 1)
    pb = picked.astype(BF16)
    rank = _dot(pb, jnp.where(ti < tj, 1.0, 0.0).astype(BF16))
    gate_ref[...] = gate
    rank_ref[...] = jnp.where(picked > 0.0, rank, NOT_PICKED)
    counts = _dot_nt(jnp.ones((8, ROW_TILE), BF16), pb)
    cnt_ref[...] = jnp.concatenate([counts, jnp.zeros_like(counts)], axis=1)

    hid =_silu(_dot(hb, wsg_ref[...])) * _dot(hb, wsu_ref[...])
    x1_ref[...] = x1 + g2 * _dot(hid.astype(BF16), wsd_ref[...])


def _out_proj(x_parts, mixer_pairs, mod, g, wo, wrt, br, wsg, wsu, wsd):
    row = lambda i: (i, 0)
    col = lambda i: (0, i)
    full = lambda shape: pl.BlockSpec(shape, lambda i: (0, 0))
    et_spec = pl.BlockSpec((N_EXPERTS, ROW_TILE), col)
    et_shape = jax.ShapeDtypeStruct((N_EXPERTS, N_TOK), F32)
    return pl.pallas_call(
        functools.partial(_out_kernel, n_x=len(x_parts)),
        grid=(N_BLK,),
        in_specs=_row_specs(x_parts, D_MODEL) + 4 * _ctx_lat_specs(256) + [
            pl.BlockSpec((None, 1, 6 * D_MODEL), lambda i: (_mod_row(i), 0, 0)),
            full((1, D_MODEL)), full((D_MODEL, D_MODEL)),
            full((N_EXPERTS, D_MODEL)), full((N_EXPERTS, 1)),
            full((D_MODEL, D_SHARED)), full((D_MODEL, D_SHARED)), full((D_SHARED, D_MODEL))],
        out_specs=[pl.BlockSpec((ROW_TILE, D_MODEL), row), pl.BlockSpec((ROW_TILE, D_MODEL), row),
                   et_spec, et_spec,
                   pl.BlockSpec((None, 8, 128), lambda i: (i, 0, 0))],
        out_shape=[jax.ShapeDtypeStruct((N_TOK, D_MODEL), F32),
                   jax.ShapeDtypeStruct((N_TOK, D_MODEL), BF16),
                   et_shape, et_shape,
                   jax.ShapeDtypeStruct((N_BLK, 8, 128), F32)],
        compiler_params=_params("arbitrary"),
        name="out_proj",
    )(*x_parts, *[a for pair in mixer_pairs for a in pair], mod, g, wo, wrt, br, wsg, wsu, wsd)


def _segment_plan(cnt):
    cnt = cnt[:, 0, :N_EXPERTS].astype(jnp.int32)
    seg = (cnt + (SEG_ALIGN - 1)) // SEG_ALIGN * SEG_ALIGN
    local = jnp.cumsum(seg, axis=1) - seg
    total = jnp.sum(seg, axis=1)
    per_expert = jnp.sum(seg, axis=0)
    padded = (per_expert + (EXPERT_TILE - 1)) // EXPERT_TILE * EXPERT_TILE
    ends = jnp.cumsum(padded)
    start = ends - padded
    dest = start[None, :] + jnp.cumsum(seg, axis=0) - seg
    n_tiles = ends[-1] // EXPERT_TILE
    tiles = jnp.arange(MAX_TILES, dtype=jnp.int32)
    tile_expert = jnp.sum((ends // EXPERT_TILE)[None, :] <= jnp.minimum(tiles, n_tiles - 1)[:, None], axis=1)
    tile_expert = tile_expert.astype(jnp.int32)
    plan = dict(seg=seg, local=local, total=total.astype(jnp.int32), dest=dest.astype(jnp.int32),
                tail_start=(start + per_expert).astype(jnp.int32), tail_rows=(padded - per_expert).astype(jnp.int32),
                n_tiles=n_tiles.reshape(1).astype(jnp.int32), tile_expert=tile_expert)
    segf, localf = seg.astype(F32), local.astype(F32)
    pad_lanes = lambda a: jnp.concatenate([a, jnp.zeros_like(a)], axis=1)[:, None, :]
    plan.update(seg_row=pad_lanes(segf), local_row=pad_lanes(localf),
                seg_col=segf[:, :, None], local_col=localf[:, :, None])
    return plan


def _local_dest_digits(rank, local_start):
    dest = jnp.where(rank >= 0.0, local_start + rank, NO_DEST)
    hi = jnp.floor(dest * (1.0 / 64.0))
    return hi, dest - 64.0 * hi


def _dispatch_kernel(dest_ref, seg_ref, local_ref, total_ref, tail_start_ref, tail_rows_ref, nt_ref,
                     h_ref, rank_ref, local_col_ref, local_row_ref, seg_row_ref,
                     xs_hbm, buf, zeros, sems, zsem, usem):
    b = pl.program_id(0)
    slot = b % 2

    def wait_block(blk, s):
        n = pl.multiple_of(total_ref[blk], SEG_ALIGN)
        pltpu.make_async_copy(buf.at[s, pl.ds(0, n)], xs_hbm.at[pl.ds(0, n)], sems.at[s]).wait()

    @pl.when(b == 0)
    def _():
        zeros[...] = jnp.zeros_like(zeros)

    def unused_tiles(action):
        for k in range(-(-MAX_TILES // N_BLK)):
            t = nt_ref[0] + b + k * N_BLK

            @pl.when(t < MAX_TILES)
            def _():
                cp = pltpu.make_async_copy(zeros, xs_hbm.at[pl.ds(pl.multiple_of(t * EXPERT_TILE, EXPERT_TILE),
                                                                  EXPERT_TILE)], usem)
                cp.start() if action == "start" else cp.wait()

    unused_tiles("start")

    @pl.when(b >= 2)
    def _():
        wait_block(b - 2, slot)

    hi, lo = _local_dest_digits(rank_ref[...], local_col_ref[...])
    code = jnp.concatenate([hi, lo], axis=0).astype(BF16)
    hb = h_ref[...]
    lstart = local_row_ref[:, :N_EXPERTS]
    lend = lstart + seg_row_ref[:, :N_EXPERTS]
    for c in range(LOCAL_ROWS // LOCAL_CHUNK):
        r = (lax.broadcasted_iota(jnp.int32, (LOCAL_CHUNK, N_EXPERTS), 0) + c * LOCAL_CHUNK).astype(F32)
        member = (r >= lstart) & (r < lend)
        table = jnp.concatenate([jnp.where(member, 64.0, 0.0), jnp.where(member, 1.0, 0.0)], axis=1).astype(BF16)
        d = _dot(table, code)
        rr = (lax.broadcasted_iota(jnp.int32, (LOCAL_CHUNK, ROW_TILE), 0) + c * LOCAL_CHUNK).astype(F32)
        onehot = jnp.where(d == rr, 1.0, 0.0).astype(BF16)
        buf[slot, c * LOCAL_CHUNK:(c + 1) * LOCAL_CHUNK, :] = _dot(onehot, hb).astype(BF16)

    for e in range(N_EXPERTS):
        n = pl.multiple_of(seg_ref[b, e], SEG_ALIGN)

        @pl.when(n > 0)
        def _():
            src = pl.multiple_of(local_ref[b, e], SEG_ALIGN)
            dst = pl.multiple_of(dest_ref[b, e], SEG_ALIGN)
            pltpu.make_async_copy(buf.at[slot, pl.ds(src, n)], xs_hbm.at[pl.ds(dst, n)], sems.at[slot]).start()

    unused_tiles("wait")

    @pl.when(b == N_BLK - 1)
    def _():
        for e in range(N_EXPERTS):
            n = pl.multiple_of(tail_rows_ref[e], SEG_ALIGN)

            @pl.when(n > 0)
            def _():
                dst = pl.multiple_of(tail_start_ref[e], SEG_ALIGN)
                pltpu.make_async_copy(zeros.at[pl.ds(0, n)], xs_hbm.at[pl.ds(dst, n)], zsem).start()
        for e in range(N_EXPERTS):
            n = pl.multiple_of(tail_rows_ref[e], SEG_ALIGN)

            @pl.when(n > 0)
            def _():
                pltpu.make_async_copy(zeros.at[pl.ds(0, n)], xs_hbm.at[pl.ds(0, n)], zsem).wait()
        wait_block(b - 1, 1 - slot)
        wait_block(b, slot)


def _dispatch(h, rank_t, plan):
    blk = lambda shape, imap: pl.BlockSpec(shape, imap)
    return pl.pallas_call(
        _dispatch_kernel,
        grid_spec=pltpu.PrefetchScalarGridSpec(
            num_scalar_prefetch=7, grid=(N_BLK,),
            in_specs=[blk((ROW_TILE, D_MODEL), lambda b, *_: (b, 0)),
                      blk((N_EXPERTS, ROW_TILE), lambda b, *_: (0, b)),
                      blk((None, N_EXPERTS, 1), lambda b, *_: (b, 0, 0)),
                      blk((None, 1, 128), lambda b, *_: (b, 0, 0)),
                      blk((None, 1, 128), lambda b, *_: (b, 0, 0))],
            out_specs=pl.BlockSpec(memory_space=pl.ANY),
            scratch_shapes=[pltpu.VMEM((2, LOCAL_ROWS, D_MODEL), BF16),
                            pltpu.VMEM((EXPERT_TILE, D_MODEL), BF16),
                            pltpu.SemaphoreType.DMA((2,)), pltpu.SemaphoreType.DMA(()),
                            pltpu.SemaphoreType.DMA(())]),
        out_shape=jax.ShapeDtypeStruct((SORTED_ROWS, D_MODEL), BF16),
        compiler_params=_params("arbitrary"),
        name="dispatch",
    )(plan['dest'], plan['seg'], plan['local'], plan['total'], plan['tail_start'], plan['tail_rows'],
      plan['n_tiles'], h, rank_t, plan['local_col'], plan['local_row'], plan['seg_row'])


def _expert_kernel(te_ref, nt_ref, x_ref, wg_ref, wu_ref, wd_ref, y_ref, wg_s, wu_s, wd_s):
    i = pl.program_id(0)
    active = i < nt_ref[0]

    @pl.when((i == 0) | (te_ref[i] != te_ref[jnp.maximum(i - 1, 0)]))
    def _():
        wg_s[...] = wg_ref[...].astype(BF16)
        wu_s[...] = wu_ref[...].astype(BF16)
        wd_s[...] = wd_ref[...].astype(BF16)

    @pl.when(active)
    def _():
        x = x_ref[...]
        hid = _silu(_dot(x, wg_s[...])) * _dot(x, wu_s[...])
        y_ref[...] = _dot(hid.astype(BF16), wd_s[...]).astype(BF16)

    @pl.when(jnp.logical_not(active))
    def _():
        y_ref[...] = jnp.zeros_like(y_ref)


def _experts(xs, plan, w_gate, w_up, w_down, l):
    rows_in = pl.BlockSpec((EXPERT_TILE, D_MODEL), lambda i, te, nt: (jnp.minimum(i, nt[0] - 1), 0))
    rows_out = pl.BlockSpec((EXPERT_TILE, D_MODEL), lambda i, te, nt: (i, 0))
    wspec = lambda shape: pl.BlockSpec((None, None) + shape, lambda i, te, nt: (l, te[i], 0, 0))
    return pl.pallas_call(
        _expert_kernel,
        grid_spec=pltpu.PrefetchScalarGridSpec(
            num_scalar_prefetch=2, grid=(MAX_TILES,),
            in_specs=[rows_in, wspec((D_MODEL, D_EXPERT)), wspec((D_MODEL, D_EXPERT)), wspec((D_EXPERT, D_MODEL))],
            out_specs=rows_out,
            scratch_shapes=[pltpu.VMEM((D_MODEL, D_EXPERT), BF16), pltpu.VMEM((D_MODEL, D_EXPERT), BF16),
                            pltpu.VMEM((D_EXPERT, D_MODEL), BF16)]),
        out_shape=jax.ShapeDtypeStruct((SORTED_ROWS, D_MODEL), BF16),
        compiler_params=_params("arbitrary"),
        name="experts",
    )(plan['tile_expert'], plan['n_tiles'], xs, w_gate, w_up, w_down)


def _combine_kernel(*refs, final):
    refs = list(refs)
    dest_ref, seg_ref, local_ref, total_ref = refs[:4]
    x1_ref, ys_hbm, gate_ref, rank_ref, mod_ref, local_row_ref, local_col_ref, seg_col_ref = refs[4:12]
    rest = refs[12:]
    fg_ref = rest.pop(0) if final else None
    outs, (buf, sems) = rest[:-2], rest[-2:]
    b = pl.program_id(0)
    slot = b % 2

    def fetch(blk, s):
        for e in range(N_EXPERTS):
            n = pl.multiple_of(seg_ref[blk, e], SEG_ALIGN)

            @pl.when(n > 0)
            def _():
                src = pl.multiple_of(dest_ref[blk, e], SEG_ALIGN)
                dst = pl.multiple_of(local_ref[blk, e], SEG_ALIGN)
                pltpu.make_async_copy(ys_hbm.at[pl.ds(src, n)], buf.at[s, pl.ds(dst, n)], sems.at[s]).start()

    @pl.when(b == 0)
    def _():
        buf[...] = jnp.zeros_like(buf)
        fetch(0, 0)

    @pl.when(b + 1 < N_BLK)
    def _():
        fetch(b + 1, 1 - slot)

    n_rows = pl.multiple_of(total_ref[b], SEG_ALIGN)
    pltpu.make_async_copy(ys_hbm.at[pl.ds(0, n_rows)], buf.at[slot, pl.ds(0, n_rows)], sems.at[slot]).wait()

    gate = gate_ref[...].T
    hi, lo = _local_dest_digits(rank_ref[...].T, local_row_ref[:, :N_EXPERTS])
    code = jnp.concatenate([hi, lo], axis=1).astype(BF16)
    gb = gate.astype(BF16)
    lstart = local_col_ref[...]
    lend = lstart + seg_col_ref[...]
    routed = jnp.zeros((ROW_TILE, D_MODEL), F32)
    for c in range(LOCAL_ROWS // LOCAL_CHUNK):
        r = (lax.broadcasted_iota(jnp.int32, (N_EXPERTS, LOCAL_CHUNK), 1) + c * LOCAL_CHUNK).astype(F32)
        member = (r >= lstart) & (r < lend)
        ones = jnp.where(member, 1.0, 0.0)
        table = jnp.concatenate([ones * 64.0, ones], axis=0).astype(BF16)
        d = _dot(code, table)
        gx = _dot(gb, ones.astype(BF16))
        rr = (lax.broadcasted_iota(jnp.int32, (ROW_TILE, LOCAL_CHUNK), 1) + c * LOCAL_CHUNK).astype(F32)
        weights = jnp.where(d == rr, gx, 0.0).astype(BF16)
        routed = routed + _dot(weights, buf[slot, c * LOCAL_CHUNK:(c + 1) * LOCAL_CHUNK, :])
    x = x1_ref[...] + mod_ref[:, 5 * D_MODEL:6 * D_MODEL] * routed
    if final:
        y = _rms(x, fg_ref[...])

        @pl.when(b < CTX_TILES)
        def _():
            outs[0][...] = y

        @pl.when(b >= CTX_TILES)
        def _():
            outs[1][...] = y
    else:
        outs[0][...] = x


def _combine(x1, ys, gate_t, rank_t, mod, plan, final_g):
    final = final_g is not None
    blk = lambda shape, imap: pl.BlockSpec(shape, imap)
    xspec = blk((ROW_TILE, D_MODEL), lambda b, *_: (b, 0))
    if final:
        out_specs = _ctx_lat_specs(D_MODEL)
        out_shape = [jax.ShapeDtypeStruct((N_CTX, D_MODEL), F32), jax.ShapeDtypeStruct((N_LAT, D_MODEL), F32)]
    else:
        out_specs, out_shape = xspec, jax.ShapeDtypeStruct((N_TOK, D_MODEL), F32)
    et_spec = blk((N_EXPERTS, ROW_TILE), lambda b, *_: (0, b))
    in_specs = [xspec, pl.BlockSpec(memory_space=pl.ANY), et_spec, et_spec,
                blk((None, 1, 6 * D_MODEL), lambda b, *_: (_mod_row(b), 0, 0)),
                blk((None, 1, 128), lambda b, *_: (b, 0, 0)),
                blk((None, N_EXPERTS, 1), lambda b, *_: (b, 0, 0)),
                blk((None, N_EXPERTS, 1), lambda b, *_: (b, 0, 0))]
    args = [x1, ys, gate_t, rank_t, mod, plan['local_row'], plan['local_col'], plan['seg_col']]
    if final:
        in_specs.append(blk((1, D_MODEL), lambda b, *_: (0, 0)))
        args.append(final_g)
    return pl.pallas_call(
        functools.partial(_combine_kernel, final=final),
        grid_spec=pltpu.PrefetchScalarGridSpec(
            num_scalar_prefetch=4, grid=(N_BLK,),
            in_specs=in_specs, out_specs=out_specs,
            scratch_shapes=[pltpu.VMEM((2, LOCAL_ROWS, D_MODEL), BF16), pltpu.SemaphoreType.DMA((2,))]),
        out_shape=out_shape,
        compiler_params=_params("arbitrary"),
        name="combine",
    )(plan['dest'], plan['seg'], plan['local'], plan['total'], *args)


def _rope_full_tables(dim, n_rep):
    rows = DEC_SEQ // GRID_W
    r_idx, c_idx = np.meshgrid(np.arange(rows), np.arange(GRID_W), indexing='ij')
    pos = jnp.asarray(np.stack([r_idx.reshape(-1), c_idx.reshape(-1)], axis=-1), F32)
    nf = dim // 4
    inv = ROPE_BASE ** (-jnp.arange(nf, dtype=F32) / nf)
    ang = pos[:, :, None] * inv
    ang = jnp.repeat(ang.reshape(DEC_SEQ, 2 * nf), 2, axis=1)
    sign = jnp.tile(jnp.asarray([-1.0, 1.0], F32), dim // 2)
    return jnp.tile(jnp.cos(ang), (1, n_rep)), jnp.tile(jnp.sin(ang) * sign, (1, n_rep))


def _pack_w_in(w):
    c0 = MLA_Q_LORA + MLA_KV_LORA
    kr = w[:, c0:MLA_IN]
    s0 = MLA_IN + HG_IN + FN_IN
    qh = [w[:, s0 + h * SWA_HD:s0 + (h + 1) * SWA_HD] for h in SWA_STACK_ORDER]
    return jnp.concatenate([w[:, :c0], kr, kr, kr, kr, w[:, MLA_IN:s0]] + qh
                           + [w[:, s0 + SWA_QW:]], axis=1).astype(BF16)


def _pack_w_q_up(w):
    hd = MLA_NOPE + MLA_ROPE
    nope = [w[:, h * hd:h * hd + MLA_NOPE] for h in range(MLA_HEADS)]
    rope = [w[:, h * hd + MLA_NOPE:(h + 1) * hd] for h in range(MLA_HEADS)]
    return jnp.concatenate(nope + rope, axis=1).astype(BF16)


def _pack_w_kv_up(w):
    hd = MLA_NOPE + MLA_V
    kn = [w[:, h * hd:h * hd + MLA_NOPE] for h in range(MLA_HEADS)]
    vv = [w[:, h * hd + MLA_NOPE:(h + 1) * hd] for h in range(MLA_HEADS)]
    return jnp.concatenate(kn + vv, axis=1).astype(BF16)


def _pack_w_out(w):
    s0 = 3 * 256
    rows = [w[s0 + h * SWA_HD:s0 + (h + 1) * SWA_HD] for h in SWA_STACK_ORDER]
    return jnp.concatenate([w[:s0]] + rows, axis=0).astype(BF16)


def kernel(x_prompt, x_sample, c, cache_mla_ckv, cache_mla_krope, cache_swa_k, cache_swa_v, state_hgrn,
           c_ctx, w_ada, b_ada, norm1_g, norm2_g, w_in, mla_q_norm_g, mla_w_q_up, mla_kv_norm_g, mla_w_kv_up,
           hg_lb_logits, hg_norm_g, fn_w, swa_sink, w_out, moe_w_router, moe_b_router, moe_w_gate, moe_w_up,
           moe_w_down, sh_w_gate, sh_w_up, sh_w_down, final_norm_g):
    x_parts = (x_prompt.reshape(N_CTX, D_MODEL), x_sample.reshape(N_LAT, D_MODEL))
    cv8 = jnp.concatenate([c_ctx[None, :], c, jnp.zeros((8 - 1 - DEC_BATCH, D_MODEL), F32)], axis=0)
    mods = _ada(cv8, w_ada, b_ada).reshape(DEPTH, 8, 1, 6 * D_MODEL)

    lb = jnp.cumsum(jax.nn.softmax(hg_lb_logits.astype(F32), axis=1), axis=1)
    lb = lb - lb[:, :1]

    cos_m, sin_m = _rope_full_tables(MLA_ROPE, MLA_HEADS)
    cos_q, sin_q = _rope_full_tables(SWA_HD, SWA_HEADS)
    cos_k, sin_k = cos_q[:, :SWA_KW], sin_q[:, :SWA_KW]
    cache_k = cache_swa_k.reshape(DEC_BATCH, DEPTH, PAST_LEN, SWA_KW)
    cache_v = cache_swa_v.reshape(DEC_BATCH, DEPTH, PAST_LEN, SWA_KW)
    state_t = jnp.swapaxes(state_hgrn, -1, -2)

    ctx_blk_lat = N_CTX // DEC_SEQ
    new_ckv, new_kr, new_k, new_v, new_st = [], [], [], [], []
    for l in range(DEPTH):
        u_mla, u_hg, u_fn, u_swa = _in_proj(x_parts, mods[l], norm1_g[l][None], _pack_w_in(w_in[l]))

        qg, kvg = mla_q_norm_g[l][None], mla_kv_norm_g[l][None]
        wq, wkv = _pack_w_q_up(mla_w_q_up[l]), _pack_w_kv_up(mla_w_kv_up[l])
        o_mla_c, ckv_c = _mla_ctx(u_mla, qg, wq, kvg, wkv)
        o_mla_l = _mla_lat(u_mla, cache_mla_ckv, cache_mla_krope, l, cos_m, sin_m, qg, wq, kvg, wkv)

        lbf, lbb = lb[0, l][None], lb[1, l][None]
        ng4 = jnp.tile(hg_norm_g[l], HG_HEADS)[None]
        o_hg_c, st_c = _hgrn(u_hg, lbf, lbb, ng4, None, seq=SEQ, n_batch=BATCH, row_block0=0)
        o_hg_l = _hgrn(u_hg, lbf, lbb, ng4, state_t[:, l], seq=DEC_SEQ, n_batch=DEC_BATCH,
                       row_block0=ctx_blk_lat)

        fw = fn_w[l].astype(BF16)
        o_fn_c = _fourier(u_fn, fw, seq=SEQ, n_batch=BATCH, row_block0=0)
        o_fn_l = _fourier(u_fn, fw, seq=DEC_SEQ, n_batch=DEC_BATCH, row_block0=ctx_blk_lat)

        sink = swa_sink[l]
        o_swa_c = _swa_ctx(u_swa, sink)
        o_swa_l = _swa_lat(u_swa, cache_k, cache_v, l, sink, cos_q, sin_q, cos_k, sin_k)

        x1, h2, gate_t, rank_t, cnt = _out_proj(
            x_parts, ((o_mla_c, o_mla_l), (o_hg_c, o_hg_l), (o_fn_c, o_fn_l), (o_swa_c, o_swa_l)),
            mods[l], norm2_g[l][None], _pack_w_out(w_out[l]),
            moe_w_router[l].T, moe_b_router[l][:, None],
            sh_w_gate[l].astype(BF16), sh_w_up[l].astype(BF16), sh_w_down[l].astype(BF16))
        plan = _segment_plan(cnt)
        xs = _dispatch(h2, rank_t, plan)
        ys = _experts(xs, plan, moe_w_gate, moe_w_up, moe_w_down, l)
        if l < DEPTH - 1:
            x_parts = (_combine(x1, ys, gate_t, rank_t, mods[l], plan, None),)
        else:
            y_prompt, y_sample = _combine(x1, ys, gate_t, rank_t, mods[l], plan, final_norm_g[None])

        new_ckv.append(ckv_c.reshape(BATCH, SEQ, MLA_KV_LORA))
        new_kr.append(u_mla[:N_CTX, MLA_Q_LORA + MLA_KV_LORA:MLA_IN].reshape(BATCH, SEQ, MLA_ROPE))
        new_k.append(u_swa[:N_CTX, SWA_QW:SWA_QW + SWA_KW].reshape(BATCH, SEQ, SWA_KV_HEADS, SWA_HD))
        new_v.append(u_swa[:N_CTX, SWA_QW + SWA_KW:].reshape(BATCH, SEQ, SWA_KV_HEADS, SWA_HD))
        new_st.append(jnp.swapaxes(st_c, -1, -2))

    y_prompt = y_prompt.reshape(BATCH, SEQ, D_MODEL)
    y_sample = y_sample.reshape(DEC_BATCH, DEC_SEQ, D_MODEL)
    stack = lambda xs: jnp.stack(xs, axis=1)
    return (y_prompt, y_sample, stack(new_ckv), stack(new_kr), stack(new_k), stack(new_v), stack(new_st))
```

```python
import functools

import numpy as np
import jax
import jax.numpy as jnp
from jax import lax
from jax.experimental import pallas as pl
from jax.experimental.pallas import tpu as pltpu

F32 = jnp.float32
BF16 = jnp.bfloat16

D_MODEL = 1024
BATCH = 32
SEQ = 256
DEPTH = 2
DEC_BATCH = 2
DEC_SEQ = 1024
PAST_LEN = 256
GRID_W = 64
EPS = 1e-6
ROPE_BASE = 10000.0
NEG_INF = -1e30

MLA_HEADS = 4
MLA_NOPE = 64
MLA_ROPE = 32
MLA_V = 64
MLA_Q_LORA = 256
MLA_KV_LORA = 128
HG_HEADS = 4
HG_DK = 64
HG_DV = 64
HG_W = HG_HEADS * HG_DK
FN_GROUPS = 4
FN_WIDTH = 256
SWA_HEADS = 4
SWA_KV_HEADS = 2
SWA_HD = 64
WINDOW = 128
N_EXPERTS = 64
TOP_K = 6
D_EXPERT = 256
D_SHARED = 256
ROUTE_SCALE = 2.5

MLA_IN = MLA_Q_LORA + MLA_KV_LORA + MLA_ROPE
HG_IN = 3 * HG_HEADS * HG_DK + 2 * HG_HEADS * HG_DV
FN_IN = FN_WIDTH
SWA_IN = (SWA_HEADS + 2 * SWA_KV_HEADS) * SWA_HD

N_CTX = BATCH * SEQ
N_LAT = DEC_BATCH * DEC_SEQ
N_TOK = N_CTX + N_LAT

MLA_PACK = 512
U_COLS = MLA_PACK + HG_IN + FN_IN + SWA_IN

ROW_TILE = 256
CTX_TILES = N_CTX // ROW_TILE
LAT_TILES_PER_BATCH = DEC_SEQ // ROW_TILE
HG_CHUNK = 32
HG_BLOCK = 256
SWA_QBLK = 128
MLA_QBLK = 256
CTX_PER_STEP = 4
HG_CTX_PER_STEP = 2
VMEM_LIMIT = 56 * 1024 * 1024


def _dot(a, b):
    return jnp.dot(a, b, preferred_element_type=F32)


def _dot_nt(a, b):
    return lax.dot_general(a, b, (((1,), (1,)), ((), ())), preferred_element_type=F32)


def _dot_tn(a, b):
    return lax.dot_general(a, b, (((0,), (0,)), ((), ())), preferred_element_type=F32)


def _split3(x):
    hi = x.astype(BF16)
    r1 = x - hi.astype(F32)
    mid = r1.astype(BF16)
    return hi, mid, (r1 - mid.astype(F32)).astype(BF16)


def _dot_exact_lhs(a, b):
    ab = a.astype(BF16)
    hi, mid, lo = _split3(b)
    return (_dot(ab, lo) + _dot(ab, mid)) + _dot(ab, hi)


def _dot_exact_rhs(a, b):
    bb = b.astype(BF16)
    hi, mid, lo = _split3(a)
    return (_dot(lo, bb) + _dot(mid, bb)) + _dot(hi, bb)


def _rms(x, g):
    return x * lax.rsqrt(jnp.mean(x * x, axis=-1, keepdims=True) + EPS) * g


def _silu(x):
    return x * jax.nn.sigmoid(x)


def _mod_row(i):
    return jnp.where(i < CTX_TILES, 0, 1 + (i - CTX_TILES) // LAT_TILES_PER_BATCH)


def _params(*sem):
    return pltpu.CompilerParams(dimension_semantics=sem, vmem_limit_bytes=VMEM_LIMIT)


ADA_COLS = 1536


def _ada_kernel(cv_ref, w_ref, b_ref, o_ref):
    a = _silu(cv_ref[...]).astype(BF16)
    o_ref[...] = _dot(a, w_ref[...].astype(BF16)) + b_ref[...]


def _ada(cv8, w_ada, b_ada):
    return pl.pallas_call(
        _ada_kernel,
        grid=(DEPTH, 6 * D_MODEL // ADA_COLS),
        in_specs=[
            pl.BlockSpec((8, D_MODEL), lambda l, j: (0, 0)),
            pl.BlockSpec((None, D_MODEL, ADA_COLS), lambda l, j: (l, 0, j)),
            pl.BlockSpec((None, 1, ADA_COLS), lambda l, j: (l, 0, j)),
        ],
        out_specs=pl.BlockSpec((None, 8, ADA_COLS), lambda l, j: (l, 0, j)),
        out_shape=jax.ShapeDtypeStruct((DEPTH, 8, 6 * D_MODEL), F32),
        compiler_params=_params("arbitrary", "arbitrary"),
        name="ada",
    )(cv8, w_ada, b_ada.reshape(DEPTH, 1, 6 * D_MODEL))


def _ctx_lat_specs(width):
    return [pl.BlockSpec((ROW_TILE, width), lambda i, *_: (jnp.minimum(i, CTX_TILES - 1), 0)),
            pl.BlockSpec((ROW_TILE, width), lambda i, *_: (jnp.maximum(i - CTX_TILES, 0), 0))]


def _row_specs(parts, width):
    return _ctx_lat_specs(width) if len(parts) == 2 else [pl.BlockSpec((ROW_TILE, width), lambda i, *_: (i, 0))]


def _read_rows(refs):
    if len(refs) == 1:
        return refs[0][...]
    return jnp.where(pl.program_id(0) < CTX_TILES, refs[0][...], refs[1][...])


def _in_kernel(*refs, n_x):
    x = _read_rows(refs[:n_x])
    mod_ref, g_ref, w_ref, umla_ref, uhg_ref, ufn_ref, uswa_ref = refs[n_x:]
    sh1 = mod_ref[:, 0:D_MODEL]
    sc1 = mod_ref[:, D_MODEL:2 * D_MODEL]
    h = _rms(x, g_ref[...]) * (1.0 + sc1) + sh1
    u = _dot(h.astype(BF16), w_ref[...])
    o = 0
    for ref, width in ((umla_ref, MLA_PACK), (uhg_ref, HG_IN), (ufn_ref, FN_IN), (uswa_ref, SWA_IN)):
        ref[...] = u[:, o:o + width].astype(ref.dtype)
        o += width


def _in_proj(x_parts, mod, g, w):
    row = lambda i: (i, 0)
    widths = (MLA_PACK, HG_IN, FN_IN, SWA_IN)
    return pl.pallas_call(
        functools.partial(_in_kernel, n_x=len(x_parts)),
        grid=(N_TOK // ROW_TILE,),
        in_specs=_row_specs(x_parts, D_MODEL) + [
            pl.BlockSpec((None, 1, 6 * D_MODEL), lambda i: (_mod_row(i), 0, 0)),
            pl.BlockSpec((1, D_MODEL), lambda i: (0, 0)),
            pl.BlockSpec((D_MODEL, U_COLS), lambda i: (0, 0))],
        out_specs=[pl.BlockSpec((ROW_TILE, wd), row) for wd in widths],
        out_shape=[jax.ShapeDtypeStruct((N_TOK, wd), F32) for wd in widths],
        compiler_params=_params("arbitrary"),
        name="in_proj",
    )(*x_parts, mod, g, w)


def _rope(x, cos, sin_signed):
    lane = lax.broadcasted_iota(jnp.int32, x.shape, 1)
    width = x.shape[1]
    swapped = jnp.where(lane % 2 == 0, pltpu.roll(x, width - 1, 1), pltpu.roll(x, 1, 1))
    return x * cos + swapped * sin_signed


def _stack_heads(x, n_heads, head_w):
    lane = lax.broadcasted_iota(jnp.int32, x.shape, 1)
    return jnp.concatenate([jnp.where(lane // head_w == h, x, 0.0) for h in range(n_heads)], axis=0)


def _unstack_heads(o, n_heads, head_w):
    t = o.shape[0] // n_heads
    lane = lax.broadcasted_iota(jnp.int32, (t, o.shape[1]), 1)
    out = jnp.zeros((t, o.shape[1]), F32)
    for h in range(n_heads):
        out = jnp.where(lane // head_w == h, o[h * t:(h + 1) * t], out)
    return out


MLA_SCALE = (MLA_NOPE + MLA_ROPE) ** -0.5
MLA_QW = MLA_HEADS * MLA_NOPE + MLA_HEADS * MLA_ROPE
MLA_NW = MLA_HEADS * MLA_NOPE


def _mla_attend(q, kcat, v):
    qs = jnp.concatenate([_stack_heads(q[:, :MLA_NW], MLA_HEADS, MLA_NOPE),
                          _stack_heads(q[:, MLA_NW:], MLA_HEADS, MLA_ROPE)], axis=1)
    s = _dot_nt(qs.astype(BF16), kcat) * MLA_SCALE
    p = jnp.exp(s - jnp.max(s, axis=-1, keepdims=True))
    o = _dot(p.astype(BF16), v) / jnp.sum(p, axis=-1, keepdims=True)
    return _unstack_heads(o, MLA_HEADS, MLA_V)


def _mla_ctx_kernel(u_ref, qg_ref, wq_ref, kvg_ref, wkv_ref, o_ref, ckv_ref):
    for j in range(CTX_PER_STEP):
        rows = slice(j * SEQ, (j + 1) * SEQ)
        u = u_ref[rows, :]
        q = _dot(_rms(u[:, :MLA_Q_LORA], qg_ref[...]).astype(BF16), wq_ref[...])
        ckv = _rms(u[:, MLA_Q_LORA:MLA_Q_LORA + MLA_KV_LORA], kvg_ref[...])
        ckv_ref[rows, :] = ckv
        kv = _dot(ckv.astype(BF16), wkv_ref[...])
        kr4 = u[:, MLA_Q_LORA + MLA_KV_LORA:]
        kcat = jnp.concatenate([kv[:, :MLA_NW], kr4], axis=1).astype(BF16)
        o_ref[rows, :] = _mla_attend(q, kcat, kv[:, MLA_NW:].astype(BF16)).astype(o_ref.dtype)


def _mla_ctx(u_mla, qg, wq, kvg, wkv):
    full = lambda shape: pl.BlockSpec(shape, lambda b: (0, 0))
    rows = CTX_PER_STEP * SEQ
    return pl.pallas_call(
        _mla_ctx_kernel,
        grid=(BATCH // CTX_PER_STEP,),
        in_specs=[pl.BlockSpec((rows, MLA_PACK), lambda b: (b, 0)),
                  full((1, MLA_Q_LORA)), full((MLA_Q_LORA, MLA_QW)),
                  full((1, MLA_KV_LORA)), full((MLA_KV_LORA, 2 * MLA_NW))],
        out_specs=[pl.BlockSpec((rows, MLA_NW), lambda b: (b, 0)),
                   pl.BlockSpec((rows, MLA_KV_LORA), lambda b: (b, 0))],
        out_shape=[jax.ShapeDtypeStruct((N_CTX, MLA_NW), BF16),
                   jax.ShapeDtypeStruct((N_CTX, MLA_KV_LORA), F32)],
        compiler_params=_params("arbitrary"),
        name="mla_ctx",
    )(u_mla, qg, wq, kvg, wkv)


MLA_TK = PAST_LEN + DEC_SEQ


def _mla_lat_kernel(u_ref, cckv_ref, ckr_ref, cos_ref, sin_ref, qg_ref, wq_ref, kvg_ref, wkv_ref,
                    o_ref, kcat_s, v_s):
    i = pl.program_id(1)

    @pl.when(i == 0)
    def _():
        u = u_ref[...]
        ckv_new = _rms(u[:, MLA_Q_LORA:MLA_Q_LORA + MLA_KV_LORA], kvg_ref[...])
        ckv_all = jnp.concatenate([cckv_ref[...], ckv_new], axis=0)
        kv = _dot(ckv_all.astype(BF16), wkv_ref[...])
        kr_new = _rope(u[:, MLA_Q_LORA + MLA_KV_LORA:], cos_ref[...], sin_ref[...])
        ckr = ckr_ref[...]
        kr_all = jnp.concatenate([jnp.concatenate([ckr] * MLA_HEADS, axis=1), kr_new], axis=0)
        kcat_s[...] = jnp.concatenate([kv[:, :MLA_NW], kr_all], axis=1).astype(BF16)
        v_s[...] = kv[:, MLA_NW:].astype(BF16)

    r0 = pl.multiple_of(i * MLA_QBLK, MLA_QBLK)
    cq = u_ref[pl.ds(r0, MLA_QBLK), 0:MLA_Q_LORA]
    q = _dot(_rms(cq, qg_ref[...]).astype(BF16), wq_ref[...])
    qr = _rope(q[:, MLA_NW:], cos_ref[pl.ds(r0, MLA_QBLK), :], sin_ref[pl.ds(r0, MLA_QBLK), :])
    q = jnp.concatenate([q[:, :MLA_NW], qr], axis=1)
    o_ref[...] = _mla_attend(q, kcat_s[...], v_s[...]).astype(o_ref.dtype)


def _mla_lat(u_mla, cache_ckv, cache_kr, l, cos, sin, qg, wq, kvg, wkv):
    full = lambda shape: pl.BlockSpec(shape, lambda b, i: (0, 0))
    nq = DEC_SEQ // MLA_QBLK
    return pl.pallas_call(
        _mla_lat_kernel,
        grid=(DEC_BATCH, nq),
        in_specs=[pl.BlockSpec((DEC_SEQ, MLA_PACK), lambda b, i: (N_CTX // DEC_SEQ + b, 0)),
                  pl.BlockSpec((None, None, PAST_LEN, MLA_KV_LORA), lambda b, i: (b, l, 0, 0)),
                  pl.BlockSpec((None, None, PAST_LEN, MLA_ROPE), lambda b, i: (b, l, 0, 0)),
                  full((DEC_SEQ, MLA_HEADS * MLA_ROPE)), full((DEC_SEQ, MLA_HEADS * MLA_ROPE)),
                  full((1, MLA_Q_LORA)), full((MLA_Q_LORA, MLA_QW)),
                  full((1, MLA_KV_LORA)), full((MLA_KV_LORA, 2 * MLA_NW))],
        out_specs=pl.BlockSpec((MLA_QBLK, MLA_NW), lambda b, i: (b * nq + i, 0)),
        out_shape=jax.ShapeDtypeStruct((N_LAT, MLA_NW), BF16),
        scratch_shapes=[pltpu.VMEM((MLA_TK, MLA_QW), BF16), pltpu.VMEM((MLA_TK, MLA_NW), BF16)],
        compiler_params=_params("arbitrary", "arbitrary"),
        name="mla_lat",
    )(u_mla, cache_ckv, cache_kr, cos, sin, qg, wq, kvg, wkv)


def _hgrn_kernel(*refs, seq, n_seq, has_state):
    if has_state:
        (u_ref, lbf_ref, lbb_ref, ng_ref, s0_ref, o_ref,
         q_s, kf_s, gf_s, kb_s, gb_s, of_s, ob_s, stf_s, stb_s) = refs
    else:
        (u_ref, lbf_ref, lbb_ref, ng_ref, o_ref, so_ref,
         q_s, kf_s, gf_s, kb_s, gb_s, of_s, ob_s, stf_s, stb_s) = refs
    C = HG_CHUNK
    W = HG_W

    q_s[...] = _silu(u_ref[:, 0:W])
    ff = lbf_ref[...] + (1.0 - lbf_ref[...]) * jax.nn.sigmoid(u_ref[:, W:2 * W])
    kf_s[...] = 1.0 - ff
    gf_s[...] = jnp.log(ff)
    fb = lbb_ref[...] + (1.0 - lbb_ref[...]) * jax.nn.sigmoid(u_ref[:, 2 * W:3 * W])
    kb_s[...] = 1.0 - fb
    gb_s[...] = jnp.log(fb)

    rr = lax.broadcasted_iota(jnp.int32, (W, W), 0)
    cc = lax.broadcasted_iota(jnp.int32, (W, W), 1)
    blockdiag = rr // HG_DK == cc // HG_DK
    if has_state:
        for st, d in ((stf_s, 0), (stb_s, 1)):
            rows = []
            for h in range(HG_HEADS):
                z = lambda n: jnp.zeros((HG_DV, n * HG_DK), F32)
                parts = ([z(h)] if h else []) + [s0_ref[d, h]] + ([z(HG_HEADS - 1 - h)] if h < HG_HEADS - 1 else [])
                rows.append(jnp.concatenate(parts, axis=1) if len(parts) > 1 else parts[0])
            st[0] = jnp.concatenate(rows, axis=0)
    else:
        stf_s[...] = jnp.zeros_like(stf_s)
        stb_s[...] = jnp.zeros_like(stb_s)

    B = HG_BLOCK
    per_block = B // C
    n_blocks = seq // B
    ri = lax.broadcasted_iota(jnp.int32, (B, B), 0)
    ci = lax.broadcasted_iota(jnp.int32, (B, B), 1)
    same_chunk = ri // C == ci // C
    rs = lax.broadcasted_iota(jnp.int32, (HG_HEADS * B, B), 0) % B
    cs = lax.broadcasted_iota(jnp.int32, (HG_HEADS * B, B), 1)
    same_chunk_s = rs // C == cs // C

    def sums(to_end, to_mid):
        f = lambda m: jnp.where(same_chunk & m, 1.0, 0.0)
        whole = jnp.where(same_chunk, 1.0, 0.0)
        return jnp.concatenate([f(to_end), f(to_end) - f(to_mid), whole - f(to_end)], axis=0)

    mid_f = (ri // C) * C + (C // 2 - 1)
    mid_b = (ri // C) * C + C // 2
    sums_f = sums(ci <= ri, ci <= mid_f)
    sums_b = sums(ci >= ri, ci >= mid_b)
    keep_f = same_chunk_s & (rs >= cs)
    keep_b = same_chunk_s & (cs >= rs)

    def block(r, k_s, g_s, o_s, st_s, sum_mat, keep, order):
        q = q_s[pl.ds(r, B), :]
        k = k_s[pl.ds(r, B), :]
        v = u_ref[pl.ds(r, B), 3 * W:4 * W].astype(BF16)
        gs = _dot_exact_lhs(sum_mat, g_s[pl.ds(r, B), :])
        G, Gq, Gk2 = gs[:B], gs[B:2 * B], gs[2 * B:]
        qe = _stack_heads(q * jnp.exp(Gq), HG_HEADS, HG_DK)
        ke = k * jnp.exp(-Gq)
        A = jnp.where(keep, _dot_nt(qe.astype(BF16), ke.astype(BF16)), 0.0)
        o_intra = _unstack_heads(_dot(A.astype(BF16), v), HG_HEADS, HG_DV)
        qg = (q * jnp.exp(G)).astype(BF16)
        k2 = (k * jnp.exp(Gk2)).astype(BF16)
        decay = jnp.exp(G + Gk2)
        st = st_s[...]
        o_inter = [None] * per_block
        for c in order:
            rows = slice(c * C, (c + 1) * C)
            o_inter[c] = _dot_nt(qg[rows], st.astype(BF16))
            st = st * decay[c * C:c * C + 1] + jnp.where(blockdiag, _dot_tn(v[rows], k2[rows]), 0.0)
        st_s[...] = st
        o_s[pl.ds(r, B), :] = o_intra + jnp.concatenate(o_inter, axis=0)

    def fwd(j, r):
        block(r, kf_s, gf_s, of_s, stf_s.at[j], sums_f, keep_f, range(per_block))

    def bwd(j, r):
        block(r, kb_s, gb_s, ob_s, stb_s.at[j], sums_b, keep_b, range(per_block - 1, -1, -1))

    for j in range(n_seq):
        if n_blocks == 1:
            fwd(j, j * seq)
            bwd(j, j * seq)
        else:
            lax.fori_loop(0, n_blocks, lambda i, c, j=j: (fwd(j, pl.multiple_of(j * seq + i * B, B)), c)[1], 0)
            lax.fori_loop(0, n_blocks,
                          lambda i, c, j=j: (bwd(j, pl.multiple_of(j * seq + (n_blocks - 1 - i) * B, B)), c)[1], 0)

    o = of_s[...] + ob_s[...]
    ms = _dot_exact_rhs(o * o, jnp.where(blockdiag, 1.0 / HG_DV, 0.0))
    on = o * lax.rsqrt(ms + EPS) * ng_ref[...]
    o_ref[...] = (on * _silu(u_ref[:, 4 * W:5 * W])).astype(o_ref.dtype)

    if not has_state:
        for j in range(n_seq):
            for st, d in ((stf_s, 0), (stb_s, 1)):
                for h in range(HG_HEADS):
                    so_ref[j, d, h] = st[j, h * HG_DV:(h + 1) * HG_DV, h * HG_DK:(h + 1) * HG_DK]


def _hgrn(u_hg, lbf, lbb, ng4, state_t, *, seq, n_batch, row_block0):
    has_state = state_t is not None
    n_seq = 1 if has_state else HG_CTX_PER_STEP
    rows = n_seq * seq
    full = lambda shape: pl.BlockSpec(shape, lambda b: (0, 0))
    in_specs = [pl.BlockSpec((rows, HG_IN), lambda b: (row_block0 + b, 0)),
                full((1, HG_W)), full((1, HG_W)), full((1, HG_W))]
    args = [u_hg, lbf, lbb, ng4]
    o_spec = pl.BlockSpec((rows, HG_W), lambda b: (b, 0))
    o_shape = jax.ShapeDtypeStruct((n_batch * seq, HG_W), BF16)
    if has_state:
        in_specs.append(pl.BlockSpec((None, 2, HG_HEADS, HG_DV, HG_DK), lambda b: (b, 0, 0, 0, 0)))
        args.append(state_t)
        out_specs, out_shape = o_spec, o_shape
    else:
        out_specs = [o_spec, pl.BlockSpec((n_seq, 2, HG_HEADS, HG_DV, HG_DK), lambda b: (b, 0, 0, 0, 0))]
        out_shape = [o_shape, jax.ShapeDtypeStruct((n_batch, 2, HG_HEADS, HG_DV, HG_DK), F32)]
    return pl.pallas_call(
        functools.partial(_hgrn_kernel, seq=seq, n_seq=n_seq, has_state=has_state),
        grid=(n_batch // n_seq,),
        in_specs=in_specs, out_specs=out_specs, out_shape=out_shape,
        scratch_shapes=[pltpu.VMEM((rows, HG_W), F32)] * 7 + [pltpu.VMEM((n_seq, HG_W, HG_W), F32)] * 2,
        compiler_params=_params("arbitrary"),
        name="hgrn_lat" if has_state else "hgrn_ctx",
    )(*args)


def _dft_tables(n):
    j = np.arange(n, dtype=np.int64)
    ang = 2.0 * np.pi * ((j[:, None] * j[None, :]) % n).astype(np.float64) / n
    return np.cos(ang) / np.sqrt(n), np.sin(ang) / np.sqrt(n)


def _fourier_tables(seq):
    gw = FN_WIDTH // FN_GROUPS
    cg, sg = _dft_tables(gw)
    eye = np.eye(FN_GROUPS)
    chan = np.concatenate([np.kron(eye, cg), np.kron(eye, sg)], axis=1)
    ct, st = _dft_tables(seq)
    pos = np.concatenate([ct, -st], axis=1)
    return jnp.asarray(chan, F32).astype(BF16), jnp.asarray(pos, F32).astype(BF16)


def _fourier_kernel(x_ref, chan_ref, pos_ref, w_ref, o_ref):
    x12 = _dot(x_ref[...].astype(BF16), chan_ref[...])
    z = jnp.concatenate([x12[:, :FN_WIDTH], x12[:, FN_WIDTH:]], axis=0).astype(BF16)
    y = _dot(pos_ref[...], z)
    o_ref[...] = _dot(y.astype(BF16), w_ref[...]).astype(o_ref.dtype)


def _fourier(u_fn, w, *, seq, n_batch, row_block0):
    chan, pos = _fourier_tables(seq)
    full = lambda shape: pl.BlockSpec(shape, lambda b: (0, 0))
    return pl.pallas_call(
        _fourier_kernel,
        grid=(n_batch,),
        in_specs=[pl.BlockSpec((seq, FN_WIDTH), lambda b: (row_block0 + b, 0)),
                  full((FN_WIDTH, 2 * FN_WIDTH)), full((seq, 2 * seq)), full((FN_WIDTH, FN_WIDTH))],
        out_specs=pl.BlockSpec((seq, FN_WIDTH), lambda b: (b, 0)),
        out_shape=jax.ShapeDtypeStruct((n_batch * seq, FN_WIDTH), BF16),
        compiler_params=_params("arbitrary"),
        name="fourier",
    )(u_fn, chan, pos, w)


SWA_SCALE = SWA_HD ** -0.5
SWA_QW = SWA_HEADS * SWA_HD
SWA_KW = SWA_KV_HEADS * SWA_HD
SWA_STACK_ORDER = (0, 2, 1, 3)


def _swa_stack_q(q):
    return jnp.concatenate([_stack_heads(q[:, :SWA_KW], SWA_KV_HEADS, SWA_HD),
                            _stack_heads(q[:, SWA_KW:], SWA_KV_HEADS, SWA_HD)], axis=0)


def _swa_unstack_o(o):
    t = o.shape[0] // SWA_HEADS
    return jnp.concatenate([_unstack_heads(o[:2 * t], SWA_KV_HEADS, SWA_HD),
                            _unstack_heads(o[2 * t:], SWA_KV_HEADS, SWA_HD)], axis=1)


def _sink_rows(sink_ref, t):
    return jnp.concatenate([jnp.full((t, 1), sink_ref[h], F32) for h in SWA_STACK_ORDER], axis=0)


def _swa_ctx_kernel(sink_ref, u_ref, o_ref):
    sink = _sink_rows(sink_ref, SEQ)
    for j in range(CTX_PER_STEP):
        rows = slice(j * SEQ, (j + 1) * SEQ)
        u = u_ref[rows, :]
        qs = _swa_stack_q(u[:, :SWA_QW]).astype(BF16)
        k = u[:, SWA_QW:SWA_QW + SWA_KW].astype(BF16)
        v = u[:, SWA_QW + SWA_KW:].astype(BF16)
        s = _dot_nt(qs, k) * SWA_SCALE
        m = jnp.maximum(jnp.max(s, axis=-1, keepdims=True), sink)
        p = jnp.exp(s - m)
        denom = jnp.sum(p, axis=-1, keepdims=True) + jnp.exp(sink - m)
        o_ref[rows, :] = _swa_unstack_o(_dot(p.astype(BF16), v) / denom).astype(o_ref.dtype)


def _swa_ctx(u_swa, sink):
    rows = CTX_PER_STEP * SEQ
    return pl.pallas_call(
        _swa_ctx_kernel,
        grid=(BATCH // CTX_PER_STEP,),
        in_specs=[pl.BlockSpec(memory_space=pltpu.SMEM),
                  pl.BlockSpec((rows, SWA_IN), lambda b: (b, 0))],
        out_specs=pl.BlockSpec((rows, SWA_QW), lambda b: (b, 0)),
        out_shape=jax.ShapeDtypeStruct((N_CTX, SWA_QW), BF16),
        compiler_params=_params("arbitrary"),
        name="swa_ctx",
    )(sink, u_swa)


SWA_PAD = DEC_SEQ + 2 * SWA_QBLK


def _swa_lat_kernel(sink_ref, u_ref, kc_ref, vc_ref, cosq_ref, sinq_ref, cosk_ref, sin_k_ref,
                    o_ref, k_s, v_s):
    i = pl.program_id(1)
    B = SWA_QBLK

    @pl.when(i == 0)
    def _():
        zeros = jnp.zeros((B, SWA_KW), BF16)
        k = _rope(u_ref[:, SWA_QW:SWA_QW + SWA_KW], cosk_ref[...], sin_k_ref[...]).astype(BF16)
        k_s[...] = jnp.concatenate([zeros, k, zeros], axis=0)
        v_s[...] = jnp.concatenate([zeros, u_ref[:, SWA_QW + SWA_KW:].astype(BF16), zeros], axis=0)

    r0 = pl.multiple_of(i * B, B)
    q = _rope(u_ref[pl.ds(r0, B), 0:SWA_QW], cosq_ref[pl.ds(r0, B), :], sinq_ref[pl.ds(r0, B), :])
    qs = _swa_stack_q(q).astype(BF16)
    s_loc = _dot_nt(qs, k_s[pl.ds(r0, 3 * B), :]) * SWA_SCALE
    row = lax.broadcasted_iota(jnp.int32, s_loc.shape, 0) % B
    col = lax.broadcasted_iota(jnp.int32, s_loc.shape, 1)
    kpos = r0 - B + col
    valid = (jnp.abs(row + B - col) <= WINDOW) & (kpos >= 0) & (kpos < DEC_SEQ)
    s_loc = jnp.where(valid, s_loc, NEG_INF)
    s_ctx = _dot_nt(qs, kc_ref[...].astype(BF16)) * SWA_SCALE
    sink = _sink_rows(sink_ref, B)
    m = jnp.maximum(jnp.maximum(jnp.max(s_loc, axis=-1, keepdims=True),
                                jnp.max(s_ctx, axis=-1, keepdims=True)), sink)
    p_loc = jnp.exp(s_loc - m)
    p_ctx = jnp.exp(s_ctx - m)
    denom = (jnp.sum(p_loc, axis=-1, keepdims=True) + jnp.sum(p_ctx, axis=-1, keepdims=True)
             + jnp.exp(sink - m))
    o = _dot(p_loc.astype(BF16), v_s[pl.ds(r0, 3 * B), :]) + _dot(p_ctx.astype(BF16), vc_ref[...].astype(BF16))
    o_ref[...] = _swa_unstack_o(o / denom).astype(o_ref.dtype)


def _swa_lat(u_swa, cache_k, cache_v, l, sink, cosq, sinq, cosk, sink_k):
    full = lambda shape: pl.BlockSpec(shape, lambda b, i: (0, 0))
    nq = DEC_SEQ // SWA_QBLK
    cache_spec = pl.BlockSpec((None, None, PAST_LEN, SWA_KW), lambda b, i: (b, l, 0, 0))
    return pl.pallas_call(
        _swa_lat_kernel,
        grid=(DEC_BATCH, nq),
        in_specs=[pl.BlockSpec(memory_space=pltpu.SMEM),
                  pl.BlockSpec((DEC_SEQ, SWA_IN), lambda b, i: (N_CTX // DEC_SEQ + b, 0)),
                  cache_spec, cache_spec,
                  full((DEC_SEQ, SWA_QW)), full((DEC_SEQ, SWA_QW)),
                  full((DEC_SEQ, SWA_KW)), full((DEC_SEQ, SWA_KW))],
        out_specs=pl.BlockSpec((SWA_QBLK, SWA_QW), lambda b, i: (b * nq + i, 0)),
        out_shape=jax.ShapeDtypeStruct((N_LAT, SWA_QW), BF16),
        scratch_shapes=[pltpu.VMEM((SWA_PAD, SWA_KW), BF16), pltpu.VMEM((SWA_PAD, SWA_KW), BF16)],
        compiler_params=_params("arbitrary", "arbitrary"),
        name="swa_lat",
    )(sink, u_swa, cache_k, cache_v, cosq, sinq, cosk, sink_k)


N_BLK = N_TOK // ROW_TILE
SEG_ALIGN = 16
LOCAL_ROWS = 2560
LOCAL_CHUNK = 512
EXPERT_TILE = 512
SORTED_ROWS = -(-(N_TOK * TOP_K + N_BLK * N_EXPERTS * (SEG_ALIGN - 1) + N_EXPERTS * (EXPERT_TILE - SEG_ALIGN))
                // EXPERT_TILE) * EXPERT_TILE
MAX_TILES = SORTED_ROWS // EXPERT_TILE
NOT_PICKED = -1.0
NO_DEST = 4095.0


def _out_kernel(*refs, n_x):
    x = _read_rows(refs[:n_x])
    mixers = refs[n_x:n_x + 8]
    (mod_ref, g_ref, wo_ref, wrt_ref, br_ref, wsg_ref, wsu_ref, wsd_ref,
     x1_ref, h_ref, gate_ref, rank_ref, cnt_ref) = refs[n_x + 8:]
    mix = jnp.zeros((ROW_TILE, D_MODEL), F32)
    for j in range(4):
        mix = mix + _dot(_read_rows(mixers[2 * j:2 * j + 2]), wo_ref[j * 256:(j + 1) * 256, :])
    g1 = mod_ref[:, 2 * D_MODEL:3 * D_MODEL]
    sh2 = mod_ref[:, 3 * D_MODEL:4 * D_MODEL]
    sc2 = mod_ref[:, 4 * D_MODEL:5 * D_MODEL]
    g2 = mod_ref[:, 5 * D_MODEL:6 * D_MODEL]
    x1 = x + g1 * mix
    h = _rms(x1, g_ref[...]) * (1.0 + sc2) + sh2
    hb = h.astype(BF16)
    h_ref[...] = hb

    logits = lax.dot_general(wrt_ref[...], h, (((1,), (1,)), ((), ())),
                             precision=lax.Precision.HIGHEST, preferred_element_type=F32)
    scores = jax.nn.sigmoid(logits)
    sel = scores + br_ref[...]
    eidx = lax.broadcasted_iota(jnp.int32, sel.shape, 0)
    gate = jnp.zeros_like(scores)
    picked = jnp.zeros_like(scores)
    for _ in range(TOP_K):
        best = jnp.max(sel, axis=0, keepdims=True)
        first = jnp.min(jnp.where(sel == best, eidx, N_EXPERTS), axis=0, keepdims=True)
        pick = eidx == first
        gate = jnp.where(pick, scores, gate)
        picked = jnp.where(pick, 1.0, picked)
        sel = jnp.where(pick, -jnp.inf, sel)
    gate = ROUTE_SCALE * gate / jnp.sum(gate, axis=0, keepdims=True)

    ti = lax.broadcasted_iota(jnp.int32, (ROW_TILE, ROW_TILE), 0)
    tj = lax.broadcasted_iota(jnp.int32, (ROW_TILE, ROW_TILE), 1)
    pb = picked.astype(BF16)
    rank = _dot(pb, jnp.where(ti < tj, 1.0, 0.0).astype(BF16))
    gate_ref[...] = gate
    rank_ref[...] = jnp.where(picked > 0.0, rank, NOT_PICKED)
    counts = _dot_nt(jnp.ones((8, ROW_TILE), BF16), pb)
    cnt_ref[...] = jnp.concatenate([counts, jnp.zeros_like(counts)], axis=1)

    hid =_silu(_dot(hb, wsg_ref[...])) * _dot(hb, wsu_ref[...])
    x1_ref[...] = x1 + g2 * _dot(hid.astype(BF16), wsd_ref[...])


def _out_proj(x_parts, mixer_pairs, mod, g, wo, wrt, br, wsg, wsu, wsd):
    row = lambda i: (i, 0)
    col = lambda i: (0, i)
    full = lambda shape: pl.BlockSpec(shape, lambda i: (0, 0))
    et_spec = pl.BlockSpec((N_EXPERTS, ROW_TILE), col)
    et_shape = jax.ShapeDtypeStruct((N_EXPERTS, N_TOK), F32)
    return pl.pallas_call(
        functools.partial(_out_kernel, n_x=len(x_parts)),
        grid=(N_BLK,),
        in_specs=_row_specs(x_parts, D_MODEL) + 4 * _ctx_lat_specs(256) + [
            pl.BlockSpec((None, 1, 6 * D_MODEL), lambda i: (_mod_row(i), 0, 0)),
            full((1, D_MODEL)), full((D_MODEL, D_MODEL)),
            full((N_EXPERTS, D_MODEL)), full((N_EXPERTS, 1)),
            full((D_MODEL, D_SHARED)), full((D_MODEL, D_SHARED)), full((D_SHARED, D_MODEL))],
        out_specs=[pl.BlockSpec((ROW_TILE, D_MODEL), row), pl.BlockSpec((ROW_TILE, D_MODEL), row),
                   et_spec, et_spec,
                   pl.BlockSpec((None, 8, 128), lambda i: (i, 0, 0))],
        out_shape=[jax.ShapeDtypeStruct((N_TOK, D_MODEL), F32),
                   jax.ShapeDtypeStruct((N_TOK, D_MODEL), BF16),
                   et_shape, et_shape,
                   jax.ShapeDtypeStruct((N_BLK, 8, 128), F32)],
        compiler_params=_params("arbitrary"),
        name="out_proj",
    )(*x_parts, *[a for pair in mixer_pairs for a in pair], mod, g, wo, wrt, br, wsg, wsu, wsd)


def _segment_plan(cnt):
    cnt = cnt[:, 0, :N_EXPERTS].astype(jnp.int32)
    seg = (cnt + (SEG_ALIGN - 1)) // SEG_ALIGN * SEG_ALIGN
    local = jnp.cumsum(seg, axis=1) - seg
    total = jnp.sum(seg, axis=1)
    per_expert = jnp.sum(seg, axis=0)
    padded = (per_expert + (EXPERT_TILE - 1)) // EXPERT_TILE * EXPERT_TILE
    ends = jnp.cumsum(padded)
    start = ends - padded
    dest = start[None, :] + jnp.cumsum(seg, axis=0) - seg
    n_tiles = ends[-1] // EXPERT_TILE
    tiles = jnp.arange(MAX_TILES, dtype=jnp.int32)
    tile_expert = jnp.sum((ends // EXPERT_TILE)[None, :] <= jnp.minimum(tiles, n_tiles - 1)[:, None], axis=1)
    tile_expert = tile_expert.astype(jnp.int32)
    plan = dict(seg=seg, local=local, total=total.astype(jnp.int32), dest=dest.astype(jnp.int32),
                tail_start=(start + per_expert).astype(jnp.int32), tail_rows=(padded - per_expert).astype(jnp.int32),
                n_tiles=n_tiles.reshape(1).astype(jnp.int32), tile_expert=tile_expert)
    segf, localf = seg.astype(F32), local.astype(F32)
    pad_lanes = lambda a: jnp.concatenate([a, jnp.zeros_like(a)], axis=1)[:, None, :]
    plan.update(seg_row=pad_lanes(segf), local_row=pad_lanes(localf),
                seg_col=segf[:, :, None], local_col=localf[:, :, None])
    return plan


def _local_dest_digits(rank, local_start):
    dest = jnp.where(rank >= 0.0, local_start + rank, NO_DEST)
    hi = jnp.floor(dest * (1.0 / 64.0))
    return hi, dest - 64.0 * hi


def _dispatch_kernel(dest_ref, seg_ref, local_ref, total_ref, tail_start_ref, tail_rows_ref, nt_ref,
                     h_ref, rank_ref, local_col_ref, local_row_ref, seg_row_ref,
                     xs_hbm, buf, zeros, sems, zsem, usem):
    b = pl.program_id(0)
    slot = b % 2

    def wait_block(blk, s):
        n = pl.multiple_of(total_ref[blk], SEG_ALIGN)
        pltpu.make_async_copy(buf.at[s, pl.ds(0, n)], xs_hbm.at[pl.ds(0, n)], sems.at[s]).wait()

    @pl.when(b == 0)
    def _():
        zeros[...] = jnp.zeros_like(zeros)

    def unused_tiles(action):
        for k in range(-(-MAX_TILES // N_BLK)):
            t = nt_ref[0] + b + k * N_BLK

            @pl.when(t < MAX_TILES)
            def _():
                cp = pltpu.make_async_copy(zeros, xs_hbm.at[pl.ds(pl.multiple_of(t * EXPERT_TILE, EXPERT_TILE),
                                                                  EXPERT_TILE)], usem)
                cp.start() if action == "start" else cp.wait()

    unused_tiles("start")

    @pl.when(b >= 2)
    def _():
        wait_block(b - 2, slot)

    hi, lo = _local_dest_digits(rank_ref[...], local_col_ref[...])
    code = jnp.concatenate([hi, lo], axis=0).astype(BF16)
    hb = h_ref[...]
    lstart = local_row_ref[:, :N_EXPERTS]
    lend = lstart + seg_row_ref[:, :N_EXPERTS]
    for c in range(LOCAL_ROWS // LOCAL_CHUNK):
        r = (lax.broadcasted_iota(jnp.int32, (LOCAL_CHUNK, N_EXPERTS), 0) + c * LOCAL_CHUNK).astype(F32)
        member = (r >= lstart) & (r < lend)
        table = jnp.concatenate([jnp.where(member, 64.0, 0.0), jnp.where(member, 1.0, 0.0)], axis=1).astype(BF16)
        d = _dot(table, code)
        rr = (lax.broadcasted_iota(jnp.int32, (LOCAL_CHUNK, ROW_TILE), 0) + c * LOCAL_CHUNK).astype(F32)
        onehot = jnp.where(d == rr, 1.0, 0.0).astype(BF16)
        buf[slot, c * LOCAL_CHUNK:(c + 1) * LOCAL_CHUNK, :] = _dot(onehot, hb).astype(BF16)

    for e in range(N_EXPERTS):
        n = pl.multiple_of(seg_ref[b, e], SEG_ALIGN)

        @pl.when(n > 0)
        def _():
            src = pl.multiple_of(local_ref[b, e], SEG_ALIGN)
            dst = pl.multiple_of(dest_ref[b, e], SEG_ALIGN)
            pltpu.make_async_copy(buf.at[slot, pl.ds(src, n)], xs_hbm.at[pl.ds(dst, n)], sems.at[slot]).start()

    unused_tiles("wait")

    @pl.when(b == N_BLK - 1)
    def _():
        for e in range(N_EXPERTS):
            n = pl.multiple_of(tail_rows_ref[e], SEG_ALIGN)

            @pl.when(n > 0)
            def _():
                dst = pl.multiple_of(tail_start_ref[e], SEG_ALIGN)
                pltpu.make_async_copy(zeros.at[pl.ds(0, n)], xs_hbm.at[pl.ds(dst, n)], zsem).start()
        for e in range(N_EXPERTS):
            n = pl.multiple_of(tail_rows_ref[e], SEG_ALIGN)

            @pl.when(n > 0)
            def _():
                pltpu.make_async_copy(zeros.at[pl.ds(0, n)], xs_hbm.at[pl.ds(0, n)], zsem).wait()
        wait_block(b - 1, 1 - slot)
        wait_block(b, slot)


def _dispatch(h, rank_t, plan):
    blk = lambda shape, imap: pl.BlockSpec(shape, imap)
    return pl.pallas_call(
        _dispatch_kernel,
        grid_spec=pltpu.PrefetchScalarGridSpec(
            num_scalar_prefetch=7, grid=(N_BLK,),
            in_specs=[blk((ROW_TILE, D_MODEL), lambda b, *_: (b, 0)),
                      blk((N_EXPERTS, ROW_TILE), lambda b, *_: (0, b)),
                      blk((None, N_EXPERTS, 1), lambda b, *_: (b, 0, 0)),
                      blk((None, 1, 128), lambda b, *_: (b, 0, 0)),
                      blk((None, 1, 128), lambda b, *_: (b, 0, 0))],
            out_specs=pl.BlockSpec(memory_space=pl.ANY),
            scratch_shapes=[pltpu.VMEM((2, LOCAL_ROWS, D_MODEL), BF16),
                            pltpu.VMEM((EXPERT_TILE, D_MODEL), BF16),
                            pltpu.SemaphoreType.DMA((2,)), pltpu.SemaphoreType.DMA(()),
                            pltpu.SemaphoreType.DMA(())]),
        out_shape=jax.ShapeDtypeStruct((SORTED_ROWS, D_MODEL), BF16),
        compiler_params=_params("arbitrary"),
        name="dispatch",
    )(plan['dest'], plan['seg'], plan['local'], plan['total'], plan['tail_start'], plan['tail_rows'],
      plan['n_tiles'], h, rank_t, plan['local_col'], plan['local_row'], plan['seg_row'])


def _expert_kernel(te_ref, nt_ref, x_ref, wg_ref, wu_ref, wd_ref, y_ref, wg_s, wu_s, wd_s):
    i = pl.program_id(0)
    active = i < nt_ref[0]

    @pl.when((i == 0) | (te_ref[i] != te_ref[jnp.maximum(i - 1, 0)]))
    def _():
        wg_s[...] = wg_ref[...].astype(BF16)
        wu_s[...] = wu_ref[...].astype(BF16)
        wd_s[...] = wd_ref[...].astype(BF16)

    @pl.when(active)
    def _():
        x = x_ref[...]
        hid = _silu(_dot(x, wg_s[...])) * _dot(x, wu_s[...])
        y_ref[...] = _dot(hid.astype(BF16), wd_s[...]).astype(BF16)

    @pl.when(jnp.logical_not(active))
    def _():
        y_ref[...] = jnp.zeros_like(y_ref)


def _experts(xs, plan, w_gate, w_up, w_down, l):
    rows_in = pl.BlockSpec((EXPERT_TILE, D_MODEL), lambda i, te, nt: (jnp.minimum(i, nt[0] - 1), 0))
    rows_out = pl.BlockSpec((EXPERT_TILE, D_MODEL), lambda i, te, nt: (i, 0))
    wspec = lambda shape: pl.BlockSpec((None, None) + shape, lambda i, te, nt: (l, te[i], 0, 0))
    return pl.pallas_call(
        _expert_kernel,
        grid_spec=pltpu.PrefetchScalarGridSpec(
            num_scalar_prefetch=2, grid=(MAX_TILES,),
            in_specs=[rows_in, wspec((D_MODEL, D_EXPERT)), wspec((D_MODEL, D_EXPERT)), wspec((D_EXPERT, D_MODEL))],
            out_specs=rows_out,
            scratch_shapes=[pltpu.VMEM((D_MODEL, D_EXPERT), BF16), pltpu.VMEM((D_MODEL, D_EXPERT), BF16),
                            pltpu.VMEM((D_EXPERT, D_MODEL), BF16)]),
        out_shape=jax.ShapeDtypeStruct((SORTED_ROWS, D_MODEL), BF16),
        compiler_params=_params("arbitrary"),
        name="experts",
    )(plan['tile_expert'], plan['n_tiles'], xs, w_gate, w_up, w_down)


def _combine_kernel(*refs, final):
    refs = list(refs)
    dest_ref, seg_ref, local_ref, total_ref = refs[:4]
    x1_ref, ys_hbm, gate_ref, rank_ref, mod_ref, local_row_ref, local_col_ref, seg_col_ref = refs[4:12]
    rest = refs[12:]
    fg_ref = rest.pop(0) if final else None
    outs, (buf, sems) = rest[:-2], rest[-2:]
    b = pl.program_id(0)
    slot = b % 2

    def fetch(blk, s):
        for e in range(N_EXPERTS):
            n = pl.multiple_of(seg_ref[blk, e], SEG_ALIGN)

            @pl.when(n > 0)
            def _():
                src = pl.multiple_of(dest_ref[blk, e], SEG_ALIGN)
                dst = pl.multiple_of(local_ref[blk, e], SEG_ALIGN)
                pltpu.make_async_copy(ys_hbm.at[pl.ds(src, n)], buf.at[s, pl.ds(dst, n)], sems.at[s]).start()

    @pl.when(b == 0)
    def _():
        buf[...] = jnp.zeros_like(buf)
        fetch(0, 0)

    @pl.when(b + 1 < N_BLK)
    def _():
        fetch(b + 1, 1 - slot)

    n_rows = pl.multiple_of(total_ref[b], SEG_ALIGN)
    pltpu.make_async_copy(ys_hbm.at[pl.ds(0, n_rows)], buf.at[slot, pl.ds(0, n_rows)], sems.at[slot]).wait()

    gate = gate_ref[...].T
    hi, lo = _local_dest_digits(rank_ref[...].T, local_row_ref[:, :N_EXPERTS])
    code = jnp.concatenate([hi, lo], axis=1).astype(BF16)
    gb = gate.astype(BF16)
    lstart = local_col_ref[...]
    lend = lstart + seg_col_ref[...]
    routed = jnp.zeros((ROW_TILE, D_MODEL), F32)
    for c in range(LOCAL_ROWS // LOCAL_CHUNK):
        r = (lax.broadcasted_iota(jnp.int32, (N_EXPERTS, LOCAL_CHUNK), 1) + c * LOCAL_CHUNK).astype(F32)
        member = (r >= lstart) & (r < lend)
        ones = jnp.where(member, 1.0, 0.0)
        table = jnp.concatenate([ones * 64.0, ones], axis=0).astype(BF16)
        d = _dot(code, table)
        gx = _dot(gb, ones.astype(BF16))
        rr = (lax.broadcasted_iota(jnp.int32, (ROW_TILE, LOCAL_CHUNK), 1) + c * LOCAL_CHUNK).astype(F32)
        weights = jnp.where(d == rr, gx, 0.0).astype(BF16)
        routed = routed + _dot(weights, buf[slot, c * LOCAL_CHUNK:(c + 1) * LOCAL_CHUNK, :])
    x = x1_ref[...] + mod_ref[:, 5 * D_MODEL:6 * D_MODEL] * routed
    if final:
        y = _rms(x, fg_ref[...])

        @pl.when(b < CTX_TILES)
        def _():
            outs[0][...] = y

        @pl.when(b >= CTX_TILES)
        def _():
            outs[1][...] = y
    else:
        outs[0][...] = x


def _combine(x1, ys, gate_t, rank_t, mod, plan, final_g):
    final = final_g is not None
    blk = lambda shape, imap: pl.BlockSpec(shape, imap)
    xspec = blk((ROW_TILE, D_MODEL), lambda b, *_: (b, 0))
    if final:
        out_specs = _ctx_lat_specs(D_MODEL)
        out_shape = [jax.ShapeDtypeStruct((N_CTX, D_MODEL), F32), jax.ShapeDtypeStruct((N_LAT, D_MODEL), F32)]
    else:
        out_specs, out_shape = xspec, jax.ShapeDtypeStruct((N_TOK, D_MODEL), F32)
    et_spec = blk((N_EXPERTS, ROW_TILE), lambda b, *_: (0, b))
    in_specs = [xspec, pl.BlockSpec(memory_space=pl.ANY), et_spec, et_spec,
                blk((None, 1, 6 * D_MODEL), lambda b, *_: (_mod_row(b), 0, 0)),
                blk((None, 1, 128), lambda b, *_: (b, 0, 0)),
                blk((None, N_EXPERTS, 1), lambda b, *_: (b, 0, 0)),
                blk((None, N_EXPERTS, 1), lambda b, *_: (b, 0, 0))]
    args = [x1, ys, gate_t, rank_t, mod, plan['local_row'], plan['local_col'], plan['seg_col']]
    if final:
        in_specs.append(blk((1, D_MODEL), lambda b, *_: (0, 0)))
        args.append(final_g)
    return pl.pallas_call(
        functools.partial(_combine_kernel, final=final),
        grid_spec=pltpu.PrefetchScalarGridSpec(
            num_scalar_prefetch=4, grid=(N_BLK,),
            in_specs=in_specs, out_specs=out_specs,
            scratch_shapes=[pltpu.VMEM((2, LOCAL_ROWS, D_MODEL), BF16), pltpu.SemaphoreType.DMA((2,))]),
        out_shape=out_shape,
        compiler_params=_params("arbitrary"),
        name="combine",
    )(plan['dest'], plan['seg'], plan['local'], plan['total'], *args)


def _rope_full_tables(dim, n_rep):
    rows = DEC_SEQ // GRID_W
    r_idx, c_idx = np.meshgrid(np.arange(rows), np.arange(GRID_W), indexing='ij')
    pos = jnp.asarray(np.stack([r_idx.reshape(-1), c_idx.reshape(-1)], axis=-1), F32)
    nf = dim // 4
    inv = ROPE_BASE ** (-jnp.arange(nf, dtype=F32) / nf)
    ang = pos[:, :, None] * inv
    ang = jnp.repeat(ang.reshape(DEC_SEQ, 2 * nf), 2, axis=1)
    sign = jnp.tile(jnp.asarray([-1.0, 1.0], F32), dim // 2)
    return jnp.tile(jnp.cos(ang), (1, n_rep)), jnp.tile(jnp.sin(ang) * sign, (1, n_rep))


def _pack_w_in(w):
    c0 = MLA_Q_LORA + MLA_KV_LORA
    kr = w[:, c0:MLA_IN]
    s0 = MLA_IN + HG_IN + FN_IN
    qh = [w[:, s0 + h * SWA_HD:s0 + (h + 1) * SWA_HD] for h in SWA_STACK_ORDER]
    return jnp.concatenate([w[:, :c0], kr, kr, kr, kr, w[:, MLA_IN:s0]] + qh
                           + [w[:, s0 + SWA_QW:]], axis=1).astype(BF16)


def _pack_w_q_up(w):
    hd = MLA_NOPE + MLA_ROPE
    nope = [w[:, h * hd:h * hd + MLA_NOPE] for h in range(MLA_HEADS)]
    rope = [w[:, h * hd + MLA_NOPE:(h + 1) * hd] for h in range(MLA_HEADS)]
    return jnp.concatenate(nope + rope, axis=1).astype(BF16)


def _pack_w_kv_up(w):
    hd = MLA_NOPE + MLA_V
    kn = [w[:, h * hd:h * hd + MLA_NOPE] for h in range(MLA_HEADS)]
    vv = [w[:, h * hd + MLA_NOPE:(h + 1) * hd] for h in range(MLA_HEADS)]
    return jnp.concatenate(kn + vv, axis=1).astype(BF16)


def _pack_w_out(w):
    s0 = 3 * 256
    rows = [w[s0 + h * SWA_HD:s0 + (h + 1) * SWA_HD] for h in SWA_STACK_ORDER]
    return jnp.concatenate([w[:s0]] + rows, axis=0).astype(BF16)


def kernel(x_prompt, x_sample, c, cache_mla_ckv, cache_mla_krope, cache_swa_k, cache_swa_v, state_hgrn,
           c_ctx, w_ada, b_ada, norm1_g, norm2_g, w_in, mla_q_norm_g, mla_w_q_up, mla_kv_norm_g, mla_w_kv_up,
           hg_lb_logits, hg_norm_g, fn_w, swa_sink, w_out, moe_w_router, moe_b_router, moe_w_gate, moe_w_up,
           moe_w_down, sh_w_gate, sh_w_up, sh_w_down, final_norm_g):
    x_parts = (x_prompt.reshape(N_CTX, D_MODEL), x_sample.reshape(N_LAT, D_MODEL))
    cv8 = jnp.concatenate([c_ctx[None, :], c, jnp.zeros((8 - 1 - DEC_BATCH, D_MODEL), F32)], axis=0)
    mods = _ada(cv8, w_ada, b_ada).reshape(DEPTH, 8, 1, 6 * D_MODEL)

    lb = jnp.cumsum(jax.nn.softmax(hg_lb_logits.astype(F32), axis=1), axis=1)
    lb = lb - lb[:, :1]

    cos_m, sin_m = _rope_full_tables(MLA_ROPE, MLA_HEADS)
    cos_q, sin_q = _rope_full_tables(SWA_HD, SWA_HEADS)
    cos_k, sin_k = cos_q[:, :SWA_KW], sin_q[:, :SWA_KW]
    cache_k = cache_swa_k.reshape(DEC_BATCH, DEPTH, PAST_LEN, SWA_KW)
    cache_v = cache_swa_v.reshape(DEC_BATCH, DEPTH, PAST_LEN, SWA_KW)
    state_t = jnp.swapaxes(state_hgrn, -1, -2)

    ctx_blk_lat = N_CTX // DEC_SEQ
    new_ckv, new_kr, new_k, new_v, new_st = [], [], [], [], []
    for l in range(DEPTH):
        u_mla, u_hg, u_fn, u_swa = _in_proj(x_parts, mods[l], norm1_g[l][None], _pack_w_in(w_in[l]))

        qg, kvg = mla_q_norm_g[l][None], mla_kv_norm_g[l][None]
        wq, wkv = _pack_w_q_up(mla_w_q_up[l]), _pack_w_kv_up(mla_w_kv_up[l])
        o_mla_c, ckv_c = _mla_ctx(u_mla, qg, wq, kvg, wkv)
        o_mla_l = _mla_lat(u_mla, cache_mla_ckv, cache_mla_krope, l, cos_m, sin_m, qg, wq, kvg, wkv)

        lbf, lbb = lb[0, l][None], lb[1, l][None]
        ng4 = jnp.tile(hg_norm_g[l], HG_HEADS)[None]
        o_hg_c, st_c = _hgrn(u_hg, lbf, lbb, ng4, None, seq=SEQ, n_batch=BATCH, row_block0=0)
        o_hg_l = _hgrn(u_hg, lbf, lbb, ng4, state_t[:, l], seq=DEC_SEQ, n_batch=DEC_BATCH,
                       row_block0=ctx_blk_lat)

        fw = fn_w[l].astype(BF16)
        o_fn_c = _fourier(u_fn, fw, seq=SEQ, n_batch=BATCH, row_block0=0)
        o_fn_l = _fourier(u_fn, fw, seq=DEC_SEQ, n_batch=DEC_BATCH, row_block0=ctx_blk_lat)

        sink = swa_sink[l]
        o_swa_c = _swa_ctx(u_swa, sink)
        o_swa_l = _swa_lat(u_swa, cache_k, cache_v, l, sink, cos_q, sin_q, cos_k, sin_k)

        x1, h2, gate_t, rank_t, cnt = _out_proj(
            x_parts, ((o_mla_c, o_mla_l), (o_hg_c, o_hg_l), (o_fn_c, o_fn_l), (o_swa_c, o_swa_l)),
            mods[l], norm2_g[l][None], _pack_w_out(w_out[l]),
            moe_w_router[l].T, moe_b_router[l][:, None],
            sh_w_gate[l].astype(BF16), sh_w_up[l].astype(BF16), sh_w_down[l].astype(BF16))
        plan = _segment_plan(cnt)
        xs = _dispatch(h2, rank_t, plan)
        ys = _experts(xs, plan, moe_w_gate, moe_w_up, moe_w_down, l)
        if l < DEPTH - 1:
            x_parts = (_combine(x1, ys, gate_t, rank_t, mods[l], plan, None),)
        else:
            y_prompt, y_sample = _combine(x1, ys, gate_t, rank_t, mods[l], plan, final_norm_g[None])

        new_ckv.append(ckv_c.reshape(BATCH, SEQ, MLA_KV_LORA))
        new_kr.append(u_mla[:N_CTX, MLA_Q_LORA + MLA_KV_LORA:MLA_IN].reshape(BATCH, SEQ, MLA_ROPE))
        new_k.append(u_swa[:N_CTX, SWA_QW:SWA_QW + SWA_KW].reshape(BATCH, SEQ, SWA_KV_HEADS, SWA_HD))
        new_v.append(u_swa[:N_CTX, SWA_QW + SWA_KW:].reshape(BATCH, SEQ, SWA_KV_HEADS, SWA_HD))
        new_st.append(jnp.swapaxes(st_c, -1, -2))

    y_prompt = y_prompt.reshape(BATCH, SEQ, D_MODEL)
    y_sample = y_sample.reshape(DEC_BATCH, DEC_SEQ, D_MODEL)
    stack = lambda xs: jnp.stack(xs, axis=1)
    return (y_prompt, y_sample, stack(new_ckv), stack(new_kr), stack(new_k), stack(new_v), stack(new_st))
```

```python
import functools

import numpy as np
import jax
import jax.numpy as jnp
from jax import lax
from jax.experimental import pallas as pl
from jax.experimental.pallas import tpu as pltpu

F32 = jnp.float32
BF16 = jnp.bfloat16

D_MODEL = 1024
BATCH = 32
SEQ = 256
DEPTH = 2
DEC_BATCH = 2
DEC_SEQ = 1024
PAST_LEN = 256
GRID_W = 64
EPS = 1e-6
ROPE_BASE = 10000.0
NEG_INF = -1e30

MLA_HEADS = 4
MLA_NOPE = 64
MLA_ROPE = 32
MLA_V = 64
MLA_Q_LORA = 256
MLA_KV_LORA = 128
HG_HEADS = 4
HG_DK = 64
HG_DV = 64
HG_W = HG_HEADS * HG_DK
FN_GROUPS = 4
FN_WIDTH = 256
SWA_HEADS = 4
SWA_KV_HEADS = 2
SWA_HD = 64
WINDOW = 128
N_EXPERTS = 64
TOP_K = 6
D_EXPERT = 256
D_SHARED = 256
ROUTE_SCALE = 2.5

MLA_IN = MLA_Q_LORA + MLA_KV_LORA + MLA_ROPE
HG_IN = 3 * HG_HEADS * HG_DK + 2 * HG_HEADS * HG_DV
FN_IN = FN_WIDTH
SWA_IN = (SWA_HEADS + 2 * SWA_KV_HEADS) * SWA_HD

N_CTX = BATCH * SEQ
N_LAT = DEC_BATCH * DEC_SEQ
N_TOK = N_CTX + N_LAT

MLA_PACK = 512
U_COLS = MLA_PACK + HG_IN + FN_IN + SWA_IN

ROW_TILE = 256
CTX_TILES = N_CTX // ROW_TILE
LAT_TILES_PER_BATCH = DEC_SEQ // ROW_TILE
HG_CHUNK = 32
HG_BLOCK = 256
SWA_QBLK = 128
MLA_QBLK = 256
CTX_PER_STEP = 4
HG_CTX_PER_STEP = 2
VMEM_LIMIT = 56 * 1024 * 1024


def _dot(a, b):
    return jnp.dot(a, b, preferred_element_type=F32)


def _dot_nt(a, b):
    return lax.dot_general(a, b, (((1,), (1,)), ((), ())), preferred_element_type=F32)


def _dot_tn(a, b):
    return lax.dot_general(a, b, (((0,), (0,)), ((), ())), preferred_element_type=F32)


def _split3(x):
    hi = x.astype(BF16)
    r1 = x - hi.astype(F32)
    mid = r1.astype(BF16)
    return hi, mid, (r1 - mid.astype(F32)).astype(BF16)


def _dot_exact_lhs(a, b):
    ab = a.astype(BF16)
    hi, mid, lo = _split3(b)
    return (_dot(ab, lo) + _dot(ab, mid)) + _dot(ab, hi)


def _dot_exact_rhs(a, b):
    bb = b.astype(BF16)
    hi, mid, lo = _split3(a)
    return (_dot(lo, bb) + _dot(mid, bb)) + _dot(hi, bb)


def _rms(x, g):
    return x * lax.rsqrt(jnp.mean(x * x, axis=-1, keepdims=True) + EPS) * g


def _silu(x):
    return x * jax.nn.sigmoid(x)


def _mod_row(i):
    return jnp.where(i < CTX_TILES, 0, 1 + (i - CTX_TILES) // LAT_TILES_PER_BATCH)


def _params(*sem):
    return pltpu.CompilerParams(dimension_semantics=sem, vmem_limit_bytes=VMEM_LIMIT)


ADA_COLS = 1536


def _ada_kernel(cv_ref, w_ref, b_ref, o_ref):
    a = _silu(cv_ref[...]).astype(BF16)
    o_ref[...] = _dot(a, w_ref[...].astype(BF16)) + b_ref[...]


def _ada(cv8, w_ada, b_ada):
    return pl.pallas_call(
        _ada_kernel,
        grid=(DEPTH, 6 * D_MODEL // ADA_COLS),
        in_specs=[
            pl.BlockSpec((8, D_MODEL), lambda l, j: (0, 0)),
            pl.BlockSpec((None, D_MODEL, ADA_COLS), lambda l, j: (l, 0, j)),
            pl.BlockSpec((None, 1, ADA_COLS), lambda l, j: (l, 0, j)),
        ],
        out_specs=pl.BlockSpec((None, 8, ADA_COLS), lambda l, j: (l, 0, j)),
        out_shape=jax.ShapeDtypeStruct((DEPTH, 8, 6 * D_MODEL), F32),
        compiler_params=_params("arbitrary", "arbitrary"),
        name="ada",
    )(cv8, w_ada, b_ada.reshape(DEPTH, 1, 6 * D_MODEL))


def _ctx_lat_specs(width):
    return [pl.BlockSpec((ROW_TILE, width), lambda i, *_: (jnp.minimum(i, CTX_TILES - 1), 0)),
            pl.BlockSpec((ROW_TILE, width), lambda i, *_: (jnp.maximum(i - CTX_TILES, 0), 0))]


def _row_specs(parts, width):
    return _ctx_lat_specs(width) if len(parts) == 2 else [pl.BlockSpec((ROW_TILE, width), lambda i, *_: (i, 0))]


def _read_rows(refs):
    if len(refs) == 1:
        return refs[0][...]
    return jnp.where(pl.program_id(0) < CTX_TILES, refs[0][...], refs[1][...])


def _in_kernel(*refs, n_x):
    x = _read_rows(refs[:n_x])
    mod_ref, g_ref, w_ref, umla_ref, uhg_ref, ufn_ref, uswa_ref = refs[n_x:]
    sh1 = mod_ref[:, 0:D_MODEL]
    sc1 = mod_ref[:, D_MODEL:2 * D_MODEL]
    h = _rms(x, g_ref[...]) * (1.0 + sc1) + sh1
    u = _dot(h.astype(BF16), w_ref[...])
    o = 0
    for ref, width in ((umla_ref, MLA_PACK), (uhg_ref, HG_IN), (ufn_ref, FN_IN), (uswa_ref, SWA_IN)):
        ref[...] = u[:, o:o + width].astype(ref.dtype)
        o += width


def _in_proj(x_parts, mod, g, w):
    row = lambda i: (i, 0)
    widths = (MLA_PACK, HG_IN, FN_IN, SWA_IN)
    return pl.pallas_call(
        functools.partial(_in_kernel, n_x=len(x_parts)),
        grid=(N_TOK // ROW_TILE,),
        in_specs=_row_specs(x_parts, D_MODEL) + [
            pl.BlockSpec((None, 1, 6 * D_MODEL), lambda i: (_mod_row(i), 0, 0)),
            pl.BlockSpec((1, D_MODEL), lambda i: (0, 0)),
            pl.BlockSpec((D_MODEL, U_COLS), lambda i: (0, 0))],
        out_specs=[pl.BlockSpec((ROW_TILE, wd), row) for wd in widths],
        out_shape=[jax.ShapeDtypeStruct((N_TOK, wd), F32) for wd in widths],
        compiler_params=_params("arbitrary"),
        name="in_proj",
    )(*x_parts, mod, g, w)


def _rope(x, cos, sin_signed):
    lane = lax.broadcasted_iota(jnp.int32, x.shape, 1)
    width = x.shape[1]
    swapped = jnp.where(lane % 2 == 0, pltpu.roll(x, width - 1, 1), pltpu.roll(x, 1, 1))
    return x * cos + swapped * sin_signed


def _stack_heads(x, n_heads, head_w):
    lane = lax.broadcasted_iota(jnp.int32, x.shape, 1)
    return jnp.concatenate([jnp.where(lane // head_w == h, x, 0.0) for h in range(n_heads)], axis=0)


def _unstack_heads(o, n_heads, head_w):
    t = o.shape[0] // n_heads
    lane = lax.broadcasted_iota(jnp.int32, (t, o.shape[1]), 1)
    out = jnp.zeros((t, o.shape[1]), F32)
    for h in range(n_heads):
        out = jnp.where(lane // head_w == h, o[h * t:(h + 1) * t], out)
    return out


MLA_SCALE = (MLA_NOPE + MLA_ROPE) ** -0.5
MLA_QW = MLA_HEADS * MLA_NOPE + MLA_HEADS * MLA_ROPE
MLA_NW = MLA_HEADS * MLA_NOPE


def _mla_attend(q, kcat, v):
    qs = jnp.concatenate([_stack_heads(q[:, :MLA_NW], MLA_HEADS, MLA_NOPE),
                          _stack_heads(q[:, MLA_NW:], MLA_HEADS, MLA_ROPE)], axis=1)
    s = _dot_nt(qs.astype(BF16), kcat) * MLA_SCALE
    p = jnp.exp(s - jnp.max(s, axis=-1, keepdims=True))
    o = _dot(p.astype(BF16), v) / jnp.sum(p, axis=-1, keepdims=True)
    return _unstack_heads(o, MLA_HEADS, MLA_V)


def _mla_ctx_kernel(u_ref, qg_ref, wq_ref, kvg_ref, wkv_ref, o_ref, ckv_ref):
    for j in range(CTX_PER_STEP):
        rows = slice(j * SEQ, (j + 1) * SEQ)
        u = u_ref[rows, :]
        q = _dot(_rms(u[:, :MLA_Q_LORA], qg_ref[...]).astype(BF16), wq_ref[...])
        ckv = _rms(u[:, MLA_Q_LORA:MLA_Q_LORA + MLA_KV_LORA], kvg_ref[...])
        ckv_ref[rows, :] = ckv
        kv = _dot(ckv.astype(BF16), wkv_ref[...])
        kr4 = u[:, MLA_Q_LORA + MLA_KV_LORA:]
        kcat = jnp.concatenate([kv[:, :MLA_NW], kr4], axis=1).astype(BF16)
        o_ref[rows, :] = _mla_attend(q, kcat, kv[:, MLA_NW:].astype(BF16)).astype(o_ref.dtype)


def _mla_ctx(u_mla, qg, wq, kvg, wkv):
    full = lambda shape: pl.BlockSpec(shape, lambda b: (0, 0))
    rows = CTX_PER_STEP * SEQ
    return pl.pallas_call(
        _mla_ctx_kernel,
        grid=(BATCH // CTX_PER_STEP,),
        in_specs=[pl.BlockSpec((rows, MLA_PACK), lambda b: (b, 0)),
                  full((1, MLA_Q_LORA)), full((MLA_Q_LORA, MLA_QW)),
                  full((1, MLA_KV_LORA)), full((MLA_KV_LORA, 2 * MLA_NW))],
        out_specs=[pl.BlockSpec((rows, MLA_NW), lambda b: (b, 0)),
                   pl.BlockSpec((rows, MLA_KV_LORA), lambda b: (b, 0))],
        out_shape=[jax.ShapeDtypeStruct((N_CTX, MLA_NW), BF16),
                   jax.ShapeDtypeStruct((N_CTX, MLA_KV_LORA), F32)],
        compiler_params=_params("arbitrary"),
        name="mla_ctx",
    )(u_mla, qg, wq, kvg, wkv)


MLA_TK = PAST_LEN + DEC_SEQ


def _mla_lat_kernel(u_ref, cckv_ref, ckr_ref, cos_ref, sin_ref, qg_ref, wq_ref, kvg_ref, wkv_ref,
                    o_ref, kcat_s, v_s):
    i = pl.program_id(1)

    @pl.when(i == 0)
    def _():
        u = u_ref[...]
        ckv_new = _rms(u[:, MLA_Q_LORA:MLA_Q_LORA + MLA_KV_LORA], kvg_ref[...])
        ckv_all = jnp.concatenate([cckv_ref[...], ckv_new], axis=0)
        kv = _dot(ckv_all.astype(BF16), wkv_ref[...])
        kr_new = _rope(u[:, MLA_Q_LORA + MLA_KV_LORA:], cos_ref[...], sin_ref[...])
        ckr = ckr_ref[...]
        kr_all = jnp.concatenate([jnp.concatenate([ckr] * MLA_HEADS, axis=1), kr_new], axis=0)
        kcat_s[...] = jnp.concatenate([kv[:, :MLA_NW], kr_all], axis=1).astype(BF16)
        v_s[...] = kv[:, MLA_NW:].astype(BF16)

    r0 = pl.multiple_of(i * MLA_QBLK, MLA_QBLK)
    cq = u_ref[pl.ds(r0, MLA_QBLK), 0:MLA_Q_LORA]
    q = _dot(_rms(cq, qg_ref[...]).astype(BF16), wq_ref[...])
    qr = _rope(q[:, MLA_NW:], cos_ref[pl.ds(r0, MLA_QBLK), :], sin_ref[pl.ds(r0, MLA_QBLK), :])
    q = jnp.concatenate([q[:, :MLA_NW], qr], axis=1)
    o_ref[...] = _mla_attend(q, kcat_s[...], v_s[...]).astype(o_ref.dtype)


def _mla_lat(u_mla, cache_ckv, cache_kr, l, cos, sin, qg, wq, kvg, wkv):
    full = lambda shape: pl.BlockSpec(shape, lambda b, i: (0, 0))
    nq = DEC_SEQ // MLA_QBLK
    return pl.pallas_call(
        _mla_lat_kernel,
        grid=(DEC_BATCH, nq),
        in_specs=[pl.BlockSpec((DEC_SEQ, MLA_PACK), lambda b, i: (N_CTX // DEC_SEQ + b, 0)),
                  pl.BlockSpec((None, None, PAST_LEN, MLA_KV_LORA), lambda b, i: (b, l, 0, 0)),
                  pl.BlockSpec((None, None, PAST_LEN, MLA_ROPE), lambda b, i: (b, l, 0, 0)),
                  full((DEC_SEQ, MLA_HEADS * MLA_ROPE)), full((DEC_SEQ, MLA_HEADS * MLA_ROPE)),
                  full((1, MLA_Q_LORA)), full((MLA_Q_LORA, MLA_QW)),
                  full((1, MLA_KV_LORA)), full((MLA_KV_LORA, 2 * MLA_NW))],
        out_specs=pl.BlockSpec((MLA_QBLK, MLA_NW), lambda b, i: (b * nq + i, 0)),
        out_shape=jax.ShapeDtypeStruct((N_LAT, MLA_NW), BF16),
        scratch_shapes=[pltpu.VMEM((MLA_TK, MLA_QW), BF16), pltpu.VMEM((MLA_TK, MLA_NW), BF16)],
        compiler_params=_params("arbitrary", "arbitrary"),
        name="mla_lat",
    )(u_mla, cache_ckv, cache_kr, cos, sin, qg, wq, kvg, wkv)


def _hgrn_kernel(*refs, seq, n_seq, has_state):
    if has_state:
        (u_ref, lbf_ref, lbb_ref, ng_ref, s0_ref, o_ref,
         q_s, kf_s, gf_s, kb_s, gb_s, of_s, ob_s, stf_s, stb_s) = refs
    else:
        (u_ref, lbf_ref, lbb_ref, ng_ref, o_ref, so_ref,
         q_s, kf_s, gf_s, kb_s, gb_s, of_s, ob_s, stf_s, stb_s) = refs
    C = HG_CHUNK
    W = HG_W

    q_s[...] = _silu(u_ref[:, 0:W])
    ff = lbf_ref[...] + (1.0 - lbf_ref[...]) * jax.nn.sigmoid(u_ref[:, W:2 * W])
    kf_s[...] = 1.0 - ff
    gf_s[...] = jnp.log(ff)
    fb = lbb_ref[...] + (1.0 - lbb_ref[...]) * jax.nn.sigmoid(u_ref[:, 2 * W:3 * W])
    kb_s[...] = 1.0 - fb
    gb_s[...] = jnp.log(fb)

    rr = lax.broadcasted_iota(jnp.int32, (W, W), 0)
    cc = lax.broadcasted_iota(jnp.int32, (W, W), 1)
    blockdiag = rr // HG_DK == cc // HG_DK
    if has_state:
        for st, d in ((stf_s, 0), (stb_s, 1)):
            rows = []
            for h in range(HG_HEADS):
                z = lambda n: jnp.zeros((HG_DV, n * HG_DK), F32)
                parts = ([z(h)] if h else []) + [s0_ref[d, h]] + ([z(HG_HEADS - 1 - h)] if h < HG_HEADS - 1 else [])
                rows.append(jnp.concatenate(parts, axis=1) if len(parts) > 1 else parts[0])
            st[0] = jnp.concatenate(rows, axis=0)
    else:
        stf_s[...] = jnp.zeros_like(stf_s)
        stb_s[...] = jnp.zeros_like(stb_s)

    B = HG_BLOCK
    per_block = B // C
    n_blocks = seq // B
    ri = lax.broadcasted_iota(jnp.int32, (B, B), 0)
    ci = lax.broadcasted_iota(jnp.int32, (B, B), 1)
    same_chunk = ri // C == ci // C
    rs = lax.broadcasted_iota(jnp.int32, (HG_HEADS * B, B), 0) % B
    cs = lax.broadcasted_iota(jnp.int32, (HG_HEADS * B, B), 1)
    same_chunk_s = rs // C == cs // C

    def sums(to_end, to_mid):
        f = lambda m: jnp.where(same_chunk & m, 1.0, 0.0)
        whole = jnp.where(same_chunk, 1.0, 0.0)
        return jnp.concatenate([f(to_end), f(to_end) - f(to_mid), whole - f(to_end)], axis=0)

    mid_f = (ri // C) * C + (C // 2 - 1)
    mid_b = (ri // C) * C + C // 2
    sums_f = sums(ci <= ri, ci <= mid_f)
    sums_b = sums(ci >= ri, ci >= mid_b)
    keep_f = same_chunk_s & (rs >= cs)
    keep_b = same_chunk_s & (cs >= rs)

    def block(r, k_s, g_s, o_s, st_s, sum_mat, keep, order):
        q = q_s[pl.ds(r, B), :]
        k = k_s[pl.ds(r, B), :]
        v = u_ref[pl.ds(r, B), 3 * W:4 * W].astype(BF16)
        gs = _dot_exact_lhs(sum_mat, g_s[pl.ds(r, B), :])
        G, Gq, Gk2 = gs[:B], gs[B:2 * B], gs[2 * B:]
        qe = _stack_heads(q * jnp.exp(Gq), HG_HEADS, HG_DK)
        ke = k * jnp.exp(-Gq)
        A = jnp.where(keep, _dot_nt(qe.astype(BF16), ke.astype(BF16)), 0.0)
        o_intra = _unstack_heads(_dot(A.astype(BF16), v), HG_HEADS, HG_DV)
        qg = (q * jnp.exp(G)).astype(BF16)
        k2 = (k * jnp.exp(Gk2)).astype(BF16)
        decay = jnp.exp(G + Gk2)
        st = st_s[...]
        o_inter = [None] * per_block
        for c in order:
            rows = slice(c * C, (c + 1) * C)
            o_inter[c] = _dot_nt(qg[rows], st.astype(BF16))
            st = st * decay[c * C:c * C + 1] + jnp.where(blockdiag, _dot_tn(v[rows], k2[rows]), 0.0)
        st_s[...] = st
        o_s[pl.ds(r, B), :] = o_intra + jnp.concatenate(o_inter, axis=0)

    def fwd(j, r):
        block(r, kf_s, gf_s, of_s, stf_s.at[j], sums_f, keep_f, range(per_block))

    def bwd(j, r):
        block(r, kb_s, gb_s, ob_s, stb_s.at[j], sums_b, keep_b, range(per_block - 1, -1, -1))

    for j in range(n_seq):
        if n_blocks == 1:
            fwd(j, j * seq)
            bwd(j, j * seq)
        else:
            lax.fori_loop(0, n_blocks, lambda i, c, j=j: (fwd(j, pl.multiple_of(j * seq + i * B, B)), c)[1], 0)
            lax.fori_loop(0, n_blocks,
                          lambda i, c, j=j: (bwd(j, pl.multiple_of(j * seq + (n_blocks - 1 - i) * B, B)), c)[1], 0)

    o = of_s[...] + ob_s[...]
    ms = _dot_exact_rhs(o * o, jnp.where(blockdiag, 1.0 / HG_DV, 0.0))
    on = o * lax.rsqrt(ms + EPS) * ng_ref[...]
    o_ref[...] = (on * _silu(u_ref[:, 4 * W:5 * W])).astype(o_ref.dtype)

    if not has_state:
        for j in range(n_seq):
            for st, d in ((stf_s, 0), (stb_s, 1)):
                for h in range(HG_HEADS):
                    so_ref[j, d, h] = st[j, h * HG_DV:(h + 1) * HG_DV, h * HG_DK:(h + 1) * HG_DK]


def _hgrn(u_hg, lbf, lbb, ng4, state_t, *, seq, n_batch, row_block0):
    has_state = state_t is not None
    n_seq = 1 if has_state else HG_CTX_PER_STEP
    rows = n_seq * seq
    full = lambda shape: pl.BlockSpec(shape, lambda b: (0, 0))
    in_specs = [pl.BlockSpec((rows, HG_IN), lambda b: (row_block0 + b, 0)),
                full((1, HG_W)), full((1, HG_W)), full((1, HG_W))]
    args = [u_hg, lbf, lbb, ng4]
    o_spec = pl.BlockSpec((rows, HG_W), lambda b: (b, 0))
    o_shape = jax.ShapeDtypeStruct((n_batch * seq, HG_W), BF16)
    if has_state:
        in_specs.append(pl.BlockSpec((None, 2, HG_HEADS, HG_DV, HG_DK), lambda b: (b, 0, 0, 0, 0)))
        args.append(state_t)
        out_specs, out_shape = o_spec, o_shape
    else:
        out_specs = [o_spec, pl.BlockSpec((n_seq, 2, HG_HEADS, HG_DV, HG_DK), lambda b: (b, 0, 0, 0, 0))]
        out_shape = [o_shape, jax.ShapeDtypeStruct((n_batch, 2, HG_HEADS, HG_DV, HG_DK), F32)]
    return pl.pallas_call(
        functools.partial(_hgrn_kernel, seq=seq, n_seq=n_seq, has_state=has_state),
        grid=(n_batch // n_seq,),
        in_specs=in_specs, out_specs=out_specs, out_shape=out_shape,
        scratch_shapes=[pltpu.VMEM((rows, HG_W), F32)] * 7 + [pltpu.VMEM((n_seq, HG_W, HG_W), F32)] * 2,
        compiler_params=_params("arbitrary"),
        name="hgrn_lat" if has_state else "hgrn_ctx",
    )(*args)


def _dft_tables(n):
    j = np.arange(n, dtype=np.int64)
    ang = 2.0 * np.pi * ((j[:, None] * j[None, :]) % n).astype(np.float64) / n
    return np.cos(ang) / np.sqrt(n), np.sin(ang) / np.sqrt(n)


def _fourier_tables(seq):
    gw = FN_WIDTH // FN_GROUPS
    cg, sg = _dft_tables(gw)
    eye = np.eye(FN_GROUPS)
    chan = np.concatenate([np.kron(eye, cg), np.kron(eye, sg)], axis=1)
    ct, st = _dft_tables(seq)
    pos = np.concatenate([ct, -st], axis=1)
    return jnp.asarray(chan, F32).astype(BF16), jnp.asarray(pos, F32).astype(BF16)


def _fourier_kernel(x_ref, chan_ref, pos_ref, w_ref, o_ref):
    x12 = _dot(x_ref[...].astype(BF16), chan_ref[...])
    z = jnp.concatenate([x12[:, :FN_WIDTH], x12[:, FN_WIDTH:]], axis=0).astype(BF16)
    y = _dot(pos_ref[...], z)
    o_ref[...] = _dot(y.astype(BF16), w_ref[...]).astype(o_ref.dtype)


def _fourier(u_fn, w, *, seq, n_batch, row_block0):
    chan, pos = _fourier_tables(seq)
    full = lambda shape: pl.BlockSpec(shape, lambda b: (0, 0))
    return pl.pallas_call(
        _fourier_kernel,
        grid=(n_batch,),
        in_specs=[pl.BlockSpec((seq, FN_WIDTH), lambda b: (row_block0 + b, 0)),
                  full((FN_WIDTH, 2 * FN_WIDTH)), full((seq, 2 * seq)), full((FN_WIDTH, FN_WIDTH))],
        out_specs=pl.BlockSpec((seq, FN_WIDTH), lambda b: (b, 0)),
        out_shape=jax.ShapeDtypeStruct((n_batch * seq, FN_WIDTH), BF16),
        compiler_params=_params("arbitrary"),
        name="fourier",
    )(u_fn, chan, pos, w)


SWA_SCALE = SWA_HD ** -0.5
SWA_QW = SWA_HEADS * SWA_HD
SWA_KW = SWA_KV_HEADS * SWA_HD
SWA_STACK_ORDER = (0, 2, 1, 3)


def _swa_stack_q(q):
    return jnp.concatenate([_stack_heads(q[:, :SWA_KW], SWA_KV_HEADS, SWA_HD),
                            _stack_heads(q[:, SWA_KW:], SWA_KV_HEADS, SWA_HD)], axis=0)


def _swa_unstack_o(o):
    t = o.shape[0] // SWA_HEADS
    return jnp.concatenate([_unstack_heads(o[:2 * t], SWA_KV_HEADS, SWA_HD),
                            _unstack_heads(o[2 * t:], SWA_KV_HEADS, SWA_HD)], axis=1)


def _sink_rows(sink_ref, t):
    return jnp.concatenate([jnp.full((t, 1), sink_ref[h], F32) for h in SWA_STACK_ORDER], axis=0)


def _swa_ctx_kernel(sink_ref, u_ref, o_ref):
    sink = _sink_rows(sink_ref, SEQ)
    for j in range(CTX_PER_STEP):
        rows = slice(j * SEQ, (j + 1) * SEQ)
        u = u_ref[rows, :]
        qs = _swa_stack_q(u[:, :SWA_QW]).astype(BF16)
        k = u[:, SWA_QW:SWA_QW + SWA_KW].astype(BF16)
        v = u[:, SWA_QW + SWA_KW:].astype(BF16)
        s = _dot_nt(qs, k) * SWA_SCALE
        m = jnp.maximum(jnp.max(s, axis=-1, keepdims=True), sink)
        p = jnp.exp(s - m)
        denom = jnp.sum(p, axis=-1, keepdims=True) + jnp.exp(sink - m)
        o_ref[rows, :] = _swa_unstack_o(_dot(p.astype(BF16), v) / denom).astype(o_ref.dtype)


def _swa_ctx(u_swa, sink):
    rows = CTX_PER_STEP * SEQ
    return pl.pallas_call(
        _swa_ctx_kernel,
        grid=(BATCH // CTX_PER_STEP,),
        in_specs=[pl.BlockSpec(memory_space=pltpu.SMEM),
                  pl.BlockSpec((rows, SWA_IN), lambda b: (b, 0))],
        out_specs=pl.BlockSpec((rows, SWA_QW), lambda b: (b, 0)),
        out_shape=jax.ShapeDtypeStruct((N_CTX, SWA_QW), BF16),
        compiler_params=_params("arbitrary"),
        name="swa_ctx",
    )(sink, u_swa)


SWA_PAD = DEC_SEQ + 2 * SWA_QBLK


def _swa_lat_kernel(sink_ref, u_ref, kc_ref, vc_ref, cosq_ref, sinq_ref, cosk_ref, sin_k_ref,
                    o_ref, k_s, v_s):
    i = pl.program_id(1)
    B = SWA_QBLK

    @pl.when(i == 0)
    def _():
        zeros = jnp.zeros((B, SWA_KW), BF16)
        k = _rope(u_ref[:, SWA_QW:SWA_QW + SWA_KW], cosk_ref[...], sin_k_ref[...]).astype(BF16)
        k_s[...] = jnp.concatenate([zeros, k, zeros], axis=0)
        v_s[...] = jnp.concatenate([zeros, u_ref[:, SWA_QW + SWA_KW:].astype(BF16), zeros], axis=0)

    r0 = pl.multiple_of(i * B, B)
    q = _rope(u_ref[pl.ds(r0, B), 0:SWA_QW], cosq_ref[pl.ds(r0, B), :], sinq_ref[pl.ds(r0, B), :])
    qs = _swa_stack_q(q).astype(BF16)
    s_loc = _dot_nt(qs, k_s[pl.ds(r0, 3 * B), :]) * SWA_SCALE
    row = lax.broadcasted_iota(jnp.int32, s_loc.shape, 0) % B
    col = lax.broadcasted_iota(jnp.int32, s_loc.shape, 1)
    kpos = r0 - B + col
    valid = (jnp.abs(row + B - col) <= WINDOW) & (kpos >= 0) & (kpos < DEC_SEQ)
    s_loc = jnp.where(valid, s_loc, NEG_INF)
    s_ctx = _dot_nt(qs, kc_ref[...].astype(BF16)) * SWA_SCALE
    sink = _sink_rows(sink_ref, B)
    m = jnp.maximum(jnp.maximum(jnp.max(s_loc, axis=-1, keepdims=True),
                                jnp.max(s_ctx, axis=-1, keepdims=True)), sink)
    p_loc = jnp.exp(s_loc - m)
    p_ctx = jnp.exp(s_ctx - m)
    denom = (jnp.sum(p_loc, axis=-1, keepdims=True) + jnp.sum(p_ctx, axis=-1, keepdims=True)
             + jnp.exp(sink - m))
    o = _dot(p_loc.astype(BF16), v_s[pl.ds(r0, 3 * B), :]) + _dot(p_ctx.astype(BF16), vc_ref[...].astype(BF16))
    o_ref[...] = _swa_unstack_o(o / denom).astype(o_ref.dtype)


def _swa_lat(u_swa, cache_k, cache_v, l, sink, cosq, sinq, cosk, sink_k):
    full = lambda shape: pl.BlockSpec(shape, lambda b, i: (0, 0))
    nq = DEC_SEQ // SWA_QBLK
    cache_spec = pl.BlockSpec((None, None, PAST_LEN, SWA_KW), lambda b, i: (b, l, 0, 0))
    return pl.pallas_call(
        _swa_lat_kernel,
        grid=(DEC_BATCH, nq),
        in_specs=[pl.BlockSpec(memory_space=pltpu.SMEM),
                  pl.BlockSpec((DEC_SEQ, SWA_IN), lambda b, i: (N_CTX // DEC_SEQ + b, 0)),
                  cache_spec, cache_spec,
                  full((DEC_SEQ, SWA_QW)), full((DEC_SEQ, SWA_QW)),
                  full((DEC_SEQ, SWA_KW)), full((DEC_SEQ, SWA_KW))],
        out_specs=pl.BlockSpec((SWA_QBLK, SWA_QW), lambda b, i: (b * nq + i, 0)),
        out_shape=jax.ShapeDtypeStruct((N_LAT, SWA_QW), BF16),
        scratch_shapes=[pltpu.VMEM((SWA_PAD, SWA_KW), BF16), pltpu.VMEM((SWA_PAD, SWA_KW), BF16)],
        compiler_params=_params("arbitrary", "arbitrary"),
        name="swa_lat",
    )(sink, u_swa, cache_k, cache_v, cosq, sinq, cosk, sink_k)


N_BLK = N_TOK // ROW_TILE
SEG_ALIGN = 16
LOCAL_ROWS = ROW_TILE * TOP_K + N_EXPERTS * SEG_ALIGN
LOCAL_CHUNK = 512
EXPERT_TILE = 768
SORTED_ROWS = -(-(N_TOK * TOP_K + N_BLK * N_EXPERTS * SEG_ALIGN + N_EXPERTS * (EXPERT_TILE + SEG_ALIGN))
                // EXPERT_TILE) * EXPERT_TILE
MAX_TILES = SORTED_ROWS // EXPERT_TILE
NOT_PICKED = -1.0
NO_DEST = 4095.0


def _out_kernel(*refs, n_x):
    x = _read_rows(refs[:n_x])
    mixers = refs[n_x:n_x + 8]
    (mod_ref, g_ref, wo_ref, wrt_ref, br_ref, wsg_ref, wsu_ref, wsd_ref,
     x1_ref, h_ref, gate_ref, rank_ref, cnt_ref) = refs[n_x + 8:]
    mix = jnp.zeros((ROW_TILE, D_MODEL), F32)
    for j in range(4):
        mix = mix + _dot(_read_rows(mixers[2 * j:2 * j + 2]), wo_ref[j * 256:(j + 1) * 256, :])
    g1 = mod_ref[:, 2 * D_MODEL:3 * D_MODEL]
    sh2 = mod_ref[:, 3 * D_MODEL:4 * D_MODEL]
    sc2 = mod_ref[:, 4 * D_MODEL:5 * D_MODEL]
    g2 = mod_ref[:, 5 * D_MODEL:6 * D_MODEL]
    x1 = x + g1 * mix
    h = _rms(x1, g_ref[...]) * (1.0 + sc2) + sh2
    hb = h.astype(BF16)
    h_ref[...] = hb

    logits = lax.dot_general(wrt_ref[...], h, (((1,), (1,)), ((), ())),
                             precision=lax.Precision.HIGHEST, preferred_element_type=F32)
    scores = jax.nn.sigmoid(logits)
    sel = scores + br_ref[...]
    eidx = lax.broadcasted_iota(jnp.int32, sel.shape, 0)
    gate = jnp.zeros_like(scores)
    picked = jnp.zeros_like(scores)
    for _ in range(TOP_K):
        best = jnp.max(sel, axis=0, keepdims=True)
        first = jnp.min(jnp.where(sel == best, eidx, N_EXPERTS), axis=0, keepdims=True)
        pick = eidx == first
        gate = jnp.where(pick, scores, gate)
        picked = jnp.where(pick, 1.0, picked)
        sel = jnp.where(pick, -jnp.inf, sel)
    gate = ROUTE_SCALE * gate / jnp.sum(gate, axis=0, keepdims=True)

    ti = lax.broadcasted_iota(jnp.int32, (ROW_TILE, ROW_TILE), 0)
    tj = lax.broadcasted_iota(jnp.int32, (ROW_TILE, ROW_TILE), 1)
    pb = picked.astype(BF16)
    rank = _dot(pb, jnp.where(ti < tj, 1.0, 0.0).astype(BF16))
    gate_ref[...] = gate
    rank_ref[...] = jnp.where(picked > 0.0, rank, NOT_PICKED)
    counts = _dot_nt(jnp.ones((8, ROW_TILE), BF16), pb)
    cnt_ref[...] = jnp.concatenate([counts, jnp.zeros_like(counts)], axis=1)

    hid =_silu(_dot(hb, wsg_ref[...])) * _dot(hb, wsu_ref[...])
    x1_ref[...] = x1 + g2 * _dot(hid.astype(BF16), wsd_ref[...])


def _out_proj(x_parts, mixer_pairs, mod, g, wo, wrt, br, wsg, wsu, wsd):
    row = lambda i: (i, 0)
    col = lambda i: (0, i)
    full = lambda shape: pl.BlockSpec(shape, lambda i: (0, 0))
    et_spec = pl.BlockSpec((N_EXPERTS, ROW_TILE), col)
    et_shape = jax.ShapeDtypeStruct((N_EXPERTS, N_TOK), F32)
    return pl.pallas_call(
        functools.partial(_out_kernel, n_x=len(x_parts)),
        grid=(N_BLK,),
        in_specs=_row_specs(x_parts, D_MODEL) + 4 * _ctx_lat_specs(256) + [
            pl.BlockSpec((None, 1, 6 * D_MODEL), lambda i: (_mod_row(i), 0, 0)),
            full((1, D_MODEL)), full((D_MODEL, D_MODEL)),
            full((N_EXPERTS, D_MODEL)), full((N_EXPERTS, 1)),
            full((D_MODEL, D_SHARED)), full((D_MODEL, D_SHARED)), full((D_SHARED, D_MODEL))],
        out_specs=[pl.BlockSpec((ROW_TILE, D_MODEL), row), pl.BlockSpec((ROW_TILE, D_MODEL), row),
                   et_spec, et_spec,
                   pl.BlockSpec((None, 8, 128), lambda i: (i, 0, 0))],
        out_shape=[jax.ShapeDtypeStruct((N_TOK, D_MODEL), F32),
                   jax.ShapeDtypeStruct((N_TOK, D_MODEL), BF16),
                   et_shape, et_shape,
                   jax.ShapeDtypeStruct((N_BLK, 8, 128), F32)],
        compiler_params=_params("arbitrary"),
        name="out_proj",
    )(*x_parts, *[a for pair in mixer_pairs for a in pair], mod, g, wo, wrt, br, wsg, wsu, wsd)


def _segment_plan(cnt):
    cnt = cnt[:, 0, :N_EXPERTS].astype(jnp.int32)
    seg = jnp.maximum((cnt + (SEG_ALIGN - 1)) // SEG_ALIGN, 1) * SEG_ALIGN
    local = jnp.cumsum(seg, axis=1) - seg
    total = jnp.sum(seg, axis=1)
    per_expert = jnp.sum(seg, axis=0)
    padded = (per_expert + SEG_ALIGN + (EXPERT_TILE - 1)) // EXPERT_TILE * EXPERT_TILE
    ends = jnp.cumsum(padded)
    start = ends - padded
    dest = start[None, :] + jnp.cumsum(seg, axis=0) - seg
    n_tiles = ends[-1] // EXPERT_TILE
    tiles = jnp.arange(MAX_TILES, dtype=jnp.int32)
    tile_expert = jnp.sum((ends // EXPERT_TILE)[None, :] <= jnp.minimum(tiles, n_tiles - 1)[:, None], axis=1)
    tile_expert = tile_expert.astype(jnp.int32)
    plan = dict(seg=seg, local=local, total=total.astype(jnp.int32), dest=dest.astype(jnp.int32),
                tail_start=(start + per_expert).astype(jnp.int32), tail_rows=(padded - per_expert).astype(jnp.int32),
                n_tiles=n_tiles.reshape(1).astype(jnp.int32), tile_expert=tile_expert)
    segf, localf = seg.astype(F32), local.astype(F32)
    pad_lanes = lambda a: jnp.concatenate([a, jnp.zeros_like(a)], axis=1)[:, None, :]
    plan.update(seg_row=pad_lanes(segf), local_row=pad_lanes(localf),
                seg_col=segf[:, :, None], local_col=localf[:, :, None])
    return plan


def _local_dest_digits(rank, local_start):
    dest = jnp.where(rank >= 0.0, local_start + rank, NO_DEST)
    hi = jnp.floor(dest * (1.0 / 64.0))
    return hi, dest - 64.0 * hi


def _dispatch_kernel(dest_ref, seg_ref, local_ref, total_ref, tail_start_ref, tail_rows_ref, nt_ref,
                     h_ref, rank_ref, local_col_ref, local_row_ref, seg_row_ref,
                     xs_hbm, buf, zeros, sems, zsem, usem):
    b = pl.program_id(0)
    slot = b % 2

    def wait_block(blk, s):
        n = pl.multiple_of(total_ref[blk], SEG_ALIGN)
        pltpu.make_async_copy(buf.at[s, pl.ds(0, n)], xs_hbm.at[pl.ds(0, n)], sems.at[s]).wait()

    @pl.when(b == 0)
    def _():
        zeros[...] = jnp.zeros_like(zeros)

    def unused_tiles(action):
        for k in range(-(-MAX_TILES // N_BLK)):
            t = nt_ref[0] + b + k * N_BLK

            @pl.when(t < MAX_TILES)
            def _():
                cp = pltpu.make_async_copy(zeros.at[pl.ds(0, EXPERT_TILE)],
                                           xs_hbm.at[pl.ds(pl.multiple_of(t * EXPERT_TILE, EXPERT_TILE),
                                                           EXPERT_TILE)], usem)
                cp.start() if action == "start" else cp.wait()

    unused_tiles("start")

    @pl.when(b >= 2)
    def _():
        wait_block(b - 2, slot)

    hi, lo = _local_dest_digits(rank_ref[...], local_col_ref[...])
    code = jnp.concatenate([hi, lo], axis=0).astype(BF16)
    hb = h_ref[...]
    lstart = local_row_ref[:, :N_EXPERTS]
    lend = lstart + seg_row_ref[:, :N_EXPERTS]
    for c in range(LOCAL_ROWS // LOCAL_CHUNK):
        r = (lax.broadcasted_iota(jnp.int32, (LOCAL_CHUNK, N_EXPERTS), 0) + c * LOCAL_CHUNK).astype(F32)
        member = (r >= lstart) & (r < lend)
        table = jnp.concatenate([jnp.where(member, 64.0, 0.0), jnp.where(member, 1.0, 0.0)], axis=1).astype(BF16)
        d = _dot(table, code)
        rr = (lax.broadcasted_iota(jnp.int32, (LOCAL_CHUNK, ROW_TILE), 0) + c * LOCAL_CHUNK).astype(F32)
        onehot = jnp.where(d == rr, 1.0, 0.0).astype(BF16)
        buf[slot, c * LOCAL_CHUNK:(c + 1) * LOCAL_CHUNK, :] = _dot(onehot, hb).astype(BF16)

    for e in range(N_EXPERTS):
        n = pl.multiple_of(seg_ref[b, e], SEG_ALIGN)
        src = pl.multiple_of(local_ref[b, e], SEG_ALIGN)
        dst = pl.multiple_of(dest_ref[b, e], SEG_ALIGN)
        pltpu.make_async_copy(buf.at[slot, pl.ds(src, n)], xs_hbm.at[pl.ds(dst, n)], sems.at[slot]).start()

    unused_tiles("wait")

    @pl.when(b == N_BLK - 1)
    def _():
        def tail(e):
            n = pl.multiple_of(tail_rows_ref[e], SEG_ALIGN)
            dst = pl.multiple_of(tail_start_ref[e], SEG_ALIGN)
            return pltpu.make_async_copy(zeros.at[pl.ds(0, n)], xs_hbm.at[pl.ds(dst, n)], zsem)

        for e in range(N_EXPERTS):
            tail(e).start()
        for e in range(N_EXPERTS):
            tail(e).wait()
        wait_block(b - 1, 1 - slot)
        wait_block(b, slot)


def _dispatch(h, rank_t, plan):
    blk = lambda shape, imap: pl.BlockSpec(shape, imap)
    return pl.pallas_call(
        _dispatch_kernel,
        grid_spec=pltpu.PrefetchScalarGridSpec(
            num_scalar_prefetch=7, grid=(N_BLK,),
            in_specs=[blk((ROW_TILE, D_MODEL), lambda b, *_: (b, 0)),
                      blk((N_EXPERTS, ROW_TILE), lambda b, *_: (0, b)),
                      blk((None, N_EXPERTS, 1), lambda b, *_: (b, 0, 0)),
                      blk((None, 1, 128), lambda b, *_: (b, 0, 0)),
                      blk((None, 1, 128), lambda b, *_: (b, 0, 0))],
            out_specs=pl.BlockSpec(memory_space=pl.ANY),
            scratch_shapes=[pltpu.VMEM((2, LOCAL_ROWS, D_MODEL), BF16),
                            pltpu.VMEM((EXPERT_TILE + SEG_ALIGN, D_MODEL), BF16),
                            pltpu.SemaphoreType.DMA((2,)), pltpu.SemaphoreType.DMA(()),
                            pltpu.SemaphoreType.DMA(())]),
        out_shape=jax.ShapeDtypeStruct((SORTED_ROWS, D_MODEL), BF16),
        compiler_params=_params("arbitrary"),
        name="dispatch",
    )(plan['dest'], plan['seg'], plan['local'], plan['total'], plan['tail_start'], plan['tail_rows'],
      plan['n_tiles'], h, rank_t, plan['local_col'], plan['local_row'], plan['seg_row'])


def _expert_kernel(te_ref, nt_ref, x_ref, wg_ref, wu_ref, wd_ref, y_ref, wg_s, wu_s, wd_s):
    i = pl.program_id(0)
    active = i < nt_ref[0]

    @pl.when((i == 0) | (te_ref[i] != te_ref[jnp.maximum(i - 1, 0)]))
    def _():
        wg_s[...] = wg_ref[...].astype(BF16)
        wu_s[...] = wu_ref[...].astype(BF16)
        wd_s[...] = wd_ref[...].astype(BF16)

    @pl.when(active)
    def _():
        x = x_ref[...]
        hid = _silu(_dot(x, wg_s[...])) * _dot(x, wu_s[...])
        y_ref[...] = _dot(hid.astype(BF16), wd_s[...]).astype(BF16)

    @pl.when(jnp.logical_not(active))
    def _():
        y_ref[...] = jnp.zeros_like(y_ref)


def _experts(xs, plan, w_gate, w_up, w_down, l):
    rows_in = pl.BlockSpec((EXPERT_TILE, D_MODEL), lambda i, te, nt: (jnp.minimum(i, nt[0] - 1), 0))
    rows_out = pl.BlockSpec((EXPERT_TILE, D_MODEL), lambda i, te, nt: (i, 0))
    wspec = lambda shape: pl.BlockSpec((None, None) + shape, lambda i, te, nt: (l, te[i], 0, 0))
    return pl.pallas_call(
        _expert_kernel,
        grid_spec=pltpu.PrefetchScalarGridSpec(
            num_scalar_prefetch=2, grid=(MAX_TILES,),
            in_specs=[rows_in, wspec((D_MODEL, D_EXPERT)), wspec((D_MODEL, D_EXPERT)), wspec((D_EXPERT, D_MODEL))],
            out_specs=rows_out,
            scratch_shapes=[pltpu.VMEM((D_MODEL, D_EXPERT), BF16), pltpu.VMEM((D_MODEL, D_EXPERT), BF16),
                            pltpu.VMEM((D_EXPERT, D_MODEL), BF16)]),
        out_shape=jax.ShapeDtypeStruct((SORTED_ROWS, D_MODEL), BF16),
        compiler_params=_params("arbitrary"),
        name="experts",
    )(plan['tile_expert'], plan['n_tiles'], xs, w_gate, w_up, w_down)


def _combine_kernel(*refs, final):
    refs = list(refs)
    dest_ref, seg_ref, local_ref, total_ref = refs[:4]
    x1_ref, ys_hbm, gate_ref, rank_ref, mod_ref, local_row_ref, local_col_ref, seg_col_ref = refs[4:12]
    rest = refs[12:]
    fg_ref = rest.pop(0) if final else None
    outs, (buf, sems) = rest[:-2], rest[-2:]
    b = pl.program_id(0)
    slot = b % 2

    def fetch(blk, s):
        for e in range(N_EXPERTS):
            n = pl.multiple_of(seg_ref[blk, e], SEG_ALIGN)
            src = pl.multiple_of(dest_ref[blk, e], SEG_ALIGN)
            dst = pl.multiple_of(local_ref[blk, e], SEG_ALIGN)
            pltpu.make_async_copy(ys_hbm.at[pl.ds(src, n)], buf.at[s, pl.ds(dst, n)], sems.at[s]).start()

    @pl.when(b == 0)
    def _():
        buf[...] = jnp.zeros_like(buf)
        fetch(0, 0)

    @pl.when(b + 1 < N_BLK)
    def _():
        fetch(b + 1, 1 - slot)

    n_rows = pl.multiple_of(total_ref[b], SEG_ALIGN)
    pltpu.make_async_copy(ys_hbm.at[pl.ds(0, n_rows)], buf.at[slot, pl.ds(0, n_rows)], sems.at[slot]).wait()

    gate = gate_ref[...].T
    hi, lo = _local_dest_digits(rank_ref[...].T, local_row_ref[:, :N_EXPERTS])
    code = jnp.concatenate([hi, lo], axis=1).astype(BF16)
    gb = gate.astype(BF16)
    lstart = local_col_ref[...]
    lend = lstart + seg_col_ref[...]
    routed = jnp.zeros((ROW_TILE, D_MODEL), F32)
    for c in range(LOCAL_ROWS // LOCAL_CHUNK):
        r = (lax.broadcasted_iota(jnp.int32, (N_EXPERTS, LOCAL_CHUNK), 1) + c * LOCAL_CHUNK).astype(F32)
        member = (r >= lstart) & (r < lend)
        ones = jnp.where(member, 1.0, 0.0)
        table = jnp.concatenate([ones * 64.0, ones], axis=0).astype(BF16)
        d = _dot(code, table)
        gx = _dot(gb, ones.astype(BF16))
        rr = (lax.broadcasted_iota(jnp.int32, (ROW_TILE, LOCAL_CHUNK), 1) + c * LOCAL_CHUNK).astype(F32)
        weights = jnp.where(d == rr, gx, 0.0).astype(BF16)
        routed = routed + _dot(weights, buf[slot, c * LOCAL_CHUNK:(c + 1) * LOCAL_CHUNK, :])
    x = x1_ref[...] + mod_ref[:, 5 * D_MODEL:6 * D_MODEL] * routed
    if final:
        y = _rms(x, fg_ref[...])

        @pl.when(b < CTX_TILES)
        def _():
            outs[0][...] = y

        @pl.when(b >= CTX_TILES)
        def _():
            outs[1][...] = y
    else:
        outs[0][...] = x


def _combine(x1, ys, gate_t, rank_t, mod, plan, final_g):
    final = final_g is not None
    blk = lambda shape, imap: pl.BlockSpec(shape, imap)
    xspec = blk((ROW_TILE, D_MODEL), lambda b, *_: (b, 0))
    if final:
        out_specs = _ctx_lat_specs(D_MODEL)
        out_shape = [jax.ShapeDtypeStruct((N_CTX, D_MODEL), F32), jax.ShapeDtypeStruct((N_LAT, D_MODEL), F32)]
    else:
        out_specs, out_shape = xspec, jax.ShapeDtypeStruct((N_TOK, D_MODEL), F32)
    et_spec = blk((N_EXPERTS, ROW_TILE), lambda b, *_: (0, b))
    in_specs = [xspec, pl.BlockSpec(memory_space=pl.ANY), et_spec, et_spec,
                blk((None, 1, 6 * D_MODEL), lambda b, *_: (_mod_row(b), 0, 0)),
                blk((None, 1, 128), lambda b, *_: (b, 0, 0)),
                blk((None, N_EXPERTS, 1), lambda b, *_: (b, 0, 0)),
                blk((None, N_EXPERTS, 1), lambda b, *_: (b, 0, 0))]
    args = [x1, ys, gate_t, rank_t, mod, plan['local_row'], plan['local_col'], plan['seg_col']]
    if final:
        in_specs.append(blk((1, D_MODEL), lambda b, *_: (0, 0)))
        args.append(final_g)
    return pl.pallas_call(
        functools.partial(_combine_kernel, final=final),
        grid_spec=pltpu.PrefetchScalarGridSpec(
            num_scalar_prefetch=4, grid=(N_BLK,),
            in_specs=in_specs, out_specs=out_specs,
            scratch_shapes=[pltpu.VMEM((2, LOCAL_ROWS, D_MODEL), BF16), pltpu.SemaphoreType.DMA((2,))]),
        out_shape=out_shape,
        compiler_params=_params("arbitrary"),
        name="combine",
    )(plan['dest'], plan['seg'], plan['local'], plan['total'], *args)


def _rope_full_tables(dim, n_rep):
    rows = DEC_SEQ // GRID_W
    r_idx, c_idx = np.meshgrid(np.arange(rows), np.arange(GRID_W), indexing='ij')
    pos = jnp.asarray(np.stack([r_idx.reshape(-1), c_idx.reshape(-1)], axis=-1), F32)
    nf = dim // 4
    inv = ROPE_BASE ** (-jnp.arange(nf, dtype=F32) / nf)
    ang = pos[:, :, None] * inv
    ang = jnp.repeat(ang.reshape(DEC_SEQ, 2 * nf), 2, axis=1)
    sign = jnp.tile(jnp.asarray([-1.0, 1.0], F32), dim // 2)
    return jnp.tile(jnp.cos(ang), (1, n_rep)), jnp.tile(jnp.sin(ang) * sign, (1, n_rep))


def _pack_w_in(w):
    c0 = MLA_Q_LORA + MLA_KV_LORA
    kr = w[:, c0:MLA_IN]
    s0 = MLA_IN + HG_IN + FN_IN
    qh = [w[:, s0 + h * SWA_HD:s0 + (h + 1) * SWA_HD] for h in SWA_STACK_ORDER]
    return jnp.concatenate([w[:, :c0], kr, kr, kr, kr, w[:, MLA_IN:s0]] + qh
                           + [w[:, s0 + SWA_QW:]], axis=1).astype(BF16)


def _pack_w_q_up(w):
    hd = MLA_NOPE + MLA_ROPE
    nope = [w[:, h * hd:h * hd + MLA_NOPE] for h in range(MLA_HEADS)]
    rope = [w[:, h * hd + MLA_NOPE:(h + 1) * hd] for h in range(MLA_HEADS)]
    return jnp.concatenate(nope + rope, axis=1).astype(BF16)


def _pack_w_kv_up(w):
    hd = MLA_NOPE + MLA_V
    kn = [w[:, h * hd:h * hd + MLA_NOPE] for h in range(MLA_HEADS)]
    vv = [w[:, h * hd + MLA_NOPE:(h + 1) * hd] for h in range(MLA_HEADS)]
    return jnp.concatenate(kn + vv, axis=1).astype(BF16)


def _pack_w_out(w):
    s0 = 3 * 256
    rows = [w[s0 + h * SWA_HD:s0 + (h + 1) * SWA_HD] for h in SWA_STACK_ORDER]
    return jnp.concatenate([w[:s0]] + rows, axis=0).astype(BF16)


def kernel(x_prompt, x_sample, c, cache_mla_ckv, cache_mla_krope, cache_swa_k, cache_swa_v, state_hgrn,
           c_ctx, w_ada, b_ada, norm1_g, norm2_g, w_in, mla_q_norm_g, mla_w_q_up, mla_kv_norm_g, mla_w_kv_up,
           hg_lb_logits, hg_norm_g, fn_w, swa_sink, w_out, moe_w_router, moe_b_router, moe_w_gate, moe_w_up,
           moe_w_down, sh_w_gate, sh_w_up, sh_w_down, final_norm_g):
    x_parts = (x_prompt.reshape(N_CTX, D_MODEL), x_sample.reshape(N_LAT, D_MODEL))
    cv8 = jnp.concatenate([c_ctx[None, :], c, jnp.zeros((8 - 1 - DEC_BATCH, D_MODEL), F32)], axis=0)
    mods = _ada(cv8, w_ada, b_ada).reshape(DEPTH, 8, 1, 6 * D_MODEL)

    lb = jnp.cumsum(jax.nn.softmax(hg_lb_logits.astype(F32), axis=1), axis=1)
    lb = lb - lb[:, :1]

    cos_m, sin_m = _rope_full_tables(MLA_ROPE, MLA_HEADS)
    cos_q, sin_q = _rope_full_tables(SWA_HD, SWA_HEADS)
    cos_k, sin_k = cos_q[:, :SWA_KW], sin_q[:, :SWA_KW]
    cache_k = cache_swa_k.reshape(DEC_BATCH, DEPTH, PAST_LEN, SWA_KW)
    cache_v = cache_swa_v.reshape(DEC_BATCH, DEPTH, PAST_LEN, SWA_KW)
    state_t = jnp.swapaxes(state_hgrn, -1, -2)

    ctx_blk_lat = N_CTX // DEC_SEQ
    new_ckv, new_kr, new_k, new_v, new_st = [], [], [], [], []
    for l in range(DEPTH):
        u_mla, u_hg, u_fn, u_swa = _in_proj(x_parts, mods[l], norm1_g[l][None], _pack_w_in(w_in[l]))

        qg, kvg = mla_q_norm_g[l][None], mla_kv_norm_g[l][None]
        wq, wkv = _pack_w_q_up(mla_w_q_up[l]), _pack_w_kv_up(mla_w_kv_up[l])
        o_mla_c, ckv_c = _mla_ctx(u_mla, qg, wq, kvg, wkv)
        o_mla_l = _mla_lat(u_mla, cache_mla_ckv, cache_mla_krope, l, cos_m, sin_m, qg, wq, kvg, wkv)

        lbf, lbb = lb[0, l][None], lb[1, l][None]
        ng4 = jnp.tile(hg_norm_g[l], HG_HEADS)[None]
        o_hg_c, st_c = _hgrn(u_hg, lbf, lbb, ng4, None, seq=SEQ, n_batch=BATCH, row_block0=0)
        o_hg_l = _hgrn(u_hg, lbf, lbb, ng4, state_t[:, l], seq=DEC_SEQ, n_batch=DEC_BATCH,
                       row_block0=ctx_blk_lat)

        fw = fn_w[l].astype(BF16)
        o_fn_c = _fourier(u_fn, fw, seq=SEQ, n_batch=BATCH, row_block0=0)
        o_fn_l = _fourier(u_fn, fw, seq=DEC_SEQ, n_batch=DEC_BATCH, row_block0=ctx_blk_lat)

        sink = swa_sink[l]
        o_swa_c = _swa_ctx(u_swa, sink)
        o_swa_l = _swa_lat(u_swa, cache_k, cache_v, l, sink, cos_q, sin_q, cos_k, sin_k)

        x1, h2, gate_t, rank_t, cnt = _out_proj(
            x_parts, ((o_mla_c, o_mla_l), (o_hg_c, o_hg_l), (o_fn_c, o_fn_l), (o_swa_c, o_swa_l)),
            mods[l], norm2_g[l][None], _pack_w_out(w_out[l]),
            moe_w_router[l].T, moe_b_router[l][:, None],
            sh_w_gate[l].astype(BF16), sh_w_up[l].astype(BF16), sh_w_down[l].astype(BF16))
        plan = _segment_plan(cnt)
        xs = _dispatch(h2, rank_t, plan)
        ys = _experts(xs, plan, moe_w_gate, moe_w_up, moe_w_down, l)
        if l < DEPTH - 1:
            x_parts = (_combine(x1, ys, gate_t, rank_t, mods[l], plan, None),)
        else:
            y_prompt, y_sample = _combine(x1, ys, gate_t, rank_t, mods[l], plan, final_norm_g[None])

        new_ckv.append(ckv_c.reshape(BATCH, SEQ, MLA_KV_LORA))
        new_kr.append(u_mla[:N_CTX, MLA_Q_LORA + MLA_KV_LORA:MLA_IN].reshape(BATCH, SEQ, MLA_ROPE))
        new_k.append(u_swa[:N_CTX, SWA_QW:SWA_QW + SWA_KW].reshape(BATCH, SEQ, SWA_KV_HEADS, SWA_HD))
        new_v.append(u_swa[:N_CTX, SWA_QW + SWA_KW:].reshape(BATCH, SEQ, SWA_KV_HEADS, SWA_HD))
        new_st.append(jnp.swapaxes(st_c, -1, -2))

    y_prompt = y_prompt.reshape(BATCH, SEQ, D_MODEL)
    y_sample = y_sample.reshape(DEC_BATCH, DEC_SEQ, D_MODEL)
    stack = lambda xs: jnp.stack(xs, axis=1)
    return (y_prompt, y_sample, stack(new_ckv), stack(new_kr), stack(new_k), stack(new_v), stack(new_st))
```

```python
import functools

import numpy as np
import jax
import jax.numpy as jnp
from jax import lax
from jax.experimental import pallas as pl
from jax.experimental.pallas import tpu as pltpu

F32 = jnp.float32
BF16 = jnp.bfloat16

D_MODEL = 1024
BATCH = 32
SEQ = 256
DEPTH = 2
DEC_BATCH = 2
DEC_SEQ = 1024
PAST_LEN = 256
GRID_W = 64
EPS = 1e-6
ROPE_BASE = 10000.0
NEG_INF = -1e30

MLA_HEADS = 4
MLA_NOPE = 64
MLA_ROPE = 32
MLA_V = 64
MLA_Q_LORA = 256
MLA_KV_LORA = 128
HG_HEADS = 4
HG_DK = 64
HG_DV = 64
HG_W = HG_HEADS * HG_DK
FN_GROUPS = 4
FN_WIDTH = 256
SWA_HEADS = 4
SWA_KV_HEADS = 2
SWA_HD = 64
WINDOW = 128
N_EXPERTS = 64
TOP_K = 6
D_EXPERT = 256
D_SHARED = 256
ROUTE_SCALE = 2.5

MLA_IN = MLA_Q_LORA + MLA_KV_LORA + MLA_ROPE
HG_IN = 3 * HG_HEADS * HG_DK + 2 * HG_HEADS * HG_DV
FN_IN = FN_WIDTH
SWA_IN = (SWA_HEADS + 2 * SWA_KV_HEADS) * SWA_HD

N_CTX = BATCH * SEQ
N_LAT = DEC_BATCH * DEC_SEQ
N_TOK = N_CTX + N_LAT

MLA_PACK = 512
U_COLS = MLA_PACK + HG_IN + FN_IN + SWA_IN

ROW_TILE = 256
CTX_TILES = N_CTX // ROW_TILE
LAT_TILES_PER_BATCH = DEC_SEQ // ROW_TILE
HG_CHUNK = 32
HG_BLOCK = 256
SWA_QBLK = 128
MLA_QBLK = 256
CTX_PER_STEP = 4
HG_CTX_PER_STEP = 2
OUT_ROWS = 2 * ROW_TILE
VMEM_LIMIT = 56 * 1024 * 1024


def _dot(a, b):
    return jnp.dot(a, b, preferred_element_type=F32)


def _dot_nt(a, b):
    return lax.dot_general(a, b, (((1,), (1,)), ((), ())), preferred_element_type=F32)


def _dot_tn(a, b):
    return lax.dot_general(a, b, (((0,), (0,)), ((), ())), preferred_element_type=F32)


def _split3(x):
    hi = x.astype(BF16)
    r1 = x - hi.astype(F32)
    mid = r1.astype(BF16)
    return hi, mid, (r1 - mid.astype(F32)).astype(BF16)


def _dot_exact_lhs(a, b):
    ab = a.astype(BF16)
    hi, mid, lo = _split3(b)
    return (_dot(ab, lo) + _dot(ab, mid)) + _dot(ab, hi)


def _dot_exact_rhs(a, b):
    bb = b.astype(BF16)
    hi, mid, lo = _split3(a)
    return (_dot(lo, bb) + _dot(mid, bb)) + _dot(hi, bb)


def _rms(x, g):
    return x * lax.rsqrt(jnp.mean(x * x, axis=-1, keepdims=True) + EPS) * g


def _silu(x):
    return x * jax.nn.sigmoid(x)


def _mod_row(i):
    return jnp.where(i < CTX_TILES, 0, 1 + (i - CTX_TILES) // LAT_TILES_PER_BATCH)


def _params(*sem):
    return pltpu.CompilerParams(dimension_semantics=sem, vmem_limit_bytes=VMEM_LIMIT)


ADA_COLS = 1536


def _ada_kernel(cv_ref, w_ref, b_ref, o_ref):
    a = _silu(cv_ref[...]).astype(BF16)
    o_ref[...] = _dot(a, w_ref[...].astype(BF16)) + b_ref[...]


def _ada(cv8, w_ada, b_ada):
    return pl.pallas_call(
        _ada_kernel,
        grid=(DEPTH, 6 * D_MODEL // ADA_COLS),
        in_specs=[
            pl.BlockSpec((8, D_MODEL), lambda l, j: (0, 0)),
            pl.BlockSpec((None, D_MODEL, ADA_COLS), lambda l, j: (l, 0, j)),
            pl.BlockSpec((None, 1, ADA_COLS), lambda l, j: (l, 0, j)),
        ],
        out_specs=pl.BlockSpec((None, 8, ADA_COLS), lambda l, j: (l, 0, j)),
        out_shape=jax.ShapeDtypeStruct((DEPTH, 8, 6 * D_MODEL), F32),
        compiler_params=_params("arbitrary", "arbitrary"),
        name="ada",
    )(cv8, w_ada, b_ada.reshape(DEPTH, 1, 6 * D_MODEL))


def _ctx_lat_specs(width, tile=ROW_TILE):
    n_ctx = N_CTX // tile
    return [pl.BlockSpec((tile, width), lambda i, *_: (jnp.minimum(i, n_ctx - 1), 0)),
            pl.BlockSpec((tile, width), lambda i, *_: (jnp.maximum(i - n_ctx, 0), 0))]


def _row_specs(parts, width, tile=ROW_TILE):
    if len(parts) == 2:
        return _ctx_lat_specs(width, tile)
    return [pl.BlockSpec((tile, width), lambda i, *_: (i, 0))]


def _read_rows(refs, tile=ROW_TILE):
    if len(refs) == 1:
        return refs[0][...]
    return jnp.where(pl.program_id(0) < N_CTX // tile, refs[0][...], refs[1][...])


def _in_kernel(*refs, n_x):
    x = _read_rows(refs[:n_x])
    mod_ref, g_ref, w_ref, umla_ref, uhg_ref, ufn_ref, uswa_ref = refs[n_x:]
    sh1 = mod_ref[:, 0:D_MODEL]
    sc1 = mod_ref[:, D_MODEL:2 * D_MODEL]
    h = _rms(x, g_ref[...]) * (1.0 + sc1) + sh1
    u = _dot(h.astype(BF16), w_ref[...])
    o = 0
    for ref, width in ((umla_ref, MLA_PACK), (uhg_ref, HG_IN), (ufn_ref, FN_IN), (uswa_ref, SWA_IN)):
        ref[...] = u[:, o:o + width].astype(ref.dtype)
        o += width


def _in_proj(x_parts, mod, g, w):
    row = lambda i: (i, 0)
    widths = (MLA_PACK, HG_IN, FN_IN, SWA_IN)
    return pl.pallas_call(
        functools.partial(_in_kernel, n_x=len(x_parts)),
        grid=(N_TOK // ROW_TILE,),
        in_specs=_row_specs(x_parts, D_MODEL) + [
            pl.BlockSpec((None, 1, 6 * D_MODEL), lambda i: (_mod_row(i), 0, 0)),
            pl.BlockSpec((1, D_MODEL), lambda i: (0, 0)),
            pl.BlockSpec((D_MODEL, U_COLS), lambda i: (0, 0))],
        out_specs=[pl.BlockSpec((ROW_TILE, wd), row) for wd in widths],
        out_shape=[jax.ShapeDtypeStruct((N_TOK, wd), F32) for wd in widths],
        compiler_params=_params("arbitrary"),
        name="in_proj",
    )(*x_parts, mod, g, w)


def _rope(x, cos, sin_signed):
    lane = lax.broadcasted_iota(jnp.int32, x.shape, 1)
    width = x.shape[1]
    swapped = jnp.where(lane % 2 == 0, pltpu.roll(x, width - 1, 1), pltpu.roll(x, 1, 1))
    return x * cos + swapped * sin_signed


def _stack_heads(x, n_heads, head_w):
    lane = lax.broadcasted_iota(jnp.int32, x.shape, 1)
    return jnp.concatenate([jnp.where(lane // head_w == h, x, 0.0) for h in range(n_heads)], axis=0)


def _unstack_heads(o, n_heads, head_w):
    t = o.shape[0] // n_heads
    lane = lax.broadcasted_iota(jnp.int32, (t, o.shape[1]), 1)
    out = jnp.zeros((t, o.shape[1]), F32)
    for h in range(n_heads):
        out = jnp.where(lane // head_w == h, o[h * t:(h + 1) * t], out)
    return out


MLA_SCALE = (MLA_NOPE + MLA_ROPE) ** -0.5
MLA_QW = MLA_HEADS * MLA_NOPE + MLA_HEADS * MLA_ROPE
MLA_NW = MLA_HEADS * MLA_NOPE


def _mla_attend(q, kcat, v):
    qs = jnp.concatenate([_stack_heads(q[:, :MLA_NW], MLA_HEADS, MLA_NOPE),
                          _stack_heads(q[:, MLA_NW:], MLA_HEADS, MLA_ROPE)], axis=1)
    s = _dot_nt(qs.astype(BF16), kcat) * MLA_SCALE
    p = jnp.exp(s - jnp.max(s, axis=-1, keepdims=True))
    o = _dot(p.astype(BF16), v) / jnp.sum(p, axis=-1, keepdims=True)
    return _unstack_heads(o, MLA_HEADS, MLA_V)


def _mla_ctx_kernel(u_ref, qg_ref, wq_ref, kvg_ref, wkv_ref, o_ref, ckv_ref):
    for j in range(CTX_PER_STEP):
        rows = slice(j * SEQ, (j + 1) * SEQ)
        u = u_ref[rows, :]
        q = _dot(_rms(u[:, :MLA_Q_LORA], qg_ref[...]).astype(BF16), wq_ref[...])
        ckv = _rms(u[:, MLA_Q_LORA:MLA_Q_LORA + MLA_KV_LORA], kvg_ref[...])
        ckv_ref[rows, :] = ckv
        kv = _dot(ckv.astype(BF16), wkv_ref[...])
        kr4 = u[:, MLA_Q_LORA + MLA_KV_LORA:]
        kcat = jnp.concatenate([kv[:, :MLA_NW], kr4], axis=1).astype(BF16)
        o_ref[rows, :] = _mla_attend(q, kcat, kv[:, MLA_NW:].astype(BF16)).astype(o_ref.dtype)


def _mla_ctx(u_mla, qg, wq, kvg, wkv):
    full = lambda shape: pl.BlockSpec(shape, lambda b: (0, 0))
    rows = CTX_PER_STEP * SEQ
    return pl.pallas_call(
        _mla_ctx_kernel,
        grid=(BATCH // CTX_PER_STEP,),
        in_specs=[pl.BlockSpec((rows, MLA_PACK), lambda b: (b, 0)),
                  full((1, MLA_Q_LORA)), full((MLA_Q_LORA, MLA_QW)),
                  full((1, MLA_KV_LORA)), full((MLA_KV_LORA, 2 * MLA_NW))],
        out_specs=[pl.BlockSpec((rows, MLA_NW), lambda b: (b, 0)),
                   pl.BlockSpec((rows, MLA_KV_LORA), lambda b: (b, 0))],
        out_shape=[jax.ShapeDtypeStruct((N_CTX, MLA_NW), BF16),
                   jax.ShapeDtypeStruct((N_CTX, MLA_KV_LORA), F32)],
        compiler_params=_params("arbitrary"),
        name="mla_ctx",
    )(u_mla, qg, wq, kvg, wkv)


MLA_TK = PAST_LEN + DEC_SEQ


def _mla_lat_kernel(u_ref, cckv_ref, ckr_ref, cos_ref, sin_ref, qg_ref, wq_ref, kvg_ref, wkv_ref,
                    o_ref, kcat_s, v_s):
    i = pl.program_id(1)

    @pl.when(i == 0)
    def _():
        u = u_ref[...]
        ckv_new = _rms(u[:, MLA_Q_LORA:MLA_Q_LORA + MLA_KV_LORA], kvg_ref[...])
        ckv_all = jnp.concatenate([cckv_ref[...], ckv_new], axis=0)
        kv = _dot(ckv_all.astype(BF16), wkv_ref[...])
        kr_new = _rope(u[:, MLA_Q_LORA + MLA_KV_LORA:], cos_ref[...], sin_ref[...])
        ckr = ckr_ref[...]
        kr_all = jnp.concatenate([jnp.concatenate([ckr] * MLA_HEADS, axis=1), kr_new], axis=0)
        kcat_s[...] = jnp.concatenate([kv[:, :MLA_NW], kr_all], axis=1).astype(BF16)
        v_s[...] = kv[:, MLA_NW:].astype(BF16)

    r0 = pl.multiple_of(i * MLA_QBLK, MLA_QBLK)
    cq = u_ref[pl.ds(r0, MLA_QBLK), 0:MLA_Q_LORA]
    q = _dot(_rms(cq, qg_ref[...]).astype(BF16), wq_ref[...])
    qr = _rope(q[:, MLA_NW:], cos_ref[pl.ds(r0, MLA_QBLK), :], sin_ref[pl.ds(r0, MLA_QBLK), :])
    q = jnp.concatenate([q[:, :MLA_NW], qr], axis=1)
    o_ref[...] = _mla_attend(q, kcat_s[...], v_s[...]).astype(o_ref.dtype)


def _mla_lat(u_mla, cache_ckv, cache_kr, l, cos, sin, qg, wq, kvg, wkv):
    full = lambda shape: pl.BlockSpec(shape, lambda b, i: (0, 0))
    nq = DEC_SEQ // MLA_QBLK
    return pl.pallas_call(
        _mla_lat_kernel,
        grid=(DEC_BATCH, nq),
        in_specs=[pl.BlockSpec((DEC_SEQ, MLA_PACK), lambda b, i: (N_CTX // DEC_SEQ + b, 0)),
                  pl.BlockSpec((None, None, PAST_LEN, MLA_KV_LORA), lambda b, i: (b, l, 0, 0)),
                  pl.BlockSpec((None, None, PAST_LEN, MLA_ROPE), lambda b, i: (b, l, 0, 0)),
                  full((DEC_SEQ, MLA_HEADS * MLA_ROPE)), full((DEC_SEQ, MLA_HEADS * MLA_ROPE)),
                  full((1, MLA_Q_LORA)), full((MLA_Q_LORA, MLA_QW)),
                  full((1, MLA_KV_LORA)), full((MLA_KV_LORA, 2 * MLA_NW))],
        out_specs=pl.BlockSpec((MLA_QBLK, MLA_NW), lambda b, i: (b * nq + i, 0)),
        out_shape=jax.ShapeDtypeStruct((N_LAT, MLA_NW), BF16),
        scratch_shapes=[pltpu.VMEM((MLA_TK, MLA_QW), BF16), pltpu.VMEM((MLA_TK, MLA_NW), BF16)],
        compiler_params=_params("arbitrary", "arbitrary"),
        name="mla_lat",
    )(u_mla, cache_ckv, cache_kr, cos, sin, qg, wq, kvg, wkv)


def _hgrn_kernel(*refs, seq, n_seq, has_state):
    if has_state:
        (u_ref, lbf_ref, lbb_ref, ng_ref, s0_ref, o_ref,
         q_s, kf_s, gf_s, kb_s, gb_s, of_s, ob_s, stf_s, stb_s) = refs
    else:
        (u_ref, lbf_ref, lbb_ref, ng_ref, o_ref, so_ref,
         q_s, kf_s, gf_s, kb_s, gb_s, of_s, ob_s, stf_s, stb_s) = refs
    C = HG_CHUNK
    W = HG_W

    q_s[...] = _silu(u_ref[:, 0:W])
    ff = lbf_ref[...] + (1.0 - lbf_ref[...]) * jax.nn.sigmoid(u_ref[:, W:2 * W])
    kf_s[...] = 1.0 - ff
    gf_s[...] = jnp.log(ff)
    fb = lbb_ref[...] + (1.0 - lbb_ref[...]) * jax.nn.sigmoid(u_ref[:, 2 * W:3 * W])
    kb_s[...] = 1.0 - fb
    gb_s[...] = jnp.log(fb)

    rr = lax.broadcasted_iota(jnp.int32, (W, W), 0)
    cc = lax.broadcasted_iota(jnp.int32, (W, W), 1)
    blockdiag = rr // HG_DK == cc // HG_DK
    if has_state:
        for st, d in ((stf_s, 0), (stb_s, 1)):
            rows = []
            for h in range(HG_HEADS):
                z = lambda n: jnp.zeros((HG_DV, n * HG_DK), F32)
                parts = ([z(h)] if h else []) + [s0_ref[d, h]] + ([z(HG_HEADS - 1 - h)] if h < HG_HEADS - 1 else [])
                rows.append(jnp.concatenate(parts, axis=1) if len(parts) > 1 else parts[0])
            st[0] = jnp.concatenate(rows, axis=0)
    else:
        stf_s[...] = jnp.zeros_like(stf_s)
        stb_s[...] = jnp.zeros_like(stb_s)

    B = HG_BLOCK
    per_block = B // C
    n_blocks = seq // B
    ri = lax.broadcasted_iota(jnp.int32, (B, B), 0)
    ci = lax.broadcasted_iota(jnp.int32, (B, B), 1)
    same_chunk = ri // C == ci // C
    rs = lax.broadcasted_iota(jnp.int32, (HG_HEADS * B, B), 0) % B
    cs = lax.broadcasted_iota(jnp.int32, (HG_HEADS * B, B), 1)
    same_chunk_s = rs // C == cs // C

    def sums(to_end, to_mid):
        f = lambda m: jnp.where(same_chunk & m, 1.0, 0.0)
        whole = jnp.where(same_chunk, 1.0, 0.0)
        return jnp.concatenate([f(to_end), f(to_end) - f(to_mid), whole - f(to_end)], axis=0)

    mid_f = (ri // C) * C + (C // 2 - 1)
    mid_b = (ri // C) * C + C // 2
    sums_f = sums(ci <= ri, ci <= mid_f)
    sums_b = sums(ci >= ri, ci >= mid_b)
    keep_f = same_chunk_s & (rs >= cs)
    keep_b = same_chunk_s & (cs >= rs)

    def block(r, k_s, g_s, o_s, st_s, sum_mat, keep, order):
        q = q_s[pl.ds(r, B), :]
        k = k_s[pl.ds(r, B), :]
        v = u_ref[pl.ds(r, B), 3 * W:4 * W].astype(BF16)
        gs = _dot_exact_lhs(sum_mat, g_s[pl.ds(r, B), :])
        G, Gq, Gk2 = gs[:B], gs[B:2 * B], gs[2 * B:]
        qe = _stack_heads(q * jnp.exp(Gq), HG_HEADS, HG_DK)
        ke = k * jnp.exp(-Gq)
        A = jnp.where(keep, _dot_nt(qe.astype(BF16), ke.astype(BF16)), 0.0)
        o_intra = _unstack_heads(_dot(A.astype(BF16), v), HG_HEADS, HG_DV)
        qg = (q * jnp.exp(G)).astype(BF16)
        k2 = (k * jnp.exp(Gk2)).astype(BF16)
        decay = jnp.exp(G + Gk2)
        st = st_s[...]
        o_inter = [None] * per_block
        for c in order:
            rows = slice(c * C, (c + 1) * C)
            o_inter[c] = _dot_nt(st.astype(BF16), qg[rows])
            st = st * decay[c * C:c * C + 1] + jnp.where(blockdiag, _dot_tn(v[rows], k2[rows]), 0.0)
        st_s[...] = st
        o_s[pl.ds(r, B), :] = o_intra + jnp.concatenate(o_inter, axis=1).T

    def fwd(j, r):
        block(r, kf_s, gf_s, of_s, stf_s.at[j], sums_f, keep_f, range(per_block))

    def bwd(j, r):
        block(r, kb_s, gb_s, ob_s, stb_s.at[j], sums_b, keep_b, range(per_block - 1, -1, -1))

    for j in range(n_seq):
        if n_blocks == 1:
            fwd(j, j * seq)
            bwd(j, j * seq)
        else:
            lax.fori_loop(0, n_blocks, lambda i, c, j=j: (fwd(j, pl.multiple_of(j * seq + i * B, B)), c)[1], 0)
            lax.fori_loop(0, n_blocks,
                          lambda i, c, j=j: (bwd(j, pl.multiple_of(j * seq + (n_blocks - 1 - i) * B, B)), c)[1], 0)

    o = of_s[...] + ob_s[...]
    ms = _dot_exact_rhs(o * o, jnp.where(blockdiag, 1.0 / HG_DV, 0.0))
    on = o * lax.rsqrt(ms + EPS) * ng_ref[...]
    o_ref[...] = (on * _silu(u_ref[:, 4 * W:5 * W])).astype(o_ref.dtype)

    if not has_state:
        for j in range(n_seq):
            for st, d in ((stf_s, 0), (stb_s, 1)):
                for h in range(HG_HEADS):
                    so_ref[j, d, h] = st[j, h * HG_DV:(h + 1) * HG_DV, h * HG_DK:(h + 1) * HG_DK]


def _hgrn(u_hg, lbf, lbb, ng4, state_t, *, seq, n_batch, row_block0):
    has_state = state_t is not None
    n_seq = 1 if has_state else HG_CTX_PER_STEP
    rows = n_seq * seq
    full = lambda shape: pl.BlockSpec(shape, lambda b: (0, 0))
    in_specs = [pl.BlockSpec((rows, HG_IN), lambda b: (row_block0 + b, 0)),
                full((1, HG_W)), full((1, HG_W)), full((1, HG_W))]
    args = [u_hg, lbf, lbb, ng4]
    o_spec = pl.BlockSpec((rows, HG_W), lambda b: (b, 0))
    o_shape = jax.ShapeDtypeStruct((n_batch * seq, HG_W), BF16)
    if has_state:
        in_specs.append(pl.BlockSpec((None, 2, HG_HEADS, HG_DV, HG_DK), lambda b: (b, 0, 0, 0, 0)))
        args.append(state_t)
        out_specs, out_shape = o_spec, o_shape
    else:
        out_specs = [o_spec, pl.BlockSpec((n_seq, 2, HG_HEADS, HG_DV, HG_DK), lambda b: (b, 0, 0, 0, 0))]
        out_shape = [o_shape, jax.ShapeDtypeStruct((n_batch, 2, HG_HEADS, HG_DV, HG_DK), F32)]
    return pl.pallas_call(
        functools.partial(_hgrn_kernel, seq=seq, n_seq=n_seq, has_state=has_state),
        grid=(n_batch // n_seq,),
        in_specs=in_specs, out_specs=out_specs, out_shape=out_shape,
        scratch_shapes=[pltpu.VMEM((rows, HG_W), F32)] * 7 + [pltpu.VMEM((n_seq, HG_W, HG_W), F32)] * 2,
        compiler_params=_params("arbitrary"),
        name="hgrn_lat" if has_state else "hgrn_ctx",
    )(*args)


def _dft_tables(n):
    j = np.arange(n, dtype=np.int64)
    ang = 2.0 * np.pi * ((j[:, None] * j[None, :]) % n).astype(np.float64) / n
    return np.cos(ang) / np.sqrt(n), np.sin(ang) / np.sqrt(n)


def _fourier_tables(seq):
    gw = FN_WIDTH // FN_GROUPS
    cg, sg = _dft_tables(gw)
    eye = np.eye(FN_GROUPS)
    chan = np.concatenate([np.kron(eye, cg), np.kron(eye, sg)], axis=1)
    ct, st = _dft_tables(seq)
    pos = np.concatenate([ct, -st], axis=1)
    return jnp.asarray(chan, F32).astype(BF16), jnp.asarray(pos, F32).astype(BF16)


def _fourier_kernel(x_ref, chan_ref, pos_ref, w_ref, o_ref):
    x12 = _dot(x_ref[...].astype(BF16), chan_ref[...])
    z = jnp.concatenate([x12[:, :FN_WIDTH], x12[:, FN_WIDTH:]], axis=0).astype(BF16)
    y = _dot(pos_ref[...], z)
    o_ref[...] = _dot(y.astype(BF16), w_ref[...]).astype(o_ref.dtype)


def _fourier(u_fn, w, *, seq, n_batch, row_block0):
    chan, pos = _fourier_tables(seq)
    full = lambda shape: pl.BlockSpec(shape, lambda b: (0, 0))
    return pl.pallas_call(
        _fourier_kernel,
        grid=(n_batch,),
        in_specs=[pl.BlockSpec((seq, FN_WIDTH), lambda b: (row_block0 + b, 0)),
                  full((FN_WIDTH, 2 * FN_WIDTH)), full((seq, 2 * seq)), full((FN_WIDTH, FN_WIDTH))],
        out_specs=pl.BlockSpec((seq, FN_WIDTH), lambda b: (b, 0)),
        out_shape=jax.ShapeDtypeStruct((n_batch * seq, FN_WIDTH), BF16),
        compiler_params=_params("arbitrary"),
        name="fourier",
    )(u_fn, chan, pos, w)


SWA_SCALE = SWA_HD ** -0.5
SWA_QW = SWA_HEADS * SWA_HD
SWA_KW = SWA_KV_HEADS * SWA_HD
SWA_STACK_ORDER = (0, 2, 1, 3)


def _swa_stack_q(q):
    return jnp.concatenate([_stack_heads(q[:, :SWA_KW], SWA_KV_HEADS, SWA_HD),
                            _stack_heads(q[:, SWA_KW:], SWA_KV_HEADS, SWA_HD)], axis=0)


def _swa_unstack_o(o):
    t = o.shape[0] // SWA_HEADS
    return jnp.concatenate([_unstack_heads(o[:2 * t], SWA_KV_HEADS, SWA_HD),
                            _unstack_heads(o[2 * t:], SWA_KV_HEADS, SWA_HD)], axis=1)


def _sink_rows(sink_ref, t):
    return jnp.concatenate([jnp.full((t, 1), sink_ref[h], F32) for h in SWA_STACK_ORDER], axis=0)


def _swa_ctx_kernel(sink_ref, u_ref, o_ref):
    sink = _sink_rows(sink_ref, SEQ)
    for j in range(CTX_PER_STEP):
        rows = slice(j * SEQ, (j + 1) * SEQ)
        u = u_ref[rows, :]
        qs = _swa_stack_q(u[:, :SWA_QW]).astype(BF16)
        k = u[:, SWA_QW:SWA_QW + SWA_KW].astype(BF16)
        v = u[:, SWA_QW + SWA_KW:].astype(BF16)
        s = _dot_nt(qs, k) * SWA_SCALE
        m = jnp.maximum(jnp.max(s, axis=-1, keepdims=True), sink)
        p = jnp.exp(s - m)
        denom = jnp.sum(p, axis=-1, keepdims=True) + jnp.exp(sink - m)
        o_ref[rows, :] = _swa_unstack_o(_dot(p.astype(BF16), v) / denom).astype(o_ref.dtype)


def _swa_ctx(u_swa, sink):
    rows = CTX_PER_STEP * SEQ
    return pl.pallas_call(
        _swa_ctx_kernel,
        grid=(BATCH // CTX_PER_STEP,),
        in_specs=[pl.BlockSpec(memory_space=pltpu.SMEM),
                  pl.BlockSpec((rows, SWA_IN), lambda b: (b, 0))],
        out_specs=pl.BlockSpec((rows, SWA_QW), lambda b: (b, 0)),
        out_shape=jax.ShapeDtypeStruct((N_CTX, SWA_QW), BF16),
        compiler_params=_params("arbitrary"),
        name="swa_ctx",
    )(sink, u_swa)


SWA_PAD = DEC_SEQ + 2 * SWA_QBLK


def _swa_lat_kernel(sink_ref, u_ref, kc_ref, vc_ref, cosq_ref, sinq_ref, cosk_ref, sin_k_ref,
                    o_ref, k_s, v_s):
    i = pl.program_id(1)
    B = SWA_QBLK

    @pl.when(i == 0)
    def _():
        zeros = jnp.zeros((B, SWA_KW), BF16)
        k = _rope(u_ref[:, SWA_QW:SWA_QW + SWA_KW], cosk_ref[...], sin_k_ref[...]).astype(BF16)
        k_s[...] = jnp.concatenate([zeros, k, zeros], axis=0)
        v_s[...] = jnp.concatenate([zeros, u_ref[:, SWA_QW + SWA_KW:].astype(BF16), zeros], axis=0)

    r0 = pl.multiple_of(i * B, B)
    q = _rope(u_ref[pl.ds(r0, B), 0:SWA_QW], cosq_ref[pl.ds(r0, B), :], sinq_ref[pl.ds(r0, B), :])
    qs = _swa_stack_q(q).astype(BF16)
    s_loc = _dot_nt(qs, k_s[pl.ds(r0, 3 * B), :]) * SWA_SCALE
    row = lax.broadcasted_iota(jnp.int32, s_loc.shape, 0) % B
    col = lax.broadcasted_iota(jnp.int32, s_loc.shape, 1)
    kpos = r0 - B + col
    valid = (jnp.abs(row + B - col) <= WINDOW) & (kpos >= 0) & (kpos < DEC_SEQ)
    s_loc = jnp.where(valid, s_loc, NEG_INF)
    s_ctx = _dot_nt(qs, kc_ref[...].astype(BF16)) * SWA_SCALE
    sink = _sink_rows(sink_ref, B)
    m = jnp.maximum(jnp.maximum(jnp.max(s_loc, axis=-1, keepdims=True),
                                jnp.max(s_ctx, axis=-1, keepdims=True)), sink)
    p_loc = jnp.exp(s_loc - m)
    p_ctx = jnp.exp(s_ctx - m)
    denom = (jnp.sum(p_loc, axis=-1, keepdims=True) + jnp.sum(p_ctx, axis=-1, keepdims=True)
             + jnp.exp(sink - m))
    o = _dot(p_loc.astype(BF16), v_s[pl.ds(r0, 3 * B), :]) + _dot(p_ctx.astype(BF16), vc_ref[...].astype(BF16))
    o_ref[...] = _swa_unstack_o(o / denom).astype(o_ref.dtype)


def _swa_lat(u_swa, cache_k, cache_v, l, sink, cosq, sinq, cosk, sink_k):
    full = lambda shape: pl.BlockSpec(shape, lambda b, i: (0, 0))
    nq = DEC_SEQ // SWA_QBLK
    cache_spec = pl.BlockSpec((None, None, PAST_LEN, SWA_KW), lambda b, i: (b, l, 0, 0))
    return pl.pallas_call(
        _swa_lat_kernel,
        grid=(DEC_BATCH, nq),
        in_specs=[pl.BlockSpec(memory_space=pltpu.SMEM),
                  pl.BlockSpec((DEC_SEQ, SWA_IN), lambda b, i: (N_CTX // DEC_SEQ + b, 0)),
                  cache_spec, cache_spec,
                  full((DEC_SEQ, SWA_QW)), full((DEC_SEQ, SWA_QW)),
                  full((DEC_SEQ, SWA_KW)), full((DEC_SEQ, SWA_KW))],
        out_specs=pl.BlockSpec((SWA_QBLK, SWA_QW), lambda b, i: (b * nq + i, 0)),
        out_shape=jax.ShapeDtypeStruct((N_LAT, SWA_QW), BF16),
        scratch_shapes=[pltpu.VMEM((SWA_PAD, SWA_KW), BF16), pltpu.VMEM((SWA_PAD, SWA_KW), BF16)],
        compiler_params=_params("arbitrary", "arbitrary"),
        name="swa_lat",
    )(sink, u_swa, cache_k, cache_v, cosq, sinq, cosk, sink_k)


N_BLK = N_TOK // ROW_TILE
SEG_ALIGN = 16
LOCAL_ROWS = ROW_TILE * TOP_K + N_EXPERTS * SEG_ALIGN
LOCAL_CHUNK = 512
EXPERT_TILE = 768
SORTED_ROWS = -(-(N_TOK * TOP_K + N_BLK * N_EXPERTS * SEG_ALIGN + N_EXPERTS * (EXPERT_TILE + SEG_ALIGN))
                // EXPERT_TILE) * EXPERT_TILE
MAX_TILES = SORTED_ROWS // EXPERT_TILE
NOT_PICKED = -1.0
NO_DEST = 4095.0


def _out_kernel(*refs, n_x):
    x_all = _read_rows(refs[:n_x], OUT_ROWS)
    mix_all = [_read_rows(refs[n_x + 2 * j:n_x + 2 * j + 2], OUT_ROWS) for j in range(4)]
    out_refs = refs[n_x + 8:]
    for blk in range(OUT_ROWS // ROW_TILE):
        rows = slice(blk * ROW_TILE, (blk + 1) * ROW_TILE)
        _out_block(x_all[rows], [m[rows] for m in mix_all], out_refs, blk)


def _out_block(x, mixers, refs, blk):
    (mod_ref, g_ref, wo_ref, wrt_ref, br_ref, wsg_ref, wsu_ref, wsd_ref,
     x1_ref, h_ref, gate_ref, rank_ref, cnt_ref) = refs
    rows = slice(blk * ROW_TILE, (blk + 1) * ROW_TILE)
    mix = jnp.zeros((ROW_TILE, D_MODEL), F32)
    for j in range(4):
        mix = mix + _dot(mixers[j], wo_ref[j * 256:(j + 1) * 256, :])
    g1 = mod_ref[:, 2 * D_MODEL:3 * D_MODEL]
    sh2 = mod_ref[:, 3 * D_MODEL:4 * D_MODEL]
    sc2 = mod_ref[:, 4 * D_MODEL:5 * D_MODEL]
    g2 = mod_ref[:, 5 * D_MODEL:6 * D_MODEL]
    x1 = x + g1 * mix
    h = _rms(x1, g_ref[...]) * (1.0 + sc2) + sh2
    hb = h.astype(BF16)
    h_ref[rows, :] = hb

    logits = lax.dot_general(wrt_ref[...], h, (((1,), (1,)), ((), ())),
                             precision=lax.Precision.HIGHEST, preferred_element_type=F32)
    scores = jax.nn.sigmoid(logits)
    sel = scores + br_ref[...]
    eidx = lax.broadcasted_iota(jnp.int32, sel.shape, 0)
    gate = jnp.zeros_like(scores)
    picked = jnp.zeros_like(scores)
    for _ in range(TOP_K):
        best = jnp.max(sel, axis=0, keepdims=True)
        first = jnp.min(jnp.where(sel == best, eidx, N_EXPERTS), axis=0, keepdims=True)
        pick = eidx == first
        gate = jnp.where(pick, scores, gate)
        picked = jnp.where(pick, 1.0, picked)
        sel = jnp.where(pick, -jnp.inf, sel)
    gate = ROUTE_SCALE * gate / jnp.sum(gate, axis=0, keepdims=True)

    ti = lax.broadcasted_iota(jnp.int32, (ROW_TILE, ROW_TILE), 0)
    tj = lax.broadcasted_iota(jnp.int32, (ROW_TILE, ROW_TILE), 1)
    pb = picked.astype(BF16)
    rank = _dot(pb, jnp.where(ti < tj, 1.0, 0.0).astype(BF16))
    gate_ref[:, rows] = gate
    rank_ref[:, rows] = jnp.where(picked > 0.0, rank, NOT_PICKED)
    counts = _dot_nt(jnp.ones((8, ROW_TILE), BF16), pb)
    cnt_ref[blk] = jnp.concatenate([counts, jnp.zeros_like(counts)], axis=1)

    hid = _silu(_dot(hb, wsg_ref[...])) * _dot(hb, wsu_ref[...])
    x1_ref[rows, :] = x1 + g2 * _dot(hid.astype(BF16), wsd_ref[...])


def _out_proj(x_parts, mixer_pairs, mod, g, wo, wrt, br, wsg, wsu, wsd):
    row = lambda i: (i, 0)
    col = lambda i: (0, i)
    full = lambda shape: pl.BlockSpec(shape, lambda i: (0, 0))
    per_step = OUT_ROWS // ROW_TILE
    et_spec = pl.BlockSpec((N_EXPERTS, OUT_ROWS), col)
    et_shape = jax.ShapeDtypeStruct((N_EXPERTS, N_TOK), F32)
    return pl.pallas_call(
        functools.partial(_out_kernel, n_x=len(x_parts)),
        grid=(N_TOK // OUT_ROWS,),
        in_specs=_row_specs(x_parts, D_MODEL, OUT_ROWS) + 4 * _ctx_lat_specs(256, OUT_ROWS) + [
            pl.BlockSpec((None, 1, 6 * D_MODEL), lambda i: (_mod_row(i * per_step), 0, 0)),
            full((1, D_MODEL)), full((D_MODEL, D_MODEL)),
            full((N_EXPERTS, D_MODEL)), full((N_EXPERTS, 1)),
            full((D_MODEL, D_SHARED)), full((D_MODEL, D_SHARED)), full((D_SHARED, D_MODEL))],
        out_specs=[pl.BlockSpec((OUT_ROWS, D_MODEL), row), pl.BlockSpec((OUT_ROWS, D_MODEL), row),
                   et_spec, et_spec,
                   pl.BlockSpec((per_step, 8, 128), lambda i: (i, 0, 0))],
        out_shape=[jax.ShapeDtypeStruct((N_TOK, D_MODEL), F32),
                   jax.ShapeDtypeStruct((N_TOK, D_MODEL), BF16),
                   et_shape, et_shape,
                   jax.ShapeDtypeStruct((N_BLK, 8, 128), F32)],
        compiler_params=_params("arbitrary"),
        name="out_proj",
    )(*x_parts, *[a for pair in mixer_pairs for a in pair], mod, g, wo, wrt, br, wsg, wsu, wsd)


def _segment_plan(cnt):
    cnt = cnt[:, 0, :N_EXPERTS].astype(jnp.int32)
    seg = jnp.maximum((cnt + (SEG_ALIGN - 1)) // SEG_ALIGN, 1) * SEG_ALIGN
    local = jnp.cumsum(seg, axis=1) - seg
    total = jnp.sum(seg, axis=1)
    per_expert = jnp.sum(seg, axis=0)
    padded = (per_expert + SEG_ALIGN + (EXPERT_TILE - 1)) // EXPERT_TILE * EXPERT_TILE
    ends = jnp.cumsum(padded)
    start = ends - padded
    dest = start[None, :] + jnp.cumsum(seg, axis=0) - seg
    n_tiles = ends[-1] // EXPERT_TILE
    tiles = jnp.arange(MAX_TILES, dtype=jnp.int32)
    tile_expert = jnp.sum((ends // EXPERT_TILE)[None, :] <= jnp.minimum(tiles, n_tiles - 1)[:, None], axis=1)
    tile_expert = tile_expert.astype(jnp.int32)
    plan = dict(seg=seg, local=local, total=total.astype(jnp.int32), dest=dest.astype(jnp.int32),
                tail_start=(start + per_expert).astype(jnp.int32), tail_rows=(padded - per_expert).astype(jnp.int32),
                n_tiles=n_tiles.reshape(1).astype(jnp.int32), tile_expert=tile_expert)
    segf, localf = seg.astype(F32), local.astype(F32)
    pad_lanes = lambda a: jnp.concatenate([a, jnp.zeros_like(a)], axis=1)[:, None, :]
    plan.update(seg_row=pad_lanes(segf), local_row=pad_lanes(localf),
                seg_col=segf[:, :, None], local_col=localf[:, :, None])
    return plan


def _local_dest_digits(rank, local_start):
    dest = jnp.where(rank >= 0.0, local_start + rank, NO_DEST)
    hi = jnp.floor(dest * (1.0 / 64.0))
    return hi, dest - 64.0 * hi


def _dispatch_kernel(dest_ref, seg_ref, local_ref, total_ref, tail_start_ref, tail_rows_ref, nt_ref,
                     h_ref, rank_ref, local_col_ref, local_row_ref, seg_row_ref,
                     xs_hbm, buf, zeros, sems, zsem, usem):
    b = pl.program_id(0)
    slot = b % 2

    def wait_block(blk, s):
        n = pl.multiple_of(total_ref[blk], SEG_ALIGN)
        pltpu.make_async_copy(buf.at[s, pl.ds(0, n)], xs_hbm.at[pl.ds(0, n)], sems.at[s]).wait()

    @pl.when(b == 0)
    def _():
        zeros[...] = jnp.zeros_like(zeros)

    def unused_tiles(action):
        for k in range(-(-MAX_TILES // N_BLK)):
            t = nt_ref[0] + b + k * N_BLK

            @pl.when(t < MAX_TILES)
            def _():
                cp = pltpu.make_async_copy(zeros.at[pl.ds(0, EXPERT_TILE)],
                                           xs_hbm.at[pl.ds(pl.multiple_of(t * EXPERT_TILE, EXPERT_TILE),
                                                           EXPERT_TILE)], usem)
                cp.start() if action == "start" else cp.wait()

    unused_tiles("start")

    @pl.when(b >= 2)
    def _():
        wait_block(b - 2, slot)

    hi, lo = _local_dest_digits(rank_ref[...], local_col_ref[...])
    code = jnp.concatenate([hi, lo], axis=0).astype(BF16)
    hb = h_ref[...]
    lstart = local_row_ref[:, :N_EXPERTS]
    lend = lstart + seg_row_ref[:, :N_EXPERTS]
    for c in range(LOCAL_ROWS // LOCAL_CHUNK):
        r = (lax.broadcasted_iota(jnp.int32, (LOCAL_CHUNK, N_EXPERTS), 0) + c * LOCAL_CHUNK).astype(F32)
        member = (r >= lstart) & (r < lend)
        table = jnp.concatenate([jnp.where(member, 64.0, 0.0), jnp.where(member, 1.0, 0.0)], axis=1).astype(BF16)
        d = _dot(table, code)
        rr = (lax.broadcasted_iota(jnp.int32, (LOCAL_CHUNK, ROW_TILE), 0) + c * LOCAL_CHUNK).astype(F32)
        onehot = jnp.where(d == rr, 1.0, 0.0).astype(BF16)
        buf[slot, c * LOCAL_CHUNK:(c + 1) * LOCAL_CHUNK, :] = _dot(onehot, hb).astype(BF16)

    for e in range(N_EXPERTS):
        n = pl.multiple_of(seg_ref[b, e], SEG_ALIGN)
        src = pl.multiple_of(local_ref[b, e], SEG_ALIGN)
        dst = pl.multiple_of(dest_ref[b, e], SEG_ALIGN)
        pltpu.make_async_copy(buf.at[slot, pl.ds(src, n)], xs_hbm.at[pl.ds(dst, n)], sems.at[slot]).start()

    unused_tiles("wait")

    @pl.when(b == N_BLK - 1)
    def _():
        def tail(e):
            n = pl.multiple_of(tail_rows_ref[e], SEG_ALIGN)
            dst = pl.multiple_of(tail_start_ref[e], SEG_ALIGN)
            return pltpu.make_async_copy(zeros.at[pl.ds(0, n)], xs_hbm.at[pl.ds(dst, n)], zsem)

        for e in range(N_EXPERTS):
            tail(e).start()
        for e in range(N_EXPERTS):
            tail(e).wait()
        wait_block(b - 1, 1 - slot)
        wait_block(b, slot)


def _dispatch(h, rank_t, plan):
    blk = lambda shape, imap: pl.BlockSpec(shape, imap)
    return pl.pallas_call(
        _dispatch_kernel,
        grid_spec=pltpu.PrefetchScalarGridSpec(
            num_scalar_prefetch=7, grid=(N_BLK,),
            in_specs=[blk((ROW_TILE, D_MODEL), lambda b, *_: (b, 0)),
                      blk((N_EXPERTS, ROW_TILE), lambda b, *_: (0, b)),
                      blk((None, N_EXPERTS, 1), lambda b, *_: (b, 0, 0)),
                      blk((None, 1, 128), lambda b, *_: (b, 0, 0)),
                      blk((None, 1, 128), lambda b, *_: (b, 0, 0))],
            out_specs=pl.BlockSpec(memory_space=pl.ANY),
            scratch_shapes=[pltpu.VMEM((2, LOCAL_ROWS, D_MODEL), BF16),
                            pltpu.VMEM((EXPERT_TILE + SEG_ALIGN, D_MODEL), BF16),
                            pltpu.SemaphoreType.DMA((2,)), pltpu.SemaphoreType.DMA(()),
                            pltpu.SemaphoreType.DMA(())]),
        out_shape=jax.ShapeDtypeStruct((SORTED_ROWS, D_MODEL), BF16),
        compiler_params=_params("arbitrary"),
        name="dispatch",
    )(plan['dest'], plan['seg'], plan['local'], plan['total'], plan['tail_start'], plan['tail_rows'],
      plan['n_tiles'], h, rank_t, plan['local_col'], plan['local_row'], plan['seg_row'])


def _expert_kernel(te_ref, nt_ref, x_ref, wg_ref, wu_ref, wd_ref, y_ref, wg_s, wu_s, wd_s):
    i = pl.program_id(0)
    active = i < nt_ref[0]

    @pl.when((i == 0) | (te_ref[i] != te_ref[jnp.maximum(i - 1, 0)]))
    def _():
        wg_s[...] = wg_ref[...].astype(BF16)
        wu_s[...] = wu_ref[...].astype(BF16)
        wd_s[...] = wd_ref[...].astype(BF16)

    @pl.when(active)
    def _():
        x = x_ref[...]
        hid = _silu(_dot(x, wg_s[...])) * _dot(x, wu_s[...])
        y_ref[...] = _dot(hid.astype(BF16), wd_s[...]).astype(BF16)

    @pl.when(jnp.logical_not(active))
    def _():
        y_ref[...] = jnp.zeros_like(y_ref)


def _experts(xs, plan, w_gate, w_up, w_down, l):
    rows_in = pl.BlockSpec((EXPERT_TILE, D_MODEL), lambda i, te, nt: (jnp.minimum(i, nt[0] - 1), 0))
    rows_out = pl.BlockSpec((EXPERT_TILE, D_MODEL), lambda i, te, nt: (i, 0))
    wspec = lambda shape: pl.BlockSpec((None, None) + shape, lambda i, te, nt: (l, te[i], 0, 0))
    return pl.pallas_call(
        _expert_kernel,
        grid_spec=pltpu.PrefetchScalarGridSpec(
            num_scalar_prefetch=2, grid=(MAX_TILES,),
            in_specs=[rows_in, wspec((D_MODEL, D_EXPERT)), wspec((D_MODEL, D_EXPERT)), wspec((D_EXPERT, D_MODEL))],
            out_specs=rows_out,
            scratch_shapes=[pltpu.VMEM((D_MODEL, D_EXPERT), BF16), pltpu.VMEM((D_MODEL, D_EXPERT), BF16),
                            pltpu.VMEM((D_EXPERT, D_MODEL), BF16)]),
        out_shape=jax.ShapeDtypeStruct((SORTED_ROWS, D_MODEL), BF16),
        compiler_params=_params("arbitrary"),
        name="experts",
    )(plan['tile_expert'], plan['n_tiles'], xs, w_gate, w_up, w_down)


def _combine_kernel(*refs, final):
    refs = list(refs)
    dest_ref, seg_ref, local_ref, total_ref = refs[:4]
    x1_ref, ys_hbm, gate_ref, rank_ref, mod_ref, local_row_ref, local_col_ref, seg_col_ref = refs[4:12]
    rest = refs[12:]
    fg_ref = rest.pop(0) if final else None
    outs, (buf, sems) = rest[:-2], rest[-2:]
    b = pl.program_id(0)
    slot = b % 2

    def fetch(blk, s):
        for e in range(N_EXPERTS):
            n = pl.multiple_of(seg_ref[blk, e], SEG_ALIGN)
            src = pl.multiple_of(dest_ref[blk, e], SEG_ALIGN)
            dst = pl.multiple_of(local_ref[blk, e], SEG_ALIGN)
            pltpu.make_async_copy(ys_hbm.at[pl.ds(src, n)], buf.at[s, pl.ds(dst, n)], sems.at[s]).start()

    @pl.when(b == 0)
    def _():
        buf[...] = jnp.zeros_like(buf)
        fetch(0, 0)

    @pl.when(b + 1 < N_BLK)
    def _():
        fetch(b + 1, 1 - slot)

    n_rows = pl.multiple_of(total_ref[b], SEG_ALIGN)
    pltpu.make_async_copy(ys_hbm.at[pl.ds(0, n_rows)], buf.at[slot, pl.ds(0, n_rows)], sems.at[slot]).wait()

    gate = gate_ref[...].T
    hi, lo = _local_dest_digits(rank_ref[...].T, local_row_ref[:, :N_EXPERTS])
    lhs = jnp.concatenate([jnp.concatenate([hi, lo], axis=1),
                           jnp.concatenate([jnp.zeros_like(gate), gate], axis=1)], axis=0).astype(BF16)
    lstart = local_col_ref[...]
    lend = lstart + seg_col_ref[...]
    routed = jnp.zeros((ROW_TILE, D_MODEL), F32)
    for c in range(LOCAL_ROWS // LOCAL_CHUNK):
        r = (lax.broadcasted_iota(jnp.int32, (N_EXPERTS, LOCAL_CHUNK), 1) + c * LOCAL_CHUNK).astype(F32)
        member = (r >= lstart) & (r < lend)
        ones = jnp.where(member, 1.0, 0.0)
        table = jnp.concatenate([ones * 64.0, ones], axis=0).astype(BF16)
        dg = _dot(lhs, table)
        rr = (lax.broadcasted_iota(jnp.int32, (ROW_TILE, LOCAL_CHUNK), 1) + c * LOCAL_CHUNK).astype(F32)
        weights = jnp.where(dg[:ROW_TILE] == rr, dg[ROW_TILE:], 0.0).astype(BF16)
        routed = routed + _dot(weights, buf[slot, c * LOCAL_CHUNK:(c + 1) * LOCAL_CHUNK, :])
    x = x1_ref[...] + mod_ref[:, 5 * D_MODEL:6 * D_MODEL] * routed
    if final:
        y = _rms(x, fg_ref[...])

        @pl.when(b < CTX_TILES)
        def _():
            outs[0][...] = y

        @pl.when(b >= CTX_TILES)
        def _():
            outs[1][...] = y
    else:
        outs[0][...] = x


def _combine(x1, ys, gate_t, rank_t, mod, plan, final_g):
    final = final_g is not None
    blk = lambda shape, imap: pl.BlockSpec(shape, imap)
    xspec = blk((ROW_TILE, D_MODEL), lambda b, *_: (b, 0))
    if final:
        out_specs = _ctx_lat_specs(D_MODEL)
        out_shape = [jax.ShapeDtypeStruct((N_CTX, D_MODEL), F32), jax.ShapeDtypeStruct((N_LAT, D_MODEL), F32)]
    else:
        out_specs, out_shape = xspec, jax.ShapeDtypeStruct((N_TOK, D_MODEL), F32)
    et_spec = blk((N_EXPERTS, ROW_TILE), lambda b, *_: (0, b))
    in_specs = [xspec, pl.BlockSpec(memory_space=pl.ANY), et_spec, et_spec,
                blk((None, 1, 6 * D_MODEL), lambda b, *_: (_mod_row(b), 0, 0)),
                blk((None, 1, 128), lambda b, *_: (b, 0, 0)),
                blk((None, N_EXPERTS, 1), lambda b, *_: (b, 0, 0)),
                blk((None, N_EXPERTS, 1), lambda b, *_: (b, 0, 0))]
    args = [x1, ys, gate_t, rank_t, mod, plan['local_row'], plan['local_col'], plan['seg_col']]
    if final:
        in_specs.append(blk((1, D_MODEL), lambda b, *_: (0, 0)))
        args.append(final_g)
    return pl.pallas_call(
        functools.partial(_combine_kernel, final=final),
        grid_spec=pltpu.PrefetchScalarGridSpec(
            num_scalar_prefetch=4, grid=(N_BLK,),
            in_specs=in_specs, out_specs=out_specs,
            scratch_shapes=[pltpu.VMEM((2, LOCAL_ROWS, D_MODEL), BF16), pltpu.SemaphoreType.DMA((2,))]),
        out_shape=out_shape,
        compiler_params=_params("arbitrary"),
        name="combine",
    )(plan['dest'], plan['seg'], plan['local'], plan['total'], *args)


def _rope_full_tables(dim, n_rep):
    rows = DEC_SEQ // GRID_W
    r_idx, c_idx = np.meshgrid(np.arange(rows), np.arange(GRID_W), indexing='ij')
    pos = jnp.asarray(np.stack([r_idx.reshape(-1), c_idx.reshape(-1)], axis=-1), F32)
    nf = dim // 4
    inv = ROPE_BASE ** (-jnp.arange(nf, dtype=F32) / nf)
    ang = pos[:, :, None] * inv
    ang = jnp.repeat(ang.reshape(DEC_SEQ, 2 * nf), 2, axis=1)
    sign = jnp.tile(jnp.asarray([-1.0, 1.0], F32), dim // 2)
    return jnp.tile(jnp.cos(ang), (1, n_rep)), jnp.tile(jnp.sin(ang) * sign, (1, n_rep))


def _pack_w_in(w):
    c0 = MLA_Q_LORA + MLA_KV_LORA
    kr = w[:, c0:MLA_IN]
    s0 = MLA_IN + HG_IN + FN_IN
    qh = [w[:, s0 + h * SWA_HD:s0 + (h + 1) * SWA_HD] for h in SWA_STACK_ORDER]
    return jnp.concatenate([w[:, :c0], kr, kr, kr, kr, w[:, MLA_IN:s0]] + qh
                           + [w[:, s0 + SWA_QW:]], axis=1).astype(BF16)


def _pack_w_q_up(w):
    hd = MLA_NOPE + MLA_ROPE
    nope = [w[:, h * hd:h * hd + MLA_NOPE] for h in range(MLA_HEADS)]
    rope = [w[:, h * hd + MLA_NOPE:(h + 1) * hd] for h in range(MLA_HEADS)]
    return jnp.concatenate(nope + rope, axis=1).astype(BF16)


def _pack_w_kv_up(w):
    hd = MLA_NOPE + MLA_V
    kn = [w[:, h * hd:h * hd + MLA_NOPE] for h in range(MLA_HEADS)]
    vv = [w[:, h * hd + MLA_NOPE:(h + 1) * hd] for h in range(MLA_HEADS)]
    return jnp.concatenate(kn + vv, axis=1).astype(BF16)


def _pack_w_out(w):
    s0 = 3 * 256
    rows = [w[s0 + h * SWA_HD:s0 + (h + 1) * SWA_HD] for h in SWA_STACK_ORDER]
    return jnp.concatenate([w[:s0]] + rows, axis=0).astype(BF16)


def kernel(x_prompt, x_sample, c, cache_mla_ckv, cache_mla_krope, cache_swa_k, cache_swa_v, state_hgrn,
           c_ctx, w_ada, b_ada, norm1_g, norm2_g, w_in, mla_q_norm_g, mla_w_q_up, mla_kv_norm_g, mla_w_kv_up,
           hg_lb_logits, hg_norm_g, fn_w, swa_sink, w_out, moe_w_router, moe_b_router, moe_w_gate, moe_w_up,
           moe_w_down, sh_w_gate, sh_w_up, sh_w_down, final_norm_g):
    x_parts = (x_prompt.reshape(N_CTX, D_MODEL), x_sample.reshape(N_LAT, D_MODEL))
    cv8 = jnp.concatenate([c_ctx[None, :], c, jnp.zeros((8 - 1 - DEC_BATCH, D_MODEL), F32)], axis=0)
    mods = _ada(cv8, w_ada, b_ada).reshape(DEPTH, 8, 1, 6 * D_MODEL)

    lb = jnp.cumsum(jax.nn.softmax(hg_lb_logits.astype(F32), axis=1), axis=1)
    lb = lb - lb[:, :1]

    cos_m, sin_m = _rope_full_tables(MLA_ROPE, MLA_HEADS)
    cos_q, sin_q = _rope_full_tables(SWA_HD, SWA_HEADS)
    cos_k, sin_k = cos_q[:, :SWA_KW], sin_q[:, :SWA_KW]
    cache_k = cache_swa_k.reshape(DEC_BATCH, DEPTH, PAST_LEN, SWA_KW)
    cache_v = cache_swa_v.reshape(DEC_BATCH, DEPTH, PAST_LEN, SWA_KW)
    state_t = jnp.swapaxes(state_hgrn, -1, -2)

    ctx_blk_lat = N_CTX // DEC_SEQ
    new_ckv, new_kr, new_k, new_v, new_st = [], [], [], [], []
    for l in range(DEPTH):
        u_mla, u_hg, u_fn, u_swa = _in_proj(x_parts, mods[l], norm1_g[l][None], _pack_w_in(w_in[l]))

        qg, kvg = mla_q_norm_g[l][None], mla_kv_norm_g[l][None]
        wq, wkv = _pack_w_q_up(mla_w_q_up[l]), _pack_w_kv_up(mla_w_kv_up[l])
        o_mla_c, ckv_c = _mla_ctx(u_mla, qg, wq, kvg, wkv)
        o_mla_l = _mla_lat(u_mla, cache_mla_ckv, cache_mla_krope, l, cos_m, sin_m, qg, wq, kvg, wkv)

        lbf, lbb = lb[0, l][None], lb[1, l][None]
        ng4 = jnp.tile(hg_norm_g[l], HG_HEADS)[None]
        o_hg_c, st_c = _hgrn(u_hg, lbf, lbb, ng4, None, seq=SEQ, n_batch=BATCH, row_block0=0)
        o_hg_l = _hgrn(u_hg, lbf, lbb, ng4, state_t[:, l], seq=DEC_SEQ, n_batch=DEC_BATCH,
                       row_block0=ctx_blk_lat)

        fw = fn_w[l].astype(BF16)
        o_fn_c = _fourier(u_fn, fw, seq=SEQ, n_batch=BATCH, row_block0=0)
        o_fn_l = _fourier(u_fn, fw, seq=DEC_SEQ, n_batch=DEC_BATCH, row_block0=ctx_blk_lat)

        sink = swa_sink[l]
        o_swa_c = _swa_ctx(u_swa, sink)
        o_swa_l = _swa_lat(u_swa, cache_k, cache_v, l, sink, cos_q, sin_q, cos_k, sin_k)

        x1, h2, gate_t, rank_t, cnt = _out_proj(
            x_parts, ((o_mla_c, o_mla_l), (o_hg_c, o_hg_l), (o_fn_c, o_fn_l), (o_swa_c, o_swa_l)),
            mods[l], norm2_g[l][None], _pack_w_out(w_out[l]),
            moe_w_router[l].T, moe_b_router[l][:, None],
            sh_w_gate[l].astype(BF16), sh_w_up[l].astype(BF16), sh_w_down[l].astype(BF16))
        plan = _segment_plan(cnt)
        xs = _dispatch(h2, rank_t, plan)
        ys = _experts(xs, plan, moe_w_gate, moe_w_up, moe_w_down, l)
        if l < DEPTH - 1:
            x_parts = (_combine(x1, ys, gate_t, rank_t, mods[l], plan, None),)
        else:
            y_prompt, y_sample = _combine(x1, ys, gate_t, rank_t, mods[l], plan, final_norm_g[None])

        new_ckv.append(ckv_c.reshape(BATCH, SEQ, MLA_KV_LORA))
        new_kr.append(u_mla[:N_CTX, MLA_Q_LORA + MLA_KV_LORA:MLA_IN].reshape(BATCH, SEQ, MLA_ROPE))
        new_k.append(u_swa[:N_CTX, SWA_QW:SWA_QW + SWA_KW].reshape(BATCH, SEQ, SWA_KV_HEADS, SWA_HD))
        new_v.append(u_swa[:N_CTX, SWA_QW + SWA_KW:].reshape(BATCH, SEQ, SWA_KV_HEADS, SWA_HD))
        new_st.append(jnp.swapaxes(st_c, -1, -2))

    y_prompt = y_prompt.reshape(BATCH, SEQ, D_MODEL)
    y_sample = y_sample.reshape(DEC_BATCH, DEC_SEQ, D_MODEL)
    stack = lambda xs: jnp.stack(xs, axis=1)
    return (y_prompt, y_sample, stack(new_ckv), stack(new_kr), stack(new_k), stack(new_v), stack(new_st))
```

```python
import functools

import numpy as np
import jax
import jax.numpy as jnp
from jax import lax
from jax.experimental import pallas as pl
from jax.experimental.pallas import tpu as pltpu

F32 = jnp.float32
BF16 = jnp.bfloat16

D_MODEL = 1024
BATCH = 32
SEQ = 256
DEPTH = 2
DEC_BATCH = 2
DEC_SEQ = 1024
PAST_LEN = 256
GRID_W = 64
EPS = 1e-6
ROPE_BASE = 10000.0
NEG_INF = -1e30

MLA_HEADS = 4
MLA_NOPE = 64
MLA_ROPE = 32
MLA_V = 64
MLA_Q_LORA = 256
MLA_KV_LORA = 128
HG_HEADS = 4
HG_DK = 64
HG_DV = 64
HG_W = HG_HEADS * HG_DK
FN_GROUPS = 4
FN_WIDTH = 256
SWA_HEADS = 4
SWA_KV_HEADS = 2
SWA_HD = 64
WINDOW = 128
N_EXPERTS = 64
TOP_K = 6
D_EXPERT = 256
D_SHARED = 256
ROUTE_SCALE = 2.5

MLA_IN = MLA_Q_LORA + MLA_KV_LORA + MLA_ROPE
HG_IN = 3 * HG_HEADS * HG_DK + 2 * HG_HEADS * HG_DV
FN_IN = FN_WIDTH
SWA_IN = (SWA_HEADS + 2 * SWA_KV_HEADS) * SWA_HD

N_CTX = BATCH * SEQ
N_LAT = DEC_BATCH * DEC_SEQ
N_TOK = N_CTX + N_LAT

MLA_PACK = 512
U_COLS = MLA_PACK + HG_IN + FN_IN + SWA_IN

ROW_TILE = 256
CTX_TILES = N_CTX // ROW_TILE
LAT_TILES_PER_BATCH = DEC_SEQ // ROW_TILE
HG_CHUNK = 32
HG_BLOCK = 256
SWA_QBLK = 128
MLA_QBLK = 256
CTX_PER_STEP = 4
HG_CTX_PER_STEP = 2
OUT_ROWS = 2 * ROW_TILE
VMEM_LIMIT = 56 * 1024 * 1024


def _dot(a, b):
    return jnp.dot(a, b, preferred_element_type=F32)


def _dot_nt(a, b):
    return lax.dot_general(a, b, (((1,), (1,)), ((), ())), preferred_element_type=F32)


def _dot_tn(a, b):
    return lax.dot_general(a, b, (((0,), (0,)), ((), ())), preferred_element_type=F32)


def _split3(x):
    hi = x.astype(BF16)
    r1 = x - hi.astype(F32)
    mid = r1.astype(BF16)
    return hi, mid, (r1 - mid.astype(F32)).astype(BF16)


def _dot_exact_lhs(a, b):
    ab = a.astype(BF16)
    hi, mid, lo = _split3(b)
    return (_dot(ab, lo) + _dot(ab, mid)) + _dot(ab, hi)


def _dot_exact_rhs(a, b):
    bb = b.astype(BF16)
    hi, mid, lo = _split3(a)
    return (_dot(lo, bb) + _dot(mid, bb)) + _dot(hi, bb)


def _rms(x, g):
    return x * lax.rsqrt(jnp.mean(x * x, axis=-1, keepdims=True) + EPS) * g


def _silu(x):
    return x * jax.nn.sigmoid(x)


def _mod_row(i):
    return jnp.where(i < CTX_TILES, 0, 1 + (i - CTX_TILES) // LAT_TILES_PER_BATCH)


def _params(*sem):
    return pltpu.CompilerParams(dimension_semantics=sem, vmem_limit_bytes=VMEM_LIMIT)


ADA_COLS = 1536


def _ada_kernel(cv_ref, w_ref, b_ref, o_ref):
    a = _silu(cv_ref[...]).astype(BF16)
    o_ref[...] = _dot(a, w_ref[...].astype(BF16)) + b_ref[...]


def _ada(cv8, w_ada, b_ada):
    return pl.pallas_call(
        _ada_kernel,
        grid=(DEPTH, 6 * D_MODEL // ADA_COLS),
        in_specs=[
            pl.BlockSpec((8, D_MODEL), lambda l, j: (0, 0)),
            pl.BlockSpec((None, D_MODEL, ADA_COLS), lambda l, j: (l, 0, j)),
            pl.BlockSpec((None, 1, ADA_COLS), lambda l, j: (l, 0, j)),
        ],
        out_specs=pl.BlockSpec((None, 8, ADA_COLS), lambda l, j: (l, 0, j)),
        out_shape=jax.ShapeDtypeStruct((DEPTH, 8, 6 * D_MODEL), F32),
        compiler_params=_params("arbitrary", "arbitrary"),
        name="ada",
    )(cv8, w_ada, b_ada.reshape(DEPTH, 1, 6 * D_MODEL))


def _ctx_lat_specs(width, tile=ROW_TILE):
    n_ctx = N_CTX // tile
    return [pl.BlockSpec((tile, width), lambda i, *_: (jnp.minimum(i, n_ctx - 1), 0)),
            pl.BlockSpec((tile, width), lambda i, *_: (jnp.maximum(i - n_ctx, 0), 0))]


def _row_specs(parts, width, tile=ROW_TILE):
    if len(parts) == 2:
        return _ctx_lat_specs(width, tile)
    return [pl.BlockSpec((tile, width), lambda i, *_: (i, 0))]


def _read_rows(refs, tile=ROW_TILE):
    if len(refs) == 1:
        return refs[0][...]
    return jnp.where(pl.program_id(0) < N_CTX // tile, refs[0][...], refs[1][...])


def _in_kernel(*refs, n_x):
    x = _read_rows(refs[:n_x], OUT_ROWS)
    mod_ref, g_ref, w_ref, umla_ref, uhg_ref, ufn_ref, uswa_ref = refs[n_x:]
    sh1 = mod_ref[:, 0:D_MODEL]
    sc1 = mod_ref[:, D_MODEL:2 * D_MODEL]
    h = _rms(x, g_ref[...]) * (1.0 + sc1) + sh1
    u = _dot(h.astype(BF16), w_ref[...])
    o = 0
    for ref, width in ((umla_ref, MLA_PACK), (uhg_ref, HG_IN), (ufn_ref, FN_IN), (uswa_ref, SWA_IN)):
        ref[...] = u[:, o:o + width].astype(ref.dtype)
        o += width


def _in_proj(x_parts, mod, g, w):
    row = lambda i: (i, 0)
    widths = (MLA_PACK, HG_IN, FN_IN, SWA_IN)
    return pl.pallas_call(
        functools.partial(_in_kernel, n_x=len(x_parts)),
        grid=(N_TOK // OUT_ROWS,),
        in_specs=_row_specs(x_parts, D_MODEL, OUT_ROWS) + [
            pl.BlockSpec((None, 1, 6 * D_MODEL), lambda i: (_mod_row(i * (OUT_ROWS // ROW_TILE)), 0, 0)),
            pl.BlockSpec((1, D_MODEL), lambda i: (0, 0)),
            pl.BlockSpec((D_MODEL, U_COLS), lambda i: (0, 0))],
        out_specs=[pl.BlockSpec((OUT_ROWS, wd), row) for wd in widths],
        out_shape=[jax.ShapeDtypeStruct((N_TOK, wd), F32) for wd in widths],
        compiler_params=_params("arbitrary"),
        name="in_proj",
    )(*x_parts, mod, g, w)


def _rope(x, cos, sin_signed):
    lane = lax.broadcasted_iota(jnp.int32, x.shape, 1)
    width = x.shape[1]
    swapped = jnp.where(lane % 2 == 0, pltpu.roll(x, width - 1, 1), pltpu.roll(x, 1, 1))
    return x * cos + swapped * sin_signed


def _stack_heads(x, n_heads, head_w):
    lane = lax.broadcasted_iota(jnp.int32, x.shape, 1)
    return jnp.concatenate([jnp.where(lane // head_w == h, x, 0.0) for h in range(n_heads)], axis=0)


def _unstack_heads(o, n_heads, head_w):
    t = o.shape[0] // n_heads
    lane = lax.broadcasted_iota(jnp.int32, (t, o.shape[1]), 1)
    out = jnp.zeros((t, o.shape[1]), F32)
    for h in range(n_heads):
        out = jnp.where(lane // head_w == h, o[h * t:(h + 1) * t], out)
    return out


MLA_SCALE = (MLA_NOPE + MLA_ROPE) ** -0.5
MLA_QW = MLA_HEADS * MLA_NOPE + MLA_HEADS * MLA_ROPE
MLA_NW = MLA_HEADS * MLA_NOPE


def _mla_attend(q, kcat, v):
    qs = jnp.concatenate([_stack_heads(q[:, :MLA_NW], MLA_HEADS, MLA_NOPE),
                          _stack_heads(q[:, MLA_NW:], MLA_HEADS, MLA_ROPE)], axis=1)
    s = _dot_nt(qs.astype(BF16), kcat) * MLA_SCALE
    p = jnp.exp(s - jnp.max(s, axis=-1, keepdims=True))
    o = _dot(p.astype(BF16), v) / jnp.sum(p, axis=-1, keepdims=True)
    return _unstack_heads(o, MLA_HEADS, MLA_V)


def _mla_ctx_kernel(u_ref, qg_ref, wq_ref, kvg_ref, wkv_ref, o_ref, ckv_ref):
    for j in range(CTX_PER_STEP):
        rows = slice(j * SEQ, (j + 1) * SEQ)
        u = u_ref[rows, :]
        q = _dot(_rms(u[:, :MLA_Q_LORA], qg_ref[...]).astype(BF16), wq_ref[...])
        ckv = _rms(u[:, MLA_Q_LORA:MLA_Q_LORA + MLA_KV_LORA], kvg_ref[...])
        ckv_ref[rows, :] = ckv
        kv = _dot(ckv.astype(BF16), wkv_ref[...])
        kr4 = u[:, MLA_Q_LORA + MLA_KV_LORA:]
        kcat = jnp.concatenate([kv[:, :MLA_NW], kr4], axis=1).astype(BF16)
        o_ref[rows, :] = _mla_attend(q, kcat, kv[:, MLA_NW:].astype(BF16)).astype(o_ref.dtype)


def _mla_ctx(u_mla, qg, wq, kvg, wkv):
    full = lambda shape: pl.BlockSpec(shape, lambda b: (0, 0))
    rows = CTX_PER_STEP * SEQ
    return pl.pallas_call(
        _mla_ctx_kernel,
        grid=(BATCH // CTX_PER_STEP,),
        in_specs=[pl.BlockSpec((rows, MLA_PACK), lambda b: (b, 0)),
                  full((1, MLA_Q_LORA)), full((MLA_Q_LORA, MLA_QW)),
                  full((1, MLA_KV_LORA)), full((MLA_KV_LORA, 2 * MLA_NW))],
        out_specs=[pl.BlockSpec((rows, MLA_NW), lambda b: (b, 0)),
                   pl.BlockSpec((rows, MLA_KV_LORA), lambda b: (b, 0))],
        out_shape=[jax.ShapeDtypeStruct((N_CTX, MLA_NW), BF16),
                   jax.ShapeDtypeStruct((N_CTX, MLA_KV_LORA), F32)],
        compiler_params=_params("arbitrary"),
        name="mla_ctx",
    )(u_mla, qg, wq, kvg, wkv)


MLA_TK = PAST_LEN + DEC_SEQ


def _mla_lat_kernel(u_ref, cckv_ref, ckr_ref, cos_ref, sin_ref, qg_ref, wq_ref, kvg_ref, wkv_ref,
                    o_ref, kcat_s, v_s):
    i = pl.program_id(1)

    @pl.when(i == 0)
    def _():
        u = u_ref[...]
        ckv_new = _rms(u[:, MLA_Q_LORA:MLA_Q_LORA + MLA_KV_LORA], kvg_ref[...])
        ckv_all = jnp.concatenate([cckv_ref[...], ckv_new], axis=0)
        kv = _dot(ckv_all.astype(BF16), wkv_ref[...])
        kr_new = _rope(u[:, MLA_Q_LORA + MLA_KV_LORA:], cos_ref[...], sin_ref[...])
        ckr = ckr_ref[...]
        kr_all = jnp.concatenate([jnp.concatenate([ckr] * MLA_HEADS, axis=1), kr_new], axis=0)
        kcat_s[...] = jnp.concatenate([kv[:, :MLA_NW], kr_all], axis=1).astype(BF16)
        v_s[...] = kv[:, MLA_NW:].astype(BF16)

    r0 = pl.multiple_of(i * MLA_QBLK, MLA_QBLK)
    cq = u_ref[pl.ds(r0, MLA_QBLK), 0:MLA_Q_LORA]
    q = _dot(_rms(cq, qg_ref[...]).astype(BF16), wq_ref[...])
    qr = _rope(q[:, MLA_NW:], cos_ref[pl.ds(r0, MLA_QBLK), :], sin_ref[pl.ds(r0, MLA_QBLK), :])
    q = jnp.concatenate([q[:, :MLA_NW], qr], axis=1)
    o_ref[...] = _mla_attend(q, kcat_s[...], v_s[...]).astype(o_ref.dtype)


def _mla_lat(u_mla, cache_ckv, cache_kr, l, cos, sin, qg, wq, kvg, wkv):
    full = lambda shape: pl.BlockSpec(shape, lambda b, i: (0, 0))
    nq = DEC_SEQ // MLA_QBLK
    return pl.pallas_call(
        _mla_lat_kernel,
        grid=(DEC_BATCH, nq),
        in_specs=[pl.BlockSpec((DEC_SEQ, MLA_PACK), lambda b, i: (N_CTX // DEC_SEQ + b, 0)),
                  pl.BlockSpec((None, None, PAST_LEN, MLA_KV_LORA), lambda b, i: (b, l, 0, 0)),
                  pl.BlockSpec((None, None, PAST_LEN, MLA_ROPE), lambda b, i: (b, l, 0, 0)),
                  full((DEC_SEQ, MLA_HEADS * MLA_ROPE)), full((DEC_SEQ, MLA_HEADS * MLA_ROPE)),
                  full((1, MLA_Q_LORA)), full((MLA_Q_LORA, MLA_QW)),
                  full((1, MLA_KV_LORA)), full((MLA_KV_LORA, 2 * MLA_NW))],
        out_specs=pl.BlockSpec((MLA_QBLK, MLA_NW), lambda b, i: (b * nq + i, 0)),
        out_shape=jax.ShapeDtypeStruct((N_LAT, MLA_NW), BF16),
        scratch_shapes=[pltpu.VMEM((MLA_TK, MLA_QW), BF16), pltpu.VMEM((MLA_TK, MLA_NW), BF16)],
        compiler_params=_params("arbitrary", "arbitrary"),
        name="mla_lat",
    )(u_mla, cache_ckv, cache_kr, cos, sin, qg, wq, kvg, wkv)


def _hgrn_kernel(*refs, seq, n_seq, has_state):
    if has_state:
        (u_ref, lbf_ref, lbb_ref, ng_ref, s0_ref, o_ref,
         q_s, kf_s, gf_s, kb_s, gb_s, of_s, ob_s, stf_s, stb_s) = refs
    else:
        (u_ref, lbf_ref, lbb_ref, ng_ref, o_ref, so_ref,
         q_s, kf_s, gf_s, kb_s, gb_s, of_s, ob_s, stf_s, stb_s) = refs
    C = HG_CHUNK
    W = HG_W

    q_s[...] = _silu(u_ref[:, 0:W])
    ff = lbf_ref[...] + (1.0 - lbf_ref[...]) * jax.nn.sigmoid(u_ref[:, W:2 * W])
    kf_s[...] = 1.0 - ff
    gf_s[...] = jnp.log(ff)
    fb = lbb_ref[...] + (1.0 - lbb_ref[...]) * jax.nn.sigmoid(u_ref[:, 2 * W:3 * W])
    kb_s[...] = 1.0 - fb
    gb_s[...] = jnp.log(fb)

    rr = lax.broadcasted_iota(jnp.int32, (W, W), 0)
    cc = lax.broadcasted_iota(jnp.int32, (W, W), 1)
    blockdiag = rr // HG_DK == cc // HG_DK
    if has_state:
        for st, d in ((stf_s, 0), (stb_s, 1)):
            rows = []
            for h in range(HG_HEADS):
                z = lambda n: jnp.zeros((HG_DV, n * HG_DK), F32)
                parts = ([z(h)] if h else []) + [s0_ref[d, h]] + ([z(HG_HEADS - 1 - h)] if h < HG_HEADS - 1 else [])
                rows.append(jnp.concatenate(parts, axis=1) if len(parts) > 1 else parts[0])
            st[0] = jnp.concatenate(rows, axis=0)
    else:
        stf_s[...] = jnp.zeros_like(stf_s)
        stb_s[...] = jnp.zeros_like(stb_s)

    B = HG_BLOCK
    per_block = B // C
    n_blocks = seq // B
    ri = lax.broadcasted_iota(jnp.int32, (B, B), 0)
    ci = lax.broadcasted_iota(jnp.int32, (B, B), 1)
    same_chunk = ri // C == ci // C
    rs = lax.broadcasted_iota(jnp.int32, (HG_HEADS * B, B), 0) % B
    cs = lax.broadcasted_iota(jnp.int32, (HG_HEADS * B, B), 1)
    same_chunk_s = rs // C == cs // C

    def sums(to_end, to_mid):
        f = lambda m: jnp.where(same_chunk & m, 1.0, 0.0)
        whole = jnp.where(same_chunk, 1.0, 0.0)
        return jnp.concatenate([f(to_end), f(to_end) - f(to_mid), whole - f(to_end)], axis=0)

    mid_f = (ri // C) * C + (C // 2 - 1)
    mid_b = (ri // C) * C + C // 2
    sums_f = sums(ci <= ri, ci <= mid_f)
    sums_b = sums(ci >= ri, ci >= mid_b)
    keep_f = same_chunk_s & (rs >= cs)
    keep_b = same_chunk_s & (cs >= rs)

    def block(r, k_s, g_s, o_s, st_s, sum_mat, keep, order):
        q = q_s[pl.ds(r, B), :]
        k = k_s[pl.ds(r, B), :]
        v = u_ref[pl.ds(r, B), 3 * W:4 * W].astype(BF16)
        gs = _dot_exact_lhs(sum_mat, g_s[pl.ds(r, B), :])
        G, Gq, Gk2 = gs[:B], gs[B:2 * B], gs[2 * B:]
        qe = _stack_heads(q * jnp.exp(Gq), HG_HEADS, HG_DK)
        ke = k * jnp.exp(-Gq)
        A = jnp.where(keep, _dot_nt(qe.astype(BF16), ke.astype(BF16)), 0.0)
        o_intra = _unstack_heads(_dot(A.astype(BF16), v), HG_HEADS, HG_DV)
        qg = (q * jnp.exp(G)).astype(BF16)
        k2 = (k * jnp.exp(Gk2)).astype(BF16)
        decay = jnp.exp(G + Gk2)
        st = st_s[...]
        o_inter = [None] * per_block
        for c in order:
            rows = slice(c * C, (c + 1) * C)
            o_inter[c] = _dot_nt(st.astype(BF16), qg[rows])
            st = st * decay[c * C:c * C + 1] + jnp.where(blockdiag, _dot_tn(v[rows], k2[rows]), 0.0)
        st_s[...] = st
        o_s[pl.ds(r, B), :] = o_intra + jnp.concatenate(o_inter, axis=1).T

    def fwd(j, r):
        block(r, kf_s, gf_s, of_s, stf_s.at[j], sums_f, keep_f, range(per_block))

    def bwd(j, r):
        block(r, kb_s, gb_s, ob_s, stb_s.at[j], sums_b, keep_b, range(per_block - 1, -1, -1))

    for j in range(n_seq):
        if n_blocks == 1:
            fwd(j, j * seq)
            bwd(j, j * seq)
        else:
            lax.fori_loop(0, n_blocks, lambda i, c, j=j: (fwd(j, pl.multiple_of(j * seq + i * B, B)), c)[1], 0)
            lax.fori_loop(0, n_blocks,
                          lambda i, c, j=j: (bwd(j, pl.multiple_of(j * seq + (n_blocks - 1 - i) * B, B)), c)[1], 0)

    o = of_s[...] + ob_s[...]
    ms = _dot_exact_rhs(o * o, jnp.where(blockdiag, 1.0 / HG_DV, 0.0))
    on = o * lax.rsqrt(ms + EPS) * ng_ref[...]
    o_ref[...] = (on * _silu(u_ref[:, 4 * W:5 * W])).astype(o_ref.dtype)

    if not has_state:
        for j in range(n_seq):
            for st, d in ((stf_s, 0), (stb_s, 1)):
                for h in range(HG_HEADS):
                    so_ref[j, d, h] = st[j, h * HG_DV:(h + 1) * HG_DV, h * HG_DK:(h + 1) * HG_DK]


def _hgrn(u_hg, lbf, lbb, ng4, state_t, *, seq, n_batch, row_block0):
    has_state = state_t is not None
    n_seq = 1 if has_state else HG_CTX_PER_STEP
    rows = n_seq * seq
    full = lambda shape: pl.BlockSpec(shape, lambda b: (0, 0))
    in_specs = [pl.BlockSpec((rows, HG_IN), lambda b: (row_block0 + b, 0)),
                full((1, HG_W)), full((1, HG_W)), full((1, HG_W))]
    args = [u_hg, lbf, lbb, ng4]
    o_spec = pl.BlockSpec((rows, HG_W), lambda b: (b, 0))
    o_shape = jax.ShapeDtypeStruct((n_batch * seq, HG_W), BF16)
    if has_state:
        in_specs.append(pl.BlockSpec((None, 2, HG_HEADS, HG_DV, HG_DK), lambda b: (b, 0, 0, 0, 0)))
        args.append(state_t)
        out_specs, out_shape = o_spec, o_shape
    else:
        out_specs = [o_spec, pl.BlockSpec((n_seq, 2, HG_HEADS, HG_DV, HG_DK), lambda b: (b, 0, 0, 0, 0))]
        out_shape = [o_shape, jax.ShapeDtypeStruct((n_batch, 2, HG_HEADS, HG_DV, HG_DK), F32)]
    return pl.pallas_call(
        functools.partial(_hgrn_kernel, seq=seq, n_seq=n_seq, has_state=has_state),
        grid=(n_batch // n_seq,),
        in_specs=in_specs, out_specs=out_specs, out_shape=out_shape,
        scratch_shapes=[pltpu.VMEM((rows, HG_W), F32)] * 7 + [pltpu.VMEM((n_seq, HG_W, HG_W), F32)] * 2,
        compiler_params=_params("arbitrary"),
        name="hgrn_lat" if has_state else "hgrn_ctx",
    )(*args)


def _dft_tables(n):
    j = np.arange(n, dtype=np.int64)
    ang = 2.0 * np.pi * ((j[:, None] * j[None, :]) % n).astype(np.float64) / n
    return np.cos(ang) / np.sqrt(n), np.sin(ang) / np.sqrt(n)


def _fourier_tables(seq):
    gw = FN_WIDTH // FN_GROUPS
    cg, sg = _dft_tables(gw)
    eye = np.eye(FN_GROUPS)
    chan = np.concatenate([np.kron(eye, cg), np.kron(eye, sg)], axis=1)
    ct, st = _dft_tables(seq)
    pos = np.concatenate([ct, -st], axis=1)
    return jnp.asarray(chan, F32).astype(BF16), jnp.asarray(pos, F32).astype(BF16)


def _fourier_kernel(x_ref, chan_ref, pos_ref, w_ref, o_ref, *, seq, n_seq):
    for j in range(n_seq):
        rows = slice(j * seq, (j + 1) * seq)
        x12 = _dot(x_ref[rows, :].astype(BF16), chan_ref[...])
        z = jnp.concatenate([x12[:, :FN_WIDTH], x12[:, FN_WIDTH:]], axis=0).astype(BF16)
        y = _dot(pos_ref[...], z)
        o_ref[rows, :] = _dot(y.astype(BF16), w_ref[...]).astype(o_ref.dtype)


def _fourier(u_fn, w, *, seq, n_batch, row_block0):
    chan, pos = _fourier_tables(seq)
    full = lambda shape: pl.BlockSpec(shape, lambda b: (0, 0))
    n_seq = CTX_PER_STEP if seq == SEQ else 1
    return pl.pallas_call(
        functools.partial(_fourier_kernel, seq=seq, n_seq=n_seq),
        grid=(n_batch // n_seq,),
        in_specs=[pl.BlockSpec((n_seq * seq, FN_WIDTH), lambda b: (row_block0 + b, 0)),
                  full((FN_WIDTH, 2 * FN_WIDTH)), full((seq, 2 * seq)), full((FN_WIDTH, FN_WIDTH))],
        out_specs=pl.BlockSpec((n_seq * seq, FN_WIDTH), lambda b: (b, 0)),
        out_shape=jax.ShapeDtypeStruct((n_batch * seq, FN_WIDTH), BF16),
        compiler_params=_params("arbitrary"),
        name="fourier",
    )(u_fn, chan, pos, w)


SWA_SCALE = SWA_HD ** -0.5
SWA_QW = SWA_HEADS * SWA_HD
SWA_KW = SWA_KV_HEADS * SWA_HD
SWA_STACK_ORDER = (0, 2, 1, 3)


def _swa_stack_q(q):
    return jnp.concatenate([_stack_heads(q[:, :SWA_KW], SWA_KV_HEADS, SWA_HD),
                            _stack_heads(q[:, SWA_KW:], SWA_KV_HEADS, SWA_HD)], axis=0)


def _swa_unstack_o(o):
    t = o.shape[0] // SWA_HEADS
    return jnp.concatenate([_unstack_heads(o[:2 * t], SWA_KV_HEADS, SWA_HD),
                            _unstack_heads(o[2 * t:], SWA_KV_HEADS, SWA_HD)], axis=1)


def _sink_rows(sink_ref, t):
    return jnp.concatenate([jnp.full((t, 1), sink_ref[h], F32) for h in SWA_STACK_ORDER], axis=0)


def _swa_ctx_kernel(sink_ref, u_ref, o_ref):
    sink = _sink_rows(sink_ref, SEQ)
    for j in range(CTX_PER_STEP):
        rows = slice(j * SEQ, (j + 1) * SEQ)
        u = u_ref[rows, :]
        qs = _swa_stack_q(u[:, :SWA_QW]).astype(BF16)
        k = u[:, SWA_QW:SWA_QW + SWA_KW].astype(BF16)
        v = u[:, SWA_QW + SWA_KW:].astype(BF16)
        s = _dot_nt(qs, k) * SWA_SCALE
        m = jnp.maximum(jnp.max(s, axis=-1, keepdims=True), sink)
        p = jnp.exp(s - m)
        denom = jnp.sum(p, axis=-1, keepdims=True) + jnp.exp(sink - m)
        o_ref[rows, :] = _swa_unstack_o(_dot(p.astype(BF16), v) / denom).astype(o_ref.dtype)


def _swa_ctx(u_swa, sink):
    rows = CTX_PER_STEP * SEQ
    return pl.pallas_call(
        _swa_ctx_kernel,
        grid=(BATCH // CTX_PER_STEP,),
        in_specs=[pl.BlockSpec(memory_space=pltpu.SMEM),
                  pl.BlockSpec((rows, SWA_IN), lambda b: (b, 0))],
        out_specs=pl.BlockSpec((rows, SWA_QW), lambda b: (b, 0)),
        out_shape=jax.ShapeDtypeStruct((N_CTX, SWA_QW), BF16),
        compiler_params=_params("arbitrary"),
        name="swa_ctx",
    )(sink, u_swa)


SWA_PAD = DEC_SEQ + 2 * SWA_QBLK


def _swa_lat_kernel(sink_ref, u_ref, kc_ref, vc_ref, cosq_ref, sinq_ref, cosk_ref, sin_k_ref,
                    o_ref, k_s, v_s):
    i = pl.program_id(1)
    B = SWA_QBLK

    @pl.when(i == 0)
    def _():
        zeros = jnp.zeros((B, SWA_KW), BF16)
        k = _rope(u_ref[:, SWA_QW:SWA_QW + SWA_KW], cosk_ref[...], sin_k_ref[...]).astype(BF16)
        k_s[...] = jnp.concatenate([zeros, k, zeros], axis=0)
        v_s[...] = jnp.concatenate([zeros, u_ref[:, SWA_QW + SWA_KW:].astype(BF16), zeros], axis=0)

    r0 = pl.multiple_of(i * B, B)
    q = _rope(u_ref[pl.ds(r0, B), 0:SWA_QW], cosq_ref[pl.ds(r0, B), :], sinq_ref[pl.ds(r0, B), :])
    qs = _swa_stack_q(q).astype(BF16)
    s_loc = _dot_nt(qs, k_s[pl.ds(r0, 3 * B), :]) * SWA_SCALE
    row = lax.broadcasted_iota(jnp.int32, s_loc.shape, 0) % B
    col = lax.broadcasted_iota(jnp.int32, s_loc.shape, 1)
    kpos = r0 - B + col
    valid = (jnp.abs(row + B - col) <= WINDOW) & (kpos >= 0) & (kpos < DEC_SEQ)
    s_loc = jnp.where(valid, s_loc, NEG_INF)
    s_ctx = _dot_nt(qs, kc_ref[...].astype(BF16)) * SWA_SCALE
    sink = _sink_rows(sink_ref, B)
    m = jnp.maximum(jnp.maximum(jnp.max(s_loc, axis=-1, keepdims=True),
                                jnp.max(s_ctx, axis=-1, keepdims=True)), sink)
    p_loc = jnp.exp(s_loc - m)
    p_ctx = jnp.exp(s_ctx - m)
    denom = (jnp.sum(p_loc, axis=-1, keepdims=True) + jnp.sum(p_ctx, axis=-1, keepdims=True)
             + jnp.exp(sink - m))
    o = _dot(p_loc.astype(BF16), v_s[pl.ds(r0, 3 * B), :]) + _dot(p_ctx.astype(BF16), vc_ref[...].astype(BF16))
    o_ref[...] = _swa_unstack_o(o / denom).astype(o_ref.dtype)


def _swa_lat(u_swa, cache_k, cache_v, l, sink, cosq, sinq, cosk, sink_k):
    full = lambda shape: pl.BlockSpec(shape, lambda b, i: (0, 0))
    nq = DEC_SEQ // SWA_QBLK
    cache_spec = pl.BlockSpec((None, None, PAST_LEN, SWA_KW), lambda b, i: (b, l, 0, 0))
    return pl.pallas_call(
        _swa_lat_kernel,
        grid=(DEC_BATCH, nq),
        in_specs=[pl.BlockSpec(memory_space=pltpu.SMEM),
                  pl.BlockSpec((DEC_SEQ, SWA_IN), lambda b, i: (N_CTX // DEC_SEQ + b, 0)),
                  cache_spec, cache_spec,
                  full((DEC_SEQ, SWA_QW)), full((DEC_SEQ, SWA_QW)),
                  full((DEC_SEQ, SWA_KW)), full((DEC_SEQ, SWA_KW))],
        out_specs=pl.BlockSpec((SWA_QBLK, SWA_QW), lambda b, i: (b * nq + i, 0)),
        out_shape=jax.ShapeDtypeStruct((N_LAT, SWA_QW), BF16),
        scratch_shapes=[pltpu.VMEM((SWA_PAD, SWA_KW), BF16), pltpu.VMEM((SWA_PAD, SWA_KW), BF16)],
        compiler_params=_params("arbitrary", "arbitrary"),
        name="swa_lat",
    )(sink, u_swa, cache_k, cache_v, cosq, sinq, cosk, sink_k)


N_BLK = N_TOK // ROW_TILE
SEG_ALIGN = 16
LOCAL_ROWS = ROW_TILE * TOP_K + N_EXPERTS * SEG_ALIGN
LOCAL_CHUNK = 512
EXPERT_TILE = 768
SORTED_ROWS = -(-(N_TOK * TOP_K + N_BLK * N_EXPERTS * SEG_ALIGN + N_EXPERTS * (EXPERT_TILE + SEG_ALIGN))
                // EXPERT_TILE) * EXPERT_TILE
MAX_TILES = SORTED_ROWS // EXPERT_TILE
NOT_PICKED = -1.0
NO_DEST = 4095.0


def _out_kernel(*refs, n_x):
    x_all = _read_rows(refs[:n_x], OUT_ROWS)
    mix_all = [_read_rows(refs[n_x + 2 * j:n_x + 2 * j + 2], OUT_ROWS) for j in range(4)]
    out_refs = refs[n_x + 8:]
    for blk in range(OUT_ROWS // ROW_TILE):
        rows = slice(blk * ROW_TILE, (blk + 1) * ROW_TILE)
        _out_block(x_all[rows], [m[rows] for m in mix_all], out_refs, blk)


def _out_block(x, mixers, refs, blk):
    (mod_ref, g_ref, wo_ref, wrt_ref, br_ref, wsg_ref, wsu_ref, wsd_ref,
     x1_ref, h_ref, gate_ref, rank_ref, cnt_ref) = refs
    rows = slice(blk * ROW_TILE, (blk + 1) * ROW_TILE)
    mix = jnp.zeros((ROW_TILE, D_MODEL), F32)
    for j in range(4):
        mix = mix + _dot(mixers[j], wo_ref[j * 256:(j + 1) * 256, :])
    g1 = mod_ref[:, 2 * D_MODEL:3 * D_MODEL]
    sh2 = mod_ref[:, 3 * D_MODEL:4 * D_MODEL]
    sc2 = mod_ref[:, 4 * D_MODEL:5 * D_MODEL]
    g2 = mod_ref[:, 5 * D_MODEL:6 * D_MODEL]
    x1 = x + g1 * mix
    h = _rms(x1, g_ref[...]) * (1.0 + sc2) + sh2
    hb = h.astype(BF16)
    h_ref[rows, :] = hb

    logits = lax.dot_general(wrt_ref[...], h, (((1,), (1,)), ((), ())),
                             precision=lax.Precision.HIGHEST, preferred_element_type=F32)
    scores = jax.nn.sigmoid(logits)
    sel = scores + br_ref[...]
    eidx = lax.broadcasted_iota(jnp.int32, sel.shape, 0)
    gate = jnp.zeros_like(scores)
    picked = jnp.zeros_like(scores)
    for _ in range(TOP_K):
        best = jnp.max(sel, axis=0, keepdims=True)
        first = jnp.min(jnp.where(sel == best, eidx, N_EXPERTS), axis=0, keepdims=True)
        pick = eidx == first
        gate = jnp.where(pick, scores, gate)
        picked = jnp.where(pick, 1.0, picked)
        sel = jnp.where(pick, -jnp.inf, sel)
    gate = ROUTE_SCALE * gate / jnp.sum(gate, axis=0, keepdims=True)

    ti = lax.broadcasted_iota(jnp.int32, (ROW_TILE, ROW_TILE), 0)
    tj = lax.broadcasted_iota(jnp.int32, (ROW_TILE, ROW_TILE), 1)
    pb = picked.astype(BF16)
    rank = _dot(pb, jnp.where(ti < tj, 1.0, 0.0).astype(BF16))
    gate_ref[:, rows] = gate
    rank_ref[:, rows] = jnp.where(picked > 0.0, rank, NOT_PICKED)
    counts = _dot_nt(jnp.ones((8, ROW_TILE), BF16), pb)
    cnt_ref[blk] = jnp.concatenate([counts, jnp.zeros_like(counts)], axis=1)

    hid = _silu(_dot(hb, wsg_ref[...])) * _dot(hb, wsu_ref[...])
    x1_ref[rows, :] = x1 + g2 * _dot(hid.astype(BF16), wsd_ref[...])


def _out_proj(x_parts, mixer_pairs, mod, g, wo, wrt, br, wsg, wsu, wsd):
    row = lambda i: (i, 0)
    col = lambda i: (0, i)
    full = lambda shape: pl.BlockSpec(shape, lambda i: (0, 0))
    per_step = OUT_ROWS // ROW_TILE
    et_spec = pl.BlockSpec((N_EXPERTS, OUT_ROWS), col)
    et_shape = jax.ShapeDtypeStruct((N_EXPERTS, N_TOK), F32)
    return pl.pallas_call(
        functools.partial(_out_kernel, n_x=len(x_parts)),
        grid=(N_TOK // OUT_ROWS,),
        in_specs=_row_specs(x_parts, D_MODEL, OUT_ROWS) + 4 * _ctx_lat_specs(256, OUT_ROWS) + [
            pl.BlockSpec((None, 1, 6 * D_MODEL), lambda i: (_mod_row(i * per_step), 0, 0)),
            full((1, D_MODEL)), full((D_MODEL, D_MODEL)),
            full((N_EXPERTS, D_MODEL)), full((N_EXPERTS, 1)),
            full((D_MODEL, D_SHARED)), full((D_MODEL, D_SHARED)), full((D_SHARED, D_MODEL))],
        out_specs=[pl.BlockSpec((OUT_ROWS, D_MODEL), row), pl.BlockSpec((OUT_ROWS, D_MODEL), row),
                   et_spec, et_spec,
                   pl.BlockSpec((per_step, 8, 128), lambda i: (i, 0, 0))],
        out_shape=[jax.ShapeDtypeStruct((N_TOK, D_MODEL), F32),
                   jax.ShapeDtypeStruct((N_TOK, D_MODEL), BF16),
                   et_shape, et_shape,
                   jax.ShapeDtypeStruct((N_BLK, 8, 128), F32)],
        compiler_params=_params("arbitrary"),
        name="out_proj",
    )(*x_parts, *[a for pair in mixer_pairs for a in pair], mod, g, wo, wrt, br, wsg, wsu, wsd)


def _segment_plan(cnt):
    cnt = cnt[:, 0, :N_EXPERTS].astype(jnp.int32)
    seg = jnp.maximum((cnt + (SEG_ALIGN - 1)) // SEG_ALIGN, 1) * SEG_ALIGN
    local = jnp.cumsum(seg, axis=1) - seg
    total = jnp.sum(seg, axis=1)
    per_expert = jnp.sum(seg, axis=0)
    padded = (per_expert + SEG_ALIGN + (EXPERT_TILE - 1)) // EXPERT_TILE * EXPERT_TILE
    ends = jnp.cumsum(padded)
    start = ends - padded
    dest = start[None, :] + jnp.cumsum(seg, axis=0) - seg
    n_tiles = ends[-1] // EXPERT_TILE
    tiles = jnp.arange(MAX_TILES, dtype=jnp.int32)
    tile_expert = jnp.sum((ends // EXPERT_TILE)[None, :] <= jnp.minimum(tiles, n_tiles - 1)[:, None], axis=1)
    tile_expert = tile_expert.astype(jnp.int32)
    plan = dict(seg=seg, local=local, total=total.astype(jnp.int32), dest=dest.astype(jnp.int32),
                tail_start=(start + per_expert).astype(jnp.int32), tail_rows=(padded - per_expert).astype(jnp.int32),
                n_tiles=n_tiles.reshape(1).astype(jnp.int32), tile_expert=tile_expert)
    segf, localf = seg.astype(F32), local.astype(F32)
    pad_lanes = lambda a: jnp.concatenate([a, jnp.zeros_like(a)], axis=1)[:, None, :]
    plan.update(seg_row=pad_lanes(segf), local_row=pad_lanes(localf),
                seg_col=segf[:, :, None], local_col=localf[:, :, None])
    return plan


def _local_dest_digits(rank, local_start):
    dest = jnp.where(rank >= 0.0, local_start + rank, NO_DEST)
    hi = jnp.floor(dest * (1.0 / 64.0))
    return hi, dest - 64.0 * hi


def _dispatch_kernel(dest_ref, seg_ref, local_ref, total_ref, tail_start_ref, tail_rows_ref, nt_ref,
                     h_ref, rank_ref, local_col_ref, local_row_ref, seg_row_ref,
                     xs_hbm, buf, zeros, sems, zsem, usem):
    b = pl.program_id(0)
    slot = b % 2

    def wait_block(blk, s):
        n = pl.multiple_of(total_ref[blk], SEG_ALIGN)
        pltpu.make_async_copy(buf.at[s, pl.ds(0, n)], xs_hbm.at[pl.ds(0, n)], sems.at[s]).wait()

    @pl.when(b == 0)
    def _():
        zeros[...] = jnp.zeros_like(zeros)

    def unused_tiles(action):
        for k in range(-(-MAX_TILES // N_BLK)):
            t = nt_ref[0] + b + k * N_BLK

            @pl.when(t < MAX_TILES)
            def _():
                cp = pltpu.make_async_copy(zeros.at[pl.ds(0, EXPERT_TILE)],
                                           xs_hbm.at[pl.ds(pl.multiple_of(t * EXPERT_TILE, EXPERT_TILE),
                                                           EXPERT_TILE)], usem)
                cp.start() if action == "start" else cp.wait()

    unused_tiles("start")

    @pl.when(b >= 2)
    def _():
        wait_block(b - 2, slot)

    hi, lo = _local_dest_digits(rank_ref[...], local_col_ref[...])
    code = jnp.concatenate([hi, lo], axis=0).astype(BF16)
    hb = h_ref[...]
    lstart = local_row_ref[:, :N_EXPERTS]
    lend = lstart + seg_row_ref[:, :N_EXPERTS]
    for c in range(LOCAL_ROWS // LOCAL_CHUNK):
        r = (lax.broadcasted_iota(jnp.int32, (LOCAL_CHUNK, N_EXPERTS), 0) + c * LOCAL_CHUNK).astype(F32)
        member = (r >= lstart) & (r < lend)
        table = jnp.concatenate([jnp.where(member, 64.0, 0.0), jnp.where(member, 1.0, 0.0)], axis=1).astype(BF16)
        d = _dot(table, code)
        rr = (lax.broadcasted_iota(jnp.int32, (LOCAL_CHUNK, ROW_TILE), 0) + c * LOCAL_CHUNK).astype(F32)
        onehot = jnp.where(d == rr, 1.0, 0.0).astype(BF16)
        buf[slot, c * LOCAL_CHUNK:(c + 1) * LOCAL_CHUNK, :] = _dot(onehot, hb).astype(BF16)

    for e in range(N_EXPERTS):
        n = pl.multiple_of(seg_ref[b, e], SEG_ALIGN)
        src = pl.multiple_of(local_ref[b, e], SEG_ALIGN)
        dst = pl.multiple_of(dest_ref[b, e], SEG_ALIGN)
        pltpu.make_async_copy(buf.at[slot, pl.ds(src, n)], xs_hbm.at[pl.ds(dst, n)], sems.at[slot]).start()

    unused_tiles("wait")

    @pl.when(b == N_BLK - 1)
    def _():
        def tail(e):
            n = pl.multiple_of(tail_rows_ref[e], SEG_ALIGN)
            dst = pl.multiple_of(tail_start_ref[e], SEG_ALIGN)
            return pltpu.make_async_copy(zeros.at[pl.ds(0, n)], xs_hbm.at[pl.ds(dst, n)], zsem)

        for e in range(N_EXPERTS):
            tail(e).start()
        for e in range(N_EXPERTS):
            tail(e).wait()
        wait_block(b - 1, 1 - slot)
        wait_block(b, slot)


def _dispatch(h, rank_t, plan):
    blk = lambda shape, imap: pl.BlockSpec(shape, imap)
    return pl.pallas_call(
        _dispatch_kernel,
        grid_spec=pltpu.PrefetchScalarGridSpec(
            num_scalar_prefetch=7, grid=(N_BLK,),
            in_specs=[blk((ROW_TILE, D_MODEL), lambda b, *_: (b, 0)),
                      blk((N_EXPERTS, ROW_TILE), lambda b, *_: (0, b)),
                      blk((None, N_EXPERTS, 1), lambda b, *_: (b, 0, 0)),
                      blk((None, 1, 128), lambda b, *_: (b, 0, 0)),
                      blk((None, 1, 128), lambda b, *_: (b, 0, 0))],
            out_specs=pl.BlockSpec(memory_space=pl.ANY),
            scratch_shapes=[pltpu.VMEM((2, LOCAL_ROWS, D_MODEL), BF16),
                            pltpu.VMEM((EXPERT_TILE + SEG_ALIGN, D_MODEL), BF16),
                            pltpu.SemaphoreType.DMA((2,)), pltpu.SemaphoreType.DMA(()),
                            pltpu.SemaphoreType.DMA(())]),
        out_shape=jax.ShapeDtypeStruct((SORTED_ROWS, D_MODEL), BF16),
        compiler_params=_params("arbitrary"),
        name="dispatch",
    )(plan['dest'], plan['seg'], plan['local'], plan['total'], plan['tail_start'], plan['tail_rows'],
      plan['n_tiles'], h, rank_t, plan['local_col'], plan['local_row'], plan['seg_row'])


def _expert_kernel(te_ref, nt_ref, x_ref, wg_ref, wu_ref, wd_ref, y_ref, wg_s, wu_s, wd_s):
    i = pl.program_id(0)
    active = i < nt_ref[0]

    @pl.when((i == 0) | (te_ref[i] != te_ref[jnp.maximum(i - 1, 0)]))
    def _():
        wg_s[...] = wg_ref[...].astype(BF16)
        wu_s[...] = wu_ref[...].astype(BF16)
        wd_s[...] = wd_ref[...].astype(BF16)

    @pl.when(active)
    def _():
        x = x_ref[...]
        hid = _silu(_dot(x, wg_s[...])) * _dot(x, wu_s[...])
        y_ref[...] = _dot(hid.astype(BF16), wd_s[...]).astype(BF16)


def _experts(xs, plan, w_gate, w_up, w_down, l):
    rows = pl.BlockSpec((EXPERT_TILE, D_MODEL), lambda i, te, nt: (jnp.minimum(i, nt[0] - 1), 0))
    wspec = lambda shape: pl.BlockSpec((None, None) + shape, lambda i, te, nt: (l, te[i], 0, 0))
    return pl.pallas_call(
        _expert_kernel,
        grid_spec=pltpu.PrefetchScalarGridSpec(
            num_scalar_prefetch=2, grid=(MAX_TILES,),
            in_specs=[rows, wspec((D_MODEL, D_EXPERT)), wspec((D_MODEL, D_EXPERT)), wspec((D_EXPERT, D_MODEL))],
            out_specs=rows,
            scratch_shapes=[pltpu.VMEM((D_MODEL, D_EXPERT), BF16), pltpu.VMEM((D_MODEL, D_EXPERT), BF16),
                            pltpu.VMEM((D_EXPERT, D_MODEL), BF16)]),
        out_shape=jax.ShapeDtypeStruct((SORTED_ROWS, D_MODEL), BF16),
        input_output_aliases={2: 0},
        compiler_params=_params("arbitrary"),
        name="experts",
    )(plan['tile_expert'], plan['n_tiles'], xs, w_gate, w_up, w_down)


def _combine_kernel(*refs, final):
    refs = list(refs)
    dest_ref, seg_ref, local_ref, total_ref = refs[:4]
    x1_ref, ys_hbm, gate_ref, rank_ref, mod_ref, local_row_ref, local_col_ref, seg_col_ref = refs[4:12]
    rest = refs[12:]
    fg_ref = rest.pop(0) if final else None
    outs, (buf, sems) = rest[:-2], rest[-2:]
    b = pl.program_id(0)
    slot = b % 2

    def fetch(blk, s):
        for e in range(N_EXPERTS):
            n = pl.multiple_of(seg_ref[blk, e], SEG_ALIGN)
            src = pl.multiple_of(dest_ref[blk, e], SEG_ALIGN)
            dst = pl.multiple_of(local_ref[blk, e], SEG_ALIGN)
            pltpu.make_async_copy(ys_hbm.at[pl.ds(src, n)], buf.at[s, pl.ds(dst, n)], sems.at[s]).start()

    @pl.when(b == 0)
    def _():
        buf[...] = jnp.zeros_like(buf)
        fetch(0, 0)

    @pl.when(b + 1 < N_BLK)
    def _():
        fetch(b + 1, 1 - slot)

    n_rows = pl.multiple_of(total_ref[b], SEG_ALIGN)
    pltpu.make_async_copy(ys_hbm.at[pl.ds(0, n_rows)], buf.at[slot, pl.ds(0, n_rows)], sems.at[slot]).wait()

    gate = gate_ref[...].T
    hi, lo = _local_dest_digits(rank_ref[...].T, local_row_ref[:, :N_EXPERTS])
    lhs = jnp.concatenate([jnp.concatenate([hi, lo], axis=1),
                           jnp.concatenate([jnp.zeros_like(gate), gate], axis=1)], axis=0).astype(BF16)
    lstart = local_col_ref[...]
    lend = lstart + seg_col_ref[...]
    routed = jnp.zeros((ROW_TILE, D_MODEL), F32)
    for c in range(LOCAL_ROWS // LOCAL_CHUNK):
        r = (lax.broadcasted_iota(jnp.int32, (N_EXPERTS, LOCAL_CHUNK), 1) + c * LOCAL_CHUNK).astype(F32)
        member = (r >= lstart) & (r < lend)
        ones = jnp.where(member, 1.0, 0.0)
        table = jnp.concatenate([ones * 64.0, ones], axis=0).astype(BF16)
        dg = _dot(lhs, table)
        rr = (lax.broadcasted_iota(jnp.int32, (ROW_TILE, LOCAL_CHUNK), 1) + c * LOCAL_CHUNK).astype(F32)
        weights = jnp.where(dg[:ROW_TILE] == rr, dg[ROW_TILE:], 0.0).astype(BF16)
        routed = routed + _dot(weights, buf[slot, c * LOCAL_CHUNK:(c + 1) * LOCAL_CHUNK, :])
    x = x1_ref[...] + mod_ref[:, 5 * D_MODEL:6 * D_MODEL] * routed
    if final:
        y = _rms(x, fg_ref[...])

        @pl.when(b < CTX_TILES)
        def _():
            outs[0][...] = y

        @pl.when(b >= CTX_TILES)
        def _():
            outs[1][...] = y
    else:
        outs[0][...] = x


def _combine(x1, ys, gate_t, rank_t, mod, plan, final_g):
    final = final_g is not None
    blk = lambda shape, imap: pl.BlockSpec(shape, imap)
    xspec = blk((ROW_TILE, D_MODEL), lambda b, *_: (b, 0))
    if final:
        out_specs = _ctx_lat_specs(D_MODEL)
        out_shape = [jax.ShapeDtypeStruct((N_CTX, D_MODEL), F32), jax.ShapeDtypeStruct((N_LAT, D_MODEL), F32)]
    else:
        out_specs, out_shape = xspec, jax.ShapeDtypeStruct((N_TOK, D_MODEL), F32)
    et_spec = blk((N_EXPERTS, ROW_TILE), lambda b, *_: (0, b))
    in_specs = [xspec, pl.BlockSpec(memory_space=pl.ANY), et_spec, et_spec,
                blk((None, 1, 6 * D_MODEL), lambda b, *_: (_mod_row(b), 0, 0)),
                blk((None, 1, 128), lambda b, *_: (b, 0, 0)),
                blk((None, N_EXPERTS, 1), lambda b, *_: (b, 0, 0)),
                blk((None, N_EXPERTS, 1), lambda b, *_: (b, 0, 0))]
    args = [x1, ys, gate_t, rank_t, mod, plan['local_row'], plan['local_col'], plan['seg_col']]
    if final:
        in_specs.append(blk((1, D_MODEL), lambda b, *_: (0, 0)))
        args.append(final_g)
    return pl.pallas_call(
        functools.partial(_combine_kernel, final=final),
        grid_spec=pltpu.PrefetchScalarGridSpec(
            num_scalar_prefetch=4, grid=(N_BLK,),
            in_specs=in_specs, out_specs=out_specs,
            scratch_shapes=[pltpu.VMEM((2, LOCAL_ROWS, D_MODEL), BF16), pltpu.SemaphoreType.DMA((2,))]),
        out_shape=out_shape,
        compiler_params=_params("arbitrary"),
        name="combine",
    )(plan['dest'], plan['seg'], plan['local'], plan['total'], *args)


def _rope_full_tables(dim, n_rep):
    rows = DEC_SEQ // GRID_W
    r_idx, c_idx = np.meshgrid(np.arange(rows), np.arange(GRID_W), indexing='ij')
    pos = jnp.asarray(np.stack([r_idx.reshape(-1), c_idx.reshape(-1)], axis=-1), F32)
    nf = dim // 4
    inv = ROPE_BASE ** (-jnp.arange(nf, dtype=F32) / nf)
    ang = pos[:, :, None] * inv
    ang = jnp.repeat(ang.reshape(DEC_SEQ, 2 * nf), 2, axis=1)
    sign = jnp.tile(jnp.asarray([-1.0, 1.0], F32), dim // 2)
    return jnp.tile(jnp.cos(ang), (1, n_rep)), jnp.tile(jnp.sin(ang) * sign, (1, n_rep))


def _pack_w_in(w):
    c0 = MLA_Q_LORA + MLA_KV_LORA
    kr = w[:, c0:MLA_IN]
    s0 = MLA_IN + HG_IN + FN_IN
    qh = [w[:, s0 + h * SWA_HD:s0 + (h + 1) * SWA_HD] for h in SWA_STACK_ORDER]
    return jnp.concatenate([w[:, :c0], kr, kr, kr, kr, w[:, MLA_IN:s0]] + qh
                           + [w[:, s0 + SWA_QW:]], axis=1).astype(BF16)


def _pack_w_q_up(w):
    hd = MLA_NOPE + MLA_ROPE
    nope = [w[:, h * hd:h * hd + MLA_NOPE] for h in range(MLA_HEADS)]
    rope = [w[:, h * hd + MLA_NOPE:(h + 1) * hd] for h in range(MLA_HEADS)]
    return jnp.concatenate(nope + rope, axis=1).astype(BF16)


def _pack_w_kv_up(w):
    hd = MLA_NOPE + MLA_V
    kn = [w[:, h * hd:h * hd + MLA_NOPE] for h in range(MLA_HEADS)]
    vv = [w[:, h * hd + MLA_NOPE:(h + 1) * hd] for h in range(MLA_HEADS)]
    return jnp.concatenate(kn + vv, axis=1).astype(BF16)


def _pack_w_out(w):
    s0 = 3 * 256
    rows = [w[s0 + h * SWA_HD:s0 + (h + 1) * SWA_HD] for h in SWA_STACK_ORDER]
    return jnp.concatenate([w[:s0]] + rows, axis=0).astype(BF16)


def kernel(x_prompt, x_sample, c, cache_mla_ckv, cache_mla_krope, cache_swa_k, cache_swa_v, state_hgrn,
           c_ctx, w_ada, b_ada, norm1_g, norm2_g, w_in, mla_q_norm_g, mla_w_q_up, mla_kv_norm_g, mla_w_kv_up,
           hg_lb_logits, hg_norm_g, fn_w, swa_sink, w_out, moe_w_router, moe_b_router, moe_w_gate, moe_w_up,
           moe_w_down, sh_w_gate, sh_w_up, sh_w_down, final_norm_g):
    x_parts = (x_prompt.reshape(N_CTX, D_MODEL), x_sample.reshape(N_LAT, D_MODEL))
    cv8 = jnp.concatenate([c_ctx[None, :], c, jnp.zeros((8 - 1 - DEC_BATCH, D_MODEL), F32)], axis=0)
    mods = _ada(cv8, w_ada, b_ada).reshape(DEPTH, 8, 1, 6 * D_MODEL)

    lb = jnp.cumsum(jax.nn.softmax(hg_lb_logits.astype(F32), axis=1), axis=1)
    lb = lb - lb[:, :1]

    cos_m, sin_m = _rope_full_tables(MLA_ROPE, MLA_HEADS)
    cos_q, sin_q = _rope_full_tables(SWA_HD, SWA_HEADS)
    cos_k, sin_k = cos_q[:, :SWA_KW], sin_q[:, :SWA_KW]
    cache_k = cache_swa_k.reshape(DEC_BATCH, DEPTH, PAST_LEN, SWA_KW)
    cache_v = cache_swa_v.reshape(DEC_BATCH, DEPTH, PAST_LEN, SWA_KW)
    state_t = jnp.swapaxes(state_hgrn, -1, -2)

    ctx_blk_lat = N_CTX // DEC_SEQ
    new_ckv, new_kr, new_k, new_v, new_st = [], [], [], [], []
    for l in range(DEPTH):
        u_mla, u_hg, u_fn, u_swa = _in_proj(x_parts, mods[l], norm1_g[l][None], _pack_w_in(w_in[l]))

        qg, kvg = mla_q_norm_g[l][None], mla_kv_norm_g[l][None]
        wq, wkv = _pack_w_q_up(mla_w_q_up[l]), _pack_w_kv_up(mla_w_kv_up[l])
        o_mla_c, ckv_c = _mla_ctx(u_mla, qg, wq, kvg, wkv)
        o_mla_l = _mla_lat(u_mla, cache_mla_ckv, cache_mla_krope, l, cos_m, sin_m, qg, wq, kvg, wkv)

        lbf, lbb = lb[0, l][None], lb[1, l][None]
        ng4 = jnp.tile(hg_norm_g[l], HG_HEADS)[None]
        o_hg_c, st_c = _hgrn(u_hg, lbf, lbb, ng4, None, seq=SEQ, n_batch=BATCH, row_block0=0)
        o_hg_l = _hgrn(u_hg, lbf, lbb, ng4, state_t[:, l], seq=DEC_SEQ, n_batch=DEC_BATCH,
                       row_block0=ctx_blk_lat)

        fw = fn_w[l].astype(BF16)
        o_fn_c = _fourier(u_fn, fw, seq=SEQ, n_batch=BATCH, row_block0=0)
        o_fn_l = _fourier(u_fn, fw, seq=DEC_SEQ, n_batch=DEC_BATCH, row_block0=ctx_blk_lat)

        sink = swa_sink[l]
        o_swa_c = _swa_ctx(u_swa, sink)
        o_swa_l = _swa_lat(u_swa, cache_k, cache_v, l, sink, cos_q, sin_q, cos_k, sin_k)

        x1, h2, gate_t, rank_t, cnt = _out_proj(
            x_parts, ((o_mla_c, o_mla_l), (o_hg_c, o_hg_l), (o_fn_c, o_fn_l), (o_swa_c, o_swa_l)),
            mods[l], norm2_g[l][None], _pack_w_out(w_out[l]),
            moe_w_router[l].T, moe_b_router[l][:, None],
            sh_w_gate[l].astype(BF16), sh_w_up[l].astype(BF16), sh_w_down[l].astype(BF16))
        plan = _segment_plan(cnt)
        xs = _dispatch(h2, rank_t, plan)
        ys = _experts(xs, plan, moe_w_gate, moe_w_up, moe_w_down, l)
        if l < DEPTH - 1:
            x_parts = (_combine(x1, ys, gate_t, rank_t, mods[l], plan, None),)
        else:
            y_prompt, y_sample = _combine(x1, ys, gate_t, rank_t, mods[l], plan, final_norm_g[None])

        new_ckv.append(ckv_c.reshape(BATCH, SEQ, MLA_KV_LORA))
        new_kr.append(u_mla[:N_CTX, MLA_Q_LORA + MLA_KV_LORA:MLA_IN].reshape(BATCH, SEQ, MLA_ROPE))
        new_k.append(u_swa[:N_CTX, SWA_QW:SWA_QW + SWA_KW].reshape(BATCH, SEQ, SWA_KV_HEADS, SWA_HD))
        new_v.append(u_swa[:N_CTX, SWA_QW + SWA_KW:].reshape(BATCH, SEQ, SWA_KV_HEADS, SWA_HD))
        new_st.append(jnp.swapaxes(st_c, -1, -2))

    y_prompt = y_prompt.reshape(BATCH, SEQ, D_MODEL)
    y_sample = y_sample.reshape(DEC_BATCH, DEC_SEQ, D_MODEL)
    stack = lambda xs: jnp.stack(xs, axis=1)
    return (y_prompt, y_sample, stack(new_ckv), stack(new_kr), stack(new_k), stack(new_v), stack(new_st))
```

```python
import functools

import numpy as np
import jax
import jax.numpy as jnp
from jax import lax
from jax.experimental import pallas as pl
from jax.experimental.pallas import tpu as pltpu

F32 = jnp.float32
BF16 = jnp.bfloat16

D_MODEL = 1024
BATCH = 32
SEQ = 256
DEPTH = 2
DEC_BATCH = 2
DEC_SEQ = 1024
PAST_LEN = 256
GRID_W = 64
EPS = 1e-6
ROPE_BASE = 10000.0
NEG_INF = -1e30

MLA_HEADS = 4
MLA_NOPE = 64
MLA_ROPE = 32
MLA_V = 64
MLA_Q_LORA = 256
MLA_KV_LORA = 128
HG_HEADS = 4
HG_DK = 64
HG_DV = 64
HG_W = HG_HEADS * HG_DK
FN_GROUPS = 4
FN_WIDTH = 256
SWA_HEADS = 4
SWA_KV_HEADS = 2
SWA_HD = 64
WINDOW = 128
N_EXPERTS = 64
TOP_K = 6
D_EXPERT = 256
D_SHARED = 256
ROUTE_SCALE = 2.5

MLA_IN = MLA_Q_LORA + MLA_KV_LORA + MLA_ROPE
HG_IN = 3 * HG_HEADS * HG_DK + 2 * HG_HEADS * HG_DV
FN_IN = FN_WIDTH
SWA_IN = (SWA_HEADS + 2 * SWA_KV_HEADS) * SWA_HD

N_CTX = BATCH * SEQ
N_LAT = DEC_BATCH * DEC_SEQ
N_TOK = N_CTX + N_LAT

MLA_PACK = 512
U_COLS = MLA_PACK + HG_IN + FN_IN + SWA_IN

ROW_TILE = 256
CTX_TILES = N_CTX // ROW_TILE
LAT_TILES_PER_BATCH = DEC_SEQ // ROW_TILE
HG_CHUNK = 32
HG_BLOCK = 256
SWA_QBLK = 128
MLA_QBLK = 256
CTX_PER_STEP = 4
HG_CTX_PER_STEP = 2
OUT_ROWS = 2 * ROW_TILE
VMEM_LIMIT = 56 * 1024 * 1024


def _dot(a, b):
    return jnp.dot(a, b, preferred_element_type=F32)


def _dot_nt(a, b):
    return lax.dot_general(a, b, (((1,), (1,)), ((), ())), preferred_element_type=F32)


def _dot_tn(a, b):
    return lax.dot_general(a, b, (((0,), (0,)), ((), ())), preferred_element_type=F32)


def _split3(x):
    hi = x.astype(BF16)
    r1 = x - hi.astype(F32)
    mid = r1.astype(BF16)
    return hi, mid, (r1 - mid.astype(F32)).astype(BF16)


def _dot_exact_lhs(a, b):
    ab = a.astype(BF16)
    hi, mid, lo = _split3(b)
    return (_dot(ab, lo) + _dot(ab, mid)) + _dot(ab, hi)


def _dot_exact_rhs(a, b):
    bb = b.astype(BF16)
    hi, mid, lo = _split3(a)
    return (_dot(lo, bb) + _dot(mid, bb)) + _dot(hi, bb)


def _rms(x, g):
    return x * lax.rsqrt(jnp.mean(x * x, axis=-1, keepdims=True) + EPS) * g


def _silu(x):
    return x * jax.nn.sigmoid(x)


def _mod_row(i):
    return jnp.where(i < CTX_TILES, 0, 1 + (i - CTX_TILES) // LAT_TILES_PER_BATCH)


def _params(*sem):
    return pltpu.CompilerParams(dimension_semantics=sem, vmem_limit_bytes=VMEM_LIMIT)


ADA_COLS = 1536


def _ada_kernel(cv_ref, w_ref, b_ref, o_ref):
    a = _silu(cv_ref[...]).astype(BF16)
    o_ref[...] = _dot(a, w_ref[...].astype(BF16)) + b_ref[...]


def _ada(cv8, w_ada, b_ada):
    return pl.pallas_call(
        _ada_kernel,
        grid=(DEPTH, 6 * D_MODEL // ADA_COLS),
        in_specs=[
            pl.BlockSpec((8, D_MODEL), lambda l, j: (0, 0)),
            pl.BlockSpec((None, D_MODEL, ADA_COLS), lambda l, j: (l, 0, j)),
            pl.BlockSpec((None, 1, ADA_COLS), lambda l, j: (l, 0, j)),
        ],
        out_specs=pl.BlockSpec((None, 8, ADA_COLS), lambda l, j: (l, 0, j)),
        out_shape=jax.ShapeDtypeStruct((DEPTH, 8, 6 * D_MODEL), F32),
        compiler_params=_params("arbitrary", "arbitrary"),
        name="ada",
    )(cv8, w_ada, b_ada.reshape(DEPTH, 1, 6 * D_MODEL))


def _ctx_lat_specs(width, tile=ROW_TILE):
    n_ctx = N_CTX // tile
    return [pl.BlockSpec((tile, width), lambda i, *_: (jnp.minimum(i, n_ctx - 1), 0)),
            pl.BlockSpec((tile, width), lambda i, *_: (jnp.maximum(i - n_ctx, 0), 0))]


def _row_specs(parts, width, tile=ROW_TILE):
    if len(parts) == 2:
        return _ctx_lat_specs(width, tile)
    return [pl.BlockSpec((tile, width), lambda i, *_: (i, 0))]


def _read_rows(refs, tile=ROW_TILE):
    if len(refs) == 1:
        return refs[0][...]
    return jnp.where(pl.program_id(0) < N_CTX // tile, refs[0][...], refs[1][...])


def _in_kernel(*refs, n_x):
    x = _read_rows(refs[:n_x], OUT_ROWS)
    mod_ref, g_ref, w_ref, umla_ref, uhg_ref, ufn_ref, uswa_ref = refs[n_x:]
    sh1 = mod_ref[:, 0:D_MODEL]
    sc1 = mod_ref[:, D_MODEL:2 * D_MODEL]
    h = _rms(x, g_ref[...]) * (1.0 + sc1) + sh1
    u = _dot(h.astype(BF16), w_ref[...])
    o = 0
    for ref, width in ((umla_ref, MLA_PACK), (uhg_ref, HG_IN), (ufn_ref, FN_IN), (uswa_ref, SWA_IN)):
        ref[...] = u[:, o:o + width].astype(ref.dtype)
        o += width


def _in_proj(x_parts, mod, g, w):
    row = lambda i: (i, 0)
    widths = (MLA_PACK, HG_IN, FN_IN, SWA_IN)
    return pl.pallas_call(
        functools.partial(_in_kernel, n_x=len(x_parts)),
        grid=(N_TOK // OUT_ROWS,),
        in_specs=_row_specs(x_parts, D_MODEL, OUT_ROWS) + [
            pl.BlockSpec((None, 1, 6 * D_MODEL), lambda i: (_mod_row(i * (OUT_ROWS // ROW_TILE)), 0, 0)),
            pl.BlockSpec((1, D_MODEL), lambda i: (0, 0)),
            pl.BlockSpec((D_MODEL, U_COLS), lambda i: (0, 0))],
        out_specs=[pl.BlockSpec((OUT_ROWS, wd), row) for wd in widths],
        out_shape=[jax.ShapeDtypeStruct((N_TOK, wd), F32) for wd in widths],
        compiler_params=_params("arbitrary"),
        name="in_proj",
    )(*x_parts, mod, g, w)


def _rope(x, cos, sin_signed):
    lane = lax.broadcasted_iota(jnp.int32, x.shape, 1)
    width = x.shape[1]
    swapped = jnp.where(lane % 2 == 0, pltpu.roll(x, width - 1, 1), pltpu.roll(x, 1, 1))
    return x * cos + swapped * sin_signed


def _stack_heads(x, n_heads, head_w):
    lane = lax.broadcasted_iota(jnp.int32, x.shape, 1)
    return jnp.concatenate([jnp.where(lane // head_w == h, x, 0.0) for h in range(n_heads)], axis=0)


def _unstack_heads(o, n_heads, head_w):
    t = o.shape[0] // n_heads
    lane = lax.broadcasted_iota(jnp.int32, (t, o.shape[1]), 1)
    out = jnp.zeros((t, o.shape[1]), F32)
    for h in range(n_heads):
        out = jnp.where(lane // head_w == h, o[h * t:(h + 1) * t], out)
    return out


MLA_SCALE = (MLA_NOPE + MLA_ROPE) ** -0.5
MLA_QW = MLA_HEADS * MLA_NOPE + MLA_HEADS * MLA_ROPE
MLA_NW = MLA_HEADS * MLA_NOPE


def _mla_attend(q, kcat, v):
    qs = jnp.concatenate([_stack_heads(q[:, :MLA_NW], MLA_HEADS, MLA_NOPE),
                          _stack_heads(q[:, MLA_NW:], MLA_HEADS, MLA_ROPE)], axis=1)
    s = _dot_nt(qs.astype(BF16), kcat) * MLA_SCALE
    p = jnp.exp(s - jnp.max(s, axis=-1, keepdims=True))
    o = _dot(p.astype(BF16), v) / jnp.sum(p, axis=-1, keepdims=True)
    return _unstack_heads(o, MLA_HEADS, MLA_V)


def _mla_ctx_kernel(u_ref, qg_ref, wq_ref, kvg_ref, wkv_ref, o_ref, ckv_ref):
    for j in range(CTX_PER_STEP):
        rows = slice(j * SEQ, (j + 1) * SEQ)
        u = u_ref[rows, :]
        q = _dot(_rms(u[:, :MLA_Q_LORA], qg_ref[...]).astype(BF16), wq_ref[...])
        ckv = _rms(u[:, MLA_Q_LORA:MLA_Q_LORA + MLA_KV_LORA], kvg_ref[...])
        ckv_ref[rows, :] = ckv
        kv = _dot(ckv.astype(BF16), wkv_ref[...])
        kr4 = u[:, MLA_Q_LORA + MLA_KV_LORA:]
        kcat = jnp.concatenate([kv[:, :MLA_NW], kr4], axis=1).astype(BF16)
        o_ref[rows, :] = _mla_attend(q, kcat, kv[:, MLA_NW:].astype(BF16)).astype(o_ref.dtype)


def _mla_ctx(u_mla, qg, wq, kvg, wkv):
    full = lambda shape: pl.BlockSpec(shape, lambda b: (0, 0))
    rows = CTX_PER_STEP * SEQ
    return pl.pallas_call(
        _mla_ctx_kernel,
        grid=(BATCH // CTX_PER_STEP,),
        in_specs=[pl.BlockSpec((rows, MLA_PACK), lambda b: (b, 0)),
                  full((1, MLA_Q_LORA)), full((MLA_Q_LORA, MLA_QW)),
                  full((1, MLA_KV_LORA)), full((MLA_KV_LORA, 2 * MLA_NW))],
        out_specs=[pl.BlockSpec((rows, MLA_NW), lambda b: (b, 0)),
                   pl.BlockSpec((rows, MLA_KV_LORA), lambda b: (b, 0))],
        out_shape=[jax.ShapeDtypeStruct((N_CTX, MLA_NW), BF16),
                   jax.ShapeDtypeStruct((N_CTX, MLA_KV_LORA), F32)],
        compiler_params=_params("arbitrary"),
        name="mla_ctx",
    )(u_mla, qg, wq, kvg, wkv)


MLA_TK = PAST_LEN + DEC_SEQ


def _mla_lat_kernel(u_ref, cckv_ref, ckr_ref, cos_ref, sin_ref, qg_ref, wq_ref, kvg_ref, wkv_ref,
                    o_ref, kcat_s, v_s):
    i = pl.program_id(1)

    @pl.when(i == 0)
    def _():
        u = u_ref[...]
        ckv_new = _rms(u[:, MLA_Q_LORA:MLA_Q_LORA + MLA_KV_LORA], kvg_ref[...])
        ckv_all = jnp.concatenate([cckv_ref[...], ckv_new], axis=0)
        kv = _dot(ckv_all.astype(BF16), wkv_ref[...])
        kr_new = _rope(u[:, MLA_Q_LORA + MLA_KV_LORA:], cos_ref[...], sin_ref[...])
        ckr = ckr_ref[...]
        kr_all = jnp.concatenate([jnp.concatenate([ckr] * MLA_HEADS, axis=1), kr_new], axis=0)
        kcat_s[...] = jnp.concatenate([kv[:, :MLA_NW], kr_all], axis=1).astype(BF16)
        v_s[...] = kv[:, MLA_NW:].astype(BF16)

    r0 = pl.multiple_of(i * MLA_QBLK, MLA_QBLK)
    cq = u_ref[pl.ds(r0, MLA_QBLK), 0:MLA_Q_LORA]
    q = _dot(_rms(cq, qg_ref[...]).astype(BF16), wq_ref[...])
    qr = _rope(q[:, MLA_NW:], cos_ref[pl.ds(r0, MLA_QBLK), :], sin_ref[pl.ds(r0, MLA_QBLK), :])
    q = jnp.concatenate([q[:, :MLA_NW], qr], axis=1)
    o_ref[...] = _mla_attend(q, kcat_s[...], v_s[...]).astype(o_ref.dtype)


def _mla_lat(u_mla, cache_ckv, cache_kr, l, cos, sin, qg, wq, kvg, wkv):
    full = lambda shape: pl.BlockSpec(shape, lambda b, i: (0, 0))
    nq = DEC_SEQ // MLA_QBLK
    return pl.pallas_call(
        _mla_lat_kernel,
        grid=(DEC_BATCH, nq),
        in_specs=[pl.BlockSpec((DEC_SEQ, MLA_PACK), lambda b, i: (N_CTX // DEC_SEQ + b, 0)),
                  pl.BlockSpec((None, None, PAST_LEN, MLA_KV_LORA), lambda b, i: (b, l, 0, 0)),
                  pl.BlockSpec((None, None, PAST_LEN, MLA_ROPE), lambda b, i: (b, l, 0, 0)),
                  full((DEC_SEQ, MLA_HEADS * MLA_ROPE)), full((DEC_SEQ, MLA_HEADS * MLA_ROPE)),
                  full((1, MLA_Q_LORA)), full((MLA_Q_LORA, MLA_QW)),
                  full((1, MLA_KV_LORA)), full((MLA_KV_LORA, 2 * MLA_NW))],
        out_specs=pl.BlockSpec((MLA_QBLK, MLA_NW), lambda b, i: (b * nq + i, 0)),
        out_shape=jax.ShapeDtypeStruct((N_LAT, MLA_NW), BF16),
        scratch_shapes=[pltpu.VMEM((MLA_TK, MLA_QW), BF16), pltpu.VMEM((MLA_TK, MLA_NW), BF16)],
        compiler_params=_params("arbitrary", "arbitrary"),
        name="mla_lat",
    )(u_mla, cache_ckv, cache_kr, cos, sin, qg, wq, kvg, wkv)


def _hgrn_kernel(*refs, seq, n_seq, has_state):
    if has_state:
        (u_ref, lbf_ref, lbb_ref, ng_ref, s0_ref, o_ref,
         q_s, kf_s, gf_s, kb_s, gb_s, of_s, ob_s, stf_s, stb_s) = refs
    else:
        (u_ref, lbf_ref, lbb_ref, ng_ref, o_ref, so_ref,
         q_s, kf_s, gf_s, kb_s, gb_s, of_s, ob_s, stf_s, stb_s) = refs
    C = HG_CHUNK
    W = HG_W

    q_s[...] = _silu(u_ref[:, 0:W])
    ff = lbf_ref[...] + (1.0 - lbf_ref[...]) * jax.nn.sigmoid(u_ref[:, W:2 * W])
    kf_s[...] = 1.0 - ff
    gf_s[...] = jnp.log(ff)
    fb = lbb_ref[...] + (1.0 - lbb_ref[...]) * jax.nn.sigmoid(u_ref[:, 2 * W:3 * W])
    kb_s[...] = 1.0 - fb
    gb_s[...] = jnp.log(fb)

    rr = lax.broadcasted_iota(jnp.int32, (W, W), 0)
    cc = lax.broadcasted_iota(jnp.int32, (W, W), 1)
    blockdiag = rr // HG_DK == cc // HG_DK
    if has_state:
        for st, d in ((stf_s, 0), (stb_s, 1)):
            rows = []
            for h in range(HG_HEADS):
                z = lambda n: jnp.zeros((HG_DV, n * HG_DK), F32)
                parts = ([z(h)] if h else []) + [s0_ref[d, h]] + ([z(HG_HEADS - 1 - h)] if h < HG_HEADS - 1 else [])
                rows.append(jnp.concatenate(parts, axis=1) if len(parts) > 1 else parts[0])
            st[0] = jnp.concatenate(rows, axis=0)
    else:
        stf_s[...] = jnp.zeros_like(stf_s)
        stb_s[...] = jnp.zeros_like(stb_s)

    B = HG_BLOCK
    per_block = B // C
    n_blocks = seq // B
    ri = lax.broadcasted_iota(jnp.int32, (B, B), 0)
    ci = lax.broadcasted_iota(jnp.int32, (B, B), 1)
    same_chunk = ri // C == ci // C
    rs = lax.broadcasted_iota(jnp.int32, (HG_HEADS * B, B), 0) % B
    cs = lax.broadcasted_iota(jnp.int32, (HG_HEADS * B, B), 1)
    same_chunk_s = rs // C == cs // C

    def sums(to_end, to_mid):
        f = lambda m: jnp.where(same_chunk & m, 1.0, 0.0)
        whole = jnp.where(same_chunk, 1.0, 0.0)
        return jnp.concatenate([f(to_end), f(to_end) - f(to_mid), whole - f(to_end)], axis=0)

    mid_f = (ri // C) * C + (C // 2 - 1)
    mid_b = (ri // C) * C + C // 2
    sums_f = sums(ci <= ri, ci <= mid_f)
    sums_b = sums(ci >= ri, ci >= mid_b)
    keep_f = same_chunk_s & (rs >= cs)
    keep_b = same_chunk_s & (cs >= rs)

    def block(r, k_s, g_s, o_s, st_s, sum_mat, keep, order):
        q = q_s[pl.ds(r, B), :]
        k = k_s[pl.ds(r, B), :]
        v = u_ref[pl.ds(r, B), 3 * W:4 * W].astype(BF16)
        gs = _dot_exact_lhs(sum_mat, g_s[pl.ds(r, B), :])
        G, Gq, Gk2 = gs[:B], gs[B:2 * B], gs[2 * B:]
        qe = _stack_heads(q * jnp.exp(Gq), HG_HEADS, HG_DK)
        ke = k * jnp.exp(-Gq)
        A = jnp.where(keep, _dot_nt(qe.astype(BF16), ke.astype(BF16)), 0.0)
        o_intra = _unstack_heads(_dot(A.astype(BF16), v), HG_HEADS, HG_DV)
        qg = (q * jnp.exp(G)).astype(BF16)
        k2 = (k * jnp.exp(Gk2)).astype(BF16)
        decay = jnp.exp(G + Gk2)
        st = st_s[...]
        o_inter = [None] * per_block
        for c in order:
            rows = slice(c * C, (c + 1) * C)
            o_inter[c] = _dot_nt(st.astype(BF16), qg[rows])
            st = st * decay[c * C:c * C + 1] + jnp.where(blockdiag, _dot_tn(v[rows], k2[rows]), 0.0)
        st_s[...] = st
        o_s[pl.ds(r, B), :] = o_intra + jnp.concatenate(o_inter, axis=1).T

    def fwd(j, r):
        block(r, kf_s, gf_s, of_s, stf_s.at[j], sums_f, keep_f, range(per_block))

    def bwd(j, r):
        block(r, kb_s, gb_s, ob_s, stb_s.at[j], sums_b, keep_b, range(per_block - 1, -1, -1))

    for j in range(n_seq):
        if n_blocks == 1:
            fwd(j, j * seq)
            bwd(j, j * seq)
        else:
            lax.fori_loop(0, n_blocks, lambda i, c, j=j: (fwd(j, pl.multiple_of(j * seq + i * B, B)), c)[1], 0)
            lax.fori_loop(0, n_blocks,
                          lambda i, c, j=j: (bwd(j, pl.multiple_of(j * seq + (n_blocks - 1 - i) * B, B)), c)[1], 0)

    o = of_s[...] + ob_s[...]
    ms = _dot_exact_rhs(o * o, jnp.where(blockdiag, 1.0 / HG_DV, 0.0))
    on = o * lax.rsqrt(ms + EPS) * ng_ref[...]
    o_ref[...] = (on * _silu(u_ref[:, 4 * W:5 * W])).astype(o_ref.dtype)

    if not has_state:
        for j in range(n_seq):
            for st, d in ((stf_s, 0), (stb_s, 1)):
                for h in range(HG_HEADS):
                    so_ref[j, d, h] = st[j, h * HG_DV:(h + 1) * HG_DV, h * HG_DK:(h + 1) * HG_DK]


def _hgrn(u_hg, lbf, lbb, ng4, state_t, *, seq, n_batch, row_block0):
    has_state = state_t is not None
    n_seq = 1 if has_state else HG_CTX_PER_STEP
    rows = n_seq * seq
    full = lambda shape: pl.BlockSpec(shape, lambda b: (0, 0))
    in_specs = [pl.BlockSpec((rows, HG_IN), lambda b: (row_block0 + b, 0)),
                full((1, HG_W)), full((1, HG_W)), full((1, HG_W))]
    args = [u_hg, lbf, lbb, ng4]
    o_spec = pl.BlockSpec((rows, HG_W), lambda b: (b, 0))
    o_shape = jax.ShapeDtypeStruct((n_batch * seq, HG_W), BF16)
    if has_state:
        in_specs.append(pl.BlockSpec((None, 2, HG_HEADS, HG_DV, HG_DK), lambda b: (b, 0, 0, 0, 0)))
        args.append(state_t)
        out_specs, out_shape = o_spec, o_shape
    else:
        out_specs = [o_spec, pl.BlockSpec((n_seq, 2, HG_HEADS, HG_DV, HG_DK), lambda b: (b, 0, 0, 0, 0))]
        out_shape = [o_shape, jax.ShapeDtypeStruct((n_batch, 2, HG_HEADS, HG_DV, HG_DK), F32)]
    return pl.pallas_call(
        functools.partial(_hgrn_kernel, seq=seq, n_seq=n_seq, has_state=has_state),
        grid=(n_batch // n_seq,),
        in_specs=in_specs, out_specs=out_specs, out_shape=out_shape,
        scratch_shapes=[pltpu.VMEM((rows, HG_W), F32)] * 7 + [pltpu.VMEM((n_seq, HG_W, HG_W), F32)] * 2,
        compiler_params=_params("arbitrary"),
        name="hgrn_lat" if has_state else "hgrn_ctx",
    )(*args)


def _dft_tables(n):
    j = np.arange(n, dtype=np.int64)
    ang = 2.0 * np.pi * ((j[:, None] * j[None, :]) % n).astype(np.float64) / n
    return np.cos(ang) / np.sqrt(n), np.sin(ang) / np.sqrt(n)


def _fourier_tables(seq):
    gw = FN_WIDTH // FN_GROUPS
    cg, sg = _dft_tables(gw)
    eye = np.eye(FN_GROUPS)
    chan = np.concatenate([np.kron(eye, cg), np.kron(eye, sg)], axis=1)
    ct, st = _dft_tables(seq)
    pos = np.concatenate([ct, -st], axis=1)
    return jnp.asarray(chan, F32).astype(BF16), jnp.asarray(pos, F32).astype(BF16)


def _fourier_kernel(x_ref, chan_ref, pos_ref, w_ref, o_ref, *, seq, n_seq):
    for j in range(n_seq):
        rows = slice(j * seq, (j + 1) * seq)
        x12 = _dot(x_ref[rows, :].astype(BF16), chan_ref[...])
        z = jnp.concatenate([x12[:, :FN_WIDTH], x12[:, FN_WIDTH:]], axis=0).astype(BF16)
        y = _dot(pos_ref[...], z)
        o_ref[rows, :] = _dot(y.astype(BF16), w_ref[...]).astype(o_ref.dtype)


def _fourier(u_fn, w, *, seq, n_batch, row_block0):
    chan, pos = _fourier_tables(seq)
    full = lambda shape: pl.BlockSpec(shape, lambda b: (0, 0))
    n_seq = CTX_PER_STEP if seq == SEQ else 1
    return pl.pallas_call(
        functools.partial(_fourier_kernel, seq=seq, n_seq=n_seq),
        grid=(n_batch // n_seq,),
        in_specs=[pl.BlockSpec((n_seq * seq, FN_WIDTH), lambda b: (row_block0 + b, 0)),
                  full((FN_WIDTH, 2 * FN_WIDTH)), full((seq, 2 * seq)), full((FN_WIDTH, FN_WIDTH))],
        out_specs=pl.BlockSpec((n_seq * seq, FN_WIDTH), lambda b: (b, 0)),
        out_shape=jax.ShapeDtypeStruct((n_batch * seq, FN_WIDTH), BF16),
        compiler_params=_params("arbitrary"),
        name="fourier",
    )(u_fn, chan, pos, w)


SWA_SCALE = SWA_HD ** -0.5
SWA_QW = SWA_HEADS * SWA_HD
SWA_KW = SWA_KV_HEADS * SWA_HD
SWA_STACK_ORDER = (0, 2, 1, 3)


def _swa_stack_q(q):
    return jnp.concatenate([_stack_heads(q[:, :SWA_KW], SWA_KV_HEADS, SWA_HD),
                            _stack_heads(q[:, SWA_KW:], SWA_KV_HEADS, SWA_HD)], axis=0)


def _swa_unstack_o(o):
    t = o.shape[0] // SWA_HEADS
    return jnp.concatenate([_unstack_heads(o[:2 * t], SWA_KV_HEADS, SWA_HD),
                            _unstack_heads(o[2 * t:], SWA_KV_HEADS, SWA_HD)], axis=1)


def _sink_rows(sink_ref, t):
    return jnp.concatenate([jnp.full((t, 1), sink_ref[h], F32) for h in SWA_STACK_ORDER], axis=0)


def _swa_ctx_kernel(sink_ref, u_ref, o_ref):
    sink = _sink_rows(sink_ref, SEQ)
    for j in range(CTX_PER_STEP):
        rows = slice(j * SEQ, (j + 1) * SEQ)
        u = u_ref[rows, :]
        qs = _swa_stack_q(u[:, :SWA_QW]).astype(BF16)
        k = u[:, SWA_QW:SWA_QW + SWA_KW].astype(BF16)
        v = u[:, SWA_QW + SWA_KW:].astype(BF16)
        s = _dot_nt(qs, k) * SWA_SCALE
        m = jnp.maximum(jnp.max(s, axis=-1, keepdims=True), sink)
        p = jnp.exp(s - m)
        denom = jnp.sum(p, axis=-1, keepdims=True) + jnp.exp(sink - m)
        o_ref[rows, :] = _swa_unstack_o(_dot(p.astype(BF16), v) / denom).astype(o_ref.dtype)


def _swa_ctx(u_swa, sink):
    rows = CTX_PER_STEP * SEQ
    return pl.pallas_call(
        _swa_ctx_kernel,
        grid=(BATCH // CTX_PER_STEP,),
        in_specs=[pl.BlockSpec(memory_space=pltpu.SMEM),
                  pl.BlockSpec((rows, SWA_IN), lambda b: (b, 0))],
        out_specs=pl.BlockSpec((rows, SWA_QW), lambda b: (b, 0)),
        out_shape=jax.ShapeDtypeStruct((N_CTX, SWA_QW), BF16),
        compiler_params=_params("arbitrary"),
        name="swa_ctx",
    )(sink, u_swa)


SWA_PAD = DEC_SEQ + 2 * SWA_QBLK


def _swa_lat_kernel(sink_ref, u_ref, kc_ref, vc_ref, cosq_ref, sinq_ref, cosk_ref, sin_k_ref,
                    o_ref, k_s, v_s):
    i = pl.program_id(1)
    B = SWA_QBLK

    @pl.when(i == 0)
    def _():
        zeros = jnp.zeros((B, SWA_KW), BF16)
        k = _rope(u_ref[:, SWA_QW:SWA_QW + SWA_KW], cosk_ref[...], sin_k_ref[...]).astype(BF16)
        k_s[...] = jnp.concatenate([zeros, k, zeros], axis=0)
        v_s[...] = jnp.concatenate([zeros, u_ref[:, SWA_QW + SWA_KW:].astype(BF16), zeros], axis=0)

    r0 = pl.multiple_of(i * B, B)
    q = _rope(u_ref[pl.ds(r0, B), 0:SWA_QW], cosq_ref[pl.ds(r0, B), :], sinq_ref[pl.ds(r0, B), :])
    qs = _swa_stack_q(q).astype(BF16)
    s_loc = _dot_nt(qs, k_s[pl.ds(r0, 3 * B), :]) * SWA_SCALE
    row = lax.broadcasted_iota(jnp.int32, s_loc.shape, 0) % B
    col = lax.broadcasted_iota(jnp.int32, s_loc.shape, 1)
    kpos = r0 - B + col
    valid = (jnp.abs(row + B - col) <= WINDOW) & (kpos >= 0) & (kpos < DEC_SEQ)
    s_loc = jnp.where(valid, s_loc, NEG_INF)
    s_ctx = _dot_nt(qs, kc_ref[...].astype(BF16)) * SWA_SCALE
    sink = _sink_rows(sink_ref, B)
    m = jnp.maximum(jnp.maximum(jnp.max(s_loc, axis=-1, keepdims=True),
                                jnp.max(s_ctx, axis=-1, keepdims=True)), sink)
    p_loc = jnp.exp(s_loc - m)
    p_ctx = jnp.exp(s_ctx - m)
    denom = (jnp.sum(p_loc, axis=-1, keepdims=True) + jnp.sum(p_ctx, axis=-1, keepdims=True)
             + jnp.exp(sink - m))
    o = _dot(p_loc.astype(BF16), v_s[pl.ds(r0, 3 * B), :]) + _dot(p_ctx.astype(BF16), vc_ref[...].astype(BF16))
    o_ref[...] = _swa_unstack_o(o / denom).astype(o_ref.dtype)


def _swa_lat(u_swa, cache_k, cache_v, l, sink, cosq, sinq, cosk, sink_k):
    full = lambda shape: pl.BlockSpec(shape, lambda b, i: (0, 0))
    nq = DEC_SEQ // SWA_QBLK
    cache_spec = pl.BlockSpec((None, None, PAST_LEN, SWA_KW), lambda b, i: (b, l, 0, 0))
    return pl.pallas_call(
        _swa_lat_kernel,
        grid=(DEC_BATCH, nq),
        in_specs=[pl.BlockSpec(memory_space=pltpu.SMEM),
                  pl.BlockSpec((DEC_SEQ, SWA_IN), lambda b, i: (N_CTX // DEC_SEQ + b, 0)),
                  cache_spec, cache_spec,
                  full((DEC_SEQ, SWA_QW)), full((DEC_SEQ, SWA_QW)),
                  full((DEC_SEQ, SWA_KW)), full((DEC_SEQ, SWA_KW))],
        out_specs=pl.BlockSpec((SWA_QBLK, SWA_QW), lambda b, i: (b * nq + i, 0)),
        out_shape=jax.ShapeDtypeStruct((N_LAT, SWA_QW), BF16),
        scratch_shapes=[pltpu.VMEM((SWA_PAD, SWA_KW), BF16), pltpu.VMEM((SWA_PAD, SWA_KW), BF16)],
        compiler_params=_params("arbitrary", "arbitrary"),
        name="swa_lat",
    )(sink, u_swa, cache_k, cache_v, cosq, sinq, cosk, sink_k)


N_BLK = N_TOK // ROW_TILE
SEG_ALIGN = 16
LOCAL_ROWS = ROW_TILE * TOP_K + N_EXPERTS * SEG_ALIGN
LOCAL_CHUNK = 512
EXPERT_TILE = 768
SORTED_ROWS = -(-(N_TOK * TOP_K + N_BLK * N_EXPERTS * SEG_ALIGN + N_EXPERTS * (EXPERT_TILE + SEG_ALIGN))
                // EXPERT_TILE) * EXPERT_TILE
MAX_TILES = SORTED_ROWS // EXPERT_TILE
NOT_PICKED = -1.0
NO_DEST = 4095.0


def _out_kernel(*refs, n_x):
    x_all = _read_rows(refs[:n_x], OUT_ROWS)
    mix_all = [_read_rows(refs[n_x + 2 * j:n_x + 2 * j + 2], OUT_ROWS) for j in range(4)]
    out_refs = refs[n_x + 8:]
    for blk in range(OUT_ROWS // ROW_TILE):
        rows = slice(blk * ROW_TILE, (blk + 1) * ROW_TILE)
        _out_block(x_all[rows], [m[rows] for m in mix_all], out_refs, blk)


def _out_block(x, mixers, refs, blk):
    (mod_ref, g_ref, wo_ref, wr_ref, br_ref, wsg_ref, wsu_ref, wsd_ref,
     x1_ref, h_ref, gate_ref, rank_ref, cnt_ref) = refs
    rows = slice(blk * ROW_TILE, (blk + 1) * ROW_TILE)
    mix = jnp.zeros((ROW_TILE, D_MODEL), F32)
    for j in range(4):
        mix = mix + _dot(mixers[j], wo_ref[j * 256:(j + 1) * 256, :])
    g1 = mod_ref[:, 2 * D_MODEL:3 * D_MODEL]
    sh2 = mod_ref[:, 3 * D_MODEL:4 * D_MODEL]
    sc2 = mod_ref[:, 4 * D_MODEL:5 * D_MODEL]
    g2 = mod_ref[:, 5 * D_MODEL:6 * D_MODEL]
    x1 = x + g1 * mix
    h = _rms(x1, g_ref[...]) * (1.0 + sc2) + sh2
    hb = h.astype(BF16)
    h_ref[rows, :] = hb

    w_hi = wr_ref[...].astype(BF16)
    w_lo = (wr_ref[...] - w_hi.astype(F32)).astype(BF16)
    h_lo = (h - hb.astype(F32)).astype(BF16)
    logits = ((_dot(h_lo, w_hi) + _dot(hb, w_lo)) + _dot(hb, w_hi)).T
    scores = jax.nn.sigmoid(logits)
    sel = scores + br_ref[...]
    eidx = lax.broadcasted_iota(jnp.int32, sel.shape, 0)
    gate = jnp.zeros_like(scores)
    picked = jnp.zeros_like(scores)
    for _ in range(TOP_K):
        best = jnp.max(sel, axis=0, keepdims=True)
        first = jnp.min(jnp.where(sel == best, eidx, N_EXPERTS), axis=0, keepdims=True)
        pick = eidx == first
        gate = jnp.where(pick, scores, gate)
        picked = jnp.where(pick, 1.0, picked)
        sel = jnp.where(pick, -jnp.inf, sel)
    gate = ROUTE_SCALE * gate / jnp.sum(gate, axis=0, keepdims=True)

    ti = lax.broadcasted_iota(jnp.int32, (ROW_TILE, ROW_TILE), 0)
    tj = lax.broadcasted_iota(jnp.int32, (ROW_TILE, ROW_TILE), 1)
    pb = picked.astype(BF16)
    rank = _dot(pb, jnp.where(ti < tj, 1.0, 0.0).astype(BF16))
    gate_ref[:, rows] = gate
    rank_ref[:, rows] = jnp.where(picked > 0.0, rank, NOT_PICKED)
    counts = _dot_nt(jnp.ones((8, ROW_TILE), BF16), pb)
    cnt_ref[blk] = jnp.concatenate([counts, jnp.zeros_like(counts)], axis=1)

    hid = _silu(_dot(hb, wsg_ref[...])) * _dot(hb, wsu_ref[...])
    x1_ref[rows, :] = x1 + g2 * _dot(hid.astype(BF16), wsd_ref[...])


def _out_proj(x_parts, mixer_pairs, mod, g, wo, wr, br, wsg, wsu, wsd):
    row = lambda i: (i, 0)
    col = lambda i: (0, i)
    full = lambda shape: pl.BlockSpec(shape, lambda i: (0, 0))
    per_step = OUT_ROWS // ROW_TILE
    et_spec = pl.BlockSpec((N_EXPERTS, OUT_ROWS), col)
    et_shape = jax.ShapeDtypeStruct((N_EXPERTS, N_TOK), F32)
    return pl.pallas_call(
        functools.partial(_out_kernel, n_x=len(x_parts)),
        grid=(N_TOK // OUT_ROWS,),
        in_specs=_row_specs(x_parts, D_MODEL, OUT_ROWS) + 4 * _ctx_lat_specs(256, OUT_ROWS) + [
            pl.BlockSpec((None, 1, 6 * D_MODEL), lambda i: (_mod_row(i * per_step), 0, 0)),
            full((1, D_MODEL)), full((D_MODEL, D_MODEL)),
            full((D_MODEL, N_EXPERTS)), full((N_EXPERTS, 1)),
            full((D_MODEL, D_SHARED)), full((D_MODEL, D_SHARED)), full((D_SHARED, D_MODEL))],
        out_specs=[pl.BlockSpec((OUT_ROWS, D_MODEL), row), pl.BlockSpec((OUT_ROWS, D_MODEL), row),
                   et_spec, et_spec,
                   pl.BlockSpec((per_step, 8, 128), lambda i: (i, 0, 0))],
        out_shape=[jax.ShapeDtypeStruct((N_TOK, D_MODEL), F32),
                   jax.ShapeDtypeStruct((N_TOK, D_MODEL), BF16),
                   et_shape, et_shape,
                   jax.ShapeDtypeStruct((N_BLK, 8, 128), F32)],
        compiler_params=_params("arbitrary"),
        name="out_proj",
    )(*x_parts, *[a for pair in mixer_pairs for a in pair], mod, g, wo, wr, br, wsg, wsu, wsd)


def _segment_plan(cnt):
    cnt = cnt[:, 0, :N_EXPERTS].astype(jnp.int32)
    seg = jnp.maximum((cnt + (SEG_ALIGN - 1)) // SEG_ALIGN, 1) * SEG_ALIGN
    local = jnp.cumsum(seg, axis=1) - seg
    total = jnp.sum(seg, axis=1)
    per_expert = jnp.sum(seg, axis=0)
    padded = (per_expert + SEG_ALIGN + (EXPERT_TILE - 1)) // EXPERT_TILE * EXPERT_TILE
    ends = jnp.cumsum(padded)
    start = ends - padded
    dest = start[None, :] + jnp.cumsum(seg, axis=0) - seg
    n_tiles = ends[-1] // EXPERT_TILE
    tiles = jnp.arange(MAX_TILES, dtype=jnp.int32)
    tile_expert = jnp.sum((ends // EXPERT_TILE)[None, :] <= jnp.minimum(tiles, n_tiles - 1)[:, None], axis=1)
    tile_expert = tile_expert.astype(jnp.int32)
    plan = dict(seg=seg, local=local, total=total.astype(jnp.int32), dest=dest.astype(jnp.int32),
                tail_start=(start + per_expert).astype(jnp.int32), tail_rows=(padded - per_expert).astype(jnp.int32),
                n_tiles=n_tiles.reshape(1).astype(jnp.int32), tile_expert=tile_expert)
    segf, localf = seg.astype(F32), local.astype(F32)
    pad_lanes = lambda a: jnp.concatenate([a, jnp.zeros_like(a)], axis=1)[:, None, :]
    plan.update(seg_row=pad_lanes(segf), local_row=pad_lanes(localf),
                seg_col=segf[:, :, None], local_col=localf[:, :, None])
    return plan


def _local_dest_digits(rank, local_start):
    dest = jnp.where(rank >= 0.0, local_start + rank, NO_DEST)
    hi = jnp.floor(dest * (1.0 / 64.0))
    return hi, dest - 64.0 * hi


def _dispatch_kernel(dest_ref, seg_ref, local_ref, total_ref, tail_start_ref, tail_rows_ref, nt_ref,
                     h_ref, rank_ref, local_col_ref, local_row_ref, seg_row_ref,
                     xs_hbm, buf, zeros, sems, zsem, usem):
    b = pl.program_id(0)
    slot = b % 2

    def wait_block(blk, s):
        n = pl.multiple_of(total_ref[blk], SEG_ALIGN)
        pltpu.make_async_copy(buf.at[s, pl.ds(0, n)], xs_hbm.at[pl.ds(0, n)], sems.at[s]).wait()

    @pl.when(b == 0)
    def _():
        zeros[...] = jnp.zeros_like(zeros)

    def unused_tiles(action):
        for k in range(-(-MAX_TILES // N_BLK)):
            t = nt_ref[0] + b + k * N_BLK

            @pl.when(t < MAX_TILES)
            def _():
                cp = pltpu.make_async_copy(zeros.at[pl.ds(0, EXPERT_TILE)],
                                           xs_hbm.at[pl.ds(pl.multiple_of(t * EXPERT_TILE, EXPERT_TILE),
                                                           EXPERT_TILE)], usem)
                cp.start() if action == "start" else cp.wait()

    unused_tiles("start")

    @pl.when(b >= 2)
    def _():
        wait_block(b - 2, slot)

    hi, lo = _local_dest_digits(rank_ref[...], local_col_ref[...])
    code = jnp.concatenate([hi, lo], axis=0).astype(BF16)
    hb = h_ref[...]
    lstart = local_row_ref[:, :N_EXPERTS]
    lend = lstart + seg_row_ref[:, :N_EXPERTS]
    r = lax.broadcasted_iota(jnp.int32, (LOCAL_CHUNK, N_EXPERTS), 0).astype(F32)
    rr = lax.broadcasted_iota(jnp.int32, (LOCAL_CHUNK, ROW_TILE), 0).astype(F32)
    for c in range(LOCAL_ROWS // LOCAL_CHUNK):
        first = float(c * LOCAL_CHUNK)
        member = (r >= lstart - first) & (r < lend - first)
        table = jnp.concatenate([jnp.where(member, 64.0, 0.0), jnp.where(member, 1.0, 0.0)], axis=1).astype(BF16)
        d = _dot(table, code)
        onehot = jnp.where(d == rr + first, 1.0, 0.0).astype(BF16)
        buf[slot, c * LOCAL_CHUNK:(c + 1) * LOCAL_CHUNK, :] = _dot(onehot, hb).astype(BF16)

    for e in range(N_EXPERTS):
        n = pl.multiple_of(seg_ref[b, e], SEG_ALIGN)
        src = pl.multiple_of(local_ref[b, e], SEG_ALIGN)
        dst = pl.multiple_of(dest_ref[b, e], SEG_ALIGN)
        pltpu.make_async_copy(buf.at[slot, pl.ds(src, n)], xs_hbm.at[pl.ds(dst, n)], sems.at[slot]).start()

    unused_tiles("wait")

    @pl.when(b == N_BLK - 1)
    def _():
        def tail(e):
            n = pl.multiple_of(tail_rows_ref[e], SEG_ALIGN)
            dst = pl.multiple_of(tail_start_ref[e], SEG_ALIGN)
            return pltpu.make_async_copy(zeros.at[pl.ds(0, n)], xs_hbm.at[pl.ds(dst, n)], zsem)

        for e in range(N_EXPERTS):
            tail(e).start()
        for e in range(N_EXPERTS):
            tail(e).wait()
        wait_block(b - 1, 1 - slot)
        wait_block(b, slot)


def _dispatch(h, rank_t, plan):
    blk = lambda shape, imap: pl.BlockSpec(shape, imap)
    return pl.pallas_call(
        _dispatch_kernel,
        grid_spec=pltpu.PrefetchScalarGridSpec(
            num_scalar_prefetch=7, grid=(N_BLK,),
            in_specs=[blk((ROW_TILE, D_MODEL), lambda b, *_: (b, 0)),
                      blk((N_EXPERTS, ROW_TILE), lambda b, *_: (0, b)),
                      blk((None, N_EXPERTS, 1), lambda b, *_: (b, 0, 0)),
                      blk((None, 1, 128), lambda b, *_: (b, 0, 0)),
                      blk((None, 1, 128), lambda b, *_: (b, 0, 0))],
            out_specs=pl.BlockSpec(memory_space=pl.ANY),
            scratch_shapes=[pltpu.VMEM((2, LOCAL_ROWS, D_MODEL), BF16),
                            pltpu.VMEM((EXPERT_TILE + SEG_ALIGN, D_MODEL), BF16),
                            pltpu.SemaphoreType.DMA((2,)), pltpu.SemaphoreType.DMA(()),
                            pltpu.SemaphoreType.DMA(())]),
        out_shape=jax.ShapeDtypeStruct((SORTED_ROWS, D_MODEL), BF16),
        compiler_params=_params("arbitrary"),
        name="dispatch",
    )(plan['dest'], plan['seg'], plan['local'], plan['total'], plan['tail_start'], plan['tail_rows'],
      plan['n_tiles'], h, rank_t, plan['local_col'], plan['local_row'], plan['seg_row'])


def _expert_kernel(te_ref, nt_ref, x_ref, wg_ref, wu_ref, wd_ref, y_ref, wg_s, wu_s, wd_s):
    i = pl.program_id(0)
    active = i < nt_ref[0]

    @pl.when((i == 0) | (te_ref[i] != te_ref[jnp.maximum(i - 1, 0)]))
    def _():
        wg_s[...] = wg_ref[...].astype(BF16)
        wu_s[...] = wu_ref[...].astype(BF16)
        wd_s[...] = wd_ref[...].astype(BF16)

    @pl.when(active)
    def _():
        x = x_ref[...]
        hid = _silu(_dot(x, wg_s[...])) * _dot(x, wu_s[...])
        y_ref[...] = _dot(hid.astype(BF16), wd_s[...]).astype(BF16)


def _experts(xs, plan, w_gate, w_up, w_down, l):
    rows = pl.BlockSpec((EXPERT_TILE, D_MODEL), lambda i, te, nt: (jnp.minimum(i, nt[0] - 1), 0))
    wspec = lambda shape: pl.BlockSpec((None, None) + shape, lambda i, te, nt: (l, te[i], 0, 0))
    return pl.pallas_call(
        _expert_kernel,
        grid_spec=pltpu.PrefetchScalarGridSpec(
            num_scalar_prefetch=2, grid=(MAX_TILES,),
            in_specs=[rows, wspec((D_MODEL, D_EXPERT)), wspec((D_MODEL, D_EXPERT)), wspec((D_EXPERT, D_MODEL))],
            out_specs=rows,
            scratch_shapes=[pltpu.VMEM((D_MODEL, D_EXPERT), BF16), pltpu.VMEM((D_MODEL, D_EXPERT), BF16),
                            pltpu.VMEM((D_EXPERT, D_MODEL), BF16)]),
        out_shape=jax.ShapeDtypeStruct((SORTED_ROWS, D_MODEL), BF16),
        input_output_aliases={2: 0},
        compiler_params=_params("arbitrary"),
        name="experts",
    )(plan['tile_expert'], plan['n_tiles'], xs, w_gate, w_up, w_down)


def _combine_kernel(*refs, final):
    refs = list(refs)
    dest_ref, seg_ref, local_ref, total_ref = refs[:4]
    x1_ref, ys_hbm, gate_ref, rank_ref, mod_ref, local_row_ref, local_col_ref, seg_col_ref = refs[4:12]
    rest = refs[12:]
    fg_ref = rest.pop(0) if final else None
    outs, (buf, sems) = rest[:-2], rest[-2:]
    b = pl.program_id(0)
    slot = b % 2

    def fetch(blk, s):
        for e in range(N_EXPERTS):
            n = pl.multiple_of(seg_ref[blk, e], SEG_ALIGN)
            src = pl.multiple_of(dest_ref[blk, e], SEG_ALIGN)
            dst = pl.multiple_of(local_ref[blk, e], SEG_ALIGN)
            pltpu.make_async_copy(ys_hbm.at[pl.ds(src, n)], buf.at[s, pl.ds(dst, n)], sems.at[s]).start()

    @pl.when(b == 0)
    def _():
        buf[...] = jnp.zeros_like(buf)
        fetch(0, 0)

    @pl.when(b + 1 < N_BLK)
    def _():
        fetch(b + 1, 1 - slot)

    n_rows = pl.multiple_of(total_ref[b], SEG_ALIGN)
    pltpu.make_async_copy(ys_hbm.at[pl.ds(0, n_rows)], buf.at[slot, pl.ds(0, n_rows)], sems.at[slot]).wait()

    gate = gate_ref[...].T
    hi, lo = _local_dest_digits(rank_ref[...].T, local_row_ref[:, :N_EXPERTS])
    lhs = jnp.concatenate([jnp.concatenate([hi, lo], axis=1),
                           jnp.concatenate([jnp.zeros_like(gate), gate], axis=1)], axis=0).astype(BF16)
    lstart = local_col_ref[...]
    lend = lstart + seg_col_ref[...]
    routed = jnp.zeros((ROW_TILE, D_MODEL), F32)
    r = lax.broadcasted_iota(jnp.int32, (N_EXPERTS, LOCAL_CHUNK), 1).astype(F32)
    rr = lax.broadcasted_iota(jnp.int32, (ROW_TILE, LOCAL_CHUNK), 1).astype(F32)
    for c in range(LOCAL_ROWS // LOCAL_CHUNK):
        first = float(c * LOCAL_CHUNK)
        member = (r >= lstart - first) & (r < lend - first)
        table = jnp.concatenate([jnp.where(member, 64.0, 0.0), jnp.where(member, 1.0, 0.0)], axis=0).astype(BF16)
        dg = _dot(lhs, table)
        weights = jnp.where(dg[:ROW_TILE] == rr + first, dg[ROW_TILE:], 0.0).astype(BF16)
        routed = routed + _dot(weights, buf[slot, c * LOCAL_CHUNK:(c + 1) * LOCAL_CHUNK, :])
    x = x1_ref[...] + mod_ref[:, 5 * D_MODEL:6 * D_MODEL] * routed
    if final:
        y = _rms(x, fg_ref[...])

        @pl.when(b < CTX_TILES)
        def _():
            outs[0][...] = y

        @pl.when(b >= CTX_TILES)
        def _():
            outs[1][...] = y
    else:
        outs[0][...] = x


def _combine(x1, ys, gate_t, rank_t, mod, plan, final_g):
    final = final_g is not None
    blk = lambda shape, imap: pl.BlockSpec(shape, imap)
    xspec = blk((ROW_TILE, D_MODEL), lambda b, *_: (b, 0))
    if final:
        out_specs = _ctx_lat_specs(D_MODEL)
        out_shape = [jax.ShapeDtypeStruct((N_CTX, D_MODEL), F32), jax.ShapeDtypeStruct((N_LAT, D_MODEL), F32)]
    else:
        out_specs, out_shape = xspec, jax.ShapeDtypeStruct((N_TOK, D_MODEL), F32)
    et_spec = blk((N_EXPERTS, ROW_TILE), lambda b, *_: (0, b))
    in_specs = [xspec, pl.BlockSpec(memory_space=pl.ANY), et_spec, et_spec,
                blk((None, 1, 6 * D_MODEL), lambda b, *_: (_mod_row(b), 0, 0)),
                blk((None, 1, 128), lambda b, *_: (b, 0, 0)),
                blk((None, N_EXPERTS, 1), lambda b, *_: (b, 0, 0)),
                blk((None, N_EXPERTS, 1), lambda b, *_: (b, 0, 0))]
    args = [x1, ys, gate_t, rank_t, mod, plan['local_row'], plan['local_col'], plan['seg_col']]
    if final:
        in_specs.append(blk((1, D_MODEL), lambda b, *_: (0, 0)))
        args.append(final_g)
    return pl.pallas_call(
        functools.partial(_combine_kernel, final=final),
        grid_spec=pltpu.PrefetchScalarGridSpec(
            num_scalar_prefetch=4, grid=(N_BLK,),
            in_specs=in_specs, out_specs=out_specs,
            scratch_shapes=[pltpu.VMEM((2, LOCAL_ROWS, D_MODEL), BF16), pltpu.SemaphoreType.DMA((2,))]),
        out_shape=out_shape,
        compiler_params=_params("arbitrary"),
        name="combine",
    )(plan['dest'], plan['seg'], plan['local'], plan['total'], *args)


def _rope_full_tables(dim, n_rep):
    rows = DEC_SEQ // GRID_W
    r_idx, c_idx = np.meshgrid(np.arange(rows), np.arange(GRID_W), indexing='ij')
    pos = jnp.asarray(np.stack([r_idx.reshape(-1), c_idx.reshape(-1)], axis=-1), F32)
    nf = dim // 4
    inv = ROPE_BASE ** (-jnp.arange(nf, dtype=F32) / nf)
    ang = pos[:, :, None] * inv
    ang = jnp.repeat(ang.reshape(DEC_SEQ, 2 * nf), 2, axis=1)
    sign = jnp.tile(jnp.asarray([-1.0, 1.0], F32), dim // 2)
    return jnp.tile(jnp.cos(ang), (1, n_rep)), jnp.tile(jnp.sin(ang) * sign, (1, n_rep))


def _pack_w_in(w):
    c0 = MLA_Q_LORA + MLA_KV_LORA
    kr = w[:, c0:MLA_IN]
    s0 = MLA_IN + HG_IN + FN_IN
    qh = [w[:, s0 + h * SWA_HD:s0 + (h + 1) * SWA_HD] for h in SWA_STACK_ORDER]
    return jnp.concatenate([w[:, :c0], kr, kr, kr, kr, w[:, MLA_IN:s0]] + qh
                           + [w[:, s0 + SWA_QW:]], axis=1).astype(BF16)


def _pack_w_q_up(w):
    hd = MLA_NOPE + MLA_ROPE
    nope = [w[:, h * hd:h * hd + MLA_NOPE] for h in range(MLA_HEADS)]
    rope = [w[:, h * hd + MLA_NOPE:(h + 1) * hd] for h in range(MLA_HEADS)]
    return jnp.concatenate(nope + rope, axis=1).astype(BF16)


def _pack_w_kv_up(w):
    hd = MLA_NOPE + MLA_V
    kn = [w[:, h * hd:h * hd + MLA_NOPE] for h in range(MLA_HEADS)]
    vv = [w[:, h * hd + MLA_NOPE:(h + 1) * hd] for h in range(MLA_HEADS)]
    return jnp.concatenate(kn + vv, axis=1).astype(BF16)


def _pack_w_out(w):
    s0 = 3 * 256
    rows = [w[s0 + h * SWA_HD:s0 + (h + 1) * SWA_HD] for h in SWA_STACK_ORDER]
    return jnp.concatenate([w[:s0]] + rows, axis=0).astype(BF16)


def kernel(x_prompt, x_sample, c, cache_mla_ckv, cache_mla_krope, cache_swa_k, cache_swa_v, state_hgrn,
           c_ctx, w_ada, b_ada, norm1_g, norm2_g, w_in, mla_q_norm_g, mla_w_q_up, mla_kv_norm_g, mla_w_kv_up,
           hg_lb_logits, hg_norm_g, fn_w, swa_sink, w_out, moe_w_router, moe_b_router, moe_w_gate, moe_w_up,
           moe_w_down, sh_w_gate, sh_w_up, sh_w_down, final_norm_g):
    x_parts = (x_prompt.reshape(N_CTX, D_MODEL), x_sample.reshape(N_LAT, D_MODEL))
    cv8 = jnp.concatenate([c_ctx[None, :], c, jnp.zeros((8 - 1 - DEC_BATCH, D_MODEL), F32)], axis=0)
    mods = _ada(cv8, w_ada, b_ada).reshape(DEPTH, 8, 1, 6 * D_MODEL)

    lb = jnp.cumsum(jax.nn.softmax(hg_lb_logits.astype(F32), axis=1), axis=1)
    lb = lb - lb[:, :1]

    cos_m, sin_m = _rope_full_tables(MLA_ROPE, MLA_HEADS)
    cos_q, sin_q = _rope_full_tables(SWA_HD, SWA_HEADS)
    cos_k, sin_k = cos_q[:, :SWA_KW], sin_q[:, :SWA_KW]
    cache_k = cache_swa_k.reshape(DEC_BATCH, DEPTH, PAST_LEN, SWA_KW)
    cache_v = cache_swa_v.reshape(DEC_BATCH, DEPTH, PAST_LEN, SWA_KW)
    state_t = jnp.swapaxes(state_hgrn, -1, -2)

    ctx_blk_lat = N_CTX // DEC_SEQ
    new_ckv, new_kr, new_k, new_v, new_st = [], [], [], [], []
    for l in range(DEPTH):
        u_mla, u_hg, u_fn, u_swa = _in_proj(x_parts, mods[l], norm1_g[l][None], _pack_w_in(w_in[l]))

        qg, kvg = mla_q_norm_g[l][None], mla_kv_norm_g[l][None]
        wq, wkv = _pack_w_q_up(mla_w_q_up[l]), _pack_w_kv_up(mla_w_kv_up[l])
        o_mla_c, ckv_c = _mla_ctx(u_mla, qg, wq, kvg, wkv)
        o_mla_l = _mla_lat(u_mla, cache_mla_ckv, cache_mla_krope, l, cos_m, sin_m, qg, wq, kvg, wkv)

        lbf, lbb = lb[0, l][None], lb[1, l][None]
        ng4 = jnp.tile(hg_norm_g[l], HG_HEADS)[None]
        o_hg_c, st_c = _hgrn(u_hg, lbf, lbb, ng4, None, seq=SEQ, n_batch=BATCH, row_block0=0)
        o_hg_l = _hgrn(u_hg, lbf, lbb, ng4, state_t[:, l], seq=DEC_SEQ, n_batch=DEC_BATCH,
                       row_block0=ctx_blk_lat)

        fw = fn_w[l].astype(BF16)
        o_fn_c = _fourier(u_fn, fw, seq=SEQ, n_batch=BATCH, row_block0=0)
        o_fn_l = _fourier(u_fn, fw, seq=DEC_SEQ, n_batch=DEC_BATCH, row_block0=ctx_blk_lat)

        sink = swa_sink[l]
        o_swa_c = _swa_ctx(u_swa, sink)
        o_swa_l = _swa_lat(u_swa, cache_k, cache_v, l, sink, cos_q, sin_q, cos_k, sin_k)

        x1, h2, gate_t, rank_t, cnt = _out_proj(
            x_parts, ((o_mla_c, o_mla_l), (o_hg_c, o_hg_l), (o_fn_c, o_fn_l), (o_swa_c, o_swa_l)),
            mods[l], norm2_g[l][None], _pack_w_out(w_out[l]),
            moe_w_router[l], moe_b_router[l][:, None],
            sh_w_gate[l].astype(BF16), sh_w_up[l].astype(BF16), sh_w_down[l].astype(BF16))
        plan = _segment_plan(cnt)
        xs = _dispatch(h2, rank_t, plan)
        ys = _experts(xs, plan, moe_w_gate, moe_w_up, moe_w_down, l)
        if l < DEPTH - 1:
            x_parts = (_combine(x1, ys, gate_t, rank_t, mods[l], plan, None),)
        else:
            y_prompt, y_sample = _combine(x1, ys, gate_t, rank_t, mods[l], plan, final_norm_g[None])

        new_ckv.append(ckv_c.reshape(BATCH, SEQ, MLA_KV_LORA))
        new_kr.append(u_mla[:N_CTX, MLA_Q_LORA + MLA_KV_LORA:MLA_IN].reshape(BATCH, SEQ, MLA_ROPE))
        new_k.append(u_swa[:N_CTX, SWA_QW:SWA_QW + SWA_KW].reshape(BATCH, SEQ, SWA_KV_HEADS, SWA_HD))
        new_v.append(u_swa[:N_CTX, SWA_QW + SWA_KW:].reshape(BATCH, SEQ, SWA_KV_HEADS, SWA_HD))
        new_st.append(jnp.swapaxes(st_c, -1, -2))

    y_prompt = y_prompt.reshape(BATCH, SEQ, D_MODEL)
    y_sample = y_sample.reshape(DEC_BATCH, DEC_SEQ, D_MODEL)
    stack = lambda xs: jnp.stack(xs, axis=1)
    return (y_prompt, y_sample, stack(new_ckv), stack(new_kr), stack(new_k), stack(new_v), stack(new_st))
```

```python
import functools

import numpy as np
import jax
import jax.numpy as jnp
from jax import lax
from jax.experimental import pallas as pl
from jax.experimental.pallas import tpu as pltpu

F32 = jnp.float32
BF16 = jnp.bfloat16

D_MODEL = 1024
BATCH = 32
SEQ = 256
DEPTH = 2
DEC_BATCH = 2
DEC_SEQ = 1024
PAST_LEN = 256
GRID_W = 64
EPS = 1e-6
ROPE_BASE = 10000.0
NEG_INF = -1e30

MLA_HEADS = 4
MLA_NOPE = 64
MLA_ROPE = 32
MLA_V = 64
MLA_Q_LORA = 256
MLA_KV_LORA = 128
HG_HEADS = 4
HG_DK = 64
HG_DV = 64
HG_W = HG_HEADS * HG_DK
FN_GROUPS = 4
FN_WIDTH = 256
SWA_HEADS = 4
SWA_KV_HEADS = 2
SWA_HD = 64
WINDOW = 128
N_EXPERTS = 64
TOP_K = 6
D_EXPERT = 256
D_SHARED = 256
ROUTE_SCALE = 2.5

MLA_IN = MLA_Q_LORA + MLA_KV_LORA + MLA_ROPE
HG_IN = 3 * HG_HEADS * HG_DK + 2 * HG_HEADS * HG_DV
FN_IN = FN_WIDTH
SWA_IN = (SWA_HEADS + 2 * SWA_KV_HEADS) * SWA_HD

N_CTX = BATCH * SEQ
N_LAT = DEC_BATCH * DEC_SEQ
N_TOK = N_CTX + N_LAT

MLA_PACK = 512
U_COLS = MLA_PACK + HG_IN + FN_IN + SWA_IN

ROW_TILE = 256
CTX_TILES = N_CTX // ROW_TILE
LAT_TILES_PER_BATCH = DEC_SEQ // ROW_TILE
HG_CHUNK = 32
HG_BLOCK = 256
SWA_QBLK = 128
MLA_QBLK = 256
CTX_PER_STEP = 4
HG_CTX_PER_STEP = 2
OUT_ROWS = 2 * ROW_TILE
VMEM_LIMIT = 56 * 1024 * 1024


def _dot(a, b):
    return jnp.dot(a, b, preferred_element_type=F32)


def _dot_nt(a, b):
    return lax.dot_general(a, b, (((1,), (1,)), ((), ())), preferred_element_type=F32)


def _dot_tn(a, b):
    return lax.dot_general(a, b, (((0,), (0,)), ((), ())), preferred_element_type=F32)


def _split3(x):
    hi = x.astype(BF16)
    r1 = x - hi.astype(F32)
    mid = r1.astype(BF16)
    return hi, mid, (r1 - mid.astype(F32)).astype(BF16)


def _dot_exact_lhs(a, b):
    ab = a.astype(BF16)
    hi, mid, lo = _split3(b)
    return (_dot(ab, lo) + _dot(ab, mid)) + _dot(ab, hi)


def _dot_exact_rhs(a, b):
    bb = b.astype(BF16)
    hi, mid, lo = _split3(a)
    return (_dot(lo, bb) + _dot(mid, bb)) + _dot(hi, bb)


def _rms(x, g):
    return x * lax.rsqrt(jnp.mean(x * x, axis=-1, keepdims=True) + EPS) * g


def _silu(x):
    return x * jax.nn.sigmoid(x)


def _mod_row(i):
    return jnp.where(i < CTX_TILES, 0, 1 + (i - CTX_TILES) // LAT_TILES_PER_BATCH)


def _params(*sem):
    return pltpu.CompilerParams(dimension_semantics=sem, vmem_limit_bytes=VMEM_LIMIT)


ADA_COLS = 1536


def _ada_kernel(cv_ref, w_ref, b_ref, o_ref):
    a = _silu(cv_ref[...]).astype(BF16)
    o_ref[...] = _dot(a, w_ref[...].astype(BF16)) + b_ref[...]


def _ada(cv8, w_ada, b_ada):
    return pl.pallas_call(
        _ada_kernel,
        grid=(DEPTH, 6 * D_MODEL // ADA_COLS),
        in_specs=[
            pl.BlockSpec((8, D_MODEL), lambda l, j: (0, 0)),
            pl.BlockSpec((None, D_MODEL, ADA_COLS), lambda l, j: (l, 0, j)),
            pl.BlockSpec((None, 1, ADA_COLS), lambda l, j: (l, 0, j)),
        ],
        out_specs=pl.BlockSpec((None, 8, ADA_COLS), lambda l, j: (l, 0, j)),
        out_shape=jax.ShapeDtypeStruct((DEPTH, 8, 6 * D_MODEL), F32),
        compiler_params=_params("arbitrary", "arbitrary"),
        name="ada",
    )(cv8, w_ada, b_ada.reshape(DEPTH, 1, 6 * D_MODEL))


def _ctx_lat_specs(width, tile=ROW_TILE):
    n_ctx = N_CTX // tile
    return [pl.BlockSpec((tile, width), lambda i, *_: (jnp.minimum(i, n_ctx - 1), 0)),
            pl.BlockSpec((tile, width), lambda i, *_: (jnp.maximum(i - n_ctx, 0), 0))]


def _row_specs(parts, width, tile=ROW_TILE):
    if len(parts) == 2:
        return _ctx_lat_specs(width, tile)
    return [pl.BlockSpec((tile, width), lambda i, *_: (i, 0))]


def _read_rows(refs, tile=ROW_TILE):
    if len(refs) == 1:
        return refs[0][...]
    return jnp.where(pl.program_id(0) < N_CTX // tile, refs[0][...], refs[1][...])


def _in_kernel(*refs, n_x):
    x = _read_rows(refs[:n_x], OUT_ROWS)
    mod_ref, g_ref, w_ref, umla_ref, uhg_ref, ufn_ref, uswa_ref = refs[n_x:]
    sh1 = mod_ref[:, 0:D_MODEL]
    sc1 = mod_ref[:, D_MODEL:2 * D_MODEL]
    h = _rms(x, g_ref[...]) * (1.0 + sc1) + sh1
    u = _dot(h.astype(BF16), w_ref[...])
    o = 0
    for ref, width in ((umla_ref, MLA_PACK), (uhg_ref, HG_IN), (ufn_ref, FN_IN), (uswa_ref, SWA_IN)):
        ref[...] = u[:, o:o + width].astype(ref.dtype)
        o += width


def _in_proj(x_parts, mod, g, w):
    row = lambda i: (i, 0)
    widths = (MLA_PACK, HG_IN, FN_IN, SWA_IN)
    return pl.pallas_call(
        functools.partial(_in_kernel, n_x=len(x_parts)),
        grid=(N_TOK // OUT_ROWS,),
        in_specs=_row_specs(x_parts, D_MODEL, OUT_ROWS) + [
            pl.BlockSpec((None, 1, 6 * D_MODEL), lambda i: (_mod_row(i * (OUT_ROWS // ROW_TILE)), 0, 0)),
            pl.BlockSpec((1, D_MODEL), lambda i: (0, 0)),
            pl.BlockSpec((D_MODEL, U_COLS), lambda i: (0, 0))],
        out_specs=[pl.BlockSpec((OUT_ROWS, wd), row) for wd in widths],
        out_shape=[jax.ShapeDtypeStruct((N_TOK, wd), F32) for wd in widths],
        compiler_params=_params("arbitrary"),
        name="in_proj",
    )(*x_parts, mod, g, w)


def _rope(x, cos, sin_signed):
    lane = lax.broadcasted_iota(jnp.int32, x.shape, 1)
    width = x.shape[1]
    swapped = jnp.where(lane % 2 == 0, pltpu.roll(x, width - 1, 1), pltpu.roll(x, 1, 1))
    return x * cos + swapped * sin_signed


def _stack_heads(x, n_heads, head_w):
    lane = lax.broadcasted_iota(jnp.int32, x.shape, 1)
    return jnp.concatenate([jnp.where(lane // head_w == h, x, 0.0) for h in range(n_heads)], axis=0)


def _unstack_heads(o, n_heads, head_w):
    t = o.shape[0] // n_heads
    lane = lax.broadcasted_iota(jnp.int32, (t, o.shape[1]), 1)
    out = jnp.zeros((t, o.shape[1]), F32)
    for h in range(n_heads):
        out = jnp.where(lane // head_w == h, o[h * t:(h + 1) * t], out)
    return out


MLA_SCALE = (MLA_NOPE + MLA_ROPE) ** -0.5
MLA_QW = MLA_HEADS * MLA_NOPE + MLA_HEADS * MLA_ROPE
MLA_NW = MLA_HEADS * MLA_NOPE


def _mla_attend(q, kcat, v):
    qs = jnp.concatenate([_stack_heads(q[:, :MLA_NW], MLA_HEADS, MLA_NOPE),
                          _stack_heads(q[:, MLA_NW:], MLA_HEADS, MLA_ROPE)], axis=1)
    s = _dot_nt(qs.astype(BF16), kcat) * MLA_SCALE
    p = jnp.exp(s - jnp.max(s, axis=-1, keepdims=True))
    o = _dot(p.astype(BF16), v) / jnp.sum(p, axis=-1, keepdims=True)
    return _unstack_heads(o, MLA_HEADS, MLA_V)


def _mla_ctx_kernel(u_ref, qg_ref, wq_ref, kvg_ref, wkv_ref, o_ref, ckv_ref):
    for j in range(CTX_PER_STEP):
        rows = slice(j * SEQ, (j + 1) * SEQ)
        u = u_ref[rows, :]
        q = _dot(_rms(u[:, :MLA_Q_LORA], qg_ref[...]).astype(BF16), wq_ref[...])
        ckv = _rms(u[:, MLA_Q_LORA:MLA_Q_LORA + MLA_KV_LORA], kvg_ref[...])
        ckv_ref[rows, :] = ckv
        kv = _dot(ckv.astype(BF16), wkv_ref[...])
        kr4 = u[:, MLA_Q_LORA + MLA_KV_LORA:]
        kcat = jnp.concatenate([kv[:, :MLA_NW], kr4], axis=1).astype(BF16)
        o_ref[rows, :] = _mla_attend(q, kcat, kv[:, MLA_NW:].astype(BF16)).astype(o_ref.dtype)


def _mla_ctx(u_mla, qg, wq, kvg, wkv):
    full = lambda shape: pl.BlockSpec(shape, lambda b: (0, 0))
    rows = CTX_PER_STEP * SEQ
    return pl.pallas_call(
        _mla_ctx_kernel,
        grid=(BATCH // CTX_PER_STEP,),
        in_specs=[pl.BlockSpec((rows, MLA_PACK), lambda b: (b, 0)),
                  full((1, MLA_Q_LORA)), full((MLA_Q_LORA, MLA_QW)),
                  full((1, MLA_KV_LORA)), full((MLA_KV_LORA, 2 * MLA_NW))],
        out_specs=[pl.BlockSpec((rows, MLA_NW), lambda b: (b, 0)),
                   pl.BlockSpec((rows, MLA_KV_LORA), lambda b: (b, 0))],
        out_shape=[jax.ShapeDtypeStruct((N_CTX, MLA_NW), BF16),
                   jax.ShapeDtypeStruct((N_CTX, MLA_KV_LORA), F32)],
        compiler_params=_params("arbitrary"),
        name="mla_ctx",
    )(u_mla, qg, wq, kvg, wkv)


MLA_TK = PAST_LEN + DEC_SEQ


def _mla_lat_kernel(u_ref, cckv_ref, ckr_ref, cos_ref, sin_ref, qg_ref, wq_ref, kvg_ref, wkv_ref,
                    o_ref, kcat_s, v_s):
    i = pl.program_id(1)

    @pl.when(i == 0)
    def _():
        u = u_ref[...]
        ckv_new = _rms(u[:, MLA_Q_LORA:MLA_Q_LORA + MLA_KV_LORA], kvg_ref[...])
        ckv_all = jnp.concatenate([cckv_ref[...], ckv_new], axis=0)
        kv = _dot(ckv_all.astype(BF16), wkv_ref[...])
        kr_new = _rope(u[:, MLA_Q_LORA + MLA_KV_LORA:], cos_ref[...], sin_ref[...])
        ckr = ckr_ref[...]
        kr_all = jnp.concatenate([jnp.concatenate([ckr] * MLA_HEADS, axis=1), kr_new], axis=0)
        kcat_s[...] = jnp.concatenate([kv[:, :MLA_NW], kr_all], axis=1).astype(BF16)
        v_s[...] = kv[:, MLA_NW:].astype(BF16)

    r0 = pl.multiple_of(i * MLA_QBLK, MLA_QBLK)
    cq = u_ref[pl.ds(r0, MLA_QBLK), 0:MLA_Q_LORA]
    q = _dot(_rms(cq, qg_ref[...]).astype(BF16), wq_ref[...])
    qr = _rope(q[:, MLA_NW:], cos_ref[pl.ds(r0, MLA_QBLK), :], sin_ref[pl.ds(r0, MLA_QBLK), :])
    q = jnp.concatenate([q[:, :MLA_NW], qr], axis=1)
    o_ref[...] = _mla_attend(q, kcat_s[...], v_s[...]).astype(o_ref.dtype)


def _mla_lat(u_mla, cache_ckv, cache_kr, l, cos, sin, qg, wq, kvg, wkv):
    full = lambda shape: pl.BlockSpec(shape, lambda b, i: (0, 0))
    nq = DEC_SEQ // MLA_QBLK
    return pl.pallas_call(
        _mla_lat_kernel,
        grid=(DEC_BATCH, nq),
        in_specs=[pl.BlockSpec((DEC_SEQ, MLA_PACK), lambda b, i: (N_CTX // DEC_SEQ + b, 0)),
                  pl.BlockSpec((None, None, PAST_LEN, MLA_KV_LORA), lambda b, i: (b, l, 0, 0)),
                  pl.BlockSpec((None, None, PAST_LEN, MLA_ROPE), lambda b, i: (b, l, 0, 0)),
                  full((DEC_SEQ, MLA_HEADS * MLA_ROPE)), full((DEC_SEQ, MLA_HEADS * MLA_ROPE)),
                  full((1, MLA_Q_LORA)), full((MLA_Q_LORA, MLA_QW)),
                  full((1, MLA_KV_LORA)), full((MLA_KV_LORA, 2 * MLA_NW))],
        out_specs=pl.BlockSpec((MLA_QBLK, MLA_NW), lambda b, i: (b * nq + i, 0)),
        out_shape=jax.ShapeDtypeStruct((N_LAT, MLA_NW), BF16),
        scratch_shapes=[pltpu.VMEM((MLA_TK, MLA_QW), BF16), pltpu.VMEM((MLA_TK, MLA_NW), BF16)],
        compiler_params=_params("arbitrary", "arbitrary"),
        name="mla_lat",
    )(u_mla, cache_ckv, cache_kr, cos, sin, qg, wq, kvg, wkv)


def _hgrn_kernel(*refs, seq, n_seq, has_state):
    if has_state:
        (u_ref, lbf_ref, lbb_ref, ng_ref, s0_ref, o_ref,
         q_s, kf_s, gf_s, kb_s, gb_s, of_s, ob_s, stf_s, stb_s) = refs
    else:
        (u_ref, lbf_ref, lbb_ref, ng_ref, o_ref, so_ref,
         q_s, kf_s, gf_s, kb_s, gb_s, of_s, ob_s, stf_s, stb_s) = refs
    C = HG_CHUNK
    W = HG_W

    q_s[...] = _silu(u_ref[:, 0:W])
    ff = lbf_ref[...] + (1.0 - lbf_ref[...]) * jax.nn.sigmoid(u_ref[:, W:2 * W])
    kf_s[...] = 1.0 - ff
    gf_s[...] = jnp.log(ff)
    fb = lbb_ref[...] + (1.0 - lbb_ref[...]) * jax.nn.sigmoid(u_ref[:, 2 * W:3 * W])
    kb_s[...] = 1.0 - fb
    gb_s[...] = jnp.log(fb)

    rr = lax.broadcasted_iota(jnp.int32, (W, W), 0)
    cc = lax.broadcasted_iota(jnp.int32, (W, W), 1)
    blockdiag = rr // HG_DK == cc // HG_DK
    if has_state:
        for st, d in ((stf_s, 0), (stb_s, 1)):
            rows = []
            for h in range(HG_HEADS):
                z = lambda n: jnp.zeros((HG_DV, n * HG_DK), F32)
                parts = ([z(h)] if h else []) + [s0_ref[d, h]] + ([z(HG_HEADS - 1 - h)] if h < HG_HEADS - 1 else [])
                rows.append(jnp.concatenate(parts, axis=1) if len(parts) > 1 else parts[0])
            st[0] = jnp.concatenate(rows, axis=0)
    else:
        stf_s[...] = jnp.zeros_like(stf_s)
        stb_s[...] = jnp.zeros_like(stb_s)

    B = HG_BLOCK
    per_block = B // C
    n_blocks = seq // B
    ri = lax.broadcasted_iota(jnp.int32, (B, B), 0)
    ci = lax.broadcasted_iota(jnp.int32, (B, B), 1)
    same_chunk = ri // C == ci // C
    rs = lax.broadcasted_iota(jnp.int32, (HG_HEADS * B, B), 0) % B
    cs = lax.broadcasted_iota(jnp.int32, (HG_HEADS * B, B), 1)
    same_chunk_s = rs // C == cs // C

    sums_f = jnp.where(same_chunk & (ci <= ri), 1.0, 0.0)
    sums_b = jnp.where(same_chunk & (ci >= ri), 1.0, 0.0)
    keep_f = same_chunk_s & (rs >= cs)
    keep_b = same_chunk_s & (cs >= rs)

    def chunk_row(x, i):
        x3 = x.reshape(per_block, C, W)
        return jnp.broadcast_to(x3[:, i:i + 1, :], (per_block, C, W)).reshape(B, W)

    def block(r, k_s, g_s, o_s, st_s, sum_mat, keep, order, mid, far):
        q = q_s[pl.ds(r, B), :]
        k = k_s[pl.ds(r, B), :]
        v = u_ref[pl.ds(r, B), 3 * W:4 * W].astype(BF16)
        G = _dot_exact_lhs(sum_mat, g_s[pl.ds(r, B), :])
        Gq = G - chunk_row(G, mid)
        Gk2 = chunk_row(G, far) - G
        qe = _stack_heads(q * jnp.exp(Gq), HG_HEADS, HG_DK)
        ke = k * jnp.exp(-Gq)
        A = jnp.where(keep, _dot_nt(qe.astype(BF16), ke.astype(BF16)), 0.0)
        o_intra = _unstack_heads(_dot(A.astype(BF16), v), HG_HEADS, HG_DV)
        qg = (q * jnp.exp(G)).astype(BF16)
        k2 = (k * jnp.exp(Gk2)).astype(BF16)
        decay = jnp.exp(G + Gk2)
        st = st_s[...]
        o_inter = [None] * per_block
        for c in order:
            rows = slice(c * C, (c + 1) * C)
            o_inter[c] = _dot_nt(st.astype(BF16), qg[rows])
            st = st * decay[c * C:c * C + 1] + jnp.where(blockdiag, _dot_tn(v[rows], k2[rows]), 0.0)
        st_s[...] = st
        o_s[pl.ds(r, B), :] = o_intra + jnp.concatenate(o_inter, axis=1).T

    def fwd(j, r):
        block(r, kf_s, gf_s, of_s, stf_s.at[j], sums_f, keep_f, range(per_block), C // 2 - 1, C - 1)

    def bwd(j, r):
        block(r, kb_s, gb_s, ob_s, stb_s.at[j], sums_b, keep_b, range(per_block - 1, -1, -1), C // 2, 0)

    for j in range(n_seq):
        if n_blocks == 1:
            fwd(j, j * seq)
            bwd(j, j * seq)
        else:
            lax.fori_loop(0, n_blocks, lambda i, c, j=j: (fwd(j, pl.multiple_of(j * seq + i * B, B)), c)[1], 0)
            lax.fori_loop(0, n_blocks,
                          lambda i, c, j=j: (bwd(j, pl.multiple_of(j * seq + (n_blocks - 1 - i) * B, B)), c)[1], 0)

    o = of_s[...] + ob_s[...]
    ms = _dot_exact_rhs(o * o, jnp.where(blockdiag, 1.0 / HG_DV, 0.0))
    on = o * lax.rsqrt(ms + EPS) * ng_ref[...]
    o_ref[...] = (on * _silu(u_ref[:, 4 * W:5 * W])).astype(o_ref.dtype)

    if not has_state:
        for j in range(n_seq):
            for st, d in ((stf_s, 0), (stb_s, 1)):
                for h in range(HG_HEADS):
                    so_ref[j, d, h] = st[j, h * HG_DV:(h + 1) * HG_DV, h * HG_DK:(h + 1) * HG_DK]


def _hgrn(u_hg, lbf, lbb, ng4, state_t, *, seq, n_batch, row_block0):
    has_state = state_t is not None
    n_seq = 1 if has_state else HG_CTX_PER_STEP
    rows = n_seq * seq
    full = lambda shape: pl.BlockSpec(shape, lambda b: (0, 0))
    in_specs = [pl.BlockSpec((rows, HG_IN), lambda b: (row_block0 + b, 0)),
                full((1, HG_W)), full((1, HG_W)), full((1, HG_W))]
    args = [u_hg, lbf, lbb, ng4]
    o_spec = pl.BlockSpec((rows, HG_W), lambda b: (b, 0))
    o_shape = jax.ShapeDtypeStruct((n_batch * seq, HG_W), BF16)
    if has_state:
        in_specs.append(pl.BlockSpec((None, 2, HG_HEADS, HG_DV, HG_DK), lambda b: (b, 0, 0, 0, 0)))
        args.append(state_t)
        out_specs, out_shape = o_spec, o_shape
    else:
        out_specs = [o_spec, pl.BlockSpec((n_seq, 2, HG_HEADS, HG_DV, HG_DK), lambda b: (b, 0, 0, 0, 0))]
        out_shape = [o_shape, jax.ShapeDtypeStruct((n_batch, 2, HG_HEADS, HG_DV, HG_DK), F32)]
    return pl.pallas_call(
        functools.partial(_hgrn_kernel, seq=seq, n_seq=n_seq, has_state=has_state),
        grid=(n_batch // n_seq,),
        in_specs=in_specs, out_specs=out_specs, out_shape=out_shape,
        scratch_shapes=[pltpu.VMEM((rows, HG_W), F32)] * 7 + [pltpu.VMEM((n_seq, HG_W, HG_W), F32)] * 2,
        compiler_params=_params("arbitrary"),
        name="hgrn_lat" if has_state else "hgrn_ctx",
    )(*args)


def _dft_tables(n):
    j = np.arange(n, dtype=np.int64)
    ang = 2.0 * np.pi * ((j[:, None] * j[None, :]) % n).astype(np.float64) / n
    return np.cos(ang) / np.sqrt(n), np.sin(ang) / np.sqrt(n)


def _fourier_tables(seq):
    gw = FN_WIDTH // FN_GROUPS
    cg, sg = _dft_tables(gw)
    eye = np.eye(FN_GROUPS)
    chan = np.concatenate([np.kron(eye, cg), np.kron(eye, sg)], axis=1)
    ct, st = _dft_tables(seq)
    pos = np.concatenate([ct, -st], axis=1)
    return jnp.asarray(chan, F32).astype(BF16), jnp.asarray(pos, F32).astype(BF16)


def _fourier_kernel(x_ref, chan_ref, pos_ref, w_ref, o_ref, *, seq, n_seq):
    for j in range(n_seq):
        rows = slice(j * seq, (j + 1) * seq)
        x12 = _dot(x_ref[rows, :].astype(BF16), chan_ref[...])
        z = jnp.concatenate([x12[:, :FN_WIDTH], x12[:, FN_WIDTH:]], axis=0).astype(BF16)
        y = _dot(pos_ref[...], z)
        o_ref[rows, :] = _dot(y.astype(BF16), w_ref[...]).astype(o_ref.dtype)


def _fourier(u_fn, w, *, seq, n_batch, row_block0):
    chan, pos = _fourier_tables(seq)
    full = lambda shape: pl.BlockSpec(shape, lambda b: (0, 0))
    n_seq = CTX_PER_STEP if seq == SEQ else 1
    return pl.pallas_call(
        functools.partial(_fourier_kernel, seq=seq, n_seq=n_seq),
        grid=(n_batch // n_seq,),
        in_specs=[pl.BlockSpec((n_seq * seq, FN_WIDTH), lambda b: (row_block0 + b, 0)),
                  full((FN_WIDTH, 2 * FN_WIDTH)), full((seq, 2 * seq)), full((FN_WIDTH, FN_WIDTH))],
        out_specs=pl.BlockSpec((n_seq * seq, FN_WIDTH), lambda b: (b, 0)),
        out_shape=jax.ShapeDtypeStruct((n_batch * seq, FN_WIDTH), BF16),
        compiler_params=_params("arbitrary"),
        name="fourier",
    )(u_fn, chan, pos, w)


SWA_SCALE = SWA_HD ** -0.5
SWA_QW = SWA_HEADS * SWA_HD
SWA_KW = SWA_KV_HEADS * SWA_HD
SWA_STACK_ORDER = (0, 2, 1, 3)


def _swa_stack_q(q):
    return jnp.concatenate([_stack_heads(q[:, :SWA_KW], SWA_KV_HEADS, SWA_HD),
                            _stack_heads(q[:, SWA_KW:], SWA_KV_HEADS, SWA_HD)], axis=0)


def _swa_unstack_o(o):
    t = o.shape[0] // SWA_HEADS
    return jnp.concatenate([_unstack_heads(o[:2 * t], SWA_KV_HEADS, SWA_HD),
                            _unstack_heads(o[2 * t:], SWA_KV_HEADS, SWA_HD)], axis=1)


def _sink_rows(sink_ref, t):
    return jnp.concatenate([jnp.full((t, 1), sink_ref[h], F32) for h in SWA_STACK_ORDER], axis=0)


def _swa_ctx_kernel(sink_ref, u_ref, o_ref):
    sink = _sink_rows(sink_ref, SEQ)
    for j in range(CTX_PER_STEP):
        rows = slice(j * SEQ, (j + 1) * SEQ)
        u = u_ref[rows, :]
        qs = _swa_stack_q(u[:, :SWA_QW]).astype(BF16)
        k = u[:, SWA_QW:SWA_QW + SWA_KW].astype(BF16)
        v = u[:, SWA_QW + SWA_KW:].astype(BF16)
        s = _dot_nt(qs, k) * SWA_SCALE
        m = jnp.maximum(jnp.max(s, axis=-1, keepdims=True), sink)
        p = jnp.exp(s - m)
        denom = jnp.sum(p, axis=-1, keepdims=True) + jnp.exp(sink - m)
        o_ref[rows, :] = _swa_unstack_o(_dot(p.astype(BF16), v) / denom).astype(o_ref.dtype)


def _swa_ctx(u_swa, sink):
    rows = CTX_PER_STEP * SEQ
    return pl.pallas_call(
        _swa_ctx_kernel,
        grid=(BATCH // CTX_PER_STEP,),
        in_specs=[pl.BlockSpec(memory_space=pltpu.SMEM),
                  pl.BlockSpec((rows, SWA_IN), lambda b: (b, 0))],
        out_specs=pl.BlockSpec((rows, SWA_QW), lambda b: (b, 0)),
        out_shape=jax.ShapeDtypeStruct((N_CTX, SWA_QW), BF16),
        compiler_params=_params("arbitrary"),
        name="swa_ctx",
    )(sink, u_swa)


SWA_PAD = DEC_SEQ + 2 * SWA_QBLK


def _swa_lat_kernel(sink_ref, u_ref, kc_ref, vc_ref, cosq_ref, sinq_ref, cosk_ref, sin_k_ref,
                    o_ref, k_s, v_s):
    i = pl.program_id(1)
    B = SWA_QBLK

    @pl.when(i == 0)
    def _():
        zeros = jnp.zeros((B, SWA_KW), BF16)
        k = _rope(u_ref[:, SWA_QW:SWA_QW + SWA_KW], cosk_ref[...], sin_k_ref[...]).astype(BF16)
        k_s[...] = jnp.concatenate([zeros, k, zeros], axis=0)
        v_s[...] = jnp.concatenate([zeros, u_ref[:, SWA_QW + SWA_KW:].astype(BF16), zeros], axis=0)

    r0 = pl.multiple_of(i * B, B)
    q = _rope(u_ref[pl.ds(r0, B), 0:SWA_QW], cosq_ref[pl.ds(r0, B), :], sinq_ref[pl.ds(r0, B), :])
    qs = _swa_stack_q(q).astype(BF16)
    s_loc = _dot_nt(qs, k_s[pl.ds(r0, 3 * B), :]) * SWA_SCALE
    row = lax.broadcasted_iota(jnp.int32, s_loc.shape, 0) % B
    col = lax.broadcasted_iota(jnp.int32, s_loc.shape, 1)
    kpos = r0 - B + col
    valid = (jnp.abs(row + B - col) <= WINDOW) & (kpos >= 0) & (kpos < DEC_SEQ)
    s_loc = jnp.where(valid, s_loc, NEG_INF)
    s_ctx = _dot_nt(qs, kc_ref[...].astype(BF16)) * SWA_SCALE
    sink = _sink_rows(sink_ref, B)
    m = jnp.maximum(jnp.maximum(jnp.max(s_loc, axis=-1, keepdims=True),
                                jnp.max(s_ctx, axis=-1, keepdims=True)), sink)
    p_loc = jnp.exp(s_loc - m)
    p_ctx = jnp.exp(s_ctx - m)
    denom = (jnp.sum(p_loc, axis=-1, keepdims=True) + jnp.sum(p_ctx, axis=-1, keepdims=True)
             + jnp.exp(sink - m))
    o = _dot(p_loc.astype(BF16), v_s[pl.ds(r0, 3 * B), :]) + _dot(p_ctx.astype(BF16), vc_ref[...].astype(BF16))
    o_ref[...] = _swa_unstack_o(o / denom).astype(o_ref.dtype)


def _swa_lat(u_swa, cache_k, cache_v, l, sink, cosq, sinq, cosk, sink_k):
    full = lambda shape: pl.BlockSpec(shape, lambda b, i: (0, 0))
    nq = DEC_SEQ // SWA_QBLK
    cache_spec = pl.BlockSpec((None, None, PAST_LEN, SWA_KW), lambda b, i: (b, l, 0, 0))
    return pl.pallas_call(
        _swa_lat_kernel,
        grid=(DEC_BATCH, nq),
        in_specs=[pl.BlockSpec(memory_space=pltpu.SMEM),
                  pl.BlockSpec((DEC_SEQ, SWA_IN), lambda b, i: (N_CTX // DEC_SEQ + b, 0)),
                  cache_spec, cache_spec,
                  full((DEC_SEQ, SWA_QW)), full((DEC_SEQ, SWA_QW)),
                  full((DEC_SEQ, SWA_KW)), full((DEC_SEQ, SWA_KW))],
        out_specs=pl.BlockSpec((SWA_QBLK, SWA_QW), lambda b, i: (b * nq + i, 0)),
        out_shape=jax.ShapeDtypeStruct((N_LAT, SWA_QW), BF16),
        scratch_shapes=[pltpu.VMEM((SWA_PAD, SWA_KW), BF16), pltpu.VMEM((SWA_PAD, SWA_KW), BF16)],
        compiler_params=_params("arbitrary", "arbitrary"),
        name="swa_lat",
    )(sink, u_swa, cache_k, cache_v, cosq, sinq, cosk, sink_k)


N_BLK = N_TOK // ROW_TILE
SEG_ALIGN = 16
LOCAL_ROWS = ROW_TILE * TOP_K + N_EXPERTS * SEG_ALIGN
LOCAL_CHUNK = 512
EXPERT_TILE = 768
SORTED_ROWS = -(-(N_TOK * TOP_K + N_BLK * N_EXPERTS * SEG_ALIGN + N_EXPERTS * (EXPERT_TILE + SEG_ALIGN))
                // EXPERT_TILE) * EXPERT_TILE
MAX_TILES = SORTED_ROWS // EXPERT_TILE
NOT_PICKED = -1.0
NO_DEST = 4095.0


def _out_kernel(*refs, n_x):
    x_all = _read_rows(refs[:n_x], OUT_ROWS)
    mix_all = [_read_rows(refs[n_x + 2 * j:n_x + 2 * j + 2], OUT_ROWS) for j in range(4)]
    out_refs = refs[n_x + 8:]
    for blk in range(OUT_ROWS // ROW_TILE):
        rows = slice(blk * ROW_TILE, (blk + 1) * ROW_TILE)
        _out_block(x_all[rows], [m[rows] for m in mix_all], out_refs, blk)


def _out_block(x, mixers, refs, blk):
    (mod_ref, g_ref, wo_ref, wr_ref, br_ref, wsg_ref, wsu_ref, wsd_ref,
     x1_ref, h_ref, gate_ref, rank_ref, cnt_ref) = refs
    rows = slice(blk * ROW_TILE, (blk + 1) * ROW_TILE)
    mix = jnp.zeros((ROW_TILE, D_MODEL), F32)
    for j in range(4):
        mix = mix + _dot(mixers[j], wo_ref[j * 256:(j + 1) * 256, :])
    g1 = mod_ref[:, 2 * D_MODEL:3 * D_MODEL]
    sh2 = mod_ref[:, 3 * D_MODEL:4 * D_MODEL]
    sc2 = mod_ref[:, 4 * D_MODEL:5 * D_MODEL]
    g2 = mod_ref[:, 5 * D_MODEL:6 * D_MODEL]
    x1 = x + g1 * mix
    h = _rms(x1, g_ref[...]) * (1.0 + sc2) + sh2
    hb = h.astype(BF16)
    h_ref[rows, :] = hb

    w_hi = wr_ref[...].astype(BF16)
    w_lo = (wr_ref[...] - w_hi.astype(F32)).astype(BF16)
    h_lo = (h - hb.astype(F32)).astype(BF16)
    logits = ((_dot(h_lo, w_hi) + _dot(hb, w_lo)) + _dot(hb, w_hi)).T
    scores = jax.nn.sigmoid(logits)
    sel = scores + br_ref[...]
    eidx = lax.broadcasted_iota(jnp.int32, sel.shape, 0)
    gate = jnp.zeros_like(scores)
    picked = jnp.zeros_like(scores)
    for _ in range(TOP_K):
        best = jnp.max(sel, axis=0, keepdims=True)
        first = jnp.min(jnp.where(sel == best, eidx, N_EXPERTS), axis=0, keepdims=True)
        pick = eidx == first
        gate = jnp.where(pick, scores, gate)
        picked = jnp.where(pick, 1.0, picked)
        sel = jnp.where(pick, -jnp.inf, sel)
    gate = ROUTE_SCALE * gate / jnp.sum(gate, axis=0, keepdims=True)

    ti = lax.broadcasted_iota(jnp.int32, (ROW_TILE, ROW_TILE), 0)
    tj = lax.broadcasted_iota(jnp.int32, (ROW_TILE, ROW_TILE), 1)
    pb = picked.astype(BF16)
    rank = _dot(pb, jnp.where(ti < tj, 1.0, 0.0).astype(BF16))
    gate_ref[:, rows] = gate
    rank_ref[:, rows] = jnp.where(picked > 0.0, rank, NOT_PICKED)
    counts = _dot_nt(jnp.ones((8, ROW_TILE), BF16), pb)
    cnt_ref[blk] = jnp.concatenate([counts, jnp.zeros_like(counts)], axis=1)

    hid = _silu(_dot(hb, wsg_ref[...])) * _dot(hb, wsu_ref[...])
    x1_ref[rows, :] = x1 + g2 * _dot(hid.astype(BF16), wsd_ref[...])


def _out_proj(x_parts, mixer_pairs, mod, g, wo, wr, br, wsg, wsu, wsd):
    row = lambda i: (i, 0)
    col = lambda i: (0, i)
    full = lambda shape: pl.BlockSpec(shape, lambda i: (0, 0))
    per_step = OUT_ROWS // ROW_TILE
    et_spec = pl.BlockSpec((N_EXPERTS, OUT_ROWS), col)
    et_shape = jax.ShapeDtypeStruct((N_EXPERTS, N_TOK), F32)
    return pl.pallas_call(
        functools.partial(_out_kernel, n_x=len(x_parts)),
        grid=(N_TOK // OUT_ROWS,),
        in_specs=_row_specs(x_parts, D_MODEL, OUT_ROWS) + 4 * _ctx_lat_specs(256, OUT_ROWS) + [
            pl.BlockSpec((None, 1, 6 * D_MODEL), lambda i: (_mod_row(i * per_step), 0, 0)),
            full((1, D_MODEL)), full((D_MODEL, D_MODEL)),
            full((D_MODEL, N_EXPERTS)), full((N_EXPERTS, 1)),
            full((D_MODEL, D_SHARED)), full((D_MODEL, D_SHARED)), full((D_SHARED, D_MODEL))],
        out_specs=[pl.BlockSpec((OUT_ROWS, D_MODEL), row), pl.BlockSpec((OUT_ROWS, D_MODEL), row),
                   et_spec, et_spec,
                   pl.BlockSpec((per_step, 8, 128), lambda i: (i, 0, 0))],
        out_shape=[jax.ShapeDtypeStruct((N_TOK, D_MODEL), F32),
                   jax.ShapeDtypeStruct((N_TOK, D_MODEL), BF16),
                   et_shape, et_shape,
                   jax.ShapeDtypeStruct((N_BLK, 8, 128), F32)],
        compiler_params=_params("arbitrary"),
        name="out_proj",
    )(*x_parts, *[a for pair in mixer_pairs for a in pair], mod, g, wo, wr, br, wsg, wsu, wsd)


def _segment_plan(cnt):
    cnt = cnt[:, 0, :N_EXPERTS].astype(jnp.int32)
    seg = jnp.maximum((cnt + (SEG_ALIGN - 1)) // SEG_ALIGN, 1) * SEG_ALIGN
    local = jnp.cumsum(seg, axis=1) - seg
    total = jnp.sum(seg, axis=1)
    per_expert = jnp.sum(seg, axis=0)
    padded = (per_expert + SEG_ALIGN + (EXPERT_TILE - 1)) // EXPERT_TILE * EXPERT_TILE
    ends = jnp.cumsum(padded)
    start = ends - padded
    dest = start[None, :] + jnp.cumsum(seg, axis=0) - seg
    n_tiles = ends[-1] // EXPERT_TILE
    tiles = jnp.arange(MAX_TILES, dtype=jnp.int32)
    tile_expert = jnp.sum((ends // EXPERT_TILE)[None, :] <= jnp.minimum(tiles, n_tiles - 1)[:, None], axis=1)
    tile_expert = tile_expert.astype(jnp.int32)
    plan = dict(seg=seg, local=local, total=total.astype(jnp.int32), dest=dest.astype(jnp.int32),
                tail_start=(start + per_expert).astype(jnp.int32), tail_rows=(padded - per_expert).astype(jnp.int32),
                n_tiles=n_tiles.reshape(1).astype(jnp.int32), tile_expert=tile_expert)
    segf, localf = seg.astype(F32), local.astype(F32)
    pad_lanes = lambda a: jnp.concatenate([a, jnp.zeros_like(a)], axis=1)[:, None, :]
    plan.update(seg_row=pad_lanes(segf), local_row=pad_lanes(localf),
                seg_col=segf[:, :, None], local_col=localf[:, :, None])
    return plan


def _local_dest_digits(rank, local_start):
    dest = jnp.where(rank >= 0.0, local_start + rank, NO_DEST)
    hi = jnp.floor(dest * (1.0 / 64.0))
    return hi, dest - 64.0 * hi


def _dispatch_kernel(dest_ref, seg_ref, local_ref, total_ref, tail_start_ref, tail_rows_ref, nt_ref,
                     h_ref, rank_ref, local_col_ref, local_row_ref, seg_row_ref,
                     xs_hbm, buf, zeros, sems, zsem, usem):
    b = pl.program_id(0)
    slot = b % 2

    def wait_block(blk, s):
        n = pl.multiple_of(total_ref[blk], SEG_ALIGN)
        pltpu.make_async_copy(buf.at[s, pl.ds(0, n)], xs_hbm.at[pl.ds(0, n)], sems.at[s]).wait()

    @pl.when(b == 0)
    def _():
        zeros[...] = jnp.zeros_like(zeros)

    def unused_tiles(action):
        for k in range(-(-MAX_TILES // N_BLK)):
            t = nt_ref[0] + b + k * N_BLK

            @pl.when(t < MAX_TILES)
            def _():
                cp = pltpu.make_async_copy(zeros.at[pl.ds(0, EXPERT_TILE)],
                                           xs_hbm.at[pl.ds(pl.multiple_of(t * EXPERT_TILE, EXPERT_TILE),
                                                           EXPERT_TILE)], usem)
                cp.start() if action == "start" else cp.wait()

    unused_tiles("start")

    @pl.when(b >= 2)
    def _():
        wait_block(b - 2, slot)

    hi, lo = _local_dest_digits(rank_ref[...], local_col_ref[...])
    code = jnp.concatenate([hi, lo], axis=0).astype(BF16)
    hb = h_ref[...]
    lstart = local_row_ref[:, :N_EXPERTS]
    lend = lstart + seg_row_ref[:, :N_EXPERTS]
    r = lax.broadcasted_iota(jnp.int32, (LOCAL_CHUNK, N_EXPERTS), 0).astype(F32)
    rr = lax.broadcasted_iota(jnp.int32, (LOCAL_CHUNK, ROW_TILE), 0).astype(F32)
    for c in range(LOCAL_ROWS // LOCAL_CHUNK):
        first = float(c * LOCAL_CHUNK)
        member = (r >= lstart - first) & (r < lend - first)
        table = jnp.concatenate([jnp.where(member, 64.0, 0.0), jnp.where(member, 1.0, 0.0)], axis=1).astype(BF16)
        d = _dot(table, code)
        onehot = jnp.where(d == rr + first, 1.0, 0.0).astype(BF16)
        buf[slot, c * LOCAL_CHUNK:(c + 1) * LOCAL_CHUNK, :] = _dot(onehot, hb).astype(BF16)

    for e in range(N_EXPERTS):
        n = pl.multiple_of(seg_ref[b, e], SEG_ALIGN)
        src = pl.multiple_of(local_ref[b, e], SEG_ALIGN)
        dst = pl.multiple_of(dest_ref[b, e], SEG_ALIGN)
        pltpu.make_async_copy(buf.at[slot, pl.ds(src, n)], xs_hbm.at[pl.ds(dst, n)], sems.at[slot]).start()

    unused_tiles("wait")

    @pl.when(b == N_BLK - 1)
    def _():
        def tail(e):
            n = pl.multiple_of(tail_rows_ref[e], SEG_ALIGN)
            dst = pl.multiple_of(tail_start_ref[e], SEG_ALIGN)
            return pltpu.make_async_copy(zeros.at[pl.ds(0, n)], xs_hbm.at[pl.ds(dst, n)], zsem)

        for e in range(N_EXPERTS):
            tail(e).start()
        for e in range(N_EXPERTS):
            tail(e).wait()
        wait_block(b - 1, 1 - slot)
        wait_block(b, slot)


def _dispatch(h, rank_t, plan):
    blk = lambda shape, imap: pl.BlockSpec(shape, imap)
    return pl.pallas_call(
        _dispatch_kernel,
        grid_spec=pltpu.PrefetchScalarGridSpec(
            num_scalar_prefetch=7, grid=(N_BLK,),
            in_specs=[blk((ROW_TILE, D_MODEL), lambda b, *_: (b, 0)),
                      blk((N_EXPERTS, ROW_TILE), lambda b, *_: (0, b)),
                      blk((None, N_EXPERTS, 1), lambda b, *_: (b, 0, 0)),
                      blk((None, 1, 128), lambda b, *_: (b, 0, 0)),
                      blk((None, 1, 128), lambda b, *_: (b, 0, 0))],
            out_specs=pl.BlockSpec(memory_space=pl.ANY),
            scratch_shapes=[pltpu.VMEM((2, LOCAL_ROWS, D_MODEL), BF16),
                            pltpu.VMEM((EXPERT_TILE + SEG_ALIGN, D_MODEL), BF16),
                            pltpu.SemaphoreType.DMA((2,)), pltpu.SemaphoreType.DMA(()),
                            pltpu.SemaphoreType.DMA(())]),
        out_shape=jax.ShapeDtypeStruct((SORTED_ROWS, D_MODEL), BF16),
        compiler_params=_params("arbitrary"),
        name="dispatch",
    )(plan['dest'], plan['seg'], plan['local'], plan['total'], plan['tail_start'], plan['tail_rows'],
      plan['n_tiles'], h, rank_t, plan['local_col'], plan['local_row'], plan['seg_row'])


def _expert_kernel(te_ref, nt_ref, x_ref, wg_ref, wu_ref, wd_ref, y_ref, wg_s, wu_s, wd_s):
    i = pl.program_id(0)
    active = i < nt_ref[0]

    @pl.when((i == 0) | (te_ref[i] != te_ref[jnp.maximum(i - 1, 0)]))
    def _():
        wg_s[...] = wg_ref[...].astype(BF16)
        wu_s[...] = wu_ref[...].astype(BF16)
        wd_s[...] = wd_ref[...].astype(BF16)

    @pl.when(active)
    def _():
        x = x_ref[...]
        hid = _silu(_dot(x, wg_s[...])) * _dot(x, wu_s[...])
        y_ref[...] = _dot(hid.astype(BF16), wd_s[...]).astype(BF16)


def _experts(xs, plan, w_gate, w_up, w_down, l):
    rows = pl.BlockSpec((EXPERT_TILE, D_MODEL), lambda i, te, nt: (jnp.minimum(i, nt[0] - 1), 0))
    wspec = lambda shape: pl.BlockSpec((None, None) + shape, lambda i, te, nt: (l, te[i], 0, 0))
    return pl.pallas_call(
        _expert_kernel,
        grid_spec=pltpu.PrefetchScalarGridSpec(
            num_scalar_prefetch=2, grid=(MAX_TILES,),
            in_specs=[rows, wspec((D_MODEL, D_EXPERT)), wspec((D_MODEL, D_EXPERT)), wspec((D_EXPERT, D_MODEL))],
            out_specs=rows,
            scratch_shapes=[pltpu.VMEM((D_MODEL, D_EXPERT), BF16), pltpu.VMEM((D_MODEL, D_EXPERT), BF16),
                            pltpu.VMEM((D_EXPERT, D_MODEL), BF16)]),
        out_shape=jax.ShapeDtypeStruct((SORTED_ROWS, D_MODEL), BF16),
        input_output_aliases={2: 0},
        compiler_params=_params("arbitrary"),
        name="experts",
    )(plan['tile_expert'], plan['n_tiles'], xs, w_gate, w_up, w_down)


def _combine_kernel(*refs, final):
    refs = list(refs)
    dest_ref, seg_ref, local_ref, total_ref = refs[:4]
    x1_ref, ys_hbm, gate_ref, rank_ref, mod_ref, local_row_ref, local_col_ref, seg_col_ref = refs[4:12]
    rest = refs[12:]
    fg_ref = rest.pop(0) if final else None
    outs, (buf, sems) = rest[:-2], rest[-2:]
    b = pl.program_id(0)
    slot = b % 2

    def fetch(blk, s):
        for e in range(N_EXPERTS):
            n = pl.multiple_of(seg_ref[blk, e], SEG_ALIGN)
            src = pl.multiple_of(dest_ref[blk, e], SEG_ALIGN)
            dst = pl.multiple_of(local_ref[blk, e], SEG_ALIGN)
            pltpu.make_async_copy(ys_hbm.at[pl.ds(src, n)], buf.at[s, pl.ds(dst, n)], sems.at[s]).start()

    @pl.when(b == 0)
    def _():
        buf[...] = jnp.zeros_like(buf)
        fetch(0, 0)

    def wait_rows(blk, s):
        n_rows = pl.multiple_of(total_ref[blk], SEG_ALIGN)
        pltpu.make_async_copy(ys_hbm.at[pl.ds(0, n_rows)], buf.at[s, pl.ds(0, n_rows)], sems.at[s]).wait()

    wait_rows(b, slot)
    nxt = jnp.minimum(b + 1, N_BLK - 1)
    fetch(nxt, 1 - slot)

    gate = gate_ref[...].T
    hi, lo = _local_dest_digits(rank_ref[...].T, local_row_ref[:, :N_EXPERTS])
    lhs = jnp.concatenate([jnp.concatenate([hi, lo], axis=1),
                           jnp.concatenate([jnp.zeros_like(gate), gate], axis=1)], axis=0).astype(BF16)
    lstart = local_col_ref[...]
    lend = lstart + seg_col_ref[...]
    routed = jnp.zeros((ROW_TILE, D_MODEL), F32)
    r = lax.broadcasted_iota(jnp.int32, (N_EXPERTS, LOCAL_CHUNK), 1).astype(F32)
    rr = lax.broadcasted_iota(jnp.int32, (ROW_TILE, LOCAL_CHUNK), 1).astype(F32)
    for c in range(LOCAL_ROWS // LOCAL_CHUNK):
        first = float(c * LOCAL_CHUNK)
        member = (r >= lstart - first) & (r < lend - first)
        table = jnp.concatenate([jnp.where(member, 64.0, 0.0), jnp.where(member, 1.0, 0.0)], axis=0).astype(BF16)
        dg = _dot(lhs, table)
        weights = jnp.where(dg[:ROW_TILE] == rr + first, dg[ROW_TILE:], 0.0).astype(BF16)
        routed = routed + _dot(weights, buf[slot, c * LOCAL_CHUNK:(c + 1) * LOCAL_CHUNK, :])
    x = x1_ref[...] + mod_ref[:, 5 * D_MODEL:6 * D_MODEL] * routed
    if final:
        y = _rms(x, fg_ref[...])

        @pl.when(b < CTX_TILES)
        def _():
            outs[0][...] = y

        @pl.when(b >= CTX_TILES)
        def _():
            outs[1][...] = y
    else:
        outs[0][...] = x

    @pl.when(b == N_BLK - 1)
    def _():
        wait_rows(nxt, 1 - slot)


def _combine(x1, ys, gate_t, rank_t, mod, plan, final_g):
    final = final_g is not None
    blk = lambda shape, imap: pl.BlockSpec(shape, imap)
    xspec = blk((ROW_TILE, D_MODEL), lambda b, *_: (b, 0))
    if final:
        out_specs = _ctx_lat_specs(D_MODEL)
        out_shape = [jax.ShapeDtypeStruct((N_CTX, D_MODEL), F32), jax.ShapeDtypeStruct((N_LAT, D_MODEL), F32)]
    else:
        out_specs, out_shape = xspec, jax.ShapeDtypeStruct((N_TOK, D_MODEL), F32)
    et_spec = blk((N_EXPERTS, ROW_TILE), lambda b, *_: (0, b))
    in_specs = [xspec, pl.BlockSpec(memory_space=pl.ANY), et_spec, et_spec,
                blk((None, 1, 6 * D_MODEL), lambda b, *_: (_mod_row(b), 0, 0)),
                blk((None, 1, 128), lambda b, *_: (b, 0, 0)),
                blk((None, N_EXPERTS, 1), lambda b, *_: (b, 0, 0)),
                blk((None, N_EXPERTS, 1), lambda b, *_: (b, 0, 0))]
    args = [x1, ys, gate_t, rank_t, mod, plan['local_row'], plan['local_col'], plan['seg_col']]
    if final:
        in_specs.append(blk((1, D_MODEL), lambda b, *_: (0, 0)))
        args.append(final_g)
    return pl.pallas_call(
        functools.partial(_combine_kernel, final=final),
        grid_spec=pltpu.PrefetchScalarGridSpec(
            num_scalar_prefetch=4, grid=(N_BLK,),
            in_specs=in_specs, out_specs=out_specs,
            scratch_shapes=[pltpu.VMEM((2, LOCAL_ROWS, D_MODEL), BF16), pltpu.SemaphoreType.DMA((2,))]),
        out_shape=out_shape,
        compiler_params=_params("arbitrary"),
        name="combine",
    )(plan['dest'], plan['seg'], plan['local'], plan['total'], *args)


def _rope_full_tables(dim, n_rep):
    rows = DEC_SEQ // GRID_W
    r_idx, c_idx = np.meshgrid(np.arange(rows), np.arange(GRID_W), indexing='ij')
    pos = jnp.asarray(np.stack([r_idx.reshape(-1), c_idx.reshape(-1)], axis=-1), F32)
    nf = dim // 4
    inv = ROPE_BASE ** (-jnp.arange(nf, dtype=F32) / nf)
    ang = pos[:, :, None] * inv
    ang = jnp.repeat(ang.reshape(DEC_SEQ, 2 * nf), 2, axis=1)
    sign = jnp.tile(jnp.asarray([-1.0, 1.0], F32), dim // 2)
    return jnp.tile(jnp.cos(ang), (1, n_rep)), jnp.tile(jnp.sin(ang) * sign, (1, n_rep))


def _pack_w_in(w):
    c0 = MLA_Q_LORA + MLA_KV_LORA
    kr = w[:, c0:MLA_IN]
    s0 = MLA_IN + HG_IN + FN_IN
    qh = [w[:, s0 + h * SWA_HD:s0 + (h + 1) * SWA_HD] for h in SWA_STACK_ORDER]
    return jnp.concatenate([w[:, :c0], kr, kr, kr, kr, w[:, MLA_IN:s0]] + qh
                           + [w[:, s0 + SWA_QW:]], axis=1).astype(BF16)


def _pack_w_q_up(w):
    hd = MLA_NOPE + MLA_ROPE
    nope = [w[:, h * hd:h * hd + MLA_NOPE] for h in range(MLA_HEADS)]
    rope = [w[:, h * hd + MLA_NOPE:(h + 1) * hd] for h in range(MLA_HEADS)]
    return jnp.concatenate(nope + rope, axis=1).astype(BF16)


def _pack_w_kv_up(w):
    hd = MLA_NOPE + MLA_V
    kn = [w[:, h * hd:h * hd + MLA_NOPE] for h in range(MLA_HEADS)]
    vv = [w[:, h * hd + MLA_NOPE:(h + 1) * hd] for h in range(MLA_HEADS)]
    return jnp.concatenate(kn + vv, axis=1).astype(BF16)


def _pack_w_out(w):
    s0 = 3 * 256
    rows = [w[s0 + h * SWA_HD:s0 + (h + 1) * SWA_HD] for h in SWA_STACK_ORDER]
    return jnp.concatenate([w[:s0]] + rows, axis=0).astype(BF16)


def kernel(x_prompt, x_sample, c, cache_mla_ckv, cache_mla_krope, cache_swa_k, cache_swa_v, state_hgrn,
           c_ctx, w_ada, b_ada, norm1_g, norm2_g, w_in, mla_q_norm_g, mla_w_q_up, mla_kv_norm_g, mla_w_kv_up,
           hg_lb_logits, hg_norm_g, fn_w, swa_sink, w_out, moe_w_router, moe_b_router, moe_w_gate, moe_w_up,
           moe_w_down, sh_w_gate, sh_w_up, sh_w_down, final_norm_g):
    x_parts = (x_prompt.reshape(N_CTX, D_MODEL), x_sample.reshape(N_LAT, D_MODEL))
    cv8 = jnp.concatenate([c_ctx[None, :], c, jnp.zeros((8 - 1 - DEC_BATCH, D_MODEL), F32)], axis=0)
    mods = _ada(cv8, w_ada, b_ada).reshape(DEPTH, 8, 1, 6 * D_MODEL)

    lb = jnp.cumsum(jax.nn.softmax(hg_lb_logits.astype(F32), axis=1), axis=1)
    lb = lb - lb[:, :1]

    cos_m, sin_m = _rope_full_tables(MLA_ROPE, MLA_HEADS)
    cos_q, sin_q = _rope_full_tables(SWA_HD, SWA_HEADS)
    cos_k, sin_k = cos_q[:, :SWA_KW], sin_q[:, :SWA_KW]
    cache_k = cache_swa_k.reshape(DEC_BATCH, DEPTH, PAST_LEN, SWA_KW)
    cache_v = cache_swa_v.reshape(DEC_BATCH, DEPTH, PAST_LEN, SWA_KW)
    state_t = jnp.swapaxes(state_hgrn, -1, -2)

    ctx_blk_lat = N_CTX // DEC_SEQ
    new_ckv, new_kr, new_k, new_v, new_st = [], [], [], [], []
    for l in range(DEPTH):
        u_mla, u_hg, u_fn, u_swa = _in_proj(x_parts, mods[l], norm1_g[l][None], _pack_w_in(w_in[l]))

        qg, kvg = mla_q_norm_g[l][None], mla_kv_norm_g[l][None]
        wq, wkv = _pack_w_q_up(mla_w_q_up[l]), _pack_w_kv_up(mla_w_kv_up[l])
        o_mla_c, ckv_c = _mla_ctx(u_mla, qg, wq, kvg, wkv)
        o_mla_l = _mla_lat(u_mla, cache_mla_ckv, cache_mla_krope, l, cos_m, sin_m, qg, wq, kvg, wkv)

        lbf, lbb = lb[0, l][None], lb[1, l][None]
        ng4 = jnp.tile(hg_norm_g[l], HG_HEADS)[None]
        o_hg_c, st_c = _hgrn(u_hg, lbf, lbb, ng4, None, seq=SEQ, n_batch=BATCH, row_block0=0)
        o_hg_l = _hgrn(u_hg, lbf, lbb, ng4, state_t[:, l], seq=DEC_SEQ, n_batch=DEC_BATCH,
                       row_block0=ctx_blk_lat)

        fw = fn_w[l].astype(BF16)
        o_fn_c = _fourier(u_fn, fw, seq=SEQ, n_batch=BATCH, row_block0=0)
        o_fn_l = _fourier(u_fn, fw, seq=DEC_SEQ, n_batch=DEC_BATCH, row_block0=ctx_blk_lat)

        sink = swa_sink[l]
        o_swa_c = _swa_ctx(u_swa, sink)
        o_swa_l = _swa_lat(u_swa, cache_k, cache_v, l, sink, cos_q, sin_q, cos_k, sin_k)

        x1, h2, gate_t, rank_t, cnt = _out_proj(
            x_parts, ((o_mla_c, o_mla_l), (o_hg_c, o_hg_l), (o_fn_c, o_fn_l), (o_swa_c, o_swa_l)),
            mods[l], norm2_g[l][None], _pack_w_out(w_out[l]),
            moe_w_router[l], moe_b_router[l][:, None],
            sh_w_gate[l].astype(BF16), sh_w_up[l].astype(BF16), sh_w_down[l].astype(BF16))
        plan = _segment_plan(cnt)
        xs = _dispatch(h2, rank_t, plan)
        ys = _experts(xs, plan, moe_w_gate, moe_w_up, moe_w_down, l)
        if l < DEPTH - 1:
            x_parts = (_combine(x1, ys, gate_t, rank_t, mods[l], plan, None),)
        else:
            y_prompt, y_sample = _combine(x1, ys, gate_t, rank_t, mods[l], plan, final_norm_g[None])

        new_ckv.append(ckv_c.reshape(BATCH, SEQ, MLA_KV_LORA))
        new_kr.append(u_mla[:N_CTX, MLA_Q_LORA + MLA_KV_LORA:MLA_IN].reshape(BATCH, SEQ, MLA_ROPE))
        new_k.append(u_swa[:N_CTX, SWA_QW:SWA_QW + SWA_KW].reshape(BATCH, SEQ, SWA_KV_HEADS, SWA_HD))
        new_v.append(u_swa[:N_CTX, SWA_QW + SWA_KW:].reshape(BATCH, SEQ, SWA_KV_HEADS, SWA_HD))
        new_st.append(jnp.swapaxes(st_c, -1, -2))

    y_prompt = y_prompt.reshape(BATCH, SEQ, D_MODEL)
    y_sample = y_sample.reshape(DEC_BATCH, DEC_SEQ, D_MODEL)
    stack = lambda xs: jnp.stack(xs, axis=1)
    return (y_prompt, y_sample, stack(new_ckv), stack(new_kr), stack(new_k), stack(new_v), stack(new_st))
```

```python
import functools

import numpy as np
import jax
import jax.numpy as jnp
from jax import lax
from jax.experimental import pallas as pl
from jax.experimental.pallas import tpu as pltpu

F32 = jnp.float32
BF16 = jnp.bfloat16

D_MODEL = 1024
BATCH = 32
SEQ = 256
DEPTH = 2
DEC_BATCH = 2
DEC_SEQ = 1024
PAST_LEN = 256
GRID_W = 64
EPS = 1e-6
ROPE_BASE = 10000.0
NEG_INF = -1e30

MLA_HEADS = 4
MLA_NOPE = 64
MLA_ROPE = 32
MLA_V = 64
MLA_Q_LORA = 256
MLA_KV_LORA = 128
HG_HEADS = 4
HG_DK = 64
HG_DV = 64
HG_W = HG_HEADS * HG_DK
FN_GROUPS = 4
FN_WIDTH = 256
SWA_HEADS = 4
SWA_KV_HEADS = 2
SWA_HD = 64
WINDOW = 128
N_EXPERTS = 64
TOP_K = 6
D_EXPERT = 256
D_SHARED = 256
ROUTE_SCALE = 2.5

MLA_IN = MLA_Q_LORA + MLA_KV_LORA + MLA_ROPE
HG_IN = 3 * HG_HEADS * HG_DK + 2 * HG_HEADS * HG_DV
FN_IN = FN_WIDTH
SWA_IN = (SWA_HEADS + 2 * SWA_KV_HEADS) * SWA_HD

N_CTX = BATCH * SEQ
N_LAT = DEC_BATCH * DEC_SEQ
N_TOK = N_CTX + N_LAT

MLA_PACK = 512
U_COLS = MLA_PACK + HG_IN + FN_IN + SWA_IN

ROW_TILE = 256
CTX_TILES = N_CTX // ROW_TILE
LAT_TILES_PER_BATCH = DEC_SEQ // ROW_TILE
HG_CHUNK = 32
HG_BLOCK = 256
SWA_QBLK = 128
MLA_QBLK = 256
CTX_PER_STEP = 4
HG_CTX_PER_STEP = 2
OUT_ROWS = 2 * ROW_TILE
VMEM_LIMIT = 56 * 1024 * 1024


def _dot(a, b):
    return jnp.dot(a, b, preferred_element_type=F32)


def _dot_nt(a, b):
    return lax.dot_general(a, b, (((1,), (1,)), ((), ())), preferred_element_type=F32)


def _dot_tn(a, b):
    return lax.dot_general(a, b, (((0,), (0,)), ((), ())), preferred_element_type=F32)


def _split3(x):
    hi = x.astype(BF16)
    r1 = x - hi.astype(F32)
    mid = r1.astype(BF16)
    return hi, mid, (r1 - mid.astype(F32)).astype(BF16)


def _dot_exact_lhs(a, b):
    ab = a.astype(BF16)
    hi, mid, lo = _split3(b)
    return (_dot(ab, lo) + _dot(ab, mid)) + _dot(ab, hi)


def _dot_exact_rhs(a, b):
    bb = b.astype(BF16)
    hi, mid, lo = _split3(a)
    return (_dot(lo, bb) + _dot(mid, bb)) + _dot(hi, bb)


def _rms(x, g):
    return x * lax.rsqrt(jnp.mean(x * x, axis=-1, keepdims=True) + EPS) * g


def _silu(x):
    return x * jax.nn.sigmoid(x)


def _mod_row(i):
    return jnp.where(i < CTX_TILES, 0, 1 + (i - CTX_TILES) // LAT_TILES_PER_BATCH)


def _params(*sem):
    return pltpu.CompilerParams(dimension_semantics=sem, vmem_limit_bytes=VMEM_LIMIT)


ADA_COLS = 1536


def _ada_kernel(cv_ref, w_ref, b_ref, o_ref):
    a = _silu(cv_ref[...]).astype(BF16)
    o_ref[...] = _dot(a, w_ref[...].astype(BF16)) + b_ref[...]


def _ada(cv8, w_ada, b_ada):
    return pl.pallas_call(
        _ada_kernel,
        grid=(DEPTH, 6 * D_MODEL // ADA_COLS),
        in_specs=[
            pl.BlockSpec((8, D_MODEL), lambda l, j: (0, 0)),
            pl.BlockSpec((None, D_MODEL, ADA_COLS), lambda l, j: (l, 0, j)),
            pl.BlockSpec((None, 1, ADA_COLS), lambda l, j: (l, 0, j)),
        ],
        out_specs=pl.BlockSpec((None, 8, ADA_COLS), lambda l, j: (l, 0, j)),
        out_shape=jax.ShapeDtypeStruct((DEPTH, 8, 6 * D_MODEL), F32),
        compiler_params=_params("arbitrary", "arbitrary"),
        name="ada",
    )(cv8, w_ada, b_ada.reshape(DEPTH, 1, 6 * D_MODEL))


def _ctx_lat_specs(width, tile=ROW_TILE):
    n_ctx = N_CTX // tile
    return [pl.BlockSpec((tile, width), lambda i, *_: (jnp.minimum(i, n_ctx - 1), 0)),
            pl.BlockSpec((tile, width), lambda i, *_: (jnp.maximum(i - n_ctx, 0), 0))]


def _row_specs(parts, width, tile=ROW_TILE):
    if len(parts) == 2:
        return _ctx_lat_specs(width, tile)
    return [pl.BlockSpec((tile, width), lambda i, *_: (i, 0))]


def _read_rows(refs, tile=ROW_TILE):
    if len(refs) == 1:
        return refs[0][...]
    return jnp.where(pl.program_id(0) < N_CTX // tile, refs[0][...], refs[1][...])


def _in_kernel(*refs, n_x):
    x = _read_rows(refs[:n_x], OUT_ROWS)
    mod_ref, g_ref, w_ref, umla_ref, uhg_ref, ufn_ref, uswa_ref = refs[n_x:]
    sh1 = mod_ref[:, 0:D_MODEL]
    sc1 = mod_ref[:, D_MODEL:2 * D_MODEL]
    h = _rms(x, g_ref[...]) * (1.0 + sc1) + sh1
    u = _dot(h.astype(BF16), w_ref[...])
    o = 0
    for ref, width in ((umla_ref, MLA_PACK), (uhg_ref, HG_IN), (ufn_ref, FN_IN), (uswa_ref, SWA_IN)):
        ref[...] = u[:, o:o + width].astype(ref.dtype)
        o += width


def _in_proj(x_parts, mod, g, w, l):
    row = lambda i: (i, 0)
    widths = (MLA_PACK, HG_IN, FN_IN, SWA_IN)
    return pl.pallas_call(
        functools.partial(_in_kernel, n_x=len(x_parts)),
        grid=(N_TOK // OUT_ROWS,),
        in_specs=_row_specs(x_parts, D_MODEL, OUT_ROWS) + [
            pl.BlockSpec((None, 1, 6 * D_MODEL), lambda i: (_mod_row(i * (OUT_ROWS // ROW_TILE)), 0, 0)),
            pl.BlockSpec((1, D_MODEL), lambda i: (0, 0)),
            pl.BlockSpec((None, D_MODEL, U_COLS), lambda i: (l, 0, 0))],
        out_specs=[pl.BlockSpec((OUT_ROWS, wd), row) for wd in widths],
        out_shape=[jax.ShapeDtypeStruct((N_TOK, wd), F32) for wd in widths],
        compiler_params=_params("arbitrary"),
        name="in_proj",
    )(*x_parts, mod, g, w)


def _rope(x, cos, sin_signed):
    lane = lax.broadcasted_iota(jnp.int32, x.shape, 1)
    width = x.shape[1]
    swapped = jnp.where(lane % 2 == 0, pltpu.roll(x, width - 1, 1), pltpu.roll(x, 1, 1))
    return x * cos + swapped * sin_signed


def _stack_heads(x, n_heads, head_w):
    lane = lax.broadcasted_iota(jnp.int32, x.shape, 1)
    return jnp.concatenate([jnp.where(lane // head_w == h, x, 0.0) for h in range(n_heads)], axis=0)


def _unstack_heads(o, n_heads, head_w):
    t = o.shape[0] // n_heads
    lane = lax.broadcasted_iota(jnp.int32, (t, o.shape[1]), 1)
    out = jnp.zeros((t, o.shape[1]), F32)
    for h in range(n_heads):
        out = jnp.where(lane // head_w == h, o[h * t:(h + 1) * t], out)
    return out


MLA_SCALE = (MLA_NOPE + MLA_ROPE) ** -0.5
MLA_QW = MLA_HEADS * MLA_NOPE + MLA_HEADS * MLA_ROPE
MLA_NW = MLA_HEADS * MLA_NOPE


def _mla_attend(q, kcat, v):
    qs = jnp.concatenate([_stack_heads(q[:, :MLA_NW], MLA_HEADS, MLA_NOPE),
                          _stack_heads(q[:, MLA_NW:], MLA_HEADS, MLA_ROPE)], axis=1)
    s = _dot_nt(qs.astype(BF16), kcat) * MLA_SCALE
    p = jnp.exp(s - jnp.max(s, axis=-1, keepdims=True))
    o = _dot(p.astype(BF16), v) / jnp.sum(p, axis=-1, keepdims=True)
    return _unstack_heads(o, MLA_HEADS, MLA_V)


def _mla_ctx_kernel(u_ref, qg_ref, wq_ref, kvg_ref, wkv_ref, o_ref, ckv_ref):
    for j in range(CTX_PER_STEP):
        rows = slice(j * SEQ, (j + 1) * SEQ)
        u = u_ref[rows, :]
        q = _dot(_rms(u[:, :MLA_Q_LORA], qg_ref[...]).astype(BF16), wq_ref[...])
        ckv = _rms(u[:, MLA_Q_LORA:MLA_Q_LORA + MLA_KV_LORA], kvg_ref[...])
        ckv_ref[rows, :] = ckv
        kv = _dot(ckv.astype(BF16), wkv_ref[...])
        kr4 = u[:, MLA_Q_LORA + MLA_KV_LORA:]
        kcat = jnp.concatenate([kv[:, :MLA_NW], kr4], axis=1).astype(BF16)
        o_ref[rows, :] = _mla_attend(q, kcat, kv[:, MLA_NW:].astype(BF16)).astype(o_ref.dtype)


def _mla_ctx(u_mla, qg, wq, kvg, wkv):
    full = lambda shape: pl.BlockSpec(shape, lambda b: (0, 0))
    rows = CTX_PER_STEP * SEQ
    return pl.pallas_call(
        _mla_ctx_kernel,
        grid=(BATCH // CTX_PER_STEP,),
        in_specs=[pl.BlockSpec((rows, MLA_PACK), lambda b: (b, 0)),
                  full((1, MLA_Q_LORA)), full((MLA_Q_LORA, MLA_QW)),
                  full((1, MLA_KV_LORA)), full((MLA_KV_LORA, 2 * MLA_NW))],
        out_specs=[pl.BlockSpec((rows, MLA_NW), lambda b: (b, 0)),
                   pl.BlockSpec((rows, MLA_KV_LORA), lambda b: (b, 0))],
        out_shape=[jax.ShapeDtypeStruct((N_CTX, MLA_NW), BF16),
                   jax.ShapeDtypeStruct((N_CTX, MLA_KV_LORA), F32)],
        compiler_params=_params("arbitrary"),
        name="mla_ctx",
    )(u_mla, qg, wq, kvg, wkv)


MLA_TK = PAST_LEN + DEC_SEQ


def _mla_lat_kernel(u_ref, cckv_ref, ckr_ref, cos_ref, sin_ref, qg_ref, wq_ref, kvg_ref, wkv_ref,
                    o_ref, kcat_s, v_s):
    i = pl.program_id(1)

    @pl.when(i == 0)
    def _():
        u = u_ref[...]
        ckv_new = _rms(u[:, MLA_Q_LORA:MLA_Q_LORA + MLA_KV_LORA], kvg_ref[...])
        ckv_all = jnp.concatenate([cckv_ref[...], ckv_new], axis=0)
        kv = _dot(ckv_all.astype(BF16), wkv_ref[...])
        kr_new = _rope(u[:, MLA_Q_LORA + MLA_KV_LORA:], cos_ref[...], sin_ref[...])
        ckr = ckr_ref[...]
        kr_all = jnp.concatenate([jnp.concatenate([ckr] * MLA_HEADS, axis=1), kr_new], axis=0)
        kcat_s[...] = jnp.concatenate([kv[:, :MLA_NW], kr_all], axis=1).astype(BF16)
        v_s[...] = kv[:, MLA_NW:].astype(BF16)

    r0 = pl.multiple_of(i * MLA_QBLK, MLA_QBLK)
    cq = u_ref[pl.ds(r0, MLA_QBLK), 0:MLA_Q_LORA]
    q = _dot(_rms(cq, qg_ref[...]).astype(BF16), wq_ref[...])
    qr = _rope(q[:, MLA_NW:], cos_ref[pl.ds(r0, MLA_QBLK), :], sin_ref[pl.ds(r0, MLA_QBLK), :])
    q = jnp.concatenate([q[:, :MLA_NW], qr], axis=1)
    o_ref[...] = _mla_attend(q, kcat_s[...], v_s[...]).astype(o_ref.dtype)


def _mla_lat(u_mla, cache_ckv, cache_kr, l, cos, sin, qg, wq, kvg, wkv):
    full = lambda shape: pl.BlockSpec(shape, lambda b, i: (0, 0))
    nq = DEC_SEQ // MLA_QBLK
    return pl.pallas_call(
        _mla_lat_kernel,
        grid=(DEC_BATCH, nq),
        in_specs=[pl.BlockSpec((DEC_SEQ, MLA_PACK), lambda b, i: (N_CTX // DEC_SEQ + b, 0)),
                  pl.BlockSpec((None, None, PAST_LEN, MLA_KV_LORA), lambda b, i: (b, l, 0, 0)),
                  pl.BlockSpec((None, None, PAST_LEN, MLA_ROPE), lambda b, i: (b, l, 0, 0)),
                  full((DEC_SEQ, MLA_HEADS * MLA_ROPE)), full((DEC_SEQ, MLA_HEADS * MLA_ROPE)),
                  full((1, MLA_Q_LORA)), full((MLA_Q_LORA, MLA_QW)),
                  full((1, MLA_KV_LORA)), full((MLA_KV_LORA, 2 * MLA_NW))],
        out_specs=pl.BlockSpec((MLA_QBLK, MLA_NW), lambda b, i: (b * nq + i, 0)),
        out_shape=jax.ShapeDtypeStruct((N_LAT, MLA_NW), BF16),
        scratch_shapes=[pltpu.VMEM((MLA_TK, MLA_QW), BF16), pltpu.VMEM((MLA_TK, MLA_NW), BF16)],
        compiler_params=_params("arbitrary", "arbitrary"),
        name="mla_lat",
    )(u_mla, cache_ckv, cache_kr, cos, sin, qg, wq, kvg, wkv)


def _hgrn_kernel(*refs, seq, n_seq, has_state):
    if has_state:
        (u_ref, lbf_ref, lbb_ref, ng_ref, s0_ref, o_ref,
         q_s, kf_s, gf_s, kb_s, gb_s, of_s, ob_s, stf_s, stb_s) = refs
    else:
        (u_ref, lbf_ref, lbb_ref, ng_ref, o_ref, so_ref,
         q_s, kf_s, gf_s, kb_s, gb_s, of_s, ob_s, stf_s, stb_s) = refs
    C = HG_CHUNK
    W = HG_W

    q_s[...] = _silu(u_ref[:, 0:W])
    ff = lbf_ref[...] + (1.0 - lbf_ref[...]) * jax.nn.sigmoid(u_ref[:, W:2 * W])
    kf_s[...] = 1.0 - ff
    gf_s[...] = jnp.log(ff)
    fb = lbb_ref[...] + (1.0 - lbb_ref[...]) * jax.nn.sigmoid(u_ref[:, 2 * W:3 * W])
    kb_s[...] = 1.0 - fb
    gb_s[...] = jnp.log(fb)

    rr = lax.broadcasted_iota(jnp.int32, (W, W), 0)
    cc = lax.broadcasted_iota(jnp.int32, (W, W), 1)
    blockdiag = rr // HG_DK == cc // HG_DK
    if has_state:
        for st, d in ((stf_s, 0), (stb_s, 1)):
            rows = []
            for h in range(HG_HEADS):
                z = lambda n: jnp.zeros((HG_DV, n * HG_DK), F32)
                parts = ([z(h)] if h else []) + [s0_ref[d, h]] + ([z(HG_HEADS - 1 - h)] if h < HG_HEADS - 1 else [])
                rows.append(jnp.concatenate(parts, axis=1) if len(parts) > 1 else parts[0])
            st[0] = jnp.concatenate(rows, axis=0)
    else:
        stf_s[...] = jnp.zeros_like(stf_s)
        stb_s[...] = jnp.zeros_like(stb_s)

    B = HG_BLOCK
    per_block = B // C
    n_blocks = seq // B
    ri = lax.broadcasted_iota(jnp.int32, (B, B), 0)
    ci = lax.broadcasted_iota(jnp.int32, (B, B), 1)
    same_chunk = ri // C == ci // C
    rs = lax.broadcasted_iota(jnp.int32, (HG_HEADS * B, B), 0) % B
    cs = lax.broadcasted_iota(jnp.int32, (HG_HEADS * B, B), 1)
    same_chunk_s = rs // C == cs // C

    sums_f = jnp.where(same_chunk & (ci <= ri), 1.0, 0.0)
    sums_b = jnp.where(same_chunk & (ci >= ri), 1.0, 0.0)
    keep_f = same_chunk_s & (rs >= cs)
    keep_b = same_chunk_s & (cs >= rs)

    def chunk_row(x, i):
        x3 = x.reshape(per_block, C, W)
        return jnp.broadcast_to(x3[:, i:i + 1, :], (per_block, C, W)).reshape(B, W)

    def block(r, k_s, g_s, o_s, st_s, sum_mat, keep, order, mid, far):
        q = q_s[pl.ds(r, B), :]
        k = k_s[pl.ds(r, B), :]
        v = u_ref[pl.ds(r, B), 3 * W:4 * W].astype(BF16)
        G = _dot_exact_lhs(sum_mat, g_s[pl.ds(r, B), :])
        Gq = G - chunk_row(G, mid)
        Gk2 = chunk_row(G, far) - G
        qe = _stack_heads(q * jnp.exp(Gq), HG_HEADS, HG_DK)
        ke = k * jnp.exp(-Gq)
        A = jnp.where(keep, _dot_nt(qe.astype(BF16), ke.astype(BF16)), 0.0)
        o_intra = _unstack_heads(_dot(A.astype(BF16), v), HG_HEADS, HG_DV)
        qg = (q * jnp.exp(G)).astype(BF16)
        k2 = (k * jnp.exp(Gk2)).astype(BF16)
        decay = jnp.exp(G + Gk2)
        st = st_s[...]
        o_inter = [None] * per_block
        for c in order:
            rows = slice(c * C, (c + 1) * C)
            o_inter[c] = _dot_nt(st.astype(BF16), qg[rows])
            st = st * decay[c * C:c * C + 1] + jnp.where(blockdiag, _dot_tn(v[rows], k2[rows]), 0.0)
        st_s[...] = st
        o_s[pl.ds(r, B), :] = o_intra + jnp.concatenate(o_inter, axis=1).T

    def fwd(j, r):
        block(r, kf_s, gf_s, of_s, stf_s.at[j], sums_f, keep_f, range(per_block), C // 2 - 1, C - 1)

    def bwd(j, r):
        block(r, kb_s, gb_s, ob_s, stb_s.at[j], sums_b, keep_b, range(per_block - 1, -1, -1), C // 2, 0)

    for j in range(n_seq):
        if n_blocks == 1:
            fwd(j, j * seq)
            bwd(j, j * seq)
        else:
            lax.fori_loop(0, n_blocks, lambda i, c, j=j: (fwd(j, pl.multiple_of(j * seq + i * B, B)), c)[1], 0)
            lax.fori_loop(0, n_blocks,
                          lambda i, c, j=j: (bwd(j, pl.multiple_of(j * seq + (n_blocks - 1 - i) * B, B)), c)[1], 0)

    o = of_s[...] + ob_s[...]
    ms = _dot_exact_rhs(o * o, jnp.where(blockdiag, 1.0 / HG_DV, 0.0))
    on = o * lax.rsqrt(ms + EPS) * ng_ref[...]
    o_ref[...] = (on * _silu(u_ref[:, 4 * W:5 * W])).astype(o_ref.dtype)

    if not has_state:
        for j in range(n_seq):
            for st, d in ((stf_s, 0), (stb_s, 1)):
                for h in range(HG_HEADS):
                    so_ref[j, d, h] = st[j, h * HG_DV:(h + 1) * HG_DV, h * HG_DK:(h + 1) * HG_DK]


def _hgrn(u_hg, lbf, lbb, ng4, state_t, *, seq, n_batch, row_block0):
    has_state = state_t is not None
    n_seq = 1 if has_state else HG_CTX_PER_STEP
    rows = n_seq * seq
    full = lambda shape: pl.BlockSpec(shape, lambda b: (0, 0))
    in_specs = [pl.BlockSpec((rows, HG_IN), lambda b: (row_block0 + b, 0)),
                full((1, HG_W)), full((1, HG_W)), full((1, HG_W))]
    args = [u_hg, lbf, lbb, ng4]
    o_spec = pl.BlockSpec((rows, HG_W), lambda b: (b, 0))
    o_shape = jax.ShapeDtypeStruct((n_batch * seq, HG_W), BF16)
    if has_state:
        in_specs.append(pl.BlockSpec((None, 2, HG_HEADS, HG_DV, HG_DK), lambda b: (b, 0, 0, 0, 0)))
        args.append(state_t)
        out_specs, out_shape = o_spec, o_shape
    else:
        out_specs = [o_spec, pl.BlockSpec((n_seq, 2, HG_HEADS, HG_DV, HG_DK), lambda b: (b, 0, 0, 0, 0))]
        out_shape = [o_shape, jax.ShapeDtypeStruct((n_batch, 2, HG_HEADS, HG_DV, HG_DK), F32)]
    return pl.pallas_call(
        functools.partial(_hgrn_kernel, seq=seq, n_seq=n_seq, has_state=has_state),
        grid=(n_batch // n_seq,),
        in_specs=in_specs, out_specs=out_specs, out_shape=out_shape,
        scratch_shapes=[pltpu.VMEM((rows, HG_W), F32)] * 7 + [pltpu.VMEM((n_seq, HG_W, HG_W), F32)] * 2,
        compiler_params=_params("arbitrary"),
        name="hgrn_lat" if has_state else "hgrn_ctx",
    )(*args)


def _dft_tables(n):
    j = np.arange(n, dtype=np.int64)
    ang = 2.0 * np.pi * ((j[:, None] * j[None, :]) % n).astype(np.float64) / n
    return np.cos(ang) / np.sqrt(n), np.sin(ang) / np.sqrt(n)


def _fourier_tables(seq):
    gw = FN_WIDTH // FN_GROUPS
    cg, sg = _dft_tables(gw)
    eye = np.eye(FN_GROUPS)
    chan = np.concatenate([np.kron(eye, cg), np.kron(eye, sg)], axis=1)
    ct, st = _dft_tables(seq)
    pos = np.concatenate([ct, -st], axis=1)
    return jnp.asarray(chan, F32).astype(BF16), jnp.asarray(pos, F32).astype(BF16)


def _fourier_kernel(x_ref, chan_ref, pos_ref, w_ref, o_ref, *, seq, n_seq):
    for j in range(n_seq):
        rows = slice(j * seq, (j + 1) * seq)
        x12 = _dot(x_ref[rows, :].astype(BF16), chan_ref[...])
        z = jnp.concatenate([x12[:, :FN_WIDTH], x12[:, FN_WIDTH:]], axis=0).astype(BF16)
        y = _dot(pos_ref[...], z)
        o_ref[rows, :] = _dot(y.astype(BF16), w_ref[...]).astype(o_ref.dtype)


def _fourier(u_fn, w, *, seq, n_batch, row_block0):
    chan, pos = _fourier_tables(seq)
    full = lambda shape: pl.BlockSpec(shape, lambda b: (0, 0))
    n_seq = CTX_PER_STEP if seq == SEQ else 1
    return pl.pallas_call(
        functools.partial(_fourier_kernel, seq=seq, n_seq=n_seq),
        grid=(n_batch // n_seq,),
        in_specs=[pl.BlockSpec((n_seq * seq, FN_WIDTH), lambda b: (row_block0 + b, 0)),
                  full((FN_WIDTH, 2 * FN_WIDTH)), full((seq, 2 * seq)), full((FN_WIDTH, FN_WIDTH))],
        out_specs=pl.BlockSpec((n_seq * seq, FN_WIDTH), lambda b: (b, 0)),
        out_shape=jax.ShapeDtypeStruct((n_batch * seq, FN_WIDTH), BF16),
        compiler_params=_params("arbitrary"),
        name="fourier",
    )(u_fn, chan, pos, w)


SWA_SCALE = SWA_HD ** -0.5
SWA_QW = SWA_HEADS * SWA_HD
SWA_KW = SWA_KV_HEADS * SWA_HD
SWA_STACK_ORDER = (0, 2, 1, 3)


def _swa_stack_q(q):
    return jnp.concatenate([_stack_heads(q[:, :SWA_KW], SWA_KV_HEADS, SWA_HD),
                            _stack_heads(q[:, SWA_KW:], SWA_KV_HEADS, SWA_HD)], axis=0)


def _swa_unstack_o(o):
    t = o.shape[0] // SWA_HEADS
    return jnp.concatenate([_unstack_heads(o[:2 * t], SWA_KV_HEADS, SWA_HD),
                            _unstack_heads(o[2 * t:], SWA_KV_HEADS, SWA_HD)], axis=1)


def _sink_rows(sink_ref, t):
    return jnp.concatenate([jnp.full((t, 1), sink_ref[h], F32) for h in SWA_STACK_ORDER], axis=0)


def _swa_ctx_kernel(sink_ref, u_ref, o_ref):
    sink = _sink_rows(sink_ref, SEQ)
    for j in range(CTX_PER_STEP):
        rows = slice(j * SEQ, (j + 1) * SEQ)
        u = u_ref[rows, :]
        qs = _swa_stack_q(u[:, :SWA_QW]).astype(BF16)
        k = u[:, SWA_QW:SWA_QW + SWA_KW].astype(BF16)
        v = u[:, SWA_QW + SWA_KW:].astype(BF16)
        s = _dot_nt(qs, k) * SWA_SCALE
        m = jnp.maximum(jnp.max(s, axis=-1, keepdims=True), sink)
        p = jnp.exp(s - m)
        denom = jnp.sum(p, axis=-1, keepdims=True) + jnp.exp(sink - m)
        o_ref[rows, :] = _swa_unstack_o(_dot(p.astype(BF16), v) / denom).astype(o_ref.dtype)


def _swa_ctx(u_swa, sink):
    rows = CTX_PER_STEP * SEQ
    return pl.pallas_call(
        _swa_ctx_kernel,
        grid=(BATCH // CTX_PER_STEP,),
        in_specs=[pl.BlockSpec(memory_space=pltpu.SMEM),
                  pl.BlockSpec((rows, SWA_IN), lambda b: (b, 0))],
        out_specs=pl.BlockSpec((rows, SWA_QW), lambda b: (b, 0)),
        out_shape=jax.ShapeDtypeStruct((N_CTX, SWA_QW), BF16),
        compiler_params=_params("arbitrary"),
        name="swa_ctx",
    )(sink, u_swa)


SWA_PAD = DEC_SEQ + 2 * SWA_QBLK


def _swa_lat_kernel(sink_ref, u_ref, kc_ref, vc_ref, cosq_ref, sinq_ref, cosk_ref, sin_k_ref,
                    o_ref, k_s, v_s):
    i = pl.program_id(1)
    B = SWA_QBLK

    @pl.when(i == 0)
    def _():
        zeros = jnp.zeros((B, SWA_KW), BF16)
        k = _rope(u_ref[:, SWA_QW:SWA_QW + SWA_KW], cosk_ref[...], sin_k_ref[...]).astype(BF16)
        k_s[...] = jnp.concatenate([zeros, k, zeros], axis=0)
        v_s[...] = jnp.concatenate([zeros, u_ref[:, SWA_QW + SWA_KW:].astype(BF16), zeros], axis=0)

    r0 = pl.multiple_of(i * B, B)
    q = _rope(u_ref[pl.ds(r0, B), 0:SWA_QW], cosq_ref[pl.ds(r0, B), :], sinq_ref[pl.ds(r0, B), :])
    qs = _swa_stack_q(q).astype(BF16)
    s_loc = _dot_nt(qs, k_s[pl.ds(r0, 3 * B), :]) * SWA_SCALE
    row = lax.broadcasted_iota(jnp.int32, s_loc.shape, 0) % B
    col = lax.broadcasted_iota(jnp.int32, s_loc.shape, 1)
    kpos = r0 - B + col
    valid = (jnp.abs(row + B - col) <= WINDOW) & (kpos >= 0) & (kpos < DEC_SEQ)
    s_loc = jnp.where(valid, s_loc, NEG_INF)
    s_ctx = _dot_nt(qs, kc_ref[...].astype(BF16)) * SWA_SCALE
    sink = _sink_rows(sink_ref, B)
    m = jnp.maximum(jnp.maximum(jnp.max(s_loc, axis=-1, keepdims=True),
                                jnp.max(s_ctx, axis=-1, keepdims=True)), sink)
    p_loc = jnp.exp(s_loc - m)
    p_ctx = jnp.exp(s_ctx - m)
    denom = (jnp.sum(p_loc, axis=-1, keepdims=True) + jnp.sum(p_ctx, axis=-1, keepdims=True)
             + jnp.exp(sink - m))
    o = _dot(p_loc.astype(BF16), v_s[pl.ds(r0, 3 * B), :]) + _dot(p_ctx.astype(BF16), vc_ref[...].astype(BF16))
    o_ref[...] = _swa_unstack_o(o / denom).astype(o_ref.dtype)


def _swa_lat(u_swa, cache_k, cache_v, l, sink, cosq, sinq, cosk, sink_k):
    full = lambda shape: pl.BlockSpec(shape, lambda b, i: (0, 0))
    nq = DEC_SEQ // SWA_QBLK
    cache_spec = pl.BlockSpec((None, None, PAST_LEN, SWA_KW), lambda b, i: (b, l, 0, 0))
    return pl.pallas_call(
        _swa_lat_kernel,
        grid=(DEC_BATCH, nq),
        in_specs=[pl.BlockSpec(memory_space=pltpu.SMEM),
                  pl.BlockSpec((DEC_SEQ, SWA_IN), lambda b, i: (N_CTX // DEC_SEQ + b, 0)),
                  cache_spec, cache_spec,
                  full((DEC_SEQ, SWA_QW)), full((DEC_SEQ, SWA_QW)),
                  full((DEC_SEQ, SWA_KW)), full((DEC_SEQ, SWA_KW))],
        out_specs=pl.BlockSpec((SWA_QBLK, SWA_QW), lambda b, i: (b * nq + i, 0)),
        out_shape=jax.ShapeDtypeStruct((N_LAT, SWA_QW), BF16),
        scratch_shapes=[pltpu.VMEM((SWA_PAD, SWA_KW), BF16), pltpu.VMEM((SWA_PAD, SWA_KW), BF16)],
        compiler_params=_params("arbitrary", "arbitrary"),
        name="swa_lat",
    )(sink, u_swa, cache_k, cache_v, cosq, sinq, cosk, sink_k)


N_BLK = N_TOK // ROW_TILE
SEG_ALIGN = 16
LOCAL_ROWS = ROW_TILE * TOP_K + N_EXPERTS * SEG_ALIGN
LOCAL_CHUNK = 512
EXPERT_TILE = 768
SORTED_ROWS = -(-(N_TOK * TOP_K + N_BLK * N_EXPERTS * SEG_ALIGN + N_EXPERTS * (EXPERT_TILE + SEG_ALIGN))
                // EXPERT_TILE) * EXPERT_TILE
MAX_TILES = SORTED_ROWS // EXPERT_TILE
NOT_PICKED = -1.0
NO_DEST = 4095.0


def _out_kernel(*refs, n_x):
    x_all = _read_rows(refs[:n_x], OUT_ROWS)
    mix_all = [_read_rows(refs[n_x + 2 * j:n_x + 2 * j + 2], OUT_ROWS) for j in range(4)]
    out_refs = refs[n_x + 8:]
    for blk in range(OUT_ROWS // ROW_TILE):
        rows = slice(blk * ROW_TILE, (blk + 1) * ROW_TILE)
        _out_block(x_all[rows], [m[rows] for m in mix_all], out_refs, blk)


def _out_block(x, mixers, refs, blk):
    (mod_ref, g_ref, wo_ref, wr_ref, br_ref, wsg_ref, wsu_ref, wsd_ref,
     x1_ref, h_ref, gate_ref, rank_ref, cnt_ref) = refs
    rows = slice(blk * ROW_TILE, (blk + 1) * ROW_TILE)
    mix = jnp.zeros((ROW_TILE, D_MODEL), F32)
    for j in range(4):
        mix = mix + _dot(mixers[j], wo_ref[j * 256:(j + 1) * 256, :])
    g1 = mod_ref[:, 2 * D_MODEL:3 * D_MODEL]
    sh2 = mod_ref[:, 3 * D_MODEL:4 * D_MODEL]
    sc2 = mod_ref[:, 4 * D_MODEL:5 * D_MODEL]
    g2 = mod_ref[:, 5 * D_MODEL:6 * D_MODEL]
    x1 = x + g1 * mix
    h = _rms(x1, g_ref[...]) * (1.0 + sc2) + sh2
    hb = h.astype(BF16)
    h_ref[rows, :] = hb

    w_hi = wr_ref[...].astype(BF16)
    w_lo = (wr_ref[...] - w_hi.astype(F32)).astype(BF16)
    h_lo = (h - hb.astype(F32)).astype(BF16)
    logits = ((_dot(h_lo, w_hi) + _dot(hb, w_lo)) + _dot(hb, w_hi)).T
    scores = jax.nn.sigmoid(logits)
    sel = scores + br_ref[...]
    eidx = lax.broadcasted_iota(jnp.int32, sel.shape, 0)
    gate = jnp.zeros_like(scores)
    picked = jnp.zeros_like(scores)
    for _ in range(TOP_K):
        best = jnp.max(sel, axis=0, keepdims=True)
        first = jnp.min(jnp.where(sel == best, eidx, N_EXPERTS), axis=0, keepdims=True)
        pick = eidx == first
        gate = jnp.where(pick, scores, gate)
        picked = jnp.where(pick, 1.0, picked)
        sel = jnp.where(pick, -jnp.inf, sel)
    gate = ROUTE_SCALE * gate / jnp.sum(gate, axis=0, keepdims=True)

    ti = lax.broadcasted_iota(jnp.int32, (ROW_TILE, ROW_TILE), 0)
    tj = lax.broadcasted_iota(jnp.int32, (ROW_TILE, ROW_TILE), 1)
    pb = picked.astype(BF16)
    rank = _dot(pb, jnp.where(ti < tj, 1.0, 0.0).astype(BF16))
    gate_ref[:, rows] = gate
    rank_ref[:, rows] = jnp.where(picked > 0.0, rank, NOT_PICKED)
    counts = _dot_nt(jnp.ones((8, ROW_TILE), BF16), pb)
    cnt_ref[blk] = jnp.concatenate([counts, jnp.zeros_like(counts)], axis=1)

    hid = _silu(_dot(hb, wsg_ref[...])) * _dot(hb, wsu_ref[...])
    x1_ref[rows, :] = x1 + g2 * _dot(hid.astype(BF16), wsd_ref[...])


def _out_proj(x_parts, mixer_pairs, mod, g, wo, l, wr, br, wsg, wsu, wsd):
    row = lambda i: (i, 0)
    col = lambda i: (0, i)
    full = lambda shape: pl.BlockSpec(shape, lambda i: (0, 0))
    per_step = OUT_ROWS // ROW_TILE
    et_spec = pl.BlockSpec((N_EXPERTS, OUT_ROWS), col)
    et_shape = jax.ShapeDtypeStruct((N_EXPERTS, N_TOK), F32)
    return pl.pallas_call(
        functools.partial(_out_kernel, n_x=len(x_parts)),
        grid=(N_TOK // OUT_ROWS,),
        in_specs=_row_specs(x_parts, D_MODEL, OUT_ROWS) + 4 * _ctx_lat_specs(256, OUT_ROWS) + [
            pl.BlockSpec((None, 1, 6 * D_MODEL), lambda i: (_mod_row(i * per_step), 0, 0)),
            full((1, D_MODEL)), pl.BlockSpec((None, D_MODEL, D_MODEL), lambda i: (l, 0, 0)),
            full((D_MODEL, N_EXPERTS)), full((N_EXPERTS, 1)),
            full((D_MODEL, D_SHARED)), full((D_MODEL, D_SHARED)), full((D_SHARED, D_MODEL))],
        out_specs=[pl.BlockSpec((OUT_ROWS, D_MODEL), row), pl.BlockSpec((OUT_ROWS, D_MODEL), row),
                   et_spec, et_spec,
                   pl.BlockSpec((per_step, 8, 128), lambda i: (i, 0, 0))],
        out_shape=[jax.ShapeDtypeStruct((N_TOK, D_MODEL), F32),
                   jax.ShapeDtypeStruct((N_TOK, D_MODEL), BF16),
                   et_shape, et_shape,
                   jax.ShapeDtypeStruct((N_BLK, 8, 128), F32)],
        compiler_params=_params("arbitrary"),
        name="out_proj",
    )(*x_parts, *[a for pair in mixer_pairs for a in pair], mod, g, wo, wr, br, wsg, wsu, wsd)


def _segment_plan(cnt):
    cnt = cnt[:, 0, :N_EXPERTS].astype(jnp.int32)
    seg = jnp.maximum((cnt + (SEG_ALIGN - 1)) // SEG_ALIGN, 1) * SEG_ALIGN
    local = jnp.cumsum(seg, axis=1) - seg
    total = jnp.sum(seg, axis=1)
    per_expert = jnp.sum(seg, axis=0)
    padded = (per_expert + SEG_ALIGN + (EXPERT_TILE - 1)) // EXPERT_TILE * EXPERT_TILE
    ends = jnp.cumsum(padded)
    start = ends - padded
    dest = start[None, :] + jnp.cumsum(seg, axis=0) - seg
    n_tiles = ends[-1] // EXPERT_TILE
    tiles = jnp.arange(MAX_TILES, dtype=jnp.int32)
    tile_expert = jnp.sum((ends // EXPERT_TILE)[None, :] <= jnp.minimum(tiles, n_tiles - 1)[:, None], axis=1)
    tile_expert = tile_expert.astype(jnp.int32)
    plan = dict(seg=seg, local=local, total=total.astype(jnp.int32), dest=dest.astype(jnp.int32),
                tail_start=(start + per_expert).astype(jnp.int32), tail_rows=(padded - per_expert).astype(jnp.int32),
                n_tiles=n_tiles.reshape(1).astype(jnp.int32), tile_expert=tile_expert)
    segf, localf = seg.astype(F32), local.astype(F32)
    pad_lanes = lambda a: jnp.concatenate([a, jnp.zeros_like(a)], axis=1)[:, None, :]
    plan.update(seg_row=pad_lanes(segf), local_row=pad_lanes(localf),
                seg_col=segf[:, :, None], local_col=localf[:, :, None])
    return plan


def _local_dest_digits(rank, local_start):
    dest = jnp.where(rank >= 0.0, local_start + rank, NO_DEST)
    hi = jnp.floor(dest * (1.0 / 64.0))
    return hi, dest - 64.0 * hi


def _dispatch_kernel(dest_ref, seg_ref, local_ref, total_ref, tail_start_ref, tail_rows_ref, nt_ref,
                     h_ref, rank_ref, local_col_ref, local_row_ref, seg_row_ref,
                     xs_hbm, buf, zeros, sems, zsem, usem):
    b = pl.program_id(0)
    slot = b % 2

    def wait_block(blk, s):
        n = pl.multiple_of(total_ref[blk], SEG_ALIGN)
        pltpu.make_async_copy(buf.at[s, pl.ds(0, n)], xs_hbm.at[pl.ds(0, n)], sems.at[s]).wait()

    @pl.when(b == 0)
    def _():
        zeros[...] = jnp.zeros_like(zeros)

    def unused_tiles(action):
        for k in range(-(-MAX_TILES // N_BLK)):
            t = nt_ref[0] + b + k * N_BLK

            @pl.when(t < MAX_TILES)
            def _():
                cp = pltpu.make_async_copy(zeros.at[pl.ds(0, EXPERT_TILE)],
                                           xs_hbm.at[pl.ds(pl.multiple_of(t * EXPERT_TILE, EXPERT_TILE),
                                                           EXPERT_TILE)], usem)
                cp.start() if action == "start" else cp.wait()

    unused_tiles("start")

    @pl.when(b >= 2)
    def _():
        wait_block(b - 2, slot)

    hi, lo = _local_dest_digits(rank_ref[...], local_col_ref[...])
    code = jnp.concatenate([hi, lo], axis=0).astype(BF16)
    hb = h_ref[...]
    lstart = local_row_ref[:, :N_EXPERTS]
    lend = lstart + seg_row_ref[:, :N_EXPERTS]
    r = lax.broadcasted_iota(jnp.int32, (LOCAL_CHUNK, N_EXPERTS), 0).astype(F32)
    rr = lax.broadcasted_iota(jnp.int32, (LOCAL_CHUNK, ROW_TILE), 0).astype(F32)
    def compact(c):
        first = float(c * LOCAL_CHUNK)
        member = (r >= lstart - first) & (r < lend - first)
        table = jnp.concatenate([jnp.where(member, 64.0, 0.0), jnp.where(member, 1.0, 0.0)], axis=1).astype(BF16)
        d = _dot(table, code)
        onehot = jnp.where(d == rr + first, 1.0, 0.0).astype(BF16)
        buf[slot, c * LOCAL_CHUNK:(c + 1) * LOCAL_CHUNK, :] = _dot(onehot, hb).astype(BF16)

    n_chunks = LOCAL_ROWS // LOCAL_CHUNK
    for c in range(n_chunks - 1):
        compact(c)
    pl.when(total_ref[b] > (n_chunks - 1) * LOCAL_CHUNK)(functools.partial(compact, n_chunks - 1))

    for e in range(N_EXPERTS):
        n = pl.multiple_of(seg_ref[b, e], SEG_ALIGN)
        src = pl.multiple_of(local_ref[b, e], SEG_ALIGN)
        dst = pl.multiple_of(dest_ref[b, e], SEG_ALIGN)
        pltpu.make_async_copy(buf.at[slot, pl.ds(src, n)], xs_hbm.at[pl.ds(dst, n)], sems.at[slot]).start()

    unused_tiles("wait")

    @pl.when(b == N_BLK - 1)
    def _():
        def tail(e):
            n = pl.multiple_of(tail_rows_ref[e], SEG_ALIGN)
            dst = pl.multiple_of(tail_start_ref[e], SEG_ALIGN)
            return pltpu.make_async_copy(zeros.at[pl.ds(0, n)], xs_hbm.at[pl.ds(dst, n)], zsem)

        for e in range(N_EXPERTS):
            tail(e).start()
        for e in range(N_EXPERTS):
            tail(e).wait()
        wait_block(b - 1, 1 - slot)
        wait_block(b, slot)


def _dispatch(h, rank_t, plan):
    blk = lambda shape, imap: pl.BlockSpec(shape, imap)
    return pl.pallas_call(
        _dispatch_kernel,
        grid_spec=pltpu.PrefetchScalarGridSpec(
            num_scalar_prefetch=7, grid=(N_BLK,),
            in_specs=[blk((ROW_TILE, D_MODEL), lambda b, *_: (b, 0)),
                      blk((N_EXPERTS, ROW_TILE), lambda b, *_: (0, b)),
                      blk((None, N_EXPERTS, 1), lambda b, *_: (b, 0, 0)),
                      blk((None, 1, 128), lambda b, *_: (b, 0, 0)),
                      blk((None, 1, 128), lambda b, *_: (b, 0, 0))],
            out_specs=pl.BlockSpec(memory_space=pl.ANY),
            scratch_shapes=[pltpu.VMEM((2, LOCAL_ROWS, D_MODEL), BF16),
                            pltpu.VMEM((EXPERT_TILE + SEG_ALIGN, D_MODEL), BF16),
                            pltpu.SemaphoreType.DMA((2,)), pltpu.SemaphoreType.DMA(()),
                            pltpu.SemaphoreType.DMA(())]),
        out_shape=jax.ShapeDtypeStruct((SORTED_ROWS, D_MODEL), BF16),
        compiler_params=_params("arbitrary"),
        name="dispatch",
    )(plan['dest'], plan['seg'], plan['local'], plan['total'], plan['tail_start'], plan['tail_rows'],
      plan['n_tiles'], h, rank_t, plan['local_col'], plan['local_row'], plan['seg_row'])


def _expert_kernel(te_ref, nt_ref, x_ref, wg_ref, wu_ref, wd_ref, y_ref, wg_s, wu_s, wd_s):
    i = pl.program_id(0)
    active = i < nt_ref[0]

    @pl.when((i == 0) | (te_ref[i] != te_ref[jnp.maximum(i - 1, 0)]))
    def _():
        wg_s[...] = wg_ref[...].astype(BF16)
        wu_s[...] = wu_ref[...].astype(BF16)
        wd_s[...] = wd_ref[...].astype(BF16)

    @pl.when(active)
    def _():
        x = x_ref[...]
        hid = _silu(_dot(x, wg_s[...])) * _dot(x, wu_s[...])
        y_ref[...] = _dot(hid.astype(BF16), wd_s[...]).astype(BF16)


def _experts(xs, plan, w_gate, w_up, w_down, l):
    rows = pl.BlockSpec((EXPERT_TILE, D_MODEL), lambda i, te, nt: (jnp.minimum(i, nt[0] - 1), 0))
    wspec = lambda shape: pl.BlockSpec((None, None) + shape, lambda i, te, nt: (l, te[i], 0, 0))
    return pl.pallas_call(
        _expert_kernel,
        grid_spec=pltpu.PrefetchScalarGridSpec(
            num_scalar_prefetch=2, grid=(MAX_TILES,),
            in_specs=[rows, wspec((D_MODEL, D_EXPERT)), wspec((D_MODEL, D_EXPERT)), wspec((D_EXPERT, D_MODEL))],
            out_specs=rows,
            scratch_shapes=[pltpu.VMEM((D_MODEL, D_EXPERT), BF16), pltpu.VMEM((D_MODEL, D_EXPERT), BF16),
                            pltpu.VMEM((D_EXPERT, D_MODEL), BF16)]),
        out_shape=jax.ShapeDtypeStruct((SORTED_ROWS, D_MODEL), BF16),
        input_output_aliases={2: 0},
        compiler_params=_params("arbitrary"),
        name="experts",
    )(plan['tile_expert'], plan['n_tiles'], xs, w_gate, w_up, w_down)


def _combine_kernel(*refs, final):
    refs = list(refs)
    dest_ref, seg_ref, local_ref, total_ref = refs[:4]
    x1_ref, ys_hbm, gate_ref, rank_ref, mod_ref, local_row_ref, local_col_ref, seg_col_ref = refs[4:12]
    rest = refs[12:]
    fg_ref = rest.pop(0) if final else None
    outs, (buf, sems) = rest[:-2], rest[-2:]
    b = pl.program_id(0)
    slot = b % 2

    def fetch(blk, s):
        for e in range(N_EXPERTS):
            n = pl.multiple_of(seg_ref[blk, e], SEG_ALIGN)
            src = pl.multiple_of(dest_ref[blk, e], SEG_ALIGN)
            dst = pl.multiple_of(local_ref[blk, e], SEG_ALIGN)
            pltpu.make_async_copy(ys_hbm.at[pl.ds(src, n)], buf.at[s, pl.ds(dst, n)], sems.at[s]).start()

    @pl.when(b == 0)
    def _():
        buf[...] = jnp.zeros_like(buf)
        fetch(0, 0)

    def wait_rows(blk, s):
        n_rows = pl.multiple_of(total_ref[blk], SEG_ALIGN)
        pltpu.make_async_copy(ys_hbm.at[pl.ds(0, n_rows)], buf.at[s, pl.ds(0, n_rows)], sems.at[s]).wait()

    wait_rows(b, slot)
    nxt = jnp.minimum(b + 1, N_BLK - 1)
    fetch(nxt, 1 - slot)

    gate = gate_ref[...].T
    hi, lo = _local_dest_digits(rank_ref[...].T, local_row_ref[:, :N_EXPERTS])
    lhs = jnp.concatenate([jnp.concatenate([hi, lo], axis=1),
                           jnp.concatenate([jnp.zeros_like(gate), gate], axis=1)], axis=0).astype(BF16)
    lstart = local_col_ref[...]
    lend = lstart + seg_col_ref[...]
    routed = jnp.zeros((ROW_TILE, D_MODEL), F32)
    r = lax.broadcasted_iota(jnp.int32, (N_EXPERTS, LOCAL_CHUNK), 1).astype(F32)
    rr = lax.broadcasted_iota(jnp.int32, (ROW_TILE, LOCAL_CHUNK), 1).astype(F32)
    def gather(c, acc):
        first = float(c * LOCAL_CHUNK)
        member = (r >= lstart - first) & (r < lend - first)
        table = jnp.concatenate([jnp.where(member, 64.0, 0.0), jnp.where(member, 1.0, 0.0)], axis=0).astype(BF16)
        dg = _dot(lhs, table)
        weights = jnp.where(dg[:ROW_TILE] == rr + first, dg[ROW_TILE:], 0.0).astype(BF16)
        return acc + _dot(weights, buf[slot, c * LOCAL_CHUNK:(c + 1) * LOCAL_CHUNK, :])

    for c in range(LOCAL_ROWS // LOCAL_CHUNK):
        routed = gather(c, routed)
    x = x1_ref[...] + mod_ref[:, 5 * D_MODEL:6 * D_MODEL] * routed
    if final:
        y = _rms(x, fg_ref[...])

        @pl.when(b < CTX_TILES)
        def _():
            outs[0][...] = y

        @pl.when(b >= CTX_TILES)
        def _():
            outs[1][...] = y
    else:
        outs[0][...] = x

    @pl.when(b == N_BLK - 1)
    def _():
        wait_rows(nxt, 1 - slot)


def _combine(x1, ys, gate_t, rank_t, mod, plan, final_g):
    final = final_g is not None
    blk = lambda shape, imap: pl.BlockSpec(shape, imap)
    xspec = blk((ROW_TILE, D_MODEL), lambda b, *_: (b, 0))
    if final:
        out_specs = _ctx_lat_specs(D_MODEL)
        out_shape = [jax.ShapeDtypeStruct((N_CTX, D_MODEL), F32), jax.ShapeDtypeStruct((N_LAT, D_MODEL), F32)]
    else:
        out_specs, out_shape = xspec, jax.ShapeDtypeStruct((N_TOK, D_MODEL), F32)
    et_spec = blk((N_EXPERTS, ROW_TILE), lambda b, *_: (0, b))
    in_specs = [xspec, pl.BlockSpec(memory_space=pl.ANY), et_spec, et_spec,
                blk((None, 1, 6 * D_MODEL), lambda b, *_: (_mod_row(b), 0, 0)),
                blk((None, 1, 128), lambda b, *_: (b, 0, 0)),
                blk((None, N_EXPERTS, 1), lambda b, *_: (b, 0, 0)),
                blk((None, N_EXPERTS, 1), lambda b, *_: (b, 0, 0))]
    args = [x1, ys, gate_t, rank_t, mod, plan['local_row'], plan['local_col'], plan['seg_col']]
    if final:
        in_specs.append(blk((1, D_MODEL), lambda b, *_: (0, 0)))
        args.append(final_g)
    return pl.pallas_call(
        functools.partial(_combine_kernel, final=final),
        grid_spec=pltpu.PrefetchScalarGridSpec(
            num_scalar_prefetch=4, grid=(N_BLK,),
            in_specs=in_specs, out_specs=out_specs,
            scratch_shapes=[pltpu.VMEM((2, LOCAL_ROWS, D_MODEL), BF16), pltpu.SemaphoreType.DMA((2,))]),
        out_shape=out_shape,
        compiler_params=_params("arbitrary"),
        name="combine",
    )(plan['dest'], plan['seg'], plan['local'], plan['total'], *args)


def _rope_full_tables(dim, n_rep):
    rows = DEC_SEQ // GRID_W
    r_idx, c_idx = np.meshgrid(np.arange(rows), np.arange(GRID_W), indexing='ij')
    pos = np.stack([r_idx.reshape(-1), c_idx.reshape(-1)], axis=-1).astype(np.float32)
    nf = dim // 4
    inv = np.float32(ROPE_BASE) ** (-np.arange(nf, dtype=np.float32) / np.float32(nf))
    ang = (pos[:, :, None] * inv).astype(np.float32)
    ang = np.repeat(ang.reshape(DEC_SEQ, 2 * nf), 2, axis=1)
    sign = np.tile(np.asarray([-1.0, 1.0], np.float32), dim // 2)
    cos = np.tile(np.cos(ang.astype(np.float64)), (1, n_rep))
    sin = np.tile(np.sin(ang.astype(np.float64)) * sign, (1, n_rep))
    return jnp.asarray(cos, F32), jnp.asarray(sin, F32)


def _pack_w_in(w):
    c0 = MLA_Q_LORA + MLA_KV_LORA
    kr = w[..., c0:MLA_IN]
    s0 = MLA_IN + HG_IN + FN_IN
    qh = [w[..., s0 + h * SWA_HD:s0 + (h + 1) * SWA_HD] for h in SWA_STACK_ORDER]
    return jnp.concatenate([w[..., :c0], kr, kr, kr, kr, w[..., MLA_IN:s0]] + qh
                           + [w[..., s0 + SWA_QW:]], axis=-1).astype(BF16)


def _pack_w_q_up(w):
    hd = MLA_NOPE + MLA_ROPE
    nope = [w[:, h * hd:h * hd + MLA_NOPE] for h in range(MLA_HEADS)]
    rope = [w[:, h * hd + MLA_NOPE:(h + 1) * hd] for h in range(MLA_HEADS)]
    return jnp.concatenate(nope + rope, axis=1).astype(BF16)


def _pack_w_kv_up(w):
    hd = MLA_NOPE + MLA_V
    kn = [w[:, h * hd:h * hd + MLA_NOPE] for h in range(MLA_HEADS)]
    vv = [w[:, h * hd + MLA_NOPE:(h + 1) * hd] for h in range(MLA_HEADS)]
    return jnp.concatenate(kn + vv, axis=1).astype(BF16)


def _pack_w_out(w):
    s0 = 3 * 256
    rows = [w[..., s0 + h * SWA_HD:s0 + (h + 1) * SWA_HD, :] for h in SWA_STACK_ORDER]
    return jnp.concatenate([w[..., :s0, :]] + rows, axis=-2).astype(BF16)


def kernel(x_prompt, x_sample, c, cache_mla_ckv, cache_mla_krope, cache_swa_k, cache_swa_v, state_hgrn,
           c_ctx, w_ada, b_ada, norm1_g, norm2_g, w_in, mla_q_norm_g, mla_w_q_up, mla_kv_norm_g, mla_w_kv_up,
           hg_lb_logits, hg_norm_g, fn_w, swa_sink, w_out, moe_w_router, moe_b_router, moe_w_gate, moe_w_up,
           moe_w_down, sh_w_gate, sh_w_up, sh_w_down, final_norm_g):
    x_parts = (x_prompt.reshape(N_CTX, D_MODEL), x_sample.reshape(N_LAT, D_MODEL))
    cv8 = jnp.concatenate([c_ctx[None, :], c, jnp.zeros((8 - 1 - DEC_BATCH, D_MODEL), F32)], axis=0)
    mods = _ada(cv8, w_ada, b_ada).reshape(DEPTH, 8, 1, 6 * D_MODEL)

    lb = jnp.cumsum(jax.nn.softmax(hg_lb_logits.astype(F32), axis=1), axis=1)
    lb = lb - lb[:, :1]

    cos_m, sin_m = _rope_full_tables(MLA_ROPE, MLA_HEADS)
    cos_q, sin_q = _rope_full_tables(SWA_HD, SWA_HEADS)
    cos_k, sin_k = cos_q[:, :SWA_KW], sin_q[:, :SWA_KW]
    cache_k = cache_swa_k.reshape(DEC_BATCH, DEPTH, PAST_LEN, SWA_KW)
    cache_v = cache_swa_v.reshape(DEC_BATCH, DEPTH, PAST_LEN, SWA_KW)
    state_t = jnp.swapaxes(state_hgrn, -1, -2)

    ctx_blk_lat = N_CTX // DEC_SEQ
    w_in_p, w_out_p = _pack_w_in(w_in), _pack_w_out(w_out)
    new_ckv, new_kr, new_k, new_v, new_st = [], [], [], [], []
    for l in range(DEPTH):
        u_mla, u_hg, u_fn, u_swa = _in_proj(x_parts, mods[l], norm1_g[l][None], w_in_p, l)

        qg, kvg = mla_q_norm_g[l][None], mla_kv_norm_g[l][None]
        wq, wkv = _pack_w_q_up(mla_w_q_up[l]), _pack_w_kv_up(mla_w_kv_up[l])
        o_mla_c, ckv_c = _mla_ctx(u_mla, qg, wq, kvg, wkv)
        o_mla_l = _mla_lat(u_mla, cache_mla_ckv, cache_mla_krope, l, cos_m, sin_m, qg, wq, kvg, wkv)

        lbf, lbb = lb[0, l][None], lb[1, l][None]
        ng4 = jnp.tile(hg_norm_g[l], HG_HEADS)[None]
        o_hg_c, st_c = _hgrn(u_hg, lbf, lbb, ng4, None, seq=SEQ, n_batch=BATCH, row_block0=0)
        o_hg_l = _hgrn(u_hg, lbf, lbb, ng4, state_t[:, l], seq=DEC_SEQ, n_batch=DEC_BATCH,
                       row_block0=ctx_blk_lat)

        fw = fn_w[l].astype(BF16)
        o_fn_c = _fourier(u_fn, fw, seq=SEQ, n_batch=BATCH, row_block0=0)
        o_fn_l = _fourier(u_fn, fw, seq=DEC_SEQ, n_batch=DEC_BATCH, row_block0=ctx_blk_lat)

        sink = swa_sink[l]
        o_swa_c = _swa_ctx(u_swa, sink)
        o_swa_l = _swa_lat(u_swa, cache_k, cache_v, l, sink, cos_q, sin_q, cos_k, sin_k)

        x1, h2, gate_t, rank_t, cnt = _out_proj(
            x_parts, ((o_mla_c, o_mla_l), (o_hg_c, o_hg_l), (o_fn_c, o_fn_l), (o_swa_c, o_swa_l)),
            mods[l], norm2_g[l][None], w_out_p, l,
            moe_w_router[l], moe_b_router[l][:, None],
            sh_w_gate[l].astype(BF16), sh_w_up[l].astype(BF16), sh_w_down[l].astype(BF16))
        plan = _segment_plan(cnt)
        xs = _dispatch(h2, rank_t, plan)
        ys = _experts(xs, plan, moe_w_gate, moe_w_up, moe_w_down, l)
        if l < DEPTH - 1:
            x_parts = (_combine(x1, ys, gate_t, rank_t, mods[l], plan, None),)
        else:
            y_prompt, y_sample = _combine(x1, ys, gate_t, rank_t, mods[l], plan, final_norm_g[None])

        new_ckv.append(ckv_c.reshape(BATCH, SEQ, MLA_KV_LORA))
        new_kr.append(u_mla[:N_CTX, MLA_Q_LORA + MLA_KV_LORA:MLA_IN].reshape(BATCH, SEQ, MLA_ROPE))
        new_k.append(u_swa[:N_CTX, SWA_QW:SWA_QW + SWA_KW].reshape(BATCH, SEQ, SWA_KV_HEADS, SWA_HD))
        new_v.append(u_swa[:N_CTX, SWA_QW + SWA_KW:].reshape(BATCH, SEQ, SWA_KV_HEADS, SWA_HD))
        new_st.append(jnp.swapaxes(st_c, -1, -2))

    y_prompt = y_prompt.reshape(BATCH, SEQ, D_MODEL)
    y_sample = y_sample.reshape(DEC_BATCH, DEC_SEQ, D_MODEL)
    stack = lambda xs: jnp.stack(xs, axis=1)
    return (y_prompt, y_sample, stack(new_ckv), stack(new_kr), stack(new_k), stack(new_v), stack(new_st))
```

```python
import functools

import numpy as np
import jax
import jax.numpy as jnp
from jax import lax
from jax.experimental import pallas as pl
from jax.experimental.pallas import tpu as pltpu

F32 = jnp.float32
BF16 = jnp.bfloat16

D_MODEL = 1024
BATCH = 32
SEQ = 256
DEPTH = 2
DEC_BATCH = 2
DEC_SEQ = 1024
PAST_LEN = 256
GRID_W = 64
EPS = 1e-6
ROPE_BASE = 10000.0
NEG_INF = -1e30

MLA_HEADS = 4
MLA_NOPE = 64
MLA_ROPE = 32
MLA_V = 64
MLA_Q_LORA = 256
MLA_KV_LORA = 128
HG_HEADS = 4
HG_DK = 64
HG_DV = 64
HG_W = HG_HEADS * HG_DK
FN_GROUPS = 4
FN_WIDTH = 256
SWA_HEADS = 4
SWA_KV_HEADS = 2
SWA_HD = 64
WINDOW = 128
N_EXPERTS = 64
TOP_K = 6
D_EXPERT = 256
D_SHARED = 256
ROUTE_SCALE = 2.5

MLA_IN = MLA_Q_LORA + MLA_KV_LORA + MLA_ROPE
HG_IN = 3 * HG_HEADS * HG_DK + 2 * HG_HEADS * HG_DV
FN_IN = FN_WIDTH
SWA_IN = (SWA_HEADS + 2 * SWA_KV_HEADS) * SWA_HD

N_CTX = BATCH * SEQ
N_LAT = DEC_BATCH * DEC_SEQ
N_TOK = N_CTX + N_LAT

MLA_PACK = 512
U_COLS = MLA_PACK + HG_IN + FN_IN + SWA_IN

ROW_TILE = 256
CTX_TILES = N_CTX // ROW_TILE
LAT_TILES_PER_BATCH = DEC_SEQ // ROW_TILE
HG_CHUNK = 32
HG_BLOCK = 256
SWA_QBLK = 128
MLA_QBLK = 256
CTX_PER_STEP = 4
SWA_CTX_PER_STEP = 8
HG_CTX_PER_STEP = 4
OUT_ROWS = 2 * ROW_TILE
VMEM_LIMIT = 56 * 1024 * 1024


def _dot(a, b):
    return jnp.dot(a, b, preferred_element_type=F32)


def _dot_nt(a, b):
    return lax.dot_general(a, b, (((1,), (1,)), ((), ())), preferred_element_type=F32)


def _dot_tn(a, b):
    return lax.dot_general(a, b, (((0,), (0,)), ((), ())), preferred_element_type=F32)


def _split3(x):
    hi = x.astype(BF16)
    r1 = x - hi.astype(F32)
    mid = r1.astype(BF16)
    return hi, mid, (r1 - mid.astype(F32)).astype(BF16)


def _dot_exact_lhs(a, b):
    ab = a.astype(BF16)
    hi, mid, lo = _split3(b)
    return (_dot(ab, lo) + _dot(ab, mid)) + _dot(ab, hi)


def _dot_exact_rhs(a, b):
    bb = b.astype(BF16)
    hi, mid, lo = _split3(a)
    return (_dot(lo, bb) + _dot(mid, bb)) + _dot(hi, bb)


def _rms(x, g):
    return x * lax.rsqrt(jnp.mean(x * x, axis=-1, keepdims=True) + EPS) * g


def _silu(x):
    return x * jax.nn.sigmoid(x)


def _mod_row(i):
    return jnp.where(i < CTX_TILES, 0, 1 + (i - CTX_TILES) // LAT_TILES_PER_BATCH)


def _params(*sem):
    return pltpu.CompilerParams(dimension_semantics=sem, vmem_limit_bytes=VMEM_LIMIT)


ADA_COLS = 1536


def _ada_kernel(cv_ref, w_ref, b_ref, o_ref):
    a = _silu(cv_ref[...]).astype(BF16)
    o_ref[...] = _dot(a, w_ref[...].astype(BF16)) + b_ref[...]


def _ada(cv8, w_ada, b_ada):
    return pl.pallas_call(
        _ada_kernel,
        grid=(DEPTH, 6 * D_MODEL // ADA_COLS),
        in_specs=[
            pl.BlockSpec((8, D_MODEL), lambda l, j: (0, 0)),
            pl.BlockSpec((None, D_MODEL, ADA_COLS), lambda l, j: (l, 0, j)),
            pl.BlockSpec((None, 1, ADA_COLS), lambda l, j: (l, 0, j)),
        ],
        out_specs=pl.BlockSpec((None, 8, ADA_COLS), lambda l, j: (l, 0, j)),
        out_shape=jax.ShapeDtypeStruct((DEPTH, 8, 6 * D_MODEL), F32),
        compiler_params=_params("arbitrary", "arbitrary"),
        name="ada",
    )(cv8, w_ada, b_ada.reshape(DEPTH, 1, 6 * D_MODEL))


def _ctx_lat_specs(width, tile=ROW_TILE):
    n_ctx = N_CTX // tile
    return [pl.BlockSpec((tile, width), lambda i, *_: (jnp.minimum(i, n_ctx - 1), 0)),
            pl.BlockSpec((tile, width), lambda i, *_: (jnp.maximum(i - n_ctx, 0), 0))]


def _row_specs(parts, width, tile=ROW_TILE):
    if len(parts) == 2:
        return _ctx_lat_specs(width, tile)
    return [pl.BlockSpec((tile, width), lambda i, *_: (i, 0))]


def _read_rows(refs, tile=ROW_TILE):
    if len(refs) == 1:
        return refs[0][...]
    return jnp.where(pl.program_id(0) < N_CTX // tile, refs[0][...], refs[1][...])


def _in_kernel(*refs, n_x):
    x = _read_rows(refs[:n_x], OUT_ROWS)
    mod_ref, g_ref, w_ref, umla_ref, uhg_ref, ufn_ref, uswa_ref = refs[n_x:]
    sh1 = mod_ref[:, 0:D_MODEL]
    sc1 = mod_ref[:, D_MODEL:2 * D_MODEL]
    h = _rms(x, g_ref[...]) * (1.0 + sc1) + sh1
    u = _dot(h.astype(BF16), w_ref[...])
    o = 0
    for ref, width in ((umla_ref, MLA_PACK), (uhg_ref, HG_IN), (ufn_ref, FN_IN), (uswa_ref, SWA_IN)):
        ref[...] = u[:, o:o + width].astype(ref.dtype)
        o += width


def _in_proj(x_parts, mod, g, w, l):
    row = lambda i: (i, 0)
    widths = (MLA_PACK, HG_IN, FN_IN, SWA_IN)
    return pl.pallas_call(
        functools.partial(_in_kernel, n_x=len(x_parts)),
        grid=(N_TOK // OUT_ROWS,),
        in_specs=_row_specs(x_parts, D_MODEL, OUT_ROWS) + [
            pl.BlockSpec((None, 1, 6 * D_MODEL), lambda i: (_mod_row(i * (OUT_ROWS // ROW_TILE)), 0, 0)),
            pl.BlockSpec((1, D_MODEL), lambda i: (0, 0)),
            pl.BlockSpec((None, D_MODEL, U_COLS), lambda i: (l, 0, 0))],
        out_specs=[pl.BlockSpec((OUT_ROWS, wd), row) for wd in widths],
        out_shape=[jax.ShapeDtypeStruct((N_TOK, wd), F32) for wd in widths],
        compiler_params=_params("arbitrary"),
        name="in_proj",
    )(*x_parts, mod, g, w)


def _rope(x, cos, sin_signed):
    lane = lax.broadcasted_iota(jnp.int32, x.shape, 1)
    width = x.shape[1]
    swapped = jnp.where(lane % 2 == 0, pltpu.roll(x, width - 1, 1), pltpu.roll(x, 1, 1))
    return x * cos + swapped * sin_signed


def _stack_heads(x, n_heads, head_w):
    lane = lax.broadcasted_iota(jnp.int32, x.shape, 1)
    return jnp.concatenate([jnp.where(lane // head_w == h, x, 0.0) for h in range(n_heads)], axis=0)


def _unstack_heads(o, n_heads, head_w):
    t = o.shape[0] // n_heads
    lane = lax.broadcasted_iota(jnp.int32, (t, o.shape[1]), 1)
    out = jnp.zeros((t, o.shape[1]), F32)
    for h in range(n_heads):
        out = jnp.where(lane // head_w == h, o[h * t:(h + 1) * t], out)
    return out


MLA_SCALE = (MLA_NOPE + MLA_ROPE) ** -0.5
MLA_QW = MLA_HEADS * MLA_NOPE + MLA_HEADS * MLA_ROPE
MLA_NW = MLA_HEADS * MLA_NOPE


def _mla_attend(q, kcat, v):
    qs = jnp.concatenate([_stack_heads(q[:, :MLA_NW], MLA_HEADS, MLA_NOPE),
                          _stack_heads(q[:, MLA_NW:], MLA_HEADS, MLA_ROPE)], axis=1)
    s = _dot_nt(qs.astype(BF16), kcat) * MLA_SCALE
    p = jnp.exp(s - jnp.max(s, axis=-1, keepdims=True))
    o = _dot(p.astype(BF16), v) / jnp.sum(p, axis=-1, keepdims=True)
    return _unstack_heads(o, MLA_HEADS, MLA_V)


def _mla_ctx_kernel(u_ref, qg_ref, wq_ref, kvg_ref, wkv_ref, o_ref, ckv_ref):
    for j in range(CTX_PER_STEP):
        rows = slice(j * SEQ, (j + 1) * SEQ)
        u = u_ref[rows, :]
        q = _dot(_rms(u[:, :MLA_Q_LORA], qg_ref[...]).astype(BF16), wq_ref[...])
        ckv = _rms(u[:, MLA_Q_LORA:MLA_Q_LORA + MLA_KV_LORA], kvg_ref[...])
        ckv_ref[rows, :] = ckv
        kv = _dot(ckv.astype(BF16), wkv_ref[...])
        kr4 = u[:, MLA_Q_LORA + MLA_KV_LORA:]
        kcat = jnp.concatenate([kv[:, :MLA_NW], kr4], axis=1).astype(BF16)
        o_ref[rows, :] = _mla_attend(q, kcat, kv[:, MLA_NW:].astype(BF16)).astype(o_ref.dtype)


def _mla_ctx(u_mla, qg, wq, kvg, wkv):
    full = lambda shape: pl.BlockSpec(shape, lambda b: (0, 0))
    rows = CTX_PER_STEP * SEQ
    return pl.pallas_call(
        _mla_ctx_kernel,
        grid=(BATCH // CTX_PER_STEP,),
        in_specs=[pl.BlockSpec((rows, MLA_PACK), lambda b: (b, 0)),
                  full((1, MLA_Q_LORA)), full((MLA_Q_LORA, MLA_QW)),
                  full((1, MLA_KV_LORA)), full((MLA_KV_LORA, 2 * MLA_NW))],
        out_specs=[pl.BlockSpec((rows, MLA_NW), lambda b: (b, 0)),
                   pl.BlockSpec((rows, MLA_KV_LORA), lambda b: (b, 0))],
        out_shape=[jax.ShapeDtypeStruct((N_CTX, MLA_NW), BF16),
                   jax.ShapeDtypeStruct((N_CTX, MLA_KV_LORA), F32)],
        compiler_params=_params("arbitrary"),
        name="mla_ctx",
    )(u_mla, qg, wq, kvg, wkv)


MLA_TK = PAST_LEN + DEC_SEQ


def _mla_lat_kernel(u_ref, cckv_ref, ckr_ref, cos_ref, sin_ref, qg_ref, wq_ref, kvg_ref, wkv_ref,
                    o_ref, kcat_s, v_s):
    i = pl.program_id(1)

    @pl.when(i == 0)
    def _():
        u = u_ref[...]
        ckv_new = _rms(u[:, MLA_Q_LORA:MLA_Q_LORA + MLA_KV_LORA], kvg_ref[...])
        ckv_all = jnp.concatenate([cckv_ref[...], ckv_new], axis=0)
        kv = _dot(ckv_all.astype(BF16), wkv_ref[...])
        kr_new = _rope(u[:, MLA_Q_LORA + MLA_KV_LORA:], cos_ref[...], sin_ref[...])
        ckr = ckr_ref[...]
        kr_all = jnp.concatenate([jnp.concatenate([ckr] * MLA_HEADS, axis=1), kr_new], axis=0)
        kcat_s[...] = jnp.concatenate([kv[:, :MLA_NW], kr_all], axis=1).astype(BF16)
        v_s[...] = kv[:, MLA_NW:].astype(BF16)

    r0 = pl.multiple_of(i * MLA_QBLK, MLA_QBLK)
    cq = u_ref[pl.ds(r0, MLA_QBLK), 0:MLA_Q_LORA]
    q = _dot(_rms(cq, qg_ref[...]).astype(BF16), wq_ref[...])
    qr = _rope(q[:, MLA_NW:], cos_ref[pl.ds(r0, MLA_QBLK), :], sin_ref[pl.ds(r0, MLA_QBLK), :])
    q = jnp.concatenate([q[:, :MLA_NW], qr], axis=1)
    o_ref[...] = _mla_attend(q, kcat_s[...], v_s[...]).astype(o_ref.dtype)


def _mla_lat(u_mla, cache_ckv, cache_kr, l, cos, sin, qg, wq, kvg, wkv):
    full = lambda shape: pl.BlockSpec(shape, lambda b, i: (0, 0))
    nq = DEC_SEQ // MLA_QBLK
    return pl.pallas_call(
        _mla_lat_kernel,
        grid=(DEC_BATCH, nq),
        in_specs=[pl.BlockSpec((DEC_SEQ, MLA_PACK), lambda b, i: (N_CTX // DEC_SEQ + b, 0)),
                  pl.BlockSpec((None, None, PAST_LEN, MLA_KV_LORA), lambda b, i: (b, l, 0, 0)),
                  pl.BlockSpec((None, None, PAST_LEN, MLA_ROPE), lambda b, i: (b, l, 0, 0)),
                  full((DEC_SEQ, MLA_HEADS * MLA_ROPE)), full((DEC_SEQ, MLA_HEADS * MLA_ROPE)),
                  full((1, MLA_Q_LORA)), full((MLA_Q_LORA, MLA_QW)),
                  full((1, MLA_KV_LORA)), full((MLA_KV_LORA, 2 * MLA_NW))],
        out_specs=pl.BlockSpec((MLA_QBLK, MLA_NW), lambda b, i: (b * nq + i, 0)),
        out_shape=jax.ShapeDtypeStruct((N_LAT, MLA_NW), BF16),
        scratch_shapes=[pltpu.VMEM((MLA_TK, MLA_QW), BF16), pltpu.VMEM((MLA_TK, MLA_NW), BF16)],
        compiler_params=_params("arbitrary", "arbitrary"),
        name="mla_lat",
    )(u_mla, cache_ckv, cache_kr, cos, sin, qg, wq, kvg, wkv)


def _hgrn_kernel(*refs, seq, n_seq, has_state):
    if has_state:
        (u_ref, lbf_ref, lbb_ref, ng_ref, s0_ref, o_ref,
         q_s, kf_s, gf_s, kb_s, gb_s, of_s, ob_s, stf_s, stb_s) = refs
    else:
        (u_ref, lbf_ref, lbb_ref, ng_ref, o_ref, so_ref,
         q_s, kf_s, gf_s, kb_s, gb_s, of_s, ob_s, stf_s, stb_s) = refs
    C = HG_CHUNK
    W = HG_W

    q_s[...] = _silu(u_ref[:, 0:W])
    ff = lbf_ref[...] + (1.0 - lbf_ref[...]) * jax.nn.sigmoid(u_ref[:, W:2 * W])
    kf_s[...] = 1.0 - ff
    gf_s[...] = jnp.log(ff)
    fb = lbb_ref[...] + (1.0 - lbb_ref[...]) * jax.nn.sigmoid(u_ref[:, 2 * W:3 * W])
    kb_s[...] = 1.0 - fb
    gb_s[...] = jnp.log(fb)

    rr = lax.broadcasted_iota(jnp.int32, (W, W), 0)
    cc = lax.broadcasted_iota(jnp.int32, (W, W), 1)
    blockdiag = rr // HG_DK == cc // HG_DK
    if has_state:
        for st, d in ((stf_s, 0), (stb_s, 1)):
            rows = []
            for h in range(HG_HEADS):
                z = lambda n: jnp.zeros((HG_DV, n * HG_DK), F32)
                parts = ([z(h)] if h else []) + [s0_ref[d, h]] + ([z(HG_HEADS - 1 - h)] if h < HG_HEADS - 1 else [])
                rows.append(jnp.concatenate(parts, axis=1) if len(parts) > 1 else parts[0])
            st[0] = jnp.concatenate(rows, axis=0)
    else:
        stf_s[...] = jnp.zeros_like(stf_s)
        stb_s[...] = jnp.zeros_like(stb_s)

    B = HG_BLOCK
    per_block = B // C
    n_blocks = seq // B
    ri = lax.broadcasted_iota(jnp.int32, (B, B), 0)
    ci = lax.broadcasted_iota(jnp.int32, (B, B), 1)
    same_chunk = ri // C == ci // C
    rs = lax.broadcasted_iota(jnp.int32, (HG_HEADS * B, B), 0) % B
    cs = lax.broadcasted_iota(jnp.int32, (HG_HEADS * B, B), 1)
    same_chunk_s = rs // C == cs // C

    sums_f = jnp.where(same_chunk & (ci <= ri), 1.0, 0.0)
    sums_b = jnp.where(same_chunk & (ci >= ri), 1.0, 0.0)
    keep_f = same_chunk_s & (rs >= cs)
    keep_b = same_chunk_s & (cs >= rs)

    def chunk_row(x, i):
        x3 = x.reshape(per_block, C, W)
        return jnp.broadcast_to(x3[:, i:i + 1, :], (per_block, C, W)).reshape(B, W)

    def block(r, k_s, g_s, o_s, st_s, sum_mat, keep, order, mid, far):
        q = q_s[pl.ds(r, B), :]
        k = k_s[pl.ds(r, B), :]
        v = u_ref[pl.ds(r, B), 3 * W:4 * W].astype(BF16)
        G = _dot_exact_lhs(sum_mat, g_s[pl.ds(r, B), :])
        Gq = G - chunk_row(G, mid)
        Gk2 = chunk_row(G, far) - G
        qe = _stack_heads(q * jnp.exp(Gq), HG_HEADS, HG_DK)
        ke = k * jnp.exp(-Gq)
        A = jnp.where(keep, _dot_nt(qe.astype(BF16), ke.astype(BF16)), 0.0)
        o_intra = _unstack_heads(_dot(A.astype(BF16), v), HG_HEADS, HG_DV)
        qg = (q * jnp.exp(G)).astype(BF16)
        k2 = (k * jnp.exp(Gk2)).astype(BF16)
        decay = jnp.exp(G + Gk2)
        st = st_s[...]
        o_inter = [None] * per_block
        for c in order:
            rows = slice(c * C, (c + 1) * C)
            o_inter[c] = _dot_nt(st.astype(BF16), qg[rows])
            st = st * decay[c * C:c * C + 1] + jnp.where(blockdiag, _dot_tn(v[rows], k2[rows]), 0.0)
        st_s[...] = st
        o_s[pl.ds(r, B), :] = o_intra + jnp.concatenate(o_inter, axis=1).T

    def fwd(j, r):
        block(r, kf_s, gf_s, of_s, stf_s.at[j], sums_f, keep_f, range(per_block), C // 2 - 1, C - 1)

    def bwd(j, r):
        block(r, kb_s, gb_s, ob_s, stb_s.at[j], sums_b, keep_b, range(per_block - 1, -1, -1), C // 2, 0)

    for j in range(n_seq):
        if n_blocks == 1:
            fwd(j, j * seq)
            bwd(j, j * seq)
        else:
            lax.fori_loop(0, n_blocks, lambda i, c, j=j: (fwd(j, pl.multiple_of(j * seq + i * B, B)), c)[1], 0)
            lax.fori_loop(0, n_blocks,
                          lambda i, c, j=j: (bwd(j, pl.multiple_of(j * seq + (n_blocks - 1 - i) * B, B)), c)[1], 0)

    o = of_s[...] + ob_s[...]
    ms = _dot_exact_rhs(o * o, jnp.where(blockdiag, 1.0 / HG_DV, 0.0))
    on = o * lax.rsqrt(ms + EPS) * ng_ref[...]
    o_ref[...] = (on * _silu(u_ref[:, 4 * W:5 * W])).astype(o_ref.dtype)

    if not has_state:
        for j in range(n_seq):
            for st, d in ((stf_s, 0), (stb_s, 1)):
                for h in range(HG_HEADS):
                    so_ref[j, d, h] = st[j, h * HG_DV:(h + 1) * HG_DV, h * HG_DK:(h + 1) * HG_DK]


def _hgrn(u_hg, lbf, lbb, ng4, state_t, *, seq, n_batch, row_block0):
    has_state = state_t is not None
    n_seq = 1 if has_state else HG_CTX_PER_STEP
    rows = n_seq * seq
    full = lambda shape: pl.BlockSpec(shape, lambda b: (0, 0))
    in_specs = [pl.BlockSpec((rows, HG_IN), lambda b: (row_block0 + b, 0)),
                full((1, HG_W)), full((1, HG_W)), full((1, HG_W))]
    args = [u_hg, lbf, lbb, ng4]
    o_spec = pl.BlockSpec((rows, HG_W), lambda b: (b, 0))
    o_shape = jax.ShapeDtypeStruct((n_batch * seq, HG_W), BF16)
    if has_state:
        in_specs.append(pl.BlockSpec((None, 2, HG_HEADS, HG_DV, HG_DK), lambda b: (b, 0, 0, 0, 0)))
        args.append(state_t)
        out_specs, out_shape = o_spec, o_shape
    else:
        out_specs = [o_spec, pl.BlockSpec((n_seq, 2, HG_HEADS, HG_DV, HG_DK), lambda b: (b, 0, 0, 0, 0))]
        out_shape = [o_shape, jax.ShapeDtypeStruct((n_batch, 2, HG_HEADS, HG_DV, HG_DK), F32)]
    return pl.pallas_call(
        functools.partial(_hgrn_kernel, seq=seq, n_seq=n_seq, has_state=has_state),
        grid=(n_batch // n_seq,),
        in_specs=in_specs, out_specs=out_specs, out_shape=out_shape,
        scratch_shapes=[pltpu.VMEM((rows, HG_W), F32)] * 7 + [pltpu.VMEM((n_seq, HG_W, HG_W), F32)] * 2,
        compiler_params=_params("arbitrary"),
        name="hgrn_lat" if has_state else "hgrn_ctx",
    )(*args)


def _dft_tables(n):
    j = np.arange(n, dtype=np.int64)
    ang = 2.0 * np.pi * ((j[:, None] * j[None, :]) % n).astype(np.float64) / n
    return np.cos(ang) / np.sqrt(n), np.sin(ang) / np.sqrt(n)


def _fourier_tables(seq):
    gw = FN_WIDTH // FN_GROUPS
    cg, sg = _dft_tables(gw)
    eye = np.eye(FN_GROUPS)
    chan = np.concatenate([np.kron(eye, cg), np.kron(eye, sg)], axis=1)
    ct, st = _dft_tables(seq)
    pos = np.concatenate([ct, -st], axis=1)
    return jnp.asarray(chan, F32).astype(BF16), jnp.asarray(pos, F32).astype(BF16)


def _fourier_kernel(x_ref, chan_ref, pos_ref, w_ref, o_ref, *, seq, n_seq):
    for j in range(n_seq):
        rows = slice(j * seq, (j + 1) * seq)
        x12 = _dot(x_ref[rows, :].astype(BF16), chan_ref[...])
        z = jnp.concatenate([x12[:, :FN_WIDTH], x12[:, FN_WIDTH:]], axis=0).astype(BF16)
        y = _dot(pos_ref[...], z)
        o_ref[rows, :] = _dot(y.astype(BF16), w_ref[...]).astype(o_ref.dtype)


def _fourier(u_fn, w, *, seq, n_batch, row_block0):
    chan, pos = _fourier_tables(seq)
    full = lambda shape: pl.BlockSpec(shape, lambda b: (0, 0))
    n_seq = CTX_PER_STEP if seq == SEQ else 1
    return pl.pallas_call(
        functools.partial(_fourier_kernel, seq=seq, n_seq=n_seq),
        grid=(n_batch // n_seq,),
        in_specs=[pl.BlockSpec((n_seq * seq, FN_WIDTH), lambda b: (row_block0 + b, 0)),
                  full((FN_WIDTH, 2 * FN_WIDTH)), full((seq, 2 * seq)), full((FN_WIDTH, FN_WIDTH))],
        out_specs=pl.BlockSpec((n_seq * seq, FN_WIDTH), lambda b: (b, 0)),
        out_shape=jax.ShapeDtypeStruct((n_batch * seq, FN_WIDTH), BF16),
        compiler_params=_params("arbitrary"),
        name="fourier",
    )(u_fn, chan, pos, w)


SWA_SCALE = SWA_HD ** -0.5
SWA_QW = SWA_HEADS * SWA_HD
SWA_KW = SWA_KV_HEADS * SWA_HD
SWA_STACK_ORDER = (0, 2, 1, 3)


def _swa_stack_q(q):
    return jnp.concatenate([_stack_heads(q[:, :SWA_KW], SWA_KV_HEADS, SWA_HD),
                            _stack_heads(q[:, SWA_KW:], SWA_KV_HEADS, SWA_HD)], axis=0)


def _swa_unstack_o(o):
    t = o.shape[0] // SWA_HEADS
    return jnp.concatenate([_unstack_heads(o[:2 * t], SWA_KV_HEADS, SWA_HD),
                            _unstack_heads(o[2 * t:], SWA_KV_HEADS, SWA_HD)], axis=1)


def _sink_rows(sink_ref, t):
    return jnp.concatenate([jnp.full((t, 1), sink_ref[h], F32) for h in SWA_STACK_ORDER], axis=0)


def _swa_ctx_kernel(sink_ref, u_ref, o_ref):
    sink = _sink_rows(sink_ref, SEQ)
    for j in range(SWA_CTX_PER_STEP):
        rows = slice(j * SEQ, (j + 1) * SEQ)
        u = u_ref[rows, :]
        qs = _swa_stack_q(u[:, :SWA_QW]).astype(BF16)
        k = u[:, SWA_QW:SWA_QW + SWA_KW].astype(BF16)
        v = u[:, SWA_QW + SWA_KW:].astype(BF16)
        s = _dot_nt(qs, k) * SWA_SCALE
        m = jnp.maximum(jnp.max(s, axis=-1, keepdims=True), sink)
        p = jnp.exp(s - m)
        denom = jnp.sum(p, axis=-1, keepdims=True) + jnp.exp(sink - m)
        o_ref[rows, :] = _swa_unstack_o(_dot(p.astype(BF16), v) / denom).astype(o_ref.dtype)


def _swa_ctx(u_swa, sink):
    rows = SWA_CTX_PER_STEP * SEQ
    return pl.pallas_call(
        _swa_ctx_kernel,
        grid=(BATCH // SWA_CTX_PER_STEP,),
        in_specs=[pl.BlockSpec(memory_space=pltpu.SMEM),
                  pl.BlockSpec((rows, SWA_IN), lambda b: (b, 0))],
        out_specs=pl.BlockSpec((rows, SWA_QW), lambda b: (b, 0)),
        out_shape=jax.ShapeDtypeStruct((N_CTX, SWA_QW), BF16),
        compiler_params=_params("arbitrary"),
        name="swa_ctx",
    )(sink, u_swa)


SWA_PAD = DEC_SEQ + 2 * SWA_QBLK


def _swa_lat_kernel(sink_ref, u_ref, kc_ref, vc_ref, cosq_ref, sinq_ref, cosk_ref, sin_k_ref,
                    o_ref, k_s, v_s):
    i = pl.program_id(1)
    B = SWA_QBLK

    @pl.when(i == 0)
    def _():
        zeros = jnp.zeros((B, SWA_KW), BF16)
        k = _rope(u_ref[:, SWA_QW:SWA_QW + SWA_KW], cosk_ref[...], sin_k_ref[...]).astype(BF16)
        k_s[...] = jnp.concatenate([zeros, k, zeros], axis=0)
        v_s[...] = jnp.concatenate([zeros, u_ref[:, SWA_QW + SWA_KW:].astype(BF16), zeros], axis=0)

    r0 = pl.multiple_of(i * B, B)
    q = _rope(u_ref[pl.ds(r0, B), 0:SWA_QW], cosq_ref[pl.ds(r0, B), :], sinq_ref[pl.ds(r0, B), :])
    qs = _swa_stack_q(q).astype(BF16)
    s_loc = _dot_nt(qs, k_s[pl.ds(r0, 3 * B), :]) * SWA_SCALE
    row = lax.broadcasted_iota(jnp.int32, s_loc.shape, 0) % B
    col = lax.broadcasted_iota(jnp.int32, s_loc.shape, 1)
    kpos = r0 - B + col
    valid = (jnp.abs(row + B - col) <= WINDOW) & (kpos >= 0) & (kpos < DEC_SEQ)
    s_loc = jnp.where(valid, s_loc, NEG_INF)
    s_ctx = _dot_nt(qs, kc_ref[...].astype(BF16)) * SWA_SCALE
    sink = _sink_rows(sink_ref, B)
    m = jnp.maximum(jnp.maximum(jnp.max(s_loc, axis=-1, keepdims=True),
                                jnp.max(s_ctx, axis=-1, keepdims=True)), sink)
    p_loc = jnp.exp(s_loc - m)
    p_ctx = jnp.exp(s_ctx - m)
    denom = (jnp.sum(p_loc, axis=-1, keepdims=True) + jnp.sum(p_ctx, axis=-1, keepdims=True)
             + jnp.exp(sink - m))
    o = _dot(p_loc.astype(BF16), v_s[pl.ds(r0, 3 * B), :]) + _dot(p_ctx.astype(BF16), vc_ref[...].astype(BF16))
    o_ref[...] = _swa_unstack_o(o / denom).astype(o_ref.dtype)


def _swa_lat(u_swa, cache_k, cache_v, l, sink, cosq, sinq, cosk, sink_k):
    full = lambda shape: pl.BlockSpec(shape, lambda b, i: (0, 0))
    nq = DEC_SEQ // SWA_QBLK
    cache_spec = pl.BlockSpec((None, None, PAST_LEN, SWA_KW), lambda b, i: (b, l, 0, 0))
    return pl.pallas_call(
        _swa_lat_kernel,
        grid=(DEC_BATCH, nq),
        in_specs=[pl.BlockSpec(memory_space=pltpu.SMEM),
                  pl.BlockSpec((DEC_SEQ, SWA_IN), lambda b, i: (N_CTX // DEC_SEQ + b, 0)),
                  cache_spec, cache_spec,
                  full((DEC_SEQ, SWA_QW)), full((DEC_SEQ, SWA_QW)),
                  full((DEC_SEQ, SWA_KW)), full((DEC_SEQ, SWA_KW))],
        out_specs=pl.BlockSpec((SWA_QBLK, SWA_QW), lambda b, i: (b * nq + i, 0)),
        out_shape=jax.ShapeDtypeStruct((N_LAT, SWA_QW), BF16),
        scratch_shapes=[pltpu.VMEM((SWA_PAD, SWA_KW), BF16), pltpu.VMEM((SWA_PAD, SWA_KW), BF16)],
        compiler_params=_params("arbitrary", "arbitrary"),
        name="swa_lat",
    )(sink, u_swa, cache_k, cache_v, cosq, sinq, cosk, sink_k)


N_BLK = N_TOK // ROW_TILE
SEG_ALIGN = 16
LOCAL_ROWS = ROW_TILE * TOP_K + N_EXPERTS * SEG_ALIGN
LOCAL_CHUNK = 512
EXPERT_TILE = 768
SORTED_ROWS = -(-(N_TOK * TOP_K + N_BLK * N_EXPERTS * SEG_ALIGN + N_EXPERTS * (EXPERT_TILE + SEG_ALIGN))
                // EXPERT_TILE) * EXPERT_TILE
MAX_TILES = SORTED_ROWS // EXPERT_TILE
NOT_PICKED = -1.0
NO_DEST = 4095.0


def _out_kernel(*refs, n_x):
    x_all = _read_rows(refs[:n_x], OUT_ROWS)
    mix_all = [_read_rows(refs[n_x + 2 * j:n_x + 2 * j + 2], OUT_ROWS) for j in range(4)]
    out_refs = refs[n_x + 8:]
    for blk in range(OUT_ROWS // ROW_TILE):
        rows = slice(blk * ROW_TILE, (blk + 1) * ROW_TILE)
        _out_block(x_all[rows], [m[rows] for m in mix_all], out_refs, blk)


def _out_block(x, mixers, refs, blk):
    (mod_ref, g_ref, wo_ref, wr_ref, br_ref, wsg_ref, wsu_ref, wsd_ref,
     x1_ref, h_ref, gate_ref, rank_ref, cnt_ref) = refs
    rows = slice(blk * ROW_TILE, (blk + 1) * ROW_TILE)
    mix = jnp.zeros((ROW_TILE, D_MODEL), F32)
    for j in range(4):
        mix = mix + _dot(mixers[j], wo_ref[j * 256:(j + 1) * 256, :])
    g1 = mod_ref[:, 2 * D_MODEL:3 * D_MODEL]
    sh2 = mod_ref[:, 3 * D_MODEL:4 * D_MODEL]
    sc2 = mod_ref[:, 4 * D_MODEL:5 * D_MODEL]
    g2 = mod_ref[:, 5 * D_MODEL:6 * D_MODEL]
    x1 = x + g1 * mix
    h = _rms(x1, g_ref[...]) * (1.0 + sc2) + sh2
    hb = h.astype(BF16)
    h_ref[rows, :] = hb

    w_hi = wr_ref[...].astype(BF16)
    w_lo = (wr_ref[...] - w_hi.astype(F32)).astype(BF16)
    h_lo = (h - hb.astype(F32)).astype(BF16)
    logits = ((_dot(h_lo, w_hi) + _dot(hb, w_lo)) + _dot(hb, w_hi)).T
    scores = jax.nn.sigmoid(logits)
    sel = scores + br_ref[...]
    eidx = lax.broadcasted_iota(jnp.int32, sel.shape, 0)
    gate = jnp.zeros_like(scores)
    picked = jnp.zeros_like(scores)
    for _ in range(TOP_K):
        best = jnp.max(sel, axis=0, keepdims=True)
        first = jnp.min(jnp.where(sel == best, eidx, N_EXPERTS), axis=0, keepdims=True)
        pick = eidx == first
        gate = jnp.where(pick, scores, gate)
        picked = jnp.where(pick, 1.0, picked)
        sel = jnp.where(pick, -jnp.inf, sel)
    gate = ROUTE_SCALE * gate / jnp.sum(gate, axis=0, keepdims=True)

    ti = lax.broadcasted_iota(jnp.int32, (ROW_TILE, ROW_TILE), 0)
    tj = lax.broadcasted_iota(jnp.int32, (ROW_TILE, ROW_TILE), 1)
    pb = picked.astype(BF16)
    rank = _dot(pb, jnp.where(ti < tj, 1.0, 0.0).astype(BF16))
    gate_ref[:, rows] = gate
    rank_ref[:, rows] = jnp.where(picked > 0.0, rank, NOT_PICKED)
    counts = _dot_nt(jnp.ones((8, ROW_TILE), BF16), pb)
    cnt_ref[blk] = jnp.concatenate([counts, jnp.zeros_like(counts)], axis=1)

    hid = _silu(_dot(hb, wsg_ref[...])) * _dot(hb, wsu_ref[...])
    x1_ref[rows, :] = x1 + g2 * _dot(hid.astype(BF16), wsd_ref[...])


def _out_proj(x_parts, mixer_pairs, mod, g, wo, l, wr, br, wsg, wsu, wsd):
    row = lambda i: (i, 0)
    col = lambda i: (0, i)
    full = lambda shape: pl.BlockSpec(shape, lambda i: (0, 0))
    per_step = OUT_ROWS // ROW_TILE
    et_spec = pl.BlockSpec((N_EXPERTS, OUT_ROWS), col)
    et_shape = jax.ShapeDtypeStruct((N_EXPERTS, N_TOK), F32)
    return pl.pallas_call(
        functools.partial(_out_kernel, n_x=len(x_parts)),
        grid=(N_TOK // OUT_ROWS,),
        in_specs=_row_specs(x_parts, D_MODEL, OUT_ROWS) + 4 * _ctx_lat_specs(256, OUT_ROWS) + [
            pl.BlockSpec((None, 1, 6 * D_MODEL), lambda i: (_mod_row(i * per_step), 0, 0)),
            full((1, D_MODEL)), pl.BlockSpec((None, D_MODEL, D_MODEL), lambda i: (l, 0, 0)),
            full((D_MODEL, N_EXPERTS)), full((N_EXPERTS, 1)),
            full((D_MODEL, D_SHARED)), full((D_MODEL, D_SHARED)), full((D_SHARED, D_MODEL))],
        out_specs=[pl.BlockSpec((OUT_ROWS, D_MODEL), row), pl.BlockSpec((OUT_ROWS, D_MODEL), row),
                   et_spec, et_spec,
                   pl.BlockSpec((per_step, 8, 128), lambda i: (i, 0, 0))],
        out_shape=[jax.ShapeDtypeStruct((N_TOK, D_MODEL), F32),
                   jax.ShapeDtypeStruct((N_TOK, D_MODEL), BF16),
                   et_shape, et_shape,
                   jax.ShapeDtypeStruct((N_BLK, 8, 128), F32)],
        compiler_params=_params("arbitrary"),
        name="out_proj",
    )(*x_parts, *[a for pair in mixer_pairs for a in pair], mod, g, wo, wr, br, wsg, wsu, wsd)


def _segment_plan(cnt):
    cnt = cnt[:, 0, :N_EXPERTS].astype(jnp.int32)
    seg = jnp.maximum((cnt + (SEG_ALIGN - 1)) // SEG_ALIGN, 1) * SEG_ALIGN
    local = jnp.cumsum(seg, axis=1) - seg
    total = jnp.sum(seg, axis=1)
    per_expert = jnp.sum(seg, axis=0)
    padded = (per_expert + SEG_ALIGN + (EXPERT_TILE - 1)) // EXPERT_TILE * EXPERT_TILE
    ends = jnp.cumsum(padded)
    start = ends - padded
    dest = start[None, :] + jnp.cumsum(seg, axis=0) - seg
    n_tiles = ends[-1] // EXPERT_TILE
    tiles = jnp.arange(MAX_TILES, dtype=jnp.int32)
    tile_expert = jnp.sum((ends // EXPERT_TILE)[None, :] <= jnp.minimum(tiles, n_tiles - 1)[:, None], axis=1)
    tile_expert = tile_expert.astype(jnp.int32)
    plan = dict(seg=seg, local=local, total=total.astype(jnp.int32), dest=dest.astype(jnp.int32),
                tail_start=(start + per_expert).astype(jnp.int32), tail_rows=(padded - per_expert).astype(jnp.int32),
                n_tiles=n_tiles.reshape(1).astype(jnp.int32), tile_expert=tile_expert)
    segf, localf = seg.astype(F32), local.astype(F32)
    pad_lanes = lambda a: jnp.concatenate([a, jnp.zeros_like(a)], axis=1)[:, None, :]
    plan.update(seg_row=pad_lanes(segf), local_row=pad_lanes(localf),
                seg_col=segf[:, :, None], local_col=localf[:, :, None])
    return plan


def _local_dest_digits(rank, local_start):
    dest = jnp.where(rank >= 0.0, local_start + rank, NO_DEST)
    hi = jnp.floor(dest * (1.0 / 64.0))
    return hi, dest - 64.0 * hi


def _dispatch_kernel(dest_ref, seg_ref, local_ref, total_ref, tail_start_ref, tail_rows_ref, nt_ref,
                     h_ref, rank_ref, local_col_ref, local_row_ref, seg_row_ref,
                     xs_hbm, buf, zeros, sems, zsem, usem):
    b = pl.program_id(0)
    slot = b % 2

    def wait_block(blk, s):
        n = pl.multiple_of(total_ref[blk], SEG_ALIGN)
        pltpu.make_async_copy(buf.at[s, pl.ds(0, n)], xs_hbm.at[pl.ds(0, n)], sems.at[s]).wait()

    @pl.when(b == 0)
    def _():
        zeros[...] = jnp.zeros_like(zeros)

    def unused_tiles(action):
        for k in range(-(-MAX_TILES // N_BLK)):
            t = nt_ref[0] + b + k * N_BLK

            @pl.when(t < MAX_TILES)
            def _():
                cp = pltpu.make_async_copy(zeros.at[pl.ds(0, EXPERT_TILE)],
                                           xs_hbm.at[pl.ds(pl.multiple_of(t * EXPERT_TILE, EXPERT_TILE),
                                                           EXPERT_TILE)], usem)
                cp.start() if action == "start" else cp.wait()

    unused_tiles("start")

    @pl.when(b >= 2)
    def _():
        wait_block(b - 2, slot)

    hi, lo = _local_dest_digits(rank_ref[...], local_col_ref[...])
    code = jnp.concatenate([hi, lo], axis=0).astype(BF16)
    hb = h_ref[...]
    lstart = local_row_ref[:, :N_EXPERTS]
    lend = lstart + seg_row_ref[:, :N_EXPERTS]
    r = lax.broadcasted_iota(jnp.int32, (LOCAL_CHUNK, N_EXPERTS), 0).astype(F32)
    rr = lax.broadcasted_iota(jnp.int32, (LOCAL_CHUNK, ROW_TILE), 0).astype(F32)
    def compact(c):
        first = float(c * LOCAL_CHUNK)
        member = (r >= lstart - first) & (r < lend - first)
        table = jnp.concatenate([jnp.where(member, 64.0, 0.0), jnp.where(member, 1.0, 0.0)], axis=1).astype(BF16)
        d = _dot(table, code)
        onehot = jnp.where(d == rr + first, 1.0, 0.0).astype(BF16)
        buf[slot, c * LOCAL_CHUNK:(c + 1) * LOCAL_CHUNK, :] = _dot(onehot, hb).astype(BF16)

    n_chunks = LOCAL_ROWS // LOCAL_CHUNK
    for c in range(n_chunks - 1):
        compact(c)
    pl.when(total_ref[b] > (n_chunks - 1) * LOCAL_CHUNK)(functools.partial(compact, n_chunks - 1))

    for e in range(N_EXPERTS):
        n = pl.multiple_of(seg_ref[b, e], SEG_ALIGN)
        src = pl.multiple_of(local_ref[b, e], SEG_ALIGN)
        dst = pl.multiple_of(dest_ref[b, e], SEG_ALIGN)
        pltpu.make_async_copy(buf.at[slot, pl.ds(src, n)], xs_hbm.at[pl.ds(dst, n)], sems.at[slot]).start()

    unused_tiles("wait")

    @pl.when(b == N_BLK - 1)
    def _():
        def tail(e):
            n = pl.multiple_of(tail_rows_ref[e], SEG_ALIGN)
            dst = pl.multiple_of(tail_start_ref[e], SEG_ALIGN)
            return pltpu.make_async_copy(zeros.at[pl.ds(0, n)], xs_hbm.at[pl.ds(dst, n)], zsem)

        for e in range(N_EXPERTS):
            tail(e).start()
        for e in range(N_EXPERTS):
            tail(e).wait()
        wait_block(b - 1, 1 - slot)
        wait_block(b, slot)


def _dispatch(h, rank_t, plan):
    blk = lambda shape, imap: pl.BlockSpec(shape, imap)
    return pl.pallas_call(
        _dispatch_kernel,
        grid_spec=pltpu.PrefetchScalarGridSpec(
            num_scalar_prefetch=7, grid=(N_BLK,),
            in_specs=[blk((ROW_TILE, D_MODEL), lambda b, *_: (b, 0)),
                      blk((N_EXPERTS, ROW_TILE), lambda b, *_: (0, b)),
                      blk((None, N_EXPERTS, 1), lambda b, *_: (b, 0, 0)),
                      blk((None, 1, 128), lambda b, *_: (b, 0, 0)),
                      blk((None, 1, 128), lambda b, *_: (b, 0, 0))],
            out_specs=pl.BlockSpec(memory_space=pl.ANY),
            scratch_shapes=[pltpu.VMEM((2, LOCAL_ROWS, D_MODEL), BF16),
                            pltpu.VMEM((EXPERT_TILE + SEG_ALIGN, D_MODEL), BF16),
                            pltpu.SemaphoreType.DMA((2,)), pltpu.SemaphoreType.DMA(()),
                            pltpu.SemaphoreType.DMA(())]),
        out_shape=jax.ShapeDtypeStruct((SORTED_ROWS, D_MODEL), BF16),
        compiler_params=_params("arbitrary"),
        name="dispatch",
    )(plan['dest'], plan['seg'], plan['local'], plan['total'], plan['tail_start'], plan['tail_rows'],
      plan['n_tiles'], h, rank_t, plan['local_col'], plan['local_row'], plan['seg_row'])


def _expert_kernel(te_ref, nt_ref, x_ref, wg_ref, wu_ref, wd_ref, y_ref, wg_s, wu_s, wd_s):
    i = pl.program_id(0)
    active = i < nt_ref[0]

    @pl.when((i == 0) | (te_ref[i] != te_ref[jnp.maximum(i - 1, 0)]))
    def _():
        wg_s[...] = wg_ref[...].astype(BF16)
        wu_s[...] = wu_ref[...].astype(BF16)
        wd_s[...] = wd_ref[...].astype(BF16)

    @pl.when(active)
    def _():
        x = x_ref[...]
        hid = _silu(_dot(x, wg_s[...])) * _dot(x, wu_s[...])
        y_ref[...] = _dot(hid.astype(BF16), wd_s[...]).astype(BF16)


def _experts(xs, plan, w_gate, w_up, w_down, l):
    rows = pl.BlockSpec((EXPERT_TILE, D_MODEL), lambda i, te, nt: (jnp.minimum(i, nt[0] - 1), 0))
    wspec = lambda shape: pl.BlockSpec((None, None) + shape, lambda i, te, nt: (l, te[i], 0, 0))
    return pl.pallas_call(
        _expert_kernel,
        grid_spec=pltpu.PrefetchScalarGridSpec(
            num_scalar_prefetch=2, grid=(MAX_TILES,),
            in_specs=[rows, wspec((D_MODEL, D_EXPERT)), wspec((D_MODEL, D_EXPERT)), wspec((D_EXPERT, D_MODEL))],
            out_specs=rows,
            scratch_shapes=[pltpu.VMEM((D_MODEL, D_EXPERT), BF16), pltpu.VMEM((D_MODEL, D_EXPERT), BF16),
                            pltpu.VMEM((D_EXPERT, D_MODEL), BF16)]),
        out_shape=jax.ShapeDtypeStruct((SORTED_ROWS, D_MODEL), BF16),
        input_output_aliases={2: 0},
        compiler_params=_params("arbitrary"),
        name="experts",
    )(plan['tile_expert'], plan['n_tiles'], xs, w_gate, w_up, w_down)


def _combine_kernel(*refs, final):
    refs = list(refs)
    dest_ref, seg_ref, local_ref, total_ref = refs[:4]
    x1_ref, ys_hbm, gate_ref, rank_ref, mod_ref, local_row_ref, local_col_ref, seg_col_ref = refs[4:12]
    rest = refs[12:]
    fg_ref = rest.pop(0) if final else None
    outs, (buf, sems) = rest[:-2], rest[-2:]
    b = pl.program_id(0)
    slot = b % 2

    def fetch(blk, s):
        for e in range(N_EXPERTS):
            n = pl.multiple_of(seg_ref[blk, e], SEG_ALIGN)
            src = pl.multiple_of(dest_ref[blk, e], SEG_ALIGN)
            dst = pl.multiple_of(local_ref[blk, e], SEG_ALIGN)
            pltpu.make_async_copy(ys_hbm.at[pl.ds(src, n)], buf.at[s, pl.ds(dst, n)], sems.at[s]).start()

    @pl.when(b == 0)
    def _():
        buf[...] = jnp.zeros_like(buf)
        fetch(0, 0)

    def wait_rows(blk, s):
        n_rows = pl.multiple_of(total_ref[blk], SEG_ALIGN)
        pltpu.make_async_copy(ys_hbm.at[pl.ds(0, n_rows)], buf.at[s, pl.ds(0, n_rows)], sems.at[s]).wait()

    wait_rows(b, slot)
    nxt = jnp.minimum(b + 1, N_BLK - 1)
    fetch(nxt, 1 - slot)

    gate = gate_ref[...].T
    hi, lo = _local_dest_digits(rank_ref[...].T, local_row_ref[:, :N_EXPERTS])
    lhs = jnp.concatenate([jnp.concatenate([hi, lo], axis=1),
                           jnp.concatenate([jnp.zeros_like(gate), gate], axis=1)], axis=0).astype(BF16)
    lstart = local_col_ref[...]
    lend = lstart + seg_col_ref[...]
    routed = jnp.zeros((ROW_TILE, D_MODEL), F32)
    r = lax.broadcasted_iota(jnp.int32, (N_EXPERTS, LOCAL_CHUNK), 1).astype(F32)
    rr = lax.broadcasted_iota(jnp.int32, (ROW_TILE, LOCAL_CHUNK), 1).astype(F32)
    def gather(c, acc):
        first = float(c * LOCAL_CHUNK)
        member = (r >= lstart - first) & (r < lend - first)
        table = jnp.concatenate([jnp.where(member, 64.0, 0.0), jnp.where(member, 1.0, 0.0)], axis=0).astype(BF16)
        dg = _dot(lhs, table)
        weights = jnp.where(dg[:ROW_TILE] == rr + first, dg[ROW_TILE:], 0.0).astype(BF16)
        return acc + _dot(weights, buf[slot, c * LOCAL_CHUNK:(c + 1) * LOCAL_CHUNK, :])

    for c in range(LOCAL_ROWS // LOCAL_CHUNK):
        routed = gather(c, routed)
    x = x1_ref[...] + mod_ref[:, 5 * D_MODEL:6 * D_MODEL] * routed
    if final:
        y = _rms(x, fg_ref[...])

        @pl.when(b < CTX_TILES)
        def _():
            outs[0][...] = y

        @pl.when(b >= CTX_TILES)
        def _():
            outs[1][...] = y
    else:
        outs[0][...] = x

    @pl.when(b == N_BLK - 1)
    def _():
        wait_rows(nxt, 1 - slot)


def _combine(x1, ys, gate_t, rank_t, mod, plan, final_g):
    final = final_g is not None
    blk = lambda shape, imap: pl.BlockSpec(shape, imap)
    xspec = blk((ROW_TILE, D_MODEL), lambda b, *_: (b, 0))
    if final:
        out_specs = _ctx_lat_specs(D_MODEL)
        out_shape = [jax.ShapeDtypeStruct((N_CTX, D_MODEL), F32), jax.ShapeDtypeStruct((N_LAT, D_MODEL), F32)]
    else:
        out_specs, out_shape = xspec, jax.ShapeDtypeStruct((N_TOK, D_MODEL), F32)
    et_spec = blk((N_EXPERTS, ROW_TILE), lambda b, *_: (0, b))
    in_specs = [xspec, pl.BlockSpec(memory_space=pl.ANY), et_spec, et_spec,
                blk((None, 1, 6 * D_MODEL), lambda b, *_: (_mod_row(b), 0, 0)),
                blk((None, 1, 128), lambda b, *_: (b, 0, 0)),
                blk((None, N_EXPERTS, 1), lambda b, *_: (b, 0, 0)),
                blk((None, N_EXPERTS, 1), lambda b, *_: (b, 0, 0))]
    args = [x1, ys, gate_t, rank_t, mod, plan['local_row'], plan['local_col'], plan['seg_col']]
    if final:
        in_specs.append(blk((1, D_MODEL), lambda b, *_: (0, 0)))
        args.append(final_g)
    return pl.pallas_call(
        functools.partial(_combine_kernel, final=final),
        grid_spec=pltpu.PrefetchScalarGridSpec(
            num_scalar_prefetch=4, grid=(N_BLK,),
            in_specs=in_specs, out_specs=out_specs,
            scratch_shapes=[pltpu.VMEM((2, LOCAL_ROWS, D_MODEL), BF16), pltpu.SemaphoreType.DMA((2,))]),
        out_shape=out_shape,
        compiler_params=_params("arbitrary"),
        name="combine",
    )(plan['dest'], plan['seg'], plan['local'], plan['total'], *args)


def _rope_full_tables(dim, n_rep):
    rows = DEC_SEQ // GRID_W
    r_idx, c_idx = np.meshgrid(np.arange(rows), np.arange(GRID_W), indexing='ij')
    pos = np.stack([r_idx.reshape(-1), c_idx.reshape(-1)], axis=-1).astype(np.float32)
    nf = dim // 4
    inv = np.float32(ROPE_BASE) ** (-np.arange(nf, dtype=np.float32) / np.float32(nf))
    ang = (pos[:, :, None] * inv).astype(np.float32)
    ang = np.repeat(ang.reshape(DEC_SEQ, 2 * nf), 2, axis=1)
    sign = np.tile(np.asarray([-1.0, 1.0], np.float32), dim // 2)
    cos = np.tile(np.cos(ang.astype(np.float64)), (1, n_rep))
    sin = np.tile(np.sin(ang.astype(np.float64)) * sign, (1, n_rep))
    return jnp.asarray(cos, F32), jnp.asarray(sin, F32)


def _pack_w_in(w):
    c0 = MLA_Q_LORA + MLA_KV_LORA
    kr = w[..., c0:MLA_IN]
    s0 = MLA_IN + HG_IN + FN_IN
    qh = [w[..., s0 + h * SWA_HD:s0 + (h + 1) * SWA_HD] for h in SWA_STACK_ORDER]
    return jnp.concatenate([w[..., :c0], kr, kr, kr, kr, w[..., MLA_IN:s0]] + qh
                           + [w[..., s0 + SWA_QW:]], axis=-1).astype(BF16)


def _pack_w_q_up(w):
    hd = MLA_NOPE + MLA_ROPE
    nope = [w[:, h * hd:h * hd + MLA_NOPE] for h in range(MLA_HEADS)]
    rope = [w[:, h * hd + MLA_NOPE:(h + 1) * hd] for h in range(MLA_HEADS)]
    return jnp.concatenate(nope + rope, axis=1).astype(BF16)


def _pack_w_kv_up(w):
    hd = MLA_NOPE + MLA_V
    kn = [w[:, h * hd:h * hd + MLA_NOPE] for h in range(MLA_HEADS)]
    vv = [w[:, h * hd + MLA_NOPE:(h + 1) * hd] for h in range(MLA_HEADS)]
    return jnp.concatenate(kn + vv, axis=1).astype(BF16)


def _pack_w_out(w):
    s0 = 3 * 256
    rows = [w[..., s0 + h * SWA_HD:s0 + (h + 1) * SWA_HD, :] for h in SWA_STACK_ORDER]
    return jnp.concatenate([w[..., :s0, :]] + rows, axis=-2).astype(BF16)


def kernel(x_prompt, x_sample, c, cache_mla_ckv, cache_mla_krope, cache_swa_k, cache_swa_v, state_hgrn,
           c_ctx, w_ada, b_ada, norm1_g, norm2_g, w_in, mla_q_norm_g, mla_w_q_up, mla_kv_norm_g, mla_w_kv_up,
           hg_lb_logits, hg_norm_g, fn_w, swa_sink, w_out, moe_w_router, moe_b_router, moe_w_gate, moe_w_up,
           moe_w_down, sh_w_gate, sh_w_up, sh_w_down, final_norm_g):
    x_parts = (x_prompt.reshape(N_CTX, D_MODEL), x_sample.reshape(N_LAT, D_MODEL))
    cv8 = jnp.concatenate([c_ctx[None, :], c, jnp.zeros((8 - 1 - DEC_BATCH, D_MODEL), F32)], axis=0)
    mods = _ada(cv8, w_ada, b_ada).reshape(DEPTH, 8, 1, 6 * D_MODEL)

    lb = jnp.cumsum(jax.nn.softmax(hg_lb_logits.astype(F32), axis=1), axis=1)
    lb = lb - lb[:, :1]

    cos_m, sin_m = _rope_full_tables(MLA_ROPE, MLA_HEADS)
    cos_q, sin_q = _rope_full_tables(SWA_HD, SWA_HEADS)
    cos_k, sin_k = cos_q[:, :SWA_KW], sin_q[:, :SWA_KW]
    cache_k = cache_swa_k.reshape(DEC_BATCH, DEPTH, PAST_LEN, SWA_KW)
    cache_v = cache_swa_v.reshape(DEC_BATCH, DEPTH, PAST_LEN, SWA_KW)
    state_t = jnp.swapaxes(state_hgrn, -1, -2)

    ctx_blk_lat = N_CTX // DEC_SEQ
    w_in_p, w_out_p = _pack_w_in(w_in), _pack_w_out(w_out)
    new_ckv, new_kr, new_k, new_v, new_st = [], [], [], [], []
    for l in range(DEPTH):
        u_mla, u_hg, u_fn, u_swa = _in_proj(x_parts, mods[l], norm1_g[l][None], w_in_p, l)

        qg, kvg = mla_q_norm_g[l][None], mla_kv_norm_g[l][None]
        wq, wkv = _pack_w_q_up(mla_w_q_up[l]), _pack_w_kv_up(mla_w_kv_up[l])
        o_mla_c, ckv_c = _mla_ctx(u_mla, qg, wq, kvg, wkv)
        o_mla_l = _mla_lat(u_mla, cache_mla_ckv, cache_mla_krope, l, cos_m, sin_m, qg, wq, kvg, wkv)

        lbf, lbb = lb[0, l][None], lb[1, l][None]
        ng4 = jnp.tile(hg_norm_g[l], HG_HEADS)[None]
        o_hg_c, st_c = _hgrn(u_hg, lbf, lbb, ng4, None, seq=SEQ, n_batch=BATCH, row_block0=0)
        o_hg_l = _hgrn(u_hg, lbf, lbb, ng4, state_t[:, l], seq=DEC_SEQ, n_batch=DEC_BATCH,
                       row_block0=ctx_blk_lat)

        fw = fn_w[l].astype(BF16)
        o_fn_c = _fourier(u_fn, fw, seq=SEQ, n_batch=BATCH, row_block0=0)
        o_fn_l = _fourier(u_fn, fw, seq=DEC_SEQ, n_batch=DEC_BATCH, row_block0=ctx_blk_lat)

        sink = swa_sink[l]
        o_swa_c = _swa_ctx(u_swa, sink)
        o_swa_l = _swa_lat(u_swa, cache_k, cache_v, l, sink, cos_q, sin_q, cos_k, sin_k)

        x1, h2, gate_t, rank_t, cnt = _out_proj(
            x_parts, ((o_mla_c, o_mla_l), (o_hg_c, o_hg_l), (o_fn_c, o_fn_l), (o_swa_c, o_swa_l)),
            mods[l], norm2_g[l][None], w_out_p, l,
            moe_w_router[l], moe_b_router[l][:, None],
            sh_w_gate[l].astype(BF16), sh_w_up[l].astype(BF16), sh_w_down[l].astype(BF16))
        plan = _segment_plan(cnt)
        xs = _dispatch(h2, rank_t, plan)
        ys = _experts(xs, plan, moe_w_gate, moe_w_up, moe_w_down, l)
        if l < DEPTH - 1:
            x_parts = (_combine(x1, ys, gate_t, rank_t, mods[l], plan, None),)
        else:
            y_prompt, y_sample = _combine(x1, ys, gate_t, rank_t, mods[l], plan, final_norm_g[None])

        new_ckv.append(ckv_c.reshape(BATCH, SEQ, MLA_KV_LORA))
        new_kr.append(u_mla[:N_CTX, MLA_Q_LORA + MLA_KV_LORA:MLA_IN].reshape(BATCH, SEQ, MLA_ROPE))
        new_k.append(u_swa[:N_CTX, SWA_QW:SWA_QW + SWA_KW].reshape(BATCH, SEQ, SWA_KV_HEADS, SWA_HD))
        new_v.append(u_swa[:N_CTX, SWA_QW + SWA_KW:].reshape(BATCH, SEQ, SWA_KV_HEADS, SWA_HD))
        new_st.append(jnp.swapaxes(st_c, -1, -2))

    y_prompt = y_prompt.reshape(BATCH, SEQ, D_MODEL)
    y_sample = y_sample.reshape(DEC_BATCH, DEC_SEQ, D_MODEL)
    stack = lambda xs: jnp.stack(xs, axis=1)
    return (y_prompt, y_sample, stack(new_ckv), stack(new_kr), stack(new_k), stack(new_v), stack(new_st))
```

```python
import functools

import numpy as np
import jax
import jax.numpy as jnp
from jax import lax
from jax.experimental import pallas as pl
from jax.experimental.pallas import tpu as pltpu

F32 = jnp.float32
BF16 = jnp.bfloat16

D_MODEL = 1024
BATCH = 32
SEQ = 256
DEPTH = 2
DEC_BATCH = 2
DEC_SEQ = 1024
PAST_LEN = 256
GRID_W = 64
EPS = 1e-6
ROPE_BASE = 10000.0
NEG_INF = -1e30

MLA_HEADS = 4
MLA_NOPE = 64
MLA_ROPE = 32
MLA_V = 64
MLA_Q_LORA = 256
MLA_KV_LORA = 128
HG_HEADS = 4
HG_DK = 64
HG_DV = 64
HG_W = HG_HEADS * HG_DK
FN_GROUPS = 4
FN_WIDTH = 256
SWA_HEADS = 4
SWA_KV_HEADS = 2
SWA_HD = 64
WINDOW = 128
N_EXPERTS = 64
TOP_K = 6
D_EXPERT = 256
D_SHARED = 256
ROUTE_SCALE = 2.5

MLA_IN = MLA_Q_LORA + MLA_KV_LORA + MLA_ROPE
HG_IN = 3 * HG_HEADS * HG_DK + 2 * HG_HEADS * HG_DV
FN_IN = FN_WIDTH
SWA_IN = (SWA_HEADS + 2 * SWA_KV_HEADS) * SWA_HD

N_CTX = BATCH * SEQ
N_LAT = DEC_BATCH * DEC_SEQ
N_TOK = N_CTX + N_LAT

MLA_PACK = 512
U_COLS = MLA_PACK + HG_IN + FN_IN + SWA_IN

ROW_TILE = 256
CTX_TILES = N_CTX // ROW_TILE
LAT_TILES_PER_BATCH = DEC_SEQ // ROW_TILE
HG_CHUNK = 32
HG_BLOCK = 256
SWA_QBLK = 128
MLA_QBLK = 256
CTX_PER_STEP = 4
SWA_CTX_PER_STEP = 8
HG_CTX_PER_STEP = 4
OUT_ROWS = 2 * ROW_TILE
VMEM_LIMIT = 56 * 1024 * 1024


def _dot(a, b):
    return jnp.dot(a, b, preferred_element_type=F32)


def _dot_nt(a, b):
    return lax.dot_general(a, b, (((1,), (1,)), ((), ())), preferred_element_type=F32)


def _dot_tn(a, b):
    return lax.dot_general(a, b, (((0,), (0,)), ((), ())), preferred_element_type=F32)


def _split3(x):
    hi = x.astype(BF16)
    r1 = x - hi.astype(F32)
    mid = r1.astype(BF16)
    return hi, mid, (r1 - mid.astype(F32)).astype(BF16)


def _dot_exact_lhs(a, b):
    ab = a.astype(BF16)
    hi, mid, lo = _split3(b)
    return (_dot(ab, lo) + _dot(ab, mid)) + _dot(ab, hi)


def _dot_exact_rhs(a, b):
    bb = b.astype(BF16)
    hi, mid, lo = _split3(a)
    return (_dot(lo, bb) + _dot(mid, bb)) + _dot(hi, bb)


def _rms(x, g):
    return x * lax.rsqrt(jnp.mean(x * x, axis=-1, keepdims=True) + EPS) * g


def _silu(x):
    return x * jax.nn.sigmoid(x)


def _mod_row(i):
    return jnp.where(i < CTX_TILES, 0, 1 + (i - CTX_TILES) // LAT_TILES_PER_BATCH)


def _params(*sem):
    return pltpu.CompilerParams(dimension_semantics=sem, vmem_limit_bytes=VMEM_LIMIT)


ADA_COLS = 1536


def _ada_kernel(cv_ref, w_ref, b_ref, o_ref):
    a = _silu(cv_ref[...]).astype(BF16)
    o_ref[...] = _dot(a, w_ref[...].astype(BF16)) + b_ref[...]


def _ada(cv8, w_ada, b_ada):
    return pl.pallas_call(
        _ada_kernel,
        grid=(DEPTH, 6 * D_MODEL // ADA_COLS),
        in_specs=[
            pl.BlockSpec((8, D_MODEL), lambda l, j: (0, 0)),
            pl.BlockSpec((None, D_MODEL, ADA_COLS), lambda l, j: (l, 0, j)),
            pl.BlockSpec((None, 1, ADA_COLS), lambda l, j: (l, 0, j)),
        ],
        out_specs=pl.BlockSpec((None, 8, ADA_COLS), lambda l, j: (l, 0, j)),
        out_shape=jax.ShapeDtypeStruct((DEPTH, 8, 6 * D_MODEL), F32),
        compiler_params=_params("arbitrary", "arbitrary"),
        name="ada",
    )(cv8, w_ada, b_ada.reshape(DEPTH, 1, 6 * D_MODEL))


def _ctx_lat_specs(width, tile=ROW_TILE):
    n_ctx = N_CTX // tile
    return [pl.BlockSpec((tile, width), lambda i, *_: (jnp.minimum(i, n_ctx - 1), 0)),
            pl.BlockSpec((tile, width), lambda i, *_: (jnp.maximum(i - n_ctx, 0), 0))]


def _row_specs(parts, width, tile=ROW_TILE):
    if len(parts) == 2:
        return _ctx_lat_specs(width, tile)
    return [pl.BlockSpec((tile, width), lambda i, *_: (i, 0))]


def _read_rows(refs, tile=ROW_TILE):
    if len(refs) == 1:
        return refs[0][...]
    return jnp.where(pl.program_id(0) < N_CTX // tile, refs[0][...], refs[1][...])


def _in_kernel(*refs, n_x):
    x = _read_rows(refs[:n_x], OUT_ROWS)
    mod_ref, g_ref, w_ref, umla_ref, uhg_ref, ufn_ref, uswa_ref = refs[n_x:]
    sh1 = mod_ref[:, 0:D_MODEL]
    sc1 = mod_ref[:, D_MODEL:2 * D_MODEL]
    h = _rms(x, g_ref[...]) * (1.0 + sc1) + sh1
    u = _dot(h.astype(BF16), w_ref[...])
    o = 0
    for ref, width in ((umla_ref, MLA_PACK), (uhg_ref, HG_IN), (ufn_ref, FN_IN), (uswa_ref, SWA_IN)):
        ref[...] = u[:, o:o + width].astype(ref.dtype)
        o += width


def _in_proj(x_parts, mod, g, w, l):
    row = lambda i: (i, 0)
    widths = (MLA_PACK, HG_IN, FN_IN, SWA_IN)
    return pl.pallas_call(
        functools.partial(_in_kernel, n_x=len(x_parts)),
        grid=(N_TOK // OUT_ROWS,),
        in_specs=_row_specs(x_parts, D_MODEL, OUT_ROWS) + [
            pl.BlockSpec((None, 1, 6 * D_MODEL), lambda i: (_mod_row(i * (OUT_ROWS // ROW_TILE)), 0, 0)),
            pl.BlockSpec((1, D_MODEL), lambda i: (0, 0)),
            pl.BlockSpec((None, D_MODEL, U_COLS), lambda i: (l, 0, 0))],
        out_specs=[pl.BlockSpec((OUT_ROWS, wd), row) for wd in widths],
        out_shape=[jax.ShapeDtypeStruct((N_TOK, wd), F32) for wd in widths],
        compiler_params=_params("arbitrary"),
        name="in_proj",
    )(*x_parts, mod, g, w)


def _rope(x, cos, sin_signed):
    lane = lax.broadcasted_iota(jnp.int32, x.shape, 1)
    width = x.shape[1]
    swapped = jnp.where(lane % 2 == 0, pltpu.roll(x, width - 1, 1), pltpu.roll(x, 1, 1))
    return x * cos + swapped * sin_signed


def _stack_heads(x, n_heads, head_w):
    lane = lax.broadcasted_iota(jnp.int32, x.shape, 1)
    return jnp.concatenate([jnp.where(lane // head_w == h, x, 0.0) for h in range(n_heads)], axis=0)


def _unstack_heads(o, n_heads, head_w):
    t = o.shape[0] // n_heads
    lane = lax.broadcasted_iota(jnp.int32, (t, o.shape[1]), 1)
    out = jnp.zeros((t, o.shape[1]), F32)
    for h in range(n_heads):
        out = jnp.where(lane // head_w == h, o[h * t:(h + 1) * t], out)
    return out


MLA_SCALE = (MLA_NOPE + MLA_ROPE) ** -0.5
MLA_QW = MLA_HEADS * MLA_NOPE + MLA_HEADS * MLA_ROPE
MLA_NW = MLA_HEADS * MLA_NOPE


def _mla_attend(q, kcat, v):
    qs = jnp.concatenate([_stack_heads(q[:, :MLA_NW], MLA_HEADS, MLA_NOPE),
                          _stack_heads(q[:, MLA_NW:], MLA_HEADS, MLA_ROPE)], axis=1)
    s = _dot_nt(qs.astype(BF16), kcat) * MLA_SCALE
    p = jnp.exp(s - jnp.max(s, axis=-1, keepdims=True))
    o = _dot(p.astype(BF16), v) / jnp.sum(p, axis=-1, keepdims=True)
    return _unstack_heads(o, MLA_HEADS, MLA_V)


def _mla_ctx_kernel(u_ref, qg_ref, wq_ref, kvg_ref, wkv_ref, o_ref, ckv_ref):
    for j in range(CTX_PER_STEP):
        rows = slice(j * SEQ, (j + 1) * SEQ)
        u = u_ref[rows, :]
        q = _dot(_rms(u[:, :MLA_Q_LORA], qg_ref[...]).astype(BF16), wq_ref[...])
        ckv = _rms(u[:, MLA_Q_LORA:MLA_Q_LORA + MLA_KV_LORA], kvg_ref[...])
        ckv_ref[rows, :] = ckv
        kv = _dot(ckv.astype(BF16), wkv_ref[...])
        kr4 = u[:, MLA_Q_LORA + MLA_KV_LORA:]
        kcat = jnp.concatenate([kv[:, :MLA_NW], kr4], axis=1).astype(BF16)
        o_ref[rows, :] = _mla_attend(q, kcat, kv[:, MLA_NW:].astype(BF16)).astype(o_ref.dtype)


def _mla_ctx(u_mla, qg, wq, kvg, wkv):
    full = lambda shape: pl.BlockSpec(shape, lambda b: (0, 0))
    rows = CTX_PER_STEP * SEQ
    return pl.pallas_call(
        _mla_ctx_kernel,
        grid=(BATCH // CTX_PER_STEP,),
        in_specs=[pl.BlockSpec((rows, MLA_PACK), lambda b: (b, 0)),
                  full((1, MLA_Q_LORA)), full((MLA_Q_LORA, MLA_QW)),
                  full((1, MLA_KV_LORA)), full((MLA_KV_LORA, 2 * MLA_NW))],
        out_specs=[pl.BlockSpec((rows, MLA_NW), lambda b: (b, 0)),
                   pl.BlockSpec((rows, MLA_KV_LORA), lambda b: (b, 0))],
        out_shape=[jax.ShapeDtypeStruct((N_CTX, MLA_NW), BF16),
                   jax.ShapeDtypeStruct((N_CTX, MLA_KV_LORA), F32)],
        compiler_params=_params("arbitrary"),
        name="mla_ctx",
    )(u_mla, qg, wq, kvg, wkv)


MLA_TK = PAST_LEN + DEC_SEQ


def _mla_lat_kernel(u_ref, cckv_ref, ckr_ref, cos_ref, sin_ref, qg_ref, wq_ref, kvg_ref, wkv_ref,
                    o_ref, kcat_s, v_s):
    i = pl.program_id(1)

    @pl.when(i == 0)
    def _():
        u = u_ref[...]
        ckv_new = _rms(u[:, MLA_Q_LORA:MLA_Q_LORA + MLA_KV_LORA], kvg_ref[...])
        ckv_all = jnp.concatenate([cckv_ref[...], ckv_new], axis=0)
        kv = _dot(ckv_all.astype(BF16), wkv_ref[...])
        kr_new = _rope(u[:, MLA_Q_LORA + MLA_KV_LORA:], cos_ref[...], sin_ref[...])
        ckr = ckr_ref[...]
        kr_all = jnp.concatenate([jnp.concatenate([ckr] * MLA_HEADS, axis=1), kr_new], axis=0)
        kcat_s[...] = jnp.concatenate([kv[:, :MLA_NW], kr_all], axis=1).astype(BF16)
        v_s[...] = kv[:, MLA_NW:].astype(BF16)

    r0 = pl.multiple_of(i * MLA_QBLK, MLA_QBLK)
    cq = u_ref[pl.ds(r0, MLA_QBLK), 0:MLA_Q_LORA]
    q = _dot(_rms(cq, qg_ref[...]).astype(BF16), wq_ref[...])
    qr = _rope(q[:, MLA_NW:], cos_ref[pl.ds(r0, MLA_QBLK), :], sin_ref[pl.ds(r0, MLA_QBLK), :])
    q = jnp.concatenate([q[:, :MLA_NW], qr], axis=1)
    o_ref[...] = _mla_attend(q, kcat_s[...], v_s[...]).astype(o_ref.dtype)


def _mla_lat(u_mla, cache_ckv, cache_kr, l, cos, sin, qg, wq, kvg, wkv):
    full = lambda shape: pl.BlockSpec(shape, lambda b, i: (0, 0))
    nq = DEC_SEQ // MLA_QBLK
    return pl.pallas_call(
        _mla_lat_kernel,
        grid=(DEC_BATCH, nq),
        in_specs=[pl.BlockSpec((DEC_SEQ, MLA_PACK), lambda b, i: (N_CTX // DEC_SEQ + b, 0)),
                  pl.BlockSpec((None, None, PAST_LEN, MLA_KV_LORA), lambda b, i: (b, l, 0, 0)),
                  pl.BlockSpec((None, None, PAST_LEN, MLA_ROPE), lambda b, i: (b, l, 0, 0)),
                  full((DEC_SEQ, MLA_HEADS * MLA_ROPE)), full((DEC_SEQ, MLA_HEADS * MLA_ROPE)),
                  full((1, MLA_Q_LORA)), full((MLA_Q_LORA, MLA_QW)),
                  full((1, MLA_KV_LORA)), full((MLA_KV_LORA, 2 * MLA_NW))],
        out_specs=pl.BlockSpec((MLA_QBLK, MLA_NW), lambda b, i: (b * nq + i, 0)),
        out_shape=jax.ShapeDtypeStruct((N_LAT, MLA_NW), BF16),
        scratch_shapes=[pltpu.VMEM((MLA_TK, MLA_QW), BF16), pltpu.VMEM((MLA_TK, MLA_NW), BF16)],
        compiler_params=_params("arbitrary", "arbitrary"),
        name="mla_lat",
    )(u_mla, cache_ckv, cache_kr, cos, sin, qg, wq, kvg, wkv)


def _hgrn_kernel(*refs, seq, n_seq, has_state):
    if has_state:
        (u_ref, lbf_ref, lbb_ref, ng_ref, s0_ref, o_ref,
         q_s, kf_s, gf_s, kb_s, gb_s, of_s, ob_s, stf_s, stb_s) = refs
    else:
        (u_ref, lbf_ref, lbb_ref, ng_ref, o_ref, so_ref,
         q_s, kf_s, gf_s, kb_s, gb_s, of_s, ob_s, stf_s, stb_s) = refs
    C = HG_CHUNK
    W = HG_W

    q_s[...] = _silu(u_ref[:, 0:W])
    ff = lbf_ref[...] + (1.0 - lbf_ref[...]) * jax.nn.sigmoid(u_ref[:, W:2 * W])
    kf_s[...] = 1.0 - ff
    gf_s[...] = jnp.log(ff)
    fb = lbb_ref[...] + (1.0 - lbb_ref[...]) * jax.nn.sigmoid(u_ref[:, 2 * W:3 * W])
    kb_s[...] = 1.0 - fb
    gb_s[...] = jnp.log(fb)

    rr = lax.broadcasted_iota(jnp.int32, (W, W), 0)
    cc = lax.broadcasted_iota(jnp.int32, (W, W), 1)
    blockdiag = rr // HG_DK == cc // HG_DK
    if has_state:
        for st, d in ((stf_s, 0), (stb_s, 1)):
            rows = []
            for h in range(HG_HEADS):
                z = lambda n: jnp.zeros((HG_DV, n * HG_DK), F32)
                parts = ([z(h)] if h else []) + [s0_ref[d, h]] + ([z(HG_HEADS - 1 - h)] if h < HG_HEADS - 1 else [])
                rows.append(jnp.concatenate(parts, axis=1) if len(parts) > 1 else parts[0])
            st[0] = jnp.concatenate(rows, axis=0)
    else:
        stf_s[...] = jnp.zeros_like(stf_s)
        stb_s[...] = jnp.zeros_like(stb_s)

    B = HG_BLOCK
    per_block = B // C
    n_blocks = seq // B
    ri = lax.broadcasted_iota(jnp.int32, (B, B), 0)
    ci = lax.broadcasted_iota(jnp.int32, (B, B), 1)
    same_chunk = ri // C == ci // C
    rs = lax.broadcasted_iota(jnp.int32, (HG_HEADS * B, B), 0) % B
    cs = lax.broadcasted_iota(jnp.int32, (HG_HEADS * B, B), 1)
    same_chunk_s = rs // C == cs // C

    sums_f = jnp.where(same_chunk & (ci <= ri), 1.0, 0.0)
    sums_b = jnp.where(same_chunk & (ci >= ri), 1.0, 0.0)
    keep_f = same_chunk_s & (rs >= cs)
    keep_b = same_chunk_s & (cs >= rs)

    def chunk_row(x, i):
        x3 = x.reshape(per_block, C, W)
        return jnp.broadcast_to(x3[:, i:i + 1, :], (per_block, C, W)).reshape(B, W)

    def block(r, k_s, g_s, o_s, st_s, sum_mat, keep, order, mid, far):
        q = q_s[pl.ds(r, B), :]
        k = k_s[pl.ds(r, B), :]
        v = u_ref[pl.ds(r, B), 3 * W:4 * W].astype(BF16)
        G = _dot_exact_lhs(sum_mat, g_s[pl.ds(r, B), :])
        Gq = G - chunk_row(G, mid)
        Gk2 = chunk_row(G, far) - G
        qe = _stack_heads(q * jnp.exp(Gq), HG_HEADS, HG_DK)
        ke = k * jnp.exp(-Gq)
        A = jnp.where(keep, _dot_nt(qe.astype(BF16), ke.astype(BF16)), 0.0)
        o_intra = _unstack_heads(_dot(A.astype(BF16), v), HG_HEADS, HG_DV)
        qg = (q * jnp.exp(G)).astype(BF16)
        k2 = (k * jnp.exp(Gk2)).astype(BF16)
        decay = jnp.exp(G + Gk2)
        st = st_s[...]
        o_inter = [None] * per_block
        for c in order:
            rows = slice(c * C, (c + 1) * C)
            o_inter[c] = _dot_nt(st.astype(BF16), qg[rows])
            st = st * decay[c * C:c * C + 1] + jnp.where(blockdiag, _dot_tn(v[rows], k2[rows]), 0.0)
        st_s[...] = st
        o_s[pl.ds(r, B), :] = o_intra + jnp.concatenate(o_inter, axis=1).T

    def fwd(j, r):
        block(r, kf_s, gf_s, of_s, stf_s.at[j], sums_f, keep_f, range(per_block), C // 2 - 1, C - 1)

    def bwd(j, r):
        block(r, kb_s, gb_s, ob_s, stb_s.at[j], sums_b, keep_b, range(per_block - 1, -1, -1), C // 2, 0)

    for j in range(n_seq):
        if n_blocks == 1:
            fwd(j, j * seq)
            bwd(j, j * seq)
        else:
            def both_directions(i, carry, j=j):
                fwd(j, pl.multiple_of(j * seq + i * B, B))
                bwd(j, pl.multiple_of(j * seq + (n_blocks - 1 - i) * B, B))
                return carry

            lax.fori_loop(0, n_blocks, both_directions, 0)

    o = of_s[...] + ob_s[...]
    ms = _dot_exact_rhs(o * o, jnp.where(blockdiag, 1.0 / HG_DV, 0.0))
    on = o * lax.rsqrt(ms + EPS) * ng_ref[...]
    o_ref[...] = (on * _silu(u_ref[:, 4 * W:5 * W])).astype(o_ref.dtype)

    if not has_state:
        for j in range(n_seq):
            for st, d in ((stf_s, 0), (stb_s, 1)):
                for h in range(HG_HEADS):
                    so_ref[j, d, h] = st[j, h * HG_DV:(h + 1) * HG_DV, h * HG_DK:(h + 1) * HG_DK]


def _hgrn(u_hg, lbf, lbb, ng4, state_t, *, seq, n_batch, row_block0):
    has_state = state_t is not None
    n_seq = 1 if has_state else HG_CTX_PER_STEP
    rows = n_seq * seq
    full = lambda shape: pl.BlockSpec(shape, lambda b: (0, 0))
    in_specs = [pl.BlockSpec((rows, HG_IN), lambda b: (row_block0 + b, 0)),
                full((1, HG_W)), full((1, HG_W)), full((1, HG_W))]
    args = [u_hg, lbf, lbb, ng4]
    o_spec = pl.BlockSpec((rows, HG_W), lambda b: (b, 0))
    o_shape = jax.ShapeDtypeStruct((n_batch * seq, HG_W), BF16)
    if has_state:
        in_specs.append(pl.BlockSpec((None, 2, HG_HEADS, HG_DV, HG_DK), lambda b: (b, 0, 0, 0, 0)))
        args.append(state_t)
        out_specs, out_shape = o_spec, o_shape
    else:
        out_specs = [o_spec, pl.BlockSpec((n_seq, 2, HG_HEADS, HG_DV, HG_DK), lambda b: (b, 0, 0, 0, 0))]
        out_shape = [o_shape, jax.ShapeDtypeStruct((n_batch, 2, HG_HEADS, HG_DV, HG_DK), F32)]
    return pl.pallas_call(
        functools.partial(_hgrn_kernel, seq=seq, n_seq=n_seq, has_state=has_state),
        grid=(n_batch // n_seq,),
        in_specs=in_specs, out_specs=out_specs, out_shape=out_shape,
        scratch_shapes=[pltpu.VMEM((rows, HG_W), F32)] * 7 + [pltpu.VMEM((n_seq, HG_W, HG_W), F32)] * 2,
        compiler_params=_params("arbitrary"),
        name="hgrn_lat" if has_state else "hgrn_ctx",
    )(*args)


def _dft_tables(n):
    j = np.arange(n, dtype=np.int64)
    ang = 2.0 * np.pi * ((j[:, None] * j[None, :]) % n).astype(np.float64) / n
    return np.cos(ang) / np.sqrt(n), np.sin(ang) / np.sqrt(n)


def _fourier_tables(seq):
    gw = FN_WIDTH // FN_GROUPS
    cg, sg = _dft_tables(gw)
    eye = np.eye(FN_GROUPS)
    chan = np.concatenate([np.kron(eye, cg), np.kron(eye, sg)], axis=1)
    ct, st = _dft_tables(seq)
    pos = np.concatenate([ct, -st], axis=1)
    return jnp.asarray(chan, F32).astype(BF16), jnp.asarray(pos, F32).astype(BF16)


def _fourier_kernel(x_ref, chan_ref, pos_ref, w_ref, o_ref, *, seq, n_seq):
    for j in range(n_seq):
        rows = slice(j * seq, (j + 1) * seq)
        x12 = _dot(x_ref[rows, :].astype(BF16), chan_ref[...])
        z = jnp.concatenate([x12[:, :FN_WIDTH], x12[:, FN_WIDTH:]], axis=0).astype(BF16)
        y = _dot(pos_ref[...], z)
        o_ref[rows, :] = _dot(y.astype(BF16), w_ref[...]).astype(o_ref.dtype)


def _fourier(u_fn, w, *, seq, n_batch, row_block0):
    chan, pos = _fourier_tables(seq)
    full = lambda shape: pl.BlockSpec(shape, lambda b: (0, 0))
    n_seq = CTX_PER_STEP if seq == SEQ else 1
    return pl.pallas_call(
        functools.partial(_fourier_kernel, seq=seq, n_seq=n_seq),
        grid=(n_batch // n_seq,),
        in_specs=[pl.BlockSpec((n_seq * seq, FN_WIDTH), lambda b: (row_block0 + b, 0)),
                  full((FN_WIDTH, 2 * FN_WIDTH)), full((seq, 2 * seq)), full((FN_WIDTH, FN_WIDTH))],
        out_specs=pl.BlockSpec((n_seq * seq, FN_WIDTH), lambda b: (b, 0)),
        out_shape=jax.ShapeDtypeStruct((n_batch * seq, FN_WIDTH), BF16),
        compiler_params=_params("arbitrary"),
        name="fourier",
    )(u_fn, chan, pos, w)


SWA_SCALE = SWA_HD ** -0.5
SWA_QW = SWA_HEADS * SWA_HD
SWA_KW = SWA_KV_HEADS * SWA_HD
SWA_STACK_ORDER = (0, 2, 1, 3)


def _swa_stack_q(q):
    return jnp.concatenate([_stack_heads(q[:, :SWA_KW], SWA_KV_HEADS, SWA_HD),
                            _stack_heads(q[:, SWA_KW:], SWA_KV_HEADS, SWA_HD)], axis=0)


def _swa_unstack_o(o):
    t = o.shape[0] // SWA_HEADS
    return jnp.concatenate([_unstack_heads(o[:2 * t], SWA_KV_HEADS, SWA_HD),
                            _unstack_heads(o[2 * t:], SWA_KV_HEADS, SWA_HD)], axis=1)


def _sink_rows(sink_ref, t):
    return jnp.concatenate([jnp.full((t, 1), sink_ref[h], F32) for h in SWA_STACK_ORDER], axis=0)


def _swa_ctx_kernel(sink_ref, u_ref, o_ref):
    sink = _sink_rows(sink_ref, SEQ)
    for j in range(SWA_CTX_PER_STEP):
        rows = slice(j * SEQ, (j + 1) * SEQ)
        u = u_ref[rows, :]
        qs = _swa_stack_q(u[:, :SWA_QW]).astype(BF16)
        k = u[:, SWA_QW:SWA_QW + SWA_KW].astype(BF16)
        v = u[:, SWA_QW + SWA_KW:].astype(BF16)
        s = _dot_nt(qs, k) * SWA_SCALE
        m = jnp.maximum(jnp.max(s, axis=-1, keepdims=True), sink)
        p = jnp.exp(s - m)
        denom = jnp.sum(p, axis=-1, keepdims=True) + jnp.exp(sink - m)
        o_ref[rows, :] = _swa_unstack_o(_dot(p.astype(BF16), v) / denom).astype(o_ref.dtype)


def _swa_ctx(u_swa, sink):
    rows = SWA_CTX_PER_STEP * SEQ
    return pl.pallas_call(
        _swa_ctx_kernel,
        grid=(BATCH // SWA_CTX_PER_STEP,),
        in_specs=[pl.BlockSpec(memory_space=pltpu.SMEM),
                  pl.BlockSpec((rows, SWA_IN), lambda b: (b, 0))],
        out_specs=pl.BlockSpec((rows, SWA_QW), lambda b: (b, 0)),
        out_shape=jax.ShapeDtypeStruct((N_CTX, SWA_QW), BF16),
        compiler_params=_params("arbitrary"),
        name="swa_ctx",
    )(sink, u_swa)


SWA_PAD = DEC_SEQ + 2 * SWA_QBLK


def _swa_lat_kernel(sink_ref, u_ref, kc_ref, vc_ref, cosq_ref, sinq_ref, cosk_ref, sin_k_ref,
                    o_ref, k_s, v_s):
    i = pl.program_id(1)
    B = SWA_QBLK

    @pl.when(i == 0)
    def _():
        zeros = jnp.zeros((B, SWA_KW), BF16)
        k = _rope(u_ref[:, SWA_QW:SWA_QW + SWA_KW], cosk_ref[...], sin_k_ref[...]).astype(BF16)
        k_s[...] = jnp.concatenate([zeros, k, zeros], axis=0)
        v_s[...] = jnp.concatenate([zeros, u_ref[:, SWA_QW + SWA_KW:].astype(BF16), zeros], axis=0)

    r0 = pl.multiple_of(i * B, B)
    q = _rope(u_ref[pl.ds(r0, B), 0:SWA_QW], cosq_ref[pl.ds(r0, B), :], sinq_ref[pl.ds(r0, B), :])
    qs = _swa_stack_q(q).astype(BF16)
    s_loc = _dot_nt(qs, k_s[pl.ds(r0, 3 * B), :]) * SWA_SCALE
    row = lax.broadcasted_iota(jnp.int32, s_loc.shape, 0) % B
    col = lax.broadcasted_iota(jnp.int32, s_loc.shape, 1)
    kpos = r0 - B + col
    valid = (jnp.abs(row + B - col) <= WINDOW) & (kpos >= 0) & (kpos < DEC_SEQ)
    s_loc = jnp.where(valid, s_loc, NEG_INF)
    s_ctx = _dot_nt(qs, kc_ref[...].astype(BF16)) * SWA_SCALE
    sink = _sink_rows(sink_ref, B)
    m = jnp.maximum(jnp.maximum(jnp.max(s_loc, axis=-1, keepdims=True),
                                jnp.max(s_ctx, axis=-1, keepdims=True)), sink)
    p_loc = jnp.exp(s_loc - m)
    p_ctx = jnp.exp(s_ctx - m)
    denom = (jnp.sum(p_loc, axis=-1, keepdims=True) + jnp.sum(p_ctx, axis=-1, keepdims=True)
             + jnp.exp(sink - m))
    o = _dot(p_loc.astype(BF16), v_s[pl.ds(r0, 3 * B), :]) + _dot(p_ctx.astype(BF16), vc_ref[...].astype(BF16))
    o_ref[...] = _swa_unstack_o(o / denom).astype(o_ref.dtype)


def _swa_lat(u_swa, cache_k, cache_v, l, sink, cosq, sinq, cosk, sink_k):
    full = lambda shape: pl.BlockSpec(shape, lambda b, i: (0, 0))
    nq = DEC_SEQ // SWA_QBLK
    cache_spec = pl.BlockSpec((None, None, PAST_LEN, SWA_KW), lambda b, i: (b, l, 0, 0))
    return pl.pallas_call(
        _swa_lat_kernel,
        grid=(DEC_BATCH, nq),
        in_specs=[pl.BlockSpec(memory_space=pltpu.SMEM),
                  pl.BlockSpec((DEC_SEQ, SWA_IN), lambda b, i: (N_CTX // DEC_SEQ + b, 0)),
                  cache_spec, cache_spec,
                  full((DEC_SEQ, SWA_QW)), full((DEC_SEQ, SWA_QW)),
                  full((DEC_SEQ, SWA_KW)), full((DEC_SEQ, SWA_KW))],
        out_specs=pl.BlockSpec((SWA_QBLK, SWA_QW), lambda b, i: (b * nq + i, 0)),
        out_shape=jax.ShapeDtypeStruct((N_LAT, SWA_QW), BF16),
        scratch_shapes=[pltpu.VMEM((SWA_PAD, SWA_KW), BF16), pltpu.VMEM((SWA_PAD, SWA_KW), BF16)],
        compiler_params=_params("arbitrary", "arbitrary"),
        name="swa_lat",
    )(sink, u_swa, cache_k, cache_v, cosq, sinq, cosk, sink_k)


N_BLK = N_TOK // ROW_TILE
SEG_ALIGN = 16
LOCAL_ROWS = ROW_TILE * TOP_K + N_EXPERTS * SEG_ALIGN
LOCAL_CHUNK = 512
EXPERT_TILE = 768
SORTED_ROWS = -(-(N_TOK * TOP_K + N_BLK * N_EXPERTS * SEG_ALIGN + N_EXPERTS * (EXPERT_TILE + SEG_ALIGN))
                // EXPERT_TILE) * EXPERT_TILE
MAX_TILES = SORTED_ROWS // EXPERT_TILE
NOT_PICKED = -1.0
NO_DEST = 4095.0


def _out_kernel(*refs, n_x):
    x_all = _read_rows(refs[:n_x], OUT_ROWS)
    mix_all = [_read_rows(refs[n_x + 2 * j:n_x + 2 * j + 2], OUT_ROWS) for j in range(4)]
    out_refs = refs[n_x + 8:]
    for blk in range(OUT_ROWS // ROW_TILE):
        rows = slice(blk * ROW_TILE, (blk + 1) * ROW_TILE)
        _out_block(x_all[rows], [m[rows] for m in mix_all], out_refs, blk)


def _out_block(x, mixers, refs, blk):
    (mod_ref, g_ref, wo_ref, wr_ref, br_ref, wsg_ref, wsu_ref, wsd_ref,
     x1_ref, h_ref, gate_ref, rank_ref, cnt_ref) = refs
    rows = slice(blk * ROW_TILE, (blk + 1) * ROW_TILE)
    mix = jnp.zeros((ROW_TILE, D_MODEL), F32)
    for j in range(4):
        mix = mix + _dot(mixers[j], wo_ref[j * 256:(j + 1) * 256, :])
    g1 = mod_ref[:, 2 * D_MODEL:3 * D_MODEL]
    sh2 = mod_ref[:, 3 * D_MODEL:4 * D_MODEL]
    sc2 = mod_ref[:, 4 * D_MODEL:5 * D_MODEL]
    g2 = mod_ref[:, 5 * D_MODEL:6 * D_MODEL]
    x1 = x + g1 * mix
    h = _rms(x1, g_ref[...]) * (1.0 + sc2) + sh2
    hb = h.astype(BF16)
    h_ref[rows, :] = hb

    w_hi = wr_ref[...].astype(BF16)
    w_lo = (wr_ref[...] - w_hi.astype(F32)).astype(BF16)
    h_lo = (h - hb.astype(F32)).astype(BF16)
    logits = ((_dot(h_lo, w_hi) + _dot(hb, w_lo)) + _dot(hb, w_hi)).T
    scores = jax.nn.sigmoid(logits)
    sel = scores + br_ref[...]
    eidx = lax.broadcasted_iota(jnp.int32, sel.shape, 0)
    gate = jnp.zeros_like(scores)
    picked = jnp.zeros_like(scores)
    for _ in range(TOP_K):
        best = jnp.max(sel, axis=0, keepdims=True)
        first = jnp.min(jnp.where(sel == best, eidx, N_EXPERTS), axis=0, keepdims=True)
        pick = eidx == first
        gate = jnp.where(pick, scores, gate)
        picked = jnp.where(pick, 1.0, picked)
        sel = jnp.where(pick, -jnp.inf, sel)
    gate = ROUTE_SCALE * gate / jnp.sum(gate, axis=0, keepdims=True)

    ti = lax.broadcasted_iota(jnp.int32, (ROW_TILE, ROW_TILE), 0)
    tj = lax.broadcasted_iota(jnp.int32, (ROW_TILE, ROW_TILE), 1)
    pb = picked.astype(BF16)
    rank = _dot(pb, jnp.where(ti < tj, 1.0, 0.0).astype(BF16))
    gate_ref[:, rows] = gate
    rank_ref[:, rows] = jnp.where(picked > 0.0, rank, NOT_PICKED)
    counts = _dot_nt(jnp.ones((8, ROW_TILE), BF16), pb)
    cnt_ref[blk] = jnp.concatenate([counts, jnp.zeros_like(counts)], axis=1)

    hid = _silu(_dot(hb, wsg_ref[...])) * _dot(hb, wsu_ref[...])
    x1_ref[rows, :] = x1 + g2 * _dot(hid.astype(BF16), wsd_ref[...])


def _out_proj(x_parts, mixer_pairs, mod, g, wo, l, wr, br, wsg, wsu, wsd):
    row = lambda i: (i, 0)
    col = lambda i: (0, i)
    full = lambda shape: pl.BlockSpec(shape, lambda i: (0, 0))
    per_step = OUT_ROWS // ROW_TILE
    et_spec = pl.BlockSpec((N_EXPERTS, OUT_ROWS), col)
    et_shape = jax.ShapeDtypeStruct((N_EXPERTS, N_TOK), F32)
    return pl.pallas_call(
        functools.partial(_out_kernel, n_x=len(x_parts)),
        grid=(N_TOK // OUT_ROWS,),
        in_specs=_row_specs(x_parts, D_MODEL, OUT_ROWS) + 4 * _ctx_lat_specs(256, OUT_ROWS) + [
            pl.BlockSpec((None, 1, 6 * D_MODEL), lambda i: (_mod_row(i * per_step), 0, 0)),
            full((1, D_MODEL)), pl.BlockSpec((None, D_MODEL, D_MODEL), lambda i: (l, 0, 0)),
            full((D_MODEL, N_EXPERTS)), full((N_EXPERTS, 1)),
            full((D_MODEL, D_SHARED)), full((D_MODEL, D_SHARED)), full((D_SHARED, D_MODEL))],
        out_specs=[pl.BlockSpec((OUT_ROWS, D_MODEL), row), pl.BlockSpec((OUT_ROWS, D_MODEL), row),
                   et_spec, et_spec,
                   pl.BlockSpec((per_step, 8, 128), lambda i: (i, 0, 0))],
        out_shape=[jax.ShapeDtypeStruct((N_TOK, D_MODEL), F32),
                   jax.ShapeDtypeStruct((N_TOK, D_MODEL), BF16),
                   et_shape, et_shape,
                   jax.ShapeDtypeStruct((N_BLK, 8, 128), F32)],
        compiler_params=_params("arbitrary"),
        name="out_proj",
    )(*x_parts, *[a for pair in mixer_pairs for a in pair], mod, g, wo, wr, br, wsg, wsu, wsd)


def _segment_plan(cnt):
    cnt = cnt[:, 0, :N_EXPERTS].astype(jnp.int32)
    seg = jnp.maximum((cnt + (SEG_ALIGN - 1)) // SEG_ALIGN, 1) * SEG_ALIGN
    local = jnp.cumsum(seg, axis=1) - seg
    total = jnp.sum(seg, axis=1)
    per_expert = jnp.sum(seg, axis=0)
    padded = (per_expert + SEG_ALIGN + (EXPERT_TILE - 1)) // EXPERT_TILE * EXPERT_TILE
    ends = jnp.cumsum(padded)
    start = ends - padded
    dest = start[None, :] + jnp.cumsum(seg, axis=0) - seg
    n_tiles = ends[-1] // EXPERT_TILE
    tiles = jnp.arange(MAX_TILES, dtype=jnp.int32)
    tile_expert = jnp.sum((ends // EXPERT_TILE)[None, :] <= jnp.minimum(tiles, n_tiles - 1)[:, None], axis=1)
    tile_expert = tile_expert.astype(jnp.int32)
    plan = dict(seg=seg, local=local, total=total.astype(jnp.int32), dest=dest.astype(jnp.int32),
                tail_start=(start + per_expert).astype(jnp.int32), tail_rows=(padded - per_expert).astype(jnp.int32),
                n_tiles=n_tiles.reshape(1).astype(jnp.int32), tile_expert=tile_expert)
    segf, localf = seg.astype(F32), local.astype(F32)
    pad_lanes = lambda a: jnp.concatenate([a, jnp.zeros_like(a)], axis=1)[:, None, :]
    plan.update(seg_row=pad_lanes(segf), local_row=pad_lanes(localf),
                seg_col=segf[:, :, None], local_col=localf[:, :, None])
    return plan


def _local_dest_digits(rank, local_start):
    dest = jnp.where(rank >= 0.0, local_start + rank, NO_DEST)
    hi = jnp.floor(dest * (1.0 / 64.0))
    return hi, dest - 64.0 * hi


def _dispatch_kernel(dest_ref, seg_ref, local_ref, total_ref, tail_start_ref, tail_rows_ref, nt_ref,
                     h_ref, rank_ref, local_col_ref, local_row_ref, seg_row_ref,
                     xs_hbm, buf, zeros, sems, zsem, usem):
    b = pl.program_id(0)
    slot = b % 2

    def wait_block(blk, s):
        n = pl.multiple_of(total_ref[blk], SEG_ALIGN)
        pltpu.make_async_copy(buf.at[s, pl.ds(0, n)], xs_hbm.at[pl.ds(0, n)], sems.at[s]).wait()

    @pl.when(b == 0)
    def _():
        zeros[...] = jnp.zeros_like(zeros)

    def unused_tiles(action):
        for k in range(-(-MAX_TILES // N_BLK)):
            t = nt_ref[0] + b + k * N_BLK

            @pl.when(t < MAX_TILES)
            def _():
                cp = pltpu.make_async_copy(zeros.at[pl.ds(0, EXPERT_TILE)],
                                           xs_hbm.at[pl.ds(pl.multiple_of(t * EXPERT_TILE, EXPERT_TILE),
                                                           EXPERT_TILE)], usem)
                cp.start() if action == "start" else cp.wait()

    unused_tiles("start")

    @pl.when(b >= 2)
    def _():
        wait_block(b - 2, slot)

    hi, lo = _local_dest_digits(rank_ref[...], local_col_ref[...])
    code = jnp.concatenate([hi, lo], axis=0).astype(BF16)
    hb = h_ref[...]
    lstart = local_row_ref[:, :N_EXPERTS]
    lend = lstart + seg_row_ref[:, :N_EXPERTS]
    r = lax.broadcasted_iota(jnp.int32, (LOCAL_CHUNK, N_EXPERTS), 0).astype(F32)
    rr = lax.broadcasted_iota(jnp.int32, (LOCAL_CHUNK, ROW_TILE), 0).astype(F32)
    def compact(c):
        first = float(c * LOCAL_CHUNK)
        member = (r >= lstart - first) & (r < lend - first)
        table = jnp.concatenate([jnp.where(member, 64.0, 0.0), jnp.where(member, 1.0, 0.0)], axis=1).astype(BF16)
        d = _dot(table, code)
        onehot = jnp.where(d == rr + first, 1.0, 0.0).astype(BF16)
        buf[slot, c * LOCAL_CHUNK:(c + 1) * LOCAL_CHUNK, :] = _dot(onehot, hb).astype(BF16)

    n_chunks = LOCAL_ROWS // LOCAL_CHUNK
    for c in range(n_chunks - 1):
        compact(c)
    pl.when(total_ref[b] > (n_chunks - 1) * LOCAL_CHUNK)(functools.partial(compact, n_chunks - 1))

    for e in range(N_EXPERTS):
        n = pl.multiple_of(seg_ref[b, e], SEG_ALIGN)
        src = pl.multiple_of(local_ref[b, e], SEG_ALIGN)
        dst = pl.multiple_of(dest_ref[b, e], SEG_ALIGN)
        pltpu.make_async_copy(buf.at[slot, pl.ds(src, n)], xs_hbm.at[pl.ds(dst, n)], sems.at[slot]).start()

    unused_tiles("wait")

    @pl.when(b == N_BLK - 1)
    def _():
        def tail(e):
            n = pl.multiple_of(tail_rows_ref[e], SEG_ALIGN)
            dst = pl.multiple_of(tail_start_ref[e], SEG_ALIGN)
            return pltpu.make_async_copy(zeros.at[pl.ds(0, n)], xs_hbm.at[pl.ds(dst, n)], zsem)

        for e in range(N_EXPERTS):
            tail(e).start()
        for e in range(N_EXPERTS):
            tail(e).wait()
        wait_block(b - 1, 1 - slot)
        wait_block(b, slot)


def _dispatch(h, rank_t, plan):
    blk = lambda shape, imap: pl.BlockSpec(shape, imap)
    return pl.pallas_call(
        _dispatch_kernel,
        grid_spec=pltpu.PrefetchScalarGridSpec(
            num_scalar_prefetch=7, grid=(N_BLK,),
            in_specs=[blk((ROW_TILE, D_MODEL), lambda b, *_: (b, 0)),
                      blk((N_EXPERTS, ROW_TILE), lambda b, *_: (0, b)),
                      blk((None, N_EXPERTS, 1), lambda b, *_: (b, 0, 0)),
                      blk((None, 1, 128), lambda b, *_: (b, 0, 0)),
                      blk((None, 1, 128), lambda b, *_: (b, 0, 0))],
            out_specs=pl.BlockSpec(memory_space=pl.ANY),
            scratch_shapes=[pltpu.VMEM((2, LOCAL_ROWS, D_MODEL), BF16),
                            pltpu.VMEM((EXPERT_TILE + SEG_ALIGN, D_MODEL), BF16),
                            pltpu.SemaphoreType.DMA((2,)), pltpu.SemaphoreType.DMA(()),
                            pltpu.SemaphoreType.DMA(())]),
        out_shape=jax.ShapeDtypeStruct((SORTED_ROWS, D_MODEL), BF16),
        compiler_params=_params("arbitrary"),
        name="dispatch",
    )(plan['dest'], plan['seg'], plan['local'], plan['total'], plan['tail_start'], plan['tail_rows'],
      plan['n_tiles'], h, rank_t, plan['local_col'], plan['local_row'], plan['seg_row'])


def _expert_kernel(te_ref, nt_ref, x_ref, wg_ref, wu_ref, wd_ref, y_ref, wg_s, wu_s, wd_s):
    i = pl.program_id(0)
    active = i < nt_ref[0]

    @pl.when((i == 0) | (te_ref[i] != te_ref[jnp.maximum(i - 1, 0)]))
    def _():
        wg_s[...] = wg_ref[...].astype(BF16)
        wu_s[...] = wu_ref[...].astype(BF16)
        wd_s[...] = wd_ref[...].astype(BF16)

    @pl.when(active)
    def _():
        x = x_ref[...]
        hid = _silu(_dot(x, wg_s[...])) * _dot(x, wu_s[...])
        y_ref[...] = _dot(hid.astype(BF16), wd_s[...]).astype(BF16)


def _experts(xs, plan, w_gate, w_up, w_down, l):
    rows = pl.BlockSpec((EXPERT_TILE, D_MODEL), lambda i, te, nt: (jnp.minimum(i, nt[0] - 1), 0))
    wspec = lambda shape: pl.BlockSpec((None, None) + shape, lambda i, te, nt: (l, te[i], 0, 0))
    return pl.pallas_call(
        _expert_kernel,
        grid_spec=pltpu.PrefetchScalarGridSpec(
            num_scalar_prefetch=2, grid=(MAX_TILES,),
            in_specs=[rows, wspec((D_MODEL, D_EXPERT)), wspec((D_MODEL, D_EXPERT)), wspec((D_EXPERT, D_MODEL))],
            out_specs=rows,
            scratch_shapes=[pltpu.VMEM((D_MODEL, D_EXPERT), BF16), pltpu.VMEM((D_MODEL, D_EXPERT), BF16),
                            pltpu.VMEM((D_EXPERT, D_MODEL), BF16)]),
        out_shape=jax.ShapeDtypeStruct((SORTED_ROWS, D_MODEL), BF16),
        input_output_aliases={2: 0},
        compiler_params=_params("arbitrary"),
        name="experts",
    )(plan['tile_expert'], plan['n_tiles'], xs, w_gate, w_up, w_down)


def _combine_kernel(*refs, final):
    refs = list(refs)
    dest_ref, seg_ref, local_ref, total_ref = refs[:4]
    x1_ref, ys_hbm, gate_ref, rank_ref, mod_ref, local_row_ref, local_col_ref, seg_col_ref = refs[4:12]
    rest = refs[12:]
    fg_ref = rest.pop(0) if final else None
    outs, (buf, sems) = rest[:-2], rest[-2:]
    b = pl.program_id(0)
    slot = b % 2

    def fetch(blk, s):
        for e in range(N_EXPERTS):
            n = pl.multiple_of(seg_ref[blk, e], SEG_ALIGN)
            src = pl.multiple_of(dest_ref[blk, e], SEG_ALIGN)
            dst = pl.multiple_of(local_ref[blk, e], SEG_ALIGN)
            pltpu.make_async_copy(ys_hbm.at[pl.ds(src, n)], buf.at[s, pl.ds(dst, n)], sems.at[s]).start()

    @pl.when(b == 0)
    def _():
        buf[...] = jnp.zeros_like(buf)
        fetch(0, 0)

    def wait_rows(blk, s):
        n_rows = pl.multiple_of(total_ref[blk], SEG_ALIGN)
        pltpu.make_async_copy(ys_hbm.at[pl.ds(0, n_rows)], buf.at[s, pl.ds(0, n_rows)], sems.at[s]).wait()

    wait_rows(b, slot)
    nxt = jnp.minimum(b + 1, N_BLK - 1)
    fetch(nxt, 1 - slot)

    gate = gate_ref[...].T
    hi, lo = _local_dest_digits(rank_ref[...].T, local_row_ref[:, :N_EXPERTS])
    lhs = jnp.concatenate([jnp.concatenate([hi, lo], axis=1),
                           jnp.concatenate([jnp.zeros_like(gate), gate], axis=1)], axis=0).astype(BF16)
    lstart = local_col_ref[...]
    lend = lstart + seg_col_ref[...]
    routed = jnp.zeros((ROW_TILE, D_MODEL), F32)
    r = lax.broadcasted_iota(jnp.int32, (N_EXPERTS, LOCAL_CHUNK), 1).astype(F32)
    rr = lax.broadcasted_iota(jnp.int32, (ROW_TILE, LOCAL_CHUNK), 1).astype(F32)
    def gather(c, acc):
        first = float(c * LOCAL_CHUNK)
        member = (r >= lstart - first) & (r < lend - first)
        table = jnp.concatenate([jnp.where(member, 64.0, 0.0), jnp.where(member, 1.0, 0.0)], axis=0).astype(BF16)
        dg = _dot(lhs, table)
        weights = jnp.where(dg[:ROW_TILE] == rr + first, dg[ROW_TILE:], 0.0).astype(BF16)
        return acc + _dot(weights, buf[slot, c * LOCAL_CHUNK:(c + 1) * LOCAL_CHUNK, :])

    for c in range(LOCAL_ROWS // LOCAL_CHUNK):
        routed = gather(c, routed)
    x = x1_ref[...] + mod_ref[:, 5 * D_MODEL:6 * D_MODEL] * routed
    if final:
        y = _rms(x, fg_ref[...])

        @pl.when(b < CTX_TILES)
        def _():
            outs[0][...] = y

        @pl.when(b >= CTX_TILES)
        def _():
            outs[1][...] = y
    else:
        outs[0][...] = x

    @pl.when(b == N_BLK - 1)
    def _():
        wait_rows(nxt, 1 - slot)


def _combine(x1, ys, gate_t, rank_t, mod, plan, final_g):
    final = final_g is not None
    blk = lambda shape, imap: pl.BlockSpec(shape, imap)
    xspec = blk((ROW_TILE, D_MODEL), lambda b, *_: (b, 0))
    if final:
        out_specs = _ctx_lat_specs(D_MODEL)
        out_shape = [jax.ShapeDtypeStruct((N_CTX, D_MODEL), F32), jax.ShapeDtypeStruct((N_LAT, D_MODEL), F32)]
    else:
        out_specs, out_shape = xspec, jax.ShapeDtypeStruct((N_TOK, D_MODEL), F32)
    et_spec = blk((N_EXPERTS, ROW_TILE), lambda b, *_: (0, b))
    in_specs = [xspec, pl.BlockSpec(memory_space=pl.ANY), et_spec, et_spec,
                blk((None, 1, 6 * D_MODEL), lambda b, *_: (_mod_row(b), 0, 0)),
                blk((None, 1, 128), lambda b, *_: (b, 0, 0)),
                blk((None, N_EXPERTS, 1), lambda b, *_: (b, 0, 0)),
                blk((None, N_EXPERTS, 1), lambda b, *_: (b, 0, 0))]
    args = [x1, ys, gate_t, rank_t, mod, plan['local_row'], plan['local_col'], plan['seg_col']]
    if final:
        in_specs.append(blk((1, D_MODEL), lambda b, *_: (0, 0)))
        args.append(final_g)
    return pl.pallas_call(
        functools.partial(_combine_kernel, final=final),
        grid_spec=pltpu.PrefetchScalarGridSpec(
            num_scalar_prefetch=4, grid=(N_BLK,),
            in_specs=in_specs, out_specs=out_specs,
            scratch_shapes=[pltpu.VMEM((2, LOCAL_ROWS, D_MODEL), BF16), pltpu.SemaphoreType.DMA((2,))]),
        out_shape=out_shape,
        compiler_params=_params("arbitrary"),
        name="combine",
    )(plan['dest'], plan['seg'], plan['local'], plan['total'], *args)


def _rope_full_tables(dim, n_rep):
    rows = DEC_SEQ // GRID_W
    r_idx, c_idx = np.meshgrid(np.arange(rows), np.arange(GRID_W), indexing='ij')
    pos = np.stack([r_idx.reshape(-1), c_idx.reshape(-1)], axis=-1).astype(np.float32)
    nf = dim // 4
    inv = np.float32(ROPE_BASE) ** (-np.arange(nf, dtype=np.float32) / np.float32(nf))
    ang = (pos[:, :, None] * inv).astype(np.float32)
    ang = np.repeat(ang.reshape(DEC_SEQ, 2 * nf), 2, axis=1)
    sign = np.tile(np.asarray([-1.0, 1.0], np.float32), dim // 2)
    cos = np.tile(np.cos(ang.astype(np.float64)), (1, n_rep))
    sin = np.tile(np.sin(ang.astype(np.float64)) * sign, (1, n_rep))
    return jnp.asarray(cos, F32), jnp.asarray(sin, F32)


def _pack_w_in(w):
    c0 = MLA_Q_LORA + MLA_KV_LORA
    kr = w[..., c0:MLA_IN]
    s0 = MLA_IN + HG_IN + FN_IN
    qh = [w[..., s0 + h * SWA_HD:s0 + (h + 1) * SWA_HD] for h in SWA_STACK_ORDER]
    return jnp.concatenate([w[..., :c0], kr, kr, kr, kr, w[..., MLA_IN:s0]] + qh
                           + [w[..., s0 + SWA_QW:]], axis=-1).astype(BF16)


def _pack_w_q_up(w):
    hd = MLA_NOPE + MLA_ROPE
    nope = [w[:, h * hd:h * hd + MLA_NOPE] for h in range(MLA_HEADS)]
    rope = [w[:, h * hd + MLA_NOPE:(h + 1) * hd] for h in range(MLA_HEADS)]
    return jnp.concatenate(nope + rope, axis=1).astype(BF16)


def _pack_w_kv_up(w):
    hd = MLA_NOPE + MLA_V
    kn = [w[:, h * hd:h * hd + MLA_NOPE] for h in range(MLA_HEADS)]
    vv = [w[:, h * hd + MLA_NOPE:(h + 1) * hd] for h in range(MLA_HEADS)]
    return jnp.concatenate(kn + vv, axis=1).astype(BF16)


def _pack_w_out(w):
    s0 = 3 * 256
    rows = [w[..., s0 + h * SWA_HD:s0 + (h + 1) * SWA_HD, :] for h in SWA_STACK_ORDER]
    return jnp.concatenate([w[..., :s0, :]] + rows, axis=-2).astype(BF16)


def kernel(x_prompt, x_sample, c, cache_mla_ckv, cache_mla_krope, cache_swa_k, cache_swa_v, state_hgrn,
           c_ctx, w_ada, b_ada, norm1_g, norm2_g, w_in, mla_q_norm_g, mla_w_q_up, mla_kv_norm_g, mla_w_kv_up,
           hg_lb_logits, hg_norm_g, fn_w, swa_sink, w_out, moe_w_router, moe_b_router, moe_w_gate, moe_w_up,
           moe_w_down, sh_w_gate, sh_w_up, sh_w_down, final_norm_g):
    x_parts = (x_prompt.reshape(N_CTX, D_MODEL), x_sample.reshape(N_LAT, D_MODEL))
    cv8 = jnp.concatenate([c_ctx[None, :], c, jnp.zeros((8 - 1 - DEC_BATCH, D_MODEL), F32)], axis=0)
    mods = _ada(cv8, w_ada, b_ada).reshape(DEPTH, 8, 1, 6 * D_MODEL)

    lb = jnp.cumsum(jax.nn.softmax(hg_lb_logits.astype(F32), axis=1), axis=1)
    lb = lb - lb[:, :1]

    cos_m, sin_m = _rope_full_tables(MLA_ROPE, MLA_HEADS)
    cos_q, sin_q = _rope_full_tables(SWA_HD, SWA_HEADS)
    cos_k, sin_k = cos_q[:, :SWA_KW], sin_q[:, :SWA_KW]
    cache_k = cache_swa_k.reshape(DEC_BATCH, DEPTH, PAST_LEN, SWA_KW)
    cache_v = cache_swa_v.reshape(DEC_BATCH, DEPTH, PAST_LEN, SWA_KW)
    state_t = jnp.swapaxes(state_hgrn, -1, -2)

    ctx_blk_lat = N_CTX // DEC_SEQ
    w_in_p, w_out_p = _pack_w_in(w_in), _pack_w_out(w_out)
    new_ckv, new_kr, new_k, new_v, new_st = [], [], [], [], []
    for l in range(DEPTH):
        u_mla, u_hg, u_fn, u_swa = _in_proj(x_parts, mods[l], norm1_g[l][None], w_in_p, l)

        qg, kvg = mla_q_norm_g[l][None], mla_kv_norm_g[l][None]
        wq, wkv = _pack_w_q_up(mla_w_q_up[l]), _pack_w_kv_up(mla_w_kv_up[l])
        o_mla_c, ckv_c = _mla_ctx(u_mla, qg, wq, kvg, wkv)
        o_mla_l = _mla_lat(u_mla, cache_mla_ckv, cache_mla_krope, l, cos_m, sin_m, qg, wq, kvg, wkv)

        lbf, lbb = lb[0, l][None], lb[1, l][None]
        ng4 = jnp.tile(hg_norm_g[l], HG_HEADS)[None]
        o_hg_c, st_c = _hgrn(u_hg, lbf, lbb, ng4, None, seq=SEQ, n_batch=BATCH, row_block0=0)
        o_hg_l = _hgrn(u_hg, lbf, lbb, ng4, state_t[:, l], seq=DEC_SEQ, n_batch=DEC_BATCH,
                       row_block0=ctx_blk_lat)

        fw = fn_w[l].astype(BF16)
        o_fn_c = _fourier(u_fn, fw, seq=SEQ, n_batch=BATCH, row_block0=0)
        o_fn_l = _fourier(u_fn, fw, seq=DEC_SEQ, n_batch=DEC_BATCH, row_block0=ctx_blk_lat)

        sink = swa_sink[l]
        o_swa_c = _swa_ctx(u_swa, sink)
        o_swa_l = _swa_lat(u_swa, cache_k, cache_v, l, sink, cos_q, sin_q, cos_k, sin_k)

        x1, h2, gate_t, rank_t, cnt = _out_proj(
            x_parts, ((o_mla_c, o_mla_l), (o_hg_c, o_hg_l), (o_fn_c, o_fn_l), (o_swa_c, o_swa_l)),
            mods[l], norm2_g[l][None], w_out_p, l,
            moe_w_router[l], moe_b_router[l][:, None],
            sh_w_gate[l].astype(BF16), sh_w_up[l].astype(BF16), sh_w_down[l].astype(BF16))
        plan = _segment_plan(cnt)
        xs = _dispatch(h2, rank_t, plan)
        ys = _experts(xs, plan, moe_w_gate, moe_w_up, moe_w_down, l)
        if l < DEPTH - 1:
            x_parts = (_combine(x1, ys, gate_t, rank_t, mods[l], plan, None),)
        else:
            y_prompt, y_sample = _combine(x1, ys, gate_t, rank_t, mods[l], plan, final_norm_g[None])

        new_ckv.append(ckv_c.reshape(BATCH, SEQ, MLA_KV_LORA))
        new_kr.append(u_mla[:N_CTX, MLA_Q_LORA + MLA_KV_LORA:MLA_IN].reshape(BATCH, SEQ, MLA_ROPE))
        new_k.append(u_swa[:N_CTX, SWA_QW:SWA_QW + SWA_KW].reshape(BATCH, SEQ, SWA_KV_HEADS, SWA_HD))
        new_v.append(u_swa[:N_CTX, SWA_QW + SWA_KW:].reshape(BATCH, SEQ, SWA_KV_HEADS, SWA_HD))
        new_st.append(jnp.swapaxes(st_c, -1, -2))

    y_prompt = y_prompt.reshape(BATCH, SEQ, D_MODEL)
    y_sample = y_sample.reshape(DEC_BATCH, DEC_SEQ, D_MODEL)
    stack = lambda xs: jnp.stack(xs, axis=1)
    return (y_prompt, y_sample, stack(new_ckv), stack(new_kr), stack(new_k), stack(new_v), stack(new_st))
```

```python
import functools

import numpy as np
import jax
import jax.numpy as jnp
from jax import lax
from jax.experimental import pallas as pl
from jax.experimental.pallas import tpu as pltpu

F32 = jnp.float32
BF16 = jnp.bfloat16

D_MODEL = 1024
BATCH = 32
SEQ = 256
DEPTH = 2
DEC_BATCH = 2
DEC_SEQ = 1024
PAST_LEN = 256
GRID_W = 64
EPS = 1e-6
ROPE_BASE = 10000.0
NEG_INF = -1e30

MLA_HEADS = 4
MLA_NOPE = 64
MLA_ROPE = 32
MLA_V = 64
MLA_Q_LORA = 256
MLA_KV_LORA = 128
HG_HEADS = 4
HG_DK = 64
HG_DV = 64
HG_W = HG_HEADS * HG_DK
FN_GROUPS = 4
FN_WIDTH = 256
SWA_HEADS = 4
SWA_KV_HEADS = 2
SWA_HD = 64
WINDOW = 128
N_EXPERTS = 64
TOP_K = 6
D_EXPERT = 256
D_SHARED = 256
ROUTE_SCALE = 2.5

MLA_IN = MLA_Q_LORA + MLA_KV_LORA + MLA_ROPE
HG_IN = 3 * HG_HEADS * HG_DK + 2 * HG_HEADS * HG_DV
FN_IN = FN_WIDTH
SWA_IN = (SWA_HEADS + 2 * SWA_KV_HEADS) * SWA_HD

N_CTX = BATCH * SEQ
N_LAT = DEC_BATCH * DEC_SEQ
N_TOK = N_CTX + N_LAT

MLA_PACK = 512
U_COLS = MLA_PACK + HG_IN + FN_IN + SWA_IN

ROW_TILE = 256
CTX_TILES = N_CTX // ROW_TILE
LAT_TILES_PER_BATCH = DEC_SEQ // ROW_TILE
HG_CHUNK = 32
HG_BLOCK = 256
SWA_QBLK = 128
MLA_QBLK = 256
CTX_PER_STEP = 4
SWA_CTX_PER_STEP = 8
HG_CTX_PER_STEP = 4
OUT_ROWS = 2 * ROW_TILE
VMEM_LIMIT = 56 * 1024 * 1024


def _dot(a, b):
    return jnp.dot(a, b, preferred_element_type=F32)


def _dot_nt(a, b):
    return lax.dot_general(a, b, (((1,), (1,)), ((), ())), preferred_element_type=F32)


def _dot_tn(a, b):
    return lax.dot_general(a, b, (((0,), (0,)), ((), ())), preferred_element_type=F32)


def _split3(x):
    hi = x.astype(BF16)
    r1 = x - hi.astype(F32)
    mid = r1.astype(BF16)
    return hi, mid, (r1 - mid.astype(F32)).astype(BF16)


def _dot_exact_lhs(a, b):
    ab = a.astype(BF16)
    hi, mid, lo = _split3(b)
    return (_dot(ab, lo) + _dot(ab, mid)) + _dot(ab, hi)


def _dot_exact_rhs(a, b):
    bb = b.astype(BF16)
    hi, mid, lo = _split3(a)
    return (_dot(lo, bb) + _dot(mid, bb)) + _dot(hi, bb)


def _rms(x, g):
    return x * lax.rsqrt(jnp.mean(x * x, axis=-1, keepdims=True) + EPS) * g


def _silu(x):
    return x * jax.nn.sigmoid(x)


def _mod_row(i):
    return jnp.where(i < CTX_TILES, 0, 1 + (i - CTX_TILES) // LAT_TILES_PER_BATCH)


def _params(*sem):
    return pltpu.CompilerParams(dimension_semantics=sem, vmem_limit_bytes=VMEM_LIMIT)


ADA_COLS = 1536


def _ada_kernel(cv_ref, w_ref, b_ref, o_ref):
    a = _silu(cv_ref[...]).astype(BF16)
    o_ref[...] = _dot(a, w_ref[...].astype(BF16)) + b_ref[...]


def _ada(cv8, w_ada, b_ada):
    return pl.pallas_call(
        _ada_kernel,
        grid=(DEPTH, 6 * D_MODEL // ADA_COLS),
        in_specs=[
            pl.BlockSpec((8, D_MODEL), lambda l, j: (0, 0)),
            pl.BlockSpec((None, D_MODEL, ADA_COLS), lambda l, j: (l, 0, j)),
            pl.BlockSpec((None, 1, ADA_COLS), lambda l, j: (l, 0, j)),
        ],
        out_specs=pl.BlockSpec((None, 8, ADA_COLS), lambda l, j: (l, 0, j)),
        out_shape=jax.ShapeDtypeStruct((DEPTH, 8, 6 * D_MODEL), F32),
        compiler_params=_params("arbitrary", "arbitrary"),
        name="ada",
    )(cv8, w_ada, b_ada.reshape(DEPTH, 1, 6 * D_MODEL))


def _ctx_lat_specs(width, tile=ROW_TILE):
    n_ctx = N_CTX // tile
    return [pl.BlockSpec((tile, width), lambda i, *_: (jnp.minimum(i, n_ctx - 1), 0)),
            pl.BlockSpec((tile, width), lambda i, *_: (jnp.maximum(i - n_ctx, 0), 0))]


def _row_specs(parts, width, tile=ROW_TILE):
    if len(parts) == 2:
        return _ctx_lat_specs(width, tile)
    return [pl.BlockSpec((tile, width), lambda i, *_: (i, 0))]


def _read_rows(refs, tile=ROW_TILE):
    if len(refs) == 1:
        return refs[0][...]
    return jnp.where(pl.program_id(0) < N_CTX // tile, refs[0][...], refs[1][...])


def _in_kernel(*refs, n_x):
    x = _read_rows(refs[:n_x], OUT_ROWS)
    mod_ref, g_ref, w_ref, umla_ref, uhg_ref, ufn_ref, uswa_ref, w_s = refs[n_x:]

    @pl.when(pl.program_id(0) == 0)
    def _():
        w_s[...] = w_ref[...].astype(BF16)

    sh1 = mod_ref[:, 0:D_MODEL]
    sc1 = mod_ref[:, D_MODEL:2 * D_MODEL]
    h = _rms(x, g_ref[...]) * (1.0 + sc1) + sh1
    u = _dot(h.astype(BF16), w_s[...])
    o = 0
    for ref, width in ((umla_ref, MLA_PACK), (uhg_ref, HG_IN), (ufn_ref, FN_IN), (uswa_ref, SWA_IN)):
        ref[...] = u[:, o:o + width].astype(ref.dtype)
        o += width


def _in_proj(x_parts, mod, g, w, l):
    row = lambda i: (i, 0)
    widths = (MLA_PACK, HG_IN, FN_IN, SWA_IN)
    return pl.pallas_call(
        functools.partial(_in_kernel, n_x=len(x_parts)),
        grid=(N_TOK // OUT_ROWS,),
        in_specs=_row_specs(x_parts, D_MODEL, OUT_ROWS) + [
            pl.BlockSpec((None, 1, 6 * D_MODEL), lambda i: (_mod_row(i * (OUT_ROWS // ROW_TILE)), 0, 0)),
            pl.BlockSpec((1, D_MODEL), lambda i: (0, 0)),
            pl.BlockSpec((None, D_MODEL, U_COLS), lambda i: (l, 0, 0))],
        out_specs=[pl.BlockSpec((OUT_ROWS, wd), row) for wd in widths],
        out_shape=[jax.ShapeDtypeStruct((N_TOK, wd), F32) for wd in widths],
        scratch_shapes=[pltpu.VMEM((D_MODEL, U_COLS), BF16)],
        compiler_params=_params("arbitrary"),
        name="in_proj",
    )(*x_parts, mod, g, w)


def _rope(x, cos, sin_signed):
    lane = lax.broadcasted_iota(jnp.int32, x.shape, 1)
    width = x.shape[1]
    swapped = jnp.where(lane % 2 == 0, pltpu.roll(x, width - 1, 1), pltpu.roll(x, 1, 1))
    return x * cos + swapped * sin_signed


def _stack_heads(x, n_heads, head_w):
    lane = lax.broadcasted_iota(jnp.int32, x.shape, 1)
    return jnp.concatenate([jnp.where(lane // head_w == h, x, 0.0) for h in range(n_heads)], axis=0)


def _unstack_heads(o, n_heads, head_w):
    t = o.shape[0] // n_heads
    lane = lax.broadcasted_iota(jnp.int32, (t, o.shape[1]), 1)
    out = jnp.zeros((t, o.shape[1]), F32)
    for h in range(n_heads):
        out = jnp.where(lane // head_w == h, o[h * t:(h + 1) * t], out)
    return out


MLA_SCALE = (MLA_NOPE + MLA_ROPE) ** -0.5
MLA_QW = MLA_HEADS * MLA_NOPE + MLA_HEADS * MLA_ROPE
MLA_NW = MLA_HEADS * MLA_NOPE


def _mla_attend(q, kcat, v):
    qs = jnp.concatenate([_stack_heads(q[:, :MLA_NW], MLA_HEADS, MLA_NOPE),
                          _stack_heads(q[:, MLA_NW:], MLA_HEADS, MLA_ROPE)], axis=1)
    s = _dot_nt(qs.astype(BF16), kcat) * MLA_SCALE
    p = jnp.exp(s - jnp.max(s, axis=-1, keepdims=True))
    o = _dot(p.astype(BF16), v) / jnp.sum(p, axis=-1, keepdims=True)
    return _unstack_heads(o, MLA_HEADS, MLA_V)


def _mla_ctx_kernel(u_ref, qg_ref, wq_ref, kvg_ref, wkv_ref, o_ref, ckv_ref):
    for j in range(CTX_PER_STEP):
        rows = slice(j * SEQ, (j + 1) * SEQ)
        u = u_ref[rows, :]
        q = _dot(_rms(u[:, :MLA_Q_LORA], qg_ref[...]).astype(BF16), wq_ref[...])
        ckv = _rms(u[:, MLA_Q_LORA:MLA_Q_LORA + MLA_KV_LORA], kvg_ref[...])
        ckv_ref[rows, :] = ckv
        kv = _dot(ckv.astype(BF16), wkv_ref[...])
        kr4 = u[:, MLA_Q_LORA + MLA_KV_LORA:]
        kcat = jnp.concatenate([kv[:, :MLA_NW], kr4], axis=1).astype(BF16)
        o_ref[rows, :] = _mla_attend(q, kcat, kv[:, MLA_NW:].astype(BF16)).astype(o_ref.dtype)


def _mla_ctx(u_mla, qg, wq, kvg, wkv):
    full = lambda shape: pl.BlockSpec(shape, lambda b: (0, 0))
    rows = CTX_PER_STEP * SEQ
    return pl.pallas_call(
        _mla_ctx_kernel,
        grid=(BATCH // CTX_PER_STEP,),
        in_specs=[pl.BlockSpec((rows, MLA_PACK), lambda b: (b, 0)),
                  full((1, MLA_Q_LORA)), full((MLA_Q_LORA, MLA_QW)),
                  full((1, MLA_KV_LORA)), full((MLA_KV_LORA, 2 * MLA_NW))],
        out_specs=[pl.BlockSpec((rows, MLA_NW), lambda b: (b, 0)),
                   pl.BlockSpec((rows, MLA_KV_LORA), lambda b: (b, 0))],
        out_shape=[jax.ShapeDtypeStruct((N_CTX, MLA_NW), BF16),
                   jax.ShapeDtypeStruct((N_CTX, MLA_KV_LORA), F32)],
        compiler_params=_params("arbitrary"),
        name="mla_ctx",
    )(u_mla, qg, wq, kvg, wkv)


MLA_TK = PAST_LEN + DEC_SEQ


def _mla_lat_kernel(u_ref, cckv_ref, ckr_ref, cos_ref, sin_ref, qg_ref, wq_ref, kvg_ref, wkv_ref,
                    o_ref, kcat_s, v_s):
    i = pl.program_id(1)

    @pl.when(i == 0)
    def _():
        u = u_ref[...]
        ckv_new = _rms(u[:, MLA_Q_LORA:MLA_Q_LORA + MLA_KV_LORA], kvg_ref[...])
        ckv_all = jnp.concatenate([cckv_ref[...], ckv_new], axis=0)
        kv = _dot(ckv_all.astype(BF16), wkv_ref[...])
        kr_new = _rope(u[:, MLA_Q_LORA + MLA_KV_LORA:], cos_ref[...], sin_ref[...])
        ckr = ckr_ref[...]
        kr_all = jnp.concatenate([jnp.concatenate([ckr] * MLA_HEADS, axis=1), kr_new], axis=0)
        kcat_s[...] = jnp.concatenate([kv[:, :MLA_NW], kr_all], axis=1).astype(BF16)
        v_s[...] = kv[:, MLA_NW:].astype(BF16)

    r0 = pl.multiple_of(i * MLA_QBLK, MLA_QBLK)
    cq = u_ref[pl.ds(r0, MLA_QBLK), 0:MLA_Q_LORA]
    q = _dot(_rms(cq, qg_ref[...]).astype(BF16), wq_ref[...])
    qr = _rope(q[:, MLA_NW:], cos_ref[pl.ds(r0, MLA_QBLK), :], sin_ref[pl.ds(r0, MLA_QBLK), :])
    q = jnp.concatenate([q[:, :MLA_NW], qr], axis=1)
    o_ref[...] = _mla_attend(q, kcat_s[...], v_s[...]).astype(o_ref.dtype)


def _mla_lat(u_mla, cache_ckv, cache_kr, l, cos, sin, qg, wq, kvg, wkv):
    full = lambda shape: pl.BlockSpec(shape, lambda b, i: (0, 0))
    nq = DEC_SEQ // MLA_QBLK
    return pl.pallas_call(
        _mla_lat_kernel,
        grid=(DEC_BATCH, nq),
        in_specs=[pl.BlockSpec((DEC_SEQ, MLA_PACK), lambda b, i: (N_CTX // DEC_SEQ + b, 0)),
                  pl.BlockSpec((None, None, PAST_LEN, MLA_KV_LORA), lambda b, i: (b, l, 0, 0)),
                  pl.BlockSpec((None, None, PAST_LEN, MLA_ROPE), lambda b, i: (b, l, 0, 0)),
                  full((DEC_SEQ, MLA_HEADS * MLA_ROPE)), full((DEC_SEQ, MLA_HEADS * MLA_ROPE)),
                  full((1, MLA_Q_LORA)), full((MLA_Q_LORA, MLA_QW)),
                  full((1, MLA_KV_LORA)), full((MLA_KV_LORA, 2 * MLA_NW))],
        out_specs=pl.BlockSpec((MLA_QBLK, MLA_NW), lambda b, i: (b * nq + i, 0)),
        out_shape=jax.ShapeDtypeStruct((N_LAT, MLA_NW), BF16),
        scratch_shapes=[pltpu.VMEM((MLA_TK, MLA_QW), BF16), pltpu.VMEM((MLA_TK, MLA_NW), BF16)],
        compiler_params=_params("arbitrary", "arbitrary"),
        name="mla_lat",
    )(u_mla, cache_ckv, cache_kr, cos, sin, qg, wq, kvg, wkv)


def _hgrn_kernel(*refs, seq, n_seq, has_state):
    if has_state:
        (u_ref, lbf_ref, lbb_ref, ng_ref, s0_ref, o_ref,
         q_s, kf_s, gf_s, kb_s, gb_s, of_s, ob_s, stf_s, stb_s) = refs
    else:
        (u_ref, lbf_ref, lbb_ref, ng_ref, o_ref, so_ref,
         q_s, kf_s, gf_s, kb_s, gb_s, of_s, ob_s, stf_s, stb_s) = refs
    C = HG_CHUNK
    W = HG_W

    q_s[...] = _silu(u_ref[:, 0:W])
    ff = lbf_ref[...] + (1.0 - lbf_ref[...]) * jax.nn.sigmoid(u_ref[:, W:2 * W])
    kf_s[...] = 1.0 - ff
    gf_s[...] = jnp.log(ff)
    fb = lbb_ref[...] + (1.0 - lbb_ref[...]) * jax.nn.sigmoid(u_ref[:, 2 * W:3 * W])
    kb_s[...] = 1.0 - fb
    gb_s[...] = jnp.log(fb)

    rr = lax.broadcasted_iota(jnp.int32, (W, W), 0)
    cc = lax.broadcasted_iota(jnp.int32, (W, W), 1)
    blockdiag = rr // HG_DK == cc // HG_DK
    if has_state:
        for st, d in ((stf_s, 0), (stb_s, 1)):
            rows = []
            for h in range(HG_HEADS):
                z = lambda n: jnp.zeros((HG_DV, n * HG_DK), F32)
                parts = ([z(h)] if h else []) + [s0_ref[d, h]] + ([z(HG_HEADS - 1 - h)] if h < HG_HEADS - 1 else [])
                rows.append(jnp.concatenate(parts, axis=1) if len(parts) > 1 else parts[0])
            st[0] = jnp.concatenate(rows, axis=0)
    else:
        stf_s[...] = jnp.zeros_like(stf_s)
        stb_s[...] = jnp.zeros_like(stb_s)

    B = HG_BLOCK
    per_block = B // C
    n_blocks = seq // B
    ri = lax.broadcasted_iota(jnp.int32, (B, B), 0)
    ci = lax.broadcasted_iota(jnp.int32, (B, B), 1)
    same_chunk = ri // C == ci // C
    rs = lax.broadcasted_iota(jnp.int32, (HG_HEADS * B, B), 0) % B
    cs = lax.broadcasted_iota(jnp.int32, (HG_HEADS * B, B), 1)
    same_chunk_s = rs // C == cs // C

    sums_f = jnp.where(same_chunk & (ci <= ri), 1.0, 0.0)
    sums_b = jnp.where(same_chunk & (ci >= ri), 1.0, 0.0)
    keep_f = same_chunk_s & (rs >= cs)
    keep_b = same_chunk_s & (cs >= rs)

    def chunk_row(x, i):
        x3 = x.reshape(per_block, C, W)
        return jnp.broadcast_to(x3[:, i:i + 1, :], (per_block, C, W)).reshape(B, W)

    def block(r, k_s, g_s, o_s, st_s, sum_mat, keep, order, mid, far):
        q = q_s[pl.ds(r, B), :]
        k = k_s[pl.ds(r, B), :]
        v = u_ref[pl.ds(r, B), 3 * W:4 * W].astype(BF16)
        G = _dot_exact_lhs(sum_mat, g_s[pl.ds(r, B), :])
        Gq = G - chunk_row(G, mid)
        Gk2 = chunk_row(G, far) - G
        qe = _stack_heads(q * jnp.exp(Gq), HG_HEADS, HG_DK)
        ke = k * jnp.exp(-Gq)
        A = jnp.where(keep, _dot_nt(qe.astype(BF16), ke.astype(BF16)), 0.0)
        o_intra = _unstack_heads(_dot(A.astype(BF16), v), HG_HEADS, HG_DV)
        qg = (q * jnp.exp(G)).astype(BF16)
        k2 = (k * jnp.exp(Gk2)).astype(BF16)
        decay = jnp.exp(G + Gk2)
        st = st_s[...]
        o_inter = [None] * per_block
        for c in order:
            rows = slice(c * C, (c + 1) * C)
            o_inter[c] = _dot_nt(st.astype(BF16), qg[rows])
            st = st * decay[c * C:c * C + 1] + jnp.where(blockdiag, _dot_tn(v[rows], k2[rows]), 0.0)
        st_s[...] = st
        o_s[pl.ds(r, B), :] = o_intra + jnp.concatenate(o_inter, axis=1).T

    def fwd(j, r):
        block(r, kf_s, gf_s, of_s, stf_s.at[j], sums_f, keep_f, range(per_block), C // 2 - 1, C - 1)

    def bwd(j, r):
        block(r, kb_s, gb_s, ob_s, stb_s.at[j], sums_b, keep_b, range(per_block - 1, -1, -1), C // 2, 0)

    for j in range(n_seq):
        if n_blocks == 1:
            fwd(j, j * seq)
            bwd(j, j * seq)
        else:
            def both_directions(i, carry, j=j):
                fwd(j, pl.multiple_of(j * seq + i * B, B))
                bwd(j, pl.multiple_of(j * seq + (n_blocks - 1 - i) * B, B))
                return carry

            lax.fori_loop(0, n_blocks, both_directions, 0)

    o = of_s[...] + ob_s[...]
    ms = _dot_exact_rhs(o * o, jnp.where(blockdiag, 1.0 / HG_DV, 0.0))
    on = o * lax.rsqrt(ms + EPS) * ng_ref[...]
    o_ref[...] = (on * _silu(u_ref[:, 4 * W:5 * W])).astype(o_ref.dtype)

    if not has_state:
        for j in range(n_seq):
            for st, d in ((stf_s, 0), (stb_s, 1)):
                for h in range(HG_HEADS):
                    so_ref[j, d, h] = st[j, h * HG_DV:(h + 1) * HG_DV, h * HG_DK:(h + 1) * HG_DK]


def _hgrn(u_hg, lbf, lbb, ng4, state_t, *, seq, n_batch, row_block0):
    has_state = state_t is not None
    n_seq = 1 if has_state else HG_CTX_PER_STEP
    rows = n_seq * seq
    full = lambda shape: pl.BlockSpec(shape, lambda b: (0, 0))
    in_specs = [pl.BlockSpec((rows, HG_IN), lambda b: (row_block0 + b, 0)),
                full((1, HG_W)), full((1, HG_W)), full((1, HG_W))]
    args = [u_hg, lbf, lbb, ng4]
    o_spec = pl.BlockSpec((rows, HG_W), lambda b: (b, 0))
    o_shape = jax.ShapeDtypeStruct((n_batch * seq, HG_W), BF16)
    if has_state:
        in_specs.append(pl.BlockSpec((None, 2, HG_HEADS, HG_DV, HG_DK), lambda b: (b, 0, 0, 0, 0)))
        args.append(state_t)
        out_specs, out_shape = o_spec, o_shape
    else:
        out_specs = [o_spec, pl.BlockSpec((n_seq, 2, HG_HEADS, HG_DV, HG_DK), lambda b: (b, 0, 0, 0, 0))]
        out_shape = [o_shape, jax.ShapeDtypeStruct((n_batch, 2, HG_HEADS, HG_DV, HG_DK), F32)]
    return pl.pallas_call(
        functools.partial(_hgrn_kernel, seq=seq, n_seq=n_seq, has_state=has_state),
        grid=(n_batch // n_seq,),
        in_specs=in_specs, out_specs=out_specs, out_shape=out_shape,
        scratch_shapes=[pltpu.VMEM((rows, HG_W), F32)] * 7 + [pltpu.VMEM((n_seq, HG_W, HG_W), F32)] * 2,
        compiler_params=_params("arbitrary"),
        name="hgrn_lat" if has_state else "hgrn_ctx",
    )(*args)


def _dft_tables(n):
    j = np.arange(n, dtype=np.int64)
    ang = 2.0 * np.pi * ((j[:, None] * j[None, :]) % n).astype(np.float64) / n
    return np.cos(ang) / np.sqrt(n), np.sin(ang) / np.sqrt(n)


def _fourier_tables(seq):
    gw = FN_WIDTH // FN_GROUPS
    cg, sg = _dft_tables(gw)
    eye = np.eye(FN_GROUPS)
    chan = np.concatenate([np.kron(eye, cg), np.kron(eye, sg)], axis=1)
    ct, st = _dft_tables(seq)
    pos = np.concatenate([ct, -st], axis=1)
    return jnp.asarray(chan, F32).astype(BF16), jnp.asarray(pos, F32).astype(BF16)


def _fourier_kernel(x_ref, chan_ref, pos_ref, w_ref, o_ref, *, seq, n_seq):
    for j in range(n_seq):
        rows = slice(j * seq, (j + 1) * seq)
        x12 = _dot(x_ref[rows, :].astype(BF16), chan_ref[...])
        z = jnp.concatenate([x12[:, :FN_WIDTH], x12[:, FN_WIDTH:]], axis=0).astype(BF16)
        y = _dot(pos_ref[...], z)
        o_ref[rows, :] = _dot(y.astype(BF16), w_ref[...]).astype(o_ref.dtype)


def _fourier(u_fn, w, *, seq, n_batch, row_block0):
    chan, pos = _fourier_tables(seq)
    full = lambda shape: pl.BlockSpec(shape, lambda b: (0, 0))
    n_seq = CTX_PER_STEP if seq == SEQ else 1
    return pl.pallas_call(
        functools.partial(_fourier_kernel, seq=seq, n_seq=n_seq),
        grid=(n_batch // n_seq,),
        in_specs=[pl.BlockSpec((n_seq * seq, FN_WIDTH), lambda b: (row_block0 + b, 0)),
                  full((FN_WIDTH, 2 * FN_WIDTH)), full((seq, 2 * seq)), full((FN_WIDTH, FN_WIDTH))],
        out_specs=pl.BlockSpec((n_seq * seq, FN_WIDTH), lambda b: (b, 0)),
        out_shape=jax.ShapeDtypeStruct((n_batch * seq, FN_WIDTH), BF16),
        compiler_params=_params("arbitrary"),
        name="fourier",
    )(u_fn, chan, pos, w)


SWA_SCALE = SWA_HD ** -0.5
SWA_QW = SWA_HEADS * SWA_HD
SWA_KW = SWA_KV_HEADS * SWA_HD
SWA_STACK_ORDER = (0, 2, 1, 3)


def _swa_stack_q(q):
    return jnp.concatenate([_stack_heads(q[:, :SWA_KW], SWA_KV_HEADS, SWA_HD),
                            _stack_heads(q[:, SWA_KW:], SWA_KV_HEADS, SWA_HD)], axis=0)


def _swa_unstack_o(o):
    t = o.shape[0] // SWA_HEADS
    return jnp.concatenate([_unstack_heads(o[:2 * t], SWA_KV_HEADS, SWA_HD),
                            _unstack_heads(o[2 * t:], SWA_KV_HEADS, SWA_HD)], axis=1)


def _sink_rows(sink_ref, t):
    return jnp.concatenate([jnp.full((t, 1), sink_ref[h], F32) for h in SWA_STACK_ORDER], axis=0)


def _swa_ctx_kernel(sink_ref, u_ref, o_ref):
    sink = _sink_rows(sink_ref, SEQ)
    for j in range(SWA_CTX_PER_STEP):
        rows = slice(j * SEQ, (j + 1) * SEQ)
        u = u_ref[rows, :]
        qs = _swa_stack_q(u[:, :SWA_QW]).astype(BF16)
        k = u[:, SWA_QW:SWA_QW + SWA_KW].astype(BF16)
        v = u[:, SWA_QW + SWA_KW:].astype(BF16)
        s = _dot_nt(qs, k) * SWA_SCALE
        m = jnp.maximum(jnp.max(s, axis=-1, keepdims=True), sink)
        p = jnp.exp(s - m)
        denom = jnp.sum(p, axis=-1, keepdims=True) + jnp.exp(sink - m)
        o_ref[rows, :] = _swa_unstack_o(_dot(p.astype(BF16), v) / denom).astype(o_ref.dtype)


def _swa_ctx(u_swa, sink):
    rows = SWA_CTX_PER_STEP * SEQ
    return pl.pallas_call(
        _swa_ctx_kernel,
        grid=(BATCH // SWA_CTX_PER_STEP,),
        in_specs=[pl.BlockSpec(memory_space=pltpu.SMEM),
                  pl.BlockSpec((rows, SWA_IN), lambda b: (b, 0))],
        out_specs=pl.BlockSpec((rows, SWA_QW), lambda b: (b, 0)),
        out_shape=jax.ShapeDtypeStruct((N_CTX, SWA_QW), BF16),
        compiler_params=_params("arbitrary"),
        name="swa_ctx",
    )(sink, u_swa)


SWA_PAD = DEC_SEQ + 2 * SWA_QBLK


def _swa_lat_kernel(sink_ref, u_ref, kc_ref, vc_ref, cosq_ref, sinq_ref, cosk_ref, sin_k_ref,
                    o_ref, k_s, v_s):
    i = pl.program_id(1)
    B = SWA_QBLK

    @pl.when(i == 0)
    def _():
        zeros = jnp.zeros((B, SWA_KW), BF16)
        k = _rope(u_ref[:, SWA_QW:SWA_QW + SWA_KW], cosk_ref[...], sin_k_ref[...]).astype(BF16)
        k_s[...] = jnp.concatenate([zeros, k, zeros], axis=0)
        v_s[...] = jnp.concatenate([zeros, u_ref[:, SWA_QW + SWA_KW:].astype(BF16), zeros], axis=0)

    r0 = pl.multiple_of(i * B, B)
    q = _rope(u_ref[pl.ds(r0, B), 0:SWA_QW], cosq_ref[pl.ds(r0, B), :], sinq_ref[pl.ds(r0, B), :])
    qs = _swa_stack_q(q).astype(BF16)
    s_loc = _dot_nt(qs, k_s[pl.ds(r0, 3 * B), :]) * SWA_SCALE
    row = lax.broadcasted_iota(jnp.int32, s_loc.shape, 0) % B
    col = lax.broadcasted_iota(jnp.int32, s_loc.shape, 1)
    kpos = r0 - B + col
    valid = (jnp.abs(row + B - col) <= WINDOW) & (kpos >= 0) & (kpos < DEC_SEQ)
    s_loc = jnp.where(valid, s_loc, NEG_INF)
    s_ctx = _dot_nt(qs, kc_ref[...].astype(BF16)) * SWA_SCALE
    sink = _sink_rows(sink_ref, B)
    m = jnp.maximum(jnp.maximum(jnp.max(s_loc, axis=-1, keepdims=True),
                                jnp.max(s_ctx, axis=-1, keepdims=True)), sink)
    p_loc = jnp.exp(s_loc - m)
    p_ctx = jnp.exp(s_ctx - m)
    denom = (jnp.sum(p_loc, axis=-1, keepdims=True) + jnp.sum(p_ctx, axis=-1, keepdims=True)
             + jnp.exp(sink - m))
    o = _dot(p_loc.astype(BF16), v_s[pl.ds(r0, 3 * B), :]) + _dot(p_ctx.astype(BF16), vc_ref[...].astype(BF16))
    o_ref[...] = _swa_unstack_o(o / denom).astype(o_ref.dtype)


def _swa_lat(u_swa, cache_k, cache_v, l, sink, cosq, sinq, cosk, sink_k):
    full = lambda shape: pl.BlockSpec(shape, lambda b, i: (0, 0))
    nq = DEC_SEQ // SWA_QBLK
    cache_spec = pl.BlockSpec((None, None, PAST_LEN, SWA_KW), lambda b, i: (b, l, 0, 0))
    return pl.pallas_call(
        _swa_lat_kernel,
        grid=(DEC_BATCH, nq),
        in_specs=[pl.BlockSpec(memory_space=pltpu.SMEM),
                  pl.BlockSpec((DEC_SEQ, SWA_IN), lambda b, i: (N_CTX // DEC_SEQ + b, 0)),
                  cache_spec, cache_spec,
                  full((DEC_SEQ, SWA_QW)), full((DEC_SEQ, SWA_QW)),
                  full((DEC_SEQ, SWA_KW)), full((DEC_SEQ, SWA_KW))],
        out_specs=pl.BlockSpec((SWA_QBLK, SWA_QW), lambda b, i: (b * nq + i, 0)),
        out_shape=jax.ShapeDtypeStruct((N_LAT, SWA_QW), BF16),
        scratch_shapes=[pltpu.VMEM((SWA_PAD, SWA_KW), BF16), pltpu.VMEM((SWA_PAD, SWA_KW), BF16)],
        compiler_params=_params("arbitrary", "arbitrary"),
        name="swa_lat",
    )(sink, u_swa, cache_k, cache_v, cosq, sinq, cosk, sink_k)


N_BLK = N_TOK // ROW_TILE
SEG_ALIGN = 16
LOCAL_ROWS = ROW_TILE * TOP_K + N_EXPERTS * SEG_ALIGN
LOCAL_CHUNK = 512
EXPERT_TILE = 768
SORTED_ROWS = -(-(N_TOK * TOP_K + N_BLK * N_EXPERTS * SEG_ALIGN + N_EXPERTS * (EXPERT_TILE + SEG_ALIGN))
                // EXPERT_TILE) * EXPERT_TILE
MAX_TILES = SORTED_ROWS // EXPERT_TILE
NOT_PICKED = -1.0
NO_DEST = 4095.0


def _out_kernel(*refs, n_x):
    x_all = _read_rows(refs[:n_x], OUT_ROWS)
    mix_all = [_read_rows(refs[n_x + 2 * j:n_x + 2 * j + 2], OUT_ROWS) for j in range(4)]
    out_refs = refs[n_x + 8:]
    for blk in range(OUT_ROWS // ROW_TILE):
        rows = slice(blk * ROW_TILE, (blk + 1) * ROW_TILE)
        _out_block(x_all[rows], [m[rows] for m in mix_all], out_refs, blk)


def _out_block(x, mixers, refs, blk):
    (mod_ref, g_ref, wo_ref, wr_ref, br_ref, wsg_ref, wsu_ref, wsd_ref,
     x1_ref, h_ref, gate_ref, rank_ref, cnt_ref) = refs
    rows = slice(blk * ROW_TILE, (blk + 1) * ROW_TILE)
    mix = jnp.zeros((ROW_TILE, D_MODEL), F32)
    for j in range(4):
        mix = mix + _dot(mixers[j], wo_ref[j * 256:(j + 1) * 256, :])
    g1 = mod_ref[:, 2 * D_MODEL:3 * D_MODEL]
    sh2 = mod_ref[:, 3 * D_MODEL:4 * D_MODEL]
    sc2 = mod_ref[:, 4 * D_MODEL:5 * D_MODEL]
    g2 = mod_ref[:, 5 * D_MODEL:6 * D_MODEL]
    x1 = x + g1 * mix
    h = _rms(x1, g_ref[...]) * (1.0 + sc2) + sh2
    hb = h.astype(BF16)
    h_ref[rows, :] = hb

    w_hi = wr_ref[...].astype(BF16)
    w_lo = (wr_ref[...] - w_hi.astype(F32)).astype(BF16)
    h_lo = (h - hb.astype(F32)).astype(BF16)
    logits = ((_dot(h_lo, w_hi) + _dot(hb, w_lo)) + _dot(hb, w_hi)).T
    scores = jax.nn.sigmoid(logits)
    sel = scores + br_ref[...]
    eidx = lax.broadcasted_iota(jnp.int32, sel.shape, 0)
    gate = jnp.zeros_like(scores)
    picked = jnp.zeros_like(scores)
    for _ in range(TOP_K):
        best = jnp.max(sel, axis=0, keepdims=True)
        first = jnp.min(jnp.where(sel == best, eidx, N_EXPERTS), axis=0, keepdims=True)
        pick = eidx == first
        gate = jnp.where(pick, scores, gate)
        picked = jnp.where(pick, 1.0, picked)
        sel = jnp.where(pick, -jnp.inf, sel)
    gate = ROUTE_SCALE * gate / jnp.sum(gate, axis=0, keepdims=True)

    ti = lax.broadcasted_iota(jnp.int32, (ROW_TILE, ROW_TILE), 0)
    tj = lax.broadcasted_iota(jnp.int32, (ROW_TILE, ROW_TILE), 1)
    pb = picked.astype(BF16)
    rank = _dot(pb, jnp.where(ti < tj, 1.0, 0.0).astype(BF16))
    gate_ref[:, rows] = gate
    rank_ref[:, rows] = jnp.where(picked > 0.0, rank, NOT_PICKED)
    counts = _dot_nt(jnp.ones((8, ROW_TILE), BF16), pb)
    cnt_ref[blk] = jnp.concatenate([counts, jnp.zeros_like(counts)], axis=1)

    hid = _silu(_dot(hb, wsg_ref[...])) * _dot(hb, wsu_ref[...])
    x1_ref[rows, :] = x1 + g2 * _dot(hid.astype(BF16), wsd_ref[...])


def _out_proj(x_parts, mixer_pairs, mod, g, wo, l, wr, br, wsg, wsu, wsd):
    row = lambda i: (i, 0)
    col = lambda i: (0, i)
    full = lambda shape: pl.BlockSpec(shape, lambda i: (0, 0))
    per_step = OUT_ROWS // ROW_TILE
    et_spec = pl.BlockSpec((N_EXPERTS, OUT_ROWS), col)
    et_shape = jax.ShapeDtypeStruct((N_EXPERTS, N_TOK), F32)
    return pl.pallas_call(
        functools.partial(_out_kernel, n_x=len(x_parts)),
        grid=(N_TOK // OUT_ROWS,),
        in_specs=_row_specs(x_parts, D_MODEL, OUT_ROWS) + 4 * _ctx_lat_specs(256, OUT_ROWS) + [
            pl.BlockSpec((None, 1, 6 * D_MODEL), lambda i: (_mod_row(i * per_step), 0, 0)),
            full((1, D_MODEL)), pl.BlockSpec((None, D_MODEL, D_MODEL), lambda i: (l, 0, 0)),
            full((D_MODEL, N_EXPERTS)), full((N_EXPERTS, 1)),
            full((D_MODEL, D_SHARED)), full((D_MODEL, D_SHARED)), full((D_SHARED, D_MODEL))],
        out_specs=[pl.BlockSpec((OUT_ROWS, D_MODEL), row), pl.BlockSpec((OUT_ROWS, D_MODEL), row),
                   et_spec, et_spec,
                   pl.BlockSpec((per_step, 8, 128), lambda i: (i, 0, 0))],
        out_shape=[jax.ShapeDtypeStruct((N_TOK, D_MODEL), F32),
                   jax.ShapeDtypeStruct((N_TOK, D_MODEL), BF16),
                   et_shape, et_shape,
                   jax.ShapeDtypeStruct((N_BLK, 8, 128), F32)],
        compiler_params=_params("arbitrary"),
        name="out_proj",
    )(*x_parts, *[a for pair in mixer_pairs for a in pair], mod, g, wo, wr, br, wsg, wsu, wsd)


def _segment_plan(cnt):
    cnt = cnt[:, 0, :N_EXPERTS].astype(jnp.int32)
    seg = jnp.maximum((cnt + (SEG_ALIGN - 1)) // SEG_ALIGN, 1) * SEG_ALIGN
    local = jnp.cumsum(seg, axis=1) - seg
    total = jnp.sum(seg, axis=1)
    per_expert = jnp.sum(seg, axis=0)
    padded = (per_expert + SEG_ALIGN + (EXPERT_TILE - 1)) // EXPERT_TILE * EXPERT_TILE
    ends = jnp.cumsum(padded)
    start = ends - padded
    dest = start[None, :] + jnp.cumsum(seg, axis=0) - seg
    n_tiles = ends[-1] // EXPERT_TILE
    tiles = jnp.arange(MAX_TILES, dtype=jnp.int32)
    tile_expert = jnp.sum((ends // EXPERT_TILE)[None, :] <= jnp.minimum(tiles, n_tiles - 1)[:, None], axis=1)
    tile_expert = tile_expert.astype(jnp.int32)
    plan = dict(seg=seg, local=local, total=total.astype(jnp.int32), dest=dest.astype(jnp.int32),
                tail_start=(start + per_expert).astype(jnp.int32), tail_rows=(padded - per_expert).astype(jnp.int32),
                n_tiles=n_tiles.reshape(1).astype(jnp.int32), tile_expert=tile_expert)
    segf, localf = seg.astype(F32), local.astype(F32)
    pad_lanes = lambda a: jnp.concatenate([a, jnp.zeros_like(a)], axis=1)[:, None, :]
    plan.update(seg_row=pad_lanes(segf), local_row=pad_lanes(localf),
                seg_col=segf[:, :, None], local_col=localf[:, :, None])
    return plan


def _local_dest_digits(rank, local_start):
    dest = jnp.where(rank >= 0.0, local_start + rank, NO_DEST)
    hi = jnp.floor(dest * (1.0 / 64.0))
    return hi, dest - 64.0 * hi


def _dispatch_kernel(dest_ref, seg_ref, local_ref, total_ref, tail_start_ref, tail_rows_ref, nt_ref,
                     h_ref, rank_ref, local_col_ref, local_row_ref, seg_row_ref,
                     xs_hbm, buf, zeros, sems, zsem, usem):
    b = pl.program_id(0)
    slot = b % 2

    def wait_block(blk, s):
        n = pl.multiple_of(total_ref[blk], SEG_ALIGN)
        pltpu.make_async_copy(buf.at[s, pl.ds(0, n)], xs_hbm.at[pl.ds(0, n)], sems.at[s]).wait()

    @pl.when(b == 0)
    def _():
        zeros[...] = jnp.zeros_like(zeros)

    def unused_tiles(action):
        for k in range(-(-MAX_TILES // N_BLK)):
            t = nt_ref[0] + b + k * N_BLK

            @pl.when(t < MAX_TILES)
            def _():
                cp = pltpu.make_async_copy(zeros.at[pl.ds(0, EXPERT_TILE)],
                                           xs_hbm.at[pl.ds(pl.multiple_of(t * EXPERT_TILE, EXPERT_TILE),
                                                           EXPERT_TILE)], usem)
                cp.start() if action == "start" else cp.wait()

    unused_tiles("start")

    @pl.when(b >= 2)
    def _():
        wait_block(b - 2, slot)

    hi, lo = _local_dest_digits(rank_ref[...], local_col_ref[...])
    code = jnp.concatenate([hi, lo], axis=0).astype(BF16)
    hb = h_ref[...]
    lstart = local_row_ref[:, :N_EXPERTS]
    lend = lstart + seg_row_ref[:, :N_EXPERTS]
    r = lax.broadcasted_iota(jnp.int32, (LOCAL_CHUNK, N_EXPERTS), 0).astype(F32)
    rr = lax.broadcasted_iota(jnp.int32, (LOCAL_CHUNK, ROW_TILE), 0).astype(F32)
    def compact(c):
        first = float(c * LOCAL_CHUNK)
        member = (r >= lstart - first) & (r < lend - first)
        table = jnp.concatenate([jnp.where(member, 64.0, 0.0), jnp.where(member, 1.0, 0.0)], axis=1).astype(BF16)
        d = _dot(table, code)
        onehot = jnp.where(d == rr + first, 1.0, 0.0).astype(BF16)
        buf[slot, c * LOCAL_CHUNK:(c + 1) * LOCAL_CHUNK, :] = _dot(onehot, hb).astype(BF16)

    n_chunks = LOCAL_ROWS // LOCAL_CHUNK
    for c in range(n_chunks - 1):
        compact(c)
    pl.when(total_ref[b] > (n_chunks - 1) * LOCAL_CHUNK)(functools.partial(compact, n_chunks - 1))

    for e in range(N_EXPERTS):
        n = pl.multiple_of(seg_ref[b, e], SEG_ALIGN)
        src = pl.multiple_of(local_ref[b, e], SEG_ALIGN)
        dst = pl.multiple_of(dest_ref[b, e], SEG_ALIGN)
        pltpu.make_async_copy(buf.at[slot, pl.ds(src, n)], xs_hbm.at[pl.ds(dst, n)], sems.at[slot]).start()

    unused_tiles("wait")

    @pl.when(b == N_BLK - 1)
    def _():
        def tail(e):
            n = pl.multiple_of(tail_rows_ref[e], SEG_ALIGN)
            dst = pl.multiple_of(tail_start_ref[e], SEG_ALIGN)
            return pltpu.make_async_copy(zeros.at[pl.ds(0, n)], xs_hbm.at[pl.ds(dst, n)], zsem)

        for e in range(N_EXPERTS):
            tail(e).start()
        for e in range(N_EXPERTS):
            tail(e).wait()
        wait_block(b - 1, 1 - slot)
        wait_block(b, slot)


def _dispatch(h, rank_t, plan):
    blk = lambda shape, imap: pl.BlockSpec(shape, imap)
    return pl.pallas_call(
        _dispatch_kernel,
        grid_spec=pltpu.PrefetchScalarGridSpec(
            num_scalar_prefetch=7, grid=(N_BLK,),
            in_specs=[blk((ROW_TILE, D_MODEL), lambda b, *_: (b, 0)),
                      blk((N_EXPERTS, ROW_TILE), lambda b, *_: (0, b)),
                      blk((None, N_EXPERTS, 1), lambda b, *_: (b, 0, 0)),
                      blk((None, 1, 128), lambda b, *_: (b, 0, 0)),
                      blk((None, 1, 128), lambda b, *_: (b, 0, 0))],
            out_specs=pl.BlockSpec(memory_space=pl.ANY),
            scratch_shapes=[pltpu.VMEM((2, LOCAL_ROWS, D_MODEL), BF16),
                            pltpu.VMEM((EXPERT_TILE + SEG_ALIGN, D_MODEL), BF16),
                            pltpu.SemaphoreType.DMA((2,)), pltpu.SemaphoreType.DMA(()),
                            pltpu.SemaphoreType.DMA(())]),
        out_shape=jax.ShapeDtypeStruct((SORTED_ROWS, D_MODEL), BF16),
        compiler_params=_params("arbitrary"),
        name="dispatch",
    )(plan['dest'], plan['seg'], plan['local'], plan['total'], plan['tail_start'], plan['tail_rows'],
      plan['n_tiles'], h, rank_t, plan['local_col'], plan['local_row'], plan['seg_row'])


def _expert_kernel(te_ref, nt_ref, x_ref, wg_ref, wu_ref, wd_ref, y_ref, wg_s, wu_s, wd_s):
    i = pl.program_id(0)
    active = i < nt_ref[0]

    @pl.when((i == 0) | (te_ref[i] != te_ref[jnp.maximum(i - 1, 0)]))
    def _():
        wg_s[...] = wg_ref[...].astype(BF16)
        wu_s[...] = wu_ref[...].astype(BF16)
        wd_s[...] = wd_ref[...].astype(BF16)

    @pl.when(active)
    def _():
        x = x_ref[...]
        hid = _silu(_dot(x, wg_s[...])) * _dot(x, wu_s[...])
        y_ref[...] = _dot(hid.astype(BF16), wd_s[...]).astype(BF16)


def _experts(xs, plan, w_gate, w_up, w_down, l):
    rows = pl.BlockSpec((EXPERT_TILE, D_MODEL), lambda i, te, nt: (jnp.minimum(i, nt[0] - 1), 0))
    wspec = lambda shape: pl.BlockSpec((None, None) + shape, lambda i, te, nt: (l, te[i], 0, 0))
    return pl.pallas_call(
        _expert_kernel,
        grid_spec=pltpu.PrefetchScalarGridSpec(
            num_scalar_prefetch=2, grid=(MAX_TILES,),
            in_specs=[rows, wspec((D_MODEL, D_EXPERT)), wspec((D_MODEL, D_EXPERT)), wspec((D_EXPERT, D_MODEL))],
            out_specs=rows,
            scratch_shapes=[pltpu.VMEM((D_MODEL, D_EXPERT), BF16), pltpu.VMEM((D_MODEL, D_EXPERT), BF16),
                            pltpu.VMEM((D_EXPERT, D_MODEL), BF16)]),
        out_shape=jax.ShapeDtypeStruct((SORTED_ROWS, D_MODEL), BF16),
        input_output_aliases={2: 0},
        compiler_params=_params("arbitrary"),
        name="experts",
    )(plan['tile_expert'], plan['n_tiles'], xs, w_gate, w_up, w_down)


def _combine_kernel(*refs, final):
    refs = list(refs)
    dest_ref, seg_ref, local_ref, total_ref = refs[:4]
    x1_ref, ys_hbm, gate_ref, rank_ref, mod_ref, local_row_ref, local_col_ref, seg_col_ref = refs[4:12]
    rest = refs[12:]
    fg_ref = rest.pop(0) if final else None
    outs, (buf, sems) = rest[:-2], rest[-2:]
    b = pl.program_id(0)
    slot = b % 2

    def fetch(blk, s):
        for e in range(N_EXPERTS):
            n = pl.multiple_of(seg_ref[blk, e], SEG_ALIGN)
            src = pl.multiple_of(dest_ref[blk, e], SEG_ALIGN)
            dst = pl.multiple_of(local_ref[blk, e], SEG_ALIGN)
            pltpu.make_async_copy(ys_hbm.at[pl.ds(src, n)], buf.at[s, pl.ds(dst, n)], sems.at[s]).start()

    @pl.when(b == 0)
    def _():
        buf[...] = jnp.zeros_like(buf)
        fetch(0, 0)

    def wait_rows(blk, s):
        n_rows = pl.multiple_of(total_ref[blk], SEG_ALIGN)
        pltpu.make_async_copy(ys_hbm.at[pl.ds(0, n_rows)], buf.at[s, pl.ds(0, n_rows)], sems.at[s]).wait()

    wait_rows(b, slot)
    nxt = jnp.minimum(b + 1, N_BLK - 1)
    fetch(nxt, 1 - slot)

    gate = gate_ref[...].T
    hi, lo = _local_dest_digits(rank_ref[...].T, local_row_ref[:, :N_EXPERTS])
    lhs = jnp.concatenate([jnp.concatenate([hi, lo], axis=1),
                           jnp.concatenate([jnp.zeros_like(gate), gate], axis=1)], axis=0).astype(BF16)
    lstart = local_col_ref[...]
    lend = lstart + seg_col_ref[...]
    routed = jnp.zeros((ROW_TILE, D_MODEL), F32)
    r = lax.broadcasted_iota(jnp.int32, (N_EXPERTS, LOCAL_CHUNK), 1).astype(F32)
    rr = lax.broadcasted_iota(jnp.int32, (ROW_TILE, LOCAL_CHUNK), 1).astype(F32)
    def gather(c, acc):
        first = float(c * LOCAL_CHUNK)
        member = (r >= lstart - first) & (r < lend - first)
        table = jnp.concatenate([jnp.where(member, 64.0, 0.0), jnp.where(member, 1.0, 0.0)], axis=0).astype(BF16)
        dg = _dot(lhs, table)
        weights = jnp.where(dg[:ROW_TILE] == rr + first, dg[ROW_TILE:], 0.0).astype(BF16)
        return acc + _dot(weights, buf[slot, c * LOCAL_CHUNK:(c + 1) * LOCAL_CHUNK, :])

    for c in range(LOCAL_ROWS // LOCAL_CHUNK):
        routed = gather(c, routed)
    x = x1_ref[...] + mod_ref[:, 5 * D_MODEL:6 * D_MODEL] * routed
    if final:
        y = _rms(x, fg_ref[...])

        @pl.when(b < CTX_TILES)
        def _():
            outs[0][...] = y

        @pl.when(b >= CTX_TILES)
        def _():
            outs[1][...] = y
    else:
        outs[0][...] = x

    @pl.when(b == N_BLK - 1)
    def _():
        wait_rows(nxt, 1 - slot)


def _combine(x1, ys, gate_t, rank_t, mod, plan, final_g):
    final = final_g is not None
    blk = lambda shape, imap: pl.BlockSpec(shape, imap)
    xspec = blk((ROW_TILE, D_MODEL), lambda b, *_: (b, 0))
    if final:
        out_specs = _ctx_lat_specs(D_MODEL)
        out_shape = [jax.ShapeDtypeStruct((N_CTX, D_MODEL), F32), jax.ShapeDtypeStruct((N_LAT, D_MODEL), F32)]
    else:
        out_specs, out_shape = xspec, jax.ShapeDtypeStruct((N_TOK, D_MODEL), F32)
    et_spec = blk((N_EXPERTS, ROW_TILE), lambda b, *_: (0, b))
    in_specs = [xspec, pl.BlockSpec(memory_space=pl.ANY), et_spec, et_spec,
                blk((None, 1, 6 * D_MODEL), lambda b, *_: (_mod_row(b), 0, 0)),
                blk((None, 1, 128), lambda b, *_: (b, 0, 0)),
                blk((None, N_EXPERTS, 1), lambda b, *_: (b, 0, 0)),
                blk((None, N_EXPERTS, 1), lambda b, *_: (b, 0, 0))]
    args = [x1, ys, gate_t, rank_t, mod, plan['local_row'], plan['local_col'], plan['seg_col']]
    if final:
        in_specs.append(blk((1, D_MODEL), lambda b, *_: (0, 0)))
        args.append(final_g)
    return pl.pallas_call(
        functools.partial(_combine_kernel, final=final),
        grid_spec=pltpu.PrefetchScalarGridSpec(
            num_scalar_prefetch=4, grid=(N_BLK,),
            in_specs=in_specs, out_specs=out_specs,
            scratch_shapes=[pltpu.VMEM((2, LOCAL_ROWS, D_MODEL), BF16), pltpu.SemaphoreType.DMA((2,))]),
        out_shape=out_shape,
        compiler_params=_params("arbitrary"),
        name="combine",
    )(plan['dest'], plan['seg'], plan['local'], plan['total'], *args)


def _rope_full_tables(dim, n_rep):
    rows = DEC_SEQ // GRID_W
    r_idx, c_idx = np.meshgrid(np.arange(rows), np.arange(GRID_W), indexing='ij')
    pos = np.stack([r_idx.reshape(-1), c_idx.reshape(-1)], axis=-1).astype(np.float32)
    nf = dim // 4
    inv = np.float32(ROPE_BASE) ** (-np.arange(nf, dtype=np.float32) / np.float32(nf))
    ang = (pos[:, :, None] * inv).astype(np.float32)
    ang = np.repeat(ang.reshape(DEC_SEQ, 2 * nf), 2, axis=1)
    sign = np.tile(np.asarray([-1.0, 1.0], np.float32), dim // 2)
    cos = np.tile(np.cos(ang.astype(np.float64)), (1, n_rep))
    sin = np.tile(np.sin(ang.astype(np.float64)) * sign, (1, n_rep))
    return jnp.asarray(cos, F32), jnp.asarray(sin, F32)


def _pack_w_in(w):
    c0 = MLA_Q_LORA + MLA_KV_LORA
    kr = w[..., c0:MLA_IN]
    s0 = MLA_IN + HG_IN + FN_IN
    qh = [w[..., s0 + h * SWA_HD:s0 + (h + 1) * SWA_HD] for h in SWA_STACK_ORDER]
    return jnp.concatenate([w[..., :c0], kr, kr, kr, kr, w[..., MLA_IN:s0]] + qh
                           + [w[..., s0 + SWA_QW:]], axis=-1)


def _pack_w_q_up(w):
    hd = MLA_NOPE + MLA_ROPE
    nope = [w[:, h * hd:h * hd + MLA_NOPE] for h in range(MLA_HEADS)]
    rope = [w[:, h * hd + MLA_NOPE:(h + 1) * hd] for h in range(MLA_HEADS)]
    return jnp.concatenate(nope + rope, axis=1).astype(BF16)


def _pack_w_kv_up(w):
    hd = MLA_NOPE + MLA_V
    kn = [w[:, h * hd:h * hd + MLA_NOPE] for h in range(MLA_HEADS)]
    vv = [w[:, h * hd + MLA_NOPE:(h + 1) * hd] for h in range(MLA_HEADS)]
    return jnp.concatenate(kn + vv, axis=1).astype(BF16)


def _pack_w_out(w):
    s0 = 3 * 256
    rows = [w[..., s0 + h * SWA_HD:s0 + (h + 1) * SWA_HD, :] for h in SWA_STACK_ORDER]
    return jnp.concatenate([w[..., :s0, :]] + rows, axis=-2).astype(BF16)


def kernel(x_prompt, x_sample, c, cache_mla_ckv, cache_mla_krope, cache_swa_k, cache_swa_v, state_hgrn,
           c_ctx, w_ada, b_ada, norm1_g, norm2_g, w_in, mla_q_norm_g, mla_w_q_up, mla_kv_norm_g, mla_w_kv_up,
           hg_lb_logits, hg_norm_g, fn_w, swa_sink, w_out, moe_w_router, moe_b_router, moe_w_gate, moe_w_up,
           moe_w_down, sh_w_gate, sh_w_up, sh_w_down, final_norm_g):
    x_parts = (x_prompt.reshape(N_CTX, D_MODEL), x_sample.reshape(N_LAT, D_MODEL))
    cv8 = jnp.concatenate([c_ctx[None, :], c, jnp.zeros((8 - 1 - DEC_BATCH, D_MODEL), F32)], axis=0)
    mods = _ada(cv8, w_ada, b_ada).reshape(DEPTH, 8, 1, 6 * D_MODEL)

    lb = jnp.cumsum(jax.nn.softmax(hg_lb_logits.astype(F32), axis=1), axis=1)
    lb = lb - lb[:, :1]

    cos_m, sin_m = _rope_full_tables(MLA_ROPE, MLA_HEADS)
    cos_q, sin_q = _rope_full_tables(SWA_HD, SWA_HEADS)
    cos_k, sin_k = cos_q[:, :SWA_KW], sin_q[:, :SWA_KW]
    cache_k = cache_swa_k.reshape(DEC_BATCH, DEPTH, PAST_LEN, SWA_KW)
    cache_v = cache_swa_v.reshape(DEC_BATCH, DEPTH, PAST_LEN, SWA_KW)
    state_t = jnp.swapaxes(state_hgrn, -1, -2)

    ctx_blk_lat = N_CTX // DEC_SEQ
    w_in_p, w_out_p = _pack_w_in(w_in), _pack_w_out(w_out)
    new_ckv, new_kr, new_k, new_v, new_st = [], [], [], [], []
    for l in range(DEPTH):
        u_mla, u_hg, u_fn, u_swa = _in_proj(x_parts, mods[l], norm1_g[l][None], w_in_p, l)

        qg, kvg = mla_q_norm_g[l][None], mla_kv_norm_g[l][None]
        wq, wkv = _pack_w_q_up(mla_w_q_up[l]), _pack_w_kv_up(mla_w_kv_up[l])
        o_mla_c, ckv_c = _mla_ctx(u_mla, qg, wq, kvg, wkv)
        o_mla_l = _mla_lat(u_mla, cache_mla_ckv, cache_mla_krope, l, cos_m, sin_m, qg, wq, kvg, wkv)

        lbf, lbb = lb[0, l][None], lb[1, l][None]
        ng4 = jnp.tile(hg_norm_g[l], HG_HEADS)[None]
        o_hg_c, st_c = _hgrn(u_hg, lbf, lbb, ng4, None, seq=SEQ, n_batch=BATCH, row_block0=0)
        o_hg_l = _hgrn(u_hg, lbf, lbb, ng4, state_t[:, l], seq=DEC_SEQ, n_batch=DEC_BATCH,
                       row_block0=ctx_blk_lat)

        fw = fn_w[l].astype(BF16)
        o_fn_c = _fourier(u_fn, fw, seq=SEQ, n_batch=BATCH, row_block0=0)
        o_fn_l = _fourier(u_fn, fw, seq=DEC_SEQ, n_batch=DEC_BATCH, row_block0=ctx_blk_lat)

        sink = swa_sink[l]
        o_swa_c = _swa_ctx(u_swa, sink)
        o_swa_l = _swa_lat(u_swa, cache_k, cache_v, l, sink, cos_q, sin_q, cos_k, sin_k)

        x1, h2, gate_t, rank_t, cnt = _out_proj(
            x_parts, ((o_mla_c, o_mla_l), (o_hg_c, o_hg_l), (o_fn_c, o_fn_l), (o_swa_c, o_swa_l)),
            mods[l], norm2_g[l][None], w_out_p, l,
            moe_w_router[l], moe_b_router[l][:, None],
            sh_w_gate[l].astype(BF16), sh_w_up[l].astype(BF16), sh_w_down[l].astype(BF16))
        plan = _segment_plan(cnt)
        xs = _dispatch(h2, rank_t, plan)
        ys = _experts(xs, plan, moe_w_gate, moe_w_up, moe_w_down, l)
        if l < DEPTH - 1:
            x_parts = (_combine(x1, ys, gate_t, rank_t, mods[l], plan, None),)
        else:
            y_prompt, y_sample = _combine(x1, ys, gate_t, rank_t, mods[l], plan, final_norm_g[None])

        new_ckv.append(ckv_c.reshape(BATCH, SEQ, MLA_KV_LORA))
        new_kr.append(u_mla[:N_CTX, MLA_Q_LORA + MLA_KV_LORA:MLA_IN].reshape(BATCH, SEQ, MLA_ROPE))
        new_k.append(u_swa[:N_CTX, SWA_QW:SWA_QW + SWA_KW].reshape(BATCH, SEQ, SWA_KV_HEADS, SWA_HD))
        new_v.append(u_swa[:N_CTX, SWA_QW + SWA_KW:].reshape(BATCH, SEQ, SWA_KV_HEADS, SWA_HD))
        new_st.append(jnp.swapaxes(st_c, -1, -2))

    y_prompt = y_prompt.reshape(BATCH, SEQ, D_MODEL)
    y_sample = y_sample.reshape(DEC_BATCH, DEC_SEQ, D_MODEL)
    stack = lambda xs: jnp.stack(xs, axis=1)
    return (y_prompt, y_sample, stack(new_ckv), stack(new_kr), stack(new_k), stack(new_v), stack(new_st))
```

```python
import functools

import numpy as np
import jax
import jax.numpy as jnp
from jax import lax
from jax.experimental import pallas as pl
from jax.experimental.pallas import tpu as pltpu

F32 = jnp.float32
BF16 = jnp.bfloat16

D_MODEL = 1024
BATCH = 32
SEQ = 256
DEPTH = 2
DEC_BATCH = 2
DEC_SEQ = 1024
PAST_LEN = 256
GRID_W = 64
EPS = 1e-6
ROPE_BASE = 10000.0
NEG_INF = -1e30

MLA_HEADS = 4
MLA_NOPE = 64
MLA_ROPE = 32
MLA_V = 64
MLA_Q_LORA = 256
MLA_KV_LORA = 128
HG_HEADS = 4
HG_DK = 64
HG_DV = 64
HG_W = HG_HEADS * HG_DK
FN_GROUPS = 4
FN_WIDTH = 256
SWA_HEADS = 4
SWA_KV_HEADS = 2
SWA_HD = 64
WINDOW = 128
N_EXPERTS = 64
TOP_K = 6
D_EXPERT = 256
D_SHARED = 256
ROUTE_SCALE = 2.5

MLA_IN = MLA_Q_LORA + MLA_KV_LORA + MLA_ROPE
HG_IN = 3 * HG_HEADS * HG_DK + 2 * HG_HEADS * HG_DV
FN_IN = FN_WIDTH
SWA_IN = (SWA_HEADS + 2 * SWA_KV_HEADS) * SWA_HD

N_CTX = BATCH * SEQ
N_LAT = DEC_BATCH * DEC_SEQ
N_TOK = N_CTX + N_LAT

MLA_PACK = 512
U_COLS = MLA_PACK + HG_IN + FN_IN + SWA_IN

ROW_TILE = 256
CTX_TILES = N_CTX // ROW_TILE
LAT_TILES_PER_BATCH = DEC_SEQ // ROW_TILE
HG_CHUNK = 32
HG_BLOCK = 256
SWA_QBLK = 128
MLA_QBLK = 256
CTX_PER_STEP = 4
SWA_CTX_PER_STEP = 8
HG_CTX_PER_STEP = 4
OUT_ROWS = 2 * ROW_TILE
VMEM_LIMIT = 56 * 1024 * 1024


def _dot(a, b):
    return jnp.dot(a, b, preferred_element_type=F32)


def _dot_nt(a, b):
    return lax.dot_general(a, b, (((1,), (1,)), ((), ())), preferred_element_type=F32)


def _dot_tn(a, b):
    return lax.dot_general(a, b, (((0,), (0,)), ((), ())), preferred_element_type=F32)


def _split3(x):
    hi = x.astype(BF16)
    r1 = x - hi.astype(F32)
    mid = r1.astype(BF16)
    return hi, mid, (r1 - mid.astype(F32)).astype(BF16)


def _dot_exact_lhs(a, b):
    ab = a.astype(BF16)
    hi, mid, lo = _split3(b)
    return (_dot(ab, lo) + _dot(ab, mid)) + _dot(ab, hi)


def _dot_exact_rhs(a, b):
    bb = b.astype(BF16)
    hi, mid, lo = _split3(a)
    return (_dot(lo, bb) + _dot(mid, bb)) + _dot(hi, bb)


def _rms(x, g):
    return x * lax.rsqrt(jnp.mean(x * x, axis=-1, keepdims=True) + EPS) * g


def _silu(x):
    return x * jax.nn.sigmoid(x)


def _mod_row(i):
    return jnp.where(i < CTX_TILES, 0, 1 + (i - CTX_TILES) // LAT_TILES_PER_BATCH)


def _params(*sem):
    return pltpu.CompilerParams(dimension_semantics=sem, vmem_limit_bytes=VMEM_LIMIT)


ADA_COLS = 1536


def _ada_kernel(cv_ref, w_ref, b_ref, o_ref):
    a = _silu(cv_ref[...]).astype(BF16)
    o_ref[...] = _dot(a, w_ref[...].astype(BF16)) + b_ref[...]


def _ada(cv8, w_ada, b_ada):
    return pl.pallas_call(
        _ada_kernel,
        grid=(DEPTH, 6 * D_MODEL // ADA_COLS),
        in_specs=[
            pl.BlockSpec((8, D_MODEL), lambda l, j: (0, 0)),
            pl.BlockSpec((None, D_MODEL, ADA_COLS), lambda l, j: (l, 0, j)),
            pl.BlockSpec((None, 1, ADA_COLS), lambda l, j: (l, 0, j)),
        ],
        out_specs=pl.BlockSpec((None, 8, ADA_COLS), lambda l, j: (l, 0, j)),
        out_shape=jax.ShapeDtypeStruct((DEPTH, 8, 6 * D_MODEL), F32),
        compiler_params=_params("arbitrary", "arbitrary"),
        name="ada",
    )(cv8, w_ada, b_ada.reshape(DEPTH, 1, 6 * D_MODEL))


def _ctx_lat_specs(width, tile=ROW_TILE):
    n_ctx = N_CTX // tile
    return [pl.BlockSpec((tile, width), lambda i, *_: (jnp.minimum(i, n_ctx - 1), 0)),
            pl.BlockSpec((tile, width), lambda i, *_: (jnp.maximum(i - n_ctx, 0), 0))]


def _row_specs(parts, width, tile=ROW_TILE):
    if len(parts) == 2:
        return _ctx_lat_specs(width, tile)
    return [pl.BlockSpec((tile, width), lambda i, *_: (i, 0))]


def _read_rows(refs, tile=ROW_TILE):
    if len(refs) == 1:
        return refs[0][...]
    return jnp.where(pl.program_id(0) < N_CTX // tile, refs[0][...], refs[1][...])


def _in_kernel(*refs, n_x):
    x = _read_rows(refs[:n_x], OUT_ROWS)
    mod_ref, g_ref, w_ref, umla_ref, uhg_ref, ufn_ref, uswa_ref = refs[n_x:]
    sh1 = mod_ref[:, 0:D_MODEL]
    sc1 = mod_ref[:, D_MODEL:2 * D_MODEL]
    h = _rms(x, g_ref[...]) * (1.0 + sc1) + sh1
    u = _dot(h.astype(BF16), w_ref[...])
    o = 0
    for ref, width in ((umla_ref, MLA_PACK), (uhg_ref, HG_IN), (ufn_ref, FN_IN), (uswa_ref, SWA_IN)):
        ref[...] = u[:, o:o + width].astype(ref.dtype)
        o += width


def _in_proj(x_parts, mod, g, w, l):
    row = lambda i: (i, 0)
    widths = (MLA_PACK, HG_IN, FN_IN, SWA_IN)
    return pl.pallas_call(
        functools.partial(_in_kernel, n_x=len(x_parts)),
        grid=(N_TOK // OUT_ROWS,),
        in_specs=_row_specs(x_parts, D_MODEL, OUT_ROWS) + [
            pl.BlockSpec((None, 1, 6 * D_MODEL), lambda i: (_mod_row(i * (OUT_ROWS // ROW_TILE)), 0, 0)),
            pl.BlockSpec((1, D_MODEL), lambda i: (0, 0)),
            pl.BlockSpec((None, D_MODEL, U_COLS), lambda i: (l, 0, 0))],
        out_specs=[pl.BlockSpec((OUT_ROWS, wd), row) for wd in widths],
        out_shape=[jax.ShapeDtypeStruct((N_TOK, wd), F32) for wd in widths],
        compiler_params=_params("arbitrary"),
        name="in_proj",
    )(*x_parts, mod, g, w)


def _rope(x, cos, sin_signed):
    lane = lax.broadcasted_iota(jnp.int32, x.shape, 1)
    width = x.shape[1]
    swapped = jnp.where(lane % 2 == 0, pltpu.roll(x, width - 1, 1), pltpu.roll(x, 1, 1))
    return x * cos + swapped * sin_signed


def _stack_heads(x, n_heads, head_w):
    lane = lax.broadcasted_iota(jnp.int32, x.shape, 1)
    return jnp.concatenate([jnp.where(lane // head_w == h, x, 0.0) for h in range(n_heads)], axis=0)


def _unstack_heads(o, n_heads, head_w):
    t = o.shape[0] // n_heads
    lane = lax.broadcasted_iota(jnp.int32, (t, o.shape[1]), 1)
    out = jnp.zeros((t, o.shape[1]), F32)
    for h in range(n_heads):
        out = jnp.where(lane // head_w == h, o[h * t:(h + 1) * t], out)
    return out


MLA_SCALE = (MLA_NOPE + MLA_ROPE) ** -0.5
MLA_QW = MLA_HEADS * MLA_NOPE + MLA_HEADS * MLA_ROPE
MLA_NW = MLA_HEADS * MLA_NOPE


def _mla_attend(q, kcat, v):
    qs = jnp.concatenate([_stack_heads(q[:, :MLA_NW], MLA_HEADS, MLA_NOPE),
                          _stack_heads(q[:, MLA_NW:], MLA_HEADS, MLA_ROPE)], axis=1)
    s = _dot_nt(qs.astype(BF16), kcat) * MLA_SCALE
    p = jnp.exp(s - jnp.max(s, axis=-1, keepdims=True))
    o = _dot(p.astype(BF16), v) / jnp.sum(p, axis=-1, keepdims=True)
    return _unstack_heads(o, MLA_HEADS, MLA_V)


def _mla_ctx_kernel(u_ref, qg_ref, wq_ref, kvg_ref, wkv_ref, o_ref, ckv_ref):
    for j in range(CTX_PER_STEP):
        rows = slice(j * SEQ, (j + 1) * SEQ)
        u = u_ref[rows, :]
        q = _dot(_rms(u[:, :MLA_Q_LORA], qg_ref[...]).astype(BF16), wq_ref[...])
        ckv = _rms(u[:, MLA_Q_LORA:MLA_Q_LORA + MLA_KV_LORA], kvg_ref[...])
        ckv_ref[rows, :] = ckv
        kv = _dot(ckv.astype(BF16), wkv_ref[...])
        kr4 = u[:, MLA_Q_LORA + MLA_KV_LORA:]
        kcat = jnp.concatenate([kv[:, :MLA_NW], kr4], axis=1).astype(BF16)
        o_ref[rows, :] = _mla_attend(q, kcat, kv[:, MLA_NW:].astype(BF16)).astype(o_ref.dtype)


def _mla_ctx(u_mla, qg, wq, kvg, wkv):
    full = lambda shape: pl.BlockSpec(shape, lambda b: (0, 0))
    rows = CTX_PER_STEP * SEQ
    return pl.pallas_call(
        _mla_ctx_kernel,
        grid=(BATCH // CTX_PER_STEP,),
        in_specs=[pl.BlockSpec((rows, MLA_PACK), lambda b: (b, 0)),
                  full((1, MLA_Q_LORA)), full((MLA_Q_LORA, MLA_QW)),
                  full((1, MLA_KV_LORA)), full((MLA_KV_LORA, 2 * MLA_NW))],
        out_specs=[pl.BlockSpec((rows, MLA_NW), lambda b: (b, 0)),
                   pl.BlockSpec((rows, MLA_KV_LORA), lambda b: (b, 0))],
        out_shape=[jax.ShapeDtypeStruct((N_CTX, MLA_NW), BF16),
                   jax.ShapeDtypeStruct((N_CTX, MLA_KV_LORA), F32)],
        compiler_params=_params("arbitrary"),
        name="mla_ctx",
    )(u_mla, qg, wq, kvg, wkv)


MLA_TK = PAST_LEN + DEC_SEQ


def _mla_lat_kernel(u_ref, cckv_ref, ckr_ref, cos_ref, sin_ref, qg_ref, wq_ref, kvg_ref, wkv_ref,
                    o_ref, kcat_s, v_s):
    i = pl.program_id(1)

    @pl.when(i == 0)
    def _():
        u = u_ref[...]
        ckv_new = _rms(u[:, MLA_Q_LORA:MLA_Q_LORA + MLA_KV_LORA], kvg_ref[...])
        ckv_all = jnp.concatenate([cckv_ref[...], ckv_new], axis=0)
        kv = _dot(ckv_all.astype(BF16), wkv_ref[...])
        kr_new = _rope(u[:, MLA_Q_LORA + MLA_KV_LORA:], cos_ref[...], sin_ref[...])
        ckr = ckr_ref[...]
        kr_all = jnp.concatenate([jnp.concatenate([ckr] * MLA_HEADS, axis=1), kr_new], axis=0)
        kcat_s[...] = jnp.concatenate([kv[:, :MLA_NW], kr_all], axis=1).astype(BF16)
        v_s[...] = kv[:, MLA_NW:].astype(BF16)

    r0 = pl.multiple_of(i * MLA_QBLK, MLA_QBLK)
    cq = u_ref[pl.ds(r0, MLA_QBLK), 0:MLA_Q_LORA]
    q = _dot(_rms(cq, qg_ref[...]).astype(BF16), wq_ref[...])
    qr = _rope(q[:, MLA_NW:], cos_ref[pl.ds(r0, MLA_QBLK), :], sin_ref[pl.ds(r0, MLA_QBLK), :])
    q = jnp.concatenate([q[:, :MLA_NW], qr], axis=1)
    o_ref[...] = _mla_attend(q, kcat_s[...], v_s[...]).astype(o_ref.dtype)


def _mla_lat(u_mla, cache_ckv, cache_kr, l, cos, sin, qg, wq, kvg, wkv):
    full = lambda shape: pl.BlockSpec(shape, lambda b, i: (0, 0))
    nq = DEC_SEQ // MLA_QBLK
    return pl.pallas_call(
        _mla_lat_kernel,
        grid=(DEC_BATCH, nq),
        in_specs=[pl.BlockSpec((DEC_SEQ, MLA_PACK), lambda b, i: (N_CTX // DEC_SEQ + b, 0)),
                  pl.BlockSpec((None, None, PAST_LEN, MLA_KV_LORA), lambda b, i: (b, l, 0, 0)),
                  pl.BlockSpec((None, None, PAST_LEN, MLA_ROPE), lambda b, i: (b, l, 0, 0)),
                  full((DEC_SEQ, MLA_HEADS * MLA_ROPE)), full((DEC_SEQ, MLA_HEADS * MLA_ROPE)),
                  full((1, MLA_Q_LORA)), full((MLA_Q_LORA, MLA_QW)),
                  full((1, MLA_KV_LORA)), full((MLA_KV_LORA, 2 * MLA_NW))],
        out_specs=pl.BlockSpec((MLA_QBLK, MLA_NW), lambda b, i: (b * nq + i, 0)),
        out_shape=jax.ShapeDtypeStruct((N_LAT, MLA_NW), BF16),
        scratch_shapes=[pltpu.VMEM((MLA_TK, MLA_QW), BF16), pltpu.VMEM((MLA_TK, MLA_NW), BF16)],
        compiler_params=_params("arbitrary", "arbitrary"),
        name="mla_lat",
    )(u_mla, cache_ckv, cache_kr, cos, sin, qg, wq, kvg, wkv)


def _hgrn_kernel(*refs, seq, n_seq, has_state):
    if has_state:
        (u_ref, lbf_ref, lbb_ref, ng_ref, s0_ref, o_ref,
         q_s, kf_s, gf_s, kb_s, gb_s, of_s, ob_s, stf_s, stb_s) = refs
    else:
        (u_ref, lbf_ref, lbb_ref, ng_ref, o_ref, so_ref,
         q_s, kf_s, gf_s, kb_s, gb_s, of_s, ob_s, stf_s, stb_s) = refs
    C = HG_CHUNK
    W = HG_W

    q_s[...] = _silu(u_ref[:, 0:W])
    ff = lbf_ref[...] + (1.0 - lbf_ref[...]) * jax.nn.sigmoid(u_ref[:, W:2 * W])
    kf_s[...] = 1.0 - ff
    gf_s[...] = jnp.log(ff)
    fb = lbb_ref[...] + (1.0 - lbb_ref[...]) * jax.nn.sigmoid(u_ref[:, 2 * W:3 * W])
    kb_s[...] = 1.0 - fb
    gb_s[...] = jnp.log(fb)

    rr = lax.broadcasted_iota(jnp.int32, (W, W), 0)
    cc = lax.broadcasted_iota(jnp.int32, (W, W), 1)
    blockdiag = rr // HG_DK == cc // HG_DK
    if has_state:
        for st, d in ((stf_s, 0), (stb_s, 1)):
            rows = []
            for h in range(HG_HEADS):
                z = lambda n: jnp.zeros((HG_DV, n * HG_DK), F32)
                parts = ([z(h)] if h else []) + [s0_ref[d, h]] + ([z(HG_HEADS - 1 - h)] if h < HG_HEADS - 1 else [])
                rows.append(jnp.concatenate(parts, axis=1) if len(parts) > 1 else parts[0])
            st[0] = jnp.concatenate(rows, axis=0)
    else:
        stf_s[...] = jnp.zeros_like(stf_s)
        stb_s[...] = jnp.zeros_like(stb_s)

    B = HG_BLOCK
    per_block = B // C
    n_blocks = seq // B
    ri = lax.broadcasted_iota(jnp.int32, (B, B), 0)
    ci = lax.broadcasted_iota(jnp.int32, (B, B), 1)
    same_chunk = ri // C == ci // C
    rs = lax.broadcasted_iota(jnp.int32, (HG_HEADS * B, B), 0) % B
    cs = lax.broadcasted_iota(jnp.int32, (HG_HEADS * B, B), 1)
    same_chunk_s = rs // C == cs // C

    sums_f = jnp.where(same_chunk & (ci <= ri), 1.0, 0.0)
    sums_b = jnp.where(same_chunk & (ci >= ri), 1.0, 0.0)
    keep_f = same_chunk_s & (rs >= cs)
    keep_b = same_chunk_s & (cs >= rs)

    def chunk_row(x, i):
        x3 = x.reshape(per_block, C, W)
        return jnp.broadcast_to(x3[:, i:i + 1, :], (per_block, C, W)).reshape(B, W)

    def block(r, k_s, g_s, o_s, st_s, sum_mat, keep, order, mid, far):
        q = q_s[pl.ds(r, B), :]
        k = k_s[pl.ds(r, B), :]
        v = u_ref[pl.ds(r, B), 3 * W:4 * W].astype(BF16)
        G = _dot_exact_lhs(sum_mat, g_s[pl.ds(r, B), :])
        Gq = G - chunk_row(G, mid)
        Gk2 = chunk_row(G, far) - G
        qe = _stack_heads(q * jnp.exp(Gq), HG_HEADS, HG_DK)
        ke = k * jnp.exp(-Gq)
        A = jnp.where(keep, _dot_nt(qe.astype(BF16), ke.astype(BF16)), 0.0)
        o_intra = _unstack_heads(_dot(A.astype(BF16), v), HG_HEADS, HG_DV)
        qg = (q * jnp.exp(G)).astype(BF16)
        k2 = (k * jnp.exp(Gk2)).astype(BF16)
        decay = jnp.exp(G + Gk2)
        st = st_s[...]
        o_inter = [None] * per_block
        for c in order:
            rows = slice(c * C, (c + 1) * C)
            o_inter[c] = _dot_nt(st.astype(BF16), qg[rows])
            st = st * decay[c * C:c * C + 1] + jnp.where(blockdiag, _dot_tn(v[rows], k2[rows]), 0.0)
        st_s[...] = st
        o_s[pl.ds(r, B), :] = o_intra + jnp.concatenate(o_inter, axis=1).T

    def fwd(j, r):
        block(r, kf_s, gf_s, of_s, stf_s.at[j], sums_f, keep_f, range(per_block), C // 2 - 1, C - 1)

    def bwd(j, r):
        block(r, kb_s, gb_s, ob_s, stb_s.at[j], sums_b, keep_b, range(per_block - 1, -1, -1), C // 2, 0)

    for j in range(n_seq):
        if n_blocks == 1:
            fwd(j, j * seq)
            bwd(j, j * seq)
        else:
            def both_directions(i, carry, j=j):
                fwd(j, pl.multiple_of(j * seq + i * B, B))
                bwd(j, pl.multiple_of(j * seq + (n_blocks - 1 - i) * B, B))
                return carry

            lax.fori_loop(0, n_blocks, both_directions, 0)

    o = of_s[...] + ob_s[...]
    ms = _dot_exact_rhs(o * o, jnp.where(blockdiag, 1.0 / HG_DV, 0.0))
    on = o * lax.rsqrt(ms + EPS) * ng_ref[...]
    o_ref[...] = (on * _silu(u_ref[:, 4 * W:5 * W])).astype(o_ref.dtype)

    if not has_state:
        for j in range(n_seq):
            for st, d in ((stf_s, 0), (stb_s, 1)):
                for h in range(HG_HEADS):
                    so_ref[j, d, h] = st[j, h * HG_DV:(h + 1) * HG_DV, h * HG_DK:(h + 1) * HG_DK]


def _hgrn(u_hg, lbf, lbb, ng4, state_t, *, seq, n_batch, row_block0):
    has_state = state_t is not None
    n_seq = 1 if has_state else HG_CTX_PER_STEP
    rows = n_seq * seq
    full = lambda shape: pl.BlockSpec(shape, lambda b: (0, 0))
    in_specs = [pl.BlockSpec((rows, HG_IN), lambda b: (row_block0 + b, 0)),
                full((1, HG_W)), full((1, HG_W)), full((1, HG_W))]
    args = [u_hg, lbf, lbb, ng4]
    o_spec = pl.BlockSpec((rows, HG_W), lambda b: (b, 0))
    o_shape = jax.ShapeDtypeStruct((n_batch * seq, HG_W), BF16)
    if has_state:
        in_specs.append(pl.BlockSpec((None, 2, HG_HEADS, HG_DV, HG_DK), lambda b: (b, 0, 0, 0, 0)))
        args.append(state_t)
        out_specs, out_shape = o_spec, o_shape
    else:
        out_specs = [o_spec, pl.BlockSpec((n_seq, 2, HG_HEADS, HG_DV, HG_DK), lambda b: (b, 0, 0, 0, 0))]
        out_shape = [o_shape, jax.ShapeDtypeStruct((n_batch, 2, HG_HEADS, HG_DV, HG_DK), F32)]
    return pl.pallas_call(
        functools.partial(_hgrn_kernel, seq=seq, n_seq=n_seq, has_state=has_state),
        grid=(n_batch // n_seq,),
        in_specs=in_specs, out_specs=out_specs, out_shape=out_shape,
        scratch_shapes=[pltpu.VMEM((rows, HG_W), F32)] * 7 + [pltpu.VMEM((n_seq, HG_W, HG_W), F32)] * 2,
        compiler_params=_params("arbitrary"),
        name="hgrn_lat" if has_state else "hgrn_ctx",
    )(*args)


def _dft_tables(n):
    j = np.arange(n, dtype=np.int64)
    ang = 2.0 * np.pi * ((j[:, None] * j[None, :]) % n).astype(np.float64) / n
    return np.cos(ang) / np.sqrt(n), np.sin(ang) / np.sqrt(n)


def _fourier_tables(seq):
    gw = FN_WIDTH // FN_GROUPS
    cg, sg = _dft_tables(gw)
    eye = np.eye(FN_GROUPS)
    chan = np.concatenate([np.kron(eye, cg), np.kron(eye, sg)], axis=1)
    ct, st = _dft_tables(seq)
    pos = np.concatenate([ct, -st], axis=1)
    return jnp.asarray(chan, F32).astype(BF16), jnp.asarray(pos, F32).astype(BF16)


def _fourier_kernel(x_ref, chan_ref, pos_ref, w_ref, o_ref, *, seq, n_seq):
    for j in range(n_seq):
        rows = slice(j * seq, (j + 1) * seq)
        x12 = _dot(x_ref[rows, :].astype(BF16), chan_ref[...])
        z = jnp.concatenate([x12[:, :FN_WIDTH], x12[:, FN_WIDTH:]], axis=0).astype(BF16)
        y = _dot(pos_ref[...], z)
        o_ref[rows, :] = _dot(y.astype(BF16), w_ref[...]).astype(o_ref.dtype)


def _fourier(u_fn, w, *, seq, n_batch, row_block0):
    chan, pos = _fourier_tables(seq)
    full = lambda shape: pl.BlockSpec(shape, lambda b: (0, 0))
    n_seq = CTX_PER_STEP if seq == SEQ else 1
    return pl.pallas_call(
        functools.partial(_fourier_kernel, seq=seq, n_seq=n_seq),
        grid=(n_batch // n_seq,),
        in_specs=[pl.BlockSpec((n_seq * seq, FN_WIDTH), lambda b: (row_block0 + b, 0)),
                  full((FN_WIDTH, 2 * FN_WIDTH)), full((seq, 2 * seq)), full((FN_WIDTH, FN_WIDTH))],
        out_specs=pl.BlockSpec((n_seq * seq, FN_WIDTH), lambda b: (b, 0)),
        out_shape=jax.ShapeDtypeStruct((n_batch * seq, FN_WIDTH), BF16),
        compiler_params=_params("arbitrary"),
        name="fourier",
    )(u_fn, chan, pos, w)


SWA_SCALE = SWA_HD ** -0.5
SWA_QW = SWA_HEADS * SWA_HD
SWA_KW = SWA_KV_HEADS * SWA_HD
SWA_STACK_ORDER = (0, 2, 1, 3)


def _swa_stack_q(q):
    return jnp.concatenate([_stack_heads(q[:, :SWA_KW], SWA_KV_HEADS, SWA_HD),
                            _stack_heads(q[:, SWA_KW:], SWA_KV_HEADS, SWA_HD)], axis=0)


def _swa_unstack_o(o):
    t = o.shape[0] // SWA_HEADS
    return jnp.concatenate([_unstack_heads(o[:2 * t], SWA_KV_HEADS, SWA_HD),
                            _unstack_heads(o[2 * t:], SWA_KV_HEADS, SWA_HD)], axis=1)


def _sink_rows(sink_ref, t):
    return jnp.concatenate([jnp.full((t, 1), sink_ref[h], F32) for h in SWA_STACK_ORDER], axis=0)


def _swa_ctx_kernel(sink_ref, u_ref, o_ref):
    sink = _sink_rows(sink_ref, SEQ)
    for j in range(SWA_CTX_PER_STEP):
        rows = slice(j * SEQ, (j + 1) * SEQ)
        u = u_ref[rows, :]
        qs = _swa_stack_q(u[:, :SWA_QW]).astype(BF16)
        k = u[:, SWA_QW:SWA_QW + SWA_KW].astype(BF16)
        v = u[:, SWA_QW + SWA_KW:].astype(BF16)
        s = _dot_nt(qs, k) * SWA_SCALE
        m = jnp.maximum(jnp.max(s, axis=-1, keepdims=True), sink)
        p = jnp.exp(s - m)
        denom = jnp.sum(p, axis=-1, keepdims=True) + jnp.exp(sink - m)
        o_ref[rows, :] = _swa_unstack_o(_dot(p.astype(BF16), v) / denom).astype(o_ref.dtype)


def _swa_ctx(u_swa, sink):
    rows = SWA_CTX_PER_STEP * SEQ
    return pl.pallas_call(
        _swa_ctx_kernel,
        grid=(BATCH // SWA_CTX_PER_STEP,),
        in_specs=[pl.BlockSpec(memory_space=pltpu.SMEM),
                  pl.BlockSpec((rows, SWA_IN), lambda b: (b, 0))],
        out_specs=pl.BlockSpec((rows, SWA_QW), lambda b: (b, 0)),
        out_shape=jax.ShapeDtypeStruct((N_CTX, SWA_QW), BF16),
        compiler_params=_params("arbitrary"),
        name="swa_ctx",
    )(sink, u_swa)


SWA_PAD = DEC_SEQ + 2 * SWA_QBLK


def _swa_lat_kernel(sink_ref, u_ref, kc_ref, vc_ref, cosq_ref, sinq_ref, cosk_ref, sin_k_ref,
                    o_ref, k_s, v_s):
    i = pl.program_id(1)
    B = SWA_QBLK

    @pl.when(i == 0)
    def _():
        zeros = jnp.zeros((B, SWA_KW), BF16)
        k = _rope(u_ref[:, SWA_QW:SWA_QW + SWA_KW], cosk_ref[...], sin_k_ref[...]).astype(BF16)
        k_s[...] = jnp.concatenate([zeros, k, zeros], axis=0)
        v_s[...] = jnp.concatenate([zeros, u_ref[:, SWA_QW + SWA_KW:].astype(BF16), zeros], axis=0)

    r0 = pl.multiple_of(i * B, B)
    q = _rope(u_ref[pl.ds(r0, B), 0:SWA_QW], cosq_ref[pl.ds(r0, B), :], sinq_ref[pl.ds(r0, B), :])
    qs = _swa_stack_q(q).astype(BF16)
    s_loc = _dot_nt(qs, k_s[pl.ds(r0, 3 * B), :]) * SWA_SCALE
    row = lax.broadcasted_iota(jnp.int32, s_loc.shape, 0) % B
    col = lax.broadcasted_iota(jnp.int32, s_loc.shape, 1)
    kpos = r0 - B + col
    valid = (jnp.abs(row + B - col) <= WINDOW) & (kpos >= 0) & (kpos < DEC_SEQ)
    s_loc = jnp.where(valid, s_loc, NEG_INF)
    s_ctx = _dot_nt(qs, kc_ref[...].astype(BF16)) * SWA_SCALE
    sink = _sink_rows(sink_ref, B)
    m = jnp.maximum(jnp.maximum(jnp.max(s_loc, axis=-1, keepdims=True),
                                jnp.max(s_ctx, axis=-1, keepdims=True)), sink)
    p_loc = jnp.exp(s_loc - m)
    p_ctx = jnp.exp(s_ctx - m)
    denom = (jnp.sum(p_loc, axis=-1, keepdims=True) + jnp.sum(p_ctx, axis=-1, keepdims=True)
             + jnp.exp(sink - m))
    o = _dot(p_loc.astype(BF16), v_s[pl.ds(r0, 3 * B), :]) + _dot(p_ctx.astype(BF16), vc_ref[...].astype(BF16))
    o_ref[...] = _swa_unstack_o(o / denom).astype(o_ref.dtype)


def _swa_lat(u_swa, cache_k, cache_v, l, sink, cosq, sinq, cosk, sink_k):
    full = lambda shape: pl.BlockSpec(shape, lambda b, i: (0, 0))
    nq = DEC_SEQ // SWA_QBLK
    cache_spec = pl.BlockSpec((None, None, PAST_LEN, SWA_KW), lambda b, i: (b, l, 0, 0))
    return pl.pallas_call(
        _swa_lat_kernel,
        grid=(DEC_BATCH, nq),
        in_specs=[pl.BlockSpec(memory_space=pltpu.SMEM),
                  pl.BlockSpec((DEC_SEQ, SWA_IN), lambda b, i: (N_CTX // DEC_SEQ + b, 0)),
                  cache_spec, cache_spec,
                  full((DEC_SEQ, SWA_QW)), full((DEC_SEQ, SWA_QW)),
                  full((DEC_SEQ, SWA_KW)), full((DEC_SEQ, SWA_KW))],
        out_specs=pl.BlockSpec((SWA_QBLK, SWA_QW), lambda b, i: (b * nq + i, 0)),
        out_shape=jax.ShapeDtypeStruct((N_LAT, SWA_QW), BF16),
        scratch_shapes=[pltpu.VMEM((SWA_PAD, SWA_KW), BF16), pltpu.VMEM((SWA_PAD, SWA_KW), BF16)],
        compiler_params=_params("arbitrary", "arbitrary"),
        name="swa_lat",
    )(sink, u_swa, cache_k, cache_v, cosq, sinq, cosk, sink_k)


N_BLK = N_TOK // ROW_TILE
SEG_ALIGN = 16
LOCAL_ROWS = ROW_TILE * TOP_K + N_EXPERTS * SEG_ALIGN
LOCAL_CHUNK = 512
EXPERT_TILE = 768
SORTED_ROWS = -(-(N_TOK * TOP_K + N_BLK * N_EXPERTS * SEG_ALIGN + N_EXPERTS * (EXPERT_TILE + SEG_ALIGN))
                // EXPERT_TILE) * EXPERT_TILE
MAX_TILES = SORTED_ROWS // EXPERT_TILE
NOT_PICKED = -1.0
NO_DEST = 4095.0


def _out_kernel(*refs, n_x):
    x_all = _read_rows(refs[:n_x], OUT_ROWS)
    mix_all = [_read_rows(refs[n_x + 2 * j:n_x + 2 * j + 2], OUT_ROWS) for j in range(4)]
    out_refs = refs[n_x + 8:]
    for blk in range(OUT_ROWS // ROW_TILE):
        rows = slice(blk * ROW_TILE, (blk + 1) * ROW_TILE)
        _out_block(x_all[rows], [m[rows] for m in mix_all], out_refs, blk)


def _out_block(x, mixers, refs, blk):
    (mod_ref, g_ref, wo_ref, wr_ref, br_ref, wsg_ref, wsu_ref, wsd_ref,
     x1_ref, h_ref, gate_ref, rank_ref, cnt_ref) = refs
    rows = slice(blk * ROW_TILE, (blk + 1) * ROW_TILE)
    mix = jnp.zeros((ROW_TILE, D_MODEL), F32)
    for j in range(4):
        mix = mix + _dot(mixers[j], wo_ref[j * 256:(j + 1) * 256, :])
    g1 = mod_ref[:, 2 * D_MODEL:3 * D_MODEL]
    sh2 = mod_ref[:, 3 * D_MODEL:4 * D_MODEL]
    sc2 = mod_ref[:, 4 * D_MODEL:5 * D_MODEL]
    g2 = mod_ref[:, 5 * D_MODEL:6 * D_MODEL]
    x1 = x + g1 * mix
    h = _rms(x1, g_ref[...]) * (1.0 + sc2) + sh2
    hb = h.astype(BF16)
    h_ref[rows, :] = hb

    w_hi = wr_ref[...].astype(BF16)
    w_lo = (wr_ref[...] - w_hi.astype(F32)).astype(BF16)
    h_lo = (h - hb.astype(F32)).astype(BF16)
    logits = ((_dot(h_lo, w_hi) + _dot(hb, w_lo)) + _dot(hb, w_hi)).T
    scores = jax.nn.sigmoid(logits)
    sel = scores + br_ref[...]
    eidx = lax.broadcasted_iota(jnp.int32, sel.shape, 0)
    gate = jnp.zeros_like(scores)
    picked = jnp.zeros_like(scores)
    for _ in range(TOP_K):
        best = jnp.max(sel, axis=0, keepdims=True)
        first = jnp.min(jnp.where(sel == best, eidx, N_EXPERTS), axis=0, keepdims=True)
        pick = eidx == first
        gate = jnp.where(pick, scores, gate)
        picked = jnp.where(pick, 1.0, picked)
        sel = jnp.where(pick, -jnp.inf, sel)
    gate = ROUTE_SCALE * gate / jnp.sum(gate, axis=0, keepdims=True)

    ti = lax.broadcasted_iota(jnp.int32, (ROW_TILE, ROW_TILE), 0)
    tj = lax.broadcasted_iota(jnp.int32, (ROW_TILE, ROW_TILE), 1)
    pb = picked.astype(BF16)
    rank = _dot(pb, jnp.where(ti < tj, 1.0, 0.0).astype(BF16))
    gate_ref[:, rows] = gate
    rank_ref[:, rows] = jnp.where(picked > 0.0, rank, NOT_PICKED)
    counts = _dot_nt(jnp.ones((8, ROW_TILE), BF16), pb)
    cnt_ref[blk] = jnp.concatenate([counts, jnp.zeros_like(counts)], axis=1)

    hid = _silu(_dot(hb, wsg_ref[...])) * _dot(hb, wsu_ref[...])
    x1_ref[rows, :] = x1 + g2 * _dot(hid.astype(BF16), wsd_ref[...])


def _out_proj(x_parts, mixer_pairs, mod, g, wo, l, wr, br, wsg, wsu, wsd):
    row = lambda i: (i, 0)
    col = lambda i: (0, i)
    full = lambda shape: pl.BlockSpec(shape, lambda i: (0, 0))
    per_step = OUT_ROWS // ROW_TILE
    et_spec = pl.BlockSpec((N_EXPERTS, OUT_ROWS), col)
    et_shape = jax.ShapeDtypeStruct((N_EXPERTS, N_TOK), F32)
    return pl.pallas_call(
        functools.partial(_out_kernel, n_x=len(x_parts)),
        grid=(N_TOK // OUT_ROWS,),
        in_specs=_row_specs(x_parts, D_MODEL, OUT_ROWS) + 4 * _ctx_lat_specs(256, OUT_ROWS) + [
            pl.BlockSpec((None, 1, 6 * D_MODEL), lambda i: (_mod_row(i * per_step), 0, 0)),
            full((1, D_MODEL)), pl.BlockSpec((None, D_MODEL, D_MODEL), lambda i: (l, 0, 0)),
            full((D_MODEL, N_EXPERTS)), full((N_EXPERTS, 1)),
            full((D_MODEL, D_SHARED)), full((D_MODEL, D_SHARED)), full((D_SHARED, D_MODEL))],
        out_specs=[pl.BlockSpec((OUT_ROWS, D_MODEL), row), pl.BlockSpec((OUT_ROWS, D_MODEL), row),
                   et_spec, et_spec,
                   pl.BlockSpec((per_step, 8, 128), lambda i: (i, 0, 0))],
        out_shape=[jax.ShapeDtypeStruct((N_TOK, D_MODEL), F32),
                   jax.ShapeDtypeStruct((N_TOK, D_MODEL), BF16),
                   et_shape, et_shape,
                   jax.ShapeDtypeStruct((N_BLK, 8, 128), F32)],
        compiler_params=_params("arbitrary"),
        name="out_proj",
    )(*x_parts, *[a for pair in mixer_pairs for a in pair], mod, g, wo, wr, br, wsg, wsu, wsd)


def _segment_plan(cnt):
    cnt = cnt[:, 0, :N_EXPERTS].astype(jnp.int32)
    seg = jnp.maximum((cnt + (SEG_ALIGN - 1)) // SEG_ALIGN, 1) * SEG_ALIGN
    local = jnp.cumsum(seg, axis=1) - seg
    total = jnp.sum(seg, axis=1)
    per_expert = jnp.sum(seg, axis=0)
    padded = (per_expert + SEG_ALIGN + (EXPERT_TILE - 1)) // EXPERT_TILE * EXPERT_TILE
    ends = jnp.cumsum(padded)
    start = ends - padded
    dest = start[None, :] + jnp.cumsum(seg, axis=0) - seg
    n_tiles = ends[-1] // EXPERT_TILE
    tiles = jnp.arange(MAX_TILES, dtype=jnp.int32)
    tile_expert = jnp.sum((ends // EXPERT_TILE)[None, :] <= jnp.minimum(tiles, n_tiles - 1)[:, None], axis=1)
    tile_expert = tile_expert.astype(jnp.int32)
    plan = dict(seg=seg, local=local, total=total.astype(jnp.int32), dest=dest.astype(jnp.int32),
                tail_start=(start + per_expert).astype(jnp.int32), tail_rows=(padded - per_expert).astype(jnp.int32),
                n_tiles=n_tiles.reshape(1).astype(jnp.int32), tile_expert=tile_expert)
    segf, localf = seg.astype(F32), local.astype(F32)
    pad_lanes = lambda a: jnp.concatenate([a, jnp.zeros_like(a)], axis=1)[:, None, :]
    plan.update(seg_row=pad_lanes(segf), local_row=pad_lanes(localf),
                seg_col=segf[:, :, None], local_col=localf[:, :, None])
    return plan


def _local_dest_digits(rank, local_start):
    dest = jnp.where(rank >= 0.0, local_start + rank, NO_DEST)
    hi = jnp.floor(dest * (1.0 / 64.0))
    return hi, dest - 64.0 * hi


def _dispatch_kernel(dest_ref, seg_ref, local_ref, total_ref, tail_start_ref, tail_rows_ref, nt_ref,
                     h_ref, rank_ref, local_col_ref, local_row_ref, seg_row_ref,
                     xs_hbm, buf, zeros, sems, zsem, usem):
    b = pl.program_id(0)
    slot = b % 2

    def wait_block(blk, s):
        n = pl.multiple_of(total_ref[blk], SEG_ALIGN)
        pltpu.make_async_copy(buf.at[s, pl.ds(0, n)], xs_hbm.at[pl.ds(0, n)], sems.at[s]).wait()

    @pl.when(b == 0)
    def _():
        zeros[...] = jnp.zeros_like(zeros)

    def unused_tiles(action):
        for k in range(-(-MAX_TILES // N_BLK)):
            t = nt_ref[0] + b + k * N_BLK

            @pl.when(t < MAX_TILES)
            def _():
                cp = pltpu.make_async_copy(zeros.at[pl.ds(0, EXPERT_TILE)],
                                           xs_hbm.at[pl.ds(pl.multiple_of(t * EXPERT_TILE, EXPERT_TILE),
                                                           EXPERT_TILE)], usem)
                cp.start() if action == "start" else cp.wait()

    unused_tiles("start")

    @pl.when(b >= 2)
    def _():
        wait_block(b - 2, slot)

    hi, lo = _local_dest_digits(rank_ref[...], local_col_ref[...])
    code = jnp.concatenate([hi, lo], axis=0).astype(BF16)
    hb = h_ref[...]
    lstart = local_row_ref[:, :N_EXPERTS]
    lend = lstart + seg_row_ref[:, :N_EXPERTS]
    r = lax.broadcasted_iota(jnp.int32, (LOCAL_CHUNK, N_EXPERTS), 0).astype(F32)
    rr = lax.broadcasted_iota(jnp.int32, (LOCAL_CHUNK, ROW_TILE), 0).astype(F32)
    def compact(c):
        first = float(c * LOCAL_CHUNK)
        member = (r >= lstart - first) & (r < lend - first)
        table = jnp.concatenate([jnp.where(member, 64.0, 0.0), jnp.where(member, 1.0, 0.0)], axis=1).astype(BF16)
        d = _dot(table, code)
        onehot = jnp.where(d == rr + first, 1.0, 0.0).astype(BF16)
        buf[slot, c * LOCAL_CHUNK:(c + 1) * LOCAL_CHUNK, :] = _dot(onehot, hb).astype(BF16)

    n_chunks = LOCAL_ROWS // LOCAL_CHUNK
    for c in range(n_chunks - 1):
        compact(c)
    pl.when(total_ref[b] > (n_chunks - 1) * LOCAL_CHUNK)(functools.partial(compact, n_chunks - 1))

    for e in range(N_EXPERTS):
        n = pl.multiple_of(seg_ref[b, e], SEG_ALIGN)
        src = pl.multiple_of(local_ref[b, e], SEG_ALIGN)
        dst = pl.multiple_of(dest_ref[b, e], SEG_ALIGN)
        pltpu.make_async_copy(buf.at[slot, pl.ds(src, n)], xs_hbm.at[pl.ds(dst, n)],
                              sems.at[slot]).start(priority=e % 2)

    unused_tiles("wait")

    @pl.when(b == N_BLK - 1)
    def _():
        def tail(e):
            n = pl.multiple_of(tail_rows_ref[e], SEG_ALIGN)
            dst = pl.multiple_of(tail_start_ref[e], SEG_ALIGN)
            return pltpu.make_async_copy(zeros.at[pl.ds(0, n)], xs_hbm.at[pl.ds(dst, n)], zsem)

        for e in range(N_EXPERTS):
            tail(e).start()
        for e in range(N_EXPERTS):
            tail(e).wait()
        wait_block(b - 1, 1 - slot)
        wait_block(b, slot)


def _dispatch(h, rank_t, plan):
    blk = lambda shape, imap: pl.BlockSpec(shape, imap)
    return pl.pallas_call(
        _dispatch_kernel,
        grid_spec=pltpu.PrefetchScalarGridSpec(
            num_scalar_prefetch=7, grid=(N_BLK,),
            in_specs=[blk((ROW_TILE, D_MODEL), lambda b, *_: (b, 0)),
                      blk((N_EXPERTS, ROW_TILE), lambda b, *_: (0, b)),
                      blk((None, N_EXPERTS, 1), lambda b, *_: (b, 0, 0)),
                      blk((None, 1, 128), lambda b, *_: (b, 0, 0)),
                      blk((None, 1, 128), lambda b, *_: (b, 0, 0))],
            out_specs=pl.BlockSpec(memory_space=pl.ANY),
            scratch_shapes=[pltpu.VMEM((2, LOCAL_ROWS, D_MODEL), BF16),
                            pltpu.VMEM((EXPERT_TILE + SEG_ALIGN, D_MODEL), BF16),
                            pltpu.SemaphoreType.DMA((2,)), pltpu.SemaphoreType.DMA(()),
                            pltpu.SemaphoreType.DMA(())]),
        out_shape=jax.ShapeDtypeStruct((SORTED_ROWS, D_MODEL), BF16),
        compiler_params=_params("arbitrary"),
        name="dispatch",
    )(plan['dest'], plan['seg'], plan['local'], plan['total'], plan['tail_start'], plan['tail_rows'],
      plan['n_tiles'], h, rank_t, plan['local_col'], plan['local_row'], plan['seg_row'])


def _expert_kernel(te_ref, nt_ref, x_ref, wg_ref, wu_ref, wd_ref, y_ref, wg_s, wu_s, wd_s):
    i = pl.program_id(0)
    active = i < nt_ref[0]

    @pl.when((i == 0) | (te_ref[i] != te_ref[jnp.maximum(i - 1, 0)]))
    def _():
        wg_s[...] = wg_ref[...].astype(BF16)
        wu_s[...] = wu_ref[...].astype(BF16)
        wd_s[...] = wd_ref[...].astype(BF16)

    @pl.when(active)
    def _():
        x = x_ref[...]
        hid = _silu(_dot(x, wg_s[...])) * _dot(x, wu_s[...])
        y_ref[...] = _dot(hid.astype(BF16), wd_s[...]).astype(BF16)


def _experts(xs, plan, w_gate, w_up, w_down, l):
    rows = pl.BlockSpec((EXPERT_TILE, D_MODEL), lambda i, te, nt: (jnp.minimum(i, nt[0] - 1), 0))
    wspec = lambda shape: pl.BlockSpec((None, None) + shape, lambda i, te, nt: (l, te[i], 0, 0))
    return pl.pallas_call(
        _expert_kernel,
        grid_spec=pltpu.PrefetchScalarGridSpec(
            num_scalar_prefetch=2, grid=(MAX_TILES,),
            in_specs=[rows, wspec((D_MODEL, D_EXPERT)), wspec((D_MODEL, D_EXPERT)), wspec((D_EXPERT, D_MODEL))],
            out_specs=rows,
            scratch_shapes=[pltpu.VMEM((D_MODEL, D_EXPERT), BF16), pltpu.VMEM((D_MODEL, D_EXPERT), BF16),
                            pltpu.VMEM((D_EXPERT, D_MODEL), BF16)]),
        out_shape=jax.ShapeDtypeStruct((SORTED_ROWS, D_MODEL), BF16),
        input_output_aliases={2: 0},
        compiler_params=_params("arbitrary"),
        name="experts",
    )(plan['tile_expert'], plan['n_tiles'], xs, w_gate, w_up, w_down)


def _combine_kernel(*refs, final):
    refs = list(refs)
    dest_ref, seg_ref, local_ref, total_ref = refs[:4]
    x1_ref, ys_hbm, gate_ref, rank_ref, mod_ref, local_row_ref, local_col_ref, seg_col_ref = refs[4:12]
    rest = refs[12:]
    fg_ref = rest.pop(0) if final else None
    outs, (buf, sems) = rest[:-2], rest[-2:]
    b = pl.program_id(0)
    slot = b % 2

    def fetch(blk, s):
        for e in range(N_EXPERTS):
            n = pl.multiple_of(seg_ref[blk, e], SEG_ALIGN)
            src = pl.multiple_of(dest_ref[blk, e], SEG_ALIGN)
            dst = pl.multiple_of(local_ref[blk, e], SEG_ALIGN)
            pltpu.make_async_copy(ys_hbm.at[pl.ds(src, n)], buf.at[s, pl.ds(dst, n)],
                                  sems.at[s]).start(priority=e % 2)

    @pl.when(b == 0)
    def _():
        buf[...] = jnp.zeros_like(buf)
        fetch(0, 0)

    def wait_rows(blk, s):
        n_rows = pl.multiple_of(total_ref[blk], SEG_ALIGN)
        pltpu.make_async_copy(ys_hbm.at[pl.ds(0, n_rows)], buf.at[s, pl.ds(0, n_rows)], sems.at[s]).wait()

    wait_rows(b, slot)
    nxt = jnp.minimum(b + 1, N_BLK - 1)
    fetch(nxt, 1 - slot)

    gate = gate_ref[...].T
    hi, lo = _local_dest_digits(rank_ref[...].T, local_row_ref[:, :N_EXPERTS])
    lhs = jnp.concatenate([jnp.concatenate([hi, lo], axis=1),
                           jnp.concatenate([jnp.zeros_like(gate), gate], axis=1)], axis=0).astype(BF16)
    lstart = local_col_ref[...]
    lend = lstart + seg_col_ref[...]
    routed = jnp.zeros((ROW_TILE, D_MODEL), F32)
    r = lax.broadcasted_iota(jnp.int32, (N_EXPERTS, LOCAL_CHUNK), 1).astype(F32)
    rr = lax.broadcasted_iota(jnp.int32, (ROW_TILE, LOCAL_CHUNK), 1).astype(F32)
    def gather(c, acc):
        first = float(c * LOCAL_CHUNK)
        member = (r >= lstart - first) & (r < lend - first)
        table = jnp.concatenate([jnp.where(member, 64.0, 0.0), jnp.where(member, 1.0, 0.0)], axis=0).astype(BF16)
        dg = _dot(lhs, table)
        weights = jnp.where(dg[:ROW_TILE] == rr + first, dg[ROW_TILE:], 0.0).astype(BF16)
        return acc + _dot(weights, buf[slot, c * LOCAL_CHUNK:(c + 1) * LOCAL_CHUNK, :])

    for c in range(LOCAL_ROWS // LOCAL_CHUNK):
        routed = gather(c, routed)
    x = x1_ref[...] + mod_ref[:, 5 * D_MODEL:6 * D_MODEL] * routed
    if final:
        y = _rms(x, fg_ref[...])

        @pl.when(b < CTX_TILES)
        def _():
            outs[0][...] = y

        @pl.when(b >= CTX_TILES)
        def _():
            outs[1][...] = y
    else:
        outs[0][...] = x

    @pl.when(b == N_BLK - 1)
    def _():
        wait_rows(nxt, 1 - slot)


def _combine(x1, ys, gate_t, rank_t, mod, plan, final_g):
    final = final_g is not None
    blk = lambda shape, imap: pl.BlockSpec(shape, imap)
    xspec = blk((ROW_TILE, D_MODEL), lambda b, *_: (b, 0))
    if final:
        out_specs = _ctx_lat_specs(D_MODEL)
        out_shape = [jax.ShapeDtypeStruct((N_CTX, D_MODEL), F32), jax.ShapeDtypeStruct((N_LAT, D_MODEL), F32)]
    else:
        out_specs, out_shape = xspec, jax.ShapeDtypeStruct((N_TOK, D_MODEL), F32)
    et_spec = blk((N_EXPERTS, ROW_TILE), lambda b, *_: (0, b))
    in_specs = [xspec, pl.BlockSpec(memory_space=pl.ANY), et_spec, et_spec,
                blk((None, 1, 6 * D_MODEL), lambda b, *_: (_mod_row(b), 0, 0)),
                blk((None, 1, 128), lambda b, *_: (b, 0, 0)),
                blk((None, N_EXPERTS, 1), lambda b, *_: (b, 0, 0)),
                blk((None, N_EXPERTS, 1), lambda b, *_: (b, 0, 0))]
    args = [x1, ys, gate_t, rank_t, mod, plan['local_row'], plan['local_col'], plan['seg_col']]
    if final:
        in_specs.append(blk((1, D_MODEL), lambda b, *_: (0, 0)))
        args.append(final_g)
    return pl.pallas_call(
        functools.partial(_combine_kernel, final=final),
        grid_spec=pltpu.PrefetchScalarGridSpec(
            num_scalar_prefetch=4, grid=(N_BLK,),
            in_specs=in_specs, out_specs=out_specs,
            scratch_shapes=[pltpu.VMEM((2, LOCAL_ROWS, D_MODEL), BF16), pltpu.SemaphoreType.DMA((2,))]),
        out_shape=out_shape,
        compiler_params=_params("arbitrary"),
        name="combine",
    )(plan['dest'], plan['seg'], plan['local'], plan['total'], *args)


def _rope_full_tables(dim, n_rep):
    rows = DEC_SEQ // GRID_W
    r_idx, c_idx = np.meshgrid(np.arange(rows), np.arange(GRID_W), indexing='ij')
    pos = np.stack([r_idx.reshape(-1), c_idx.reshape(-1)], axis=-1).astype(np.float32)
    nf = dim // 4
    inv = np.float32(ROPE_BASE) ** (-np.arange(nf, dtype=np.float32) / np.float32(nf))
    ang = (pos[:, :, None] * inv).astype(np.float32)
    ang = np.repeat(ang.reshape(DEC_SEQ, 2 * nf), 2, axis=1)
    sign = np.tile(np.asarray([-1.0, 1.0], np.float32), dim // 2)
    cos = np.tile(np.cos(ang.astype(np.float64)), (1, n_rep))
    sin = np.tile(np.sin(ang.astype(np.float64)) * sign, (1, n_rep))
    return jnp.asarray(cos, F32), jnp.asarray(sin, F32)


def _pack_w_in(w):
    c0 = MLA_Q_LORA + MLA_KV_LORA
    kr = w[..., c0:MLA_IN]
    s0 = MLA_IN + HG_IN + FN_IN
    qh = [w[..., s0 + h * SWA_HD:s0 + (h + 1) * SWA_HD] for h in SWA_STACK_ORDER]
    return jnp.concatenate([w[..., :c0], kr, kr, kr, kr, w[..., MLA_IN:s0]] + qh
                           + [w[..., s0 + SWA_QW:]], axis=-1).astype(BF16)


def _pack_w_q_up(w):
    hd = MLA_NOPE + MLA_ROPE
    nope = [w[:, h * hd:h * hd + MLA_NOPE] for h in range(MLA_HEADS)]
    rope = [w[:, h * hd + MLA_NOPE:(h + 1) * hd] for h in range(MLA_HEADS)]
    return jnp.concatenate(nope + rope, axis=1).astype(BF16)


def _pack_w_kv_up(w):
    hd = MLA_NOPE + MLA_V
    kn = [w[:, h * hd:h * hd + MLA_NOPE] for h in range(MLA_HEADS)]
    vv = [w[:, h * hd + MLA_NOPE:(h + 1) * hd] for h in range(MLA_HEADS)]
    return jnp.concatenate(kn + vv, axis=1).astype(BF16)


def _pack_w_out(w):
    s0 = 3 * 256
    rows = [w[..., s0 + h * SWA_HD:s0 + (h + 1) * SWA_HD, :] for h in SWA_STACK_ORDER]
    return jnp.concatenate([w[..., :s0, :]] + rows, axis=-2).astype(BF16)


def kernel(x_prompt, x_sample, c, cache_mla_ckv, cache_mla_krope, cache_swa_k, cache_swa_v, state_hgrn,
           c_ctx, w_ada, b_ada, norm1_g, norm2_g, w_in, mla_q_norm_g, mla_w_q_up, mla_kv_norm_g, mla_w_kv_up,
           hg_lb_logits, hg_norm_g, fn_w, swa_sink, w_out, moe_w_router, moe_b_router, moe_w_gate, moe_w_up,
           moe_w_down, sh_w_gate, sh_w_up, sh_w_down, final_norm_g):
    x_parts = (x_prompt.reshape(N_CTX, D_MODEL), x_sample.reshape(N_LAT, D_MODEL))
    cv8 = jnp.concatenate([c_ctx[None, :], c, jnp.zeros((8 - 1 - DEC_BATCH, D_MODEL), F32)], axis=0)
    mods = _ada(cv8, w_ada, b_ada).reshape(DEPTH, 8, 1, 6 * D_MODEL)

    lb = jnp.cumsum(jax.nn.softmax(hg_lb_logits.astype(F32), axis=1), axis=1)
    lb = lb - lb[:, :1]

    cos_m, sin_m = _rope_full_tables(MLA_ROPE, MLA_HEADS)
    cos_q, sin_q = _rope_full_tables(SWA_HD, SWA_HEADS)
    cos_k, sin_k = cos_q[:, :SWA_KW], sin_q[:, :SWA_KW]
    cache_k = cache_swa_k.reshape(DEC_BATCH, DEPTH, PAST_LEN, SWA_KW)
    cache_v = cache_swa_v.reshape(DEC_BATCH, DEPTH, PAST_LEN, SWA_KW)
    state_t = jnp.swapaxes(state_hgrn, -1, -2)

    ctx_blk_lat = N_CTX // DEC_SEQ
    w_in_p, w_out_p = _pack_w_in(w_in), _pack_w_out(w_out)
    new_ckv, new_kr, new_k, new_v, new_st = [], [], [], [], []
    for l in range(DEPTH):
        u_mla, u_hg, u_fn, u_swa = _in_proj(x_parts, mods[l], norm1_g[l][None], w_in_p, l)

        qg, kvg = mla_q_norm_g[l][None], mla_kv_norm_g[l][None]
        wq, wkv = _pack_w_q_up(mla_w_q_up[l]), _pack_w_kv_up(mla_w_kv_up[l])
        o_mla_c, ckv_c = _mla_ctx(u_mla, qg, wq, kvg, wkv)
        o_mla_l = _mla_lat(u_mla, cache_mla_ckv, cache_mla_krope, l, cos_m, sin_m, qg, wq, kvg, wkv)

        lbf, lbb = lb[0, l][None], lb[1, l][None]
        ng4 = jnp.tile(hg_norm_g[l], HG_HEADS)[None]
        o_hg_c, st_c = _hgrn(u_hg, lbf, lbb, ng4, None, seq=SEQ, n_batch=BATCH, row_block0=0)
        o_hg_l = _hgrn(u_hg, lbf, lbb, ng4, state_t[:, l], seq=DEC_SEQ, n_batch=DEC_BATCH,
                       row_block0=ctx_blk_lat)

        fw = fn_w[l].astype(BF16)
        o_fn_c = _fourier(u_fn, fw, seq=SEQ, n_batch=BATCH, row_block0=0)
        o_fn_l = _fourier(u_fn, fw, seq=DEC_SEQ, n_batch=DEC_BATCH, row_block0=ctx_blk_lat)

        sink = swa_sink[l]
        o_swa_c = _swa_ctx(u_swa, sink)
        o_swa_l = _swa_lat(u_swa, cache_k, cache_v, l, sink, cos_q, sin_q, cos_k, sin_k)

        x1, h2, gate_t, rank_t, cnt = _out_proj(
            x_parts, ((o_mla_c, o_mla_l), (o_hg_c, o_hg_l), (o_fn_c, o_fn_l), (o_swa_c, o_swa_l)),
            mods[l], norm2_g[l][None], w_out_p, l,
            moe_w_router[l], moe_b_router[l][:, None],
            sh_w_gate[l].astype(BF16), sh_w_up[l].astype(BF16), sh_w_down[l].astype(BF16))
        plan = _segment_plan(cnt)
        xs = _dispatch(h2, rank_t, plan)
        ys = _experts(xs, plan, moe_w_gate, moe_w_up, moe_w_down, l)
        if l < DEPTH - 1:
            x_parts = (_combine(x1, ys, gate_t, rank_t, mods[l], plan, None),)
        else:
            y_prompt, y_sample = _combine(x1, ys, gate_t, rank_t, mods[l], plan, final_norm_g[None])

        new_ckv.append(ckv_c.reshape(BATCH, SEQ, MLA_KV_LORA))
        new_kr.append(u_mla[:N_CTX, MLA_Q_LORA + MLA_KV_LORA:MLA_IN].reshape(BATCH, SEQ, MLA_ROPE))
        new_k.append(u_swa[:N_CTX, SWA_QW:SWA_QW + SWA_KW].reshape(BATCH, SEQ, SWA_KV_HEADS, SWA_HD))
        new_v.append(u_swa[:N_CTX, SWA_QW + SWA_KW:].reshape(BATCH, SEQ, SWA_KV_HEADS, SWA_HD))
        new_st.append(jnp.swapaxes(st_c, -1, -2))

    y_prompt = y_prompt.reshape(BATCH, SEQ, D_MODEL)
    y_sample = y_sample.reshape(DEC_BATCH, DEC_SEQ, D_MODEL)
    stack = lambda xs: jnp.stack(xs, axis=1)
    return (y_prompt, y_sample, stack(new_ckv), stack(new_kr), stack(new_k), stack(new_v), stack(new_st))
```

```python
import functools

import numpy as np
import jax
import jax.numpy as jnp
from jax import lax
from jax.experimental import pallas as pl
from jax.experimental.pallas import tpu as pltpu

F32 = jnp.float32
BF16 = jnp.bfloat16

D_MODEL = 1024
BATCH = 32
SEQ = 256
DEPTH = 2
DEC_BATCH = 2
DEC_SEQ = 1024
PAST_LEN = 256
GRID_W = 64
EPS = 1e-6
ROPE_BASE = 10000.0
NEG_INF = -1e30

MLA_HEADS = 4
MLA_NOPE = 64
MLA_ROPE = 32
MLA_V = 64
MLA_Q_LORA = 256
MLA_KV_LORA = 128
HG_HEADS = 4
HG_DK = 64
HG_DV = 64
HG_W = HG_HEADS * HG_DK
FN_GROUPS = 4
FN_WIDTH = 256
SWA_HEADS = 4
SWA_KV_HEADS = 2
SWA_HD = 64
WINDOW = 128
N_EXPERTS = 64
TOP_K = 6
D_EXPERT = 256
D_SHARED = 256
ROUTE_SCALE = 2.5

MLA_IN = MLA_Q_LORA + MLA_KV_LORA + MLA_ROPE
HG_IN = 3 * HG_HEADS * HG_DK + 2 * HG_HEADS * HG_DV
FN_IN = FN_WIDTH
SWA_IN = (SWA_HEADS + 2 * SWA_KV_HEADS) * SWA_HD

N_CTX = BATCH * SEQ
N_LAT = DEC_BATCH * DEC_SEQ
N_TOK = N_CTX + N_LAT

MLA_PACK = 512
U_COLS = MLA_PACK + HG_IN + FN_IN + SWA_IN

ROW_TILE = 256
CTX_TILES = N_CTX // ROW_TILE
LAT_TILES_PER_BATCH = DEC_SEQ // ROW_TILE
HG_CHUNK = 32
HG_BLOCK = 256
SWA_QBLK = 128
MLA_QBLK = 256
CTX_PER_STEP = 4
SWA_CTX_PER_STEP = 8
HG_CTX_PER_STEP = 4
OUT_ROWS = 2 * ROW_TILE
VMEM_LIMIT = 56 * 1024 * 1024


def _dot(a, b):
    return jnp.dot(a, b, preferred_element_type=F32)


def _dot_nt(a, b):
    return lax.dot_general(a, b, (((1,), (1,)), ((), ())), preferred_element_type=F32)


def _dot_tn(a, b):
    return lax.dot_general(a, b, (((0,), (0,)), ((), ())), preferred_element_type=F32)


def _split3(x):
    hi = x.astype(BF16)
    r1 = x - hi.astype(F32)
    mid = r1.astype(BF16)
    return hi, mid, (r1 - mid.astype(F32)).astype(BF16)


def _dot_exact_lhs(a, b):
    ab = a.astype(BF16)
    hi, mid, lo = _split3(b)
    return (_dot(ab, lo) + _dot(ab, mid)) + _dot(ab, hi)


def _dot_exact_rhs(a, b):
    bb = b.astype(BF16)
    hi, mid, lo = _split3(a)
    return (_dot(lo, bb) + _dot(mid, bb)) + _dot(hi, bb)


def _rms(x, g):
    return x * lax.rsqrt(jnp.mean(x * x, axis=-1, keepdims=True) + EPS) * g


def _silu(x):
    return x * jax.nn.sigmoid(x)


def _mod_row(i):
    return jnp.where(i < CTX_TILES, 0, 1 + (i - CTX_TILES) // LAT_TILES_PER_BATCH)


def _params(*sem):
    return pltpu.CompilerParams(dimension_semantics=sem, vmem_limit_bytes=VMEM_LIMIT)


ADA_COLS = 1536


def _ada_kernel(cv_ref, w_ref, b_ref, o_ref):
    a = _silu(cv_ref[...]).astype(BF16)
    o_ref[...] = _dot(a, w_ref[...].astype(BF16)) + b_ref[...]


def _ada(cv8, w_ada, b_ada):
    return pl.pallas_call(
        _ada_kernel,
        grid=(DEPTH, 6 * D_MODEL // ADA_COLS),
        in_specs=[
            pl.BlockSpec((8, D_MODEL), lambda l, j: (0, 0)),
            pl.BlockSpec((None, D_MODEL, ADA_COLS), lambda l, j: (l, 0, j)),
            pl.BlockSpec((None, 1, ADA_COLS), lambda l, j: (l, 0, j)),
        ],
        out_specs=pl.BlockSpec((None, 8, ADA_COLS), lambda l, j: (l, 0, j)),
        out_shape=jax.ShapeDtypeStruct((DEPTH, 8, 6 * D_MODEL), F32),
        compiler_params=_params("arbitrary", "arbitrary"),
        name="ada",
    )(cv8, w_ada, b_ada.reshape(DEPTH, 1, 6 * D_MODEL))


def _ctx_lat_specs(width, tile=ROW_TILE):
    n_ctx = N_CTX // tile
    return [pl.BlockSpec((tile, width), lambda i, *_: (jnp.minimum(i, n_ctx - 1), 0)),
            pl.BlockSpec((tile, width), lambda i, *_: (jnp.maximum(i - n_ctx, 0), 0))]


def _row_specs(parts, width, tile=ROW_TILE):
    if len(parts) == 2:
        return _ctx_lat_specs(width, tile)
    return [pl.BlockSpec((tile, width), lambda i, *_: (i, 0))]


def _read_rows(refs, tile=ROW_TILE):
    if len(refs) == 1:
        return refs[0][...]
    return jnp.where(pl.program_id(0) < N_CTX // tile, refs[0][...], refs[1][...])


def _in_kernel(*refs, n_x):
    x = _read_rows(refs[:n_x], OUT_ROWS)
    mod_ref, g_ref, w_ref, umla_ref, uhg_ref, ufn_ref, uswa_ref = refs[n_x:]
    sh1 = mod_ref[:, 0:D_MODEL]
    sc1 = mod_ref[:, D_MODEL:2 * D_MODEL]
    h = _rms(x, g_ref[...]) * (1.0 + sc1) + sh1
    u = _dot(h.astype(BF16), w_ref[...])
    o = 0
    for ref, width in ((umla_ref, MLA_PACK), (uhg_ref, HG_IN), (ufn_ref, FN_IN), (uswa_ref, SWA_IN)):
        ref[...] = u[:, o:o + width].astype(ref.dtype)
        o += width


def _in_proj(x_parts, mod, g, w, l):
    row = lambda i: (i, 0)
    widths = (MLA_PACK, HG_IN, FN_IN, SWA_IN)
    return pl.pallas_call(
        functools.partial(_in_kernel, n_x=len(x_parts)),
        grid=(N_TOK // OUT_ROWS,),
        in_specs=_row_specs(x_parts, D_MODEL, OUT_ROWS) + [
            pl.BlockSpec((None, 1, 6 * D_MODEL), lambda i: (_mod_row(i * (OUT_ROWS // ROW_TILE)), 0, 0)),
            pl.BlockSpec((1, D_MODEL), lambda i: (0, 0)),
            pl.BlockSpec((None, D_MODEL, U_COLS), lambda i: (l, 0, 0))],
        out_specs=[pl.BlockSpec((OUT_ROWS, wd), row) for wd in widths],
        out_shape=[jax.ShapeDtypeStruct((N_TOK, wd), F32) for wd in widths],
        compiler_params=_params("arbitrary"),
        name="in_proj",
    )(*x_parts, mod, g, w)


def _rope(x, cos, sin_signed):
    lane = lax.broadcasted_iota(jnp.int32, x.shape, 1)
    width = x.shape[1]
    swapped = jnp.where(lane % 2 == 0, pltpu.roll(x, width - 1, 1), pltpu.roll(x, 1, 1))
    return x * cos + swapped * sin_signed


def _stack_heads(x, n_heads, head_w):
    lane = lax.broadcasted_iota(jnp.int32, x.shape, 1)
    return jnp.concatenate([jnp.where(lane // head_w == h, x, 0.0) for h in range(n_heads)], axis=0)


def _unstack_heads(o, n_heads, head_w):
    t = o.shape[0] // n_heads
    lane = lax.broadcasted_iota(jnp.int32, (t, o.shape[1]), 1)
    out = jnp.zeros((t, o.shape[1]), F32)
    for h in range(n_heads):
        out = jnp.where(lane // head_w == h, o[h * t:(h + 1) * t], out)
    return out


MLA_SCALE = (MLA_NOPE + MLA_ROPE) ** -0.5
MLA_QW = MLA_HEADS * MLA_NOPE + MLA_HEADS * MLA_ROPE
MLA_NW = MLA_HEADS * MLA_NOPE


def _mla_attend(q, kcat, v):
    qs = jnp.concatenate([_stack_heads(q[:, :MLA_NW], MLA_HEADS, MLA_NOPE),
                          _stack_heads(q[:, MLA_NW:], MLA_HEADS, MLA_ROPE)], axis=1)
    s = _dot_nt(qs.astype(BF16), kcat) * MLA_SCALE
    p = jnp.exp(s - jnp.max(s, axis=-1, keepdims=True))
    o = _dot(p.astype(BF16), v) / jnp.sum(p, axis=-1, keepdims=True)
    return _unstack_heads(o, MLA_HEADS, MLA_V)


def _mla_ctx_kernel(u_ref, qg_ref, wq_ref, kvg_ref, wkv_ref, o_ref, ckv_ref):
    for j in range(CTX_PER_STEP):
        rows = slice(j * SEQ, (j + 1) * SEQ)
        u = u_ref[rows, :]
        q = _dot(_rms(u[:, :MLA_Q_LORA], qg_ref[...]).astype(BF16), wq_ref[...])
        ckv = _rms(u[:, MLA_Q_LORA:MLA_Q_LORA + MLA_KV_LORA], kvg_ref[...])
        ckv_ref[rows, :] = ckv
        kv = _dot(ckv.astype(BF16), wkv_ref[...])
        kr4 = u[:, MLA_Q_LORA + MLA_KV_LORA:]
        kcat = jnp.concatenate([kv[:, :MLA_NW], kr4], axis=1).astype(BF16)
        o_ref[rows, :] = _mla_attend(q, kcat, kv[:, MLA_NW:].astype(BF16)).astype(o_ref.dtype)


def _mla_ctx(u_mla, qg, wq, kvg, wkv):
    full = lambda shape: pl.BlockSpec(shape, lambda b: (0, 0))
    rows = CTX_PER_STEP * SEQ
    return pl.pallas_call(
        _mla_ctx_kernel,
        grid=(BATCH // CTX_PER_STEP,),
        in_specs=[pl.BlockSpec((rows, MLA_PACK), lambda b: (b, 0)),
                  full((1, MLA_Q_LORA)), full((MLA_Q_LORA, MLA_QW)),
                  full((1, MLA_KV_LORA)), full((MLA_KV_LORA, 2 * MLA_NW))],
        out_specs=[pl.BlockSpec((rows, MLA_NW), lambda b: (b, 0)),
                   pl.BlockSpec((rows, MLA_KV_LORA), lambda b: (b, 0))],
        out_shape=[jax.ShapeDtypeStruct((N_CTX, MLA_NW), BF16),
                   jax.ShapeDtypeStruct((N_CTX, MLA_KV_LORA), F32)],
        compiler_params=_params("arbitrary"),
        name="mla_ctx",
    )(u_mla, qg, wq, kvg, wkv)


MLA_TK = PAST_LEN + DEC_SEQ


def _mla_lat_kernel(u_ref, cckv_ref, ckr_ref, cos_ref, sin_ref, qg_ref, wq_ref, kvg_ref, wkv_ref,
                    o_ref, kcat_s, v_s):
    i = pl.program_id(1)

    @pl.when(i == 0)
    def _():
        u = u_ref[...]
        ckv_new = _rms(u[:, MLA_Q_LORA:MLA_Q_LORA + MLA_KV_LORA], kvg_ref[...])
        ckv_all = jnp.concatenate([cckv_ref[...], ckv_new], axis=0)
        kv = _dot(ckv_all.astype(BF16), wkv_ref[...])
        kr_new = _rope(u[:, MLA_Q_LORA + MLA_KV_LORA:], cos_ref[...], sin_ref[...])
        ckr = ckr_ref[...]
        kr_all = jnp.concatenate([jnp.concatenate([ckr] * MLA_HEADS, axis=1), kr_new], axis=0)
        kcat_s[...] = jnp.concatenate([kv[:, :MLA_NW], kr_all], axis=1).astype(BF16)
        v_s[...] = kv[:, MLA_NW:].astype(BF16)

    r0 = pl.multiple_of(i * MLA_QBLK, MLA_QBLK)
    cq = u_ref[pl.ds(r0, MLA_QBLK), 0:MLA_Q_LORA]
    q = _dot(_rms(cq, qg_ref[...]).astype(BF16), wq_ref[...])
    qr = _rope(q[:, MLA_NW:], cos_ref[pl.ds(r0, MLA_QBLK), :], sin_ref[pl.ds(r0, MLA_QBLK), :])
    q = jnp.concatenate([q[:, :MLA_NW], qr], axis=1)
    o_ref[...] = _mla_attend(q, kcat_s[...], v_s[...]).astype(o_ref.dtype)


def _mla_lat(u_mla, cache_ckv, cache_kr, l, cos, sin, qg, wq, kvg, wkv):
    full = lambda shape: pl.BlockSpec(shape, lambda b, i: (0, 0))
    nq = DEC_SEQ // MLA_QBLK
    return pl.pallas_call(
        _mla_lat_kernel,
        grid=(DEC_BATCH, nq),
        in_specs=[pl.BlockSpec((DEC_SEQ, MLA_PACK), lambda b, i: (N_CTX // DEC_SEQ + b, 0)),
                  pl.BlockSpec((None, None, PAST_LEN, MLA_KV_LORA), lambda b, i: (b, l, 0, 0)),
                  pl.BlockSpec((None, None, PAST_LEN, MLA_ROPE), lambda b, i: (b, l, 0, 0)),
                  full((DEC_SEQ, MLA_HEADS * MLA_ROPE)), full((DEC_SEQ, MLA_HEADS * MLA_ROPE)),
                  full((1, MLA_Q_LORA)), full((MLA_Q_LORA, MLA_QW)),
                  full((1, MLA_KV_LORA)), full((MLA_KV_LORA, 2 * MLA_NW))],
        out_specs=pl.BlockSpec((MLA_QBLK, MLA_NW), lambda b, i: (b * nq + i, 0)),
        out_shape=jax.ShapeDtypeStruct((N_LAT, MLA_NW), BF16),
        scratch_shapes=[pltpu.VMEM((MLA_TK, MLA_QW), BF16), pltpu.VMEM((MLA_TK, MLA_NW), BF16)],
        compiler_params=_params("arbitrary", "arbitrary"),
        name="mla_lat",
    )(u_mla, cache_ckv, cache_kr, cos, sin, qg, wq, kvg, wkv)


def _hgrn_kernel(*refs, seq, n_seq, has_state):
    if has_state:
        (u_ref, lbf_ref, lbb_ref, ng_ref, s0_ref, o_ref,
         q_s, kf_s, gf_s, kb_s, gb_s, of_s, ob_s, stf_s, stb_s) = refs
    else:
        (u_ref, lbf_ref, lbb_ref, ng_ref, o_ref, so_ref,
         q_s, kf_s, gf_s, kb_s, gb_s, of_s, ob_s, stf_s, stb_s) = refs
    C = HG_CHUNK
    W = HG_W

    q_s[...] = _silu(u_ref[:, 0:W])
    ff = lbf_ref[...] + (1.0 - lbf_ref[...]) * jax.nn.sigmoid(u_ref[:, W:2 * W])
    kf_s[...] = 1.0 - ff
    gf_s[...] = jnp.log(ff)
    fb = lbb_ref[...] + (1.0 - lbb_ref[...]) * jax.nn.sigmoid(u_ref[:, 2 * W:3 * W])
    kb_s[...] = 1.0 - fb
    gb_s[...] = jnp.log(fb)

    rr = lax.broadcasted_iota(jnp.int32, (W, W), 0)
    cc = lax.broadcasted_iota(jnp.int32, (W, W), 1)
    blockdiag = rr // HG_DK == cc // HG_DK
    if has_state:
        for st, d in ((stf_s, 0), (stb_s, 1)):
            rows = []
            for h in range(HG_HEADS):
                z = lambda n: jnp.zeros((HG_DV, n * HG_DK), F32)
                parts = ([z(h)] if h else []) + [s0_ref[d, h]] + ([z(HG_HEADS - 1 - h)] if h < HG_HEADS - 1 else [])
                rows.append(jnp.concatenate(parts, axis=1) if len(parts) > 1 else parts[0])
            st[0] = jnp.concatenate(rows, axis=0)
    else:
        stf_s[...] = jnp.zeros_like(stf_s)
        stb_s[...] = jnp.zeros_like(stb_s)

    B = HG_BLOCK
    per_block = B // C
    n_blocks = seq // B
    ri = lax.broadcasted_iota(jnp.int32, (B, B), 0)
    ci = lax.broadcasted_iota(jnp.int32, (B, B), 1)
    same_chunk = ri // C == ci // C
    rs = lax.broadcasted_iota(jnp.int32, (HG_HEADS * B, B), 0) % B
    cs = lax.broadcasted_iota(jnp.int32, (HG_HEADS * B, B), 1)
    same_chunk_s = rs // C == cs // C

    sums_f = jnp.where(same_chunk & (ci <= ri), 1.0, 0.0)
    sums_b = jnp.where(same_chunk & (ci >= ri), 1.0, 0.0)
    keep_f = same_chunk_s & (rs >= cs)
    keep_b = same_chunk_s & (cs >= rs)

    def chunk_row(x, i):
        x3 = x.reshape(per_block, C, W)
        return jnp.broadcast_to(x3[:, i:i + 1, :], (per_block, C, W)).reshape(B, W)

    def block(r, k_s, g_s, o_s, st_s, sum_mat, keep, order, mid, far):
        q = q_s[pl.ds(r, B), :]
        k = k_s[pl.ds(r, B), :]
        v = u_ref[pl.ds(r, B), 3 * W:4 * W].astype(BF16)
        G = _dot_exact_lhs(sum_mat, g_s[pl.ds(r, B), :])
        Gq = G - chunk_row(G, mid)
        Gk2 = chunk_row(G, far) - G
        qe = _stack_heads(q * jnp.exp(Gq), HG_HEADS, HG_DK)
        ke = k * jnp.exp(-Gq)
        A = jnp.where(keep, _dot_nt(qe.astype(BF16), ke.astype(BF16)), 0.0)
        o_intra = _unstack_heads(_dot(A.astype(BF16), v), HG_HEADS, HG_DV)
        qg = (q * jnp.exp(G)).astype(BF16)
        k2 = (k * jnp.exp(Gk2)).astype(BF16)
        decay = jnp.exp(G + Gk2)
        st = st_s[...]
        o_inter = [None] * per_block
        for c in order:
            rows = slice(c * C, (c + 1) * C)
            o_inter[c] = _dot_nt(st.astype(BF16), qg[rows])
            st = st * decay[c * C:c * C + 1] + jnp.where(blockdiag, _dot_tn(v[rows], k2[rows]), 0.0)
        st_s[...] = st
        o_s[pl.ds(r, B), :] = o_intra + jnp.concatenate(o_inter, axis=1).T

    def fwd(j, r):
        block(r, kf_s, gf_s, of_s, stf_s.at[j], sums_f, keep_f, range(per_block), C // 2 - 1, C - 1)

    def bwd(j, r):
        block(r, kb_s, gb_s, ob_s, stb_s.at[j], sums_b, keep_b, range(per_block - 1, -1, -1), C // 2, 0)

    for j in range(n_seq):
        if n_blocks == 1:
            fwd(j, j * seq)
            bwd(j, j * seq)
        else:
            def both_directions(i, carry, j=j):
                fwd(j, pl.multiple_of(j * seq + i * B, B))
                bwd(j, pl.multiple_of(j * seq + (n_blocks - 1 - i) * B, B))
                return carry

            lax.fori_loop(0, n_blocks, both_directions, 0)

    o = of_s[...] + ob_s[...]
    ms = _dot_exact_rhs(o * o, jnp.where(blockdiag, 1.0 / HG_DV, 0.0))
    on = o * lax.rsqrt(ms + EPS) * ng_ref[...]
    o_ref[...] = (on * _silu(u_ref[:, 4 * W:5 * W])).astype(o_ref.dtype)

    if not has_state:
        for j in range(n_seq):
            for st, d in ((stf_s, 0), (stb_s, 1)):
                for h in range(HG_HEADS):
                    so_ref[j, d, h] = st[j, h * HG_DV:(h + 1) * HG_DV, h * HG_DK:(h + 1) * HG_DK]


def _hgrn(u_hg, lbf, lbb, ng4, state_t, *, seq, n_batch, row_block0):
    has_state = state_t is not None
    n_seq = 1 if has_state else HG_CTX_PER_STEP
    rows = n_seq * seq
    full = lambda shape: pl.BlockSpec(shape, lambda b: (0, 0))
    in_specs = [pl.BlockSpec((rows, HG_IN), lambda b: (row_block0 + b, 0)),
                full((1, HG_W)), full((1, HG_W)), full((1, HG_W))]
    args = [u_hg, lbf, lbb, ng4]
    o_spec = pl.BlockSpec((rows, HG_W), lambda b: (b, 0))
    o_shape = jax.ShapeDtypeStruct((n_batch * seq, HG_W), BF16)
    if has_state:
        in_specs.append(pl.BlockSpec((None, 2, HG_HEADS, HG_DV, HG_DK), lambda b: (b, 0, 0, 0, 0)))
        args.append(state_t)
        out_specs, out_shape = o_spec, o_shape
    else:
        out_specs = [o_spec, pl.BlockSpec((n_seq, 2, HG_HEADS, HG_DV, HG_DK), lambda b: (b, 0, 0, 0, 0))]
        out_shape = [o_shape, jax.ShapeDtypeStruct((n_batch, 2, HG_HEADS, HG_DV, HG_DK), F32)]
    return pl.pallas_call(
        functools.partial(_hgrn_kernel, seq=seq, n_seq=n_seq, has_state=has_state),
        grid=(n_batch // n_seq,),
        in_specs=in_specs, out_specs=out_specs, out_shape=out_shape,
        scratch_shapes=[pltpu.VMEM((rows, HG_W), F32)] * 7 + [pltpu.VMEM((n_seq, HG_W, HG_W), F32)] * 2,
        compiler_params=_params("arbitrary"),
        name="hgrn_lat" if has_state else "hgrn_ctx",
    )(*args)


def _dft_tables(n):
    j = np.arange(n, dtype=np.int64)
    ang = 2.0 * np.pi * ((j[:, None] * j[None, :]) % n).astype(np.float64) / n
    return np.cos(ang) / np.sqrt(n), np.sin(ang) / np.sqrt(n)


def _fourier_tables(seq):
    gw = FN_WIDTH // FN_GROUPS
    cg, sg = _dft_tables(gw)
    eye = np.eye(FN_GROUPS)
    chan = np.concatenate([np.kron(eye, cg), np.kron(eye, sg)], axis=1)
    ct, st = _dft_tables(seq)
    pos = np.concatenate([ct, -st], axis=1)
    return jnp.asarray(chan, F32).astype(BF16), jnp.asarray(pos, F32).astype(BF16)


def _fourier_kernel(x_ref, chan_ref, pos_ref, w_ref, o_ref, *, seq, n_seq):
    for j in range(n_seq):
        rows = slice(j * seq, (j + 1) * seq)
        x12 = _dot(x_ref[rows, :].astype(BF16), chan_ref[...])
        z = jnp.concatenate([x12[:, :FN_WIDTH], x12[:, FN_WIDTH:]], axis=0).astype(BF16)
        y = _dot(pos_ref[...], z)
        o_ref[rows, :] = _dot(y.astype(BF16), w_ref[...]).astype(o_ref.dtype)


def _fourier(u_fn, w, *, seq, n_batch, row_block0):
    chan, pos = _fourier_tables(seq)
    full = lambda shape: pl.BlockSpec(shape, lambda b: (0, 0))
    n_seq = CTX_PER_STEP if seq == SEQ else 1
    return pl.pallas_call(
        functools.partial(_fourier_kernel, seq=seq, n_seq=n_seq),
        grid=(n_batch // n_seq,),
        in_specs=[pl.BlockSpec((n_seq * seq, FN_WIDTH), lambda b: (row_block0 + b, 0)),
                  full((FN_WIDTH, 2 * FN_WIDTH)), full((seq, 2 * seq)), full((FN_WIDTH, FN_WIDTH))],
        out_specs=pl.BlockSpec((n_seq * seq, FN_WIDTH), lambda b: (b, 0)),
        out_shape=jax.ShapeDtypeStruct((n_batch * seq, FN_WIDTH), BF16),
        compiler_params=_params("arbitrary"),
        name="fourier",
    )(u_fn, chan, pos, w)


SWA_SCALE = SWA_HD ** -0.5
SWA_QW = SWA_HEADS * SWA_HD
SWA_KW = SWA_KV_HEADS * SWA_HD
SWA_STACK_ORDER = (0, 2, 1, 3)


def _swa_stack_q(q):
    return jnp.concatenate([_stack_heads(q[:, :SWA_KW], SWA_KV_HEADS, SWA_HD),
                            _stack_heads(q[:, SWA_KW:], SWA_KV_HEADS, SWA_HD)], axis=0)


def _swa_unstack_o(o):
    t = o.shape[0] // SWA_HEADS
    return jnp.concatenate([_unstack_heads(o[:2 * t], SWA_KV_HEADS, SWA_HD),
                            _unstack_heads(o[2 * t:], SWA_KV_HEADS, SWA_HD)], axis=1)


def _sink_rows(sink_ref, t):
    return jnp.concatenate([jnp.full((t, 1), sink_ref[h], F32) for h in SWA_STACK_ORDER], axis=0)


def _swa_ctx_kernel(sink_ref, u_ref, o_ref):
    sink = _sink_rows(sink_ref, SEQ)
    for j in range(SWA_CTX_PER_STEP):
        rows = slice(j * SEQ, (j + 1) * SEQ)
        u = u_ref[rows, :]
        qs = _swa_stack_q(u[:, :SWA_QW]).astype(BF16)
        k = u[:, SWA_QW:SWA_QW + SWA_KW].astype(BF16)
        v = u[:, SWA_QW + SWA_KW:].astype(BF16)
        s = _dot_nt(qs, k) * SWA_SCALE
        m = jnp.maximum(jnp.max(s, axis=-1, keepdims=True), sink)
        p = jnp.exp(s - m)
        denom = jnp.sum(p, axis=-1, keepdims=True) + jnp.exp(sink - m)
        o_ref[rows, :] = _swa_unstack_o(_dot(p.astype(BF16), v) / denom).astype(o_ref.dtype)


def _swa_ctx(u_swa, sink):
    rows = SWA_CTX_PER_STEP * SEQ
    return pl.pallas_call(
        _swa_ctx_kernel,
        grid=(BATCH // SWA_CTX_PER_STEP,),
        in_specs=[pl.BlockSpec(memory_space=pltpu.SMEM),
                  pl.BlockSpec((rows, SWA_IN), lambda b: (b, 0))],
        out_specs=pl.BlockSpec((rows, SWA_QW), lambda b: (b, 0)),
        out_shape=jax.ShapeDtypeStruct((N_CTX, SWA_QW), BF16),
        compiler_params=_params("arbitrary"),
        name="swa_ctx",
    )(sink, u_swa)


SWA_PAD = DEC_SEQ + 2 * SWA_QBLK


def _swa_lat_kernel(sink_ref, u_ref, kc_ref, vc_ref, cosq_ref, sinq_ref, cosk_ref, sin_k_ref,
                    o_ref, k_s, v_s):
    i = pl.program_id(1)
    B = SWA_QBLK

    @pl.when(i == 0)
    def _():
        zeros = jnp.zeros((B, SWA_KW), BF16)
        k = _rope(u_ref[:, SWA_QW:SWA_QW + SWA_KW], cosk_ref[...], sin_k_ref[...]).astype(BF16)
        k_s[...] = jnp.concatenate([zeros, k, zeros], axis=0)
        v_s[...] = jnp.concatenate([zeros, u_ref[:, SWA_QW + SWA_KW:].astype(BF16), zeros], axis=0)

    r0 = pl.multiple_of(i * B, B)
    q = _rope(u_ref[pl.ds(r0, B), 0:SWA_QW], cosq_ref[pl.ds(r0, B), :], sinq_ref[pl.ds(r0, B), :])
    qs = _swa_stack_q(q).astype(BF16)
    s_loc = _dot_nt(qs, k_s[pl.ds(r0, 3 * B), :]) * SWA_SCALE
    row = lax.broadcasted_iota(jnp.int32, s_loc.shape, 0) % B
    col = lax.broadcasted_iota(jnp.int32, s_loc.shape, 1)
    kpos = r0 - B + col
    valid = (jnp.abs(row + B - col) <= WINDOW) & (kpos >= 0) & (kpos < DEC_SEQ)
    s_loc = jnp.where(valid, s_loc, NEG_INF)
    s_ctx = _dot_nt(qs, kc_ref[...].astype(BF16)) * SWA_SCALE
    sink = _sink_rows(sink_ref, B)
    m = jnp.maximum(jnp.maximum(jnp.max(s_loc, axis=-1, keepdims=True),
                                jnp.max(s_ctx, axis=-1, keepdims=True)), sink)
    p_loc = jnp.exp(s_loc - m)
    p_ctx = jnp.exp(s_ctx - m)
    denom = (jnp.sum(p_loc, axis=-1, keepdims=True) + jnp.sum(p_ctx, axis=-1, keepdims=True)
             + jnp.exp(sink - m))
    o = _dot(p_loc.astype(BF16), v_s[pl.ds(r0, 3 * B), :]) + _dot(p_ctx.astype(BF16), vc_ref[...].astype(BF16))
    o_ref[...] = _swa_unstack_o(o / denom).astype(o_ref.dtype)


def _swa_lat(u_swa, cache_k, cache_v, l, sink, cosq, sinq, cosk, sink_k):
    full = lambda shape: pl.BlockSpec(shape, lambda b, i: (0, 0))
    nq = DEC_SEQ // SWA_QBLK
    cache_spec = pl.BlockSpec((None, None, PAST_LEN, SWA_KW), lambda b, i: (b, l, 0, 0))
    return pl.pallas_call(
        _swa_lat_kernel,
        grid=(DEC_BATCH, nq),
        in_specs=[pl.BlockSpec(memory_space=pltpu.SMEM),
                  pl.BlockSpec((DEC_SEQ, SWA_IN), lambda b, i: (N_CTX // DEC_SEQ + b, 0)),
                  cache_spec, cache_spec,
                  full((DEC_SEQ, SWA_QW)), full((DEC_SEQ, SWA_QW)),
                  full((DEC_SEQ, SWA_KW)), full((DEC_SEQ, SWA_KW))],
        out_specs=pl.BlockSpec((SWA_QBLK, SWA_QW), lambda b, i: (b * nq + i, 0)),
        out_shape=jax.ShapeDtypeStruct((N_LAT, SWA_QW), BF16),
        scratch_shapes=[pltpu.VMEM((SWA_PAD, SWA_KW), BF16), pltpu.VMEM((SWA_PAD, SWA_KW), BF16)],
        compiler_params=_params("arbitrary", "arbitrary"),
        name="swa_lat",
    )(sink, u_swa, cache_k, cache_v, cosq, sinq, cosk, sink_k)


N_BLK = N_TOK // ROW_TILE
SEG_ALIGN = 16
LOCAL_ROWS = ROW_TILE * TOP_K + N_EXPERTS * SEG_ALIGN
LOCAL_CHUNK = 512
EXPERT_TILE = 768
SORTED_ROWS = -(-(N_TOK * TOP_K + N_BLK * N_EXPERTS * SEG_ALIGN + N_EXPERTS * (EXPERT_TILE + SEG_ALIGN))
                // EXPERT_TILE) * EXPERT_TILE
MAX_TILES = SORTED_ROWS // EXPERT_TILE
NOT_PICKED = -1.0
NO_DEST = 4095.0


def _out_kernel(*refs, n_x):
    x_all = _read_rows(refs[:n_x], OUT_ROWS)
    mix_all = [_read_rows(refs[n_x + 2 * j:n_x + 2 * j + 2], OUT_ROWS) for j in range(4)]
    out_refs = refs[n_x + 8:]
    for blk in range(OUT_ROWS // ROW_TILE):
        rows = slice(blk * ROW_TILE, (blk + 1) * ROW_TILE)
        _out_block(x_all[rows], [m[rows] for m in mix_all], out_refs, blk)


def _out_block(x, mixers, refs, blk):
    (mod_ref, g_ref, wo_ref, wr_ref, br_ref, wsg_ref, wsu_ref, wsd_ref,
     x1_ref, h_ref, gate_ref, rank_ref, cnt_ref) = refs
    rows = slice(blk * ROW_TILE, (blk + 1) * ROW_TILE)
    mix = jnp.zeros((ROW_TILE, D_MODEL), F32)
    for j in range(4):
        mix = mix + _dot(mixers[j], wo_ref[j * 256:(j + 1) * 256, :])
    g1 = mod_ref[:, 2 * D_MODEL:3 * D_MODEL]
    sh2 = mod_ref[:, 3 * D_MODEL:4 * D_MODEL]
    sc2 = mod_ref[:, 4 * D_MODEL:5 * D_MODEL]
    g2 = mod_ref[:, 5 * D_MODEL:6 * D_MODEL]
    x1 = x + g1 * mix
    h = _rms(x1, g_ref[...]) * (1.0 + sc2) + sh2
    hb = h.astype(BF16)
    h_ref[rows, :] = hb

    w_hi = wr_ref[...].astype(BF16)
    w_lo = (wr_ref[...] - w_hi.astype(F32)).astype(BF16)
    h_lo = (h - hb.astype(F32)).astype(BF16)
    logits = ((_dot(h_lo, w_hi) + _dot(hb, w_lo)) + _dot(hb, w_hi)).T
    scores = jax.nn.sigmoid(logits)
    sel = scores + br_ref[...]
    eidx = lax.broadcasted_iota(jnp.int32, sel.shape, 0)
    gate = jnp.zeros_like(scores)
    picked = jnp.zeros_like(scores)
    for _ in range(TOP_K):
        best = jnp.max(sel, axis=0, keepdims=True)
        first = jnp.min(jnp.where(sel == best, eidx, N_EXPERTS), axis=0, keepdims=True)
        pick = eidx == first
        gate = jnp.where(pick, scores, gate)
        picked = jnp.where(pick, 1.0, picked)
        sel = jnp.where(pick, -jnp.inf, sel)
    gate = ROUTE_SCALE * gate / jnp.sum(gate, axis=0, keepdims=True)

    ti = lax.broadcasted_iota(jnp.int32, (ROW_TILE, ROW_TILE), 0)
    tj = lax.broadcasted_iota(jnp.int32, (ROW_TILE, ROW_TILE), 1)
    pb = picked.astype(BF16)
    rank = _dot(pb, jnp.where(ti < tj, 1.0, 0.0).astype(BF16))
    gate_ref[:, rows] = gate
    rank_ref[:, rows] = jnp.where(picked > 0.0, rank, NOT_PICKED)
    counts = _dot_nt(jnp.ones((8, ROW_TILE), BF16), pb)
    cnt_ref[blk] = jnp.concatenate([counts, jnp.zeros_like(counts)], axis=1)

    hid = _silu(_dot(hb, wsg_ref[...])) * _dot(hb, wsu_ref[...])
    x1_ref[rows, :] = x1 + g2 * _dot(hid.astype(BF16), wsd_ref[...])


def _out_proj(x_parts, mixer_pairs, mod, g, wo, l, wr, br, wsg, wsu, wsd):
    row = lambda i: (i, 0)
    col = lambda i: (0, i)
    full = lambda shape: pl.BlockSpec(shape, lambda i: (0, 0))
    per_step = OUT_ROWS // ROW_TILE
    et_spec = pl.BlockSpec((N_EXPERTS, OUT_ROWS), col)
    et_shape = jax.ShapeDtypeStruct((N_EXPERTS, N_TOK), F32)
    return pl.pallas_call(
        functools.partial(_out_kernel, n_x=len(x_parts)),
        grid=(N_TOK // OUT_ROWS,),
        in_specs=_row_specs(x_parts, D_MODEL, OUT_ROWS) + 4 * _ctx_lat_specs(256, OUT_ROWS) + [
            pl.BlockSpec((None, 1, 6 * D_MODEL), lambda i: (_mod_row(i * per_step), 0, 0)),
            full((1, D_MODEL)), pl.BlockSpec((None, D_MODEL, D_MODEL), lambda i: (l, 0, 0)),
            full((D_MODEL, N_EXPERTS)), full((N_EXPERTS, 1)),
            full((D_MODEL, D_SHARED)), full((D_MODEL, D_SHARED)), full((D_SHARED, D_MODEL))],
        out_specs=[pl.BlockSpec((OUT_ROWS, D_MODEL), row), pl.BlockSpec((OUT_ROWS, D_MODEL), row),
                   et_spec, et_spec,
                   pl.BlockSpec((per_step, 8, 128), lambda i: (i, 0, 0))],
        out_shape=[jax.ShapeDtypeStruct((N_TOK, D_MODEL), F32),
                   jax.ShapeDtypeStruct((N_TOK, D_MODEL), BF16),
                   et_shape, et_shape,
                   jax.ShapeDtypeStruct((N_BLK, 8, 128), F32)],
        compiler_params=_params("arbitrary"),
        name="out_proj",
    )(*x_parts, *[a for pair in mixer_pairs for a in pair], mod, g, wo, wr, br, wsg, wsu, wsd)


def _segment_plan(cnt):
    cnt = cnt[:, 0, :N_EXPERTS].astype(jnp.int32)
    seg = jnp.maximum((cnt + (SEG_ALIGN - 1)) // SEG_ALIGN, 1) * SEG_ALIGN
    local = jnp.cumsum(seg, axis=1) - seg
    total = jnp.sum(seg, axis=1)
    per_expert = jnp.sum(seg, axis=0)
    padded = (per_expert + SEG_ALIGN + (EXPERT_TILE - 1)) // EXPERT_TILE * EXPERT_TILE
    ends = jnp.cumsum(padded)
    start = ends - padded
    dest = start[None, :] + jnp.cumsum(seg, axis=0) - seg
    n_tiles = ends[-1] // EXPERT_TILE
    tiles = jnp.arange(MAX_TILES, dtype=jnp.int32)
    tile_expert = jnp.sum((ends // EXPERT_TILE)[None, :] <= jnp.minimum(tiles, n_tiles - 1)[:, None], axis=1)
    tile_expert = tile_expert.astype(jnp.int32)
    plan = dict(seg=seg, local=local, total=total.astype(jnp.int32), dest=dest.astype(jnp.int32),
                tail_start=(start + per_expert).astype(jnp.int32), tail_rows=(padded - per_expert).astype(jnp.int32),
                n_tiles=n_tiles.reshape(1).astype(jnp.int32), tile_expert=tile_expert)
    segf, localf = seg.astype(F32), local.astype(F32)
    pad_lanes = lambda a: jnp.concatenate([a, jnp.zeros_like(a)], axis=1)[:, None, :]
    plan.update(seg_row=pad_lanes(segf), local_row=pad_lanes(localf),
                seg_col=segf[:, :, None], local_col=localf[:, :, None])
    return plan


def _local_dest_digits(rank, local_start):
    dest = jnp.where(rank >= 0.0, local_start + rank, NO_DEST)
    hi = jnp.floor(dest * (1.0 / 64.0))
    return hi, dest - 64.0 * hi


def _dispatch_kernel(dest_ref, seg_ref, local_ref, total_ref, tail_start_ref, tail_rows_ref, nt_ref,
                     h_ref, rank_ref, local_col_ref, local_row_ref, seg_row_ref,
                     xs_hbm, buf, zeros, sems, zsem, usem):
    b = pl.program_id(0)
    slot = b % 2

    def wait_block(blk, s):
        n = pl.multiple_of(total_ref[blk], SEG_ALIGN)
        pltpu.make_async_copy(buf.at[s, pl.ds(0, n)], xs_hbm.at[pl.ds(0, n)], sems.at[s]).wait()

    @pl.when(b == 0)
    def _():
        zeros[...] = jnp.zeros_like(zeros)

    def unused_tiles(action):
        for k in range(-(-MAX_TILES // N_BLK)):
            t = nt_ref[0] + b + k * N_BLK

            @pl.when(t < MAX_TILES)
            def _():
                cp = pltpu.make_async_copy(zeros.at[pl.ds(0, EXPERT_TILE)],
                                           xs_hbm.at[pl.ds(pl.multiple_of(t * EXPERT_TILE, EXPERT_TILE),
                                                           EXPERT_TILE)], usem)
                cp.start() if action == "start" else cp.wait()

    unused_tiles("start")

    @pl.when(b >= 2)
    def _():
        wait_block(b - 2, slot)

    hi, lo = _local_dest_digits(rank_ref[...], local_col_ref[...])
    code = jnp.concatenate([hi, lo], axis=0).astype(BF16)
    hb = h_ref[...]
    lstart = local_row_ref[:, :N_EXPERTS]
    lend = lstart + seg_row_ref[:, :N_EXPERTS]
    r = lax.broadcasted_iota(jnp.int32, (LOCAL_CHUNK, N_EXPERTS), 0).astype(F32)
    rr = lax.broadcasted_iota(jnp.int32, (LOCAL_CHUNK, ROW_TILE), 0).astype(F32)
    def compact(c):
        first = float(c * LOCAL_CHUNK)
        member = (r >= lstart - first) & (r < lend - first)
        table = jnp.concatenate([jnp.where(member, 64.0, 0.0), jnp.where(member, 1.0, 0.0)], axis=1).astype(BF16)
        d = _dot(table, code)
        onehot = jnp.where(d == rr + first, 1.0, 0.0).astype(BF16)
        buf[slot, c * LOCAL_CHUNK:(c + 1) * LOCAL_CHUNK, :] = _dot(onehot, hb).astype(BF16)

    n_chunks = LOCAL_ROWS // LOCAL_CHUNK
    for c in range(n_chunks - 1):
        compact(c)
    pl.when(total_ref[b] > (n_chunks - 1) * LOCAL_CHUNK)(functools.partial(compact, n_chunks - 1))

    for e in range(N_EXPERTS):
        n = pl.multiple_of(seg_ref[b, e], SEG_ALIGN)
        src = pl.multiple_of(local_ref[b, e], SEG_ALIGN)
        dst = pl.multiple_of(dest_ref[b, e], SEG_ALIGN)
        pltpu.make_async_copy(buf.at[slot, pl.ds(src, n)], xs_hbm.at[pl.ds(dst, n)],
                              sems.at[slot]).start(priority=e % 2)

    unused_tiles("wait")

    @pl.when(b == N_BLK - 1)
    def _():
        def tail(e):
            n = pl.multiple_of(tail_rows_ref[e], SEG_ALIGN)
            dst = pl.multiple_of(tail_start_ref[e], SEG_ALIGN)
            return pltpu.make_async_copy(zeros.at[pl.ds(0, n)], xs_hbm.at[pl.ds(dst, n)], zsem)

        for e in range(N_EXPERTS):
            tail(e).start()
        for e in range(N_EXPERTS):
            tail(e).wait()
        wait_block(b - 1, 1 - slot)
        wait_block(b, slot)


def _dispatch(h, rank_t, plan):
    blk = lambda shape, imap: pl.BlockSpec(shape, imap)
    return pl.pallas_call(
        _dispatch_kernel,
        grid_spec=pltpu.PrefetchScalarGridSpec(
            num_scalar_prefetch=7, grid=(N_BLK,),
            in_specs=[blk((ROW_TILE, D_MODEL), lambda b, *_: (b, 0)),
                      blk((N_EXPERTS, ROW_TILE), lambda b, *_: (0, b)),
                      blk((None, N_EXPERTS, 1), lambda b, *_: (b, 0, 0)),
                      blk((None, 1, 128), lambda b, *_: (b, 0, 0)),
                      blk((None, 1, 128), lambda b, *_: (b, 0, 0))],
            out_specs=pl.BlockSpec(memory_space=pl.ANY),
            scratch_shapes=[pltpu.VMEM((2, LOCAL_ROWS, D_MODEL), BF16),
                            pltpu.VMEM((EXPERT_TILE + SEG_ALIGN, D_MODEL), BF16),
                            pltpu.SemaphoreType.DMA((2,)), pltpu.SemaphoreType.DMA(()),
                            pltpu.SemaphoreType.DMA(())]),
        out_shape=jax.ShapeDtypeStruct((SORTED_ROWS, D_MODEL), BF16),
        compiler_params=_params("arbitrary"),
        name="dispatch",
    )(plan['dest'], plan['seg'], plan['local'], plan['total'], plan['tail_start'], plan['tail_rows'],
      plan['n_tiles'], h, rank_t, plan['local_col'], plan['local_row'], plan['seg_row'])


def _expert_kernel(te_ref, nt_ref, x_ref, wg_ref, wu_ref, wd_ref, y_ref, wg_s, wu_s, wd_s):
    i = pl.program_id(0)
    active = i < nt_ref[0]

    @pl.when((i == 0) | (te_ref[i] != te_ref[jnp.maximum(i - 1, 0)]))
    def _():
        wg_s[...] = wg_ref[...].astype(BF16)
        wu_s[...] = wu_ref[...].astype(BF16)
        wd_s[...] = wd_ref[...].astype(BF16)

    @pl.when(active)
    def _():
        x = x_ref[...]
        hid = _silu(_dot(x, wg_s[...])) * _dot(x, wu_s[...])
        y_ref[...] = _dot(hid.astype(BF16), wd_s[...]).astype(BF16)


def _experts(xs, plan, w_gate, w_up, w_down, l):
    rows = pl.BlockSpec((EXPERT_TILE, D_MODEL), lambda i, te, nt: (jnp.minimum(i, nt[0] - 1), 0))
    wspec = lambda shape: pl.BlockSpec((None, None) + shape, lambda i, te, nt: (l, te[i], 0, 0))
    return pl.pallas_call(
        _expert_kernel,
        grid_spec=pltpu.PrefetchScalarGridSpec(
            num_scalar_prefetch=2, grid=(MAX_TILES,),
            in_specs=[rows, wspec((D_MODEL, D_EXPERT)), wspec((D_MODEL, D_EXPERT)), wspec((D_EXPERT, D_MODEL))],
            out_specs=rows,
            scratch_shapes=[pltpu.VMEM((D_MODEL, D_EXPERT), BF16), pltpu.VMEM((D_MODEL, D_EXPERT), BF16),
                            pltpu.VMEM((D_EXPERT, D_MODEL), BF16)]),
        out_shape=jax.ShapeDtypeStruct((SORTED_ROWS, D_MODEL), BF16),
        input_output_aliases={2: 0},
        compiler_params=_params("arbitrary"),
        name="experts",
    )(plan['tile_expert'], plan['n_tiles'], xs, w_gate, w_up, w_down)


def _combine_kernel(*refs, final):
    refs = list(refs)
    dest_ref, seg_ref, local_ref, total_ref = refs[:4]
    x1_ref, ys_hbm, gate_ref, rank_ref, mod_ref, local_row_ref, local_col_ref, seg_col_ref = refs[4:12]
    rest = refs[12:]
    fg_ref = rest.pop(0) if final else None
    outs, (buf, sems, acc) = rest[:-3], rest[-3:]
    b = pl.program_id(0)
    slot = b % 2

    def fetch(blk, s):
        for e in range(N_EXPERTS):
            n = pl.multiple_of(seg_ref[blk, e], SEG_ALIGN)
            src = pl.multiple_of(dest_ref[blk, e], SEG_ALIGN)
            dst = pl.multiple_of(local_ref[blk, e], SEG_ALIGN)
            pltpu.make_async_copy(ys_hbm.at[pl.ds(src, n)], buf.at[s, pl.ds(dst, n)],
                                  sems.at[s]).start(priority=e % 2)

    @pl.when(b == 0)
    def _():
        buf[...] = jnp.zeros_like(buf)
        fetch(0, 0)

    def wait_rows(blk, s):
        n_rows = pl.multiple_of(total_ref[blk], SEG_ALIGN)
        pltpu.make_async_copy(ys_hbm.at[pl.ds(0, n_rows)], buf.at[s, pl.ds(0, n_rows)], sems.at[s]).wait()

    wait_rows(b, slot)
    nxt = jnp.minimum(b + 1, N_BLK - 1)
    fetch(nxt, 1 - slot)

    gate = gate_ref[...].T
    hi, lo = _local_dest_digits(rank_ref[...].T, local_row_ref[:, :N_EXPERTS])
    lhs = jnp.concatenate([jnp.concatenate([hi, lo], axis=1),
                           jnp.concatenate([jnp.zeros_like(gate), gate], axis=1)], axis=0).astype(BF16)
    lstart = local_col_ref[...]
    lend = lstart + seg_col_ref[...]
    routed = jnp.zeros((ROW_TILE, D_MODEL), F32)
    r = lax.broadcasted_iota(jnp.int32, (N_EXPERTS, LOCAL_CHUNK), 1).astype(F32)
    rr = lax.broadcasted_iota(jnp.int32, (ROW_TILE, LOCAL_CHUNK), 1).astype(F32)
    def gather(c, acc):
        first = float(c * LOCAL_CHUNK)
        member = (r >= lstart - first) & (r < lend - first)
        table = jnp.concatenate([jnp.where(member, 64.0, 0.0), jnp.where(member, 1.0, 0.0)], axis=0).astype(BF16)
        dg = _dot(lhs, table)
        weights = jnp.where(dg[:ROW_TILE] == rr + first, dg[ROW_TILE:], 0.0).astype(BF16)
        return acc + _dot(weights, buf[slot, c * LOCAL_CHUNK:(c + 1) * LOCAL_CHUNK, :])

    n_chunks = LOCAL_ROWS // LOCAL_CHUNK
    for c in range(n_chunks - 1):
        routed = gather(c, routed)
    acc[...] = routed

    @pl.when(total_ref[b] > (n_chunks - 1) * LOCAL_CHUNK)
    def _():
        acc[...] = gather(n_chunks - 1, acc[...])

    x = x1_ref[...] + mod_ref[:, 5 * D_MODEL:6 * D_MODEL] * acc[...]
    if final:
        y = _rms(x, fg_ref[...])

        @pl.when(b < CTX_TILES)
        def _():
            outs[0][...] = y

        @pl.when(b >= CTX_TILES)
        def _():
            outs[1][...] = y
    else:
        outs[0][...] = x

    @pl.when(b == N_BLK - 1)
    def _():
        wait_rows(nxt, 1 - slot)


def _combine(x1, ys, gate_t, rank_t, mod, plan, final_g):
    final = final_g is not None
    blk = lambda shape, imap: pl.BlockSpec(shape, imap)
    xspec = blk((ROW_TILE, D_MODEL), lambda b, *_: (b, 0))
    if final:
        out_specs = _ctx_lat_specs(D_MODEL)
        out_shape = [jax.ShapeDtypeStruct((N_CTX, D_MODEL), F32), jax.ShapeDtypeStruct((N_LAT, D_MODEL), F32)]
    else:
        out_specs, out_shape = xspec, jax.ShapeDtypeStruct((N_TOK, D_MODEL), F32)
    et_spec = blk((N_EXPERTS, ROW_TILE), lambda b, *_: (0, b))
    in_specs = [xspec, pl.BlockSpec(memory_space=pl.ANY), et_spec, et_spec,
                blk((None, 1, 6 * D_MODEL), lambda b, *_: (_mod_row(b), 0, 0)),
                blk((None, 1, 128), lambda b, *_: (b, 0, 0)),
                blk((None, N_EXPERTS, 1), lambda b, *_: (b, 0, 0)),
                blk((None, N_EXPERTS, 1), lambda b, *_: (b, 0, 0))]
    args = [x1, ys, gate_t, rank_t, mod, plan['local_row'], plan['local_col'], plan['seg_col']]
    if final:
        in_specs.append(blk((1, D_MODEL), lambda b, *_: (0, 0)))
        args.append(final_g)
    return pl.pallas_call(
        functools.partial(_combine_kernel, final=final),
        grid_spec=pltpu.PrefetchScalarGridSpec(
            num_scalar_prefetch=4, grid=(N_BLK,),
            in_specs=in_specs, out_specs=out_specs,
            scratch_shapes=[pltpu.VMEM((2, LOCAL_ROWS, D_MODEL), BF16), pltpu.SemaphoreType.DMA((2,)),
                            pltpu.VMEM((ROW_TILE, D_MODEL), F32)]),
        out_shape=out_shape,
        compiler_params=_params("arbitrary"),
        name="combine",
    )(plan['dest'], plan['seg'], plan['local'], plan['total'], *args)


def _rope_full_tables(dim, n_rep):
    rows = DEC_SEQ // GRID_W
    r_idx, c_idx = np.meshgrid(np.arange(rows), np.arange(GRID_W), indexing='ij')
    pos = np.stack([r_idx.reshape(-1), c_idx.reshape(-1)], axis=-1).astype(np.float32)
    nf = dim // 4
    inv = np.float32(ROPE_BASE) ** (-np.arange(nf, dtype=np.float32) / np.float32(nf))
    ang = (pos[:, :, None] * inv).astype(np.float32)
    ang = np.repeat(ang.reshape(DEC_SEQ, 2 * nf), 2, axis=1)
    sign = np.tile(np.asarray([-1.0, 1.0], np.float32), dim // 2)
    cos = np.tile(np.cos(ang.astype(np.float64)), (1, n_rep))
    sin = np.tile(np.sin(ang.astype(np.float64)) * sign, (1, n_rep))
    return jnp.asarray(cos, F32), jnp.asarray(sin, F32)


def _pack_w_in(w):
    c0 = MLA_Q_LORA + MLA_KV_LORA
    kr = w[..., c0:MLA_IN]
    s0 = MLA_IN + HG_IN + FN_IN
    qh = [w[..., s0 + h * SWA_HD:s0 + (h + 1) * SWA_HD] for h in SWA_STACK_ORDER]
    return jnp.concatenate([w[..., :c0], kr, kr, kr, kr, w[..., MLA_IN:s0]] + qh
                           + [w[..., s0 + SWA_QW:]], axis=-1).astype(BF16)


def _pack_w_q_up(w):
    hd = MLA_NOPE + MLA_ROPE
    nope = [w[:, h * hd:h * hd + MLA_NOPE] for h in range(MLA_HEADS)]
    rope = [w[:, h * hd + MLA_NOPE:(h + 1) * hd] for h in range(MLA_HEADS)]
    return jnp.concatenate(nope + rope, axis=1).astype(BF16)


def _pack_w_kv_up(w):
    hd = MLA_NOPE + MLA_V
    kn = [w[:, h * hd:h * hd + MLA_NOPE] for h in range(MLA_HEADS)]
    vv = [w[:, h * hd + MLA_NOPE:(h + 1) * hd] for h in range(MLA_HEADS)]
    return jnp.concatenate(kn + vv, axis=1).astype(BF16)


def _pack_w_out(w):
    s0 = 3 * 256
    rows = [w[..., s0 + h * SWA_HD:s0 + (h + 1) * SWA_HD, :] for h in SWA_STACK_ORDER]
    return jnp.concatenate([w[..., :s0, :]] + rows, axis=-2).astype(BF16)


def kernel(x_prompt, x_sample, c, cache_mla_ckv, cache_mla_krope, cache_swa_k, cache_swa_v, state_hgrn,
           c_ctx, w_ada, b_ada, norm1_g, norm2_g, w_in, mla_q_norm_g, mla_w_q_up, mla_kv_norm_g, mla_w_kv_up,
           hg_lb_logits, hg_norm_g, fn_w, swa_sink, w_out, moe_w_router, moe_b_router, moe_w_gate, moe_w_up,
           moe_w_down, sh_w_gate, sh_w_up, sh_w_down, final_norm_g):
    x_parts = (x_prompt.reshape(N_CTX, D_MODEL), x_sample.reshape(N_LAT, D_MODEL))
    cv8 = jnp.concatenate([c_ctx[None, :], c, jnp.zeros((8 - 1 - DEC_BATCH, D_MODEL), F32)], axis=0)
    mods = _ada(cv8, w_ada, b_ada).reshape(DEPTH, 8, 1, 6 * D_MODEL)

    lb = jnp.cumsum(jax.nn.softmax(hg_lb_logits.astype(F32), axis=1), axis=1)
    lb = lb - lb[:, :1]

    cos_m, sin_m = _rope_full_tables(MLA_ROPE, MLA_HEADS)
    cos_q, sin_q = _rope_full_tables(SWA_HD, SWA_HEADS)
    cos_k, sin_k = cos_q[:, :SWA_KW], sin_q[:, :SWA_KW]
    cache_k = cache_swa_k.reshape(DEC_BATCH, DEPTH, PAST_LEN, SWA_KW)
    cache_v = cache_swa_v.reshape(DEC_BATCH, DEPTH, PAST_LEN, SWA_KW)
    state_t = jnp.swapaxes(state_hgrn, -1, -2)

    ctx_blk_lat = N_CTX // DEC_SEQ
    w_in_p, w_out_p = _pack_w_in(w_in), _pack_w_out(w_out)
    new_ckv, new_kr, new_k, new_v, new_st = [], [], [], [], []
    for l in range(DEPTH):
        u_mla, u_hg, u_fn, u_swa = _in_proj(x_parts, mods[l], norm1_g[l][None], w_in_p, l)

        qg, kvg = mla_q_norm_g[l][None], mla_kv_norm_g[l][None]
        wq, wkv = _pack_w_q_up(mla_w_q_up[l]), _pack_w_kv_up(mla_w_kv_up[l])
        o_mla_c, ckv_c = _mla_ctx(u_mla, qg, wq, kvg, wkv)
        o_mla_l = _mla_lat(u_mla, cache_mla_ckv, cache_mla_krope, l, cos_m, sin_m, qg, wq, kvg, wkv)

        lbf, lbb = lb[0, l][None], lb[1, l][None]
        ng4 = jnp.tile(hg_norm_g[l], HG_HEADS)[None]
        o_hg_c, st_c = _hgrn(u_hg, lbf, lbb, ng4, None, seq=SEQ, n_batch=BATCH, row_block0=0)
        o_hg_l = _hgrn(u_hg, lbf, lbb, ng4, state_t[:, l], seq=DEC_SEQ, n_batch=DEC_BATCH,
                       row_block0=ctx_blk_lat)

        fw = fn_w[l].astype(BF16)
        o_fn_c = _fourier(u_fn, fw, seq=SEQ, n_batch=BATCH, row_block0=0)
        o_fn_l = _fourier(u_fn, fw, seq=DEC_SEQ, n_batch=DEC_BATCH, row_block0=ctx_blk_lat)

        sink = swa_sink[l]
        o_swa_c = _swa_ctx(u_swa, sink)
        o_swa_l = _swa_lat(u_swa, cache_k, cache_v, l, sink, cos_q, sin_q, cos_k, sin_k)

        x1, h2, gate_t, rank_t, cnt = _out_proj(
            x_parts, ((o_mla_c, o_mla_l), (o_hg_c, o_hg_l), (o_fn_c, o_fn_l), (o_swa_c, o_swa_l)),
            mods[l], norm2_g[l][None], w_out_p, l,
            moe_w_router[l], moe_b_router[l][:, None],
            sh_w_gate[l].astype(BF16), sh_w_up[l].astype(BF16), sh_w_down[l].astype(BF16))
        plan = _segment_plan(cnt)
        xs = _dispatch(h2, rank_t, plan)
        ys = _experts(xs, plan, moe_w_gate, moe_w_up, moe_w_down, l)
        if l < DEPTH - 1:
            x_parts = (_combine(x1, ys, gate_t, rank_t, mods[l], plan, None),)
        else:
            y_prompt, y_sample = _combine(x1, ys, gate_t, rank_t, mods[l], plan, final_norm_g[None])

        new_ckv.append(ckv_c.reshape(BATCH, SEQ, MLA_KV_LORA))
        new_kr.append(u_mla[:N_CTX, MLA_Q_LORA + MLA_KV_LORA:MLA_IN].reshape(BATCH, SEQ, MLA_ROPE))
        new_k.append(u_swa[:N_CTX, SWA_QW:SWA_QW + SWA_KW].reshape(BATCH, SEQ, SWA_KV_HEADS, SWA_HD))
        new_v.append(u_swa[:N_CTX, SWA_QW + SWA_KW:].reshape(BATCH, SEQ, SWA_KV_HEADS, SWA_HD))
        new_st.append(jnp.swapaxes(st_c, -1, -2))

    y_prompt = y_prompt.reshape(BATCH, SEQ, D_MODEL)
    y_sample = y_sample.reshape(DEC_BATCH, DEC_SEQ, D_MODEL)
    stack = lambda xs: jnp.stack(xs, axis=1)
    return (y_prompt, y_sample, stack(new_ckv), stack(new_kr), stack(new_k), stack(new_v), stack(new_st))
```
